```python
import jax, jax.numpy as jnp
from jax import lax
import numpy as np

D_MODEL = 1024
BATCH = 8
SEQ = 2048
DEPTH = 1

HG_HEADS = 4
HG_HD = 128
HG_WIDTH = HG_HEADS * HG_HD
HG_CHUNK = 64
NSA_HEADS = 8
NSA_KV_HEADS = 2
NSA_HD = 64
NSA_GROUP = NSA_HEADS // NSA_KV_HEADS
NSA_WIDTH = NSA_HEADS * NSA_HD
NSA_KV_WIDTH = NSA_KV_HEADS * NSA_HD
N_BRANCH = 3
CMP_BLOCK = 32
CMP_STRIDE = 16
CMP_HIDDEN = 256
SLC_BLOCK = 64
SLC_TOPK = 16
SLC_Q_CHUNK = 64
WIN = 512
WIN_Q_BLOCK = 128
MIX_WIDTH = HG_WIDTH + NSA_WIDTH
ROPE_DIM = NSA_HD // 4
ROPE_THETA = 500000.0
D_FF = 2816
CONV_W = 3
EPS = 1e-6
NEG = -1e30
IN_SIZES = [HG_WIDTH] * 4 + [NSA_WIDTH] + [NSA_KV_WIDTH] * 6 + [N_BRANCH * NSA_HEADS]
IN_COLS = sum(IN_SIZES)

kernel_name = "hymba_hgrn2_nsa_convffn_adaln"


def rmsnorm(x, g):
    xf = x.astype(jnp.float32)
    y = xf * lax.rsqrt(jnp.mean(xf * xf, axis=-1, keepdims=True) + EPS)
    return (y * g.astype(jnp.float32)).astype(x.dtype)


def rope_partial(x, pos):
    half = ROPE_DIM // 2
    inv = ROPE_THETA ** (-jnp.arange(half, dtype=jnp.float32) * 2.0 / ROPE_DIM)
    ang = pos.astype(jnp.float32)[:, None, :, None] * inv
    cos, sin = jnp.cos(ang), jnp.sin(ang)
    xf = x.astype(jnp.float32)
    x1, x2, rest = xf[..., :half], xf[..., half:ROPE_DIM], xf[..., ROPE_DIM:]
    out = jnp.concatenate([x1 * cos - x2 * sin, x2 * cos + x1 * sin, rest], axis=-1)
    return out.astype(x.dtype)


def to_heads(z, n, hd):
    B, T, _ = z.shape
    return z.reshape(B, T, n, hd).transpose(0, 2, 1, 3)


def hgrn2(q_pre, f_pre, i_pre, g_pre, lb, norm_g):
    B, T, _ = q_pre.shape
    f32 = jnp.float32
    nc = T // HG_CHUNK
    zf = f_pre.astype(f32)
    lbf = lb.astype(f32)
    logf = jnp.logaddexp(jnp.log(lbf), jnp.log1p(-lbf) + jax.nn.log_sigmoid(zf))
    k = (1.0 - lbf) * jax.nn.sigmoid(-zf)
    q = jax.nn.silu(q_pre.astype(f32))
    v = i_pre.astype(f32)

    def chunked(z):
        return z.reshape(B, nc, HG_CHUNK, HG_HEADS, HG_HD).transpose(1, 0, 3, 2, 4)

    qc, kc, vc = chunked(q), chunked(k), chunked(v)
    bc = jnp.cumsum(chunked(logf), axis=-2)
    causal = jnp.tril(jnp.ones((HG_CHUNK, HG_CHUNK), dtype=bool))

    def step(S, inp):
        qi, ki, vi, bi = inp
        o_inter = jnp.einsum('bhtk,bhkv->bhtv', qi * jnp.exp(bi), S)
        diff = bi[:, :, :, None, :] - bi[:, :, None, :, :]
        dec = jnp.where(causal[:, :, None], jnp.exp(jnp.minimum(diff, 0.0)), 0.0)
        A = jnp.einsum('bhtsk,bhsk->bhts', dec * qi[:, :, :, None, :], ki)
        o = o_inter + jnp.einsum('bhts,bhsv->bhtv', A, vi)
        bl = bi[:, :, -1:, :]
        S = jnp.exp(bl[:, :, 0, :])[..., None] * S + jnp.einsum('bhsk,bhsv->bhkv', ki * jnp.exp(bl - bi), vi)
        return S, o

    S0 = jnp.zeros((B, HG_HEADS, HG_HD, HG_HD), f32)
    _, o = lax.scan(step, S0, (qc, kc, vc, bc))
    o = o.transpose(1, 0, 3, 2, 4).reshape(B, T, HG_HEADS, HG_HD)
    g = jax.nn.silu(g_pre.astype(f32)).reshape(B, T, HG_HEADS, HG_HD)
    o = rmsnorm(o, norm_g) * g
    return o.reshape(B, T, HG_WIDTH).astype(q_pre.dtype)


def compress(blocks, pe, w1, w2):
    B, G, n, L, hd = blocks.shape
    flat = (blocks + pe).reshape(B, G, n, L * hd)
    return jax.nn.silu(flat @ w1) @ w2


def nsa(q_pre, kc_pre, vc_pre, ks_pre, vs_pre, kw_pre, vw_pre, g_pre, pos, q_g, k_g, pe, w1, w2):
    B, T, _ = q_pre.shape
    f32 = jnp.float32
    G, R, hd = NSA_KV_HEADS, NSA_GROUP, NSA_HD
    scale = NSA_HD ** -0.5
    tpos = jnp.arange(T)

    q = rope_partial(rmsnorm(to_heads(q_pre, NSA_HEADS, hd), q_g), pos).reshape(B, G, R, T, hd)
    k_c = rope_partial(rmsnorm(to_heads(kc_pre, G, hd), k_g[0]), pos)
    k_s = rope_partial(rmsnorm(to_heads(ks_pre, G, hd), k_g[1]), pos)
    k_w = rope_partial(rmsnorm(to_heads(kw_pre, G, hd), k_g[2]), pos)
    v_c, v_s, v_w = to_heads(vc_pre, G, hd), to_heads(vs_pre, G, hd), to_heads(vw_pre, G, hd)

    n_cmp = (T - CMP_BLOCK) // CMP_STRIDE + 1
    cidx = np.arange(n_cmp)[:, None] * CMP_STRIDE + np.arange(CMP_BLOCK)[None]
    k_cmp = compress(k_c[:, :, cidx], pe[0], w1[0], w2[0])
    v_cmp = compress(v_c[:, :, cidx], pe[1], w1[1], w2[1])
    s = jnp.einsum('bgrtd,bgnd->bgrtn', q, k_cmp).astype(f32) * scale
    cvalid = jnp.asarray(cidx[:, -1])[None, :] <= tpos[:, None]
    any_valid = jnp.any(cvalid, axis=-1, keepdims=True)
    p_cmp = jax.nn.softmax(jnp.where(cvalid, s, NEG), axis=-1) * any_valid
    o_cmp = jnp.einsum('bgrtn,bgnd->bgrtd', p_cmp.astype(v_cmp.dtype), v_cmp)

    nb = T // SLC_BLOCK
    n_sel = min(SLC_TOPK, nb)
    cst = np.arange(n_cmp) * CMP_STRIDE
    sst = np.arange(nb) * SLC_BLOCK
    ovl = np.clip(np.minimum(cst[:, None] + CMP_BLOCK, sst[None] + SLC_BLOCK)
                  - np.maximum(cst[:, None], sst[None]), 0, None) / CMP_BLOCK
    M = jnp.asarray(ovl, dtype=f32)
    imp = jnp.einsum('bgtn,nj->bgtj', p_cmp.sum(axis=2), M)
    cur = tpos // SLC_BLOCK
    j = jnp.arange(nb)
    forced = (j[None] == 0) | (j[None] == cur[:, None]) | (j[None] == cur[:, None] - 1)
    blk_causal = j[None] <= cur[:, None]
    imp = jnp.where(blk_causal, jnp.where(forced, jnp.inf, imp), -1.0)
    _, sel = lax.top_k(imp, n_sel)

    kb = k_s.reshape(B, G, nb, SLC_BLOCK, hd)
    vb = v_s.reshape(B, G, nb, SLC_BLOCK, hd)
    nq = T // SLC_Q_CHUNK
    q_ch = q.reshape(B, G, R, nq, SLC_Q_CHUNK, hd).transpose(3, 0, 1, 2, 4, 5)
    sel_ch = sel.reshape(B, G, nq, SLC_Q_CHUNK, n_sel).transpose(2, 0, 1, 3, 4)
    t_ch = tpos.reshape(nq, SLC_Q_CHUNK)
    gather = jax.vmap(jax.vmap(lambda blocks, ix: blocks[ix]))
    offs = jnp.arange(SLC_BLOCK)

    def sel_chunk(args):
        qc, ic, tc = args
        kg = gather(kb, ic)
        vg = gather(vb, ic)
        sc = jnp.einsum('bgrqd,bgqnld->bgrqnl', qc, kg).astype(f32) * scale
        kpos = ic[..., None] * SLC_BLOCK + offs
        m = kpos <= tc[None, None, :, None, None]
        sc = jnp.where(m[:, :, None], sc, NEG)
        shp = sc.shape
        p = jax.nn.softmax(sc.reshape(shp[:-2] + (shp[-2] * shp[-1],)), axis=-1).reshape(shp)
        return jnp.einsum('bgrqnl,bgqnld->bgrqd', p.astype(vg.dtype), vg)

    o_slc = lax.map(sel_chunk, (q_ch, sel_ch, t_ch))
    o_slc = o_slc.transpose(1, 2, 3, 0, 4, 5).reshape(B, G, R, T, hd)

    nwb = T // WIN_Q_BLOCK
    span = WIN + WIN_Q_BLOCK
    widx = np.arange(nwb)[:, None] * WIN_Q_BLOCK + np.arange(span)[None]
    padw = ((0, 0), (0, 0), (WIN, 0), (0, 0))
    kwin = jnp.pad(k_w, padw)[:, :, widx]
    vwin = jnp.pad(v_w, padw)[:, :, widx]
    qw = q.reshape(B, G, R, nwb, WIN_Q_BLOCK, hd)
    sw = jnp.einsum('bgrnqd,bgnkd->bgrnqk', qw, kwin).astype(f32) * scale
    kpos = jnp.asarray(widx - WIN)[:, None, :]
    qpos = (jnp.arange(nwb)[:, None] * WIN_Q_BLOCK + jnp.arange(WIN_Q_BLOCK)[None])[:, :, None]
    mw = (kpos <= qpos) & (qpos - kpos < WIN) & (kpos >= 0)
    pw = jax.nn.softmax(jnp.where(mw, sw, NEG), axis=-1)
    o_win = jnp.einsum('bgrnqk,bgnkd->bgrnqd', pw.astype(vwin.dtype), vwin).reshape(B, G, R, T, hd)

    gates = jax.nn.sigmoid(g_pre.astype(f32)).reshape(B, T, NSA_HEADS, N_BRANCH)
    gates = gates.transpose(0, 2, 1, 3).reshape(B, G, R, T, N_BRANCH)
    o = (gates[..., 0:1] * o_cmp.astype(f32) + gates[..., 1:2] * o_slc.astype(f32)
         + gates[..., 2:3] * o_win.astype(f32))
    o = o.reshape(B, NSA_HEADS, T, hd).transpose(0, 2, 1, 3).reshape(B, T, NSA_WIDTH)
    return o.astype(q_pre.dtype)


def causal_dwconv(u, w, b):
    T = u.shape[1]
    up = jnp.pad(u, ((0, 0), (CONV_W - 1, 0), (0, 0)))
    y = b
    for j in range(CONV_W):
        y = y + up[:, j:j + T] * w[j]
    return y


def setup_inputs(seed: int = 0) -> dict:
    key = jax.random.key(seed)
    ks = jax.random.split(key, 24)
    f32 = jnp.float32

    def nrm(k, shape, s):
        return jax.random.normal(k, shape, f32) * s

    return {
        "x": nrm(ks[0], (BATCH, SEQ, D_MODEL), 1.0),
        "c": nrm(ks[1], (BATCH, D_MODEL), 1.0),
        "positions": (jnp.arange(SEQ, dtype=jnp.int32)[None]
                      + jax.random.randint(ks[2], (BATCH, 1), 0, 4096, dtype=jnp.int32)),
        "w_ada": nrm(ks[3], (DEPTH, D_MODEL, 6 * D_MODEL), D_MODEL ** -0.5),
        "b_ada": nrm(ks[4], (DEPTH, 6 * D_MODEL), 0.02),
        "norm1_g": 1.0 + nrm(ks[5], (DEPTH, D_MODEL), 0.02),
        "w_in": nrm(ks[6], (DEPTH, D_MODEL, IN_COLS), D_MODEL ** -0.5),
        "lb_logits": nrm(ks[7], (DEPTH + 1, HG_WIDTH), 1.0),
        "hg_norm_g": 1.0 + nrm(ks[8], (DEPTH, HG_HD), 0.02),
        "q_norm_g": 1.0 + nrm(ks[9], (DEPTH, NSA_HD), 0.02),
        "k_norm_g": 1.0 + nrm(ks[10], (DEPTH, N_BRANCH, NSA_HD), 0.02),
        "pe_cmp": nrm(ks[11], (DEPTH, 2, CMP_BLOCK, NSA_HD), 0.02),
        "w_cmp1": nrm(ks[12], (DEPTH, 2, CMP_BLOCK * NSA_HD, CMP_HIDDEN), (CMP_BLOCK * NSA_HD) ** -0.5),
        "w_cmp2": nrm(ks[13], (DEPTH, 2, CMP_HIDDEN, NSA_HD), CMP_HIDDEN ** -0.5),
        "w_out": nrm(ks[14], (DEPTH, MIX_WIDTH, D_MODEL), MIX_WIDTH ** -0.5),
        "norm2_g": 1.0 + nrm(ks[15], (DEPTH, D_MODEL), 0.02),
        "w_up": nrm(ks[16], (DEPTH, D_MODEL, 2 * D_FF), D_MODEL ** -0.5),
        "conv_w": nrm(ks[17], (DEPTH, CONV_W, 2 * D_FF), CONV_W ** -0.5),
        "conv_b": nrm(ks[18], (DEPTH, 2 * D_FF), 0.02),
        "w_down": nrm(ks[19], (DEPTH, D_FF, D_MODEL), D_FF ** -0.5),
    }


def reference(x, c, positions, w_ada, b_ada, norm1_g, w_in, lb_logits, hg_norm_g, q_norm_g,
              k_norm_g, pe_cmp, w_cmp1, w_cmp2, w_out, norm2_g, w_up, conv_w, conv_b, w_down):
    offsets = np.cumsum(IN_SIZES)[:-1].tolist()
    lbs = jnp.cumsum(jax.nn.softmax(lb_logits.astype(jnp.float32), axis=0), axis=0)
    cs = jax.nn.silu(c)
    for l in range(DEPTH):
        mod = cs @ w_ada[l] + b_ada[l]
        sh1, sc1, gt1, sh2, sc2, gt2 = [m[:, None, :] for m in jnp.split(mod, 6, axis=-1)]

        h = rmsnorm(x, norm1_g[l]) * (1.0 + sc1) + sh1
        z = h @ w_in[l]
        (hq, hf, hi, hgt, nq_, kc_, vc_, ks_, vs_, kw_, vw_, ng_) = jnp.split(z, offsets, axis=-1)
        o_hg = hgrn2(hq, hf, hi, hgt, lbs[l], hg_norm_g[l])
        o_nsa = nsa(nq_, kc_, vc_, ks_, vs_, kw_, vw_, ng_, positions, q_norm_g[l], k_norm_g[l],
                    pe_cmp[l], w_cmp1[l], w_cmp2[l])
        mix = jnp.concatenate([o_hg, o_nsa], axis=-1) @ w_out[l]
        x = x + gt1 * mix

        h2 = rmsnorm(x, norm2_g[l]) * (1.0 + sc2) + sh2
        u = causal_dwconv(h2 @ w_up[l], conv_w[l], conv_b[l])
        a, v = jnp.split(u, 2, axis=-1)
        x = x + gt2 * ((jax.nn.silu(a) * v) @ w_down[l])
    return x
```

```python
import functools

import jax
import jax.numpy as jnp
import numpy as np
from jax import lax
from jax.experimental import pallas as pl
from jax.experimental.pallas import tpu as pltpu

D_MODEL = 1024
HG_HEADS = 4
HG_HD = 128
HG_WIDTH = HG_HEADS * HG_HD
HG_CHUNK = 64
HG_SUB = 16
NSA_HEADS = 8
NSA_KV_HEADS = 2
NSA_HD = 64
NSA_GROUP = NSA_HEADS // NSA_KV_HEADS
NSA_WIDTH = NSA_HEADS * NSA_HD
N_BRANCH = 3
CMP_BLOCK = 32
CMP_STRIDE = 16
CMP_HIDDEN = 256
SLC_BLOCK = 64
SLC_TOPK = 16
WIN = 512
ROPE_DIM = NSA_HD // 4
ROPE_THETA = 500000.0
D_FF = 2816
CONV_W = 3
EPS = 1e-6
NEG = -1e30

LANES = 128
SUBLANES = 8
VMEM_LIMIT = 56 * 1024 * 1024

OFF_HG = 0
OFF_Q = 4 * HG_WIDTH
OFF_KV = OFF_Q + NSA_WIDTH
OFF_G = OFF_KV + 6 * NSA_KV_HEADS * NSA_HD
IN_COLS = OFF_G + N_BRANCH * NSA_HEADS
IN_COLS_PAD = OFF_G + LANES

BF16 = jnp.bfloat16
F32 = jnp.float32


def _dot(a, b):
    return jnp.dot(a, b, preferred_element_type=F32)


def _dot_nt(a, b):
    return lax.dot_general(a, b, (((1,), (1,)), ((), ())), preferred_element_type=F32)


def _sigmoid(x):
    return 1.0 / (1.0 + jnp.exp(-x))


def _silu(x):
    return x * _sigmoid(x)


def _ada_kernel(c_ref, w_ref, b_ref, o_ref):
    cs = _silu(c_ref[...])
    o_ref[...] = jnp.dot(cs, w_ref[...], preferred_element_type=F32,
                         precision=lax.Precision.HIGHEST) + b_ref[...]


def _ada(c, w, b):
    bsz = c.shape[0]
    n = w.shape[1]
    tn = D_MODEL
    return pl.pallas_call(
        _ada_kernel,
        grid=(n // tn,),
        in_specs=[pl.BlockSpec((bsz, D_MODEL), lambda j: (0, 0)),
                  pl.BlockSpec((D_MODEL, tn), lambda j: (0, j)),
                  pl.BlockSpec((1, tn), lambda j: (0, j))],
        out_specs=pl.BlockSpec((bsz, tn), lambda j: (0, j)),
        out_shape=jax.ShapeDtypeStruct((bsz, n), F32),
        name="ada",
    )(c, w, b.reshape(1, n))


def _pair_norm_rope(xp, g, cosv, sinv, lane):
    left = lane < NSA_HD
    sq = xp * xp
    s_l = jnp.sum(jnp.where(left, sq, 0.0), axis=-1, keepdims=True)
    s_r = jnp.sum(jnp.where(left, 0.0, sq), axis=-1, keepdims=True)
    ms = jnp.where(left, s_l, s_r) * (1.0 / NSA_HD)
    xn = xp * lax.rsqrt(ms + EPS) * g
    half = ROPE_DIM // 2
    first = jnp.bitwise_and(lane, NSA_HD - 1) < half
    partner = jnp.where(first, pltpu.roll(xn, LANES - half, axis=1), pltpu.roll(xn, half, axis=1))
    return xn * cosv + partner * sinv


def _inproj_kernel(x_ref, mod_ref, g1_ref, w_ref, pos_ref, inv_ref, sgn_ref, qg_ref, kg_ref,
                   zhg_ref, q_ref, kc_ref, vc_ref, ks_ref, vs_ref, kw_ref, vw_ref, gate_ref):
    x = x_ref[0]
    ms = jnp.mean(x * x, axis=-1, keepdims=True)
    y = x * lax.rsqrt(ms + EPS) * g1_ref[...]
    h = (y * (1.0 + mod_ref[0, 1:2, :]) + mod_ref[0, 0:1, :]).astype(BF16)

    zhg_ref[0] = _dot(h, w_ref[:, OFF_HG:OFF_Q])

    tm = x.shape[0]
    lane = lax.broadcasted_iota(jnp.int32, (tm, LANES), 1)
    ang = pos_ref[0].astype(F32) * inv_ref[...]
    cosv = jnp.cos(ang)
    sinv = jnp.sin(ang) * sgn_ref[...]

    zq = _dot(h, w_ref[:, OFF_Q:OFF_KV])
    scale = NSA_HD ** -0.5
    for p in range(NSA_HEADS // 2):
        r = _pair_norm_rope(zq[:, p * LANES:(p + 1) * LANES], qg_ref[...], cosv, sinv, lane)
        r = (r * scale).astype(q_ref.dtype)
        q_ref[0, 2 * p] = r[:, :NSA_HD]
        q_ref[0, 2 * p + 1] = r[:, NSA_HD:]

    zkv = _dot(h, w_ref[:, OFF_KV:OFF_G])
    k_outs = (kc_ref, ks_ref, kw_ref)
    v_outs = (vc_ref, vs_ref, vw_ref)
    for br in range(N_BRANCH):
        kk = _pair_norm_rope(zkv[:, (2 * br) * LANES:(2 * br + 1) * LANES],
                             kg_ref[br:br + 1, :], cosv, sinv, lane).astype(k_outs[br].dtype)
        vv = zkv[:, (2 * br + 1) * LANES:(2 * br + 2) * LANES].astype(v_outs[br].dtype)
        for g in range(NSA_KV_HEADS):
            k_outs[br][0, g] = kk[:, g * NSA_HD:(g + 1) * NSA_HD]
            v_outs[br][0, g] = vv[:, g * NSA_HD:(g + 1) * NSA_HD]

    gate_ref[0] = _sigmoid(_dot(h, w_ref[:, OFF_G:IN_COLS_PAD]))


def _inproj(x, mod, g1, w_in_p, pos3, inv_full, sgn, qg2, kg2, tm):
    bsz, t, _ = x.shape
    grid = (bsz, t // tm)
    kv_shape = (bsz, NSA_KV_HEADS, t, NSA_HD)
    kv_spec = pl.BlockSpec((1, NSA_KV_HEADS, tm, NSA_HD), lambda b, i: (b, 0, i, 0))
    const = lambda b, i: (0, 0)
    return pl.pallas_call(
        _inproj_kernel,
        grid=grid,
        in_specs=[pl.BlockSpec((1, tm, D_MODEL), lambda b, i: (b, i, 0)),
                  pl.BlockSpec((1, 6, D_MODEL), lambda b, i: (b, 0, 0)),
                  pl.BlockSpec((1, D_MODEL), const),
                  pl.BlockSpec((D_MODEL, IN_COLS_PAD), const),
                  pl.BlockSpec((1, tm, 1), lambda b, i: (b, i, 0)),
                  pl.BlockSpec((1, LANES), const),
                  pl.BlockSpec((1, LANES), const),
                  pl.BlockSpec((1, LANES), const),
                  pl.BlockSpec((N_BRANCH, LANES), const)],
        out_specs=[pl.BlockSpec((1, tm, 4 * HG_WIDTH), lambda b, i: (b, i, 0)),
                   pl.BlockSpec((1, NSA_HEADS, tm, NSA_HD), lambda b, i: (b, 0, i, 0)),
                   kv_spec, kv_spec, kv_spec, kv_spec, kv_spec, kv_spec,
                   pl.BlockSpec((1, tm, LANES), lambda b, i: (b, i, 0))],
        out_shape=[jax.ShapeDtypeStruct((bsz, t, 4 * HG_WIDTH), F32),
                   jax.ShapeDtypeStruct((bsz, NSA_HEADS, t, NSA_HD), BF16),
                   jax.ShapeDtypeStruct(kv_shape, F32),
                   jax.ShapeDtypeStruct(kv_shape, F32),
                   jax.ShapeDtypeStruct(kv_shape, BF16),
                   jax.ShapeDtypeStruct(kv_shape, BF16),
                   jax.ShapeDtypeStruct(kv_shape, BF16),
                   jax.ShapeDtypeStruct(kv_shape, BF16),
                   jax.ShapeDtypeStruct((bsz, t, LANES), F32)],
        compiler_params=pltpu.CompilerParams(
            dimension_semantics=("arbitrary", "arbitrary"), vmem_limit_bytes=VMEM_LIMIT),
        name="inproj",
    )(x, mod, g1, w_in_p, pos3, inv_full, sgn, qg2, kg2)


def _hgrn_chunk(zq, zf, zi, zg, lb, hg_g, st_ref, h, tri):
    c = HG_CHUNK
    logsig = jnp.minimum(zf, 0.0) - jnp.log1p(jnp.exp(-jnp.abs(zf)))
    a = jnp.log(lb)
    bb = jnp.log1p(-lb) + logsig
    logf = jnp.maximum(a, bb) + jnp.log1p(jnp.exp(-jnp.abs(a - bb)))
    k = (1.0 - lb) * (1.0 / (1.0 + jnp.exp(zf)))
    q = _silu(zq)
    v = zi
    bc = jnp.dot(tri, logf, preferred_element_type=F32, precision=lax.Precision.HIGHEST)

    col = lax.broadcasted_iota(jnp.int32, (HG_SUB, c), 1)
    row = lax.broadcasted_iota(jnp.int32, (HG_SUB, c), 0)
    rows_a = []
    for i in range(c // HG_SUB):
        lo = i * HG_SUB
        b_i = bc[lo:lo + HG_SUB]
        q_i = q[lo:lo + HG_SUB]
        k_i = k[lo:lo + HG_SUB]
        a_i = jnp.zeros((HG_SUB, c), F32)
        for s in range(HG_SUB):
            e = jnp.exp(jnp.minimum(b_i - b_i[s:s + 1], 0.0)) * q_i * k_i[s:s + 1]
            a_i = jnp.where(col == lo + s, jnp.sum(e, axis=-1, keepdims=True), a_i)
        if i > 0:
            r_i = b_i[0:1]
            qh = (q_i * jnp.exp(b_i - r_i)).astype(BF16)
            kh = (k * jnp.exp(jnp.minimum(r_i - bc, 0.0))).astype(BF16)
            a_i = jnp.where(col < lo, _dot_nt(qh, kh), a_i)
        a_i = jnp.where(col <= row + lo, a_i, 0.0)
        rows_a.append(a_i)
    amat = jnp.concatenate(rows_a, axis=0)

    st = st_ref[h]
    o = _dot_nt((q * jnp.exp(bc)).astype(BF16), st.astype(BF16)) + _dot(amat.astype(BF16), v.astype(BF16))
    bl = bc[c - 1:c]
    kdec = (k * jnp.exp(bl - bc)).astype(BF16)
    st_ref[h] = jnp.exp(bl) * st + _dot(v.T.astype(BF16), kdec)

    y = o * lax.rsqrt(jnp.mean(o * o, axis=-1, keepdims=True) + EPS) * hg_g
    return y * _silu(zg)


def _hgrn_kernel(z_ref, lbl_ref, g_ref, o_ref, st_ref, *, l_idx):
    @pl.when(pl.program_id(1) == 0)
    def _():
        st_ref[...] = jnp.zeros_like(st_ref)

    lg = lbl_ref[...]
    ex = jnp.exp(lg - jnp.max(lg, axis=0, keepdims=True))
    sm = ex / jnp.sum(ex, axis=0, keepdims=True)
    lb_all = jnp.sum(sm[:l_idx + 1], axis=0, keepdims=True)

    c = HG_CHUNK
    ri = lax.broadcasted_iota(jnp.int32, (c, c), 0)
    ci = lax.broadcasted_iota(jnp.int32, (c, c), 1)
    tri = (ci <= ri).astype(F32)
    n_chunks = z_ref.shape[1] // c

    def body(ch, carry):
        r0 = pl.multiple_of(ch * c, c)
        for h in range(HG_HEADS):
            sl = slice(h * HG_HD, (h + 1) * HG_HD)
            zq = z_ref[0, pl.ds(r0, c), h * HG_HD:(h + 1) * HG_HD]
            zf = z_ref[0, pl.ds(r0, c), HG_WIDTH + h * HG_HD:HG_WIDTH + (h + 1) * HG_HD]
            zi = z_ref[0, pl.ds(r0, c), 2 * HG_WIDTH + h * HG_HD:2 * HG_WIDTH + (h + 1) * HG_HD]
            zg = z_ref[0, pl.ds(r0, c), 3 * HG_WIDTH + h * HG_HD:3 * HG_WIDTH + (h + 1) * HG_HD]
            y = _hgrn_chunk(zq, zf, zi, zg, lb_all[:, sl], g_ref[...], st_ref, h, tri)
            o_ref[0, pl.ds(r0, c), h * HG_HD:(h + 1) * HG_HD] = y.astype(o_ref.dtype)
        return carry

    lax.fori_loop(0, n_chunks, body, 0)


def _hgrn(zhg, lb_logits, hg_g, l_idx, tb):
    bsz, t, _ = zhg.shape
    return pl.pallas_call(
        functools.partial(_hgrn_kernel, l_idx=l_idx),
        grid=(bsz, t // tb),
        in_specs=[pl.BlockSpec((1, tb, 4 * HG_WIDTH), lambda b, i: (b, i, 0)),
                  pl.BlockSpec(lb_logits.shape, lambda b, i: (0, 0)),
                  pl.BlockSpec((1, HG_HD), lambda b, i: (0, 0))],
        out_specs=pl.BlockSpec((1, tb, HG_WIDTH), lambda b, i: (b, i, 0)),
        out_shape=jax.ShapeDtypeStruct((bsz, t, HG_WIDTH), BF16),
        scratch_shapes=[pltpu.VMEM((HG_HEADS, HG_HD, HG_HD), F32)],
        compiler_params=pltpu.CompilerParams(
            dimension_semantics=("arbitrary", "arbitrary"), vmem_limit_bytes=VMEM_LIMIT),
        name="hgrn",
    )(zhg, lb_logits, hg_g)


def _compress_kernel(xk_ref, xv_ref, pe_ref, w1_ref, w2_ref, ko_ref, vo_ref):
    half = CMP_STRIDE * NSA_HD
    for kv, (x_ref, o_ref) in enumerate(((xk_ref, ko_ref), (xv_ref, vo_ref))):
        x = x_ref[0, 0]
        ha = _dot((x + pe_ref[kv, 0:1, :]).astype(BF16), w1_ref[kv, :half, :])
        hb = _dot((x + pe_ref[kv, 1:2, :]).astype(BF16), w1_ref[kv, half:, :])
        n = x.shape[0]
        pre = ha + pltpu.roll(hb, n - 1, axis=0)
        o_ref[0, 0] = _dot(_silu(pre).astype(BF16), w2_ref[kv]).astype(o_ref.dtype)


def _compress(xk, xv, pe2, w1, w2):
    bsz, g, nseg, width = xk.shape
    x_spec = pl.BlockSpec((1, 1, nseg, width), lambda b, j: (b, j, 0, 0))
    o_spec = pl.BlockSpec((1, 1, nseg, NSA_HD), lambda b, j: (b, j, 0, 0))
    o_shape = jax.ShapeDtypeStruct((bsz, g, nseg, NSA_HD), BF16)
    return pl.pallas_call(
        _compress_kernel,
        grid=(bsz, g),
        in_specs=[x_spec, x_spec,
                  pl.BlockSpec(pe2.shape, lambda b, j: (0, 0, 0)),
                  pl.BlockSpec(w1.shape, lambda b, j: (0, 0, 0)),
                  pl.BlockSpec(w2.shape, lambda b, j: (0, 0, 0))],
        out_specs=[o_spec, o_spec],
        out_shape=[o_shape, o_shape],
        compiler_params=pltpu.CompilerParams(
            dimension_semantics=("arbitrary", "arbitrary"), vmem_limit_bytes=VMEM_LIMIT),
        name="compress",
    )(xk, xv, pe2, w1, w2)


def _flash(q4, k_ref, v_ref, lo, hi, mask_fn, tk):
    rows = q4.shape[0]
    r = NSA_GROUP
    tq = rows // r

    def body(kt, carry):
        m, l, acc = carry
        k0 = pl.multiple_of(kt * tk, tk)
        kb = k_ref[0, 0, pl.ds(k0, tk), :]
        vb = v_ref[0, 0, pl.ds(k0, tk), :]
        s = _dot_nt(q4, kb).reshape(r, tq, tk)
        s = jnp.where(mask_fn(kt)[None], s, NEG).reshape(rows, tk)
        m_new = jnp.maximum(m, jnp.max(s, axis=-1, keepdims=True))
        alpha = jnp.exp(m - m_new)
        p = jnp.exp(s - m_new)
        l = alpha * l + jnp.sum(p, axis=-1, keepdims=True)
        acc = alpha * acc + _dot(p.astype(vb.dtype), vb)
        return m_new, l, acc

    init = (jnp.full((rows, 1), NEG, F32), jnp.zeros((rows, 1), F32), jnp.zeros((rows, NSA_HD), F32))
    m, l, acc = lax.fori_loop(lo, hi, body, init)
    return acc / l


def _nsa_kernel(q_ref, kc_ref, vc_ref, ks_ref, vs_ref, kw_ref, vw_ref, gate_ref, mt_ref, ex_ref,
                o_ref, msk_ref, *, tq):
    r = NSA_GROUP
    g = pl.program_id(1)
    qt = pl.program_id(2)
    q0 = qt * tq
    q4 = q_ref[0].reshape(r * tq, NSA_HD)

    n_blk = kc_ref.shape[2]
    s = _dot_nt(q4, kc_ref[0, 0]).reshape(r, tq, n_blk)
    t_col = q0 + lax.broadcasted_iota(jnp.int32, (tq, n_blk), 0)
    n_row = lax.broadcasted_iota(jnp.int32, (tq, n_blk), 1)
    cvalid = n_row * CMP_STRIDE + (CMP_BLOCK - 1) <= t_col
    s = jnp.where(cvalid[None], s, NEG)
    e = jnp.exp(s - jnp.max(s, axis=-1, keepdims=True))
    p = e / jnp.sum(e, axis=-1, keepdims=True)
    p = jnp.where((t_col >= CMP_BLOCK - 1)[None], p, 0.0)
    o_cmp = _dot(p.reshape(r * tq, n_blk).astype(vc_ref.dtype), vc_ref[0, 0])

    nb = mt_ref.shape[0]
    psum = jnp.sum(p, axis=0)
    imp = lax.dot_general(mt_ref[...], psum, (((1,), (1,)), ((), ())),
                          preferred_element_type=F32, precision=lax.Precision.HIGHEST)
    j = lax.broadcasted_iota(jnp.int32, (nb, tq), 0)
    cur = jnp.right_shift(q0 + lax.broadcasted_iota(jnp.int32, (nb, tq), 1), SLC_BLOCK.bit_length() - 1)
    forced = (j == 0) | (j == cur) | (j == cur - 1)
    imp = jnp.where(j <= cur, jnp.where(forced, jnp.inf, imp), -1.0)
    rank = jnp.zeros((nb, tq), jnp.int32)
    for i in range(nb):
        row_i = imp[i:i + 1, :]
        ahead = (row_i > imp) | ((row_i == imp) & (j > i))
        rank = rank + ahead.astype(jnp.int32)
    sel = (rank < min(SLC_TOPK, nb)).astype(F32)
    msk_ref[...] = _dot(sel.T.astype(BF16), ex_ref[...])

    row_q = q0 + lax.broadcasted_iota(jnp.int32, (tq, tq), 0)
    col_k = lax.broadcasted_iota(jnp.int32, (tq, tq), 1)

    def slc_mask(kt):
        kpos = kt * tq + col_k
        chosen = msk_ref[:, pl.ds(pl.multiple_of(kt * tq, tq), tq)] > 0.5
        return chosen & (kpos <= row_q)

    o_slc = _flash(q4, ks_ref, vs_ref, 0, qt + 1, slc_mask, tq)

    def win_mask(kt):
        kpos = kt * tq + col_k
        return (kpos <= row_q) & (row_q - kpos < WIN)

    o_win = _flash(q4, kw_ref, vw_ref, jnp.maximum(qt - WIN // tq, 0), qt + 1, win_mask, tq)

    gates = gate_ref[0]
    for hh in range(r):
        rows = slice(hh * tq, (hh + 1) * tq)
        o_h = jnp.zeros((tq, NSA_HD), F32)
        for br, o_b in enumerate((o_cmp, o_slc, o_win)):
            gcol = gates[:, hh * N_BRANCH + br:hh * N_BRANCH + br + 1]
            for gg in range(1, NSA_KV_HEADS):
                cidx = (gg * r + hh) * N_BRANCH + br
                gcol = jnp.where(g == gg, gates[:, cidx:cidx + 1], gcol)
            o_h = o_h + gcol * o_b[rows]
        o_ref[0, :, hh * NSA_HD:(hh + 1) * NSA_HD] = o_h.astype(o_ref.dtype)


def _nsa(q, kc, vc, ks, vs, kw, vw, gates, mt, ex, tq):
    bsz, _, t, _ = q.shape
    n_blk = kc.shape[2]
    full = lambda b, g, i: (b, g, 0, 0)
    kv_spec = pl.BlockSpec((1, 1, t, NSA_HD), full)
    cm_spec = pl.BlockSpec((1, 1, n_blk, NSA_HD), full)
    return pl.pallas_call(
        functools.partial(_nsa_kernel, tq=tq),
        grid=(bsz, NSA_KV_HEADS, t // tq),
        in_specs=[pl.BlockSpec((1, NSA_GROUP, tq, NSA_HD), lambda b, g, i: (b, g, i, 0)),
                  cm_spec, cm_spec, kv_spec, kv_spec, kv_spec, kv_spec,
                  pl.BlockSpec((1, tq, LANES), lambda b, g, i: (b, i, 0)),
                  pl.BlockSpec(mt.shape, lambda b, g, i: (0, 0)),
                  pl.BlockSpec(ex.shape, lambda b, g, i: (0, 0))],
        out_specs=pl.BlockSpec((1, tq, NSA_GROUP * NSA_HD), lambda b, g, i: (b, i, g)),
        out_shape=jax.ShapeDtypeStruct((bsz, t, NSA_WIDTH), BF16),
        scratch_shapes=[pltpu.VMEM((tq, t), F32)],
        compiler_params=pltpu.CompilerParams(
            dimension_semantics=("arbitrary", "arbitrary", "arbitrary"), vmem_limit_bytes=VMEM_LIMIT),
        name="nsa",
    )(q, kc, vc, ks, vs, kw, vw, gates, mt, ex)


def _causal_conv(u, prev, cw, cb, row):
    u1 = jnp.where(row == 0, prev[SUBLANES - 1:SUBLANES], pltpu.roll(u, 1, axis=0))
    u2 = jnp.where(row == 0, prev[SUBLANES - 2:SUBLANES - 1],
                   jnp.where(row == 1, prev[SUBLANES - 1:SUBLANES], pltpu.roll(u, 2, axis=0)))
    return cb + u2 * cw[0:1] + u1 * cw[1:2] + u * cw[2:3]


def _ffn_kernel(x_ref, hg_ref, ns_ref, mod_ref, wo_ref, g2_ref, wa_ref, wv_ref, cwa_ref, cwv_ref,
                cba_ref, cbv_ref, wd_ref, o_ref, x1_ref, h2_ref, acc_ref, ca_ref, cv_ref):
    m = pl.program_id(1)
    f = pl.program_id(2)
    nf = pl.num_programs(2)

    @pl.when(f == 0)
    def _():
        mix = _dot(hg_ref[0], wo_ref[:HG_WIDTH, :]) + _dot(ns_ref[0], wo_ref[HG_WIDTH:, :])
        x1 = x_ref[0] + mod_ref[0, 2:3, :] * mix
        x1_ref[...] = x1
        y = x1 * lax.rsqrt(jnp.mean(x1 * x1, axis=-1, keepdims=True) + EPS) * g2_ref[...]
        h2_ref[...] = (y * (1.0 + mod_ref[0, 4:5, :]) + mod_ref[0, 3:4, :]).astype(h2_ref.dtype)
        acc_ref[...] = jnp.zeros_like(acc_ref)

    h2 = h2_ref[...]
    ua = _dot(h2, wa_ref[...])
    uv = _dot(h2, wv_ref[...])
    tm = ua.shape[0]
    first = m == 0
    pa = jnp.where(first, 0.0, ca_ref[f])
    pv = jnp.where(first, 0.0, cv_ref[f])
    ca_ref[f] = ua[tm - SUBLANES:]
    cv_ref[f] = uv[tm - SUBLANES:]
    row = lax.broadcasted_iota(jnp.int32, ua.shape, 0)
    a = _causal_conv(ua, pa, cwa_ref[...], cba_ref[...], row)
    v = _causal_conv(uv, pv, cwv_ref[...], cbv_ref[...], row)
    acc_ref[...] += _dot((_silu(a) * v).astype(BF16), wd_ref[...])

    @pl.when(f == nf - 1)
    def _():
        o_ref[0] = x1_ref[...] + mod_ref[0, 5:6, :] * acc_ref[...]


def _ffn(x, o_hg, o_nsa, mod, w_out, g2, w_up, conv_w, conv_b, w_down, tm, fc):
    bsz, t, _ = x.shape
    nf = D_FF // fc
    row_spec = lambda w: pl.BlockSpec((1, tm, w), lambda b, i, f: (b, i, 0))
    const2 = lambda b, i, f: (0, 0)
    return pl.pallas_call(
        _ffn_kernel,
        grid=(bsz, t // tm, nf),
        in_specs=[row_spec(D_MODEL), row_spec(HG_WIDTH), row_spec(NSA_WIDTH),
                  pl.BlockSpec((1, 6, D_MODEL), lambda b, i, f: (b, 0, 0)),
                  pl.BlockSpec(w_out.shape, const2),
                  pl.BlockSpec((1, D_MODEL), const2),
                  pl.BlockSpec((D_MODEL, fc), lambda b, i, f: (0, f)),
                  pl.BlockSpec((D_MODEL, fc), lambda b, i, f: (0, nf + f)),
                  pl.BlockSpec((CONV_W, fc), lambda b, i, f: (0, f)),
                  pl.BlockSpec((CONV_W, fc), lambda b, i, f: (0, nf + f)),
                  pl.BlockSpec((1, fc), lambda b, i, f: (0, f)),
                  pl.BlockSpec((1, fc), lambda b, i, f: (0, nf + f)),
                  pl.BlockSpec((fc, D_MODEL), lambda b, i, f: (f, 0))],
        out_specs=row_spec(D_MODEL),
        out_shape=jax.ShapeDtypeStruct(x.shape, F32),
        scratch_shapes=[pltpu.VMEM((tm, D_MODEL), F32),
                        pltpu.VMEM((tm, D_MODEL), BF16),
                        pltpu.VMEM((tm, D_MODEL), F32),
                        pltpu.VMEM((nf, SUBLANES, fc), F32),
                        pltpu.VMEM((nf, SUBLANES, fc), F32)],
        compiler_params=pltpu.CompilerParams(
            dimension_semantics=("arbitrary", "arbitrary", "arbitrary"), vmem_limit_bytes=VMEM_LIMIT),
        name="ffn",
    )(x, o_hg, o_nsa, mod, w_out, g2, w_up, w_up, conv_w, conv_w, conv_b, conv_b, w_down)


def _rope_tables():
    half = ROPE_DIM // 2
    inv = ROPE_THETA ** (-jnp.arange(half, dtype=F32) * 2.0 / ROPE_DIM)
    d = np.arange(LANES) % NSA_HD
    inv_full = jnp.where(jnp.asarray(d < ROPE_DIM), jnp.tile(inv, LANES // half), 0.0)
    sgn = np.where(d < half, -1.0, np.where(d < ROPE_DIM, 1.0, 0.0)).astype(np.float32)
    return inv_full.reshape(1, LANES).astype(F32), jnp.asarray(sgn).reshape(1, LANES)


def _selection_tables(t):
    n_seg = t // CMP_STRIDE
    nb = t // SLC_BLOCK
    cst = np.arange(n_seg) * CMP_STRIDE
    sst = np.arange(nb) * SLC_BLOCK
    ovl = np.clip(np.minimum(cst[:, None] + CMP_BLOCK, sst[None] + SLC_BLOCK)
                  - np.maximum(cst[:, None], sst[None]), 0, None) / CMP_BLOCK
    ovl[(t - CMP_BLOCK) // CMP_STRIDE + 1:] = 0.0
    expand = (np.arange(t)[None, :] // SLC_BLOCK == np.arange(nb)[:, None])
    return jnp.asarray(ovl.T, dtype=F32), jnp.asarray(expand, dtype=BF16)


def _layer(x, mod, pos3, l, p, tables):
    bsz, t, _ = x.shape
    inv_full, sgn, mt, ex = tables
    w_in_p = jnp.pad(p["w_in"][l], ((0, 0), (0, IN_COLS_PAD - IN_COLS))).astype(BF16)
    qg2 = jnp.tile(p["q_norm_g"][l].reshape(1, NSA_HD), (1, LANES // NSA_HD))
    kg2 = jnp.tile(p["k_norm_g"][l], (1, LANES // NSA_HD))
    zhg, q, kc, vc, ks, vs, kw, vw, gates = _inproj(
        x, mod, p["norm1_g"][l].reshape(1, D_MODEL), w_in_p, pos3, inv_full, sgn, qg2, kg2, tm=256)

    o_hg = _hgrn(zhg, p["lb_logits"], p["hg_norm_g"][l].reshape(1, HG_HD), l, tb=512)

    n_seg = t // CMP_STRIDE
    seg_w = CMP_STRIDE * NSA_HD
    pe2 = p["pe_cmp"][l].reshape(2, 2, seg_w)
    kcmp, vcmp = _compress(kc.reshape(bsz, NSA_KV_HEADS, n_seg, seg_w),
                           vc.reshape(bsz, NSA_KV_HEADS, n_seg, seg_w),
                           pe2, p["w_cmp1"][l].astype(BF16), p["w_cmp2"][l].astype(BF16))
    o_nsa = _nsa(q, kcmp, vcmp, ks, vs, kw, vw, gates, mt, ex, tq=128)

    return _ffn(x, o_hg, o_nsa, mod, p["w_out"][l].astype(BF16), p["norm2_g"][l].reshape(1, D_MODEL),
                p["w_up"][l].astype(BF16), p["conv_w"][l], p["conv_b"][l].reshape(1, 2 * D_FF),
                p["w_down"][l].astype(BF16), tm=256, fc=1408)


def kernel(x, c, positions, w_ada, b_ada, norm1_g, w_in, lb_logits, hg_norm_g, q_norm_g, k_norm_g, pe_cmp, w_cmp1, w_cmp2, w_out, norm2_g, w_up, conv_w, conv_b, w_down):
    p = dict(w_in=w_in, norm1_g=norm1_g, lb_logits=lb_logits, hg_norm_g=hg_norm_g, q_norm_g=q_norm_g,
             k_norm_g=k_norm_g, pe_cmp=pe_cmp, w_cmp1=w_cmp1, w_cmp2=w_cmp2, w_out=w_out,
             norm2_g=norm2_g, w_up=w_up, conv_w=conv_w, conv_b=conv_b, w_down=w_down)
    bsz, t, _ = x.shape
    tables = _rope_tables() + _selection_tables(t)
    pos3 = positions.reshape(bsz, t, 1)
    for l in range(w_ada.shape[0]):
        mod = _ada(c, w_ada[l], b_ada[l]).reshape(bsz, 6, D_MODEL)
        x = _layer(x, mod, pos3, l, p, tables)
    return x
```

```python
import functools

import jax
import jax.numpy as jnp
import numpy as np
from jax import lax
from jax.experimental import pallas as pl
from jax.experimental.pallas import tpu as pltpu

D_MODEL = 1024
HG_HEADS = 4
HG_HD = 128
HG_WIDTH = HG_HEADS * HG_HD
HG_CHUNK = 64
HG_SUB = 16
NSA_HEADS = 8
NSA_KV_HEADS = 2
NSA_HD = 64
NSA_GROUP = NSA_HEADS // NSA_KV_HEADS
NSA_WIDTH = NSA_HEADS * NSA_HD
N_BRANCH = 3
CMP_BLOCK = 32
CMP_STRIDE = 16
CMP_HIDDEN = 256
SLC_BLOCK = 64
SLC_TOPK = 16
WIN = 512
ROPE_DIM = NSA_HD // 4
ROPE_THETA = 500000.0
D_FF = 2816
CONV_W = 3
EPS = 1e-6
NEG = -1e30

LANES = 128
SUBLANES = 8
VMEM_LIMIT = 56 * 1024 * 1024

OFF_HG = 0
OFF_Q = 4 * HG_WIDTH
OFF_KV = OFF_Q + NSA_WIDTH
OFF_G = OFF_KV + 6 * NSA_KV_HEADS * NSA_HD
IN_COLS = OFF_G + N_BRANCH * NSA_HEADS
IN_COLS_PAD = OFF_G + LANES
GATE_ROWS = 32

BF16 = jnp.bfloat16
F32 = jnp.float32


def _dot(a, b):
    return jnp.dot(a, b, preferred_element_type=F32)


def _dot_nt(a, b):
    return lax.dot_general(a, b, (((1,), (1,)), ((), ())), preferred_element_type=F32)


def _sigmoid(x):
    return 1.0 / (1.0 + jnp.exp(-x))


def _silu(x):
    return x * _sigmoid(x)


def _ada_kernel(c_ref, w_ref, b_ref, o_ref):
    cs = _silu(c_ref[...])
    o_ref[...] = jnp.dot(cs, w_ref[...], preferred_element_type=F32,
                         precision=lax.Precision.HIGHEST) + b_ref[...]


def _ada(c, w, b):
    bsz = c.shape[0]
    n = w.shape[1]
    tn = D_MODEL
    return pl.pallas_call(
        _ada_kernel,
        grid=(n // tn,),
        in_specs=[pl.BlockSpec((bsz, D_MODEL), lambda j: (0, 0)),
                  pl.BlockSpec((D_MODEL, tn), lambda j: (0, j)),
                  pl.BlockSpec((1, tn), lambda j: (0, j))],
        out_specs=pl.BlockSpec((bsz, tn), lambda j: (0, j)),
        out_shape=jax.ShapeDtypeStruct((bsz, n), F32),
        name="ada",
    )(c, w, b.reshape(1, n))


def _pair_norm_rope(xp, g, cosv, sinv, lane):
    left = lane < NSA_HD
    sq = xp * xp
    s_l = jnp.sum(jnp.where(left, sq, 0.0), axis=-1, keepdims=True)
    s_r = jnp.sum(jnp.where(left, 0.0, sq), axis=-1, keepdims=True)
    ms = jnp.where(left, s_l, s_r) * (1.0 / NSA_HD)
    xn = xp * lax.rsqrt(ms + EPS) * g
    half = ROPE_DIM // 2
    first = jnp.bitwise_and(lane, NSA_HD - 1) < half
    partner = jnp.where(first, pltpu.roll(xn, LANES - half, axis=1), pltpu.roll(xn, half, axis=1))
    return xn * cosv + partner * sinv


def _inproj_kernel(x_ref, mod_ref, g1_ref, w_ref, pos_ref, inv_ref, sgn_ref, qg_ref, kg_ref,
                   zhg_ref, qt_ref, kc_ref, vc_ref, ks_ref, vst_ref, kw_ref, vwt_ref, gt_ref):
    x = x_ref[0]
    ms = jnp.mean(x * x, axis=-1, keepdims=True)
    y = x * lax.rsqrt(ms + EPS) * g1_ref[...]
    h = (y * (1.0 + mod_ref[0, 1:2, :]) + mod_ref[0, 0:1, :]).astype(BF16)

    zhg_ref[0] = _dot(h, w_ref[:, OFF_HG:OFF_Q])

    tm = x.shape[0]
    lane = lax.broadcasted_iota(jnp.int32, (tm, LANES), 1)
    ang = pos_ref[0].astype(F32) * inv_ref[...]
    cosv = jnp.cos(ang)
    sinv = jnp.sin(ang) * sgn_ref[...]

    zq = _dot(h, w_ref[:, OFF_Q:OFF_KV])
    scale = NSA_HD ** -0.5
    for p in range(NSA_HEADS // 2):
        r = _pair_norm_rope(zq[:, p * LANES:(p + 1) * LANES], qg_ref[...], cosv, sinv, lane)
        rt = (r * scale).T.astype(qt_ref.dtype)
        qt_ref[0, 2 * p] = rt[:NSA_HD]
        qt_ref[0, 2 * p + 1] = rt[NSA_HD:]

    zkv = _dot(h, w_ref[:, OFF_KV:OFF_G])
    k_outs = (kc_ref, ks_ref, kw_ref)
    for br in range(N_BRANCH):
        kk = _pair_norm_rope(zkv[:, (2 * br) * LANES:(2 * br + 1) * LANES],
                             kg_ref[br:br + 1, :], cosv, sinv, lane).astype(k_outs[br].dtype)
        for g in range(NSA_KV_HEADS):
            k_outs[br][0, g] = kk[:, g * NSA_HD:(g + 1) * NSA_HD]
    vc = zkv[:, LANES:2 * LANES]
    for g in range(NSA_KV_HEADS):
        vc_ref[0, g] = vc[:, g * NSA_HD:(g + 1) * NSA_HD]
    for br, vt_ref in ((1, vst_ref), (2, vwt_ref)):
        vt = zkv[:, (2 * br + 1) * LANES:(2 * br + 2) * LANES].T.astype(vt_ref.dtype)
        for g in range(NSA_KV_HEADS):
            vt_ref[0, g] = vt[g * NSA_HD:(g + 1) * NSA_HD]

    gates = _sigmoid(_dot(h, w_ref[:, OFF_G:IN_COLS_PAD]))
    gt_ref[0] = gates.T[:GATE_ROWS]


def _inproj(x, mod, g1, w_in_p, pos3, inv_full, sgn, qg2, kg2, tm):
    bsz, t, _ = x.shape
    grid = (bsz, t // tm)
    kv_shape = (bsz, NSA_KV_HEADS, t, NSA_HD)
    kv_spec = pl.BlockSpec((1, NSA_KV_HEADS, tm, NSA_HD), lambda b, i: (b, 0, i, 0))
    vt_shape = (bsz, NSA_KV_HEADS, NSA_HD, t)
    vt_spec = pl.BlockSpec((1, NSA_KV_HEADS, NSA_HD, tm), lambda b, i: (b, 0, 0, i))
    const = lambda b, i: (0, 0)
    return pl.pallas_call(
        _inproj_kernel,
        grid=grid,
        in_specs=[pl.BlockSpec((1, tm, D_MODEL), lambda b, i: (b, i, 0)),
                  pl.BlockSpec((1, 6, D_MODEL), lambda b, i: (b, 0, 0)),
                  pl.BlockSpec((1, D_MODEL), const),
                  pl.BlockSpec((D_MODEL, IN_COLS_PAD), const),
                  pl.BlockSpec((1, tm, 1), lambda b, i: (b, i, 0)),
                  pl.BlockSpec((1, LANES), const),
                  pl.BlockSpec((1, LANES), const),
                  pl.BlockSpec((1, LANES), const),
                  pl.BlockSpec((N_BRANCH, LANES), const)],
        out_specs=[pl.BlockSpec((1, tm, 4 * HG_WIDTH), lambda b, i: (b, i, 0)),
                   pl.BlockSpec((1, NSA_HEADS, NSA_HD, tm), lambda b, i: (b, 0, 0, i)),
                   kv_spec, kv_spec, kv_spec, vt_spec, kv_spec, vt_spec,
                   pl.BlockSpec((1, GATE_ROWS, tm), lambda b, i: (b, 0, i))],
        out_shape=[jax.ShapeDtypeStruct((bsz, t, 4 * HG_WIDTH), F32),
                   jax.ShapeDtypeStruct((bsz, NSA_HEADS, NSA_HD, t), BF16),
                   jax.ShapeDtypeStruct(kv_shape, F32),
                   jax.ShapeDtypeStruct(kv_shape, F32),
                   jax.ShapeDtypeStruct(kv_shape, BF16),
                   jax.ShapeDtypeStruct(vt_shape, BF16),
                   jax.ShapeDtypeStruct(kv_shape, BF16),
                   jax.ShapeDtypeStruct(vt_shape, BF16),
                   jax.ShapeDtypeStruct((bsz, GATE_ROWS, t), F32)],
        compiler_params=pltpu.CompilerParams(
            dimension_semantics=("arbitrary", "arbitrary"), vmem_limit_bytes=VMEM_LIMIT),
        name="inproj",
    )(x, mod, g1, w_in_p, pos3, inv_full, sgn, qg2, kg2)


def _hgrn_chunk(zq, zf, zi, zg, lb, hg_g, st_ref, h, tri):
    c = HG_CHUNK
    logsig = jnp.minimum(zf, 0.0) - jnp.log1p(jnp.exp(-jnp.abs(zf)))
    a = jnp.log(lb)
    bb = jnp.log1p(-lb) + logsig
    logf = jnp.maximum(a, bb) + jnp.log1p(jnp.exp(-jnp.abs(a - bb)))
    k = (1.0 - lb) * (1.0 / (1.0 + jnp.exp(zf)))
    q = _silu(zq)
    v = zi
    bc = jnp.dot(tri, logf, preferred_element_type=F32, precision=lax.Precision.HIGHEST)

    col = lax.broadcasted_iota(jnp.int32, (HG_SUB, c), 1)
    row = lax.broadcasted_iota(jnp.int32, (HG_SUB, c), 0)
    rows_a = []
    for i in range(c // HG_SUB):
        lo = i * HG_SUB
        b_i = bc[lo:lo + HG_SUB]
        q_i = q[lo:lo + HG_SUB]
        k_i = k[lo:lo + HG_SUB]
        a_i = jnp.zeros((HG_SUB, c), F32)
        for s in range(HG_SUB):
            e = jnp.exp(jnp.minimum(b_i - b_i[s:s + 1], 0.0)) * q_i * k_i[s:s + 1]
            a_i = jnp.where(col == lo + s, jnp.sum(e, axis=-1, keepdims=True), a_i)
        if i > 0:
            r_i = b_i[0:1]
            qh = (q_i * jnp.exp(b_i - r_i)).astype(BF16)
            kh = (k * jnp.exp(jnp.minimum(r_i - bc, 0.0))).astype(BF16)
            a_i = jnp.where(col < lo, _dot_nt(qh, kh), a_i)
        a_i = jnp.where(col <= row + lo, a_i, 0.0)
        rows_a.append(a_i)
    amat = jnp.concatenate(rows_a, axis=0)

    st = st_ref[h]
    o = _dot_nt((q * jnp.exp(bc)).astype(BF16), st.astype(BF16)) + _dot(amat.astype(BF16), v.astype(BF16))
    bl = bc[c - 1:c]
    kdec = (k * jnp.exp(bl - bc)).astype(BF16)
    st_ref[h] = jnp.exp(bl) * st + _dot(v.T.astype(BF16), kdec)

    y = o * lax.rsqrt(jnp.mean(o * o, axis=-1, keepdims=True) + EPS) * hg_g
    return y * _silu(zg)


def _hgrn_kernel(z_ref, lbl_ref, g_ref, o_ref, st_ref, *, l_idx):
    @pl.when(pl.program_id(1) == 0)
    def _():
        st_ref[...] = jnp.zeros_like(st_ref)

    lg = lbl_ref[...]
    ex = jnp.exp(lg - jnp.max(lg, axis=0, keepdims=True))
    sm = ex / jnp.sum(ex, axis=0, keepdims=True)
    lb_all = jnp.sum(sm[:l_idx + 1], axis=0, keepdims=True)

    c = HG_CHUNK
    ri = lax.broadcasted_iota(jnp.int32, (c, c), 0)
    ci = lax.broadcasted_iota(jnp.int32, (c, c), 1)
    tri = (ci <= ri).astype(F32)
    n_chunks = z_ref.shape[1] // c

    def body(ch, carry):
        r0 = pl.multiple_of(ch * c, c)
        for h in range(HG_HEADS):
            sl = slice(h * HG_HD, (h + 1) * HG_HD)
            zq = z_ref[0, pl.ds(r0, c), h * HG_HD:(h + 1) * HG_HD]
            zf = z_ref[0, pl.ds(r0, c), HG_WIDTH + h * HG_HD:HG_WIDTH + (h + 1) * HG_HD]
            zi = z_ref[0, pl.ds(r0, c), 2 * HG_WIDTH + h * HG_HD:2 * HG_WIDTH + (h + 1) * HG_HD]
            zg = z_ref[0, pl.ds(r0, c), 3 * HG_WIDTH + h * HG_HD:3 * HG_WIDTH + (h + 1) * HG_HD]
            y = _hgrn_chunk(zq, zf, zi, zg, lb_all[:, sl], g_ref[...], st_ref, h, tri)
            o_ref[0, pl.ds(r0, c), h * HG_HD:(h + 1) * HG_HD] = y.astype(o_ref.dtype)
        return carry

    lax.fori_loop(0, n_chunks, body, 0)


def _hgrn(zhg, lb_logits, hg_g, l_idx, tb):
    bsz, t, _ = zhg.shape
    return pl.pallas_call(
        functools.partial(_hgrn_kernel, l_idx=l_idx),
        grid=(bsz, t // tb),
        in_specs=[pl.BlockSpec((1, tb, 4 * HG_WIDTH), lambda b, i: (b, i, 0)),
                  pl.BlockSpec(lb_logits.shape, lambda b, i: (0, 0)),
                  pl.BlockSpec((1, HG_HD), lambda b, i: (0, 0))],
        out_specs=pl.BlockSpec((1, tb, HG_WIDTH), lambda b, i: (b, i, 0)),
        out_shape=jax.ShapeDtypeStruct((bsz, t, HG_WIDTH), BF16),
        scratch_shapes=[pltpu.VMEM((HG_HEADS, HG_HD, HG_HD), F32)],
        compiler_params=pltpu.CompilerParams(
            dimension_semantics=("arbitrary", "arbitrary"), vmem_limit_bytes=VMEM_LIMIT),
        name="hgrn",
    )(zhg, lb_logits, hg_g)


def _compress_kernel(xk_ref, xv_ref, pe_ref, w1_ref, w2_ref, ko_ref, vo_ref):
    half = CMP_STRIDE * NSA_HD
    outs = []
    for kv, x_ref in enumerate((xk_ref, xv_ref)):
        x = x_ref[0, 0]
        ha = _dot((x + pe_ref[kv, 0:1, :]).astype(BF16), w1_ref[kv, :half, :])
        hb = _dot((x + pe_ref[kv, 1:2, :]).astype(BF16), w1_ref[kv, half:, :])
        n = x.shape[0]
        pre = ha + pltpu.roll(hb, n - 1, axis=0)
        outs.append(_dot(_silu(pre).astype(BF16), w2_ref[kv]))
    ko_ref[0, 0] = outs[0].astype(ko_ref.dtype)
    vo_ref[0, 0] = outs[1].T.astype(vo_ref.dtype)


def _compress(xk, xv, pe2, w1, w2):
    bsz, g, nseg, width = xk.shape
    x_spec = pl.BlockSpec((1, 1, nseg, width), lambda b, j: (b, j, 0, 0))
    return pl.pallas_call(
        _compress_kernel,
        grid=(bsz, g),
        in_specs=[x_spec, x_spec,
                  pl.BlockSpec(pe2.shape, lambda b, j: (0, 0, 0)),
                  pl.BlockSpec(w1.shape, lambda b, j: (0, 0, 0)),
                  pl.BlockSpec(w2.shape, lambda b, j: (0, 0, 0))],
        out_specs=[pl.BlockSpec((1, 1, nseg, NSA_HD), lambda b, j: (b, j, 0, 0)),
                   pl.BlockSpec((1, 1, NSA_HD, nseg), lambda b, j: (b, j, 0, 0))],
        out_shape=[jax.ShapeDtypeStruct((bsz, g, nseg, NSA_HD), BF16),
                   jax.ShapeDtypeStruct((bsz, g, NSA_HD, nseg), BF16)],
        compiler_params=pltpu.CompilerParams(
            dimension_semantics=("arbitrary", "arbitrary"), vmem_limit_bytes=VMEM_LIMIT),
        name="compress",
    )(xk, xv, pe2, w1, w2)


def _nsa_kernel(qt_ref, kc_ref, vct_ref, ks_ref, vst_ref, kw_ref, vwt_ref, gt_ref, mt_ref,
                o_ref, sel_ref, *, tq, tks):
    r = NSA_GROUP
    g = pl.program_id(1)
    q0 = pl.program_id(2) * tq
    q_t = jnp.concatenate([qt_ref[0, hh] for hh in range(r)], axis=1)

    def tile_heads(a):
        return jnp.concatenate([a] * r, axis=1)

    n_blk = kc_ref.shape[2]
    s = _dot(kc_ref[0, 0], q_t)
    blk_end = lax.broadcasted_iota(jnp.int32, (n_blk, tq), 0) * CMP_STRIDE + (CMP_BLOCK - 1)
    t_row = q0 + lax.broadcasted_iota(jnp.int32, (1, tq), 1)
    cbias = tile_heads(jnp.where(blk_end <= t_row, 0.0, NEG))
    some = tile_heads((t_row >= CMP_BLOCK - 1).astype(F32))
    s = jnp.where(cbias < 0.0, NEG, s)
    e = jnp.exp(s - jnp.max(s, axis=0, keepdims=True))
    p = e * (some / jnp.sum(e, axis=0, keepdims=True))
    o_cmp = _dot(vct_ref[0, 0], p.astype(BF16))

    nb = mt_ref.shape[0]
    psum = p[:, 0:tq]
    for hh in range(1, r):
        psum = psum + p[:, hh * tq:(hh + 1) * tq]
    imp = jnp.dot(mt_ref[...], psum, preferred_element_type=F32, precision=lax.Precision.HIGHEST)
    j = lax.broadcasted_iota(jnp.int32, (nb, tq), 0)
    cur = jnp.right_shift(t_row, SLC_BLOCK.bit_length() - 1)
    forced = (j == 0) | (j == cur) | (j == cur - 1)
    imp = jnp.where(j <= cur, jnp.where(forced, jnp.inf, imp), -1.0)
    rank = jnp.zeros((nb, tq), jnp.int32)
    for i in range(nb):
        row_i = imp[i:i + 1, :]
        ahead = (row_i > imp) | ((row_i == imp) & (j > i))
        rank = rank + ahead.astype(jnp.int32)
    sel_ref[...] = jnp.where(rank < min(SLC_TOPK, nb), 0.0, NEG)

    blocks_per_tile = tks // SLC_BLOCK
    rel = (lax.broadcasted_iota(jnp.int32, (tks, tq), 1)
           - lax.broadcasted_iota(jnp.int32, (tks, tq), 0))

    def slc_body(it, carry):
        m, l, acc = carry
        k0 = pl.multiple_of(it * tks, tks)
        rows = sel_ref[pl.ds(pl.multiple_of(it * blocks_per_tile, blocks_per_tile), blocks_per_tile), :]
        bias = jnp.concatenate([jnp.broadcast_to(rows[jj:jj + 1], (SLC_BLOCK, tq))
                                for jj in range(blocks_per_tile)], axis=0)
        bias = jnp.where(rel + (q0 - k0) >= 0, bias, NEG)
        sc = _dot(ks_ref[0, 0, pl.ds(k0, tks), :], q_t) + tile_heads(bias)
        m_new = jnp.maximum(m, jnp.max(sc, axis=0, keepdims=True))
        alpha = jnp.exp(m - m_new)
        pe = jnp.exp(sc - m_new)
        l = alpha * l + jnp.sum(pe, axis=0, keepdims=True)
        acc = alpha * acc + _dot(vst_ref[0, 0, :, pl.ds(k0, tks)], pe.astype(BF16))
        return m_new, l, acc

    init = (jnp.full((1, r * tq), NEG, F32), jnp.zeros((1, r * tq), F32), jnp.zeros((NSA_HD, r * tq), F32))
    n_it = (q0 + tq + tks - 1) // tks
    _, l_s, acc_s = lax.fori_loop(0, n_it, slc_body, init)
    o_slc = acc_s * (1.0 / l_s)

    span = WIN + tq
    start = pl.multiple_of(jnp.maximum(q0 - WIN, 0), tq)
    dist = (lax.broadcasted_iota(jnp.int32, (span, tq), 1)
            - lax.broadcasted_iota(jnp.int32, (span, tq), 0)) + (q0 - start)
    wbias = jnp.where((dist >= 0) & (dist < WIN), 0.0, NEG)
    sw = _dot(kw_ref[0, 0, pl.ds(start, span), :], q_t) + tile_heads(wbias)
    ew = jnp.exp(sw - jnp.max(sw, axis=0, keepdims=True))
    o_win = _dot(vwt_ref[0, 0, :, pl.ds(start, span)], ew.astype(BF16)) * (1.0 / jnp.sum(ew, axis=0, keepdims=True))

    for hh in range(r):
        cols = slice(hh * tq, (hh + 1) * tq)
        o_h = jnp.zeros((NSA_HD, tq), F32)
        for br, o_b in enumerate((o_cmp, o_slc, o_win)):
            gate = gt_ref[0, pl.ds((g * r + hh) * N_BRANCH + br, 1), :]
            o_h = o_h + gate * o_b[:, cols]
        o_ref[0, :, hh * NSA_HD:(hh + 1) * NSA_HD] = o_h.T.astype(o_ref.dtype)


def _nsa(q_t, kc, vct, ks, vst, kw, vwt, gates_t, mt, tq, tks):
    bsz, _, _, t = q_t.shape
    n_blk = kc.shape[2]
    full = lambda b, g, i: (b, g, 0, 0)
    k_spec = pl.BlockSpec((1, 1, t, NSA_HD), full)
    vt_spec = pl.BlockSpec((1, 1, NSA_HD, t), full)
    return pl.pallas_call(
        functools.partial(_nsa_kernel, tq=tq, tks=tks),
        grid=(bsz, NSA_KV_HEADS, t // tq),
        in_specs=[pl.BlockSpec((1, NSA_GROUP, NSA_HD, tq), lambda b, g, i: (b, g, 0, i)),
                  pl.BlockSpec((1, 1, n_blk, NSA_HD), full),
                  pl.BlockSpec((1, 1, NSA_HD, n_blk), full),
                  k_spec, vt_spec, k_spec, vt_spec,
                  pl.BlockSpec((1, GATE_ROWS, tq), lambda b, g, i: (b, 0, i)),
                  pl.BlockSpec(mt.shape, lambda b, g, i: (0, 0))],
        out_specs=pl.BlockSpec((1, tq, NSA_GROUP * NSA_HD), lambda b, g, i: (b, i, g)),
        out_shape=jax.ShapeDtypeStruct((bsz, t, NSA_WIDTH), BF16),
        scratch_shapes=[pltpu.VMEM((mt.shape[0], tq), F32)],
        compiler_params=pltpu.CompilerParams(
            dimension_semantics=("arbitrary", "arbitrary", "arbitrary"), vmem_limit_bytes=VMEM_LIMIT),
        name="nsa",
    )(q_t, kc, vct, ks, vst, kw, vwt, gates_t, mt)


def _causal_conv(u, prev, cw, cb, row):
    u1 = jnp.where(row == 0, prev[SUBLANES - 1:SUBLANES], pltpu.roll(u, 1, axis=0))
    u2 = jnp.where(row == 0, prev[SUBLANES - 2:SUBLANES - 1],
                   jnp.where(row == 1, prev[SUBLANES - 1:SUBLANES], pltpu.roll(u, 2, axis=0)))
    return cb + u2 * cw[0:1] + u1 * cw[1:2] + u * cw[2:3]


def _ffn_kernel(x_ref, hg_ref, ns_ref, mod_ref, wo_ref, g2_ref, wa_ref, wv_ref, cwa_ref, cwv_ref,
                cba_ref, cbv_ref, wd_ref, o_ref, x1_ref, h2_ref, acc_ref, ca_ref, cv_ref):
    m = pl.program_id(1)
    f = pl.program_id(2)
    nf = pl.num_programs(2)

    @pl.when(f == 0)
    def _():
        mix = _dot(hg_ref[0], wo_ref[:HG_WIDTH, :]) + _dot(ns_ref[0], wo_ref[HG_WIDTH:, :])
        x1 = x_ref[0] + mod_ref[0, 2:3, :] * mix
        x1_ref[...] = x1
        y = x1 * lax.rsqrt(jnp.mean(x1 * x1, axis=-1, keepdims=True) + EPS) * g2_ref[...]
        h2_ref[...] = (y * (1.0 + mod_ref[0, 4:5, :]) + mod_ref[0, 3:4, :]).astype(h2_ref.dtype)
        acc_ref[...] = jnp.zeros_like(acc_ref)

    h2 = h2_ref[...]
    ua = _dot(h2, wa_ref[...])
    uv = _dot(h2, wv_ref[...])
    tm = ua.shape[0]
    first = m == 0
    pa = jnp.where(first, 0.0, ca_ref[f])
    pv = jnp.where(first, 0.0, cv_ref[f])
    ca_ref[f] = ua[tm - SUBLANES:]
    cv_ref[f] = uv[tm - SUBLANES:]
    row = lax.broadcasted_iota(jnp.int32, ua.shape, 0)
    a = _causal_conv(ua, pa, cwa_ref[...], cba_ref[...], row)
    v = _causal_conv(uv, pv, cwv_ref[...], cbv_ref[...], row)
    acc_ref[...] += _dot((_silu(a) * v).astype(BF16), wd_ref[...])

    @pl.when(f == nf - 1)
    def _():
        o_ref[0] = x1_ref[...] + mod_ref[0, 5:6, :] * acc_ref[...]


def _ffn(x, o_hg, o_nsa, mod, w_out, g2, w_up, conv_w, conv_b, w_down, tm, fc):
    bsz, t, _ = x.shape
    nf = D_FF // fc
    row_spec = lambda w: pl.BlockSpec((1, tm, w), lambda b, i, f: (b, i, 0))
    const2 = lambda b, i, f: (0, 0)
    return pl.pallas_call(
        _ffn_kernel,
        grid=(bsz, t // tm, nf),
        in_specs=[row_spec(D_MODEL), row_spec(HG_WIDTH), row_spec(NSA_WIDTH),
                  pl.BlockSpec((1, 6, D_MODEL), lambda b, i, f: (b, 0, 0)),
                  pl.BlockSpec(w_out.shape, const2),
                  pl.BlockSpec((1, D_MODEL), const2),
                  pl.BlockSpec((D_MODEL, fc), lambda b, i, f: (0, f)),
                  pl.BlockSpec((D_MODEL, fc), lambda b, i, f: (0, nf + f)),
                  pl.BlockSpec((CONV_W, fc), lambda b, i, f: (0, f)),
                  pl.BlockSpec((CONV_W, fc), lambda b, i, f: (0, nf + f)),
                  pl.BlockSpec((1, fc), lambda b, i, f: (0, f)),
                  pl.BlockSpec((1, fc), lambda b, i, f: (0, nf + f)),
                  pl.BlockSpec((fc, D_MODEL), lambda b, i, f: (f, 0))],
        out_specs=row_spec(D_MODEL),
        out_shape=jax.ShapeDtypeStruct(x.shape, F32),
        scratch_shapes=[pltpu.VMEM((tm, D_MODEL), F32),
                        pltpu.VMEM((tm, D_MODEL), BF16),
                        pltpu.VMEM((tm, D_MODEL), F32),
                        pltpu.VMEM((nf, SUBLANES, fc), F32),
                        pltpu.VMEM((nf, SUBLANES, fc), F32)],
        compiler_params=pltpu.CompilerParams(
            dimension_semantics=("arbitrary", "arbitrary", "arbitrary"), vmem_limit_bytes=VMEM_LIMIT),
        name="ffn",
    )(x, o_hg, o_nsa, mod, w_out, g2, w_up, w_up, conv_w, conv_w, conv_b, conv_b, w_down)


def _rope_tables():
    half = ROPE_DIM // 2
    inv = ROPE_THETA ** (-jnp.arange(half, dtype=F32) * 2.0 / ROPE_DIM)
    d = np.arange(LANES) % NSA_HD
    inv_full = jnp.where(jnp.asarray(d < ROPE_DIM), jnp.tile(inv, LANES // half), 0.0)
    sgn = np.where(d < half, -1.0, np.where(d < ROPE_DIM, 1.0, 0.0)).astype(np.float32)
    return inv_full.reshape(1, LANES).astype(F32), jnp.asarray(sgn).reshape(1, LANES)


def _selection_tables(t):
    n_seg = t // CMP_STRIDE
    nb = t // SLC_BLOCK
    cst = np.arange(n_seg) * CMP_STRIDE
    sst = np.arange(nb) * SLC_BLOCK
    ovl = np.clip(np.minimum(cst[:, None] + CMP_BLOCK, sst[None] + SLC_BLOCK)
                  - np.maximum(cst[:, None], sst[None]), 0, None) / CMP_BLOCK
    ovl[(t - CMP_BLOCK) // CMP_STRIDE + 1:] = 0.0
    return (jnp.asarray(ovl.T, dtype=F32),)


def _layer(x, mod, pos3, l, p, tables):
    bsz, t, _ = x.shape
    inv_full, sgn, mt = tables
    w_in_p = jnp.pad(p["w_in"][l], ((0, 0), (0, IN_COLS_PAD - IN_COLS))).astype(BF16)
    qg2 = jnp.tile(p["q_norm_g"][l].reshape(1, NSA_HD), (1, LANES // NSA_HD))
    kg2 = jnp.tile(p["k_norm_g"][l], (1, LANES // NSA_HD))
    zhg, q_t, kc, vc, ks, vst, kw, vwt, gates_t = _inproj(
        x, mod, p["norm1_g"][l].reshape(1, D_MODEL), w_in_p, pos3, inv_full, sgn, qg2, kg2, tm=256)

    o_hg = _hgrn(zhg, p["lb_logits"], p["hg_norm_g"][l].reshape(1, HG_HD), l, tb=512)

    n_seg = t // CMP_STRIDE
    seg_w = CMP_STRIDE * NSA_HD
    pe2 = p["pe_cmp"][l].reshape(2, 2, seg_w)
    kcmp, vcmp_t = _compress(kc.reshape(bsz, NSA_KV_HEADS, n_seg, seg_w),
                             vc.reshape(bsz, NSA_KV_HEADS, n_seg, seg_w),
                             pe2, p["w_cmp1"][l].astype(BF16), p["w_cmp2"][l].astype(BF16))
    o_nsa = _nsa(q_t, kcmp, vcmp_t, ks, vst, kw, vwt, gates_t, mt, tq=128, tks=512)

    return _ffn(x, o_hg, o_nsa, mod, p["w_out"][l].astype(BF16), p["norm2_g"][l].reshape(1, D_MODEL),
                p["w_up"][l].astype(BF16), p["conv_w"][l], p["conv_b"][l].reshape(1, 2 * D_FF),
                p["w_down"][l].astype(BF16), tm=256, fc=1408)


def kernel(x, c, positions, w_ada, b_ada, norm1_g, w_in, lb_logits, hg_norm_g, q_norm_g, k_norm_g, pe_cmp, w_cmp1, w_cmp2, w_out, norm2_g, w_up, conv_w, conv_b, w_down):
    p = dict(w_in=w_in, norm1_g=norm1_g, lb_logits=lb_logits, hg_norm_g=hg_norm_g, q_norm_g=q_norm_g,
             k_norm_g=k_norm_g, pe_cmp=pe_cmp, w_cmp1=w_cmp1, w_cmp2=w_cmp2, w_out=w_out,
             norm2_g=norm2_g, w_up=w_up, conv_w=conv_w, conv_b=conv_b, w_down=w_down)
    bsz, t, _ = x.shape
    tables = _rope_tables() + _selection_tables(t)
    pos3 = positions.reshape(bsz, t, 1)
    for l in range(w_ada.shape[0]):
        mod = _ada(c, w_ada[l], b_ada[l]).reshape(bsz, 6, D_MODEL)
        x = _layer(x, mod, pos3, l, p, tables)
    return x
```

```python
import functools

import jax
import jax.numpy as jnp
import numpy as np
from jax import lax
from jax.experimental import pallas as pl
from jax.experimental.pallas import tpu as pltpu

D_MODEL = 1024
HG_HEADS = 4
HG_HD = 128
HG_WIDTH = HG_HEADS * HG_HD
HG_CHUNK = 64
HG_SUB = 16
NSA_HEADS = 8
NSA_KV_HEADS = 2
NSA_HD = 64
NSA_GROUP = NSA_HEADS // NSA_KV_HEADS
NSA_WIDTH = NSA_HEADS * NSA_HD
N_BRANCH = 3
CMP_BLOCK = 32
CMP_STRIDE = 16
CMP_HIDDEN = 256
SLC_BLOCK = 64
SLC_TOPK = 16
WIN = 512
ROPE_DIM = NSA_HD // 4
ROPE_THETA = 500000.0
D_FF = 2816
CONV_W = 3
EPS = 1e-6
NEG = -1e30

LANES = 128
SUBLANES = 8
VMEM_LIMIT = 56 * 1024 * 1024

OFF_HG = 0
OFF_Q = 4 * HG_WIDTH
OFF_KV = OFF_Q + NSA_WIDTH
OFF_G = OFF_KV + 6 * NSA_KV_HEADS * NSA_HD
IN_COLS = OFF_G + N_BRANCH * NSA_HEADS
IN_COLS_PAD = OFF_G + LANES
GATE_ROWS = 32

BF16 = jnp.bfloat16
F32 = jnp.float32


def _dot(a, b):
    return jnp.dot(a, b, preferred_element_type=F32)


def _dot_nt(a, b):
    return lax.dot_general(a, b, (((1,), (1,)), ((), ())), preferred_element_type=F32)


def _sigmoid(x):
    return 1.0 / (1.0 + jnp.exp(-x))


def _silu(x):
    return x * _sigmoid(x)


def _ada_kernel(c_ref, w_ref, b_ref, o_ref):
    cs = _silu(c_ref[...])
    o_ref[...] = jnp.dot(cs, w_ref[...], preferred_element_type=F32,
                         precision=lax.Precision.HIGHEST) + b_ref[...]


def _ada(c, w, b):
    bsz = c.shape[0]
    n = w.shape[1]
    tn = D_MODEL
    return pl.pallas_call(
        _ada_kernel,
        grid=(n // tn,),
        in_specs=[pl.BlockSpec((bsz, D_MODEL), lambda j: (0, 0)),
                  pl.BlockSpec((D_MODEL, tn), lambda j: (0, j)),
                  pl.BlockSpec((1, tn), lambda j: (0, j))],
        out_specs=pl.BlockSpec((bsz, tn), lambda j: (0, j)),
        out_shape=jax.ShapeDtypeStruct((bsz, n), F32),
        name="ada",
    )(c, w, b.reshape(1, n))


def _pair_norm_rope(xp, g, cosv, sinv, lane):
    left = lane < NSA_HD
    sq = xp * xp
    s_l = jnp.sum(jnp.where(left, sq, 0.0), axis=-1, keepdims=True)
    s_r = jnp.sum(jnp.where(left, 0.0, sq), axis=-1, keepdims=True)
    ms = jnp.where(left, s_l, s_r) * (1.0 / NSA_HD)
    xn = xp * lax.rsqrt(ms + EPS) * g
    half = ROPE_DIM // 2
    first = jnp.bitwise_and(lane, NSA_HD - 1) < half
    partner = jnp.where(first, pltpu.roll(xn, LANES - half, axis=1), pltpu.roll(xn, half, axis=1))
    return xn * cosv + partner * sinv


def _inproj_kernel(x_ref, mod_ref, g1_ref, w_ref, pos_ref, inv_ref, sgn_ref, qg_ref, kg_ref,
                   zhg_ref, qt_ref, kc_ref, vc_ref, ks_ref, vst_ref, kw_ref, vwt_ref, gt_ref):
    x = x_ref[0]
    ms = jnp.mean(x * x, axis=-1, keepdims=True)
    y = x * lax.rsqrt(ms + EPS) * g1_ref[...]
    h = (y * (1.0 + mod_ref[0, 1:2, :]) + mod_ref[0, 0:1, :]).astype(BF16)

    zhg_ref[0] = _dot(h, w_ref[:, OFF_HG:OFF_Q])

    tm = x.shape[0]
    lane = lax.broadcasted_iota(jnp.int32, (tm, LANES), 1)
    ang = pos_ref[0].astype(F32) * inv_ref[...]
    cosv = jnp.cos(ang)
    sinv = jnp.sin(ang) * sgn_ref[...]

    zq = _dot(h, w_ref[:, OFF_Q:OFF_KV])
    scale = NSA_HD ** -0.5
    for p in range(NSA_HEADS // 2):
        r = _pair_norm_rope(zq[:, p * LANES:(p + 1) * LANES], qg_ref[...], cosv, sinv, lane)
        rt = (r * scale).T.astype(qt_ref.dtype)
        qt_ref[0, 2 * p] = rt[:NSA_HD]
        qt_ref[0, 2 * p + 1] = rt[NSA_HD:]

    zkv = _dot(h, w_ref[:, OFF_KV:OFF_G])
    k_outs = (kc_ref, ks_ref, kw_ref)
    for br in range(N_BRANCH):
        kk = _pair_norm_rope(zkv[:, (2 * br) * LANES:(2 * br + 1) * LANES],
                             kg_ref[br:br + 1, :], cosv, sinv, lane).astype(k_outs[br].dtype)
        for g in range(NSA_KV_HEADS):
            k_outs[br][0, g] = kk[:, g * NSA_HD:(g + 1) * NSA_HD]
    vc = zkv[:, LANES:2 * LANES]
    for g in range(NSA_KV_HEADS):
        vc_ref[0, g] = vc[:, g * NSA_HD:(g + 1) * NSA_HD]
    for br, vt_ref in ((1, vst_ref), (2, vwt_ref)):
        vt = zkv[:, (2 * br + 1) * LANES:(2 * br + 2) * LANES].T.astype(vt_ref.dtype)
        for g in range(NSA_KV_HEADS):
            vt_ref[0, g] = vt[g * NSA_HD:(g + 1) * NSA_HD]

    gates = _sigmoid(_dot(h, w_ref[:, OFF_G:IN_COLS_PAD]))
    gt_ref[0] = gates.T[:GATE_ROWS]


def _inproj(x, mod, g1, w_in_p, pos3, inv_full, sgn, qg2, kg2, tm):
    bsz, t, _ = x.shape
    grid = (bsz, t // tm)
    kv_shape = (bsz, NSA_KV_HEADS, t, NSA_HD)
    kv_spec = pl.BlockSpec((1, NSA_KV_HEADS, tm, NSA_HD), lambda b, i: (b, 0, i, 0))
    vt_shape = (bsz, NSA_KV_HEADS, NSA_HD, t)
    vt_spec = pl.BlockSpec((1, NSA_KV_HEADS, NSA_HD, tm), lambda b, i: (b, 0, 0, i))
    const = lambda b, i: (0, 0)
    return pl.pallas_call(
        _inproj_kernel,
        grid=grid,
        in_specs=[pl.BlockSpec((1, tm, D_MODEL), lambda b, i: (b, i, 0)),
                  pl.BlockSpec((1, 6, D_MODEL), lambda b, i: (b, 0, 0)),
                  pl.BlockSpec((1, D_MODEL), const),
                  pl.BlockSpec((D_MODEL, IN_COLS_PAD), const),
                  pl.BlockSpec((1, tm, 1), lambda b, i: (b, i, 0)),
                  pl.BlockSpec((1, LANES), const),
                  pl.BlockSpec((1, LANES), const),
                  pl.BlockSpec((1, LANES), const),
                  pl.BlockSpec((N_BRANCH, LANES), const)],
        out_specs=[pl.BlockSpec((1, tm, 4 * HG_WIDTH), lambda b, i: (b, i, 0)),
                   pl.BlockSpec((1, NSA_HEADS, NSA_HD, tm), lambda b, i: (b, 0, 0, i)),
                   kv_spec, kv_spec, kv_spec, vt_spec, kv_spec, vt_spec,
                   pl.BlockSpec((1, GATE_ROWS, tm), lambda b, i: (b, 0, i))],
        out_shape=[jax.ShapeDtypeStruct((bsz, t, 4 * HG_WIDTH), F32),
                   jax.ShapeDtypeStruct((bsz, NSA_HEADS, NSA_HD, t), BF16),
                   jax.ShapeDtypeStruct(kv_shape, F32),
                   jax.ShapeDtypeStruct(kv_shape, F32),
                   jax.ShapeDtypeStruct(kv_shape, BF16),
                   jax.ShapeDtypeStruct(vt_shape, BF16),
                   jax.ShapeDtypeStruct(kv_shape, BF16),
                   jax.ShapeDtypeStruct(vt_shape, BF16),
                   jax.ShapeDtypeStruct((bsz, GATE_ROWS, t), F32)],
        compiler_params=pltpu.CompilerParams(
            dimension_semantics=("arbitrary", "arbitrary"), vmem_limit_bytes=VMEM_LIMIT),
        name="inproj",
    )(x, mod, g1, w_in_p, pos3, inv_full, sgn, qg2, kg2)


def _hgrn_chunk(zq, zf, zi, zg, lb, hg_g, st_ref, h, tri):
    c = HG_CHUNK
    logsig = jnp.minimum(zf, 0.0) - jnp.log1p(jnp.exp(-jnp.abs(zf)))
    a = jnp.log(lb)
    bb = jnp.log1p(-lb) + logsig
    logf = jnp.maximum(a, bb) + jnp.log1p(jnp.exp(-jnp.abs(a - bb)))
    k = (1.0 - lb) * (1.0 / (1.0 + jnp.exp(zf)))
    q = _silu(zq)
    v = zi
    bc = jnp.dot(tri, logf, preferred_element_type=F32, precision=lax.Precision.HIGHEST)

    col = lax.broadcasted_iota(jnp.int32, (HG_SUB, c), 1)
    row = lax.broadcasted_iota(jnp.int32, (HG_SUB, c), 0)
    rows_a = []
    for i in range(c // HG_SUB):
        lo = i * HG_SUB
        b_i = bc[lo:lo + HG_SUB]
        q_i = q[lo:lo + HG_SUB]
        k_i = k[lo:lo + HG_SUB]
        a_i = jnp.zeros((HG_SUB, c), F32)
        for s in range(HG_SUB):
            e = jnp.exp(jnp.minimum(b_i - b_i[s:s + 1], 0.0)) * q_i * k_i[s:s + 1]
            a_i = jnp.where(col == lo + s, jnp.sum(e, axis=-1, keepdims=True), a_i)
        if i > 0:
            r_i = b_i[0:1]
            qh = (q_i * jnp.exp(b_i - r_i)).astype(BF16)
            kh = (k * jnp.exp(jnp.minimum(r_i - bc, 0.0))).astype(BF16)
            a_i = jnp.where(col < lo, _dot_nt(qh, kh), a_i)
        a_i = jnp.where(col <= row + lo, a_i, 0.0)
        rows_a.append(a_i)
    amat = jnp.concatenate(rows_a, axis=0)

    st = st_ref[h]
    o = _dot_nt((q * jnp.exp(bc)).astype(BF16), st.astype(BF16)) + _dot(amat.astype(BF16), v.astype(BF16))
    bl = bc[c - 1:c]
    kdec = (k * jnp.exp(bl - bc)).astype(BF16)
    st_ref[h] = jnp.exp(bl) * st + _dot(v.T.astype(BF16), kdec)

    y = o * lax.rsqrt(jnp.mean(o * o, axis=-1, keepdims=True) + EPS) * hg_g
    return y * _silu(zg)


def _hgrn_kernel(z_ref, lbl_ref, g_ref, o_ref, st_ref, *, l_idx):
    @pl.when(pl.program_id(1) == 0)
    def _():
        st_ref[...] = jnp.zeros_like(st_ref)

    lg = lbl_ref[...]
    ex = jnp.exp(lg - jnp.max(lg, axis=0, keepdims=True))
    sm = ex / jnp.sum(ex, axis=0, keepdims=True)
    lb_all = jnp.sum(sm[:l_idx + 1], axis=0, keepdims=True)

    c = HG_CHUNK
    ri = lax.broadcasted_iota(jnp.int32, (c, c), 0)
    ci = lax.broadcasted_iota(jnp.int32, (c, c), 1)
    tri = (ci <= ri).astype(F32)
    n_chunks = z_ref.shape[1] // c

    def body(ch, carry):
        r0 = pl.multiple_of(ch * c, c)
        for h in range(HG_HEADS):
            sl = slice(h * HG_HD, (h + 1) * HG_HD)
            zq = z_ref[0, pl.ds(r0, c), h * HG_HD:(h + 1) * HG_HD]
            zf = z_ref[0, pl.ds(r0, c), HG_WIDTH + h * HG_HD:HG_WIDTH + (h + 1) * HG_HD]
            zi = z_ref[0, pl.ds(r0, c), 2 * HG_WIDTH + h * HG_HD:2 * HG_WIDTH + (h + 1) * HG_HD]
            zg = z_ref[0, pl.ds(r0, c), 3 * HG_WIDTH + h * HG_HD:3 * HG_WIDTH + (h + 1) * HG_HD]
            y = _hgrn_chunk(zq, zf, zi, zg, lb_all[:, sl], g_ref[...], st_ref, h, tri)
            o_ref[0, pl.ds(r0, c), h * HG_HD:(h + 1) * HG_HD] = y.astype(o_ref.dtype)
        return carry

    lax.fori_loop(0, n_chunks, body, 0)


def _hgrn(zhg, lb_logits, hg_g, l_idx, tb):
    bsz, t, _ = zhg.shape
    return pl.pallas_call(
        functools.partial(_hgrn_kernel, l_idx=l_idx),
        grid=(bsz, t // tb),
        in_specs=[pl.BlockSpec((1, tb, 4 * HG_WIDTH), lambda b, i: (b, i, 0)),
                  pl.BlockSpec(lb_logits.shape, lambda b, i: (0, 0)),
                  pl.BlockSpec((1, HG_HD), lambda b, i: (0, 0))],
        out_specs=pl.BlockSpec((1, tb, HG_WIDTH), lambda b, i: (b, i, 0)),
        out_shape=jax.ShapeDtypeStruct((bsz, t, HG_WIDTH), BF16),
        scratch_shapes=[pltpu.VMEM((HG_HEADS, HG_HD, HG_HD), F32)],
        compiler_params=pltpu.CompilerParams(
            dimension_semantics=("arbitrary", "arbitrary"), vmem_limit_bytes=VMEM_LIMIT),
        name="hgrn",
    )(zhg, lb_logits, hg_g)


def _compress_kernel(xk_ref, xv_ref, pe_ref, w1_ref, w2_ref, ko_ref, vo_ref):
    half = CMP_STRIDE * NSA_HD
    outs = []
    for kv, x_ref in enumerate((xk_ref, xv_ref)):
        x = x_ref[0, 0]
        ha = _dot((x + pe_ref[kv, 0:1, :]).astype(BF16), w1_ref[kv, :half, :])
        hb = _dot((x + pe_ref[kv, 1:2, :]).astype(BF16), w1_ref[kv, half:, :])
        n = x.shape[0]
        pre = ha + pltpu.roll(hb, n - 1, axis=0)
        outs.append(_dot(_silu(pre).astype(BF16), w2_ref[kv]))
    ko_ref[0, 0] = outs[0].astype(ko_ref.dtype)
    vo_ref[0, 0] = outs[1].T.astype(vo_ref.dtype)


def _compress(xk, xv, pe2, w1, w2):
    bsz, g, nseg, width = xk.shape
    x_spec = pl.BlockSpec((1, 1, nseg, width), lambda b, j: (b, j, 0, 0))
    return pl.pallas_call(
        _compress_kernel,
        grid=(bsz, g),
        in_specs=[x_spec, x_spec,
                  pl.BlockSpec(pe2.shape, lambda b, j: (0, 0, 0)),
                  pl.BlockSpec(w1.shape, lambda b, j: (0, 0, 0)),
                  pl.BlockSpec(w2.shape, lambda b, j: (0, 0, 0))],
        out_specs=[pl.BlockSpec((1, 1, nseg, NSA_HD), lambda b, j: (b, j, 0, 0)),
                   pl.BlockSpec((1, 1, NSA_HD, nseg), lambda b, j: (b, j, 0, 0))],
        out_shape=[jax.ShapeDtypeStruct((bsz, g, nseg, NSA_HD), BF16),
                   jax.ShapeDtypeStruct((bsz, g, NSA_HD, nseg), BF16)],
        compiler_params=pltpu.CompilerParams(
            dimension_semantics=("arbitrary", "arbitrary"), vmem_limit_bytes=VMEM_LIMIT),
        name="compress",
    )(xk, xv, pe2, w1, w2)


def _nsa_kernel(qt_ref, kc_ref, vct_ref, ks_ref, vst_ref, kw_ref, vwt_ref, gt_ref, mt_ref,
                o_ref, sel_ref, *, tq, tks):
    r = NSA_GROUP
    g = pl.program_id(1)
    q0 = pl.program_id(2) * tq
    q_t = jnp.concatenate([qt_ref[0, hh] for hh in range(r)], axis=1)

    def tile_heads(a):
        return jnp.concatenate([a] * r, axis=1)

    n_blk = kc_ref.shape[2]
    s = _dot(kc_ref[0, 0], q_t)
    blk_end = lax.broadcasted_iota(jnp.int32, (n_blk, tq), 0) * CMP_STRIDE + (CMP_BLOCK - 1)
    t_row = q0 + lax.broadcasted_iota(jnp.int32, (1, tq), 1)
    cbias = tile_heads(jnp.where(blk_end <= t_row, 0.0, NEG))
    some = tile_heads((t_row >= CMP_BLOCK - 1).astype(F32))
    s = jnp.where(cbias < 0.0, NEG, s)
    e = jnp.exp(s - jnp.max(s, axis=0, keepdims=True))
    p = e * (some / jnp.sum(e, axis=0, keepdims=True))
    o_cmp = _dot(vct_ref[0, 0], p.astype(BF16))

    nb = mt_ref.shape[0]
    psum = p[:, 0:tq]
    for hh in range(1, r):
        psum = psum + p[:, hh * tq:(hh + 1) * tq]
    imp = jnp.dot(mt_ref[...], psum, preferred_element_type=F32, precision=lax.Precision.HIGHEST)
    j = lax.broadcasted_iota(jnp.int32, (nb, tq), 0)
    cur = jnp.right_shift(t_row, SLC_BLOCK.bit_length() - 1)
    forced = (j == 0) | (j == cur) | (j == cur - 1)
    imp = jnp.where(j <= cur, jnp.where(forced, jnp.inf, imp), -1.0)
    rank = jnp.zeros((nb, tq), jnp.int32)
    for i in range(nb):
        row_i = imp[i:i + 1, :]
        ahead = (row_i > imp) | ((row_i == imp) & (j > i))
        rank = rank + ahead.astype(jnp.int32)
    sel_ref[...] = jnp.where(rank < min(SLC_TOPK, nb), 0.0, NEG)

    blocks_per_tile = tks // SLC_BLOCK
    rel = (lax.broadcasted_iota(jnp.int32, (tks, tq), 1)
           - lax.broadcasted_iota(jnp.int32, (tks, tq), 0))

    def slc_body(it, carry):
        m, l, acc = carry
        k0 = pl.multiple_of(it * tks, tks)
        rows = sel_ref[pl.ds(pl.multiple_of(it * blocks_per_tile, blocks_per_tile), blocks_per_tile), :]
        bias = jnp.concatenate([jnp.broadcast_to(rows[jj:jj + 1], (SLC_BLOCK, tq))
                                for jj in range(blocks_per_tile)], axis=0)
        bias = jnp.where(rel + (q0 - k0) >= 0, bias, NEG)
        sc = _dot(ks_ref[0, 0, pl.ds(k0, tks), :], q_t) + tile_heads(bias)
        m_new = jnp.maximum(m, jnp.max(sc, axis=0, keepdims=True))
        alpha = jnp.exp(m - m_new)
        pe = jnp.exp(sc - m_new)
        l = alpha * l + jnp.sum(pe, axis=0, keepdims=True)
        acc = alpha * acc + _dot(vst_ref[0, 0, :, pl.ds(k0, tks)], pe.astype(BF16))
        return m_new, l, acc

    init = (jnp.full((1, r * tq), NEG, F32), jnp.zeros((1, r * tq), F32), jnp.zeros((NSA_HD, r * tq), F32))
    n_it = (q0 + tq + tks - 1) // tks
    _, l_s, acc_s = lax.fori_loop(0, n_it, slc_body, init)
    o_slc = acc_s * (1.0 / l_s)

    span = WIN + tq
    start = pl.multiple_of(jnp.maximum(q0 - WIN, 0), tq)
    dist = (lax.broadcasted_iota(jnp.int32, (span, tq), 1)
            - lax.broadcasted_iota(jnp.int32, (span, tq), 0)) + (q0 - start)
    wbias = jnp.where((dist >= 0) & (dist < WIN), 0.0, NEG)
    sw = _dot(kw_ref[0, 0, pl.ds(start, span), :], q_t) + tile_heads(wbias)
    ew = jnp.exp(sw - jnp.max(sw, axis=0, keepdims=True))
    o_win = _dot(vwt_ref[0, 0, :, pl.ds(start, span)], ew.astype(BF16)) * (1.0 / jnp.sum(ew, axis=0, keepdims=True))

    for hh in range(r):
        cols = slice(hh * tq, (hh + 1) * tq)
        o_h = jnp.zeros((NSA_HD, tq), F32)
        for br, o_b in enumerate((o_cmp, o_slc, o_win)):
            gate = gt_ref[0, pl.ds((g * r + hh) * N_BRANCH + br, 1), :]
            o_h = o_h + gate * o_b[:, cols]
        o_ref[0, :, hh * NSA_HD:(hh + 1) * NSA_HD] = o_h.T.astype(o_ref.dtype)


def _nsa(q_t, kc, vct, ks, vst, kw, vwt, gates_t, mt, tq, tks):
    bsz, _, _, t = q_t.shape
    n_blk = kc.shape[2]
    full = lambda b, g, i: (b, g, 0, 0)
    k_spec = pl.BlockSpec((1, 1, t, NSA_HD), full)
    vt_spec = pl.BlockSpec((1, 1, NSA_HD, t), full)
    return pl.pallas_call(
        functools.partial(_nsa_kernel, tq=tq, tks=tks),
        grid=(bsz, NSA_KV_HEADS, t // tq),
        in_specs=[pl.BlockSpec((1, NSA_GROUP, NSA_HD, tq), lambda b, g, i: (b, g, 0, i)),
                  pl.BlockSpec((1, 1, n_blk, NSA_HD), full),
                  pl.BlockSpec((1, 1, NSA_HD, n_blk), full),
                  k_spec, vt_spec, k_spec, vt_spec,
                  pl.BlockSpec((1, GATE_ROWS, tq), lambda b, g, i: (b, 0, i)),
                  pl.BlockSpec(mt.shape, lambda b, g, i: (0, 0))],
        out_specs=pl.BlockSpec((1, tq, NSA_GROUP * NSA_HD), lambda b, g, i: (b, i, g)),
        out_shape=jax.ShapeDtypeStruct((bsz, t, NSA_WIDTH), BF16),
        scratch_shapes=[pltpu.VMEM((mt.shape[0], tq), F32)],
        compiler_params=pltpu.CompilerParams(
            dimension_semantics=("arbitrary", "arbitrary", "arbitrary"), vmem_limit_bytes=VMEM_LIMIT),
        name="nsa",
    )(q_t, kc, vct, ks, vst, kw, vwt, gates_t, mt)


def _causal_conv(u, prev, cw, cb):
    tm = u.shape[0]
    g = SUBLANES
    r8 = lax.broadcasted_iota(jnp.int32, (g, u.shape[1]), 0)
    wrap1 = jnp.where(r8 == 0, prev[2 * g - 1:2 * g], pltpu.roll(u[tm - g:], 1, axis=0))
    wrap2 = jnp.where(r8 == 0, prev[g - 1:g], pltpu.roll(u[tm - 2 * g:tm - g], 1, axis=0))
    u1 = jnp.concatenate([wrap1, u[:tm - g]], axis=0)
    u2 = jnp.concatenate([wrap2, wrap1, u[:tm - 2 * g]], axis=0)
    return cb + u2 * cw[0:1] + u1 * cw[1:2] + u * cw[2:3]


def _ffn_kernel(x_ref, hg_ref, ns_ref, mod_ref, wo_ref, g2_ref, wu_ref, cw_ref, cb_ref, wd_ref,
                o_ref, carry_ref, buf_ref, h2_ref, u_ref, g_ref, acc_ref, *, fc):
    mix = _dot(hg_ref[0], wo_ref[:HG_WIDTH, :]) + _dot(ns_ref[0], wo_ref[HG_WIDTH:, :])
    x1_nat = x_ref[0] + mod_ref[0, 2:3, :] * mix
    n_lt = D_MODEL // LANES
    for c in range(n_lt):
        buf_ref[c] = x1_nat[:, c * LANES:(c + 1) * LANES]

    tm = buf_ref.shape[1]
    groups = tm // SUBLANES
    x1 = jnp.concatenate(
        [jnp.concatenate([buf_ref[c, pl.ds(j, SUBLANES, stride=groups), :] for j in range(groups)], axis=0)
         for c in range(n_lt)], axis=1)
    y = x1 * lax.rsqrt(jnp.mean(x1 * x1, axis=-1, keepdims=True) + EPS) * g2_ref[...]
    h2_ref[...] = (y * (1.0 + mod_ref[0, 4:5, :]) + mod_ref[0, 3:4, :]).astype(h2_ref.dtype)
    acc_ref[...] = jnp.zeros_like(acc_ref)
    first = pl.program_id(1) == 0

    def up(jc, slot):
        for half in range(2):
            off = pl.multiple_of(half * D_FF + jc * fc, fc)
            u_ref[slot, half] = _dot(h2_ref[...], wu_ref[:, pl.ds(off, fc)])

    def act(jc, slot):
        halves = []
        for half in range(2):
            cols = pl.ds(pl.multiple_of(half * D_FF + jc * fc, fc), fc)
            u = u_ref[slot, half]
            prev = jnp.where(first, 0.0, carry_ref[:, cols])
            carry_ref[:, cols] = u[tm - 2 * SUBLANES:]
            halves.append(_causal_conv(u, prev, cw_ref[:, cols], cb_ref[:, cols]))
        a, v = halves
        g_ref[slot] = (_silu(a) * v).astype(g_ref.dtype)

    def down(jc, slot):
        acc_ref[...] += _dot(g_ref[slot], wd_ref[pl.ds(pl.multiple_of(jc * fc, fc), fc), :])

    n = D_FF // fc

    def tick(i):
        if i < n:
            up(i, i % 2)
        if 0 <= i - 1 < n:
            act(i - 1, (i - 1) % 2)
        if 0 <= i - 2 < n:
            down(i - 2, i % 2)

    tick(0)
    tick(1)
    pairs = (n - 2) // 2

    def body(k, carry):
        i = 2 + 2 * k
        up(i, 0)
        act(i - 1, 1)
        down(i - 2, 0)
        up(i + 1, 1)
        act(i, 0)
        down(i - 1, 1)
        return carry

    lax.fori_loop(0, pairs, body, 0)
    for i in range(2 + 2 * pairs, n + 2):
        tick(i)

    out = x1 + mod_ref[0, 5:6, :] * acc_ref[...]
    for c in range(n_lt):
        for j in range(groups):
            buf_ref[c, pl.ds(j, SUBLANES, stride=groups), :] = out[j * SUBLANES:(j + 1) * SUBLANES,
                                                                   c * LANES:(c + 1) * LANES]
    for c in range(n_lt):
        o_ref[0, :, c * LANES:(c + 1) * LANES] = buf_ref[c]


def _ffn(x, o_hg, o_nsa, mod, w_out, g2, w_up, conv_w, conv_b, w_down, tm, fc):
    bsz, t, _ = x.shape
    row_spec = lambda w: pl.BlockSpec((1, tm, w), lambda b, i: (b, i, 0))
    resident = lambda a: pl.BlockSpec(a.shape, lambda b, i: (0, 0), pipeline_mode=pl.Buffered(1))
    return pl.pallas_call(
        functools.partial(_ffn_kernel, fc=fc),
        grid=(bsz, t // tm),
        in_specs=[row_spec(D_MODEL), row_spec(HG_WIDTH), row_spec(NSA_WIDTH),
                  pl.BlockSpec((1, 6, D_MODEL), lambda b, i: (b, 0, 0)),
                  resident(w_out), resident(g2), resident(w_up), resident(conv_w), resident(conv_b),
                  resident(w_down)],
        out_specs=row_spec(D_MODEL),
        out_shape=jax.ShapeDtypeStruct(x.shape, F32),
        scratch_shapes=[pltpu.VMEM((2 * SUBLANES, 2 * D_FF), F32),
                        pltpu.VMEM((D_MODEL // LANES, tm, LANES), F32),
                        pltpu.VMEM((tm, D_MODEL), BF16),
                        pltpu.VMEM((2, 2, tm, fc), F32),
                        pltpu.VMEM((2, tm, fc), BF16),
                        pltpu.VMEM((tm, D_MODEL), F32)],
        compiler_params=pltpu.CompilerParams(
            dimension_semantics=("arbitrary", "arbitrary"), vmem_limit_bytes=VMEM_LIMIT),
        name="ffn",
    )(x, o_hg, o_nsa, mod, w_out, g2, w_up, conv_w, conv_b, w_down)


def _rope_tables():
    half = ROPE_DIM // 2
    inv = ROPE_THETA ** (-jnp.arange(half, dtype=F32) * 2.0 / ROPE_DIM)
    d = np.arange(LANES) % NSA_HD
    inv_full = jnp.where(jnp.asarray(d < ROPE_DIM), jnp.tile(inv, LANES // half), 0.0)
    sgn = np.where(d < half, -1.0, np.where(d < ROPE_DIM, 1.0, 0.0)).astype(np.float32)
    return inv_full.reshape(1, LANES).astype(F32), jnp.asarray(sgn).reshape(1, LANES)


def _selection_tables(t):
    n_seg = t // CMP_STRIDE
    nb = t // SLC_BLOCK
    cst = np.arange(n_seg) * CMP_STRIDE
    sst = np.arange(nb) * SLC_BLOCK
    ovl = np.clip(np.minimum(cst[:, None] + CMP_BLOCK, sst[None] + SLC_BLOCK)
                  - np.maximum(cst[:, None], sst[None]), 0, None) / CMP_BLOCK
    ovl[(t - CMP_BLOCK) // CMP_STRIDE + 1:] = 0.0
    return (jnp.asarray(ovl.T, dtype=F32),)


def _layer(x, mod, pos3, l, p, tables):
    bsz, t, _ = x.shape
    inv_full, sgn, mt = tables
    w_in_p = jnp.pad(p["w_in"][l], ((0, 0), (0, IN_COLS_PAD - IN_COLS))).astype(BF16)
    qg2 = jnp.tile(p["q_norm_g"][l].reshape(1, NSA_HD), (1, LANES // NSA_HD))
    kg2 = jnp.tile(p["k_norm_g"][l], (1, LANES // NSA_HD))
    zhg, q_t, kc, vc, ks, vst, kw, vwt, gates_t = _inproj(
        x, mod, p["norm1_g"][l].reshape(1, D_MODEL), w_in_p, pos3, inv_full, sgn, qg2, kg2, tm=256)

    o_hg = _hgrn(zhg, p["lb_logits"], p["hg_norm_g"][l].reshape(1, HG_HD), l, tb=512)

    n_seg = t // CMP_STRIDE
    seg_w = CMP_STRIDE * NSA_HD
    pe2 = p["pe_cmp"][l].reshape(2, 2, seg_w)
    kcmp, vcmp_t = _compress(kc.reshape(bsz, NSA_KV_HEADS, n_seg, seg_w),
                             vc.reshape(bsz, NSA_KV_HEADS, n_seg, seg_w),
                             pe2, p["w_cmp1"][l].astype(BF16), p["w_cmp2"][l].astype(BF16))
    o_nsa = _nsa(q_t, kcmp, vcmp_t, ks, vst, kw, vwt, gates_t, mt, tq=128, tks=512)

    return _ffn(x, o_hg, o_nsa, mod, p["w_out"][l].astype(BF16), p["norm2_g"][l].reshape(1, D_MODEL),
                p["w_up"][l].astype(BF16), p["conv_w"][l], p["conv_b"][l].reshape(1, 2 * D_FF),
                p["w_down"][l].astype(BF16), tm=512, fc=256)


def kernel(x, c, positions, w_ada, b_ada, norm1_g, w_in, lb_logits, hg_norm_g, q_norm_g, k_norm_g, pe_cmp, w_cmp1, w_cmp2, w_out, norm2_g, w_up, conv_w, conv_b, w_down):
    p = dict(w_in=w_in, norm1_g=norm1_g, lb_logits=lb_logits, hg_norm_g=hg_norm_g, q_norm_g=q_norm_g,
             k_norm_g=k_norm_g, pe_cmp=pe_cmp, w_cmp1=w_cmp1, w_cmp2=w_cmp2, w_out=w_out,
             norm2_g=norm2_g, w_up=w_up, conv_w=conv_w, conv_b=conv_b, w_down=w_down)
    bsz, t, _ = x.shape
    tables = _rope_tables() + _selection_tables(t)
    pos3 = positions.reshape(bsz, t, 1)
    for l in range(w_ada.shape[0]):
        mod = _ada(c, w_ada[l], b_ada[l]).reshape(bsz, 6, D_MODEL)
        x = _layer(x, mod, pos3, l, p, tables)
    return x
```

```python
import functools

import jax
import jax.numpy as jnp
import numpy as np
from jax import lax
from jax.experimental import pallas as pl
from jax.experimental.pallas import tpu as pltpu

D_MODEL = 1024
HG_HEADS = 4
HG_HD = 128
HG_WIDTH = HG_HEADS * HG_HD
HG_CHUNK = 64
HG_SUB = 16
NSA_HEADS = 8
NSA_KV_HEADS = 2
NSA_HD = 64
NSA_GROUP = NSA_HEADS // NSA_KV_HEADS
NSA_WIDTH = NSA_HEADS * NSA_HD
N_BRANCH = 3
CMP_BLOCK = 32
CMP_STRIDE = 16
CMP_HIDDEN = 256
SLC_BLOCK = 64
SLC_TOPK = 16
WIN = 512
ROPE_DIM = NSA_HD // 4
ROPE_THETA = 500000.0
D_FF = 2816
CONV_W = 3
EPS = 1e-6
NEG = -1e30

LANES = 128
SUBLANES = 8
VMEM_LIMIT = 56 * 1024 * 1024

OFF_HG = 0
OFF_Q = 4 * HG_WIDTH
OFF_KV = OFF_Q + NSA_WIDTH
OFF_G = OFF_KV + 6 * NSA_KV_HEADS * NSA_HD
IN_COLS = OFF_G + N_BRANCH * NSA_HEADS
IN_COLS_PAD = OFF_G + LANES
GATE_ROWS = 32

BF16 = jnp.bfloat16
F32 = jnp.float32


def _dot(a, b):
    return jnp.dot(a, b, preferred_element_type=F32)


def _dot_nt(a, b):
    return lax.dot_general(a, b, (((1,), (1,)), ((), ())), preferred_element_type=F32)


def _sigmoid(x):
    return 1.0 / (1.0 + jnp.exp(-x))


def _silu(x):
    return x * _sigmoid(x)


def _ada_kernel(c_ref, w_ref, b_ref, o_ref):
    cs = _silu(c_ref[...])
    o_ref[...] = jnp.dot(cs, w_ref[...], preferred_element_type=F32,
                         precision=lax.Precision.HIGHEST) + b_ref[...]


def _ada(c, w, b):
    bsz = c.shape[0]
    n = w.shape[1]
    tn = D_MODEL
    return pl.pallas_call(
        _ada_kernel,
        grid=(n // tn,),
        in_specs=[pl.BlockSpec((bsz, D_MODEL), lambda j: (0, 0)),
                  pl.BlockSpec((D_MODEL, tn), lambda j: (0, j)),
                  pl.BlockSpec((1, tn), lambda j: (0, j))],
        out_specs=pl.BlockSpec((bsz, tn), lambda j: (0, j)),
        out_shape=jax.ShapeDtypeStruct((bsz, n), F32),
        name="ada",
    )(c, w, b.reshape(1, n))


def _pair_norm_rope(xp, g, cosv, sinv, lane):
    left = lane < NSA_HD
    sq = xp * xp
    s_l = jnp.sum(jnp.where(left, sq, 0.0), axis=-1, keepdims=True)
    s_r = jnp.sum(jnp.where(left, 0.0, sq), axis=-1, keepdims=True)
    ms = jnp.where(left, s_l, s_r) * (1.0 / NSA_HD)
    xn = xp * lax.rsqrt(ms + EPS) * g
    half = ROPE_DIM // 2
    first = jnp.bitwise_and(lane, NSA_HD - 1) < half
    partner = jnp.where(first, pltpu.roll(xn, LANES - half, axis=1), pltpu.roll(xn, half, axis=1))
    return xn * cosv + partner * sinv


def _inproj_kernel(x_ref, mod_ref, g1_ref, w_ref, pos_ref, inv_ref, sgn_ref, qg_ref, kg_ref,
                   zhg_ref, qt_ref, kc_ref, vc_ref, ks_ref, vst_ref, kw_ref, vwt_ref, gt_ref):
    x = x_ref[0]
    ms = jnp.mean(x * x, axis=-1, keepdims=True)
    y = x * lax.rsqrt(ms + EPS) * g1_ref[...]
    h = (y * (1.0 + mod_ref[0, 1:2, :]) + mod_ref[0, 0:1, :]).astype(BF16)

    zhg_ref[0] = _dot(h, w_ref[:, OFF_HG:OFF_Q])

    tm = x.shape[0]
    lane = lax.broadcasted_iota(jnp.int32, (tm, LANES), 1)
    ang = pos_ref[0].astype(F32) * inv_ref[...]
    cosv = jnp.cos(ang)
    sinv = jnp.sin(ang) * sgn_ref[...]

    zq = _dot(h, w_ref[:, OFF_Q:OFF_KV])
    scale = NSA_HD ** -0.5
    for p in range(NSA_HEADS // 2):
        r = _pair_norm_rope(zq[:, p * LANES:(p + 1) * LANES], qg_ref[...], cosv, sinv, lane)
        rt = (r * scale).T.astype(qt_ref.dtype)
        qt_ref[0, 2 * p] = rt[:NSA_HD]
        qt_ref[0, 2 * p + 1] = rt[NSA_HD:]

    zkv = _dot(h, w_ref[:, OFF_KV:OFF_G])
    k_outs = (kc_ref, ks_ref, kw_ref)
    for br in range(N_BRANCH):
        kk = _pair_norm_rope(zkv[:, (2 * br) * LANES:(2 * br + 1) * LANES],
                             kg_ref[br:br + 1, :], cosv, sinv, lane).astype(k_outs[br].dtype)
        for g in range(NSA_KV_HEADS):
            k_outs[br][0, g] = kk[:, g * NSA_HD:(g + 1) * NSA_HD]
    vc = zkv[:, LANES:2 * LANES]
    for g in range(NSA_KV_HEADS):
        vc_ref[0, g] = vc[:, g * NSA_HD:(g + 1) * NSA_HD]
    for br, vt_ref in ((1, vst_ref), (2, vwt_ref)):
        vt = zkv[:, (2 * br + 1) * LANES:(2 * br + 2) * LANES].T.astype(vt_ref.dtype)
        for g in range(NSA_KV_HEADS):
            vt_ref[0, g] = vt[g * NSA_HD:(g + 1) * NSA_HD]

    gates = _sigmoid(_dot(h, w_ref[:, OFF_G:IN_COLS_PAD]))
    gt_ref[0] = gates.T[:GATE_ROWS]


def _inproj(x, mod, g1, w_in_p, pos3, inv_full, sgn, qg2, kg2, tm):
    bsz, t, _ = x.shape
    grid = (bsz, t // tm)
    kv_shape = (bsz, NSA_KV_HEADS, t, NSA_HD)
    kv_spec = pl.BlockSpec((1, NSA_KV_HEADS, tm, NSA_HD), lambda b, i: (b, 0, i, 0))
    vt_shape = (bsz, NSA_KV_HEADS, NSA_HD, t)
    vt_spec = pl.BlockSpec((1, NSA_KV_HEADS, NSA_HD, tm), lambda b, i: (b, 0, 0, i))
    const = lambda b, i: (0, 0)
    return pl.pallas_call(
        _inproj_kernel,
        grid=grid,
        in_specs=[pl.BlockSpec((1, tm, D_MODEL), lambda b, i: (b, i, 0)),
                  pl.BlockSpec((1, 6, D_MODEL), lambda b, i: (b, 0, 0)),
                  pl.BlockSpec((1, D_MODEL), const),
                  pl.BlockSpec((D_MODEL, IN_COLS_PAD), const),
                  pl.BlockSpec((1, tm, 1), lambda b, i: (b, i, 0)),
                  pl.BlockSpec((1, LANES), const),
                  pl.BlockSpec((1, LANES), const),
                  pl.BlockSpec((1, LANES), const),
                  pl.BlockSpec((N_BRANCH, LANES), const)],
        out_specs=[pl.BlockSpec((1, tm, 4 * HG_WIDTH), lambda b, i: (b, i, 0)),
                   pl.BlockSpec((1, NSA_HEADS, NSA_HD, tm), lambda b, i: (b, 0, 0, i)),
                   kv_spec, kv_spec, kv_spec, vt_spec, kv_spec, vt_spec,
                   pl.BlockSpec((1, GATE_ROWS, tm), lambda b, i: (b, 0, i))],
        out_shape=[jax.ShapeDtypeStruct((bsz, t, 4 * HG_WIDTH), F32),
                   jax.ShapeDtypeStruct((bsz, NSA_HEADS, NSA_HD, t), BF16),
                   jax.ShapeDtypeStruct(kv_shape, F32),
                   jax.ShapeDtypeStruct(kv_shape, F32),
                   jax.ShapeDtypeStruct(kv_shape, BF16),
                   jax.ShapeDtypeStruct(vt_shape, BF16),
                   jax.ShapeDtypeStruct(kv_shape, BF16),
                   jax.ShapeDtypeStruct(vt_shape, BF16),
                   jax.ShapeDtypeStruct((bsz, GATE_ROWS, t), F32)],
        compiler_params=pltpu.CompilerParams(
            dimension_semantics=("arbitrary", "arbitrary"), vmem_limit_bytes=VMEM_LIMIT),
        name="inproj",
    )(x, mod, g1, w_in_p, pos3, inv_full, sgn, qg2, kg2)


def _hgrn_chunk(zq, zf, zi, zg, lb, hg_g, st_ref, h, tri):
    c = HG_CHUNK
    logsig = jnp.minimum(zf, 0.0) - jnp.log1p(jnp.exp(-jnp.abs(zf)))
    a = jnp.log(lb)
    bb = jnp.log1p(-lb) + logsig
    logf = jnp.maximum(a, bb) + jnp.log1p(jnp.exp(-jnp.abs(a - bb)))
    k = (1.0 - lb) * (1.0 / (1.0 + jnp.exp(zf)))
    q = _silu(zq)
    v = zi
    bc = jnp.dot(tri, logf, preferred_element_type=F32, precision=lax.Precision.HIGHEST)

    col = lax.broadcasted_iota(jnp.int32, (HG_SUB, c), 1)
    row = lax.broadcasted_iota(jnp.int32, (HG_SUB, c), 0)
    rows_a = []
    for i in range(c // HG_SUB):
        lo = i * HG_SUB
        b_i = bc[lo:lo + HG_SUB]
        q_i = q[lo:lo + HG_SUB]
        k_i = k[lo:lo + HG_SUB]
        a_i = jnp.zeros((HG_SUB, c), F32)
        for s in range(HG_SUB):
            e = jnp.exp(jnp.minimum(b_i - b_i[s:s + 1], 0.0)) * q_i * k_i[s:s + 1]
            a_i = jnp.where(col == lo + s, jnp.sum(e, axis=-1, keepdims=True), a_i)
        if i > 0:
            r_i = b_i[0:1]
            qh = (q_i * jnp.exp(b_i - r_i)).astype(BF16)
            kh = (k * jnp.exp(jnp.minimum(r_i - bc, 0.0))).astype(BF16)
            a_i = jnp.where(col < lo, _dot_nt(qh, kh), a_i)
        a_i = jnp.where(col <= row + lo, a_i, 0.0)
        rows_a.append(a_i)
    amat = jnp.concatenate(rows_a, axis=0)

    st = st_ref[h]
    o = _dot_nt((q * jnp.exp(bc)).astype(BF16), st.astype(BF16)) + _dot(amat.astype(BF16), v.astype(BF16))
    bl = bc[c - 1:c]
    kdec = (k * jnp.exp(bl - bc)).astype(BF16)
    st_ref[h] = jnp.exp(bl) * st + _dot(v.T.astype(BF16), kdec)

    y = o * lax.rsqrt(jnp.mean(o * o, axis=-1, keepdims=True) + EPS) * hg_g
    return y * _silu(zg)


def _hgrn_kernel(z_ref, lbl_ref, g_ref, o_ref, st_ref, *, l_idx):
    @pl.when(pl.program_id(1) == 0)
    def _():
        st_ref[...] = jnp.zeros_like(st_ref)

    lg = lbl_ref[...]
    ex = jnp.exp(lg - jnp.max(lg, axis=0, keepdims=True))
    sm = ex / jnp.sum(ex, axis=0, keepdims=True)
    lb_all = jnp.sum(sm[:l_idx + 1], axis=0, keepdims=True)

    c = HG_CHUNK
    ri = lax.broadcasted_iota(jnp.int32, (c, c), 0)
    ci = lax.broadcasted_iota(jnp.int32, (c, c), 1)
    tri = (ci <= ri).astype(F32)
    n_chunks = z_ref.shape[1] // c

    def body(ch, carry):
        r0 = pl.multiple_of(ch * c, c)
        for h in range(HG_HEADS):
            sl = slice(h * HG_HD, (h + 1) * HG_HD)
            zq = z_ref[0, pl.ds(r0, c), h * HG_HD:(h + 1) * HG_HD]
            zf = z_ref[0, pl.ds(r0, c), HG_WIDTH + h * HG_HD:HG_WIDTH + (h + 1) * HG_HD]
            zi = z_ref[0, pl.ds(r0, c), 2 * HG_WIDTH + h * HG_HD:2 * HG_WIDTH + (h + 1) * HG_HD]
            zg = z_ref[0, pl.ds(r0, c), 3 * HG_WIDTH + h * HG_HD:3 * HG_WIDTH + (h + 1) * HG_HD]
            y = _hgrn_chunk(zq, zf, zi, zg, lb_all[:, sl], g_ref[...], st_ref, h, tri)
            o_ref[0, pl.ds(r0, c), h * HG_HD:(h + 1) * HG_HD] = y.astype(o_ref.dtype)
        return carry

    lax.fori_loop(0, n_chunks, body, 0)


def _hgrn(zhg, lb_logits, hg_g, l_idx, tb):
    bsz, t, _ = zhg.shape
    return pl.pallas_call(
        functools.partial(_hgrn_kernel, l_idx=l_idx),
        grid=(bsz, t // tb),
        in_specs=[pl.BlockSpec((1, tb, 4 * HG_WIDTH), lambda b, i: (b, i, 0)),
                  pl.BlockSpec(lb_logits.shape, lambda b, i: (0, 0)),
                  pl.BlockSpec((1, HG_HD), lambda b, i: (0, 0))],
        out_specs=pl.BlockSpec((1, tb, HG_WIDTH), lambda b, i: (b, i, 0)),
        out_shape=jax.ShapeDtypeStruct((bsz, t, HG_WIDTH), BF16),
        scratch_shapes=[pltpu.VMEM((HG_HEADS, HG_HD, HG_HD), F32)],
        compiler_params=pltpu.CompilerParams(
            dimension_semantics=("arbitrary", "arbitrary"), vmem_limit_bytes=VMEM_LIMIT),
        name="hgrn",
    )(zhg, lb_logits, hg_g)


def _compress_kernel(xk_ref, xv_ref, pe_ref, w1_ref, w2_ref, ko_ref, vo_ref):
    half = CMP_STRIDE * NSA_HD
    outs = []
    for kv, x_ref in enumerate((xk_ref, xv_ref)):
        x = x_ref[0, 0]
        ha = _dot((x + pe_ref[kv, 0:1, :]).astype(BF16), w1_ref[kv, :half, :])
        hb = _dot((x + pe_ref[kv, 1:2, :]).astype(BF16), w1_ref[kv, half:, :])
        n = x.shape[0]
        pre = ha + pltpu.roll(hb, n - 1, axis=0)
        outs.append(_dot(_silu(pre).astype(BF16), w2_ref[kv]))
    ko_ref[0, 0] = outs[0].astype(ko_ref.dtype)
    vo_ref[0, 0] = outs[1].T.astype(vo_ref.dtype)


def _compress(xk, xv, pe2, w1, w2):
    bsz, g, nseg, width = xk.shape
    x_spec = pl.BlockSpec((1, 1, nseg, width), lambda b, j: (b, j, 0, 0))
    return pl.pallas_call(
        _compress_kernel,
        grid=(bsz, g),
        in_specs=[x_spec, x_spec,
                  pl.BlockSpec(pe2.shape, lambda b, j: (0, 0, 0)),
                  pl.BlockSpec(w1.shape, lambda b, j: (0, 0, 0)),
                  pl.BlockSpec(w2.shape, lambda b, j: (0, 0, 0))],
        out_specs=[pl.BlockSpec((1, 1, nseg, NSA_HD), lambda b, j: (b, j, 0, 0)),
                   pl.BlockSpec((1, 1, NSA_HD, nseg), lambda b, j: (b, j, 0, 0))],
        out_shape=[jax.ShapeDtypeStruct((bsz, g, nseg, NSA_HD), BF16),
                   jax.ShapeDtypeStruct((bsz, g, NSA_HD, nseg), BF16)],
        compiler_params=pltpu.CompilerParams(
            dimension_semantics=("arbitrary", "arbitrary"), vmem_limit_bytes=VMEM_LIMIT),
        name="compress",
    )(xk, xv, pe2, w1, w2)


def _nsa_kernel(qt_ref, kc_ref, vct_ref, ks_ref, vst_ref, kw_ref, vwt_ref, gt_ref, mt_ref,
                o_ref, sel_ref, *, tq, tks):
    r = NSA_GROUP
    g = pl.program_id(1)
    q0 = pl.program_id(2) * tq
    q_t = jnp.concatenate([qt_ref[0, hh] for hh in range(r)], axis=1)

    def tile_heads(a):
        return jnp.concatenate([a] * r, axis=1)

    n_blk = kc_ref.shape[2]
    s = _dot(kc_ref[0, 0], q_t)
    blk_end = lax.broadcasted_iota(jnp.int32, (n_blk, tq), 0) * CMP_STRIDE + (CMP_BLOCK - 1)
    t_row = q0 + lax.broadcasted_iota(jnp.int32, (1, tq), 1)
    cbias = tile_heads(jnp.where(blk_end <= t_row, 0.0, NEG))
    some = tile_heads((t_row >= CMP_BLOCK - 1).astype(F32))
    s = jnp.where(cbias < 0.0, NEG, s)
    e = jnp.exp(s - jnp.max(s, axis=0, keepdims=True))
    p = e * (some / jnp.sum(e, axis=0, keepdims=True))
    o_cmp = _dot(vct_ref[0, 0], p.astype(BF16))

    nb = mt_ref.shape[0]
    psum = p[:, 0:tq]
    for hh in range(1, r):
        psum = psum + p[:, hh * tq:(hh + 1) * tq]
    imp = jnp.dot(mt_ref[...], psum, preferred_element_type=F32, precision=lax.Precision.HIGHEST)
    j = lax.broadcasted_iota(jnp.int32, (nb, tq), 0)
    cur = jnp.right_shift(t_row, SLC_BLOCK.bit_length() - 1)
    forced = (j == 0) | (j == cur) | (j == cur - 1)
    imp = jnp.where(j <= cur, jnp.where(forced, jnp.inf, imp), -1.0)
    rank = jnp.zeros((nb, tq), jnp.int32)
    for i in range(nb):
        row_i = imp[i:i + 1, :]
        ahead = (row_i > imp) | ((row_i == imp) & (j > i))
        rank = rank + ahead.astype(jnp.int32)
    sel_ref[...] = jnp.where(rank < min(SLC_TOPK, nb), 0.0, NEG)

    blocks_per_tile = tks // SLC_BLOCK
    rel = (lax.broadcasted_iota(jnp.int32, (tks, tq), 1)
           - lax.broadcasted_iota(jnp.int32, (tks, tq), 0))

    def slc_body(it, carry):
        m, l, acc = carry
        k0 = pl.multiple_of(it * tks, tks)
        rows = sel_ref[pl.ds(pl.multiple_of(it * blocks_per_tile, blocks_per_tile), blocks_per_tile), :]
        bias = jnp.concatenate([jnp.broadcast_to(rows[jj:jj + 1], (SLC_BLOCK, tq))
                                for jj in range(blocks_per_tile)], axis=0)
        bias = jnp.where(rel + (q0 - k0) >= 0, bias, NEG)
        sc = _dot(ks_ref[0, 0, pl.ds(k0, tks), :], q_t) + tile_heads(bias)
        m_new = jnp.maximum(m, jnp.max(sc, axis=0, keepdims=True))
        alpha = jnp.exp(m - m_new)
        pe = jnp.exp(sc - m_new)
        l = alpha * l + jnp.sum(pe, axis=0, keepdims=True)
        acc = alpha * acc + _dot(vst_ref[0, 0, :, pl.ds(k0, tks)], pe.astype(BF16))
        return m_new, l, acc

    init = (jnp.full((1, r * tq), NEG, F32), jnp.zeros((1, r * tq), F32), jnp.zeros((NSA_HD, r * tq), F32))
    n_it = (q0 + tq + tks - 1) // tks
    _, l_s, acc_s = lax.fori_loop(0, n_it, slc_body, init)
    o_slc = acc_s * (1.0 / l_s)

    span = WIN + tq
    start = pl.multiple_of(jnp.maximum(q0 - WIN, 0), tq)
    dist = (lax.broadcasted_iota(jnp.int32, (span, tq), 1)
            - lax.broadcasted_iota(jnp.int32, (span, tq), 0)) + (q0 - start)
    wbias = jnp.where((dist >= 0) & (dist < WIN), 0.0, NEG)
    sw = _dot(kw_ref[0, 0, pl.ds(start, span), :], q_t) + tile_heads(wbias)
    ew = jnp.exp(sw - jnp.max(sw, axis=0, keepdims=True))
    o_win = _dot(vwt_ref[0, 0, :, pl.ds(start, span)], ew.astype(BF16)) * (1.0 / jnp.sum(ew, axis=0, keepdims=True))

    for hh in range(r):
        cols = slice(hh * tq, (hh + 1) * tq)
        o_h = jnp.zeros((NSA_HD, tq), F32)
        for br, o_b in enumerate((o_cmp, o_slc, o_win)):
            gate = gt_ref[0, pl.ds((g * r + hh) * N_BRANCH + br, 1), :]
            o_h = o_h + gate * o_b[:, cols]
        o_ref[0, :, hh * NSA_HD:(hh + 1) * NSA_HD] = o_h.T.astype(o_ref.dtype)


def _nsa(q_t, kc, vct, ks, vst, kw, vwt, gates_t, mt, tq, tks):
    bsz, _, _, t = q_t.shape
    n_blk = kc.shape[2]
    full = lambda b, g, i: (b, g, 0, 0)
    k_spec = pl.BlockSpec((1, 1, t, NSA_HD), full)
    vt_spec = pl.BlockSpec((1, 1, NSA_HD, t), full)
    return pl.pallas_call(
        functools.partial(_nsa_kernel, tq=tq, tks=tks),
        grid=(bsz, NSA_KV_HEADS, t // tq),
        in_specs=[pl.BlockSpec((1, NSA_GROUP, NSA_HD, tq), lambda b, g, i: (b, g, 0, i)),
                  pl.BlockSpec((1, 1, n_blk, NSA_HD), full),
                  pl.BlockSpec((1, 1, NSA_HD, n_blk), full),
                  k_spec, vt_spec, k_spec, vt_spec,
                  pl.BlockSpec((1, GATE_ROWS, tq), lambda b, g, i: (b, 0, i)),
                  pl.BlockSpec(mt.shape, lambda b, g, i: (0, 0))],
        out_specs=pl.BlockSpec((1, tq, NSA_GROUP * NSA_HD), lambda b, g, i: (b, i, g)),
        out_shape=jax.ShapeDtypeStruct((bsz, t, NSA_WIDTH), BF16),
        scratch_shapes=[pltpu.VMEM((mt.shape[0], tq), F32)],
        compiler_params=pltpu.CompilerParams(
            dimension_semantics=("arbitrary", "arbitrary", "arbitrary"), vmem_limit_bytes=VMEM_LIMIT),
        name="nsa",
    )(q_t, kc, vct, ks, vst, kw, vwt, gates_t, mt)


def _causal_conv(u, prev, cw, cb):
    tm = u.shape[0]
    g = SUBLANES
    r8 = lax.broadcasted_iota(jnp.int32, (g, u.shape[1]), 0)
    wrap1 = jnp.where(r8 == 0, prev[2 * g - 1:2 * g], pltpu.roll(u[tm - g:], 1, axis=0))
    wrap2 = jnp.where(r8 == 0, prev[g - 1:g], pltpu.roll(u[tm - 2 * g:tm - g], 1, axis=0))
    u1 = jnp.concatenate([wrap1, u[:tm - g]], axis=0)
    u2 = jnp.concatenate([wrap2, wrap1, u[:tm - 2 * g]], axis=0)
    return cb + u2 * cw[0:1] + u1 * cw[1:2] + u * cw[2:3]


def _ffn_kernel(x_ref, hg_ref, ns_ref, mod_ref, wo_ref, g2_ref, wu_ref, cw_ref, cb_ref, wd_ref,
                o_ref, carry_ref, buf_ref, h2_ref, u_ref, g_ref, acc_ref, *, fc):
    mix = _dot(hg_ref[0], wo_ref[:HG_WIDTH, :]) + _dot(ns_ref[0], wo_ref[HG_WIDTH:, :])
    x1_nat = x_ref[0] + mod_ref[0, 2:3, :] * mix
    n_lt = D_MODEL // LANES
    tm = x1_nat.shape[0]
    groups = tm // SUBLANES
    pitch = buf_ref.shape[1] // SUBLANES
    for c in range(n_lt):
        for sg in range(SUBLANES):
            buf_ref[c, sg * pitch:sg * pitch + groups] = x1_nat[sg * groups:(sg + 1) * groups,
                                                                c * LANES:(c + 1) * LANES]

    x1 = jnp.concatenate(
        [jnp.concatenate([buf_ref[c, pl.ds(j, SUBLANES, stride=pitch), :] for j in range(groups)], axis=0)
         for c in range(n_lt)], axis=1)
    y = x1 * lax.rsqrt(jnp.mean(x1 * x1, axis=-1, keepdims=True) + EPS) * g2_ref[...]
    h2_ref[...] = (y * (1.0 + mod_ref[0, 4:5, :]) + mod_ref[0, 3:4, :]).astype(h2_ref.dtype)
    acc_ref[...] = jnp.zeros_like(acc_ref)
    first = pl.program_id(1) == 0

    def up(jc, slot):
        for half in range(2):
            off = pl.multiple_of(half * D_FF + jc * fc, fc)
            u_ref[slot, half] = _dot(h2_ref[...], wu_ref[:, pl.ds(off, fc)])

    def act(jc, slot):
        halves = []
        for half in range(2):
            cols = pl.ds(pl.multiple_of(half * D_FF + jc * fc, fc), fc)
            u = u_ref[slot, half]
            prev = jnp.where(first, 0.0, carry_ref[:, cols])
            carry_ref[:, cols] = u[tm - 2 * SUBLANES:]
            halves.append(_causal_conv(u, prev, cw_ref[:, cols], cb_ref[:, cols]))
        a, v = halves
        g_ref[slot] = (_silu(a) * v).astype(g_ref.dtype)

    def down(jc, slot):
        acc_ref[...] += _dot(g_ref[slot], wd_ref[pl.ds(pl.multiple_of(jc * fc, fc), fc), :])

    n = D_FF // fc

    def tick(i):
        if i < n:
            up(i, i % 2)
        if 0 <= i - 1 < n:
            act(i - 1, (i - 1) % 2)
        if 0 <= i - 2 < n:
            down(i - 2, i % 2)

    tick(0)
    tick(1)
    pairs = (n - 2) // 2

    def body(k, carry):
        i = 2 + 2 * k
        up(i, 0)
        act(i - 1, 1)
        down(i - 2, 0)
        up(i + 1, 1)
        act(i, 0)
        down(i - 1, 1)
        return carry

    lax.fori_loop(0, pairs, body, 0)
    for i in range(2 + 2 * pairs, n + 2):
        tick(i)

    out = x1 + mod_ref[0, 5:6, :] * acc_ref[...]
    for c in range(n_lt):
        for j in range(groups):
            buf_ref[c, pl.ds(j, SUBLANES, stride=pitch), :] = out[j * SUBLANES:(j + 1) * SUBLANES,
                                                                  c * LANES:(c + 1) * LANES]
    for c in range(n_lt):
        for sg in range(SUBLANES):
            o_ref[0, sg * groups:(sg + 1) * groups, c * LANES:(c + 1) * LANES] = (
                buf_ref[c, sg * pitch:sg * pitch + groups])


def _ffn(x, o_hg, o_nsa, mod, w_out, g2, w_up, conv_w, conv_b, w_down, tm, fc):
    bsz, t, _ = x.shape
    row_spec = lambda w: pl.BlockSpec((1, tm, w), lambda b, i: (b, i, 0))
    groups = tm // SUBLANES
    pad = SUBLANES if (groups // SUBLANES) % 2 == 0 else 0
    resident = lambda a: pl.BlockSpec(a.shape, lambda b, i: (0, 0), pipeline_mode=pl.Buffered(1))
    return pl.pallas_call(
        functools.partial(_ffn_kernel, fc=fc),
        grid=(bsz, t // tm),
        in_specs=[row_spec(D_MODEL), row_spec(HG_WIDTH), row_spec(NSA_WIDTH),
                  pl.BlockSpec((1, 6, D_MODEL), lambda b, i: (b, 0, 0)),
                  resident(w_out), resident(g2), resident(w_up), resident(conv_w), resident(conv_b),
                  resident(w_down)],
        out_specs=row_spec(D_MODEL),
        out_shape=jax.ShapeDtypeStruct(x.shape, F32),
        scratch_shapes=[pltpu.VMEM((2 * SUBLANES, 2 * D_FF), F32),
                        pltpu.VMEM((D_MODEL // LANES, tm + SUBLANES * pad, LANES), F32),
                        pltpu.VMEM((tm, D_MODEL), BF16),
                        pltpu.VMEM((2, 2, tm, fc), F32),
                        pltpu.VMEM((2, tm, fc), BF16),
                        pltpu.VMEM((tm, D_MODEL), F32)],
        compiler_params=pltpu.CompilerParams(
            dimension_semantics=("arbitrary", "arbitrary"), vmem_limit_bytes=VMEM_LIMIT),
        name="ffn",
    )(x, o_hg, o_nsa, mod, w_out, g2, w_up, conv_w, conv_b, w_down)


def _rope_tables():
    half = ROPE_DIM // 2
    inv = ROPE_THETA ** (-jnp.arange(half, dtype=F32) * 2.0 / ROPE_DIM)
    d = np.arange(LANES) % NSA_HD
    inv_full = jnp.where(jnp.asarray(d < ROPE_DIM), jnp.tile(inv, LANES // half), 0.0)
    sgn = np.where(d < half, -1.0, np.where(d < ROPE_DIM, 1.0, 0.0)).astype(np.float32)
    return inv_full.reshape(1, LANES).astype(F32), jnp.asarray(sgn).reshape(1, LANES)


def _selection_tables(t):
    n_seg = t // CMP_STRIDE
    nb = t // SLC_BLOCK
    cst = np.arange(n_seg) * CMP_STRIDE
    sst = np.arange(nb) * SLC_BLOCK
    ovl = np.clip(np.minimum(cst[:, None] + CMP_BLOCK, sst[None] + SLC_BLOCK)
                  - np.maximum(cst[:, None], sst[None]), 0, None) / CMP_BLOCK
    ovl[(t - CMP_BLOCK) // CMP_STRIDE + 1:] = 0.0
    return (jnp.asarray(ovl.T, dtype=F32),)


def _layer(x, mod, pos3, l, p, tables):
    bsz, t, _ = x.shape
    inv_full, sgn, mt = tables
    w_in_p = jnp.pad(p["w_in"][l], ((0, 0), (0, IN_COLS_PAD - IN_COLS))).astype(BF16)
    qg2 = jnp.tile(p["q_norm_g"][l].reshape(1, NSA_HD), (1, LANES // NSA_HD))
    kg2 = jnp.tile(p["k_norm_g"][l], (1, LANES // NSA_HD))
    zhg, q_t, kc, vc, ks, vst, kw, vwt, gates_t = _inproj(
        x, mod, p["norm1_g"][l].reshape(1, D_MODEL), w_in_p, pos3, inv_full, sgn, qg2, kg2, tm=256)

    o_hg = _hgrn(zhg, p["lb_logits"], p["hg_norm_g"][l].reshape(1, HG_HD), l, tb=512)

    n_seg = t // CMP_STRIDE
    seg_w = CMP_STRIDE * NSA_HD
    pe2 = p["pe_cmp"][l].reshape(2, 2, seg_w)
    kcmp, vcmp_t = _compress(kc.reshape(bsz, NSA_KV_HEADS, n_seg, seg_w),
                             vc.reshape(bsz, NSA_KV_HEADS, n_seg, seg_w),
                             pe2, p["w_cmp1"][l].astype(BF16), p["w_cmp2"][l].astype(BF16))
    o_nsa = _nsa(q_t, kcmp, vcmp_t, ks, vst, kw, vwt, gates_t, mt, tq=128, tks=512)

    return _ffn(x, o_hg, o_nsa, mod, p["w_out"][l].astype(BF16), p["norm2_g"][l].reshape(1, D_MODEL),
                p["w_up"][l].astype(BF16), p["conv_w"][l], p["conv_b"][l].reshape(1, 2 * D_FF),
                p["w_down"][l].astype(BF16), tm=512, fc=256)


def kernel(x, c, positions, w_ada, b_ada, norm1_g, w_in, lb_logits, hg_norm_g, q_norm_g, k_norm_g, pe_cmp, w_cmp1, w_cmp2, w_out, norm2_g, w_up, conv_w, conv_b, w_down):
    p = dict(w_in=w_in, norm1_g=norm1_g, lb_logits=lb_logits, hg_norm_g=hg_norm_g, q_norm_g=q_norm_g,
             k_norm_g=k_norm_g, pe_cmp=pe_cmp, w_cmp1=w_cmp1, w_cmp2=w_cmp2, w_out=w_out,
             norm2_g=norm2_g, w_up=w_up, conv_w=conv_w, conv_b=conv_b, w_down=w_down)
    bsz, t, _ = x.shape
    tables = _rope_tables() + _selection_tables(t)
    pos3 = positions.reshape(bsz, t, 1)
    for l in range(w_ada.shape[0]):
        mod = _ada(c, w_ada[l], b_ada[l]).reshape(bsz, 6, D_MODEL)
        x = _layer(x, mod, pos3, l, p, tables)
    return x
```

```python
import functools

import jax
import jax.numpy as jnp
import numpy as np
from jax import lax
from jax.experimental import pallas as pl
from jax.experimental.pallas import tpu as pltpu

D_MODEL = 1024
HG_HEADS = 4
HG_HD = 128
HG_WIDTH = HG_HEADS * HG_HD
HG_CHUNK = 64
HG_SUB = 8
LOG2E = 1.4426950408889634
NSA_HEADS = 8
NSA_KV_HEADS = 2
NSA_HD = 64
NSA_GROUP = NSA_HEADS // NSA_KV_HEADS
NSA_WIDTH = NSA_HEADS * NSA_HD
N_BRANCH = 3
CMP_BLOCK = 32
CMP_STRIDE = 16
CMP_HIDDEN = 256
SLC_BLOCK = 64
SLC_TOPK = 16
WIN = 512
ROPE_DIM = NSA_HD // 4
ROPE_THETA = 500000.0
D_FF = 2816
CONV_W = 3
EPS = 1e-6
NEG = -1e30

LANES = 128
SUBLANES = 8
VMEM_LIMIT = 56 * 1024 * 1024

OFF_HG = 0
OFF_Q = 4 * HG_WIDTH
OFF_KV = OFF_Q + NSA_WIDTH
OFF_G = OFF_KV + 6 * NSA_KV_HEADS * NSA_HD
IN_COLS = OFF_G + N_BRANCH * NSA_HEADS
IN_COLS_PAD = OFF_G + LANES
GATE_ROWS = 32

BF16 = jnp.bfloat16
F32 = jnp.float32


def _dot(a, b):
    return jnp.dot(a, b, preferred_element_type=F32)


def _dot_nt(a, b):
    return lax.dot_general(a, b, (((1,), (1,)), ((), ())), preferred_element_type=F32)


def _sigmoid(x):
    return 1.0 / (1.0 + jnp.exp(-x))


def _silu(x):
    return x * _sigmoid(x)


def _ada_kernel(c_ref, w_ref, b_ref, o_ref):
    cs = _silu(c_ref[...])
    o_ref[...] = jnp.dot(cs, w_ref[...], preferred_element_type=F32,
                         precision=lax.Precision.HIGHEST) + b_ref[...]


def _ada(c, w, b):
    bsz = c.shape[0]
    n = w.shape[1]
    tn = D_MODEL
    return pl.pallas_call(
        _ada_kernel,
        grid=(n // tn,),
        in_specs=[pl.BlockSpec((bsz, D_MODEL), lambda j: (0, 0)),
                  pl.BlockSpec((D_MODEL, tn), lambda j: (0, j)),
                  pl.BlockSpec((1, tn), lambda j: (0, j))],
        out_specs=pl.BlockSpec((bsz, tn), lambda j: (0, j)),
        out_shape=jax.ShapeDtypeStruct((bsz, n), F32),
        name="ada",
    )(c, w, b.reshape(1, n))


def _pair_norm_rope(xp, g, cosv, sinv, lane):
    left = lane < NSA_HD
    sq = xp * xp
    s_l = jnp.sum(jnp.where(left, sq, 0.0), axis=-1, keepdims=True)
    s_r = jnp.sum(jnp.where(left, 0.0, sq), axis=-1, keepdims=True)
    ms = jnp.where(left, s_l, s_r) * (1.0 / NSA_HD)
    xn = xp * lax.rsqrt(ms + EPS) * g
    half = ROPE_DIM // 2
    first = jnp.bitwise_and(lane, NSA_HD - 1) < half
    partner = jnp.where(first, pltpu.roll(xn, LANES - half, axis=1), pltpu.roll(xn, half, axis=1))
    return xn * cosv + partner * sinv


def _inproj_kernel(x_ref, mod_ref, g1_ref, w_ref, pos_ref, inv_ref, sgn_ref, qg_ref, kg_ref,
                   zhg_ref, qt_ref, kc_ref, vc_ref, ks_ref, vst_ref, kw_ref, vwt_ref, gt_ref):
    x = x_ref[0]
    ms = jnp.mean(x * x, axis=-1, keepdims=True)
    y = x * lax.rsqrt(ms + EPS) * g1_ref[...]
    h = (y * (1.0 + mod_ref[0, 1:2, :]) + mod_ref[0, 0:1, :]).astype(BF16)

    zhg_ref[0] = _dot(h, w_ref[:, OFF_HG:OFF_Q])

    tm = x.shape[0]
    lane = lax.broadcasted_iota(jnp.int32, (tm, LANES), 1)
    ang = pos_ref[0].astype(F32) * inv_ref[...]
    cosv = jnp.cos(ang)
    sinv = jnp.sin(ang) * sgn_ref[...]

    zq = _dot(h, w_ref[:, OFF_Q:OFF_KV])
    scale = NSA_HD ** -0.5
    for p in range(NSA_HEADS // 2):
        r = _pair_norm_rope(zq[:, p * LANES:(p + 1) * LANES], qg_ref[...], cosv, sinv, lane)
        rt = (r * scale).T.astype(qt_ref.dtype)
        qt_ref[0, 2 * p] = rt[:NSA_HD]
        qt_ref[0, 2 * p + 1] = rt[NSA_HD:]

    zkv = _dot(h, w_ref[:, OFF_KV:OFF_G])
    k_outs = (kc_ref, ks_ref, kw_ref)
    for br in range(N_BRANCH):
        kk = _pair_norm_rope(zkv[:, (2 * br) * LANES:(2 * br + 1) * LANES],
                             kg_ref[br:br + 1, :], cosv, sinv, lane).astype(k_outs[br].dtype)
        for g in range(NSA_KV_HEADS):
            k_outs[br][0, g] = kk[:, g * NSA_HD:(g + 1) * NSA_HD]
    vc = zkv[:, LANES:2 * LANES]
    for g in range(NSA_KV_HEADS):
        vc_ref[0, g] = vc[:, g * NSA_HD:(g + 1) * NSA_HD]
    for br, vt_ref in ((1, vst_ref), (2, vwt_ref)):
        vt = zkv[:, (2 * br + 1) * LANES:(2 * br + 2) * LANES].T.astype(vt_ref.dtype)
        for g in range(NSA_KV_HEADS):
            vt_ref[0, g] = vt[g * NSA_HD:(g + 1) * NSA_HD]

    gates = _sigmoid(_dot(h, w_ref[:, OFF_G:IN_COLS_PAD]))
    gt_ref[0] = gates.T[:GATE_ROWS]


def _inproj(x, mod, g1, w_in_p, pos3, inv_full, sgn, qg2, kg2, tm):
    bsz, t, _ = x.shape
    grid = (bsz, t // tm)
    kv_shape = (bsz, NSA_KV_HEADS, t, NSA_HD)
    kv_spec = pl.BlockSpec((1, NSA_KV_HEADS, tm, NSA_HD), lambda b, i: (b, 0, i, 0))
    vt_shape = (bsz, NSA_KV_HEADS, NSA_HD, t)
    vt_spec = pl.BlockSpec((1, NSA_KV_HEADS, NSA_HD, tm), lambda b, i: (b, 0, 0, i))
    const = lambda b, i: (0, 0)
    return pl.pallas_call(
        _inproj_kernel,
        grid=grid,
        in_specs=[pl.BlockSpec((1, tm, D_MODEL), lambda b, i: (b, i, 0)),
                  pl.BlockSpec((1, 6, D_MODEL), lambda b, i: (b, 0, 0)),
                  pl.BlockSpec((1, D_MODEL), const),
                  pl.BlockSpec((D_MODEL, IN_COLS_PAD), const),
                  pl.BlockSpec((1, tm, 1), lambda b, i: (b, i, 0)),
                  pl.BlockSpec((1, LANES), const),
                  pl.BlockSpec((1, LANES), const),
                  pl.BlockSpec((1, LANES), const),
                  pl.BlockSpec((N_BRANCH, LANES), const)],
        out_specs=[pl.BlockSpec((1, tm, 4 * HG_WIDTH), lambda b, i: (b, i, 0)),
                   pl.BlockSpec((1, NSA_HEADS, NSA_HD, tm), lambda b, i: (b, 0, 0, i)),
                   kv_spec, kv_spec, kv_spec, vt_spec, kv_spec, vt_spec,
                   pl.BlockSpec((1, GATE_ROWS, tm), lambda b, i: (b, 0, i))],
        out_shape=[jax.ShapeDtypeStruct((bsz, t, 4 * HG_WIDTH), F32),
                   jax.ShapeDtypeStruct((bsz, NSA_HEADS, NSA_HD, t), BF16),
                   jax.ShapeDtypeStruct(kv_shape, F32),
                   jax.ShapeDtypeStruct(kv_shape, F32),
                   jax.ShapeDtypeStruct(kv_shape, BF16),
                   jax.ShapeDtypeStruct(vt_shape, BF16),
                   jax.ShapeDtypeStruct(kv_shape, BF16),
                   jax.ShapeDtypeStruct(vt_shape, BF16),
                   jax.ShapeDtypeStruct((bsz, GATE_ROWS, t), F32)],
        compiler_params=pltpu.CompilerParams(
            dimension_semantics=("arbitrary", "arbitrary"), vmem_limit_bytes=VMEM_LIMIT),
        name="inproj",
    )(x, mod, g1, w_in_p, pos3, inv_full, sgn, qg2, kg2)


def _hgrn_chunk(zq, zf, zi, zg, lb, hg_g, st_ref, h, tri, level):
    c = HG_CHUNK
    e_z = jnp.exp(-jnp.abs(zf))
    logsig = jnp.minimum(zf, 0.0) - jnp.log(1.0 + e_z)
    a = jnp.log(lb)
    bb = jnp.log1p(-lb) + logsig
    logf = jnp.maximum(a, bb) + jnp.log(1.0 + jnp.exp(-jnp.abs(a - bb)))
    k = (1.0 - lb) * (jnp.where(zf >= 0.0, e_z, 1.0) / (1.0 + e_z))
    q = _silu(zq)
    v = zi
    l_hi = logf.astype(BF16)
    l_mid = (logf - l_hi.astype(F32)).astype(BF16)
    l_lo = (logf - l_hi.astype(F32) - l_mid.astype(F32)).astype(BF16)
    bc = _dot(tri, l_hi) + _dot(tri, l_mid) + _dot(tri, l_lo)

    col = lax.broadcasted_iota(jnp.int32, (HG_SUB, c), 1)
    b2 = bc * LOG2E
    rows_a = []
    for i in range(c // HG_SUB):
        lo = i * HG_SUB
        b_i = b2[lo:lo + HG_SUB]
        q_i = q[lo:lo + HG_SUB]
        k_i = k[lo:lo + HG_SUB]
        a_i = jnp.zeros((HG_SUB, c), F32)
        for s in range(HG_SUB):
            e = jnp.exp2(jnp.minimum(b_i - b_i[s:s + 1], 0.0)) * q_i * k_i[s:s + 1]
            a_i = jnp.where(col == lo + s, jnp.sum(e, axis=-1, keepdims=True), a_i)
        rows_a.append(a_i)
    amat = jnp.concatenate(rows_a, axis=0)

    size, idx = c // 2, 1
    while size >= HG_SUB:
        pieces = []
        for e0 in range(0, c, 2 * size):
            o0 = e0 + size
            r = bc[o0:o0 + 1]
            pieces.append(k[e0:o0] * jnp.exp(r - bc[e0:o0]))
            pieces.append(q[o0:o0 + size] * jnp.exp(bc[o0:o0 + size] - r))
        hmat = jnp.concatenate(pieces, axis=0).astype(BF16)
        amat = jnp.where(level == idx, _dot_nt(hmat, hmat), amat)
        size, idx = size // 2, idx + 1
    amat = jnp.where(level >= 0, amat, 0.0)

    st = st_ref[h]
    o = _dot_nt((q * jnp.exp(bc)).astype(BF16), st.astype(BF16)) + _dot(amat.astype(BF16), v.astype(BF16))
    bl = bc[c - 1:c]
    kdec = (k * jnp.exp(bl - bc)).astype(BF16)
    st_ref[h] = jnp.exp(bl) * st + _dot(v.T.astype(BF16), kdec)

    y = o * lax.rsqrt(jnp.mean(o * o, axis=-1, keepdims=True) + EPS) * hg_g
    return y * _silu(zg)


def _hgrn_kernel(z_ref, lbl_ref, g_ref, o_ref, st_ref, *, l_idx):
    @pl.when(pl.program_id(1) == 0)
    def _():
        st_ref[...] = jnp.zeros_like(st_ref)

    lg = lbl_ref[...]
    ex = jnp.exp(lg - jnp.max(lg, axis=0, keepdims=True))
    sm = ex / jnp.sum(ex, axis=0, keepdims=True)
    lb_all = jnp.sum(sm[:l_idx + 1], axis=0, keepdims=True)

    c = HG_CHUNK
    ri = lax.broadcasted_iota(jnp.int32, (c, c), 0)
    ci = lax.broadcasted_iota(jnp.int32, (c, c), 1)
    tri = jnp.where(ci <= ri, 1.0, 0.0).astype(BF16)
    level = jnp.where(ci > ri, -1, 0)
    size, idx = c // 2, 1
    while size >= HG_SUB:
        sh = size.bit_length() - 1
        paired = ((ri >> (sh + 1)) == (ci >> (sh + 1))) & ((ri >> sh) != (ci >> sh)) & (ci <= ri)
        level = jnp.where(paired, idx, level)
        size, idx = size // 2, idx + 1
    n_chunks = z_ref.shape[1] // c

    def body(ch, carry):
        r0 = pl.multiple_of(ch * c, c)
        for h in range(HG_HEADS):
            sl = slice(h * HG_HD, (h + 1) * HG_HD)
            zq = z_ref[0, pl.ds(r0, c), h * HG_HD:(h + 1) * HG_HD]
            zf = z_ref[0, pl.ds(r0, c), HG_WIDTH + h * HG_HD:HG_WIDTH + (h + 1) * HG_HD]
            zi = z_ref[0, pl.ds(r0, c), 2 * HG_WIDTH + h * HG_HD:2 * HG_WIDTH + (h + 1) * HG_HD]
            zg = z_ref[0, pl.ds(r0, c), 3 * HG_WIDTH + h * HG_HD:3 * HG_WIDTH + (h + 1) * HG_HD]
            y = _hgrn_chunk(zq, zf, zi, zg, lb_all[:, sl], g_ref[...], st_ref, h, tri, level)
            o_ref[0, pl.ds(r0, c), h * HG_HD:(h + 1) * HG_HD] = y.astype(o_ref.dtype)
        return carry

    lax.fori_loop(0, n_chunks, body, 0, unroll=2)


def _hgrn(zhg, lb_logits, hg_g, l_idx, tb):
    bsz, t, _ = zhg.shape
    return pl.pallas_call(
        functools.partial(_hgrn_kernel, l_idx=l_idx),
        grid=(bsz, t // tb),
        in_specs=[pl.BlockSpec((1, tb, 4 * HG_WIDTH), lambda b, i: (b, i, 0)),
                  pl.BlockSpec(lb_logits.shape, lambda b, i: (0, 0)),
                  pl.BlockSpec((1, HG_HD), lambda b, i: (0, 0))],
        out_specs=pl.BlockSpec((1, tb, HG_WIDTH), lambda b, i: (b, i, 0)),
        out_shape=jax.ShapeDtypeStruct((bsz, t, HG_WIDTH), BF16),
        scratch_shapes=[pltpu.VMEM((HG_HEADS, HG_HD, HG_HD), F32)],
        compiler_params=pltpu.CompilerParams(
            dimension_semantics=("arbitrary", "arbitrary"), vmem_limit_bytes=VMEM_LIMIT),
        name="hgrn",
    )(zhg, lb_logits, hg_g)


def _compress_kernel(xk_ref, xv_ref, pe_ref, w1_ref, w2_ref, ko_ref, vo_ref):
    half = CMP_STRIDE * NSA_HD
    outs = []
    for kv, x_ref in enumerate((xk_ref, xv_ref)):
        x = x_ref[0, 0]
        ha = _dot((x + pe_ref[kv, 0:1, :]).astype(BF16), w1_ref[kv, :half, :])
        hb = _dot((x + pe_ref[kv, 1:2, :]).astype(BF16), w1_ref[kv, half:, :])
        n = x.shape[0]
        pre = ha + pltpu.roll(hb, n - 1, axis=0)
        outs.append(_dot(_silu(pre).astype(BF16), w2_ref[kv]))
    ko_ref[0, 0] = outs[0].astype(ko_ref.dtype)
    vo_ref[0, 0] = outs[1].T.astype(vo_ref.dtype)


def _compress(xk, xv, pe2, w1, w2):
    bsz, g, nseg, width = xk.shape
    x_spec = pl.BlockSpec((1, 1, nseg, width), lambda b, j: (b, j, 0, 0))
    return pl.pallas_call(
        _compress_kernel,
        grid=(bsz, g),
        in_specs=[x_spec, x_spec,
                  pl.BlockSpec(pe2.shape, lambda b, j: (0, 0, 0)),
                  pl.BlockSpec(w1.shape, lambda b, j: (0, 0, 0)),
                  pl.BlockSpec(w2.shape, lambda b, j: (0, 0, 0))],
        out_specs=[pl.BlockSpec((1, 1, nseg, NSA_HD), lambda b, j: (b, j, 0, 0)),
                   pl.BlockSpec((1, 1, NSA_HD, nseg), lambda b, j: (b, j, 0, 0))],
        out_shape=[jax.ShapeDtypeStruct((bsz, g, nseg, NSA_HD), BF16),
                   jax.ShapeDtypeStruct((bsz, g, NSA_HD, nseg), BF16)],
        compiler_params=pltpu.CompilerParams(
            dimension_semantics=("arbitrary", "arbitrary"), vmem_limit_bytes=VMEM_LIMIT),
        name="compress",
    )(xk, xv, pe2, w1, w2)


def _nsa_kernel(qt_ref, kc_ref, vct_ref, ks_ref, vst_ref, kw_ref, vwt_ref, gt_ref, mt_ref,
                o_ref, sel_ref, *, tq, tks):
    r = NSA_GROUP
    g = pl.program_id(1)
    q0 = pl.program_id(2) * tq
    q_t = jnp.concatenate([qt_ref[0, hh] for hh in range(r)], axis=1)

    def tile_heads(a):
        return jnp.concatenate([a] * r, axis=1)

    n_blk = kc_ref.shape[2]
    s = _dot(kc_ref[0, 0], q_t)
    blk_end = lax.broadcasted_iota(jnp.int32, (n_blk, tq), 0) * CMP_STRIDE + (CMP_BLOCK - 1)
    t_row = q0 + lax.broadcasted_iota(jnp.int32, (1, tq), 1)
    cbias = tile_heads(jnp.where(blk_end <= t_row, 0.0, NEG))
    some = tile_heads((t_row >= CMP_BLOCK - 1).astype(F32))
    s = jnp.where(cbias < 0.0, NEG, s)
    e = jnp.exp(s - jnp.max(s, axis=0, keepdims=True))
    p = e * (some / jnp.sum(e, axis=0, keepdims=True))
    o_cmp = _dot(vct_ref[0, 0], p.astype(BF16))

    nb = mt_ref.shape[0]
    psum = p[:, 0:tq]
    for hh in range(1, r):
        psum = psum + p[:, hh * tq:(hh + 1) * tq]
    imp = jnp.dot(mt_ref[...], psum, preferred_element_type=F32, precision=lax.Precision.HIGHEST)
    j = lax.broadcasted_iota(jnp.int32, (nb, tq), 0)
    cur = jnp.right_shift(t_row, SLC_BLOCK.bit_length() - 1)
    forced = (j == 0) | (j == cur) | (j == cur - 1)
    imp = jnp.where(j <= cur, jnp.where(forced, jnp.inf, imp), -1.0)
    rank = jnp.zeros((nb, tq), jnp.int32)
    for i in range(nb):
        row_i = imp[i:i + 1, :]
        ahead = (row_i > imp) | ((row_i == imp) & (j > i))
        rank = rank + ahead.astype(jnp.int32)
    sel_ref[...] = jnp.where(rank < min(SLC_TOPK, nb), 0.0, NEG)

    blocks_per_tile = tks // SLC_BLOCK
    rel = (lax.broadcasted_iota(jnp.int32, (tks, tq), 1)
           - lax.broadcasted_iota(jnp.int32, (tks, tq), 0))

    def slc_body(it, carry):
        m, l, acc = carry
        k0 = pl.multiple_of(it * tks, tks)
        rows = sel_ref[pl.ds(pl.multiple_of(it * blocks_per_tile, blocks_per_tile), blocks_per_tile), :]
        bias = jnp.concatenate([jnp.broadcast_to(rows[jj:jj + 1], (SLC_BLOCK, tq))
                                for jj in range(blocks_per_tile)], axis=0)
        bias = jnp.where(rel + (q0 - k0) >= 0, bias, NEG)
        sc = _dot(ks_ref[0, 0, pl.ds(k0, tks), :], q_t) + tile_heads(bias)
        m_new = jnp.maximum(m, jnp.max(sc, axis=0, keepdims=True))
        alpha = jnp.exp(m - m_new)
        pe = jnp.exp(sc - m_new)
        l = alpha * l + jnp.sum(pe, axis=0, keepdims=True)
        acc = alpha * acc + _dot(vst_ref[0, 0, :, pl.ds(k0, tks)], pe.astype(BF16))
        return m_new, l, acc

    init = (jnp.full((1, r * tq), NEG, F32), jnp.zeros((1, r * tq), F32), jnp.zeros((NSA_HD, r * tq), F32))
    n_it = (q0 + tq + tks - 1) // tks
    _, l_s, acc_s = lax.fori_loop(0, n_it, slc_body, init)
    o_slc = acc_s * (1.0 / l_s)

    span = WIN + tq
    start = pl.multiple_of(jnp.maximum(q0 - WIN, 0), tq)
    dist = (lax.broadcasted_iota(jnp.int32, (span, tq), 1)
            - lax.broadcasted_iota(jnp.int32, (span, tq), 0)) + (q0 - start)
    wbias = jnp.where((dist >= 0) & (dist < WIN), 0.0, NEG)
    sw = _dot(kw_ref[0, 0, pl.ds(start, span), :], q_t) + tile_heads(wbias)
    ew = jnp.exp(sw - jnp.max(sw, axis=0, keepdims=True))
    o_win = _dot(vwt_ref[0, 0, :, pl.ds(start, span)], ew.astype(BF16)) * (1.0 / jnp.sum(ew, axis=0, keepdims=True))

    for hh in range(r):
        cols = slice(hh * tq, (hh + 1) * tq)
        o_h = jnp.zeros((NSA_HD, tq), F32)
        for br, o_b in enumerate((o_cmp, o_slc, o_win)):
            gate = gt_ref[0, pl.ds((g * r + hh) * N_BRANCH + br, 1), :]
            o_h = o_h + gate * o_b[:, cols]
        o_ref[0, :, hh * NSA_HD:(hh + 1) * NSA_HD] = o_h.T.astype(o_ref.dtype)


def _nsa(q_t, kc, vct, ks, vst, kw, vwt, gates_t, mt, tq, tks):
    bsz, _, _, t = q_t.shape
    n_blk = kc.shape[2]
    full = lambda b, g, i: (b, g, 0, 0)
    k_spec = pl.BlockSpec((1, 1, t, NSA_HD), full)
    vt_spec = pl.BlockSpec((1, 1, NSA_HD, t), full)
    return pl.pallas_call(
        functools.partial(_nsa_kernel, tq=tq, tks=tks),
        grid=(bsz, NSA_KV_HEADS, t // tq),
        in_specs=[pl.BlockSpec((1, NSA_GROUP, NSA_HD, tq), lambda b, g, i: (b, g, 0, i)),
                  pl.BlockSpec((1, 1, n_blk, NSA_HD), full),
                  pl.BlockSpec((1, 1, NSA_HD, n_blk), full),
                  k_spec, vt_spec, k_spec, vt_spec,
                  pl.BlockSpec((1, GATE_ROWS, tq), lambda b, g, i: (b, 0, i)),
                  pl.BlockSpec(mt.shape, lambda b, g, i: (0, 0))],
        out_specs=pl.BlockSpec((1, tq, NSA_GROUP * NSA_HD), lambda b, g, i: (b, i, g)),
        out_shape=jax.ShapeDtypeStruct((bsz, t, NSA_WIDTH), BF16),
        scratch_shapes=[pltpu.VMEM((mt.shape[0], tq), F32)],
        compiler_params=pltpu.CompilerParams(
            dimension_semantics=("arbitrary", "arbitrary", "arbitrary"), vmem_limit_bytes=VMEM_LIMIT),
        name="nsa",
    )(q_t, kc, vct, ks, vst, kw, vwt, gates_t, mt)


def _causal_conv(u, prev, cw, cb):
    tm = u.shape[0]
    g = SUBLANES
    r8 = lax.broadcasted_iota(jnp.int32, (g, u.shape[1]), 0)
    wrap1 = jnp.where(r8 == 0, prev[2 * g - 1:2 * g], pltpu.roll(u[tm - g:], 1, axis=0))
    wrap2 = jnp.where(r8 == 0, prev[g - 1:g], pltpu.roll(u[tm - 2 * g:tm - g], 1, axis=0))
    u1 = jnp.concatenate([wrap1, u[:tm - g]], axis=0)
    u2 = jnp.concatenate([wrap2, wrap1, u[:tm - 2 * g]], axis=0)
    return cb + u2 * cw[0:1] + u1 * cw[1:2] + u * cw[2:3]


def _ffn_kernel(x_ref, hg_ref, ns_ref, mod_ref, wo_ref, g2_ref, wu_ref, cw_ref, cb_ref, wd_ref,
                o_ref, carry_ref, buf_ref, h2_ref, u_ref, g_ref, acc_ref, *, fc):
    mix = _dot(hg_ref[0], wo_ref[:HG_WIDTH, :]) + _dot(ns_ref[0], wo_ref[HG_WIDTH:, :])
    x1_nat = x_ref[0] + mod_ref[0, 2:3, :] * mix
    n_lt = D_MODEL // LANES
    tm = x1_nat.shape[0]
    groups = tm // SUBLANES
    pitch = buf_ref.shape[1] // SUBLANES
    for c in range(n_lt):
        for sg in range(SUBLANES):
            buf_ref[c, sg * pitch:sg * pitch + groups] = x1_nat[sg * groups:(sg + 1) * groups,
                                                                c * LANES:(c + 1) * LANES]

    x1 = jnp.concatenate(
        [jnp.concatenate([buf_ref[c, pl.ds(j, SUBLANES, stride=pitch), :] for j in range(groups)], axis=0)
         for c in range(n_lt)], axis=1)
    y = x1 * lax.rsqrt(jnp.mean(x1 * x1, axis=-1, keepdims=True) + EPS) * g2_ref[...]
    h2_ref[...] = (y * (1.0 + mod_ref[0, 4:5, :]) + mod_ref[0, 3:4, :]).astype(h2_ref.dtype)
    acc_ref[...] = jnp.zeros_like(acc_ref)
    first = pl.program_id(1) == 0

    def up(jc, slot):
        for half in range(2):
            off = pl.multiple_of(half * D_FF + jc * fc, fc)
            u_ref[slot, half] = _dot(h2_ref[...], wu_ref[:, pl.ds(off, fc)])

    def act(jc, slot):
        halves = []
        for half in range(2):
            cols = pl.ds(pl.multiple_of(half * D_FF + jc * fc, fc), fc)
            u = u_ref[slot, half]
            prev = jnp.where(first, 0.0, carry_ref[:, cols])
            carry_ref[:, cols] = u[tm - 2 * SUBLANES:]
            halves.append(_causal_conv(u, prev, cw_ref[:, cols], cb_ref[:, cols]))
        a, v = halves
        g_ref[slot] = (_silu(a) * v).astype(g_ref.dtype)

    def down(jc, slot):
        acc_ref[...] += _dot(g_ref[slot], wd_ref[pl.ds(pl.multiple_of(jc * fc, fc), fc), :])

    n = D_FF // fc

    def tick(i):
        if i < n:
            up(i, i % 2)
        if 0 <= i - 1 < n:
            act(i - 1, (i - 1) % 2)
        if 0 <= i - 2 < n:
            down(i - 2, i % 2)

    tick(0)
    tick(1)
    pairs = (n - 2) // 2

    def body(k, carry):
        i = 2 + 2 * k
        up(i, 0)
        act(i - 1, 1)
        down(i - 2, 0)
        up(i + 1, 1)
        act(i, 0)
        down(i - 1, 1)
        return carry

    lax.fori_loop(0, pairs, body, 0)
    for i in range(2 + 2 * pairs, n + 2):
        tick(i)

    out = x1 + mod_ref[0, 5:6, :] * acc_ref[...]
    for c in range(n_lt):
        for j in range(groups):
            buf_ref[c, pl.ds(j, SUBLANES, stride=pitch), :] = out[j * SUBLANES:(j + 1) * SUBLANES,
                                                                  c * LANES:(c + 1) * LANES]
    for c in range(n_lt):
        for sg in range(SUBLANES):
            o_ref[0, sg * groups:(sg + 1) * groups, c * LANES:(c + 1) * LANES] = (
                buf_ref[c, sg * pitch:sg * pitch + groups])


def _ffn(x, o_hg, o_nsa, mod, w_out, g2, w_up, conv_w, conv_b, w_down, tm, fc):
    bsz, t, _ = x.shape
    row_spec = lambda w: pl.BlockSpec((1, tm, w), lambda b, i: (b, i, 0))
    groups = tm // SUBLANES
    pad = SUBLANES if (groups // SUBLANES) % 2 == 0 else 0
    resident = lambda a: pl.BlockSpec(a.shape, lambda b, i: (0, 0), pipeline_mode=pl.Buffered(1))
    return pl.pallas_call(
        functools.partial(_ffn_kernel, fc=fc),
        grid=(bsz, t // tm),
        in_specs=[row_spec(D_MODEL), row_spec(HG_WIDTH), row_spec(NSA_WIDTH),
                  pl.BlockSpec((1, 6, D_MODEL), lambda b, i: (b, 0, 0)),
                  resident(w_out), resident(g2), resident(w_up), resident(conv_w), resident(conv_b),
                  resident(w_down)],
        out_specs=row_spec(D_MODEL),
        out_shape=jax.ShapeDtypeStruct(x.shape, F32),
        scratch_shapes=[pltpu.VMEM((2 * SUBLANES, 2 * D_FF), F32),
                        pltpu.VMEM((D_MODEL // LANES, tm + SUBLANES * pad, LANES), F32),
                        pltpu.VMEM((tm, D_MODEL), BF16),
                        pltpu.VMEM((2, 2, tm, fc), F32),
                        pltpu.VMEM((2, tm, fc), BF16),
                        pltpu.VMEM((tm, D_MODEL), F32)],
        compiler_params=pltpu.CompilerParams(
            dimension_semantics=("arbitrary", "arbitrary"), vmem_limit_bytes=VMEM_LIMIT),
        name="ffn",
    )(x, o_hg, o_nsa, mod, w_out, g2, w_up, conv_w, conv_b, w_down)


def _rope_tables():
    half = ROPE_DIM // 2
    inv = ROPE_THETA ** (-jnp.arange(half, dtype=F32) * 2.0 / ROPE_DIM)
    d = np.arange(LANES) % NSA_HD
    inv_full = jnp.where(jnp.asarray(d < ROPE_DIM), jnp.tile(inv, LANES // half), 0.0)
    sgn = np.where(d < half, -1.0, np.where(d < ROPE_DIM, 1.0, 0.0)).astype(np.float32)
    return inv_full.reshape(1, LANES).astype(F32), jnp.asarray(sgn).reshape(1, LANES)


def _selection_tables(t):
    n_seg = t // CMP_STRIDE
    nb = t // SLC_BLOCK
    cst = np.arange(n_seg) * CMP_STRIDE
    sst = np.arange(nb) * SLC_BLOCK
    ovl = np.clip(np.minimum(cst[:, None] + CMP_BLOCK, sst[None] + SLC_BLOCK)
                  - np.maximum(cst[:, None], sst[None]), 0, None) / CMP_BLOCK
    ovl[(t - CMP_BLOCK) // CMP_STRIDE + 1:] = 0.0
    return (jnp.asarray(ovl.T, dtype=F32),)


def _layer(x, mod, pos3, l, p, tables):
    bsz, t, _ = x.shape
    inv_full, sgn, mt = tables
    w_in_p = jnp.pad(p["w_in"][l], ((0, 0), (0, IN_COLS_PAD - IN_COLS))).astype(BF16)
    qg2 = jnp.tile(p["q_norm_g"][l].reshape(1, NSA_HD), (1, LANES // NSA_HD))
    kg2 = jnp.tile(p["k_norm_g"][l], (1, LANES // NSA_HD))
    zhg, q_t, kc, vc, ks, vst, kw, vwt, gates_t = _inproj(
        x, mod, p["norm1_g"][l].reshape(1, D_MODEL), w_in_p, pos3, inv_full, sgn, qg2, kg2, tm=256)

    o_hg = _hgrn(zhg, p["lb_logits"], p["hg_norm_g"][l].reshape(1, HG_HD), l, tb=512)

    n_seg = t // CMP_STRIDE
    seg_w = CMP_STRIDE * NSA_HD
    pe2 = p["pe_cmp"][l].reshape(2, 2, seg_w)
    kcmp, vcmp_t = _compress(kc.reshape(bsz, NSA_KV_HEADS, n_seg, seg_w),
                             vc.reshape(bsz, NSA_KV_HEADS, n_seg, seg_w),
                             pe2, p["w_cmp1"][l].astype(BF16), p["w_cmp2"][l].astype(BF16))
    o_nsa = _nsa(q_t, kcmp, vcmp_t, ks, vst, kw, vwt, gates_t, mt, tq=128, tks=512)

    return _ffn(x, o_hg, o_nsa, mod, p["w_out"][l].astype(BF16), p["norm2_g"][l].reshape(1, D_MODEL),
                p["w_up"][l].astype(BF16), p["conv_w"][l], p["conv_b"][l].reshape(1, 2 * D_FF),
                p["w_down"][l].astype(BF16), tm=512, fc=256)


def kernel(x, c, positions, w_ada, b_ada, norm1_g, w_in, lb_logits, hg_norm_g, q_norm_g, k_norm_g, pe_cmp, w_cmp1, w_cmp2, w_out, norm2_g, w_up, conv_w, conv_b, w_down):
    p = dict(w_in=w_in, norm1_g=norm1_g, lb_logits=lb_logits, hg_norm_g=hg_norm_g, q_norm_g=q_norm_g,
             k_norm_g=k_norm_g, pe_cmp=pe_cmp, w_cmp1=w_cmp1, w_cmp2=w_cmp2, w_out=w_out,
             norm2_g=norm2_g, w_up=w_up, conv_w=conv_w, conv_b=conv_b, w_down=w_down)
    bsz, t, _ = x.shape
    tables = _rope_tables() + _selection_tables(t)
    pos3 = positions.reshape(bsz, t, 1)
    for l in range(w_ada.shape[0]):
        mod = _ada(c, w_ada[l], b_ada[l]).reshape(bsz, 6, D_MODEL)
        x = _layer(x, mod, pos3, l, p, tables)
    return x
```

```python
import functools

import jax
import jax.numpy as jnp
import numpy as np
from jax import lax
from jax.experimental import pallas as pl
from jax.experimental.pallas import tpu as pltpu

D_MODEL = 1024
HG_HEADS = 4
HG_HD = 128
HG_WIDTH = HG_HEADS * HG_HD
HG_CHUNK = 64
HG_SUB = 8
LOG2E = 1.4426950408889634
NSA_HEADS = 8
NSA_KV_HEADS = 2
NSA_HD = 64
NSA_GROUP = NSA_HEADS // NSA_KV_HEADS
NSA_WIDTH = NSA_HEADS * NSA_HD
N_BRANCH = 3
CMP_BLOCK = 32
CMP_STRIDE = 16
CMP_HIDDEN = 256
SLC_BLOCK = 64
SLC_TOPK = 16
WIN = 512
ROPE_DIM = NSA_HD // 4
ROPE_THETA = 500000.0
D_FF = 2816
CONV_W = 3
EPS = 1e-6
NEG = -1e30

LANES = 128
SUBLANES = 8
VMEM_LIMIT = 56 * 1024 * 1024

OFF_HG = 0
OFF_Q = 4 * HG_WIDTH
OFF_KV = OFF_Q + NSA_WIDTH
OFF_G = OFF_KV + 6 * NSA_KV_HEADS * NSA_HD
IN_COLS = OFF_G + N_BRANCH * NSA_HEADS
IN_COLS_PAD = OFF_G + LANES
GATE_ROWS = 32

BF16 = jnp.bfloat16
F32 = jnp.float32


def _dot(a, b):
    return jnp.dot(a, b, preferred_element_type=F32)


def _dot_nt(a, b):
    return lax.dot_general(a, b, (((1,), (1,)), ((), ())), preferred_element_type=F32)


def _sigmoid(x):
    return 1.0 / (1.0 + jnp.exp(-x))


def _silu(x):
    return x * _sigmoid(x)


def _ada_kernel(c_ref, w_ref, b_ref, o_ref):
    cs = _silu(c_ref[...])
    o_ref[...] = jnp.dot(cs, w_ref[...], preferred_element_type=F32,
                         precision=lax.Precision.HIGHEST) + b_ref[...]


def _ada(c, w, b):
    bsz = c.shape[0]
    n = w.shape[1]
    tn = D_MODEL
    return pl.pallas_call(
        _ada_kernel,
        grid=(n // tn,),
        in_specs=[pl.BlockSpec((bsz, D_MODEL), lambda j: (0, 0)),
                  pl.BlockSpec((D_MODEL, tn), lambda j: (0, j)),
                  pl.BlockSpec((1, tn), lambda j: (0, j))],
        out_specs=pl.BlockSpec((bsz, tn), lambda j: (0, j)),
        out_shape=jax.ShapeDtypeStruct((bsz, n), F32),
        name="ada",
    )(c, w, b.reshape(1, n))


def _pair_norm_rope(xp, g, cosv, sinv, lane):
    left = lane < NSA_HD
    sq = xp * xp
    s_l = jnp.sum(jnp.where(left, sq, 0.0), axis=-1, keepdims=True)
    s_r = jnp.sum(jnp.where(left, 0.0, sq), axis=-1, keepdims=True)
    ms = jnp.where(left, s_l, s_r) * (1.0 / NSA_HD)
    xn = xp * lax.rsqrt(ms + EPS) * g
    half = ROPE_DIM // 2
    first = jnp.bitwise_and(lane, NSA_HD - 1) < half
    partner = jnp.where(first, pltpu.roll(xn, LANES - half, axis=1), pltpu.roll(xn, half, axis=1))
    return xn * cosv + partner * sinv


def _inproj_kernel(x_ref, mod_ref, g1_ref, w_ref, pos_ref, inv_ref, sgn_ref, qg_ref, kg_ref,
                   zhg_ref, qt_ref, kc_ref, vc_ref, ks_ref, vst_ref, kw_ref, vwt_ref, gt_ref):
    x = x_ref[0]
    ms = jnp.mean(x * x, axis=-1, keepdims=True)
    y = x * lax.rsqrt(ms + EPS) * g1_ref[...]
    h = (y * (1.0 + mod_ref[0, 1:2, :]) + mod_ref[0, 0:1, :]).astype(BF16)

    zhg_ref[0] = _dot(h, w_ref[:, OFF_HG:OFF_Q])

    tm = x.shape[0]
    lane = lax.broadcasted_iota(jnp.int32, (tm, LANES), 1)
    ang = pos_ref[0].astype(F32) * inv_ref[...]
    cosv = jnp.cos(ang)
    sinv = jnp.sin(ang) * sgn_ref[...]

    zq = _dot(h, w_ref[:, OFF_Q:OFF_KV])
    scale = NSA_HD ** -0.5
    for p in range(NSA_HEADS // 2):
        r = _pair_norm_rope(zq[:, p * LANES:(p + 1) * LANES], qg_ref[...], cosv, sinv, lane)
        rt = (r * scale).T.astype(qt_ref.dtype)
        qt_ref[0, 2 * p] = rt[:NSA_HD]
        qt_ref[0, 2 * p + 1] = rt[NSA_HD:]

    zkv = _dot(h, w_ref[:, OFF_KV:OFF_G])
    k_outs = (kc_ref, ks_ref, kw_ref)
    for br in range(N_BRANCH):
        kk = _pair_norm_rope(zkv[:, (2 * br) * LANES:(2 * br + 1) * LANES],
                             kg_ref[br:br + 1, :], cosv, sinv, lane).astype(k_outs[br].dtype)
        for g in range(NSA_KV_HEADS):
            k_outs[br][0, g] = kk[:, g * NSA_HD:(g + 1) * NSA_HD]
    vc = zkv[:, LANES:2 * LANES]
    for g in range(NSA_KV_HEADS):
        vc_ref[0, g] = vc[:, g * NSA_HD:(g + 1) * NSA_HD]
    for br, vt_ref in ((1, vst_ref), (2, vwt_ref)):
        vt = zkv[:, (2 * br + 1) * LANES:(2 * br + 2) * LANES].T.astype(vt_ref.dtype)
        for g in range(NSA_KV_HEADS):
            vt_ref[0, g] = vt[g * NSA_HD:(g + 1) * NSA_HD]

    gates = _sigmoid(_dot(h, w_ref[:, OFF_G:IN_COLS_PAD]))
    gt_ref[0] = gates.T[:GATE_ROWS]


def _inproj(x, mod, g1, w_in_p, pos3, inv_full, sgn, qg2, kg2, tm):
    bsz, t, _ = x.shape
    grid = (bsz, t // tm)
    kv_shape = (bsz, NSA_KV_HEADS, t, NSA_HD)
    kv_spec = pl.BlockSpec((1, NSA_KV_HEADS, tm, NSA_HD), lambda b, i: (b, 0, i, 0))
    vt_shape = (bsz, NSA_KV_HEADS, NSA_HD, t)
    vt_spec = pl.BlockSpec((1, NSA_KV_HEADS, NSA_HD, tm), lambda b, i: (b, 0, 0, i))
    const = lambda b, i: (0, 0)
    return pl.pallas_call(
        _inproj_kernel,
        grid=grid,
        in_specs=[pl.BlockSpec((1, tm, D_MODEL), lambda b, i: (b, i, 0)),
                  pl.BlockSpec((1, 6, D_MODEL), lambda b, i: (b, 0, 0)),
                  pl.BlockSpec((1, D_MODEL), const),
                  pl.BlockSpec((D_MODEL, IN_COLS_PAD), const),
                  pl.BlockSpec((1, tm, 1), lambda b, i: (b, i, 0)),
                  pl.BlockSpec((1, LANES), const),
                  pl.BlockSpec((1, LANES), const),
                  pl.BlockSpec((1, LANES), const),
                  pl.BlockSpec((N_BRANCH, LANES), const)],
        out_specs=[pl.BlockSpec((1, tm, 4 * HG_WIDTH), lambda b, i: (b, i, 0)),
                   pl.BlockSpec((1, NSA_HEADS, NSA_HD, tm), lambda b, i: (b, 0, 0, i)),
                   kv_spec, kv_spec, kv_spec, vt_spec, kv_spec, vt_spec,
                   pl.BlockSpec((1, GATE_ROWS, tm), lambda b, i: (b, 0, i))],
        out_shape=[jax.ShapeDtypeStruct((bsz, t, 4 * HG_WIDTH), F32),
                   jax.ShapeDtypeStruct((bsz, NSA_HEADS, NSA_HD, t), BF16),
                   jax.ShapeDtypeStruct(kv_shape, F32),
                   jax.ShapeDtypeStruct(kv_shape, F32),
                   jax.ShapeDtypeStruct(kv_shape, BF16),
                   jax.ShapeDtypeStruct(vt_shape, BF16),
                   jax.ShapeDtypeStruct(kv_shape, BF16),
                   jax.ShapeDtypeStruct(vt_shape, BF16),
                   jax.ShapeDtypeStruct((bsz, GATE_ROWS, t), F32)],
        compiler_params=pltpu.CompilerParams(
            dimension_semantics=("arbitrary", "arbitrary"), vmem_limit_bytes=VMEM_LIMIT),
        name="inproj",
    )(x, mod, g1, w_in_p, pos3, inv_full, sgn, qg2, kg2)


def _hgrn_chunk(zq, zf, zi, zg, lb, hg_g, st_ref, h, tri, level):
    c = HG_CHUNK
    e_z = jnp.exp(-jnp.abs(zf))
    logsig = jnp.minimum(zf, 0.0) - jnp.log(1.0 + e_z)
    a = jnp.log(lb)
    bb = jnp.log1p(-lb) + logsig
    logf = jnp.maximum(a, bb) + jnp.log(1.0 + jnp.exp(-jnp.abs(a - bb)))
    k = (1.0 - lb) * (jnp.where(zf >= 0.0, e_z, 1.0) / (1.0 + e_z))
    q = _silu(zq)
    v = zi
    l_hi = logf.astype(BF16)
    l_mid = (logf - l_hi.astype(F32)).astype(BF16)
    l_lo = (logf - l_hi.astype(F32) - l_mid.astype(F32)).astype(BF16)
    bc = _dot(tri, l_hi) + _dot(tri, l_mid) + _dot(tri, l_lo)

    col = lax.broadcasted_iota(jnp.int32, (HG_SUB, c), 1)
    b2 = bc * LOG2E
    rows_a = []
    for i in range(c // HG_SUB):
        lo = i * HG_SUB
        b_i = b2[lo:lo + HG_SUB]
        q_i = q[lo:lo + HG_SUB]
        k_i = k[lo:lo + HG_SUB]
        a_i = jnp.zeros((HG_SUB, c), F32)
        for s in range(HG_SUB):
            e = jnp.exp2(jnp.minimum(b_i - b_i[s:s + 1], 0.0)) * q_i * k_i[s:s + 1]
            a_i = jnp.where(col == lo + s, jnp.sum(e, axis=-1, keepdims=True), a_i)
        rows_a.append(a_i)
    amat = jnp.concatenate(rows_a, axis=0)

    size, idx = c // 2, 1
    while size >= HG_SUB:
        pieces = []
        for e0 in range(0, c, 2 * size):
            o0 = e0 + size
            r = bc[o0:o0 + 1]
            pieces.append(k[e0:o0] * jnp.exp(r - bc[e0:o0]))
            pieces.append(q[o0:o0 + size] * jnp.exp(bc[o0:o0 + size] - r))
        hmat = jnp.concatenate(pieces, axis=0).astype(BF16)
        amat = jnp.where(level == idx, _dot_nt(hmat, hmat), amat)
        size, idx = size // 2, idx + 1
    amat = jnp.where(level >= 0, amat, 0.0)

    st = st_ref[h]
    o = _dot_nt((q * jnp.exp(bc)).astype(BF16), st.astype(BF16)) + _dot(amat.astype(BF16), v.astype(BF16))
    bl = bc[c - 1:c]
    kdec = (k * jnp.exp(bl - bc)).astype(BF16)
    st_ref[h] = jnp.exp(bl) * st + _dot(v.T.astype(BF16), kdec)

    y = o * lax.rsqrt(jnp.mean(o * o, axis=-1, keepdims=True) + EPS) * hg_g
    return y * _silu(zg)


def _hgrn_kernel(z_ref, lbl_ref, g_ref, o_ref, st_ref, *, l_idx):
    @pl.when(pl.program_id(1) == 0)
    def _():
        st_ref[...] = jnp.zeros_like(st_ref)

    lg = lbl_ref[...]
    ex = jnp.exp(lg - jnp.max(lg, axis=0, keepdims=True))
    sm = ex / jnp.sum(ex, axis=0, keepdims=True)
    lb_all = jnp.sum(sm[:l_idx + 1], axis=0, keepdims=True)

    c = HG_CHUNK
    ri = lax.broadcasted_iota(jnp.int32, (c, c), 0)
    ci = lax.broadcasted_iota(jnp.int32, (c, c), 1)
    tri = jnp.where(ci <= ri, 1.0, 0.0).astype(BF16)
    level = jnp.where(ci > ri, -1, 0)
    size, idx = c // 2, 1
    while size >= HG_SUB:
        sh = size.bit_length() - 1
        paired = ((ri >> (sh + 1)) == (ci >> (sh + 1))) & ((ri >> sh) != (ci >> sh)) & (ci <= ri)
        level = jnp.where(paired, idx, level)
        size, idx = size // 2, idx + 1
    n_chunks = z_ref.shape[1] // c

    def body(ch, carry):
        r0 = pl.multiple_of(ch * c, c)
        for h in range(HG_HEADS):
            sl = slice(h * HG_HD, (h + 1) * HG_HD)
            zq = z_ref[0, pl.ds(r0, c), h * HG_HD:(h + 1) * HG_HD]
            zf = z_ref[0, pl.ds(r0, c), HG_WIDTH + h * HG_HD:HG_WIDTH + (h + 1) * HG_HD]
            zi = z_ref[0, pl.ds(r0, c), 2 * HG_WIDTH + h * HG_HD:2 * HG_WIDTH + (h + 1) * HG_HD]
            zg = z_ref[0, pl.ds(r0, c), 3 * HG_WIDTH + h * HG_HD:3 * HG_WIDTH + (h + 1) * HG_HD]
            y = _hgrn_chunk(zq, zf, zi, zg, lb_all[:, sl], g_ref[...], st_ref, h, tri, level)
            o_ref[0, pl.ds(r0, c), h * HG_HD:(h + 1) * HG_HD] = y.astype(o_ref.dtype)
        return carry

    lax.fori_loop(0, n_chunks, body, 0, unroll=2)


def _hgrn(zhg, lb_logits, hg_g, l_idx, tb):
    bsz, t, _ = zhg.shape
    return pl.pallas_call(
        functools.partial(_hgrn_kernel, l_idx=l_idx),
        grid=(bsz, t // tb),
        in_specs=[pl.BlockSpec((1, tb, 4 * HG_WIDTH), lambda b, i: (b, i, 0)),
                  pl.BlockSpec(lb_logits.shape, lambda b, i: (0, 0)),
                  pl.BlockSpec((1, HG_HD), lambda b, i: (0, 0))],
        out_specs=pl.BlockSpec((1, tb, HG_WIDTH), lambda b, i: (b, i, 0)),
        out_shape=jax.ShapeDtypeStruct((bsz, t, HG_WIDTH), BF16),
        scratch_shapes=[pltpu.VMEM((HG_HEADS, HG_HD, HG_HD), F32)],
        compiler_params=pltpu.CompilerParams(
            dimension_semantics=("arbitrary", "arbitrary"), vmem_limit_bytes=VMEM_LIMIT),
        name="hgrn",
    )(zhg, lb_logits, hg_g)


def _compress_kernel(xk_ref, xv_ref, pe_ref, w1_ref, w2_ref, ko_ref, vo_ref):
    half = CMP_STRIDE * NSA_HD
    outs = []
    for kv, x_ref in enumerate((xk_ref, xv_ref)):
        x = x_ref[0, 0]
        ha = _dot((x + pe_ref[kv, 0:1, :]).astype(BF16), w1_ref[kv, :half, :])
        hb = _dot((x + pe_ref[kv, 1:2, :]).astype(BF16), w1_ref[kv, half:, :])
        n = x.shape[0]
        pre = ha + pltpu.roll(hb, n - 1, axis=0)
        outs.append(_dot(_silu(pre).astype(BF16), w2_ref[kv]))
    ko_ref[0, 0] = outs[0].astype(ko_ref.dtype)
    vo_ref[0, 0] = outs[1].T.astype(vo_ref.dtype)


def _compress(xk, xv, pe2, w1, w2):
    bsz, g, nseg, width = xk.shape
    x_spec = pl.BlockSpec((1, 1, nseg, width), lambda b, j: (b, j, 0, 0))
    return pl.pallas_call(
        _compress_kernel,
        grid=(bsz, g),
        in_specs=[x_spec, x_spec,
                  pl.BlockSpec(pe2.shape, lambda b, j: (0, 0, 0)),
                  pl.BlockSpec(w1.shape, lambda b, j: (0, 0, 0)),
                  pl.BlockSpec(w2.shape, lambda b, j: (0, 0, 0))],
        out_specs=[pl.BlockSpec((1, 1, nseg, NSA_HD), lambda b, j: (b, j, 0, 0)),
                   pl.BlockSpec((1, 1, NSA_HD, nseg), lambda b, j: (b, j, 0, 0))],
        out_shape=[jax.ShapeDtypeStruct((bsz, g, nseg, NSA_HD), BF16),
                   jax.ShapeDtypeStruct((bsz, g, NSA_HD, nseg), BF16)],
        compiler_params=pltpu.CompilerParams(
            dimension_semantics=("arbitrary", "arbitrary"), vmem_limit_bytes=VMEM_LIMIT),
        name="compress",
    )(xk, xv, pe2, w1, w2)


def _nsa_kernel(qt_ref, kc_ref, vct_ref, ks_ref, vst_ref, kw_ref, vwt_ref, gt_ref, mt_ref,
                o_ref, sel_ref, *, tq, tks):
    r = NSA_GROUP
    n_g = NSA_KV_HEADS
    lanes = r * tq
    q0 = pl.program_id(1) * tq
    q_ts = [jnp.concatenate([qt_ref[0, g * r + hh] for hh in range(r)], axis=1)
            for g in range(n_g)]

    def tile_heads(a):
        return jnp.concatenate([a] * r, axis=1)

    n_blk = kc_ref.shape[2]
    nb = mt_ref.shape[0]
    blk_end = lax.broadcasted_iota(jnp.int32, (n_blk, tq), 0) * CMP_STRIDE + (CMP_BLOCK - 1)
    t_row = q0 + lax.broadcasted_iota(jnp.int32, (1, tq), 1)
    cvalid = tile_heads(jnp.where(blk_end <= t_row, 1.0, 0.0)) > 0.5
    some = tile_heads((t_row >= CMP_BLOCK - 1).astype(F32))
    j = lax.broadcasted_iota(jnp.int32, (nb, tq), 0)
    cur = jnp.right_shift(t_row, SLC_BLOCK.bit_length() - 1)
    forced = (j == 0) | (j == cur) | (j == cur - 1)
    o_cmp = []
    for g in range(n_g):
        s = jnp.where(cvalid, _dot(kc_ref[0, g], q_ts[g]), NEG)
        e = jnp.exp(s - jnp.max(s, axis=0, keepdims=True))
        p = e * (some / jnp.sum(e, axis=0, keepdims=True))
        o_cmp.append(_dot(vct_ref[0, g], p.astype(BF16)))
        psum = p[:, 0:tq]
        for hh in range(1, r):
            psum = psum + p[:, hh * tq:(hh + 1) * tq]
        imp = jnp.dot(mt_ref[...], psum, preferred_element_type=F32, precision=lax.Precision.HIGHEST)
        imp = jnp.where(j <= cur, jnp.where(forced, jnp.inf, imp), -1.0)
        rank = jnp.zeros((nb, tq), jnp.int32)
        for i in range(nb):
            row_i = imp[i:i + 1, :]
            ahead = (row_i > imp) | ((row_i == imp) & (j > i))
            rank = rank + ahead.astype(jnp.int32)
        sel_ref[g] = jnp.where(rank < min(SLC_TOPK, nb), 0.0, NEG)

    blocks_per_tile = tks // SLC_BLOCK
    rel = (lax.broadcasted_iota(jnp.int32, (tks, tq), 1)
           - lax.broadcasted_iota(jnp.int32, (tks, tq), 0))

    def slc_body(it, carry):
        k0 = pl.multiple_of(it * tks, tks)
        causal = rel + (q0 - k0) >= 0
        out = []
        for g in range(n_g):
            m, l, acc = carry[g]
            rows = sel_ref[g, pl.ds(pl.multiple_of(it * blocks_per_tile, blocks_per_tile), blocks_per_tile), :]
            bias = jnp.concatenate([jnp.broadcast_to(rows[jj:jj + 1], (SLC_BLOCK, tq))
                                    for jj in range(blocks_per_tile)], axis=0)
            bias = jnp.where(causal, bias, NEG)
            sc = _dot(ks_ref[0, g, pl.ds(k0, tks), :], q_ts[g]) + tile_heads(bias)
            m_new = jnp.maximum(m, jnp.max(sc, axis=0, keepdims=True))
            alpha = jnp.exp(m - m_new)
            pe = jnp.exp(sc - m_new)
            l = alpha * l + jnp.sum(pe, axis=0, keepdims=True)
            acc = alpha * acc + _dot(vst_ref[0, g, :, pl.ds(k0, tks)], pe.astype(BF16))
            out.append((m_new, l, acc))
        return tuple(out)

    init = tuple((jnp.full((1, lanes), NEG, F32), jnp.zeros((1, lanes), F32), jnp.zeros((NSA_HD, lanes), F32))
                 for _ in range(n_g))
    n_it = (q0 + tq + tks - 1) // tks
    fin = lax.fori_loop(0, n_it, slc_body, init)
    o_slc = [acc * (1.0 / l) for (_, l, acc) in fin]

    span = WIN + tq
    start = pl.multiple_of(jnp.maximum(q0 - WIN, 0), tq)
    dist = (lax.broadcasted_iota(jnp.int32, (span, tq), 1)
            - lax.broadcasted_iota(jnp.int32, (span, tq), 0)) + (q0 - start)
    wbias = tile_heads(jnp.where((dist >= 0) & (dist < WIN), 0.0, NEG))
    o_win = []
    for g in range(n_g):
        sw = _dot(kw_ref[0, g, pl.ds(start, span), :], q_ts[g]) + wbias
        ew = jnp.exp(sw - jnp.max(sw, axis=0, keepdims=True))
        o_win.append(_dot(vwt_ref[0, g, :, pl.ds(start, span)], ew.astype(BF16))
                     * (1.0 / jnp.sum(ew, axis=0, keepdims=True)))

    for g in range(n_g):
        for hh in range(r):
            cols = slice(hh * tq, (hh + 1) * tq)
            head = g * r + hh
            o_h = jnp.zeros((NSA_HD, tq), F32)
            for br, o_b in enumerate((o_cmp[g], o_slc[g], o_win[g])):
                row = head * N_BRANCH + br
                o_h = o_h + gt_ref[0, row:row + 1, :] * o_b[:, cols]
            o_ref[0, :, head * NSA_HD:(head + 1) * NSA_HD] = o_h.T.astype(o_ref.dtype)


def _nsa(q_t, kc, vct, ks, vst, kw, vwt, gates_t, mt, tq, tks):
    bsz, _, _, t = q_t.shape
    n_blk = kc.shape[2]
    n_g = NSA_KV_HEADS
    full = lambda b, i: (b, 0, 0, 0)
    k_spec = pl.BlockSpec((1, n_g, t, NSA_HD), full)
    vt_spec = pl.BlockSpec((1, n_g, NSA_HD, t), full)
    return pl.pallas_call(
        functools.partial(_nsa_kernel, tq=tq, tks=tks),
        grid=(bsz, t // tq),
        in_specs=[pl.BlockSpec((1, NSA_HEADS, NSA_HD, tq), lambda b, i: (b, 0, 0, i)),
                  pl.BlockSpec((1, n_g, n_blk, NSA_HD), full),
                  pl.BlockSpec((1, n_g, NSA_HD, n_blk), full),
                  k_spec, vt_spec, k_spec, vt_spec,
                  pl.BlockSpec((1, GATE_ROWS, tq), lambda b, i: (b, 0, i)),
                  pl.BlockSpec(mt.shape, lambda b, i: (0, 0))],
        out_specs=pl.BlockSpec((1, tq, NSA_WIDTH), lambda b, i: (b, i, 0)),
        out_shape=jax.ShapeDtypeStruct((bsz, t, NSA_WIDTH), BF16),
        scratch_shapes=[pltpu.VMEM((n_g, mt.shape[0], tq), F32)],
        compiler_params=pltpu.CompilerParams(
            dimension_semantics=("arbitrary", "arbitrary"), vmem_limit_bytes=VMEM_LIMIT),
        name="nsa",
    )(q_t, kc, vct, ks, vst, kw, vwt, gates_t, mt)


def _causal_conv(u, prev, cw, cb):
    tm = u.shape[0]
    g = SUBLANES
    r8 = lax.broadcasted_iota(jnp.int32, (g, u.shape[1]), 0)
    wrap1 = jnp.where(r8 == 0, prev[2 * g - 1:2 * g], pltpu.roll(u[tm - g:], 1, axis=0))
    wrap2 = jnp.where(r8 == 0, prev[g - 1:g], pltpu.roll(u[tm - 2 * g:tm - g], 1, axis=0))
    u1 = jnp.concatenate([wrap1, u[:tm - g]], axis=0)
    u2 = jnp.concatenate([wrap2, wrap1, u[:tm - 2 * g]], axis=0)
    return cb + u2 * cw[0:1] + u1 * cw[1:2] + u * cw[2:3]


def _ffn_kernel(x_ref, hg_ref, ns_ref, mod_ref, wo_ref, g2_ref, wu_ref, cw_ref, cb_ref, wd_ref,
                o_ref, carry_ref, buf_ref, h2_ref, u_ref, g_ref, acc_ref, *, fc):
    mix = _dot(hg_ref[0], wo_ref[:HG_WIDTH, :]) + _dot(ns_ref[0], wo_ref[HG_WIDTH:, :])
    x1_nat = x_ref[0] + mod_ref[0, 2:3, :] * mix
    n_lt = D_MODEL // LANES
    tm = x1_nat.shape[0]
    groups = tm // SUBLANES
    pitch = buf_ref.shape[1] // SUBLANES
    for c in range(n_lt):
        for sg in range(SUBLANES):
            buf_ref[c, sg * pitch:sg * pitch + groups] = x1_nat[sg * groups:(sg + 1) * groups,
                                                                c * LANES:(c + 1) * LANES]

    x1 = jnp.concatenate(
        [jnp.concatenate([buf_ref[c, pl.ds(j, SUBLANES, stride=pitch), :] for j in range(groups)], axis=0)
         for c in range(n_lt)], axis=1)
    y = x1 * lax.rsqrt(jnp.mean(x1 * x1, axis=-1, keepdims=True) + EPS) * g2_ref[...]
    h2_ref[...] = (y * (1.0 + mod_ref[0, 4:5, :]) + mod_ref[0, 3:4, :]).astype(h2_ref.dtype)
    acc_ref[...] = jnp.zeros_like(acc_ref)
    first = pl.program_id(1) == 0

    def up(jc, slot):
        for half in range(2):
            off = pl.multiple_of(half * D_FF + jc * fc, fc)
            u_ref[slot, half] = _dot(h2_ref[...], wu_ref[:, pl.ds(off, fc)])

    def act(jc, slot):
        halves = []
        for half in range(2):
            cols = pl.ds(pl.multiple_of(half * D_FF + jc * fc, fc), fc)
            u = u_ref[slot, half]
            prev = jnp.where(first, 0.0, carry_ref[:, cols])
            carry_ref[:, cols] = u[tm - 2 * SUBLANES:]
            halves.append(_causal_conv(u, prev, cw_ref[:, cols], cb_ref[:, cols]))
        a, v = halves
        g_ref[slot] = (_silu(a) * v).astype(g_ref.dtype)

    def down(jc, slot):
        acc_ref[...] += _dot(g_ref[slot], wd_ref[pl.ds(pl.multiple_of(jc * fc, fc), fc), :])

    n = D_FF // fc

    def tick(i):
        if i < n:
            up(i, i % 2)
        if 0 <= i - 1 < n:
            act(i - 1, (i - 1) % 2)
        if 0 <= i - 2 < n:
            down(i - 2, i % 2)

    tick(0)
    tick(1)
    pairs = (n - 2) // 2

    def body(k, carry):
        i = 2 + 2 * k
        up(i, 0)
        act(i - 1, 1)
        down(i - 2, 0)
        up(i + 1, 1)
        act(i, 0)
        down(i - 1, 1)
        return carry

    lax.fori_loop(0, pairs, body, 0)
    for i in range(2 + 2 * pairs, n + 2):
        tick(i)

    out = x1 + mod_ref[0, 5:6, :] * acc_ref[...]
    for c in range(n_lt):
        for j in range(groups):
            buf_ref[c, pl.ds(j, SUBLANES, stride=pitch), :] = out[j * SUBLANES:(j + 1) * SUBLANES,
                                                                  c * LANES:(c + 1) * LANES]
    for c in range(n_lt):
        for sg in range(SUBLANES):
            o_ref[0, sg * groups:(sg + 1) * groups, c * LANES:(c + 1) * LANES] = (
                buf_ref[c, sg * pitch:sg * pitch + groups])


def _ffn(x, o_hg, o_nsa, mod, w_out, g2, w_up, conv_w, conv_b, w_down, tm, fc):
    bsz, t, _ = x.shape
    row_spec = lambda w: pl.BlockSpec((1, tm, w), lambda b, i: (b, i, 0))
    groups = tm // SUBLANES
    pad = SUBLANES if (groups // SUBLANES) % 2 == 0 else 0
    resident = lambda a: pl.BlockSpec(a.shape, lambda b, i: (0, 0), pipeline_mode=pl.Buffered(1))
    return pl.pallas_call(
        functools.partial(_ffn_kernel, fc=fc),
        grid=(bsz, t // tm),
        in_specs=[row_spec(D_MODEL), row_spec(HG_WIDTH), row_spec(NSA_WIDTH),
                  pl.BlockSpec((1, 6, D_MODEL), lambda b, i: (b, 0, 0)),
                  resident(w_out), resident(g2), resident(w_up), resident(conv_w), resident(conv_b),
                  resident(w_down)],
        out_specs=row_spec(D_MODEL),
        out_shape=jax.ShapeDtypeStruct(x.shape, F32),
        scratch_shapes=[pltpu.VMEM((2 * SUBLANES, 2 * D_FF), F32),
                        pltpu.VMEM((D_MODEL // LANES, tm + SUBLANES * pad, LANES), F32),
                        pltpu.VMEM((tm, D_MODEL), BF16),
                        pltpu.VMEM((2, 2, tm, fc), F32),
                        pltpu.VMEM((2, tm, fc), BF16),
                        pltpu.VMEM((tm, D_MODEL), F32)],
        compiler_params=pltpu.CompilerParams(
            dimension_semantics=("arbitrary", "arbitrary"), vmem_limit_bytes=VMEM_LIMIT),
        name="ffn",
    )(x, o_hg, o_nsa, mod, w_out, g2, w_up, conv_w, conv_b, w_down)


def _rope_tables():
    half = ROPE_DIM // 2
    inv = ROPE_THETA ** (-jnp.arange(half, dtype=F32) * 2.0 / ROPE_DIM)
    d = np.arange(LANES) % NSA_HD
    inv_full = jnp.where(jnp.asarray(d < ROPE_DIM), jnp.tile(inv, LANES // half), 0.0)
    sgn = np.where(d < half, -1.0, np.where(d < ROPE_DIM, 1.0, 0.0)).astype(np.float32)
    return inv_full.reshape(1, LANES).astype(F32), jnp.asarray(sgn).reshape(1, LANES)


def _selection_tables(t):
    n_seg = t // CMP_STRIDE
    nb = t // SLC_BLOCK
    cst = np.arange(n_seg) * CMP_STRIDE
    sst = np.arange(nb) * SLC_BLOCK
    ovl = np.clip(np.minimum(cst[:, None] + CMP_BLOCK, sst[None] + SLC_BLOCK)
                  - np.maximum(cst[:, None], sst[None]), 0, None) / CMP_BLOCK
    ovl[(t - CMP_BLOCK) // CMP_STRIDE + 1:] = 0.0
    return (jnp.asarray(ovl.T, dtype=F32),)


def _layer(x, mod, pos3, l, p, tables):
    bsz, t, _ = x.shape
    inv_full, sgn, mt = tables
    w_in_p = jnp.pad(p["w_in"][l], ((0, 0), (0, IN_COLS_PAD - IN_COLS))).astype(BF16)
    qg2 = jnp.tile(p["q_norm_g"][l].reshape(1, NSA_HD), (1, LANES // NSA_HD))
    kg2 = jnp.tile(p["k_norm_g"][l], (1, LANES // NSA_HD))
    zhg, q_t, kc, vc, ks, vst, kw, vwt, gates_t = _inproj(
        x, mod, p["norm1_g"][l].reshape(1, D_MODEL), w_in_p, pos3, inv_full, sgn, qg2, kg2, tm=256)

    o_hg = _hgrn(zhg, p["lb_logits"], p["hg_norm_g"][l].reshape(1, HG_HD), l, tb=512)

    n_seg = t // CMP_STRIDE
    seg_w = CMP_STRIDE * NSA_HD
    pe2 = p["pe_cmp"][l].reshape(2, 2, seg_w)
    kcmp, vcmp_t = _compress(kc.reshape(bsz, NSA_KV_HEADS, n_seg, seg_w),
                             vc.reshape(bsz, NSA_KV_HEADS, n_seg, seg_w),
                             pe2, p["w_cmp1"][l].astype(BF16), p["w_cmp2"][l].astype(BF16))
    o_nsa = _nsa(q_t, kcmp, vcmp_t, ks, vst, kw, vwt, gates_t, mt, tq=128, tks=512)

    return _ffn(x, o_hg, o_nsa, mod, p["w_out"][l].astype(BF16), p["norm2_g"][l].reshape(1, D_MODEL),
                p["w_up"][l].astype(BF16), p["conv_w"][l], p["conv_b"][l].reshape(1, 2 * D_FF),
                p["w_down"][l].astype(BF16), tm=512, fc=256)


def kernel(x, c, positions, w_ada, b_ada, norm1_g, w_in, lb_logits, hg_norm_g, q_norm_g, k_norm_g, pe_cmp, w_cmp1, w_cmp2, w_out, norm2_g, w_up, conv_w, conv_b, w_down):
    p = dict(w_in=w_in, norm1_g=norm1_g, lb_logits=lb_logits, hg_norm_g=hg_norm_g, q_norm_g=q_norm_g,
             k_norm_g=k_norm_g, pe_cmp=pe_cmp, w_cmp1=w_cmp1, w_cmp2=w_cmp2, w_out=w_out,
             norm2_g=norm2_g, w_up=w_up, conv_w=conv_w, conv_b=conv_b, w_down=w_down)
    bsz, t, _ = x.shape
    tables = _rope_tables() + _selection_tables(t)
    pos3 = positions.reshape(bsz, t, 1)
    for l in range(w_ada.shape[0]):
        mod = _ada(c, w_ada[l], b_ada[l]).reshape(bsz, 6, D_MODEL)
        x = _layer(x, mod, pos3, l, p, tables)
    return x
```

```python
import functools

import jax
import jax.numpy as jnp
import numpy as np
from jax import lax
from jax.experimental import pallas as pl
from jax.experimental.pallas import tpu as pltpu

D_MODEL = 1024
HG_HEADS = 4
HG_HD = 128
HG_WIDTH = HG_HEADS * HG_HD
HG_CHUNK = 64
HG_SUB = 8
LOG2E = 1.4426950408889634
NSA_HEADS = 8
NSA_KV_HEADS = 2
NSA_HD = 64
NSA_GROUP = NSA_HEADS // NSA_KV_HEADS
NSA_WIDTH = NSA_HEADS * NSA_HD
N_BRANCH = 3
CMP_BLOCK = 32
CMP_STRIDE = 16
CMP_HIDDEN = 256
SLC_BLOCK = 64
SLC_TOPK = 16
WIN = 512
ROPE_DIM = NSA_HD // 4
ROPE_THETA = 500000.0
D_FF = 2816
CONV_W = 3
EPS = 1e-6
NEG = -1e30

LANES = 128
SUBLANES = 8
VMEM_LIMIT = 56 * 1024 * 1024

OFF_HG = 0
OFF_Q = 4 * HG_WIDTH
OFF_KV = OFF_Q + NSA_WIDTH
OFF_G = OFF_KV + 6 * NSA_KV_HEADS * NSA_HD
IN_COLS = OFF_G + N_BRANCH * NSA_HEADS
IN_COLS_PAD = OFF_G + LANES
GATE_ROWS = 32

BF16 = jnp.bfloat16
F32 = jnp.float32


def _dot(a, b):
    return jnp.dot(a, b, preferred_element_type=F32)


def _dot_nt(a, b):
    return lax.dot_general(a, b, (((1,), (1,)), ((), ())), preferred_element_type=F32)


def _sigmoid(x):
    return 1.0 / (1.0 + jnp.exp(-x))


def _silu(x):
    return x * _sigmoid(x)


def _ada_kernel(c_ref, w_ref, b_ref, o_ref):
    cs = _silu(c_ref[...])
    o_ref[...] = jnp.dot(cs, w_ref[...], preferred_element_type=F32,
                         precision=lax.Precision.HIGHEST) + b_ref[...]


def _ada(c, w, b):
    bsz = c.shape[0]
    n = w.shape[1]
    tn = D_MODEL
    return pl.pallas_call(
        _ada_kernel,
        grid=(n // tn,),
        in_specs=[pl.BlockSpec((bsz, D_MODEL), lambda j: (0, 0)),
                  pl.BlockSpec((D_MODEL, tn), lambda j: (0, j)),
                  pl.BlockSpec((1, tn), lambda j: (0, j))],
        out_specs=pl.BlockSpec((bsz, tn), lambda j: (0, j)),
        out_shape=jax.ShapeDtypeStruct((bsz, n), F32),
        name="ada",
    )(c, w, b.reshape(1, n))


def _pair_norm_rope_t(xt, g_t, cos_t, sin_t):
    half = ROPE_DIM // 2
    outs = []
    for hh in range(2):
        x = xt[hh * NSA_HD:(hh + 1) * NSA_HD]
        ms = jnp.mean(x * x, axis=0, keepdims=True)
        xn = x * lax.rsqrt(ms + EPS) * g_t[hh * NSA_HD:(hh + 1) * NSA_HD]
        x1, x2 = xn[:half], xn[half:ROPE_DIM]
        outs += [x1 * cos_t - x2 * sin_t, x2 * cos_t + x1 * sin_t, xn[ROPE_DIM:]]
    return jnp.concatenate(outs, axis=0)


def _inproj_kernel(x_ref, mod_ref, g1_ref, w_ref, pos_ref, inv_ref, qg_ref, kg_ref,
                   zhg_ref, qt_ref, kc_ref, vc_ref, ks_ref, vst_ref, kw_ref, vwt_ref, gt_ref):
    x = x_ref[0]
    ms = jnp.mean(x * x, axis=-1, keepdims=True)
    y = x * lax.rsqrt(ms + EPS) * g1_ref[...]
    h = (y * (1.0 + mod_ref[0, 1:2, :]) + mod_ref[0, 0:1, :]).astype(BF16)

    zhg_ref[0] = _dot(h, w_ref[:, OFF_HG:OFF_Q])

    tm = x.shape[0]
    reps = tm // LANES

    def lane_tile(a):
        return jnp.concatenate([a] * reps, axis=1)

    ang = lane_tile(inv_ref[...]) * pos_ref[0].astype(F32)
    cos_t = jnp.cos(ang)
    sin_t = jnp.sin(ang)

    zq = _dot(h, w_ref[:, OFF_Q:OFF_KV])
    scale = NSA_HD ** -0.5
    qg_t = lane_tile(qg_ref[...])
    for p in range(NSA_HEADS // 2):
        rt = _pair_norm_rope_t(zq[:, p * LANES:(p + 1) * LANES].T, qg_t, cos_t, sin_t)
        rt = (rt * scale).astype(qt_ref.dtype)
        qt_ref[0, 2 * p] = rt[:NSA_HD]
        qt_ref[0, 2 * p + 1] = rt[NSA_HD:]

    zkv = _dot(h, w_ref[:, OFF_KV:OFF_G])
    k_outs = (kc_ref, ks_ref, kw_ref)
    for br in range(N_BRANCH):
        kt = _pair_norm_rope_t(zkv[:, (2 * br) * LANES:(2 * br + 1) * LANES].T,
                               lane_tile(kg_ref[br]), cos_t, sin_t)
        kk = kt.T.astype(k_outs[br].dtype)
        for g in range(NSA_KV_HEADS):
            k_outs[br][0, g] = kk[:, g * NSA_HD:(g + 1) * NSA_HD]
    vc = zkv[:, LANES:2 * LANES]
    for g in range(NSA_KV_HEADS):
        vc_ref[0, g] = vc[:, g * NSA_HD:(g + 1) * NSA_HD]
    for br, vt_ref in ((1, vst_ref), (2, vwt_ref)):
        vt = zkv[:, (2 * br + 1) * LANES:(2 * br + 2) * LANES].T.astype(vt_ref.dtype)
        for g in range(NSA_KV_HEADS):
            vt_ref[0, g] = vt[g * NSA_HD:(g + 1) * NSA_HD]

    gates = _sigmoid(_dot(h, w_ref[:, OFF_G:IN_COLS_PAD]))
    gt_ref[0] = gates.T[:GATE_ROWS]


def _inproj(x, mod, g1, w_in_p, pos_row, inv_t, qg_t, kg_t, tm):
    bsz, t, _ = x.shape
    grid = (bsz, t // tm)
    kv_shape = (bsz, NSA_KV_HEADS, t, NSA_HD)
    kv_spec = pl.BlockSpec((1, NSA_KV_HEADS, tm, NSA_HD), lambda b, i: (b, 0, i, 0))
    vt_shape = (bsz, NSA_KV_HEADS, NSA_HD, t)
    vt_spec = pl.BlockSpec((1, NSA_KV_HEADS, NSA_HD, tm), lambda b, i: (b, 0, 0, i))
    const = lambda b, i: (0, 0)
    return pl.pallas_call(
        _inproj_kernel,
        grid=grid,
        in_specs=[pl.BlockSpec((1, tm, D_MODEL), lambda b, i: (b, i, 0)),
                  pl.BlockSpec((1, 6, D_MODEL), lambda b, i: (b, 0, 0)),
                  pl.BlockSpec((1, D_MODEL), const),
                  pl.BlockSpec((D_MODEL, IN_COLS_PAD), const),
                  pl.BlockSpec((1, 1, tm), lambda b, i: (b, 0, i)),
                  pl.BlockSpec(inv_t.shape, const),
                  pl.BlockSpec(qg_t.shape, const),
                  pl.BlockSpec(kg_t.shape, lambda b, i: (0, 0, 0))],
        out_specs=[pl.BlockSpec((1, tm, 4 * HG_WIDTH), lambda b, i: (b, i, 0)),
                   pl.BlockSpec((1, NSA_HEADS, NSA_HD, tm), lambda b, i: (b, 0, 0, i)),
                   kv_spec, kv_spec, kv_spec, vt_spec, kv_spec, vt_spec,
                   pl.BlockSpec((1, GATE_ROWS, tm), lambda b, i: (b, 0, i))],
        out_shape=[jax.ShapeDtypeStruct((bsz, t, 4 * HG_WIDTH), F32),
                   jax.ShapeDtypeStruct((bsz, NSA_HEADS, NSA_HD, t), BF16),
                   jax.ShapeDtypeStruct(kv_shape, F32),
                   jax.ShapeDtypeStruct(kv_shape, F32),
                   jax.ShapeDtypeStruct(kv_shape, BF16),
                   jax.ShapeDtypeStruct(vt_shape, BF16),
                   jax.ShapeDtypeStruct(kv_shape, BF16),
                   jax.ShapeDtypeStruct(vt_shape, BF16),
                   jax.ShapeDtypeStruct((bsz, GATE_ROWS, t), F32)],
        compiler_params=pltpu.CompilerParams(
            dimension_semantics=("arbitrary", "arbitrary"), vmem_limit_bytes=VMEM_LIMIT),
        name="inproj",
    )(x, mod, g1, w_in_p, pos_row, inv_t, qg_t, kg_t)


def _hgrn_chunk(zq, zf, zi, zg, lb, hg_g, st_ref, h, tri, level):
    c = HG_CHUNK
    e_z = jnp.exp(-jnp.abs(zf))
    logsig = jnp.minimum(zf, 0.0) - jnp.log(1.0 + e_z)
    a = jnp.log(lb)
    bb = jnp.log1p(-lb) + logsig
    logf = jnp.maximum(a, bb) + jnp.log(1.0 + jnp.exp(-jnp.abs(a - bb)))
    k = (1.0 - lb) * (jnp.where(zf >= 0.0, e_z, 1.0) / (1.0 + e_z))
    q = _silu(zq)
    v = zi
    l_hi = logf.astype(BF16)
    l_mid = (logf - l_hi.astype(F32)).astype(BF16)
    l_lo = (logf - l_hi.astype(F32) - l_mid.astype(F32)).astype(BF16)
    bc = _dot(tri, l_hi) + _dot(tri, l_mid) + _dot(tri, l_lo)

    col = lax.broadcasted_iota(jnp.int32, (HG_SUB, c), 1)
    b2 = bc * LOG2E
    rows_a = []
    for i in range(c // HG_SUB):
        lo = i * HG_SUB
        b_i = b2[lo:lo + HG_SUB]
        q_i = q[lo:lo + HG_SUB]
        k_i = k[lo:lo + HG_SUB]
        a_i = jnp.zeros((HG_SUB, c), F32)
        for s in range(HG_SUB):
            e = jnp.exp2(jnp.minimum(b_i - b_i[s:s + 1], 0.0)) * q_i * k_i[s:s + 1]
            a_i = jnp.where(col == lo + s, jnp.sum(e, axis=-1, keepdims=True), a_i)
        rows_a.append(a_i)
    amat = jnp.concatenate(rows_a, axis=0)

    size, idx = c // 2, 1
    while size >= HG_SUB:
        pieces = []
        for e0 in range(0, c, 2 * size):
            o0 = e0 + size
            r = bc[o0:o0 + 1]
            pieces.append(k[e0:o0] * jnp.exp(r - bc[e0:o0]))
            pieces.append(q[o0:o0 + size] * jnp.exp(bc[o0:o0 + size] - r))
        hmat = jnp.concatenate(pieces, axis=0).astype(BF16)
        amat = jnp.where(level == idx, _dot_nt(hmat, hmat), amat)
        size, idx = size // 2, idx + 1
    amat = jnp.where(level >= 0, amat, 0.0)

    st = st_ref[h]
    o = _dot_nt((q * jnp.exp(bc)).astype(BF16), st.astype(BF16)) + _dot(amat.astype(BF16), v.astype(BF16))
    bl = bc[c - 1:c]
    kdec = (k * jnp.exp(bl - bc)).astype(BF16)
    st_ref[h] = jnp.exp(bl) * st + _dot(v.T.astype(BF16), kdec)

    y = o * lax.rsqrt(jnp.mean(o * o, axis=-1, keepdims=True) + EPS) * hg_g
    return y * _silu(zg)


def _hgrn_kernel(z_ref, lbl_ref, g_ref, o_ref, st_ref, *, l_idx):
    @pl.when(pl.program_id(1) == 0)
    def _():
        st_ref[...] = jnp.zeros_like(st_ref)

    lg = lbl_ref[...]
    ex = jnp.exp(lg - jnp.max(lg, axis=0, keepdims=True))
    sm = ex / jnp.sum(ex, axis=0, keepdims=True)
    lb_all = jnp.sum(sm[:l_idx + 1], axis=0, keepdims=True)

    c = HG_CHUNK
    ri = lax.broadcasted_iota(jnp.int32, (c, c), 0)
    ci = lax.broadcasted_iota(jnp.int32, (c, c), 1)
    tri = jnp.where(ci <= ri, 1.0, 0.0).astype(BF16)
    level = jnp.where(ci > ri, -1, 0)
    size, idx = c // 2, 1
    while size >= HG_SUB:
        sh = size.bit_length() - 1
        paired = ((ri >> (sh + 1)) == (ci >> (sh + 1))) & ((ri >> sh) != (ci >> sh)) & (ci <= ri)
        level = jnp.where(paired, idx, level)
        size, idx = size // 2, idx + 1
    n_chunks = z_ref.shape[1] // c

    def body(ch, carry):
        r0 = pl.multiple_of(ch * c, c)
        for h in range(HG_HEADS):
            sl = slice(h * HG_HD, (h + 1) * HG_HD)
            zq = z_ref[0, pl.ds(r0, c), h * HG_HD:(h + 1) * HG_HD]
            zf = z_ref[0, pl.ds(r0, c), HG_WIDTH + h * HG_HD:HG_WIDTH + (h + 1) * HG_HD]
            zi = z_ref[0, pl.ds(r0, c), 2 * HG_WIDTH + h * HG_HD:2 * HG_WIDTH + (h + 1) * HG_HD]
            zg = z_ref[0, pl.ds(r0, c), 3 * HG_WIDTH + h * HG_HD:3 * HG_WIDTH + (h + 1) * HG_HD]
            y = _hgrn_chunk(zq, zf, zi, zg, lb_all[:, sl], g_ref[...], st_ref, h, tri, level)
            o_ref[0, pl.ds(r0, c), h * HG_HD:(h + 1) * HG_HD] = y.astype(o_ref.dtype)
        return carry

    lax.fori_loop(0, n_chunks, body, 0, unroll=2)


def _hgrn(zhg, lb_logits, hg_g, l_idx, tb):
    bsz, t, _ = zhg.shape
    return pl.pallas_call(
        functools.partial(_hgrn_kernel, l_idx=l_idx),
        grid=(bsz, t // tb),
        in_specs=[pl.BlockSpec((1, tb, 4 * HG_WIDTH), lambda b, i: (b, i, 0)),
                  pl.BlockSpec(lb_logits.shape, lambda b, i: (0, 0)),
                  pl.BlockSpec((1, HG_HD), lambda b, i: (0, 0))],
        out_specs=pl.BlockSpec((1, tb, HG_WIDTH), lambda b, i: (b, i, 0)),
        out_shape=jax.ShapeDtypeStruct((bsz, t, HG_WIDTH), BF16),
        scratch_shapes=[pltpu.VMEM((HG_HEADS, HG_HD, HG_HD), F32)],
        compiler_params=pltpu.CompilerParams(
            dimension_semantics=("arbitrary", "arbitrary"), vmem_limit_bytes=VMEM_LIMIT),
        name="hgrn",
    )(zhg, lb_logits, hg_g)


def _compress_kernel(xk_ref, xv_ref, pe_ref, w1_ref, w2_ref, ko_ref, vo_ref):
    half = CMP_STRIDE * NSA_HD
    outs = []
    for kv, x_ref in enumerate((xk_ref, xv_ref)):
        x = x_ref[0, 0]
        ha = _dot((x + pe_ref[kv, 0:1, :]).astype(BF16), w1_ref[kv, :half, :])
        hb = _dot((x + pe_ref[kv, 1:2, :]).astype(BF16), w1_ref[kv, half:, :])
        n = x.shape[0]
        pre = ha + pltpu.roll(hb, n - 1, axis=0)
        outs.append(_dot(_silu(pre).astype(BF16), w2_ref[kv]))
    ko_ref[0, 0] = outs[0].astype(ko_ref.dtype)
    vo_ref[0, 0] = outs[1].T.astype(vo_ref.dtype)


def _compress(xk, xv, pe2, w1, w2):
    bsz, g, nseg, width = xk.shape
    x_spec = pl.BlockSpec((1, 1, nseg, width), lambda b, j: (b, j, 0, 0))
    return pl.pallas_call(
        _compress_kernel,
        grid=(bsz, g),
        in_specs=[x_spec, x_spec,
                  pl.BlockSpec(pe2.shape, lambda b, j: (0, 0, 0)),
                  pl.BlockSpec(w1.shape, lambda b, j: (0, 0, 0)),
                  pl.BlockSpec(w2.shape, lambda b, j: (0, 0, 0))],
        out_specs=[pl.BlockSpec((1, 1, nseg, NSA_HD), lambda b, j: (b, j, 0, 0)),
                   pl.BlockSpec((1, 1, NSA_HD, nseg), lambda b, j: (b, j, 0, 0))],
        out_shape=[jax.ShapeDtypeStruct((bsz, g, nseg, NSA_HD), BF16),
                   jax.ShapeDtypeStruct((bsz, g, NSA_HD, nseg), BF16)],
        compiler_params=pltpu.CompilerParams(
            dimension_semantics=("arbitrary", "arbitrary"), vmem_limit_bytes=VMEM_LIMIT),
        name="compress",
    )(xk, xv, pe2, w1, w2)


def _nsa_kernel(qt_ref, kc_ref, vct_ref, ks_ref, vst_ref, kw_ref, vwt_ref, gt_ref, mt_ref,
                o_ref, sel_ref, *, tq, tks):
    r = NSA_GROUP
    n_g = NSA_KV_HEADS
    lanes = r * tq
    q0 = pl.program_id(1) * tq
    q_ts = [jnp.concatenate([qt_ref[0, g * r + hh] for hh in range(r)], axis=1)
            for g in range(n_g)]

    def tile_heads(a):
        return jnp.concatenate([a] * r, axis=1)

    n_blk = kc_ref.shape[2]
    nb = mt_ref.shape[0]
    blk_end = lax.broadcasted_iota(jnp.int32, (n_blk, tq), 0) * CMP_STRIDE + (CMP_BLOCK - 1)
    t_row = q0 + lax.broadcasted_iota(jnp.int32, (1, tq), 1)
    cvalid = tile_heads(jnp.where(blk_end <= t_row, 1.0, 0.0)) > 0.5
    some = tile_heads((t_row >= CMP_BLOCK - 1).astype(F32))
    j = lax.broadcasted_iota(jnp.int32, (nb, tq), 0)
    cur = jnp.right_shift(t_row, SLC_BLOCK.bit_length() - 1)
    forced = (j == 0) | (j == cur) | (j == cur - 1)
    o_cmp = []
    for g in range(n_g):
        s = jnp.where(cvalid, _dot(kc_ref[0, g], q_ts[g]), NEG)
        e = jnp.exp(s - jnp.max(s, axis=0, keepdims=True))
        p = e * (some / jnp.sum(e, axis=0, keepdims=True))
        o_cmp.append(_dot(vct_ref[0, g], p.astype(BF16)))
        psum = p[:, 0:tq]
        for hh in range(1, r):
            psum = psum + p[:, hh * tq:(hh + 1) * tq]
        imp = jnp.dot(mt_ref[...], psum, preferred_element_type=F32, precision=lax.Precision.HIGHEST)
        imp = jnp.where(j <= cur, jnp.where(forced, jnp.inf, imp), -1.0)
        rank = jnp.zeros((nb, tq), jnp.int32)
        for i in range(nb):
            row_i = imp[i:i + 1, :]
            ahead = (row_i > imp) | ((row_i == imp) & (j > i))
            rank = rank + ahead.astype(jnp.int32)
        sel_ref[g] = jnp.where(rank < min(SLC_TOPK, nb), 0.0, NEG)

    blocks_per_tile = tks // SLC_BLOCK
    rel = (lax.broadcasted_iota(jnp.int32, (tks, tq), 1)
           - lax.broadcasted_iota(jnp.int32, (tks, tq), 0))

    def slc_body(it, carry):
        k0 = pl.multiple_of(it * tks, tks)
        causal = rel + (q0 - k0) >= 0
        out = []
        for g in range(n_g):
            m, l, acc = carry[g]
            rows = sel_ref[g, pl.ds(pl.multiple_of(it * blocks_per_tile, blocks_per_tile), blocks_per_tile), :]
            bias = jnp.concatenate([jnp.broadcast_to(rows[jj:jj + 1], (SLC_BLOCK, tq))
                                    for jj in range(blocks_per_tile)], axis=0)
            bias = jnp.where(causal, bias, NEG)
            sc = _dot(ks_ref[0, g, pl.ds(k0, tks), :], q_ts[g]) + tile_heads(bias)
            m_new = jnp.maximum(m, jnp.max(sc, axis=0, keepdims=True))
            alpha = jnp.exp(m - m_new)
            pe = jnp.exp(sc - m_new)
            l = alpha * l + jnp.sum(pe, axis=0, keepdims=True)
            acc = alpha * acc + _dot(vst_ref[0, g, :, pl.ds(k0, tks)], pe.astype(BF16))
            out.append((m_new, l, acc))
        return tuple(out)

    init = tuple((jnp.full((1, lanes), NEG, F32), jnp.zeros((1, lanes), F32), jnp.zeros((NSA_HD, lanes), F32))
                 for _ in range(n_g))
    n_it = (q0 + tq + tks - 1) // tks
    fin = lax.fori_loop(0, n_it, slc_body, init)
    o_slc = [acc * (1.0 / l) for (_, l, acc) in fin]

    span = WIN + tq
    start = pl.multiple_of(jnp.maximum(q0 - WIN, 0), tq)
    dist = (lax.broadcasted_iota(jnp.int32, (span, tq), 1)
            - lax.broadcasted_iota(jnp.int32, (span, tq), 0)) + (q0 - start)
    wbias = tile_heads(jnp.where((dist >= 0) & (dist < WIN), 0.0, NEG))
    o_win = []
    for g in range(n_g):
        sw = _dot(kw_ref[0, g, pl.ds(start, span), :], q_ts[g]) + wbias
        ew = jnp.exp(sw - jnp.max(sw, axis=0, keepdims=True))
        o_win.append(_dot(vwt_ref[0, g, :, pl.ds(start, span)], ew.astype(BF16))
                     * (1.0 / jnp.sum(ew, axis=0, keepdims=True)))

    for g in range(n_g):
        for hh in range(r):
            cols = slice(hh * tq, (hh + 1) * tq)
            head = g * r + hh
            o_h = jnp.zeros((NSA_HD, tq), F32)
            for br, o_b in enumerate((o_cmp[g], o_slc[g], o_win[g])):
                row = head * N_BRANCH + br
                o_h = o_h + gt_ref[0, row:row + 1, :] * o_b[:, cols]
            o_ref[0, :, head * NSA_HD:(head + 1) * NSA_HD] = o_h.T.astype(o_ref.dtype)


def _nsa(q_t, kc, vct, ks, vst, kw, vwt, gates_t, mt, tq, tks):
    bsz, _, _, t = q_t.shape
    n_blk = kc.shape[2]
    n_g = NSA_KV_HEADS
    full = lambda b, i: (b, 0, 0, 0)
    k_spec = pl.BlockSpec((1, n_g, t, NSA_HD), full)
    vt_spec = pl.BlockSpec((1, n_g, NSA_HD, t), full)
    return pl.pallas_call(
        functools.partial(_nsa_kernel, tq=tq, tks=tks),
        grid=(bsz, t // tq),
        in_specs=[pl.BlockSpec((1, NSA_HEADS, NSA_HD, tq), lambda b, i: (b, 0, 0, i)),
                  pl.BlockSpec((1, n_g, n_blk, NSA_HD), full),
                  pl.BlockSpec((1, n_g, NSA_HD, n_blk), full),
                  k_spec, vt_spec, k_spec, vt_spec,
                  pl.BlockSpec((1, GATE_ROWS, tq), lambda b, i: (b, 0, i)),
                  pl.BlockSpec(mt.shape, lambda b, i: (0, 0))],
        out_specs=pl.BlockSpec((1, tq, NSA_WIDTH), lambda b, i: (b, i, 0)),
        out_shape=jax.ShapeDtypeStruct((bsz, t, NSA_WIDTH), BF16),
        scratch_shapes=[pltpu.VMEM((n_g, mt.shape[0], tq), F32)],
        compiler_params=pltpu.CompilerParams(
            dimension_semantics=("arbitrary", "arbitrary"), vmem_limit_bytes=VMEM_LIMIT),
        name="nsa",
    )(q_t, kc, vct, ks, vst, kw, vwt, gates_t, mt)


def _causal_conv(u, prev, cw, cb):
    tm = u.shape[0]
    g = SUBLANES
    r8 = lax.broadcasted_iota(jnp.int32, (g, u.shape[1]), 0)
    wrap1 = jnp.where(r8 == 0, prev[2 * g - 1:2 * g], pltpu.roll(u[tm - g:], 1, axis=0))
    wrap2 = jnp.where(r8 == 0, prev[g - 1:g], pltpu.roll(u[tm - 2 * g:tm - g], 1, axis=0))
    u1 = jnp.concatenate([wrap1, u[:tm - g]], axis=0)
    u2 = jnp.concatenate([wrap2, wrap1, u[:tm - 2 * g]], axis=0)
    return cb + u2 * cw[0:1] + u1 * cw[1:2] + u * cw[2:3]


def _ffn_kernel(x_ref, hg_ref, ns_ref, mod_ref, wo_ref, g2_ref, wu_ref, cw_ref, cb_ref, wd_ref,
                o_ref, carry_ref, buf_ref, h2_ref, u_ref, g_ref, acc_ref, *, fc):
    mix = _dot(hg_ref[0], wo_ref[:HG_WIDTH, :]) + _dot(ns_ref[0], wo_ref[HG_WIDTH:, :])
    x1_nat = x_ref[0] + mod_ref[0, 2:3, :] * mix
    n_lt = D_MODEL // LANES
    tm = x1_nat.shape[0]
    groups = tm // SUBLANES
    pitch = buf_ref.shape[1] // SUBLANES
    for c in range(n_lt):
        for sg in range(SUBLANES):
            buf_ref[c, sg * pitch:sg * pitch + groups] = x1_nat[sg * groups:(sg + 1) * groups,
                                                                c * LANES:(c + 1) * LANES]

    x1 = jnp.concatenate(
        [jnp.concatenate([buf_ref[c, pl.ds(j, SUBLANES, stride=pitch), :] for j in range(groups)], axis=0)
         for c in range(n_lt)], axis=1)
    y = x1 * lax.rsqrt(jnp.mean(x1 * x1, axis=-1, keepdims=True) + EPS) * g2_ref[...]
    h2_ref[...] = (y * (1.0 + mod_ref[0, 4:5, :]) + mod_ref[0, 3:4, :]).astype(h2_ref.dtype)
    acc_ref[...] = jnp.zeros_like(acc_ref)
    first = pl.program_id(1) == 0

    def up(jc, slot):
        for half in range(2):
            off = pl.multiple_of(half * D_FF + jc * fc, fc)
            u_ref[slot, half] = _dot(h2_ref[...], wu_ref[:, pl.ds(off, fc)])

    def act(jc, slot):
        halves = []
        for half in range(2):
            cols = pl.ds(pl.multiple_of(half * D_FF + jc * fc, fc), fc)
            u = u_ref[slot, half]
            prev = jnp.where(first, 0.0, carry_ref[:, cols])
            carry_ref[:, cols] = u[tm - 2 * SUBLANES:]
            halves.append(_causal_conv(u, prev, cw_ref[:, cols], cb_ref[:, cols]))
        a, v = halves
        g_ref[slot] = (_silu(a) * v).astype(g_ref.dtype)

    def down(jc, slot):
        acc_ref[...] += _dot(g_ref[slot], wd_ref[pl.ds(pl.multiple_of(jc * fc, fc), fc), :])

    n = D_FF // fc

    def tick(i):
        if i < n:
            up(i, i % 2)
        if 0 <= i - 1 < n:
            act(i - 1, (i - 1) % 2)
        if 0 <= i - 2 < n:
            down(i - 2, i % 2)

    tick(0)
    tick(1)
    pairs = (n - 2) // 2

    def body(k, carry):
        i = 2 + 2 * k
        up(i, 0)
        act(i - 1, 1)
        down(i - 2, 0)
        up(i + 1, 1)
        act(i, 0)
        down(i - 1, 1)
        return carry

    lax.fori_loop(0, pairs, body, 0)
    for i in range(2 + 2 * pairs, n + 2):
        tick(i)

    out = x1 + mod_ref[0, 5:6, :] * acc_ref[...]
    for c in range(n_lt):
        for j in range(groups):
            buf_ref[c, pl.ds(j, SUBLANES, stride=pitch), :] = out[j * SUBLANES:(j + 1) * SUBLANES,
                                                                  c * LANES:(c + 1) * LANES]
    for c in range(n_lt):
        for sg in range(SUBLANES):
            o_ref[0, sg * groups:(sg + 1) * groups, c * LANES:(c + 1) * LANES] = (
                buf_ref[c, sg * pitch:sg * pitch + groups])


def _ffn(x, o_hg, o_nsa, mod, w_out, g2, w_up, conv_w, conv_b, w_down, tm, fc):
    bsz, t, _ = x.shape
    row_spec = lambda w: pl.BlockSpec((1, tm, w), lambda b, i: (b, i, 0))
    groups = tm // SUBLANES
    pad = SUBLANES if (groups // SUBLANES) % 2 == 0 else 0
    resident = lambda a: pl.BlockSpec(a.shape, lambda b, i: (0, 0), pipeline_mode=pl.Buffered(1))
    return pl.pallas_call(
        functools.partial(_ffn_kernel, fc=fc),
        grid=(bsz, t // tm),
        in_specs=[row_spec(D_MODEL), row_spec(HG_WIDTH), row_spec(NSA_WIDTH),
                  pl.BlockSpec((1, 6, D_MODEL), lambda b, i: (b, 0, 0)),
                  resident(w_out), resident(g2), resident(w_up), resident(conv_w), resident(conv_b),
                  resident(w_down)],
        out_specs=row_spec(D_MODEL),
        out_shape=jax.ShapeDtypeStruct(x.shape, F32),
        scratch_shapes=[pltpu.VMEM((2 * SUBLANES, 2 * D_FF), F32),
                        pltpu.VMEM((D_MODEL // LANES, tm + SUBLANES * pad, LANES), F32),
                        pltpu.VMEM((tm, D_MODEL), BF16),
                        pltpu.VMEM((2, 2, tm, fc), F32),
                        pltpu.VMEM((2, tm, fc), BF16),
                        pltpu.VMEM((tm, D_MODEL), F32)],
        compiler_params=pltpu.CompilerParams(
            dimension_semantics=("arbitrary", "arbitrary"), vmem_limit_bytes=VMEM_LIMIT),
        name="ffn",
    )(x, o_hg, o_nsa, mod, w_out, g2, w_up, conv_w, conv_b, w_down)


def _rope_tables():
    half = ROPE_DIM // 2
    inv = ROPE_THETA ** (-jnp.arange(half, dtype=F32) * 2.0 / ROPE_DIM)
    return (jnp.tile(inv.reshape(half, 1), (1, LANES)),)


def _gain_t(g):
    return jnp.tile(g.reshape(NSA_HD, 1), (LANES // NSA_HD, LANES))


def _selection_tables(t):
    n_seg = t // CMP_STRIDE
    nb = t // SLC_BLOCK
    cst = np.arange(n_seg) * CMP_STRIDE
    sst = np.arange(nb) * SLC_BLOCK
    ovl = np.clip(np.minimum(cst[:, None] + CMP_BLOCK, sst[None] + SLC_BLOCK)
                  - np.maximum(cst[:, None], sst[None]), 0, None) / CMP_BLOCK
    ovl[(t - CMP_BLOCK) // CMP_STRIDE + 1:] = 0.0
    return (jnp.asarray(ovl.T, dtype=F32),)


def _layer(x, mod, pos_row, l, p, tables):
    bsz, t, _ = x.shape
    inv_t, mt = tables
    w_in_p = jnp.pad(p["w_in"][l], ((0, 0), (0, IN_COLS_PAD - IN_COLS))).astype(BF16)
    qg_t = _gain_t(p["q_norm_g"][l])
    kg_t = jnp.stack([_gain_t(p["k_norm_g"][l, br]) for br in range(N_BRANCH)])
    zhg, q_t, kc, vc, ks, vst, kw, vwt, gates_t = _inproj(
        x, mod, p["norm1_g"][l].reshape(1, D_MODEL), w_in_p, pos_row, inv_t, qg_t, kg_t, tm=512)

    o_hg = _hgrn(zhg, p["lb_logits"], p["hg_norm_g"][l].reshape(1, HG_HD), l, tb=512)

    n_seg = t // CMP_STRIDE
    seg_w = CMP_STRIDE * NSA_HD
    pe2 = p["pe_cmp"][l].reshape(2, 2, seg_w)
    kcmp, vcmp_t = _compress(kc.reshape(bsz, NSA_KV_HEADS, n_seg, seg_w),
                             vc.reshape(bsz, NSA_KV_HEADS, n_seg, seg_w),
                             pe2, p["w_cmp1"][l].astype(BF16), p["w_cmp2"][l].astype(BF16))
    o_nsa = _nsa(q_t, kcmp, vcmp_t, ks, vst, kw, vwt, gates_t, mt, tq=128, tks=512)

    return _ffn(x, o_hg, o_nsa, mod, p["w_out"][l].astype(BF16), p["norm2_g"][l].reshape(1, D_MODEL),
                p["w_up"][l].astype(BF16), p["conv_w"][l], p["conv_b"][l].reshape(1, 2 * D_FF),
                p["w_down"][l].astype(BF16), tm=512, fc=256)


def kernel(x, c, positions, w_ada, b_ada, norm1_g, w_in, lb_logits, hg_norm_g, q_norm_g, k_norm_g, pe_cmp, w_cmp1, w_cmp2, w_out, norm2_g, w_up, conv_w, conv_b, w_down):
    p = dict(w_in=w_in, norm1_g=norm1_g, lb_logits=lb_logits, hg_norm_g=hg_norm_g, q_norm_g=q_norm_g,
             k_norm_g=k_norm_g, pe_cmp=pe_cmp, w_cmp1=w_cmp1, w_cmp2=w_cmp2, w_out=w_out,
             norm2_g=norm2_g, w_up=w_up, conv_w=conv_w, conv_b=conv_b, w_down=w_down)
    bsz, t, _ = x.shape
    tables = _rope_tables() + _selection_tables(t)
    pos_row = positions.reshape(bsz, 1, t)
    for l in range(w_ada.shape[0]):
        mod = _ada(c, w_ada[l], b_ada[l]).reshape(bsz, 6, D_MODEL)
        x = _layer(x, mod, pos_row, l, p, tables)
    return x
```

```python
import functools

import jax
import jax.numpy as jnp
import numpy as np
from jax import lax
from jax.experimental import pallas as pl
from jax.experimental.pallas import tpu as pltpu

D_MODEL = 1024
HG_HEADS = 4
HG_HD = 128
HG_WIDTH = HG_HEADS * HG_HD
HG_CHUNK = 64
HG_SUB = 8
LOG2E = 1.4426950408889634
NSA_HEADS = 8
NSA_KV_HEADS = 2
NSA_HD = 64
NSA_GROUP = NSA_HEADS // NSA_KV_HEADS
NSA_CHAIN = 4
NSA_WIDTH = NSA_HEADS * NSA_HD
N_BRANCH = 3
CMP_BLOCK = 32
CMP_STRIDE = 16
CMP_HIDDEN = 256
SLC_BLOCK = 64
SLC_TOPK = 16
WIN = 512
ROPE_DIM = NSA_HD // 4
ROPE_THETA = 500000.0
D_FF = 2816
CONV_W = 3
EPS = 1e-6
NEG = -1e30
SCORE_BOUND_MARGIN = 1.02
MAX_SCORE_BOUND = 40.0

LANES = 128
SUBLANES = 8
VMEM_LIMIT = 56 * 1024 * 1024

OFF_HG = 0
OFF_Q = 4 * HG_WIDTH
OFF_KV = OFF_Q + NSA_WIDTH
OFF_G = OFF_KV + 6 * NSA_KV_HEADS * NSA_HD
IN_COLS = OFF_G + N_BRANCH * NSA_HEADS
IN_COLS_PAD = OFF_G + LANES
GATE_ROWS = 32

BF16 = jnp.bfloat16
F32 = jnp.float32


def _dot(a, b):
    return jnp.dot(a, b, preferred_element_type=F32)


def _dot_nt(a, b):
    return lax.dot_general(a, b, (((1,), (1,)), ((), ())), preferred_element_type=F32)


def _sigmoid(x):
    return 1.0 / (1.0 + jnp.exp(-x))


def _silu(x):
    return x * _sigmoid(x)


def _ada_kernel(c_ref, w_ref, b_ref, o_ref):
    cs = _silu(c_ref[...])
    o_ref[...] = jnp.dot(cs, w_ref[...], preferred_element_type=F32,
                         precision=lax.Precision.HIGHEST) + b_ref[...]


def _ada(c, w, b):
    bsz = c.shape[0]
    n = w.shape[1]
    tn = D_MODEL
    return pl.pallas_call(
        _ada_kernel,
        grid=(n // tn,),
        in_specs=[pl.BlockSpec((bsz, D_MODEL), lambda j: (0, 0)),
                  pl.BlockSpec((D_MODEL, tn), lambda j: (0, j)),
                  pl.BlockSpec((1, tn), lambda j: (0, j))],
        out_specs=pl.BlockSpec((bsz, tn), lambda j: (0, j)),
        out_shape=jax.ShapeDtypeStruct((bsz, n), F32),
        name="ada",
    )(c, w, b.reshape(1, n))


def _pair_norm_rope_t(xt, g_t, cos_t, sin_t):
    half = ROPE_DIM // 2
    outs = []
    for hh in range(2):
        x = xt[hh * NSA_HD:(hh + 1) * NSA_HD]
        ms = jnp.mean(x * x, axis=0, keepdims=True)
        xn = x * lax.rsqrt(ms + EPS) * g_t[hh * NSA_HD:(hh + 1) * NSA_HD]
        x1, x2 = xn[:half], xn[half:ROPE_DIM]
        outs += [x1 * cos_t - x2 * sin_t, x2 * cos_t + x1 * sin_t, xn[ROPE_DIM:]]
    return jnp.concatenate(outs, axis=0)


def _inproj_kernel(x_ref, mod_ref, g1_ref, w_ref, pos_ref, inv_ref, qg_ref, kg_ref,
                   zhg_ref, qt_ref, kc_ref, vc_ref, ks_ref, vst_ref, kw_ref, vwt_ref, gt_ref):
    x = x_ref[0]
    ms = jnp.mean(x * x, axis=-1, keepdims=True)
    y = x * lax.rsqrt(ms + EPS) * g1_ref[...]
    h = (y * (1.0 + mod_ref[0, 1:2, :]) + mod_ref[0, 0:1, :]).astype(BF16)

    zhg_ref[0] = _dot(h, w_ref[:, OFF_HG:OFF_Q])

    tm = x.shape[0]
    reps = tm // LANES

    def lane_tile(a):
        return jnp.concatenate([a] * reps, axis=1)

    ang = lane_tile(inv_ref[...]) * pos_ref[0].astype(F32)
    cos_t = jnp.cos(ang)
    sin_t = jnp.sin(ang)

    zq = _dot(h, w_ref[:, OFF_Q:OFF_KV])
    scale = NSA_HD ** -0.5
    qg_t = lane_tile(qg_ref[...])
    for p in range(NSA_HEADS // 2):
        rt = _pair_norm_rope_t(zq[:, p * LANES:(p + 1) * LANES].T, qg_t, cos_t, sin_t)
        rt = (rt * scale).astype(qt_ref.dtype)
        qt_ref[0, 2 * p] = rt[:NSA_HD]
        qt_ref[0, 2 * p + 1] = rt[NSA_HD:]

    zkv = _dot(h, w_ref[:, OFF_KV:OFF_G])
    k_outs = (kc_ref, ks_ref, kw_ref)
    for br in range(N_BRANCH):
        kt = _pair_norm_rope_t(zkv[:, (2 * br) * LANES:(2 * br + 1) * LANES].T,
                               lane_tile(kg_ref[br]), cos_t, sin_t)
        kk = kt.T.astype(k_outs[br].dtype)
        for g in range(NSA_KV_HEADS):
            k_outs[br][0, g] = kk[:, g * NSA_HD:(g + 1) * NSA_HD]
    vc = zkv[:, LANES:2 * LANES]
    for g in range(NSA_KV_HEADS):
        vc_ref[0, g] = vc[:, g * NSA_HD:(g + 1) * NSA_HD]
    for br, vt_ref in ((1, vst_ref), (2, vwt_ref)):
        vt = zkv[:, (2 * br + 1) * LANES:(2 * br + 2) * LANES].T.astype(vt_ref.dtype)
        for g in range(NSA_KV_HEADS):
            vt_ref[0, g] = vt[g * NSA_HD:(g + 1) * NSA_HD]

    gates = _sigmoid(_dot(h, w_ref[:, OFF_G:IN_COLS_PAD]))
    gt_ref[0] = gates.T[:GATE_ROWS]


def _inproj(x, mod, g1, w_in_p, pos_row, inv_t, qg_t, kg_t, tm):
    bsz, t, _ = x.shape
    grid = (bsz, t // tm)
    kv_shape = (bsz, NSA_KV_HEADS, t, NSA_HD)
    kv_spec = pl.BlockSpec((1, NSA_KV_HEADS, tm, NSA_HD), lambda b, i: (b, 0, i, 0))
    vt_shape = (bsz, NSA_KV_HEADS, NSA_HD, t)
    vt_spec = pl.BlockSpec((1, NSA_KV_HEADS, NSA_HD, tm), lambda b, i: (b, 0, 0, i))
    const = lambda b, i: (0, 0)
    return pl.pallas_call(
        _inproj_kernel,
        grid=grid,
        in_specs=[pl.BlockSpec((1, tm, D_MODEL), lambda b, i: (b, i, 0)),
                  pl.BlockSpec((1, 6, D_MODEL), lambda b, i: (b, 0, 0)),
                  pl.BlockSpec((1, D_MODEL), const),
                  pl.BlockSpec((D_MODEL, IN_COLS_PAD), const),
                  pl.BlockSpec((1, 1, tm), lambda b, i: (b, 0, i)),
                  pl.BlockSpec(inv_t.shape, const),
                  pl.BlockSpec(qg_t.shape, const),
                  pl.BlockSpec(kg_t.shape, lambda b, i: (0, 0, 0))],
        out_specs=[pl.BlockSpec((1, tm, 4 * HG_WIDTH), lambda b, i: (b, i, 0)),
                   pl.BlockSpec((1, NSA_HEADS, NSA_HD, tm), lambda b, i: (b, 0, 0, i)),
                   kv_spec, kv_spec, kv_spec, vt_spec, kv_spec, vt_spec,
                   pl.BlockSpec((1, GATE_ROWS, tm), lambda b, i: (b, 0, i))],
        out_shape=[jax.ShapeDtypeStruct((bsz, t, 4 * HG_WIDTH), F32),
                   jax.ShapeDtypeStruct((bsz, NSA_HEADS, NSA_HD, t), BF16),
                   jax.ShapeDtypeStruct(kv_shape, F32),
                   jax.ShapeDtypeStruct(kv_shape, F32),
                   jax.ShapeDtypeStruct(kv_shape, BF16),
                   jax.ShapeDtypeStruct(vt_shape, BF16),
                   jax.ShapeDtypeStruct(kv_shape, BF16),
                   jax.ShapeDtypeStruct(vt_shape, BF16),
                   jax.ShapeDtypeStruct((bsz, GATE_ROWS, t), F32)],
        compiler_params=pltpu.CompilerParams(
            dimension_semantics=("arbitrary", "arbitrary"), vmem_limit_bytes=VMEM_LIMIT),
        name="inproj",
    )(x, mod, g1, w_in_p, pos_row, inv_t, qg_t, kg_t)


def _hgrn_chunk(zq, zf, zi, zg, lb, hg_g, st_ref, h, tri, level):
    c = HG_CHUNK
    e_z = jnp.exp(-jnp.abs(zf))
    logsig = jnp.minimum(zf, 0.0) - jnp.log(1.0 + e_z)
    a = jnp.log(lb)
    bb = jnp.log1p(-lb) + logsig
    logf = jnp.maximum(a, bb) + jnp.log(1.0 + jnp.exp(-jnp.abs(a - bb)))
    k = (1.0 - lb) * (jnp.where(zf >= 0.0, e_z, 1.0) / (1.0 + e_z))
    q = _silu(zq)
    v = zi
    l_hi = logf.astype(BF16)
    l_mid = (logf - l_hi.astype(F32)).astype(BF16)
    l_lo = (logf - l_hi.astype(F32) - l_mid.astype(F32)).astype(BF16)
    bc = _dot(tri, l_hi) + _dot(tri, l_mid) + _dot(tri, l_lo)

    col = lax.broadcasted_iota(jnp.int32, (HG_SUB, c), 1)
    b2 = bc * LOG2E
    rows_a = []
    for i in range(c // HG_SUB):
        lo = i * HG_SUB
        b_i = b2[lo:lo + HG_SUB]
        q_i = q[lo:lo + HG_SUB]
        k_i = k[lo:lo + HG_SUB]
        a_i = jnp.zeros((HG_SUB, c), F32)
        for s in range(HG_SUB):
            e = jnp.exp2(jnp.minimum(b_i - b_i[s:s + 1], 0.0)) * q_i * k_i[s:s + 1]
            a_i = jnp.where(col == lo + s, jnp.sum(e, axis=-1, keepdims=True), a_i)
        rows_a.append(a_i)
    amat = jnp.concatenate(rows_a, axis=0)

    size, idx = c // 2, 1
    while size >= HG_SUB:
        pieces = []
        for e0 in range(0, c, 2 * size):
            o0 = e0 + size
            r = bc[o0:o0 + 1]
            pieces.append(k[e0:o0] * jnp.exp(r - bc[e0:o0]))
            pieces.append(q[o0:o0 + size] * jnp.exp(bc[o0:o0 + size] - r))
        hmat = jnp.concatenate(pieces, axis=0).astype(BF16)
        amat = jnp.where(level == idx, _dot_nt(hmat, hmat), amat)
        size, idx = size // 2, idx + 1
    amat = jnp.where(level >= 0, amat, 0.0)

    st = st_ref[h]
    o = _dot_nt((q * jnp.exp(bc)).astype(BF16), st.astype(BF16)) + _dot(amat.astype(BF16), v.astype(BF16))
    bl = bc[c - 1:c]
    kdec = (k * jnp.exp(bl - bc)).astype(BF16)
    st_ref[h] = jnp.exp(bl) * st + _dot(v.T.astype(BF16), kdec)

    y = o * lax.rsqrt(jnp.mean(o * o, axis=-1, keepdims=True) + EPS) * hg_g
    return y * _silu(zg)


def _hgrn_kernel(z_ref, lbl_ref, g_ref, o_ref, st_ref, *, l_idx):
    @pl.when(pl.program_id(1) == 0)
    def _():
        st_ref[...] = jnp.zeros_like(st_ref)

    lg = lbl_ref[...]
    ex = jnp.exp(lg - jnp.max(lg, axis=0, keepdims=True))
    sm = ex / jnp.sum(ex, axis=0, keepdims=True)
    lb_all = jnp.sum(sm[:l_idx + 1], axis=0, keepdims=True)

    c = HG_CHUNK
    ri = lax.broadcasted_iota(jnp.int32, (c, c), 0)
    ci = lax.broadcasted_iota(jnp.int32, (c, c), 1)
    tri = jnp.where(ci <= ri, 1.0, 0.0).astype(BF16)
    level = jnp.where(ci > ri, -1, 0)
    size, idx = c // 2, 1
    while size >= HG_SUB:
        sh = size.bit_length() - 1
        paired = ((ri >> (sh + 1)) == (ci >> (sh + 1))) & ((ri >> sh) != (ci >> sh)) & (ci <= ri)
        level = jnp.where(paired, idx, level)
        size, idx = size // 2, idx + 1
    n_chunks = z_ref.shape[1] // c

    def body(ch, carry):
        r0 = pl.multiple_of(ch * c, c)
        for h in range(HG_HEADS):
            sl = slice(h * HG_HD, (h + 1) * HG_HD)
            zq = z_ref[0, pl.ds(r0, c), h * HG_HD:(h + 1) * HG_HD]
            zf = z_ref[0, pl.ds(r0, c), HG_WIDTH + h * HG_HD:HG_WIDTH + (h + 1) * HG_HD]
            zi = z_ref[0, pl.ds(r0, c), 2 * HG_WIDTH + h * HG_HD:2 * HG_WIDTH + (h + 1) * HG_HD]
            zg = z_ref[0, pl.ds(r0, c), 3 * HG_WIDTH + h * HG_HD:3 * HG_WIDTH + (h + 1) * HG_HD]
            y = _hgrn_chunk(zq, zf, zi, zg, lb_all[:, sl], g_ref[...], st_ref, h, tri, level)
            o_ref[0, pl.ds(r0, c), h * HG_HD:(h + 1) * HG_HD] = y.astype(o_ref.dtype)
        return carry

    lax.fori_loop(0, n_chunks, body, 0, unroll=2)


def _hgrn(zhg, lb_logits, hg_g, l_idx, tb):
    bsz, t, _ = zhg.shape
    return pl.pallas_call(
        functools.partial(_hgrn_kernel, l_idx=l_idx),
        grid=(bsz, t // tb),
        in_specs=[pl.BlockSpec((1, tb, 4 * HG_WIDTH), lambda b, i: (b, i, 0)),
                  pl.BlockSpec(lb_logits.shape, lambda b, i: (0, 0)),
                  pl.BlockSpec((1, HG_HD), lambda b, i: (0, 0))],
        out_specs=pl.BlockSpec((1, tb, HG_WIDTH), lambda b, i: (b, i, 0)),
        out_shape=jax.ShapeDtypeStruct((bsz, t, HG_WIDTH), BF16),
        scratch_shapes=[pltpu.VMEM((HG_HEADS, HG_HD, HG_HD), F32)],
        compiler_params=pltpu.CompilerParams(
            dimension_semantics=("arbitrary", "arbitrary"), vmem_limit_bytes=VMEM_LIMIT),
        name="hgrn",
    )(zhg, lb_logits, hg_g)


def _compress_kernel(xk_ref, xv_ref, pe_ref, w1_ref, w2_ref, ko_ref, vo_ref):
    half = CMP_STRIDE * NSA_HD
    outs = []
    for kv, x_ref in enumerate((xk_ref, xv_ref)):
        x = x_ref[0, 0]
        ha = _dot((x + pe_ref[kv, 0:1, :]).astype(BF16), w1_ref[kv, :half, :])
        hb = _dot((x + pe_ref[kv, 1:2, :]).astype(BF16), w1_ref[kv, half:, :])
        n = x.shape[0]
        pre = ha + pltpu.roll(hb, n - 1, axis=0)
        outs.append(_dot(_silu(pre).astype(BF16), w2_ref[kv]))
    ko_ref[0, 0] = outs[0].astype(ko_ref.dtype)
    vo_ref[0, 0] = outs[1].T.astype(vo_ref.dtype)


def _compress(xk, xv, pe2, w1, w2):
    bsz, g, nseg, width = xk.shape
    x_spec = pl.BlockSpec((1, 1, nseg, width), lambda b, j: (b, j, 0, 0))
    return pl.pallas_call(
        _compress_kernel,
        grid=(bsz, g),
        in_specs=[x_spec, x_spec,
                  pl.BlockSpec(pe2.shape, lambda b, j: (0, 0, 0)),
                  pl.BlockSpec(w1.shape, lambda b, j: (0, 0, 0)),
                  pl.BlockSpec(w2.shape, lambda b, j: (0, 0, 0))],
        out_specs=[pl.BlockSpec((1, 1, nseg, NSA_HD), lambda b, j: (b, j, 0, 0)),
                   pl.BlockSpec((1, 1, NSA_HD, nseg), lambda b, j: (b, j, 0, 0))],
        out_shape=[jax.ShapeDtypeStruct((bsz, g, nseg, NSA_HD), BF16),
                   jax.ShapeDtypeStruct((bsz, g, NSA_HD, nseg), BF16)],
        compiler_params=pltpu.CompilerParams(
            dimension_semantics=("arbitrary", "arbitrary"), vmem_limit_bytes=VMEM_LIMIT),
        name="compress",
    )(xk, xv, pe2, w1, w2)


def _nsa_kernel(bound_ref, qt_ref, kc_ref, vct_ref, ks_ref, vst_ref, kw_ref, vwt_ref, gt_ref, mt_ref,
                o_ref, sel_ref, *, tq, tks, bounded):
    r = NSA_GROUP
    n_g = NSA_KV_HEADS
    ch = NSA_CHAIN
    lanes = ch * tq
    q0 = pl.program_id(1) * tq
    chains = [(g, g * r + c * ch) for g in range(n_g) for c in range(r // ch)]
    q_ts = [jnp.concatenate([qt_ref[0, h0 + i] for i in range(ch)], axis=1)
            for _, h0 in chains]

    def tile_heads(a):
        return jnp.concatenate([a] * ch, axis=1)

    n_blk = kc_ref.shape[2]
    nb = mt_ref.shape[0]
    blk_end = lax.broadcasted_iota(jnp.int32, (n_blk, tq), 0) * CMP_STRIDE + (CMP_BLOCK - 1)
    t_row = q0 + lax.broadcasted_iota(jnp.int32, (1, tq), 1)
    cvalid = tile_heads(jnp.where(blk_end <= t_row, 1.0, 0.0)) > 0.5
    some = tile_heads((t_row >= CMP_BLOCK - 1).astype(F32))
    j = lax.broadcasted_iota(jnp.int32, (nb, tq), 0)
    cur = jnp.right_shift(t_row, SLC_BLOCK.bit_length() - 1)
    forced = (j == 0) | (j == cur) | (j == cur - 1)
    o_cmp = []
    psum = [None] * n_g
    for (g, _), q_t in zip(chains, q_ts):
        s = jnp.where(cvalid, _dot(kc_ref[0, g], q_t), NEG)
        e = jnp.exp(s - jnp.max(s, axis=0, keepdims=True))
        p = e * (some / jnp.sum(e, axis=0, keepdims=True))
        o_cmp.append(_dot(vct_ref[0, g], p.astype(BF16)))
        for i in range(ch):
            part = p[:, i * tq:(i + 1) * tq]
            psum[g] = part if psum[g] is None else psum[g] + part
    for g in range(n_g):
        imp = jnp.dot(mt_ref[...], psum[g], preferred_element_type=F32, precision=lax.Precision.HIGHEST)
        imp = jnp.where(j <= cur, jnp.where(forced, jnp.inf, imp), -1.0)
        rank = jnp.zeros((nb, tq), jnp.int32)
        for i in range(nb):
            row_i = imp[i:i + 1, :]
            ahead = (row_i > imp) | ((row_i == imp) & (j > i))
            rank = rank + ahead.astype(jnp.int32)
        sel_ref[g] = jnp.where(rank < min(SLC_TOPK, nb), 0.0, NEG)

    blocks_per_tile = tks // SLC_BLOCK
    rel = (lax.broadcasted_iota(jnp.int32, (tks, tq), 1)
           - lax.broadcasted_iota(jnp.int32, (tks, tq), 0))

    shift = bound_ref[0] if bounded else None

    def slc_bias(it, k0):
        causal = rel + (q0 - k0) >= 0
        biases = []
        for g in range(n_g):
            rows = sel_ref[g, pl.ds(pl.multiple_of(it * blocks_per_tile, blocks_per_tile), blocks_per_tile), :]
            bias = jnp.concatenate([jnp.broadcast_to(rows[jj:jj + 1], (SLC_BLOCK, tq))
                                    for jj in range(blocks_per_tile)], axis=0)
            bias = jnp.where(causal, bias, NEG)
            biases.append(tile_heads(bias - shift if bounded else bias))
        return biases

    def slc_body(it, carry):
        k0 = pl.multiple_of(it * tks, tks)
        biases = slc_bias(it, k0)
        out = []
        for (g, _), q_t, state in zip(chains, q_ts, carry):
            sc = _dot(ks_ref[0, g, pl.ds(k0, tks), :], q_t) + biases[g]
            if bounded:
                l, acc = state
                pe = jnp.exp(sc)
                l = l + jnp.sum(pe, axis=0, keepdims=True)
                acc = acc + _dot(vst_ref[0, g, :, pl.ds(k0, tks)], pe.astype(BF16))
                out.append((l, acc))
            else:
                m, l, acc = state
                m_new = jnp.maximum(m, jnp.max(sc, axis=0, keepdims=True))
                alpha = jnp.exp(m - m_new)
                pe = jnp.exp(sc - m_new)
                l = alpha * l + jnp.sum(pe, axis=0, keepdims=True)
                acc = alpha * acc + _dot(vst_ref[0, g, :, pl.ds(k0, tks)], pe.astype(BF16))
                out.append((m_new, l, acc))
        return tuple(out)

    zero_state = (jnp.zeros((1, lanes), F32), jnp.zeros((NSA_HD, lanes), F32))
    init = tuple(zero_state if bounded else (jnp.full((1, lanes), NEG, F32),) + zero_state for _ in chains)
    n_it = (q0 + tq + tks - 1) // tks
    fin = lax.fori_loop(0, n_it, slc_body, init)
    o_slc = [st[-1] * (1.0 / st[-2]) for st in fin]

    span = WIN + tq
    start = pl.multiple_of(jnp.maximum(q0 - WIN, 0), tq)
    dist = (lax.broadcasted_iota(jnp.int32, (span, tq), 1)
            - lax.broadcasted_iota(jnp.int32, (span, tq), 0)) + (q0 - start)
    wbias = jnp.where((dist >= 0) & (dist < WIN), 0.0, NEG)
    wbias = tile_heads(wbias - shift if bounded else wbias)
    o_win = []
    for (g, _), q_t in zip(chains, q_ts):
        sw = _dot(kw_ref[0, g, pl.ds(start, span), :], q_t) + wbias
        ew = jnp.exp(sw if bounded else sw - jnp.max(sw, axis=0, keepdims=True))
        o_win.append(_dot(vwt_ref[0, g, :, pl.ds(start, span)], ew.astype(BF16))
                     * (1.0 / jnp.sum(ew, axis=0, keepdims=True)))

    for ci, (_, h0) in enumerate(chains):
        for i in range(ch):
            cols = slice(i * tq, (i + 1) * tq)
            head = h0 + i
            o_h = jnp.zeros((NSA_HD, tq), F32)
            for br, o_b in enumerate((o_cmp[ci], o_slc[ci], o_win[ci])):
                row = head * N_BRANCH + br
                o_h = o_h + gt_ref[0, row:row + 1, :] * o_b[:, cols]
            o_ref[0, :, head * NSA_HD:(head + 1) * NSA_HD] = o_h.T.astype(o_ref.dtype)


def _nsa(q_t, kc, vct, ks, vst, kw, vwt, gates_t, mt, bound, tq, tks, bounded):
    bsz, _, _, t = q_t.shape
    n_blk = kc.shape[2]
    n_g = NSA_KV_HEADS
    full = lambda b, i: (b, 0, 0, 0)
    k_spec = pl.BlockSpec((1, n_g, t, NSA_HD), full)
    vt_spec = pl.BlockSpec((1, n_g, NSA_HD, t), full)
    return pl.pallas_call(
        functools.partial(_nsa_kernel, tq=tq, tks=tks, bounded=bounded),
        grid=(bsz, t // tq),
        in_specs=[pl.BlockSpec(memory_space=pltpu.SMEM),
                  pl.BlockSpec((1, NSA_HEADS, NSA_HD, tq), lambda b, i: (b, 0, 0, i)),
                  pl.BlockSpec((1, n_g, n_blk, NSA_HD), full),
                  pl.BlockSpec((1, n_g, NSA_HD, n_blk), full),
                  k_spec, vt_spec, k_spec, vt_spec,
                  pl.BlockSpec((1, GATE_ROWS, tq), lambda b, i: (b, 0, i)),
                  pl.BlockSpec(mt.shape, lambda b, i: (0, 0))],
        out_specs=pl.BlockSpec((1, tq, NSA_WIDTH), lambda b, i: (b, i, 0)),
        out_shape=jax.ShapeDtypeStruct((bsz, t, NSA_WIDTH), BF16),
        scratch_shapes=[pltpu.VMEM((n_g, mt.shape[0], tq), F32)],
        compiler_params=pltpu.CompilerParams(
            dimension_semantics=("arbitrary", "arbitrary"), vmem_limit_bytes=VMEM_LIMIT),
        name="nsa_bounded" if bounded else "nsa",
    )(bound, q_t, kc, vct, ks, vst, kw, vwt, gates_t, mt)


def _causal_conv(u, prev, cw, cb):
    tm = u.shape[0]
    g = SUBLANES
    r8 = lax.broadcasted_iota(jnp.int32, (g, u.shape[1]), 0)
    wrap1 = jnp.where(r8 == 0, prev[2 * g - 1:2 * g], pltpu.roll(u[tm - g:], 1, axis=0))
    wrap2 = jnp.where(r8 == 0, prev[g - 1:g], pltpu.roll(u[tm - 2 * g:tm - g], 1, axis=0))
    u1 = jnp.concatenate([wrap1, u[:tm - g]], axis=0)
    u2 = jnp.concatenate([wrap2, wrap1, u[:tm - 2 * g]], axis=0)
    return cb + u2 * cw[0:1] + u1 * cw[1:2] + u * cw[2:3]


def _ffn_kernel(x_ref, hg_ref, ns_ref, mod_ref, wo_ref, g2_ref, wu_ref, cw_ref, cb_ref, wd_ref,
                o_ref, carry_ref, buf_ref, h2_ref, u_ref, g_ref, acc_ref, *, fc):
    mix = _dot(hg_ref[0], wo_ref[:HG_WIDTH, :]) + _dot(ns_ref[0], wo_ref[HG_WIDTH:, :])
    x1_nat = x_ref[0] + mod_ref[0, 2:3, :] * mix
    n_lt = D_MODEL // LANES
    tm = x1_nat.shape[0]
    groups = tm // SUBLANES
    pitch = buf_ref.shape[1] // SUBLANES
    for c in range(n_lt):
        for sg in range(SUBLANES):
            buf_ref[c, sg * pitch:sg * pitch + groups] = x1_nat[sg * groups:(sg + 1) * groups,
                                                                c * LANES:(c + 1) * LANES]

    x1 = jnp.concatenate(
        [jnp.concatenate([buf_ref[c, pl.ds(j, SUBLANES, stride=pitch), :] for j in range(groups)], axis=0)
         for c in range(n_lt)], axis=1)
    y = x1 * lax.rsqrt(jnp.mean(x1 * x1, axis=-1, keepdims=True) + EPS) * g2_ref[...]
    h2_ref[...] = (y * (1.0 + mod_ref[0, 4:5, :]) + mod_ref[0, 3:4, :]).astype(h2_ref.dtype)
    acc_ref[...] = jnp.zeros_like(acc_ref)
    first = pl.program_id(1) == 0

    def up(jc, slot):
        for half in range(2):
            off = pl.multiple_of(half * D_FF + jc * fc, fc)
            u_ref[slot, half] = _dot(h2_ref[...], wu_ref[:, pl.ds(off, fc)])

    def act(jc, slot):
        halves = []
        for half in range(2):
            cols = pl.ds(pl.multiple_of(half * D_FF + jc * fc, fc), fc)
            u = u_ref[slot, half]
            prev = jnp.where(first, 0.0, carry_ref[:, cols])
            carry_ref[:, cols] = u[tm - 2 * SUBLANES:]
            halves.append(_causal_conv(u, prev, cw_ref[:, cols], cb_ref[:, cols]))
        a, v = halves
        g_ref[slot] = (_silu(a) * v).astype(g_ref.dtype)

    def down(jc, slot):
        acc_ref[...] += _dot(g_ref[slot], wd_ref[pl.ds(pl.multiple_of(jc * fc, fc), fc), :])

    n = D_FF // fc

    def tick(i):
        if i < n:
            up(i, i % 2)
        if 0 <= i - 1 < n:
            act(i - 1, (i - 1) % 2)
        if 0 <= i - 2 < n:
            down(i - 2, i % 2)

    tick(0)
    tick(1)
    pairs = (n - 2) // 2

    def body(k, carry):
        i = 2 + 2 * k
        up(i, 0)
        act(i - 1, 1)
        down(i - 2, 0)
        up(i + 1, 1)
        act(i, 0)
        down(i - 1, 1)
        return carry

    lax.fori_loop(0, pairs, body, 0)
    for i in range(2 + 2 * pairs, n + 2):
        tick(i)

    out = x1 + mod_ref[0, 5:6, :] * acc_ref[...]
    for c in range(n_lt):
        for j in range(groups):
            buf_ref[c, pl.ds(j, SUBLANES, stride=pitch), :] = out[j * SUBLANES:(j + 1) * SUBLANES,
                                                                  c * LANES:(c + 1) * LANES]
    for c in range(n_lt):
        for sg in range(SUBLANES):
            o_ref[0, sg * groups:(sg + 1) * groups, c * LANES:(c + 1) * LANES] = (
                buf_ref[c, sg * pitch:sg * pitch + groups])


def _ffn(x, o_hg, o_nsa, mod, w_out, g2, w_up, conv_w, conv_b, w_down, tm, fc):
    bsz, t, _ = x.shape
    row_spec = lambda w: pl.BlockSpec((1, tm, w), lambda b, i: (b, i, 0))
    groups = tm // SUBLANES
    pad = SUBLANES if (groups // SUBLANES) % 2 == 0 else 0
    resident = lambda a: pl.BlockSpec(a.shape, lambda b, i: (0, 0), pipeline_mode=pl.Buffered(1))
    return pl.pallas_call(
        functools.partial(_ffn_kernel, fc=fc),
        grid=(bsz, t // tm),
        in_specs=[row_spec(D_MODEL), row_spec(HG_WIDTH), row_spec(NSA_WIDTH),
                  pl.BlockSpec((1, 6, D_MODEL), lambda b, i: (b, 0, 0)),
                  resident(w_out), resident(g2), resident(w_up), resident(conv_w), resident(conv_b),
                  resident(w_down)],
        out_specs=row_spec(D_MODEL),
        out_shape=jax.ShapeDtypeStruct(x.shape, F32),
        scratch_shapes=[pltpu.VMEM((2 * SUBLANES, 2 * D_FF), F32),
                        pltpu.VMEM((D_MODEL // LANES, tm + SUBLANES * pad, LANES), F32),
                        pltpu.VMEM((tm, D_MODEL), BF16),
                        pltpu.VMEM((2, 2, tm, fc), F32),
                        pltpu.VMEM((2, tm, fc), BF16),
                        pltpu.VMEM((tm, D_MODEL), F32)],
        compiler_params=pltpu.CompilerParams(
            dimension_semantics=("arbitrary", "arbitrary"), vmem_limit_bytes=VMEM_LIMIT),
        name="ffn",
    )(x, o_hg, o_nsa, mod, w_out, g2, w_up, conv_w, conv_b, w_down)


def _rope_tables():
    half = ROPE_DIM // 2
    inv = ROPE_THETA ** (-jnp.arange(half, dtype=F32) * 2.0 / ROPE_DIM)
    return (jnp.tile(inv.reshape(half, 1), (1, LANES)),)


def _gain_t(g):
    return jnp.tile(g.reshape(NSA_HD, 1), (LANES // NSA_HD, LANES))


def _selection_tables(t):
    n_seg = t // CMP_STRIDE
    nb = t // SLC_BLOCK
    cst = np.arange(n_seg) * CMP_STRIDE
    sst = np.arange(nb) * SLC_BLOCK
    ovl = np.clip(np.minimum(cst[:, None] + CMP_BLOCK, sst[None] + SLC_BLOCK)
                  - np.maximum(cst[:, None], sst[None]), 0, None) / CMP_BLOCK
    ovl[(t - CMP_BLOCK) // CMP_STRIDE + 1:] = 0.0
    return (jnp.asarray(ovl.T, dtype=F32),)


def _layer(x, mod, pos_row, l, p, tables):
    bsz, t, _ = x.shape
    inv_t, mt = tables
    w_in_p = jnp.pad(p["w_in"][l], ((0, 0), (0, IN_COLS_PAD - IN_COLS))).astype(BF16)
    qg_t = _gain_t(p["q_norm_g"][l])
    kg_t = jnp.stack([_gain_t(p["k_norm_g"][l, br]) for br in range(N_BRANCH)])
    zhg, q_t, kc, vc, ks, vst, kw, vwt, gates_t = _inproj(
        x, mod, p["norm1_g"][l].reshape(1, D_MODEL), w_in_p, pos_row, inv_t, qg_t, kg_t, tm=512)

    o_hg = _hgrn(zhg, p["lb_logits"], p["hg_norm_g"][l].reshape(1, HG_HD), l, tb=512)

    n_seg = t // CMP_STRIDE
    seg_w = CMP_STRIDE * NSA_HD
    pe2 = p["pe_cmp"][l].reshape(2, 2, seg_w)
    kcmp, vcmp_t = _compress(kc.reshape(bsz, NSA_KV_HEADS, n_seg, seg_w),
                             vc.reshape(bsz, NSA_KV_HEADS, n_seg, seg_w),
                             pe2, p["w_cmp1"][l].astype(BF16), p["w_cmp2"][l].astype(BF16))
    bound = (SCORE_BOUND_MARGIN * NSA_HD ** 0.5 * jnp.max(jnp.abs(p["q_norm_g"][l]))
             * jnp.max(jnp.abs(p["k_norm_g"][l, 1:]))).reshape(1).astype(F32)
    nsa_args = (q_t, kcmp, vcmp_t, ks, vst, kw, vwt, gates_t, mt, bound)
    o_nsa = lax.cond(bound[0] <= MAX_SCORE_BOUND,
                     lambda a: _nsa(*a, tq=128, tks=512, bounded=True),
                     lambda a: _nsa(*a, tq=128, tks=512, bounded=False), nsa_args)

    return _ffn(x, o_hg, o_nsa, mod, p["w_out"][l].astype(BF16), p["norm2_g"][l].reshape(1, D_MODEL),
                p["w_up"][l].astype(BF16), p["conv_w"][l], p["conv_b"][l].reshape(1, 2 * D_FF),
                p["w_down"][l].astype(BF16), tm=512, fc=256)


def kernel(x, c, positions, w_ada, b_ada, norm1_g, w_in, lb_logits, hg_norm_g, q_norm_g, k_norm_g, pe_cmp, w_cmp1, w_cmp2, w_out, norm2_g, w_up, conv_w, conv_b, w_down):
    p = dict(w_in=w_in, norm1_g=norm1_g, lb_logits=lb_logits, hg_norm_g=hg_norm_g, q_norm_g=q_norm_g,
             k_norm_g=k_norm_g, pe_cmp=pe_cmp, w_cmp1=w_cmp1, w_cmp2=w_cmp2, w_out=w_out,
             norm2_g=norm2_g, w_up=w_up, conv_w=conv_w, conv_b=conv_b, w_down=w_down)
    bsz, t, _ = x.shape
    tables = _rope_tables() + _selection_tables(t)
    pos_row = positions.reshape(bsz, 1, t)
    for l in range(w_ada.shape[0]):
        mod = _ada(c, w_ada[l], b_ada[l]).reshape(bsz, 6, D_MODEL)
        x = _layer(x, mod, pos_row, l, p, tables)
    return x
```

```python
import functools

import jax
import jax.numpy as jnp
import numpy as np
from jax import lax
from jax.experimental import pallas as pl
from jax.experimental.pallas import tpu as pltpu

D_MODEL = 1024
HG_HEADS = 4
HG_HD = 128
HG_WIDTH = HG_HEADS * HG_HD
HG_CHUNK = 64
HG_SUB = 8
LOG2E = 1.4426950408889634
NSA_HEADS = 8
NSA_KV_HEADS = 2
NSA_HD = 64
NSA_GROUP = NSA_HEADS // NSA_KV_HEADS
NSA_CHAIN = 4
NSA_WIDTH = NSA_HEADS * NSA_HD
N_BRANCH = 3
CMP_BLOCK = 32
CMP_STRIDE = 16
CMP_HIDDEN = 256
SLC_BLOCK = 64
SLC_TOPK = 16
WIN = 512
ROPE_DIM = NSA_HD // 4
ROPE_THETA = 500000.0
D_FF = 2816
CONV_W = 3
EPS = 1e-6
NEG = -1e30
SCORE_BOUND_MARGIN = 1.02
MAX_SCORE_BOUND = 40.0

LANES = 128
SUBLANES = 8
VMEM_LIMIT = 56 * 1024 * 1024

OFF_HG = 0
OFF_Q = 4 * HG_WIDTH
OFF_KV = OFF_Q + NSA_WIDTH
OFF_G = OFF_KV + 6 * NSA_KV_HEADS * NSA_HD
IN_COLS = OFF_G + N_BRANCH * NSA_HEADS
IN_COLS_PAD = OFF_G + LANES
GATE_ROWS = 32
VT_ROWS = NSA_HD + SUBLANES

BF16 = jnp.bfloat16
F32 = jnp.float32


def _dot(a, b):
    return jnp.dot(a, b, preferred_element_type=F32)


def _dot_nt(a, b):
    return lax.dot_general(a, b, (((1,), (1,)), ((), ())), preferred_element_type=F32)


def _sigmoid(x):
    return 1.0 / (1.0 + jnp.exp(-x))


def _silu(x):
    return x * _sigmoid(x)


def _ada_kernel(c_ref, w_ref, b_ref, o_ref):
    cs = _silu(c_ref[...])
    o_ref[...] = jnp.dot(cs, w_ref[...], preferred_element_type=F32,
                         precision=lax.Precision.HIGHEST) + b_ref[...]


def _ada(c, w, b):
    bsz = c.shape[0]
    n = w.shape[1]
    tn = D_MODEL
    return pl.pallas_call(
        _ada_kernel,
        grid=(n // tn,),
        in_specs=[pl.BlockSpec((bsz, D_MODEL), lambda j: (0, 0)),
                  pl.BlockSpec((D_MODEL, tn), lambda j: (0, j)),
                  pl.BlockSpec((1, tn), lambda j: (0, j))],
        out_specs=pl.BlockSpec((bsz, tn), lambda j: (0, j)),
        out_shape=jax.ShapeDtypeStruct((bsz, n), F32),
        name="ada",
    )(c, w, b.reshape(1, n))


def _pair_norm_rope_t(xt, g_t, cos_t, sin_t):
    half = ROPE_DIM // 2
    outs = []
    for hh in range(2):
        x = xt[hh * NSA_HD:(hh + 1) * NSA_HD]
        ms = jnp.mean(x * x, axis=0, keepdims=True)
        xn = x * lax.rsqrt(ms + EPS) * g_t[hh * NSA_HD:(hh + 1) * NSA_HD]
        x1, x2 = xn[:half], xn[half:ROPE_DIM]
        outs += [x1 * cos_t - x2 * sin_t, x2 * cos_t + x1 * sin_t, xn[ROPE_DIM:]]
    return jnp.concatenate(outs, axis=0)


def _inproj_kernel(x_ref, mod_ref, g1_ref, w_ref, pos_ref, inv_ref, qg_ref, kg_ref,
                   zhg_ref, qt_ref, kc_ref, vc_ref, ks_ref, vst_ref, kw_ref, vwt_ref, gt_ref):
    x = x_ref[0]
    ms = jnp.mean(x * x, axis=-1, keepdims=True)
    y = x * lax.rsqrt(ms + EPS) * g1_ref[...]
    h = (y * (1.0 + mod_ref[0, 1:2, :]) + mod_ref[0, 0:1, :]).astype(BF16)

    zhg_ref[0] = _dot(h, w_ref[:, OFF_HG:OFF_Q])

    tm = x.shape[0]
    reps = tm // LANES

    def lane_tile(a):
        return jnp.concatenate([a] * reps, axis=1)

    ang = lane_tile(inv_ref[...]) * pos_ref[0].astype(F32)
    cos_t = jnp.cos(ang)
    sin_t = jnp.sin(ang)

    zq = _dot(h, w_ref[:, OFF_Q:OFF_KV])
    scale = NSA_HD ** -0.5 * LOG2E
    qg_t = lane_tile(qg_ref[...])
    for p in range(NSA_HEADS // 2):
        rt = _pair_norm_rope_t(zq[:, p * LANES:(p + 1) * LANES].T, qg_t, cos_t, sin_t)
        rt = (rt * scale).astype(qt_ref.dtype)
        qt_ref[0, 2 * p] = rt[:NSA_HD]
        qt_ref[0, 2 * p + 1] = rt[NSA_HD:]

    zkv = _dot(h, w_ref[:, OFF_KV:OFF_G])
    k_outs = (kc_ref, ks_ref, kw_ref)
    for br in range(N_BRANCH):
        kt = _pair_norm_rope_t(zkv[:, (2 * br) * LANES:(2 * br + 1) * LANES].T,
                               lane_tile(kg_ref[br]), cos_t, sin_t)
        kk = kt.T.astype(k_outs[br].dtype)
        for g in range(NSA_KV_HEADS):
            k_outs[br][0, g] = kk[:, g * NSA_HD:(g + 1) * NSA_HD]
    vc = zkv[:, LANES:2 * LANES]
    for g in range(NSA_KV_HEADS):
        vc_ref[0, g] = vc[:, g * NSA_HD:(g + 1) * NSA_HD]
    for br, vt_ref in ((1, vst_ref), (2, vwt_ref)):
        vt = zkv[:, (2 * br + 1) * LANES:(2 * br + 2) * LANES].T.astype(vt_ref.dtype)
        ones_row = jnp.where(lax.broadcasted_iota(jnp.int32, (SUBLANES, tm), 0) == 0, 1.0, 0.0).astype(vt_ref.dtype)
        for g in range(NSA_KV_HEADS):
            vt_ref[0, g, :NSA_HD] = vt[g * NSA_HD:(g + 1) * NSA_HD]
            vt_ref[0, g, NSA_HD:] = ones_row

    gates = _sigmoid(_dot(h, w_ref[:, OFF_G:IN_COLS_PAD]))
    gt_ref[0] = gates.T[:GATE_ROWS]


def _inproj(x, mod, g1, w_in_p, pos_row, inv_t, qg_t, kg_t, tm):
    bsz, t, _ = x.shape
    grid = (bsz, t // tm)
    kv_shape = (bsz, NSA_KV_HEADS, t, NSA_HD)
    kv_spec = pl.BlockSpec((1, NSA_KV_HEADS, tm, NSA_HD), lambda b, i: (b, 0, i, 0))
    vt_shape = (bsz, NSA_KV_HEADS, VT_ROWS, t)
    vt_spec = pl.BlockSpec((1, NSA_KV_HEADS, VT_ROWS, tm), lambda b, i: (b, 0, 0, i))
    const = lambda b, i: (0, 0)
    return pl.pallas_call(
        _inproj_kernel,
        grid=grid,
        in_specs=[pl.BlockSpec((1, tm, D_MODEL), lambda b, i: (b, i, 0)),
                  pl.BlockSpec((1, 6, D_MODEL), lambda b, i: (b, 0, 0)),
                  pl.BlockSpec((1, D_MODEL), const),
                  pl.BlockSpec((D_MODEL, IN_COLS_PAD), const),
                  pl.BlockSpec((1, 1, tm), lambda b, i: (b, 0, i)),
                  pl.BlockSpec(inv_t.shape, const),
                  pl.BlockSpec(qg_t.shape, const),
                  pl.BlockSpec(kg_t.shape, lambda b, i: (0, 0, 0))],
        out_specs=[pl.BlockSpec((1, tm, 4 * HG_WIDTH), lambda b, i: (b, i, 0)),
                   pl.BlockSpec((1, NSA_HEADS, NSA_HD, tm), lambda b, i: (b, 0, 0, i)),
                   kv_spec, kv_spec, kv_spec, vt_spec, kv_spec, vt_spec,
                   pl.BlockSpec((1, GATE_ROWS, tm), lambda b, i: (b, 0, i))],
        out_shape=[jax.ShapeDtypeStruct((bsz, t, 4 * HG_WIDTH), F32),
                   jax.ShapeDtypeStruct((bsz, NSA_HEADS, NSA_HD, t), BF16),
                   jax.ShapeDtypeStruct(kv_shape, F32),
                   jax.ShapeDtypeStruct(kv_shape, F32),
                   jax.ShapeDtypeStruct(kv_shape, BF16),
                   jax.ShapeDtypeStruct(vt_shape, BF16),
                   jax.ShapeDtypeStruct(kv_shape, BF16),
                   jax.ShapeDtypeStruct(vt_shape, BF16),
                   jax.ShapeDtypeStruct((bsz, GATE_ROWS, t), F32)],
        compiler_params=pltpu.CompilerParams(
            dimension_semantics=("arbitrary", "arbitrary"), vmem_limit_bytes=VMEM_LIMIT),
        name="inproj",
    )(x, mod, g1, w_in_p, pos_row, inv_t, qg_t, kg_t)


def _hgrn_chunk(zq, zf, zi, zg, lb, hg_g, st_ref, h, tri, level):
    c = HG_CHUNK
    e_z = jnp.exp(-jnp.abs(zf))
    logsig = jnp.minimum(zf, 0.0) - jnp.log(1.0 + e_z)
    a = jnp.log(lb)
    bb = jnp.log1p(-lb) + logsig
    logf = jnp.maximum(a, bb) + jnp.log(1.0 + jnp.exp(-jnp.abs(a - bb)))
    k = (1.0 - lb) * (jnp.where(zf >= 0.0, e_z, 1.0) / (1.0 + e_z))
    q = _silu(zq)
    v = zi
    l_hi = logf.astype(BF16)
    l_mid = (logf - l_hi.astype(F32)).astype(BF16)
    l_lo = (logf - l_hi.astype(F32) - l_mid.astype(F32)).astype(BF16)
    bc = _dot(tri, l_hi) + _dot(tri, l_mid) + _dot(tri, l_lo)

    col = lax.broadcasted_iota(jnp.int32, (HG_SUB, c), 1)
    b2 = bc * LOG2E
    rows_a = []
    for i in range(c // HG_SUB):
        lo = i * HG_SUB
        b_i = b2[lo:lo + HG_SUB]
        q_i = q[lo:lo + HG_SUB]
        k_i = k[lo:lo + HG_SUB]
        a_i = jnp.zeros((HG_SUB, c), F32)
        for s in range(HG_SUB):
            e = jnp.exp2(jnp.minimum(b_i - b_i[s:s + 1], 0.0)) * q_i * k_i[s:s + 1]
            a_i = jnp.where(col == lo + s, jnp.sum(e, axis=-1, keepdims=True), a_i)
        rows_a.append(a_i)
    amat = jnp.concatenate(rows_a, axis=0)

    size, idx = c // 2, 1
    while size >= HG_SUB:
        pieces = []
        for e0 in range(0, c, 2 * size):
            o0 = e0 + size
            r = bc[o0:o0 + 1]
            pieces.append(k[e0:o0] * jnp.exp(r - bc[e0:o0]))
            pieces.append(q[o0:o0 + size] * jnp.exp(bc[o0:o0 + size] - r))
        hmat = jnp.concatenate(pieces, axis=0).astype(BF16)
        amat = jnp.where(level == idx, _dot_nt(hmat, hmat), amat)
        size, idx = size // 2, idx + 1
    amat = jnp.where(level >= 0, amat, 0.0)

    st = st_ref[h]
    o = _dot_nt((q * jnp.exp(bc)).astype(BF16), st.astype(BF16)) + _dot(amat.astype(BF16), v.astype(BF16))
    bl = bc[c - 1:c]
    kdec = (k * jnp.exp(bl - bc)).astype(BF16)
    st_ref[h] = jnp.exp(bl) * st + _dot(v.T.astype(BF16), kdec)

    y = o * lax.rsqrt(jnp.mean(o * o, axis=-1, keepdims=True) + EPS) * hg_g
    return y * _silu(zg)


def _hgrn_kernel(z_ref, lbl_ref, g_ref, o_ref, st_ref, *, l_idx):
    @pl.when(pl.program_id(1) == 0)
    def _():
        st_ref[...] = jnp.zeros_like(st_ref)

    lg = lbl_ref[...]
    ex = jnp.exp(lg - jnp.max(lg, axis=0, keepdims=True))
    sm = ex / jnp.sum(ex, axis=0, keepdims=True)
    lb_all = jnp.sum(sm[:l_idx + 1], axis=0, keepdims=True)

    c = HG_CHUNK
    ri = lax.broadcasted_iota(jnp.int32, (c, c), 0)
    ci = lax.broadcasted_iota(jnp.int32, (c, c), 1)
    tri = jnp.where(ci <= ri, 1.0, 0.0).astype(BF16)
    level = jnp.where(ci > ri, -1, 0)
    size, idx = c // 2, 1
    while size >= HG_SUB:
        sh = size.bit_length() - 1
        paired = ((ri >> (sh + 1)) == (ci >> (sh + 1))) & ((ri >> sh) != (ci >> sh)) & (ci <= ri)
        level = jnp.where(paired, idx, level)
        size, idx = size // 2, idx + 1
    n_chunks = z_ref.shape[1] // c

    def body(ch, carry):
        r0 = pl.multiple_of(ch * c, c)
        for h in range(HG_HEADS):
            sl = slice(h * HG_HD, (h + 1) * HG_HD)
            zq = z_ref[0, pl.ds(r0, c), h * HG_HD:(h + 1) * HG_HD]
            zf = z_ref[0, pl.ds(r0, c), HG_WIDTH + h * HG_HD:HG_WIDTH + (h + 1) * HG_HD]
            zi = z_ref[0, pl.ds(r0, c), 2 * HG_WIDTH + h * HG_HD:2 * HG_WIDTH + (h + 1) * HG_HD]
            zg = z_ref[0, pl.ds(r0, c), 3 * HG_WIDTH + h * HG_HD:3 * HG_WIDTH + (h + 1) * HG_HD]
            y = _hgrn_chunk(zq, zf, zi, zg, lb_all[:, sl], g_ref[...], st_ref, h, tri, level)
            o_ref[0, pl.ds(r0, c), h * HG_HD:(h + 1) * HG_HD] = y.astype(o_ref.dtype)
        return carry

    lax.fori_loop(0, n_chunks, body, 0, unroll=2)


def _hgrn(zhg, lb_logits, hg_g, l_idx, tb):
    bsz, t, _ = zhg.shape
    return pl.pallas_call(
        functools.partial(_hgrn_kernel, l_idx=l_idx),
        grid=(bsz, t // tb),
        in_specs=[pl.BlockSpec((1, tb, 4 * HG_WIDTH), lambda b, i: (b, i, 0)),
                  pl.BlockSpec(lb_logits.shape, lambda b, i: (0, 0)),
                  pl.BlockSpec((1, HG_HD), lambda b, i: (0, 0))],
        out_specs=pl.BlockSpec((1, tb, HG_WIDTH), lambda b, i: (b, i, 0)),
        out_shape=jax.ShapeDtypeStruct((bsz, t, HG_WIDTH), BF16),
        scratch_shapes=[pltpu.VMEM((HG_HEADS, HG_HD, HG_HD), F32)],
        compiler_params=pltpu.CompilerParams(
            dimension_semantics=("arbitrary", "arbitrary"), vmem_limit_bytes=VMEM_LIMIT),
        name="hgrn",
    )(zhg, lb_logits, hg_g)


def _compress_kernel(xk_ref, xv_ref, pe_ref, w1_ref, w2_ref, ko_ref, vo_ref):
    half = CMP_STRIDE * NSA_HD
    outs = []
    for kv, x_ref in enumerate((xk_ref, xv_ref)):
        x = x_ref[0, 0]
        ha = _dot((x + pe_ref[kv, 0:1, :]).astype(BF16), w1_ref[kv, :half, :])
        hb = _dot((x + pe_ref[kv, 1:2, :]).astype(BF16), w1_ref[kv, half:, :])
        n = x.shape[0]
        pre = ha + pltpu.roll(hb, n - 1, axis=0)
        outs.append(_dot(_silu(pre).astype(BF16), w2_ref[kv]))
    ko_ref[0, 0] = outs[0].astype(ko_ref.dtype)
    vo_ref[0, 0] = outs[1].T.astype(vo_ref.dtype)


def _compress(xk, xv, pe2, w1, w2):
    bsz, g, nseg, width = xk.shape
    x_spec = pl.BlockSpec((1, 1, nseg, width), lambda b, j: (b, j, 0, 0))
    return pl.pallas_call(
        _compress_kernel,
        grid=(bsz, g),
        in_specs=[x_spec, x_spec,
                  pl.BlockSpec(pe2.shape, lambda b, j: (0, 0, 0)),
                  pl.BlockSpec(w1.shape, lambda b, j: (0, 0, 0)),
                  pl.BlockSpec(w2.shape, lambda b, j: (0, 0, 0))],
        out_specs=[pl.BlockSpec((1, 1, nseg, NSA_HD), lambda b, j: (b, j, 0, 0)),
                   pl.BlockSpec((1, 1, NSA_HD, nseg), lambda b, j: (b, j, 0, 0))],
        out_shape=[jax.ShapeDtypeStruct((bsz, g, nseg, NSA_HD), BF16),
                   jax.ShapeDtypeStruct((bsz, g, NSA_HD, nseg), BF16)],
        compiler_params=pltpu.CompilerParams(
            dimension_semantics=("arbitrary", "arbitrary"), vmem_limit_bytes=VMEM_LIMIT),
        name="compress",
    )(xk, xv, pe2, w1, w2)


def _nsa_kernel(bound_ref, qt_ref, kc_ref, vct_ref, ks_ref, vst_ref, kw_ref, vwt_ref, gt_ref, mt_ref,
                o_ref, sel_ref, *, tq, tks, bounded):
    r = NSA_GROUP
    n_g = NSA_KV_HEADS
    ch = NSA_CHAIN
    lanes = ch * tq
    q0 = pl.program_id(1) * tq
    chains = [(g, g * r + c * ch) for g in range(n_g) for c in range(r // ch)]
    q_ts = [jnp.concatenate([qt_ref[0, h0 + i] for i in range(ch)], axis=1)
            for _, h0 in chains]

    def tile_heads(a):
        return jnp.concatenate([a] * ch, axis=1)

    n_blk = kc_ref.shape[2]
    nb = mt_ref.shape[0]
    blk_end = lax.broadcasted_iota(jnp.int32, (n_blk, tq), 0) * CMP_STRIDE + (CMP_BLOCK - 1)
    t_row = q0 + lax.broadcasted_iota(jnp.int32, (1, tq), 1)
    cvalid = tile_heads(jnp.where(blk_end <= t_row, 1.0, 0.0)) > 0.5
    some = tile_heads((t_row >= CMP_BLOCK - 1).astype(F32))
    j = lax.broadcasted_iota(jnp.int32, (nb, tq), 0)
    cur = jnp.right_shift(t_row, SLC_BLOCK.bit_length() - 1)
    forced = (j == 0) | (j == cur) | (j == cur - 1)
    o_cmp = []
    psum = [None] * n_g
    for (g, _), q_t in zip(chains, q_ts):
        s = jnp.where(cvalid, _dot(kc_ref[0, g], q_t), NEG)
        e = jnp.exp2(s - jnp.max(s, axis=0, keepdims=True))
        p = e * (some / jnp.sum(e, axis=0, keepdims=True))
        o_cmp.append(_dot(vct_ref[0, g], p.astype(BF16)))
        for i in range(ch):
            part = p[:, i * tq:(i + 1) * tq]
            psum[g] = part if psum[g] is None else psum[g] + part
    for g in range(n_g):
        imp = jnp.dot(mt_ref[...], psum[g], preferred_element_type=F32, precision=lax.Precision.HIGHEST)
        imp = jnp.where(j <= cur, jnp.where(forced, jnp.inf, imp), -1.0)
        rank = jnp.zeros((nb, tq), jnp.int32)
        for i in range(nb):
            row_i = imp[i:i + 1, :]
            ahead = (row_i > imp) | ((row_i == imp) & (j > i))
            rank = rank + ahead.astype(jnp.int32)
        sel_ref[g] = jnp.where(rank < min(SLC_TOPK, nb), 0.0, NEG)

    blocks_per_tile = tks // SLC_BLOCK
    rel = (lax.broadcasted_iota(jnp.int32, (tks, tq), 1)
           - lax.broadcasted_iota(jnp.int32, (tks, tq), 0))

    shift = bound_ref[0] * LOG2E if bounded else None

    def slc_bias(it, k0):
        causal = rel + (q0 - k0) >= 0
        biases = []
        for g in range(n_g):
            rows = sel_ref[g, pl.ds(pl.multiple_of(it * blocks_per_tile, blocks_per_tile), blocks_per_tile), :]
            bias = jnp.concatenate([jnp.broadcast_to(rows[jj:jj + 1], (SLC_BLOCK, tq))
                                    for jj in range(blocks_per_tile)], axis=0)
            bias = jnp.where(causal, bias, NEG)
            biases.append(tile_heads(bias - shift if bounded else bias))
        return biases

    def slc_body(it, carry):
        k0 = pl.multiple_of(it * tks, tks)
        biases = slc_bias(it, k0)
        out = []
        for (g, _), q_t, state in zip(chains, q_ts, carry):
            sc = _dot(ks_ref[0, g, pl.ds(k0, tks), :], q_t) + biases[g]
            vt = vst_ref[0, g, :, pl.ds(k0, tks)]
            if bounded:
                (acc,) = state
                acc = acc + _dot(vt, jnp.exp2(sc).astype(BF16))
                out.append((acc,))
            else:
                m, acc = state
                m_new = jnp.maximum(m, jnp.max(sc, axis=0, keepdims=True))
                acc = jnp.exp2(m - m_new) * acc + _dot(vt, jnp.exp2(sc - m_new).astype(BF16))
                out.append((m_new, acc))
        return tuple(out)

    zero_state = (jnp.zeros((VT_ROWS, lanes), F32),)
    init = tuple(zero_state if bounded else (jnp.full((1, lanes), NEG, F32),) + zero_state for _ in chains)
    n_it = (q0 + tq + tks - 1) // tks
    fin = lax.fori_loop(0, n_it, slc_body, init)
    o_slc = [st[-1][:NSA_HD] * (1.0 / st[-1][NSA_HD:NSA_HD + 1]) for st in fin]

    span = WIN + tq
    start = pl.multiple_of(jnp.maximum(q0 - WIN, 0), tq)
    dist = (lax.broadcasted_iota(jnp.int32, (span, tq), 1)
            - lax.broadcasted_iota(jnp.int32, (span, tq), 0)) + (q0 - start)
    wbias = jnp.where((dist >= 0) & (dist < WIN), 0.0, NEG)
    wbias = tile_heads(wbias - shift if bounded else wbias)
    o_win = []
    for (g, _), q_t in zip(chains, q_ts):
        sw = _dot(kw_ref[0, g, pl.ds(start, span), :], q_t) + wbias
        ew = jnp.exp2(sw if bounded else sw - jnp.max(sw, axis=0, keepdims=True))
        ow = _dot(vwt_ref[0, g, :, pl.ds(start, span)], ew.astype(BF16))
        o_win.append(ow[:NSA_HD] * (1.0 / ow[NSA_HD:NSA_HD + 1]))

    for ci, (_, h0) in enumerate(chains):
        for i in range(ch):
            cols = slice(i * tq, (i + 1) * tq)
            head = h0 + i
            o_h = jnp.zeros((NSA_HD, tq), F32)
            for br, o_b in enumerate((o_cmp[ci], o_slc[ci], o_win[ci])):
                row = head * N_BRANCH + br
                o_h = o_h + gt_ref[0, row:row + 1, :] * o_b[:, cols]
            o_ref[0, :, head * NSA_HD:(head + 1) * NSA_HD] = o_h.T.astype(o_ref.dtype)


def _nsa(q_t, kc, vct, ks, vst, kw, vwt, gates_t, mt, bound, tq, tks, bounded):
    bsz, _, _, t = q_t.shape
    n_blk = kc.shape[2]
    n_g = NSA_KV_HEADS
    full = lambda b, i: (b, 0, 0, 0)
    k_spec = pl.BlockSpec((1, n_g, t, NSA_HD), full)
    vt_spec = pl.BlockSpec((1, n_g, VT_ROWS, t), full)
    return pl.pallas_call(
        functools.partial(_nsa_kernel, tq=tq, tks=tks, bounded=bounded),
        grid=(bsz, t // tq),
        in_specs=[pl.BlockSpec(memory_space=pltpu.SMEM),
                  pl.BlockSpec((1, NSA_HEADS, NSA_HD, tq), lambda b, i: (b, 0, 0, i)),
                  pl.BlockSpec((1, n_g, n_blk, NSA_HD), full),
                  pl.BlockSpec((1, n_g, NSA_HD, n_blk), full),
                  k_spec, vt_spec, k_spec, vt_spec,
                  pl.BlockSpec((1, GATE_ROWS, tq), lambda b, i: (b, 0, i)),
                  pl.BlockSpec(mt.shape, lambda b, i: (0, 0))],
        out_specs=pl.BlockSpec((1, tq, NSA_WIDTH), lambda b, i: (b, i, 0)),
        out_shape=jax.ShapeDtypeStruct((bsz, t, NSA_WIDTH), BF16),
        scratch_shapes=[pltpu.VMEM((n_g, mt.shape[0], tq), F32)],
        compiler_params=pltpu.CompilerParams(
            dimension_semantics=("arbitrary", "arbitrary"), vmem_limit_bytes=VMEM_LIMIT),
        name="nsa_bounded" if bounded else "nsa",
    )(bound, q_t, kc, vct, ks, vst, kw, vwt, gates_t, mt)


def _causal_conv(u, prev, cw, cb):
    tm = u.shape[0]
    g = SUBLANES
    r8 = lax.broadcasted_iota(jnp.int32, (g, u.shape[1]), 0)
    wrap1 = jnp.where(r8 == 0, prev[2 * g - 1:2 * g], pltpu.roll(u[tm - g:], 1, axis=0))
    wrap2 = jnp.where(r8 == 0, prev[g - 1:g], pltpu.roll(u[tm - 2 * g:tm - g], 1, axis=0))
    u1 = jnp.concatenate([wrap1, u[:tm - g]], axis=0)
    u2 = jnp.concatenate([wrap2, wrap1, u[:tm - 2 * g]], axis=0)
    return cb + u2 * cw[0:1] + u1 * cw[1:2] + u * cw[2:3]


def _ffn_kernel(x_ref, hg_ref, ns_ref, mod_ref, wo_ref, g2_ref, wu_ref, cw_ref, cb_ref, wd_ref,
                o_ref, carry_ref, buf_ref, h2_ref, u_ref, g_ref, acc_ref, *, fc):
    mix = _dot(hg_ref[0], wo_ref[:HG_WIDTH, :]) + _dot(ns_ref[0], wo_ref[HG_WIDTH:, :])
    x1_nat = x_ref[0] + mod_ref[0, 2:3, :] * mix
    n_lt = D_MODEL // LANES
    tm = x1_nat.shape[0]
    groups = tm // SUBLANES
    pitch = buf_ref.shape[1] // SUBLANES
    for c in range(n_lt):
        for sg in range(SUBLANES):
            buf_ref[c, sg * pitch:sg * pitch + groups] = x1_nat[sg * groups:(sg + 1) * groups,
                                                                c * LANES:(c + 1) * LANES]

    x1 = jnp.concatenate(
        [jnp.concatenate([buf_ref[c, pl.ds(j, SUBLANES, stride=pitch), :] for j in range(groups)], axis=0)
         for c in range(n_lt)], axis=1)
    y = x1 * lax.rsqrt(jnp.mean(x1 * x1, axis=-1, keepdims=True) + EPS) * g2_ref[...]
    h2_ref[...] = (y * (1.0 + mod_ref[0, 4:5, :]) + mod_ref[0, 3:4, :]).astype(h2_ref.dtype)
    acc_ref[...] = jnp.zeros_like(acc_ref)
    first = pl.program_id(1) == 0

    def up(jc, slot):
        for half in range(2):
            off = pl.multiple_of(half * D_FF + jc * fc, fc)
            u_ref[slot, half] = _dot(h2_ref[...], wu_ref[:, pl.ds(off, fc)])

    def act(jc, slot):
        halves = []
        for half in range(2):
            cols = pl.ds(pl.multiple_of(half * D_FF + jc * fc, fc), fc)
            u = u_ref[slot, half]
            prev = jnp.where(first, 0.0, carry_ref[:, cols])
            carry_ref[:, cols] = u[tm - 2 * SUBLANES:]
            halves.append(_causal_conv(u, prev, cw_ref[:, cols], cb_ref[:, cols]))
        a, v = halves
        g_ref[slot] = (_silu(a) * v).astype(g_ref.dtype)

    def down(jc, slot):
        acc_ref[...] += _dot(g_ref[slot], wd_ref[pl.ds(pl.multiple_of(jc * fc, fc), fc), :])

    n = D_FF // fc

    def tick(i):
        if i < n:
            up(i, i % 2)
        if 0 <= i - 1 < n:
            act(i - 1, (i - 1) % 2)
        if 0 <= i - 2 < n:
            down(i - 2, i % 2)

    tick(0)
    tick(1)
    pairs = (n - 2) // 2

    def body(k, carry):
        i = 2 + 2 * k
        up(i, 0)
        act(i - 1, 1)
        down(i - 2, 0)
        up(i + 1, 1)
        act(i, 0)
        down(i - 1, 1)
        return carry

    lax.fori_loop(0, pairs, body, 0)
    for i in range(2 + 2 * pairs, n + 2):
        tick(i)

    out = x1 + mod_ref[0, 5:6, :] * acc_ref[...]
    for c in range(n_lt):
        for j in range(groups):
            buf_ref[c, pl.ds(j, SUBLANES, stride=pitch), :] = out[j * SUBLANES:(j + 1) * SUBLANES,
                                                                  c * LANES:(c + 1) * LANES]
    for c in range(n_lt):
        for sg in range(SUBLANES):
            o_ref[0, sg * groups:(sg + 1) * groups, c * LANES:(c + 1) * LANES] = (
                buf_ref[c, sg * pitch:sg * pitch + groups])


def _ffn(x, o_hg, o_nsa, mod, w_out, g2, w_up, conv_w, conv_b, w_down, tm, fc):
    bsz, t, _ = x.shape
    row_spec = lambda w: pl.BlockSpec((1, tm, w), lambda b, i: (b, i, 0))
    groups = tm // SUBLANES
    pad = SUBLANES if (groups // SUBLANES) % 2 == 0 else 0
    resident = lambda a: pl.BlockSpec(a.shape, lambda b, i: (0, 0), pipeline_mode=pl.Buffered(1))
    return pl.pallas_call(
        functools.partial(_ffn_kernel, fc=fc),
        grid=(bsz, t // tm),
        in_specs=[row_spec(D_MODEL), row_spec(HG_WIDTH), row_spec(NSA_WIDTH),
                  pl.BlockSpec((1, 6, D_MODEL), lambda b, i: (b, 0, 0)),
                  resident(w_out), resident(g2), resident(w_up), resident(conv_w), resident(conv_b),
                  resident(w_down)],
        out_specs=row_spec(D_MODEL),
        out_shape=jax.ShapeDtypeStruct(x.shape, F32),
        scratch_shapes=[pltpu.VMEM((2 * SUBLANES, 2 * D_FF), F32),
                        pltpu.VMEM((D_MODEL // LANES, tm + SUBLANES * pad, LANES), F32),
                        pltpu.VMEM((tm, D_MODEL), BF16),
                        pltpu.VMEM((2, 2, tm, fc), F32),
                        pltpu.VMEM((2, tm, fc), BF16),
                        pltpu.VMEM((tm, D_MODEL), F32)],
        compiler_params=pltpu.CompilerParams(
            dimension_semantics=("arbitrary", "arbitrary"), vmem_limit_bytes=VMEM_LIMIT),
        name="ffn",
    )(x, o_hg, o_nsa, mod, w_out, g2, w_up, conv_w, conv_b, w_down)


def _rope_tables():
    half = ROPE_DIM // 2
    inv = ROPE_THETA ** (-jnp.arange(half, dtype=F32) * 2.0 / ROPE_DIM)
    return (jnp.tile(inv.reshape(half, 1), (1, LANES)),)


def _gain_t(g):
    return jnp.tile(g.reshape(NSA_HD, 1), (LANES // NSA_HD, LANES))


def _selection_tables(t):
    n_seg = t // CMP_STRIDE
    nb = t // SLC_BLOCK
    cst = np.arange(n_seg) * CMP_STRIDE
    sst = np.arange(nb) * SLC_BLOCK
    ovl = np.clip(np.minimum(cst[:, None] + CMP_BLOCK, sst[None] + SLC_BLOCK)
                  - np.maximum(cst[:, None], sst[None]), 0, None) / CMP_BLOCK
    ovl[(t - CMP_BLOCK) // CMP_STRIDE + 1:] = 0.0
    return (jnp.asarray(ovl.T, dtype=F32),)


def _layer(x, mod, pos_row, l, p, tables):
    bsz, t, _ = x.shape
    inv_t, mt = tables
    w_in_p = jnp.pad(p["w_in"][l], ((0, 0), (0, IN_COLS_PAD - IN_COLS))).astype(BF16)
    qg_t = _gain_t(p["q_norm_g"][l])
    kg_t = jnp.stack([_gain_t(p["k_norm_g"][l, br]) for br in range(N_BRANCH)])
    zhg, q_t, kc, vc, ks, vst, kw, vwt, gates_t = _inproj(
        x, mod, p["norm1_g"][l].reshape(1, D_MODEL), w_in_p, pos_row, inv_t, qg_t, kg_t, tm=512)

    o_hg = _hgrn(zhg, p["lb_logits"], p["hg_norm_g"][l].reshape(1, HG_HD), l, tb=512)

    n_seg = t // CMP_STRIDE
    seg_w = CMP_STRIDE * NSA_HD
    pe2 = p["pe_cmp"][l].reshape(2, 2, seg_w)
    kcmp, vcmp_t = _compress(kc.reshape(bsz, NSA_KV_HEADS, n_seg, seg_w),
                             vc.reshape(bsz, NSA_KV_HEADS, n_seg, seg_w),
                             pe2, p["w_cmp1"][l].astype(BF16), p["w_cmp2"][l].astype(BF16))
    bound = (SCORE_BOUND_MARGIN * NSA_HD ** 0.5 * jnp.max(jnp.abs(p["q_norm_g"][l]))
             * jnp.max(jnp.abs(p["k_norm_g"][l, 1:]))).reshape(1).astype(F32)
    nsa_args = (q_t, kcmp, vcmp_t, ks, vst, kw, vwt, gates_t, mt, bound)
    o_nsa = lax.cond(bound[0] <= MAX_SCORE_BOUND,
                     lambda a: _nsa(*a, tq=128, tks=512, bounded=True),
                     lambda a: _nsa(*a, tq=128, tks=512, bounded=False), nsa_args)

    return _ffn(x, o_hg, o_nsa, mod, p["w_out"][l].astype(BF16), p["norm2_g"][l].reshape(1, D_MODEL),
                p["w_up"][l].astype(BF16), p["conv_w"][l], p["conv_b"][l].reshape(1, 2 * D_FF),
                p["w_down"][l].astype(BF16), tm=512, fc=256)


def kernel(x, c, positions, w_ada, b_ada, norm1_g, w_in, lb_logits, hg_norm_g, q_norm_g, k_norm_g, pe_cmp, w_cmp1, w_cmp2, w_out, norm2_g, w_up, conv_w, conv_b, w_down):
    p = dict(w_in=w_in, norm1_g=norm1_g, lb_logits=lb_logits, hg_norm_g=hg_norm_g, q_norm_g=q_norm_g,
             k_norm_g=k_norm_g, pe_cmp=pe_cmp, w_cmp1=w_cmp1, w_cmp2=w_cmp2, w_out=w_out,
             norm2_g=norm2_g, w_up=w_up, conv_w=conv_w, conv_b=conv_b, w_down=w_down)
    bsz, t, _ = x.shape
    tables = _rope_tables() + _selection_tables(t)
    pos_row = positions.reshape(bsz, 1, t)
    for l in range(w_ada.shape[0]):
        mod = _ada(c, w_ada[l], b_ada[l]).reshape(bsz, 6, D_MODEL)
        x = _layer(x, mod, pos_row, l, p, tables)
    return x
```

```python
import functools

import jax
import jax.numpy as jnp
import numpy as np
from jax import lax
from jax.experimental import pallas as pl
from jax.experimental.pallas import tpu as pltpu

D_MODEL = 1024
HG_HEADS = 4
HG_HD = 128
HG_WIDTH = HG_HEADS * HG_HD
HG_CHUNK = 64
HG_SUB = 8
LOG2E = 1.4426950408889634
NSA_HEADS = 8
NSA_KV_HEADS = 2
NSA_HD = 64
NSA_GROUP = NSA_HEADS // NSA_KV_HEADS
NSA_CHAIN = 4
NSA_WIDTH = NSA_HEADS * NSA_HD
N_BRANCH = 3
CMP_BLOCK = 32
CMP_STRIDE = 16
CMP_HIDDEN = 256
SLC_BLOCK = 64
SLC_TOPK = 16
WIN = 512
ROPE_DIM = NSA_HD // 4
ROPE_THETA = 500000.0
D_FF = 2816
CONV_W = 3
EPS = 1e-6
NEG = -1e30
SCORE_BOUND_MARGIN = 1.02
MAX_SCORE_BOUND = 40.0

LANES = 128
SUBLANES = 8
VMEM_LIMIT = 56 * 1024 * 1024

OFF_HG = 0
OFF_Q = 4 * HG_WIDTH
OFF_KV = OFF_Q + NSA_WIDTH
OFF_G = OFF_KV + 6 * NSA_KV_HEADS * NSA_HD
IN_COLS = OFF_G + N_BRANCH * NSA_HEADS
IN_COLS_PAD = OFF_G + LANES
GATE_ROWS = 32
VT_ROWS = NSA_HD + SUBLANES

BF16 = jnp.bfloat16
F32 = jnp.float32


def _dot(a, b):
    return jnp.dot(a, b, preferred_element_type=F32)


def _dot_nt(a, b):
    return lax.dot_general(a, b, (((1,), (1,)), ((), ())), preferred_element_type=F32)


def _sigmoid(x):
    return 1.0 / (1.0 + jnp.exp(-x))


def _silu(x):
    return x * _sigmoid(x)


def _ada_kernel(c_ref, w_ref, b_ref, o_ref):
    cs = _silu(c_ref[...])
    o_ref[...] = _dot(cs.astype(BF16), w_ref[...].astype(BF16)) + b_ref[...]


def _ada(c, w, b):
    bsz = c.shape[0]
    n = w.shape[1]
    tn = D_MODEL
    return pl.pallas_call(
        _ada_kernel,
        grid=(n // tn,),
        in_specs=[pl.BlockSpec((bsz, D_MODEL), lambda j: (0, 0)),
                  pl.BlockSpec((D_MODEL, tn), lambda j: (0, j)),
                  pl.BlockSpec((1, tn), lambda j: (0, j))],
        out_specs=pl.BlockSpec((bsz, tn), lambda j: (0, j)),
        out_shape=jax.ShapeDtypeStruct((bsz, n), F32),
        name="ada",
    )(c, w, b.reshape(1, n))


def _pair_norm_rope_t(xt, g_t, cos_t, sin_t):
    half = ROPE_DIM // 2
    outs = []
    for hh in range(2):
        x = xt[hh * NSA_HD:(hh + 1) * NSA_HD]
        ms = jnp.mean(x * x, axis=0, keepdims=True)
        xn = x * lax.rsqrt(ms + EPS) * g_t[hh * NSA_HD:(hh + 1) * NSA_HD]
        x1, x2 = xn[:half], xn[half:ROPE_DIM]
        outs += [x1 * cos_t - x2 * sin_t, x2 * cos_t + x1 * sin_t, xn[ROPE_DIM:]]
    return jnp.concatenate(outs, axis=0)


def _inproj_kernel(x_ref, mod_ref, g1_ref, w_ref, pos_ref, inv_ref, qg_ref, kg_ref,
                   zhg_ref, qt_ref, kc_ref, vc_ref, ks_ref, vst_ref, kw_ref, vwt_ref, gt_ref):
    x = x_ref[0]
    ms = jnp.mean(x * x, axis=-1, keepdims=True)
    y = x * lax.rsqrt(ms + EPS) * g1_ref[...]
    h = (y * (1.0 + mod_ref[0, 1:2, :]) + mod_ref[0, 0:1, :]).astype(BF16)

    zhg_ref[0] = _dot(h, w_ref[:, OFF_HG:OFF_Q])

    tm = x.shape[0]
    reps = tm // LANES

    def lane_tile(a):
        return jnp.concatenate([a] * reps, axis=1)

    ang = lane_tile(inv_ref[...]) * pos_ref[0].astype(F32)
    cos_t = jnp.cos(ang)
    sin_t = jnp.sin(ang)

    zq = _dot(h, w_ref[:, OFF_Q:OFF_KV])
    scale = NSA_HD ** -0.5 * LOG2E
    qg_t = lane_tile(qg_ref[...])
    for p in range(NSA_HEADS // 2):
        rt = _pair_norm_rope_t(zq[:, p * LANES:(p + 1) * LANES].T, qg_t, cos_t, sin_t)
        rt = (rt * scale).astype(qt_ref.dtype)
        qt_ref[0, 2 * p] = rt[:NSA_HD]
        qt_ref[0, 2 * p + 1] = rt[NSA_HD:]

    zkv = _dot(h, w_ref[:, OFF_KV:OFF_G])
    k_outs = (kc_ref, ks_ref, kw_ref)
    for br in range(N_BRANCH):
        kt = _pair_norm_rope_t(zkv[:, (2 * br) * LANES:(2 * br + 1) * LANES].T,
                               lane_tile(kg_ref[br]), cos_t, sin_t)
        kk = kt.T.astype(k_outs[br].dtype)
        for g in range(NSA_KV_HEADS):
            k_outs[br][0, g] = kk[:, g * NSA_HD:(g + 1) * NSA_HD]
    vc = zkv[:, LANES:2 * LANES]
    for g in range(NSA_KV_HEADS):
        vc_ref[0, g] = vc[:, g * NSA_HD:(g + 1) * NSA_HD]
    for br, vt_ref in ((1, vst_ref), (2, vwt_ref)):
        vt = zkv[:, (2 * br + 1) * LANES:(2 * br + 2) * LANES].T.astype(vt_ref.dtype)
        ones_row = jnp.where(lax.broadcasted_iota(jnp.int32, (SUBLANES, tm), 0) == 0, 1.0, 0.0).astype(vt_ref.dtype)
        for g in range(NSA_KV_HEADS):
            vt_ref[0, g, :NSA_HD] = vt[g * NSA_HD:(g + 1) * NSA_HD]
            vt_ref[0, g, NSA_HD:] = ones_row

    gates = _sigmoid(_dot(h, w_ref[:, OFF_G:IN_COLS_PAD]))
    gt_ref[0] = gates.T[:GATE_ROWS]


def _inproj(x, mod, g1, w_in_p, pos_row, inv_t, qg_t, kg_t, tm):
    bsz, t, _ = x.shape
    grid = (bsz, t // tm)
    kv_shape = (bsz, NSA_KV_HEADS, t, NSA_HD)
    kv_spec = pl.BlockSpec((1, NSA_KV_HEADS, tm, NSA_HD), lambda b, i: (b, 0, i, 0))
    vt_shape = (bsz, NSA_KV_HEADS, VT_ROWS, t)
    vt_spec = pl.BlockSpec((1, NSA_KV_HEADS, VT_ROWS, tm), lambda b, i: (b, 0, 0, i))
    const = lambda b, i: (0, 0)
    return pl.pallas_call(
        _inproj_kernel,
        grid=grid,
        in_specs=[pl.BlockSpec((1, tm, D_MODEL), lambda b, i: (b, i, 0)),
                  pl.BlockSpec((1, 6, D_MODEL), lambda b, i: (b, 0, 0)),
                  pl.BlockSpec((1, D_MODEL), const),
                  pl.BlockSpec((D_MODEL, IN_COLS_PAD), const),
                  pl.BlockSpec((1, 1, tm), lambda b, i: (b, 0, i)),
                  pl.BlockSpec(inv_t.shape, const),
                  pl.BlockSpec(qg_t.shape, const),
                  pl.BlockSpec(kg_t.shape, lambda b, i: (0, 0, 0))],
        out_specs=[pl.BlockSpec((1, tm, 4 * HG_WIDTH), lambda b, i: (b, i, 0)),
                   pl.BlockSpec((1, NSA_HEADS, NSA_HD, tm), lambda b, i: (b, 0, 0, i)),
                   kv_spec, kv_spec, kv_spec, vt_spec, kv_spec, vt_spec,
                   pl.BlockSpec((1, GATE_ROWS, tm), lambda b, i: (b, 0, i))],
        out_shape=[jax.ShapeDtypeStruct((bsz, t, 4 * HG_WIDTH), F32),
                   jax.ShapeDtypeStruct((bsz, NSA_HEADS, NSA_HD, t), BF16),
                   jax.ShapeDtypeStruct(kv_shape, F32),
                   jax.ShapeDtypeStruct(kv_shape, F32),
                   jax.ShapeDtypeStruct(kv_shape, BF16),
                   jax.ShapeDtypeStruct(vt_shape, BF16),
                   jax.ShapeDtypeStruct(kv_shape, BF16),
                   jax.ShapeDtypeStruct(vt_shape, BF16),
                   jax.ShapeDtypeStruct((bsz, GATE_ROWS, t), F32)],
        compiler_params=pltpu.CompilerParams(
            dimension_semantics=("arbitrary", "arbitrary"), vmem_limit_bytes=VMEM_LIMIT),
        name="inproj",
    )(x, mod, g1, w_in_p, pos_row, inv_t, qg_t, kg_t)


def _hgrn_chunk(zq, zf, zi, zg, lb, hg_g, st_ref, bk_ref, h, tri, level):
    c = HG_CHUNK
    e_z = jnp.exp(-jnp.abs(zf))
    logsig = jnp.minimum(zf, 0.0) - jnp.log(1.0 + e_z)
    a = jnp.log(lb)
    bb = jnp.log1p(-lb) + logsig
    logf = jnp.maximum(a, bb) + jnp.log(1.0 + jnp.exp(-jnp.abs(a - bb)))
    k = (1.0 - lb) * (jnp.where(zf >= 0.0, e_z, 1.0) / (1.0 + e_z))
    q = _silu(zq)
    v = zi
    l_hi = logf.astype(BF16)
    l_mid = (logf - l_hi.astype(F32)).astype(BF16)
    l_lo = (logf - l_hi.astype(F32) - l_mid.astype(F32)).astype(BF16)
    bc = _dot(tri, l_hi) + _dot(tri, l_mid) + _dot(tri, l_lo)

    col = lax.broadcasted_iota(jnp.int32, (HG_SUB, c), 1)
    b2 = bc * LOG2E
    bk_ref[0, h] = b2
    bk_ref[1, h] = k
    rows_a = []
    for i in range(c // HG_SUB):
        lo = i * HG_SUB
        b_i = b2[lo:lo + HG_SUB]
        q_i = q[lo:lo + HG_SUB]
        k_i = k[lo:lo + HG_SUB]
        a_i = jnp.zeros((HG_SUB, c), F32)
        for s in range(HG_SUB):
            b_s = bk_ref[0, h, lo + s:lo + s + 1, :]
            k_s = bk_ref[1, h, lo + s:lo + s + 1, :]
            e = jnp.exp2(jnp.minimum(b_i - b_s, 0.0)) * q_i * k_s
            a_i = jnp.where(col == lo + s, jnp.sum(e, axis=-1, keepdims=True), a_i)
        rows_a.append(a_i)
    amat = jnp.concatenate(rows_a, axis=0)

    size, idx = c // 2, 1
    while size >= HG_SUB:
        pieces = []
        for e0 in range(0, c, 2 * size):
            o0 = e0 + size
            r = bc[o0:o0 + 1]
            pieces.append(k[e0:o0] * jnp.exp(r - bc[e0:o0]))
            pieces.append(q[o0:o0 + size] * jnp.exp(bc[o0:o0 + size] - r))
        hmat = jnp.concatenate(pieces, axis=0).astype(BF16)
        amat = jnp.where(level == idx, _dot_nt(hmat, hmat), amat)
        size, idx = size // 2, idx + 1
    amat = jnp.where(level >= 0, amat, 0.0)

    st = st_ref[h]
    o = _dot_nt((q * jnp.exp(bc)).astype(BF16), st.astype(BF16)) + _dot(amat.astype(BF16), v.astype(BF16))
    bl = bc[c - 1:c]
    kdec = (k * jnp.exp(bl - bc)).astype(BF16)
    st_ref[h] = jnp.exp(bl) * st + _dot(v.T.astype(BF16), kdec)

    y = o * lax.rsqrt(jnp.mean(o * o, axis=-1, keepdims=True) + EPS) * hg_g
    return y * _silu(zg)


def _hgrn_kernel(z_ref, lbl_ref, g_ref, o_ref, st_ref, bk_ref, *, l_idx):
    @pl.when(pl.program_id(1) == 0)
    def _():
        st_ref[...] = jnp.zeros_like(st_ref)

    lg = lbl_ref[...]
    ex = jnp.exp(lg - jnp.max(lg, axis=0, keepdims=True))
    sm = ex / jnp.sum(ex, axis=0, keepdims=True)
    lb_all = jnp.sum(sm[:l_idx + 1], axis=0, keepdims=True)

    c = HG_CHUNK
    ri = lax.broadcasted_iota(jnp.int32, (c, c), 0)
    ci = lax.broadcasted_iota(jnp.int32, (c, c), 1)
    tri = jnp.where(ci <= ri, 1.0, 0.0).astype(BF16)
    level = jnp.where(ci > ri, -1, 0)
    size, idx = c // 2, 1
    while size >= HG_SUB:
        sh = size.bit_length() - 1
        paired = ((ri >> (sh + 1)) == (ci >> (sh + 1))) & ((ri >> sh) != (ci >> sh)) & (ci <= ri)
        level = jnp.where(paired, idx, level)
        size, idx = size // 2, idx + 1
    n_chunks = z_ref.shape[1] // c

    def body(ch, carry):
        r0 = pl.multiple_of(ch * c, c)
        for h in range(HG_HEADS):
            sl = slice(h * HG_HD, (h + 1) * HG_HD)
            zq = z_ref[0, pl.ds(r0, c), h * HG_HD:(h + 1) * HG_HD]
            zf = z_ref[0, pl.ds(r0, c), HG_WIDTH + h * HG_HD:HG_WIDTH + (h + 1) * HG_HD]
            zi = z_ref[0, pl.ds(r0, c), 2 * HG_WIDTH + h * HG_HD:2 * HG_WIDTH + (h + 1) * HG_HD]
            zg = z_ref[0, pl.ds(r0, c), 3 * HG_WIDTH + h * HG_HD:3 * HG_WIDTH + (h + 1) * HG_HD]
            y = _hgrn_chunk(zq, zf, zi, zg, lb_all[:, sl], g_ref[...], st_ref, bk_ref, h, tri, level)
            o_ref[0, pl.ds(r0, c), h * HG_HD:(h + 1) * HG_HD] = y.astype(o_ref.dtype)
        return carry

    lax.fori_loop(0, n_chunks, body, 0, unroll=4)


def _hgrn(zhg, lb_logits, hg_g, l_idx, tb):
    bsz, t, _ = zhg.shape
    return pl.pallas_call(
        functools.partial(_hgrn_kernel, l_idx=l_idx),
        grid=(bsz, t // tb),
        in_specs=[pl.BlockSpec((1, tb, 4 * HG_WIDTH), lambda b, i: (b, i, 0)),
                  pl.BlockSpec(lb_logits.shape, lambda b, i: (0, 0)),
                  pl.BlockSpec((1, HG_HD), lambda b, i: (0, 0))],
        out_specs=pl.BlockSpec((1, tb, HG_WIDTH), lambda b, i: (b, i, 0)),
        out_shape=jax.ShapeDtypeStruct((bsz, t, HG_WIDTH), BF16),
        scratch_shapes=[pltpu.VMEM((HG_HEADS, HG_HD, HG_HD), F32),
                        pltpu.VMEM((2, HG_HEADS, HG_CHUNK, HG_HD), F32)],
        compiler_params=pltpu.CompilerParams(
            dimension_semantics=("arbitrary", "arbitrary"), vmem_limit_bytes=VMEM_LIMIT),
        name="hgrn",
    )(zhg, lb_logits, hg_g)


def _compress_kernel(xk_ref, xv_ref, pe_ref, w1_ref, w2_ref, ko_ref, vo_ref):
    half = CMP_STRIDE * NSA_HD
    outs = []
    for kv, x_ref in enumerate((xk_ref, xv_ref)):
        x = x_ref[0, 0]
        ha = _dot((x + pe_ref[kv, 0:1, :]).astype(BF16), w1_ref[kv, :half, :])
        hb = _dot((x + pe_ref[kv, 1:2, :]).astype(BF16), w1_ref[kv, half:, :])
        n = x.shape[0]
        pre = ha + pltpu.roll(hb, n - 1, axis=0)
        outs.append(_dot(_silu(pre).astype(BF16), w2_ref[kv]))
    ko_ref[0, 0] = outs[0].astype(ko_ref.dtype)
    vo_ref[0, 0] = outs[1].T.astype(vo_ref.dtype)


def _compress(xk, xv, pe2, w1, w2):
    bsz, g, nseg, width = xk.shape
    x_spec = pl.BlockSpec((1, 1, nseg, width), lambda b, j: (b, j, 0, 0))
    return pl.pallas_call(
        _compress_kernel,
        grid=(bsz, g),
        in_specs=[x_spec, x_spec,
                  pl.BlockSpec(pe2.shape, lambda b, j: (0, 0, 0)),
                  pl.BlockSpec(w1.shape, lambda b, j: (0, 0, 0)),
                  pl.BlockSpec(w2.shape, lambda b, j: (0, 0, 0))],
        out_specs=[pl.BlockSpec((1, 1, nseg, NSA_HD), lambda b, j: (b, j, 0, 0)),
                   pl.BlockSpec((1, 1, NSA_HD, nseg), lambda b, j: (b, j, 0, 0))],
        out_shape=[jax.ShapeDtypeStruct((bsz, g, nseg, NSA_HD), BF16),
                   jax.ShapeDtypeStruct((bsz, g, NSA_HD, nseg), BF16)],
        compiler_params=pltpu.CompilerParams(
            dimension_semantics=("arbitrary", "arbitrary"), vmem_limit_bytes=VMEM_LIMIT),
        name="compress",
    )(xk, xv, pe2, w1, w2)


def _nsa_kernel(bound_ref, qt_ref, kc_ref, vct_ref, ks_ref, vst_ref, kw_ref, vwt_ref, gt_ref, mt_ref,
                o_ref, sel_ref, *, tq, tks, bounded):
    r = NSA_GROUP
    n_g = NSA_KV_HEADS
    ch = NSA_CHAIN
    lanes = ch * tq
    q0 = pl.program_id(1) * tq
    chains = [(g, g * r + c * ch) for g in range(n_g) for c in range(r // ch)]
    q_ts = [jnp.concatenate([qt_ref[0, h0 + i] for i in range(ch)], axis=1)
            for _, h0 in chains]

    def tile_heads(a):
        return jnp.concatenate([a] * ch, axis=1)

    n_blk = kc_ref.shape[2]
    nb = mt_ref.shape[0]
    blk_end = lax.broadcasted_iota(jnp.int32, (n_blk, tq), 0) * CMP_STRIDE + (CMP_BLOCK - 1)
    t_row = q0 + lax.broadcasted_iota(jnp.int32, (1, tq), 1)
    cvalid = tile_heads(jnp.where(blk_end <= t_row, 1.0, 0.0)) > 0.5
    some = tile_heads((t_row >= CMP_BLOCK - 1).astype(F32))
    j = lax.broadcasted_iota(jnp.int32, (nb, tq), 0)
    cur = jnp.right_shift(t_row, SLC_BLOCK.bit_length() - 1)
    forced = (j == 0) | (j == cur) | (j == cur - 1)
    o_cmp = []
    psum = [None] * n_g
    for (g, _), q_t in zip(chains, q_ts):
        s = jnp.where(cvalid, _dot(kc_ref[0, g], q_t), NEG)
        e = jnp.exp2(s - jnp.max(s, axis=0, keepdims=True))
        p = e * (some / jnp.sum(e, axis=0, keepdims=True))
        o_cmp.append(_dot(vct_ref[0, g], p.astype(BF16)))
        for i in range(ch):
            part = p[:, i * tq:(i + 1) * tq]
            psum[g] = part if psum[g] is None else psum[g] + part
    for g in range(n_g):
        imp = jnp.dot(mt_ref[...], psum[g], preferred_element_type=F32, precision=lax.Precision.HIGHEST)
        imp = jnp.where(j <= cur, jnp.where(forced, jnp.inf, imp), -1.0)
        rank = jnp.zeros((nb, tq), jnp.int32)
        for i in range(nb):
            row_i = imp[i:i + 1, :]
            ahead = (row_i > imp) | ((row_i == imp) & (j > i))
            rank = rank + ahead.astype(jnp.int32)
        sel_ref[g] = jnp.where(rank < min(SLC_TOPK, nb), 0.0, NEG)

    blocks_per_tile = tks // SLC_BLOCK
    rel = (lax.broadcasted_iota(jnp.int32, (tks, tq), 1)
           - lax.broadcasted_iota(jnp.int32, (tks, tq), 0))

    shift = bound_ref[0] * LOG2E if bounded else None

    def slc_bias(it, k0):
        causal = rel + (q0 - k0) >= 0
        biases = []
        for g in range(n_g):
            rows = sel_ref[g, pl.ds(pl.multiple_of(it * blocks_per_tile, blocks_per_tile), blocks_per_tile), :]
            bias = jnp.concatenate([jnp.broadcast_to(rows[jj:jj + 1], (SLC_BLOCK, tq))
                                    for jj in range(blocks_per_tile)], axis=0)
            bias = jnp.where(causal, bias, NEG)
            biases.append(tile_heads(bias - shift if bounded else bias))
        return biases

    def slc_body(it, carry):
        k0 = pl.multiple_of(it * tks, tks)
        biases = slc_bias(it, k0)
        out = []
        for (g, _), q_t, state in zip(chains, q_ts, carry):
            sc = _dot(ks_ref[0, g, pl.ds(k0, tks), :], q_t) + biases[g]
            vt = vst_ref[0, g, :, pl.ds(k0, tks)]
            if bounded:
                (acc,) = state
                acc = acc + _dot(vt, jnp.exp2(sc).astype(BF16))
                out.append((acc,))
            else:
                m, acc = state
                m_new = jnp.maximum(m, jnp.max(sc, axis=0, keepdims=True))
                acc = jnp.exp2(m - m_new) * acc + _dot(vt, jnp.exp2(sc - m_new).astype(BF16))
                out.append((m_new, acc))
        return tuple(out)

    zero_state = (jnp.zeros((VT_ROWS, lanes), F32),)
    init = tuple(zero_state if bounded else (jnp.full((1, lanes), NEG, F32),) + zero_state for _ in chains)
    n_it = (q0 + tq + tks - 1) // tks
    fin = lax.fori_loop(0, n_it, slc_body, init)
    o_slc = [st[-1][:NSA_HD] * (1.0 / st[-1][NSA_HD:NSA_HD + 1]) for st in fin]

    span = WIN + tq
    start = pl.multiple_of(jnp.maximum(q0 - WIN, 0), tq)
    dist = (lax.broadcasted_iota(jnp.int32, (span, tq), 1)
            - lax.broadcasted_iota(jnp.int32, (span, tq), 0)) + (q0 - start)
    wbias = jnp.where((dist >= 0) & (dist < WIN), 0.0, NEG)
    wbias = tile_heads(wbias - shift if bounded else wbias)
    o_win = []
    for (g, _), q_t in zip(chains, q_ts):
        sw = _dot(kw_ref[0, g, pl.ds(start, span), :], q_t) + wbias
        ew = jnp.exp2(sw if bounded else sw - jnp.max(sw, axis=0, keepdims=True))
        ow = _dot(vwt_ref[0, g, :, pl.ds(start, span)], ew.astype(BF16))
        o_win.append(ow[:NSA_HD] * (1.0 / ow[NSA_HD:NSA_HD + 1]))

    for ci, (_, h0) in enumerate(chains):
        for i in range(ch):
            cols = slice(i * tq, (i + 1) * tq)
            head = h0 + i
            o_h = jnp.zeros((NSA_HD, tq), F32)
            for br, o_b in enumerate((o_cmp[ci], o_slc[ci], o_win[ci])):
                row = head * N_BRANCH + br
                o_h = o_h + gt_ref[0, row:row + 1, :] * o_b[:, cols]
            o_ref[0, :, head * NSA_HD:(head + 1) * NSA_HD] = o_h.T.astype(o_ref.dtype)


def _nsa(q_t, kc, vct, ks, vst, kw, vwt, gates_t, mt, bound, tq, tks, bounded):
    bsz, _, _, t = q_t.shape
    n_blk = kc.shape[2]
    n_g = NSA_KV_HEADS
    full = lambda b, i: (b, 0, 0, 0)
    k_spec = pl.BlockSpec((1, n_g, t, NSA_HD), full)
    vt_spec = pl.BlockSpec((1, n_g, VT_ROWS, t), full)
    return pl.pallas_call(
        functools.partial(_nsa_kernel, tq=tq, tks=tks, bounded=bounded),
        grid=(bsz, t // tq),
        in_specs=[pl.BlockSpec(memory_space=pltpu.SMEM),
                  pl.BlockSpec((1, NSA_HEADS, NSA_HD, tq), lambda b, i: (b, 0, 0, i)),
                  pl.BlockSpec((1, n_g, n_blk, NSA_HD), full),
                  pl.BlockSpec((1, n_g, NSA_HD, n_blk), full),
                  k_spec, vt_spec, k_spec, vt_spec,
                  pl.BlockSpec((1, GATE_ROWS, tq), lambda b, i: (b, 0, i)),
                  pl.BlockSpec(mt.shape, lambda b, i: (0, 0))],
        out_specs=pl.BlockSpec((1, tq, NSA_WIDTH), lambda b, i: (b, i, 0)),
        out_shape=jax.ShapeDtypeStruct((bsz, t, NSA_WIDTH), BF16),
        scratch_shapes=[pltpu.VMEM((n_g, mt.shape[0], tq), F32)],
        compiler_params=pltpu.CompilerParams(
            dimension_semantics=("arbitrary", "arbitrary"), vmem_limit_bytes=VMEM_LIMIT),
        name="nsa_bounded" if bounded else "nsa",
    )(bound, q_t, kc, vct, ks, vst, kw, vwt, gates_t, mt)


def _causal_conv(u, prev, cw, cb):
    tm = u.shape[0]
    g = SUBLANES
    r8 = lax.broadcasted_iota(jnp.int32, (g, u.shape[1]), 0)
    wrap1 = jnp.where(r8 == 0, prev[2 * g - 1:2 * g], pltpu.roll(u[tm - g:], 1, axis=0))
    wrap2 = jnp.where(r8 == 0, prev[g - 1:g], pltpu.roll(u[tm - 2 * g:tm - g], 1, axis=0))
    u1 = jnp.concatenate([wrap1, u[:tm - g]], axis=0)
    u2 = jnp.concatenate([wrap2, wrap1, u[:tm - 2 * g]], axis=0)
    return cb + u2 * cw[0:1] + u1 * cw[1:2] + u * cw[2:3]


def _ffn_kernel(x_ref, hg_ref, ns_ref, mod_ref, wo_ref, g2_ref, wu_ref, cw_ref, cb_ref, wd_ref,
                o_ref, carry_ref, buf_ref, h2_ref, u_ref, g_ref, acc_ref, *, fc):
    mix = _dot(hg_ref[0], wo_ref[:HG_WIDTH, :]) + _dot(ns_ref[0], wo_ref[HG_WIDTH:, :])
    x1_nat = x_ref[0] + mod_ref[0, 2:3, :] * mix
    n_lt = D_MODEL // LANES
    tm = x1_nat.shape[0]
    groups = tm // SUBLANES
    pitch = buf_ref.shape[1] // SUBLANES
    for c in range(n_lt):
        for sg in range(SUBLANES):
            buf_ref[c, sg * pitch:sg * pitch + groups] = x1_nat[sg * groups:(sg + 1) * groups,
                                                                c * LANES:(c + 1) * LANES]

    x1 = jnp.concatenate(
        [jnp.concatenate([buf_ref[c, pl.ds(j, SUBLANES, stride=pitch), :] for j in range(groups)], axis=0)
         for c in range(n_lt)], axis=1)
    y = x1 * lax.rsqrt(jnp.mean(x1 * x1, axis=-1, keepdims=True) + EPS) * g2_ref[...]
    h2_ref[...] = (y * (1.0 + mod_ref[0, 4:5, :]) + mod_ref[0, 3:4, :]).astype(h2_ref.dtype)
    acc_ref[...] = jnp.zeros_like(acc_ref)
    first = pl.program_id(1) == 0

    def up(jc, slot):
        for half in range(2):
            off = pl.multiple_of(half * D_FF + jc * fc, fc)
            u_ref[slot, half] = _dot(h2_ref[...], wu_ref[:, pl.ds(off, fc)])

    def act(jc, slot):
        halves = []
        for half in range(2):
            cols = pl.ds(pl.multiple_of(half * D_FF + jc * fc, fc), fc)
            u = u_ref[slot, half]
            prev = jnp.where(first, 0.0, carry_ref[:, cols])
            carry_ref[:, cols] = u[tm - 2 * SUBLANES:]
            halves.append(_causal_conv(u, prev, cw_ref[:, cols], cb_ref[:, cols]))
        a, v = halves
        g_ref[slot] = (_silu(a) * v).astype(g_ref.dtype)

    def down(jc, slot):
        acc_ref[...] += _dot(g_ref[slot], wd_ref[pl.ds(pl.multiple_of(jc * fc, fc), fc), :])

    n = D_FF // fc

    def tick(i):
        if i < n:
            up(i, i % 2)
        if 0 <= i - 1 < n:
            act(i - 1, (i - 1) % 2)
        if 0 <= i - 2 < n:
            down(i - 2, i % 2)

    tick(0)
    tick(1)
    pairs = (n - 2) // 2

    def body(k, carry):
        i = 2 + 2 * k
        up(i, 0)
        act(i - 1, 1)
        down(i - 2, 0)
        up(i + 1, 1)
        act(i, 0)
        down(i - 1, 1)
        return carry

    lax.fori_loop(0, pairs, body, 0)
    for i in range(2 + 2 * pairs, n + 2):
        tick(i)

    out = x1 + mod_ref[0, 5:6, :] * acc_ref[...]
    for c in range(n_lt):
        for j in range(groups):
            buf_ref[c, pl.ds(j, SUBLANES, stride=pitch), :] = out[j * SUBLANES:(j + 1) * SUBLANES,
                                                                  c * LANES:(c + 1) * LANES]
    for c in range(n_lt):
        for sg in range(SUBLANES):
            o_ref[0, sg * groups:(sg + 1) * groups, c * LANES:(c + 1) * LANES] = (
                buf_ref[c, sg * pitch:sg * pitch + groups])


def _ffn(x, o_hg, o_nsa, mod, w_out, g2, w_up, conv_w, conv_b, w_down, tm, fc):
    bsz, t, _ = x.shape
    row_spec = lambda w: pl.BlockSpec((1, tm, w), lambda b, i: (b, i, 0))
    groups = tm // SUBLANES
    pad = SUBLANES if (groups // SUBLANES) % 2 == 0 else 0
    resident = lambda a: pl.BlockSpec(a.shape, lambda b, i: (0, 0), pipeline_mode=pl.Buffered(1))
    return pl.pallas_call(
        functools.partial(_ffn_kernel, fc=fc),
        grid=(bsz, t // tm),
        in_specs=[row_spec(D_MODEL), row_spec(HG_WIDTH), row_spec(NSA_WIDTH),
                  pl.BlockSpec((1, 6, D_MODEL), lambda b, i: (b, 0, 0)),
                  resident(w_out), resident(g2), resident(w_up), resident(conv_w), resident(conv_b),
                  resident(w_down)],
        out_specs=row_spec(D_MODEL),
        out_shape=jax.ShapeDtypeStruct(x.shape, F32),
        scratch_shapes=[pltpu.VMEM((2 * SUBLANES, 2 * D_FF), F32),
                        pltpu.VMEM((D_MODEL // LANES, tm + SUBLANES * pad, LANES), F32),
                        pltpu.VMEM((tm, D_MODEL), BF16),
                        pltpu.VMEM((2, 2, tm, fc), F32),
                        pltpu.VMEM((2, tm, fc), BF16),
                        pltpu.VMEM((tm, D_MODEL), F32)],
        compiler_params=pltpu.CompilerParams(
            dimension_semantics=("arbitrary", "arbitrary"), vmem_limit_bytes=VMEM_LIMIT),
        name="ffn",
    )(x, o_hg, o_nsa, mod, w_out, g2, w_up, conv_w, conv_b, w_down)


def _rope_tables():
    half = ROPE_DIM // 2
    inv = ROPE_THETA ** (-jnp.arange(half, dtype=F32) * 2.0 / ROPE_DIM)
    return (jnp.tile(inv.reshape(half, 1), (1, LANES)),)


def _gain_t(g):
    return jnp.tile(g.reshape(NSA_HD, 1), (LANES // NSA_HD, LANES))


def _selection_tables(t):
    n_seg = t // CMP_STRIDE
    nb = t // SLC_BLOCK
    cst = np.arange(n_seg) * CMP_STRIDE
    sst = np.arange(nb) * SLC_BLOCK
    ovl = np.clip(np.minimum(cst[:, None] + CMP_BLOCK, sst[None] + SLC_BLOCK)
                  - np.maximum(cst[:, None], sst[None]), 0, None) / CMP_BLOCK
    ovl[(t - CMP_BLOCK) // CMP_STRIDE + 1:] = 0.0
    return (jnp.asarray(ovl.T, dtype=F32),)


def _layer(x, mod, pos_row, l, p, tables):
    bsz, t, _ = x.shape
    inv_t, mt = tables
    w_in_p = jnp.pad(p["w_in"][l], ((0, 0), (0, IN_COLS_PAD - IN_COLS))).astype(BF16)
    qg_t = _gain_t(p["q_norm_g"][l])
    kg_t = jnp.stack([_gain_t(p["k_norm_g"][l, br]) for br in range(N_BRANCH)])
    zhg, q_t, kc, vc, ks, vst, kw, vwt, gates_t = _inproj(
        x, mod, p["norm1_g"][l].reshape(1, D_MODEL), w_in_p, pos_row, inv_t, qg_t, kg_t, tm=512)

    o_hg = _hgrn(zhg, p["lb_logits"], p["hg_norm_g"][l].reshape(1, HG_HD), l, tb=512)

    n_seg = t // CMP_STRIDE
    seg_w = CMP_STRIDE * NSA_HD
    pe2 = p["pe_cmp"][l].reshape(2, 2, seg_w)
    kcmp, vcmp_t = _compress(kc.reshape(bsz, NSA_KV_HEADS, n_seg, seg_w),
                             vc.reshape(bsz, NSA_KV_HEADS, n_seg, seg_w),
                             pe2, p["w_cmp1"][l].astype(BF16), p["w_cmp2"][l].astype(BF16))
    bound = (SCORE_BOUND_MARGIN * NSA_HD ** 0.5 * jnp.max(jnp.abs(p["q_norm_g"][l]))
             * jnp.max(jnp.abs(p["k_norm_g"][l, 1:]))).reshape(1).astype(F32)
    nsa_args = (q_t, kcmp, vcmp_t, ks, vst, kw, vwt, gates_t, mt, bound)
    o_nsa = lax.cond(bound[0] <= MAX_SCORE_BOUND,
                     lambda a: _nsa(*a, tq=128, tks=512, bounded=True),
                     lambda a: _nsa(*a, tq=128, tks=512, bounded=False), nsa_args)

    return _ffn(x, o_hg, o_nsa, mod, p["w_out"][l].astype(BF16), p["norm2_g"][l].reshape(1, D_MODEL),
                p["w_up"][l].astype(BF16), p["conv_w"][l], p["conv_b"][l].reshape(1, 2 * D_FF),
                p["w_down"][l].astype(BF16), tm=512, fc=256)


def kernel(x, c, positions, w_ada, b_ada, norm1_g, w_in, lb_logits, hg_norm_g, q_norm_g, k_norm_g, pe_cmp, w_cmp1, w_cmp2, w_out, norm2_g, w_up, conv_w, conv_b, w_down):
    p = dict(w_in=w_in, norm1_g=norm1_g, lb_logits=lb_logits, hg_norm_g=hg_norm_g, q_norm_g=q_norm_g,
             k_norm_g=k_norm_g, pe_cmp=pe_cmp, w_cmp1=w_cmp1, w_cmp2=w_cmp2, w_out=w_out,
             norm2_g=norm2_g, w_up=w_up, conv_w=conv_w, conv_b=conv_b, w_down=w_down)
    bsz, t, _ = x.shape
    tables = _rope_tables() + _selection_tables(t)
    pos_row = positions.reshape(bsz, 1, t)
    for l in range(w_ada.shape[0]):
        mod = _ada(c, w_ada[l], b_ada[l]).reshape(bsz, 6, D_MODEL)
        x = _layer(x, mod, pos_row, l, p, tables)
    return x
```

```python
import functools

import jax
import jax.numpy as jnp
import numpy as np
from jax import lax
from jax.experimental import pallas as pl
from jax.experimental.pallas import tpu as pltpu

D_MODEL = 1024
HG_HEADS = 4
HG_HD = 128
HG_WIDTH = HG_HEADS * HG_HD
HG_CHUNK = 64
HG_SUB = 8
LOG2E = 1.4426950408889634
NSA_HEADS = 8
NSA_KV_HEADS = 2
NSA_HD = 64
NSA_GROUP = NSA_HEADS // NSA_KV_HEADS
NSA_CHAIN = 4
NSA_WIDTH = NSA_HEADS * NSA_HD
N_BRANCH = 3
CMP_BLOCK = 32
CMP_STRIDE = 16
CMP_HIDDEN = 256
SLC_BLOCK = 64
SLC_TOPK = 16
WIN = 512
ROPE_DIM = NSA_HD // 4
ROPE_THETA = 500000.0
D_FF = 2816
CONV_W = 3
EPS = 1e-6
NEG = -1e30
SCORE_BOUND_MARGIN = 1.02
MAX_SCORE_BOUND = 40.0

LANES = 128
SUBLANES = 8
VMEM_LIMIT = 56 * 1024 * 1024

OFF_HG = 0
OFF_Q = 4 * HG_WIDTH
OFF_KV = OFF_Q + NSA_WIDTH
OFF_G = OFF_KV + 6 * NSA_KV_HEADS * NSA_HD
IN_COLS = OFF_G + N_BRANCH * NSA_HEADS
IN_COLS_PAD = OFF_G + LANES
GATE_ROWS = 32
VT_ROWS = NSA_HD + SUBLANES

BF16 = jnp.bfloat16
F32 = jnp.float32


def _dot(a, b):
    return jnp.dot(a, b, preferred_element_type=F32)


def _dot_nt(a, b):
    return lax.dot_general(a, b, (((1,), (1,)), ((), ())), preferred_element_type=F32)


def _sigmoid(x):
    return 1.0 / (1.0 + jnp.exp(-x))


def _silu(x):
    return x * _sigmoid(x)


def _ada_kernel(c_ref, w_ref, b_ref, o_ref):
    cs = _silu(c_ref[...])
    o_ref[...] = _dot(cs.astype(BF16), w_ref[...].astype(BF16)) + b_ref[...]


def _ada(c, w, b):
    bsz = c.shape[0]
    n = w.shape[1]
    tn = D_MODEL
    return pl.pallas_call(
        _ada_kernel,
        grid=(n // tn,),
        in_specs=[pl.BlockSpec((bsz, D_MODEL), lambda j: (0, 0)),
                  pl.BlockSpec((D_MODEL, tn), lambda j: (0, j)),
                  pl.BlockSpec((1, tn), lambda j: (0, j))],
        out_specs=pl.BlockSpec((bsz, tn), lambda j: (0, j)),
        out_shape=jax.ShapeDtypeStruct((bsz, n), F32),
        name="ada",
    )(c, w, b.reshape(1, n))


def _pair_norm_rope_t(xt, g_t, cos_t, sin_t):
    half = ROPE_DIM // 2
    outs = []
    for hh in range(2):
        x = xt[hh * NSA_HD:(hh + 1) * NSA_HD]
        ms = jnp.mean(x * x, axis=0, keepdims=True)
        xn = x * lax.rsqrt(ms + EPS) * g_t[hh * NSA_HD:(hh + 1) * NSA_HD]
        x1, x2 = xn[:half], xn[half:ROPE_DIM]
        outs += [x1 * cos_t - x2 * sin_t, x2 * cos_t + x1 * sin_t, xn[ROPE_DIM:]]
    return jnp.concatenate(outs, axis=0)


def _inproj_kernel(x_ref, mod_ref, g1_ref, w_ref, pos_ref, inv_ref, qg_ref, kg_ref,
                   zhg_ref, qt_ref, kc_ref, vc_ref, ks_ref, vst_ref, kw_ref, vwt_ref, gt_ref):
    x = x_ref[0]
    ms = jnp.mean(x * x, axis=-1, keepdims=True)
    y = x * lax.rsqrt(ms + EPS) * g1_ref[...]
    h = (y * (1.0 + mod_ref[0, 1:2, :]) + mod_ref[0, 0:1, :]).astype(BF16)

    zhg_ref[0] = _dot(h, w_ref[:, OFF_HG:OFF_Q])

    tm = x.shape[0]
    reps = tm // LANES

    def lane_tile(a):
        return jnp.concatenate([a] * reps, axis=1)

    ang = lane_tile(inv_ref[...]) * pos_ref[0].astype(F32)
    cos_t = jnp.cos(ang)
    sin_t = jnp.sin(ang)

    zq = _dot(h, w_ref[:, OFF_Q:OFF_KV])
    scale = NSA_HD ** -0.5 * LOG2E
    qg_t = lane_tile(qg_ref[...])
    for p in range(NSA_HEADS // 2):
        rt = _pair_norm_rope_t(zq[:, p * LANES:(p + 1) * LANES].T, qg_t, cos_t, sin_t)
        rt = (rt * scale).astype(qt_ref.dtype)
        qt_ref[0, 2 * p] = rt[:NSA_HD]
        qt_ref[0, 2 * p + 1] = rt[NSA_HD:]

    zkv = _dot(h, w_ref[:, OFF_KV:OFF_G])
    k_outs = (kc_ref, ks_ref, kw_ref)
    for br in range(N_BRANCH):
        kt = _pair_norm_rope_t(zkv[:, (2 * br) * LANES:(2 * br + 1) * LANES].T,
                               lane_tile(kg_ref[br]), cos_t, sin_t)
        kk = kt.T.astype(k_outs[br].dtype)
        for g in range(NSA_KV_HEADS):
            k_outs[br][0, g] = kk[:, g * NSA_HD:(g + 1) * NSA_HD]
    vc = zkv[:, LANES:2 * LANES]
    for g in range(NSA_KV_HEADS):
        vc_ref[0, g] = vc[:, g * NSA_HD:(g + 1) * NSA_HD]
    for br, vt_ref in ((1, vst_ref), (2, vwt_ref)):
        vt = zkv[:, (2 * br + 1) * LANES:(2 * br + 2) * LANES].T.astype(vt_ref.dtype)
        ones_row = jnp.where(lax.broadcasted_iota(jnp.int32, (SUBLANES, tm), 0) == 0, 1.0, 0.0).astype(vt_ref.dtype)
        for g in range(NSA_KV_HEADS):
            vt_ref[0, g, :NSA_HD] = vt[g * NSA_HD:(g + 1) * NSA_HD]
            vt_ref[0, g, NSA_HD:] = ones_row

    gates = _sigmoid(_dot(h, w_ref[:, OFF_G:IN_COLS_PAD]))
    gt_ref[0] = gates.T[:GATE_ROWS]


def _inproj(x, mod, g1, w_in_p, pos_row, inv_t, qg_t, kg_t, tm):
    bsz, t, _ = x.shape
    grid = (bsz, t // tm)
    kv_shape = (bsz, NSA_KV_HEADS, t, NSA_HD)
    kv_spec = pl.BlockSpec((1, NSA_KV_HEADS, tm, NSA_HD), lambda b, i: (b, 0, i, 0))
    vt_shape = (bsz, NSA_KV_HEADS, VT_ROWS, t)
    vt_spec = pl.BlockSpec((1, NSA_KV_HEADS, VT_ROWS, tm), lambda b, i: (b, 0, 0, i))
    const = lambda b, i: (0, 0)
    return pl.pallas_call(
        _inproj_kernel,
        grid=grid,
        in_specs=[pl.BlockSpec((1, tm, D_MODEL), lambda b, i: (b, i, 0)),
                  pl.BlockSpec((1, 6, D_MODEL), lambda b, i: (b, 0, 0)),
                  pl.BlockSpec((1, D_MODEL), const),
                  pl.BlockSpec((D_MODEL, IN_COLS_PAD), const),
                  pl.BlockSpec((1, 1, tm), lambda b, i: (b, 0, i)),
                  pl.BlockSpec(inv_t.shape, const),
                  pl.BlockSpec(qg_t.shape, const),
                  pl.BlockSpec(kg_t.shape, lambda b, i: (0, 0, 0))],
        out_specs=[pl.BlockSpec((1, tm, 4 * HG_WIDTH), lambda b, i: (b, i, 0)),
                   pl.BlockSpec((1, NSA_HEADS, NSA_HD, tm), lambda b, i: (b, 0, 0, i)),
                   kv_spec, kv_spec, kv_spec, vt_spec, kv_spec, vt_spec,
                   pl.BlockSpec((1, GATE_ROWS, tm), lambda b, i: (b, 0, i))],
        out_shape=[jax.ShapeDtypeStruct((bsz, t, 4 * HG_WIDTH), F32),
                   jax.ShapeDtypeStruct((bsz, NSA_HEADS, NSA_HD, t), BF16),
                   jax.ShapeDtypeStruct(kv_shape, F32),
                   jax.ShapeDtypeStruct(kv_shape, F32),
                   jax.ShapeDtypeStruct(kv_shape, BF16),
                   jax.ShapeDtypeStruct(vt_shape, BF16),
                   jax.ShapeDtypeStruct(kv_shape, BF16),
                   jax.ShapeDtypeStruct(vt_shape, BF16),
                   jax.ShapeDtypeStruct((bsz, GATE_ROWS, t), F32)],
        compiler_params=pltpu.CompilerParams(
            dimension_semantics=("arbitrary", "arbitrary"), vmem_limit_bytes=VMEM_LIMIT),
        name="inproj",
    )(x, mod, g1, w_in_p, pos_row, inv_t, qg_t, kg_t)


def _hgrn_chunk(zq, zf, zi, zg, lb, hg_g, st_ref, bk_ref, h, tri, level):
    c = HG_CHUNK
    e_z = jnp.exp(-jnp.abs(zf))
    logsig = jnp.minimum(zf, 0.0) - jnp.log(1.0 + e_z)
    a = jnp.log(lb)
    bb = jnp.log1p(-lb) + logsig
    logf = jnp.maximum(a, bb) + jnp.log(1.0 + jnp.exp(-jnp.abs(a - bb)))
    k = (1.0 - lb) * (jnp.where(zf >= 0.0, e_z, 1.0) / (1.0 + e_z))
    q = _silu(zq)
    v = zi
    l_hi = logf.astype(BF16)
    l_mid = (logf - l_hi.astype(F32)).astype(BF16)
    l_lo = (logf - l_hi.astype(F32) - l_mid.astype(F32)).astype(BF16)
    bc = _dot(tri, l_hi) + _dot(tri, l_mid) + _dot(tri, l_lo)

    col = lax.broadcasted_iota(jnp.int32, (HG_SUB, c), 1)
    b2 = bc * LOG2E
    bk_ref[0, h] = b2
    bk_ref[1, h] = k
    rows_a = []
    for i in range(c // HG_SUB):
        lo = i * HG_SUB
        b_i = b2[lo:lo + HG_SUB]
        q_i = q[lo:lo + HG_SUB]
        k_i = k[lo:lo + HG_SUB]
        a_i = jnp.zeros((HG_SUB, c), F32)
        for s in range(HG_SUB):
            b_s = bk_ref[0, h, lo + s:lo + s + 1, :]
            k_s = bk_ref[1, h, lo + s:lo + s + 1, :]
            e = jnp.exp2(jnp.minimum(b_i - b_s, 0.0)) * q_i * k_s
            a_i = jnp.where(col == lo + s, jnp.sum(e, axis=-1, keepdims=True), a_i)
        rows_a.append(a_i)
    amat = jnp.concatenate(rows_a, axis=0)

    size, idx = c // 2, 1
    while size >= HG_SUB:
        pieces = []
        for e0 in range(0, c, 2 * size):
            o0 = e0 + size
            r = bc[o0:o0 + 1]
            pieces.append(k[e0:o0] * jnp.exp(r - bc[e0:o0]))
            pieces.append(q[o0:o0 + size] * jnp.exp(bc[o0:o0 + size] - r))
        hmat = jnp.concatenate(pieces, axis=0).astype(BF16)
        amat = jnp.where(level == idx, _dot_nt(hmat, hmat), amat)
        size, idx = size // 2, idx + 1
    amat = jnp.where(level >= 0, amat, 0.0)

    st = st_ref[h]
    o = _dot_nt((q * jnp.exp(bc)).astype(BF16), st.astype(BF16)) + _dot(amat.astype(BF16), v.astype(BF16))
    bl = bc[c - 1:c]
    kdec = (k * jnp.exp(bl - bc)).astype(BF16)
    st_ref[h] = jnp.exp(bl) * st + _dot(v.T.astype(BF16), kdec)

    y = o * lax.rsqrt(jnp.mean(o * o, axis=-1, keepdims=True) + EPS) * hg_g
    return y * _silu(zg)


def _hgrn_kernel(z_ref, lbl_ref, g_ref, o_ref, st_ref, bk_ref, *, l_idx):
    @pl.when(pl.program_id(1) == 0)
    def _():
        st_ref[...] = jnp.zeros_like(st_ref)

    lg = lbl_ref[...]
    ex = jnp.exp(lg - jnp.max(lg, axis=0, keepdims=True))
    sm = ex / jnp.sum(ex, axis=0, keepdims=True)
    lb_all = jnp.sum(sm[:l_idx + 1], axis=0, keepdims=True)

    c = HG_CHUNK
    ri = lax.broadcasted_iota(jnp.int32, (c, c), 0)
    ci = lax.broadcasted_iota(jnp.int32, (c, c), 1)
    tri = jnp.where(ci <= ri, 1.0, 0.0).astype(BF16)
    level = jnp.where(ci > ri, -1, 0)
    size, idx = c // 2, 1
    while size >= HG_SUB:
        sh = size.bit_length() - 1
        paired = ((ri >> (sh + 1)) == (ci >> (sh + 1))) & ((ri >> sh) != (ci >> sh)) & (ci <= ri)
        level = jnp.where(paired, idx, level)
        size, idx = size // 2, idx + 1
    n_chunks = z_ref.shape[1] // c

    def body(ch, carry):
        r0 = pl.multiple_of(ch * c, c)
        for h in range(HG_HEADS):
            sl = slice(h * HG_HD, (h + 1) * HG_HD)
            zq = z_ref[0, pl.ds(r0, c), h * HG_HD:(h + 1) * HG_HD]
            zf = z_ref[0, pl.ds(r0, c), HG_WIDTH + h * HG_HD:HG_WIDTH + (h + 1) * HG_HD]
            zi = z_ref[0, pl.ds(r0, c), 2 * HG_WIDTH + h * HG_HD:2 * HG_WIDTH + (h + 1) * HG_HD]
            zg = z_ref[0, pl.ds(r0, c), 3 * HG_WIDTH + h * HG_HD:3 * HG_WIDTH + (h + 1) * HG_HD]
            y = _hgrn_chunk(zq, zf, zi, zg, lb_all[:, sl], g_ref[...], st_ref, bk_ref, h, tri, level)
            o_ref[0, pl.ds(r0, c), h * HG_HD:(h + 1) * HG_HD] = y.astype(o_ref.dtype)
        return carry

    lax.fori_loop(0, n_chunks, body, 0, unroll=4)


def _hgrn(zhg, lb_logits, hg_g, l_idx, tb):
    bsz, t, _ = zhg.shape
    return pl.pallas_call(
        functools.partial(_hgrn_kernel, l_idx=l_idx),
        grid=(bsz, t // tb),
        in_specs=[pl.BlockSpec((1, tb, 4 * HG_WIDTH), lambda b, i: (b, i, 0)),
                  pl.BlockSpec(lb_logits.shape, lambda b, i: (0, 0)),
                  pl.BlockSpec((1, HG_HD), lambda b, i: (0, 0))],
        out_specs=pl.BlockSpec((1, tb, HG_WIDTH), lambda b, i: (b, i, 0)),
        out_shape=jax.ShapeDtypeStruct((bsz, t, HG_WIDTH), BF16),
        scratch_shapes=[pltpu.VMEM((HG_HEADS, HG_HD, HG_HD), F32),
                        pltpu.VMEM((2, HG_HEADS, HG_CHUNK, HG_HD), F32)],
        compiler_params=pltpu.CompilerParams(
            dimension_semantics=("arbitrary", "arbitrary"), vmem_limit_bytes=VMEM_LIMIT),
        name="hgrn",
    )(zhg, lb_logits, hg_g)


def _compress_kernel(xk_ref, xv_ref, pe_ref, w1_ref, w2_ref, ko_ref, vo_ref):
    half = CMP_STRIDE * NSA_HD
    outs = []
    for kv, x_ref in enumerate((xk_ref, xv_ref)):
        x = x_ref[0, 0]
        ha = _dot((x + pe_ref[kv, 0:1, :]).astype(BF16), w1_ref[kv, :half, :])
        hb = _dot((x + pe_ref[kv, 1:2, :]).astype(BF16), w1_ref[kv, half:, :])
        n = x.shape[0]
        pre = ha + pltpu.roll(hb, n - 1, axis=0)
        outs.append(_dot(_silu(pre).astype(BF16), w2_ref[kv]))
    ko_ref[0, 0] = outs[0].astype(ko_ref.dtype)
    vo_ref[0, 0] = outs[1].T.astype(vo_ref.dtype)


def _compress(xk, xv, pe2, w1, w2):
    bsz, g, nseg, width = xk.shape
    x_spec = pl.BlockSpec((1, 1, nseg, width), lambda b, j: (b, j, 0, 0))
    return pl.pallas_call(
        _compress_kernel,
        grid=(bsz, g),
        in_specs=[x_spec, x_spec,
                  pl.BlockSpec(pe2.shape, lambda b, j: (0, 0, 0)),
                  pl.BlockSpec(w1.shape, lambda b, j: (0, 0, 0)),
                  pl.BlockSpec(w2.shape, lambda b, j: (0, 0, 0))],
        out_specs=[pl.BlockSpec((1, 1, nseg, NSA_HD), lambda b, j: (b, j, 0, 0)),
                   pl.BlockSpec((1, 1, NSA_HD, nseg), lambda b, j: (b, j, 0, 0))],
        out_shape=[jax.ShapeDtypeStruct((bsz, g, nseg, NSA_HD), BF16),
                   jax.ShapeDtypeStruct((bsz, g, NSA_HD, nseg), BF16)],
        compiler_params=pltpu.CompilerParams(
            dimension_semantics=("arbitrary", "arbitrary"), vmem_limit_bytes=VMEM_LIMIT),
        name="compress",
    )(xk, xv, pe2, w1, w2)


def _nsa_kernel(bound_ref, qt_ref, kc_ref, vct_ref, ks_ref, vst_ref, kw_ref, vwt_ref, gt_ref, mt_ref,
                o_ref, sel_ref, *, tq, tks, bounded):
    r = NSA_GROUP
    n_g = NSA_KV_HEADS
    ch = NSA_CHAIN
    lanes = ch * tq
    q0 = pl.program_id(1) * tq
    chains = [(g, g * r + c * ch) for g in range(n_g) for c in range(r // ch)]
    q_ts = [jnp.concatenate([qt_ref[0, h0 + i] for i in range(ch)], axis=1)
            for _, h0 in chains]

    def tile_heads(a):
        return jnp.concatenate([a] * ch, axis=1)

    n_blk = kc_ref.shape[2]
    nb = mt_ref.shape[0]
    blk_end = lax.broadcasted_iota(jnp.int32, (n_blk, tq), 0) * CMP_STRIDE + (CMP_BLOCK - 1)
    t_row = q0 + lax.broadcasted_iota(jnp.int32, (1, tq), 1)
    cvalid = tile_heads(jnp.where(blk_end <= t_row, 1.0, 0.0)) > 0.5
    some = tile_heads((t_row >= CMP_BLOCK - 1).astype(F32))
    j = lax.broadcasted_iota(jnp.int32, (nb, tq), 0)
    cur = jnp.right_shift(t_row, SLC_BLOCK.bit_length() - 1)
    forced = (j == 0) | (j == cur) | (j == cur - 1)
    o_cmp = []
    psum = [None] * n_g
    for (g, _), q_t in zip(chains, q_ts):
        s = jnp.where(cvalid, _dot(kc_ref[0, g], q_t), NEG)
        e = jnp.exp2(s - jnp.max(s, axis=0, keepdims=True))
        p = e * (some / jnp.sum(e, axis=0, keepdims=True))
        o_cmp.append(_dot(vct_ref[0, g], p.astype(BF16)))
        for i in range(ch):
            part = p[:, i * tq:(i + 1) * tq]
            psum[g] = part if psum[g] is None else psum[g] + part
    for g in range(n_g):
        imp = jnp.dot(mt_ref[...], psum[g], preferred_element_type=F32, precision=lax.Precision.HIGHEST)
        imp = jnp.where(j <= cur, jnp.where(forced, jnp.inf, imp), -1.0)
        rank = jnp.zeros((nb, tq), jnp.int32)
        for i in range(nb):
            row_i = imp[i:i + 1, :]
            ahead = (row_i > imp) | ((row_i == imp) & (j > i))
            rank = rank + ahead.astype(jnp.int32)
        sel_ref[g] = jnp.where(rank < min(SLC_TOPK, nb), 0.0, NEG)

    blocks_per_tile = tks // SLC_BLOCK
    rel = (lax.broadcasted_iota(jnp.int32, (tks, tq), 1)
           - lax.broadcasted_iota(jnp.int32, (tks, tq), 0))

    shift = bound_ref[0] * LOG2E if bounded else None

    def slc_bias(it, k0, need_causal):
        biases = []
        for g in range(n_g):
            if isinstance(it, int):
                rows = sel_ref[g, it * blocks_per_tile:(it + 1) * blocks_per_tile, :]
            else:
                rows = sel_ref[g, pl.ds(pl.multiple_of(it * blocks_per_tile, blocks_per_tile), blocks_per_tile), :]
            bias = jnp.concatenate([jnp.broadcast_to(rows[jj:jj + 1], (SLC_BLOCK, tq))
                                    for jj in range(blocks_per_tile)], axis=0)
            if need_causal:
                bias = jnp.where(rel + (q0 - k0) >= 0, bias, NEG)
            biases.append(tile_heads(bias - shift if bounded else bias))
        return biases

    def slc_body(it, carry, need_causal=True):
        k0 = it * tks if isinstance(it, int) else pl.multiple_of(it * tks, tks)
        biases = slc_bias(it, k0, need_causal)
        out = []
        for (g, _), q_t, state in zip(chains, q_ts, carry):
            sc = _dot(ks_ref[0, g, pl.ds(k0, tks), :], q_t) + biases[g]
            vt = vst_ref[0, g, :, pl.ds(k0, tks)]
            if bounded:
                (acc,) = state
                acc = acc + _dot(vt, jnp.exp2(sc).astype(BF16))
                out.append((acc,))
            else:
                m, acc = state
                m_new = jnp.maximum(m, jnp.max(sc, axis=0, keepdims=True))
                acc = jnp.exp2(m - m_new) * acc + _dot(vt, jnp.exp2(sc - m_new).astype(BF16))
                out.append((m_new, acc))
        return tuple(out)

    zero_state = (jnp.zeros((VT_ROWS, lanes), F32),)
    init = tuple(zero_state if bounded else (jnp.full((1, lanes), NEG, F32),) + zero_state for _ in chains)
    n_it = (q0 + tq + tks - 1) // tks
    if bounded:
        def slc_tiles(n):
            carry = init
            for it in range(n):
                carry = slc_body(it, carry, need_causal=(it == n - 1))
            return carry

        max_it = ks_ref.shape[2] // tks
        fin = lax.switch(n_it - 1, [functools.partial(slc_tiles, n) for n in range(1, max_it + 1)])
    else:
        fin = lax.fori_loop(0, n_it, slc_body, init)
    o_slc = [st[-1][:NSA_HD] * (1.0 / st[-1][NSA_HD:NSA_HD + 1]) for st in fin]

    span = WIN + tq
    start = pl.multiple_of(jnp.maximum(q0 - WIN, 0), tq)
    dist = (lax.broadcasted_iota(jnp.int32, (span, tq), 1)
            - lax.broadcasted_iota(jnp.int32, (span, tq), 0)) + (q0 - start)
    wbias = jnp.where((dist >= 0) & (dist < WIN), 0.0, NEG)
    wbias = tile_heads(wbias - shift if bounded else wbias)
    o_win = []
    for (g, _), q_t in zip(chains, q_ts):
        sw = _dot(kw_ref[0, g, pl.ds(start, span), :], q_t) + wbias
        ew = jnp.exp2(sw if bounded else sw - jnp.max(sw, axis=0, keepdims=True))
        ow = _dot(vwt_ref[0, g, :, pl.ds(start, span)], ew.astype(BF16))
        o_win.append(ow[:NSA_HD] * (1.0 / ow[NSA_HD:NSA_HD + 1]))

    for ci, (_, h0) in enumerate(chains):
        for i in range(ch):
            cols = slice(i * tq, (i + 1) * tq)
            head = h0 + i
            o_h = jnp.zeros((NSA_HD, tq), F32)
            for br, o_b in enumerate((o_cmp[ci], o_slc[ci], o_win[ci])):
                row = head * N_BRANCH + br
                o_h = o_h + gt_ref[0, row:row + 1, :] * o_b[:, cols]
            o_ref[0, :, head * NSA_HD:(head + 1) * NSA_HD] = o_h.T.astype(o_ref.dtype)


def _nsa(q_t, kc, vct, ks, vst, kw, vwt, gates_t, mt, bound, tq, tks, bounded):
    bsz, _, _, t = q_t.shape
    n_blk = kc.shape[2]
    n_g = NSA_KV_HEADS
    full = lambda b, i: (b, 0, 0, 0)
    k_spec = pl.BlockSpec((1, n_g, t, NSA_HD), full)
    vt_spec = pl.BlockSpec((1, n_g, VT_ROWS, t), full)
    return pl.pallas_call(
        functools.partial(_nsa_kernel, tq=tq, tks=tks, bounded=bounded),
        grid=(bsz, t // tq),
        in_specs=[pl.BlockSpec(memory_space=pltpu.SMEM),
                  pl.BlockSpec((1, NSA_HEADS, NSA_HD, tq), lambda b, i: (b, 0, 0, i)),
                  pl.BlockSpec((1, n_g, n_blk, NSA_HD), full),
                  pl.BlockSpec((1, n_g, NSA_HD, n_blk), full),
                  k_spec, vt_spec, k_spec, vt_spec,
                  pl.BlockSpec((1, GATE_ROWS, tq), lambda b, i: (b, 0, i)),
                  pl.BlockSpec(mt.shape, lambda b, i: (0, 0))],
        out_specs=pl.BlockSpec((1, tq, NSA_WIDTH), lambda b, i: (b, i, 0)),
        out_shape=jax.ShapeDtypeStruct((bsz, t, NSA_WIDTH), BF16),
        scratch_shapes=[pltpu.VMEM((n_g, mt.shape[0], tq), F32)],
        compiler_params=pltpu.CompilerParams(
            dimension_semantics=("arbitrary", "arbitrary"), vmem_limit_bytes=VMEM_LIMIT),
        name="nsa_bounded" if bounded else "nsa",
    )(bound, q_t, kc, vct, ks, vst, kw, vwt, gates_t, mt)


def _causal_conv(u, prev, cw, cb):
    tm = u.shape[0]
    g = SUBLANES
    r8 = lax.broadcasted_iota(jnp.int32, (g, u.shape[1]), 0)
    wrap1 = jnp.where(r8 == 0, prev[2 * g - 1:2 * g], pltpu.roll(u[tm - g:], 1, axis=0))
    wrap2 = jnp.where(r8 == 0, prev[g - 1:g], pltpu.roll(u[tm - 2 * g:tm - g], 1, axis=0))
    u1 = jnp.concatenate([wrap1, u[:tm - g]], axis=0)
    u2 = jnp.concatenate([wrap2, wrap1, u[:tm - 2 * g]], axis=0)
    return cb + u2 * cw[0:1] + u1 * cw[1:2] + u * cw[2:3]


def _ffn_kernel(x_ref, hg_ref, ns_ref, mod_ref, wo_ref, g2_ref, wu_ref, cw_ref, cb_ref, wd_ref,
                o_ref, carry_ref, buf_ref, h2_ref, u_ref, g_ref, acc_ref, *, fc):
    mix = _dot(hg_ref[0], wo_ref[:HG_WIDTH, :]) + _dot(ns_ref[0], wo_ref[HG_WIDTH:, :])
    x1_nat = x_ref[0] + mod_ref[0, 2:3, :] * mix
    n_lt = D_MODEL // LANES
    tm = x1_nat.shape[0]
    groups = tm // SUBLANES
    pitch = buf_ref.shape[1] // SUBLANES
    for c in range(n_lt):
        for sg in range(SUBLANES):
            buf_ref[c, sg * pitch:sg * pitch + groups] = x1_nat[sg * groups:(sg + 1) * groups,
                                                                c * LANES:(c + 1) * LANES]

    x1 = jnp.concatenate(
        [jnp.concatenate([buf_ref[c, pl.ds(j, SUBLANES, stride=pitch), :] for j in range(groups)], axis=0)
         for c in range(n_lt)], axis=1)
    y = x1 * lax.rsqrt(jnp.mean(x1 * x1, axis=-1, keepdims=True) + EPS) * g2_ref[...]
    h2_ref[...] = (y * (1.0 + mod_ref[0, 4:5, :]) + mod_ref[0, 3:4, :]).astype(h2_ref.dtype)
    acc_ref[...] = jnp.zeros_like(acc_ref)
    first = pl.program_id(1) == 0

    def up(jc, slot):
        for half in range(2):
            off = pl.multiple_of(half * D_FF + jc * fc, fc)
            u_ref[slot, half] = _dot(h2_ref[...], wu_ref[:, pl.ds(off, fc)])

    def act(jc, slot):
        halves = []
        for half in range(2):
            cols = pl.ds(pl.multiple_of(half * D_FF + jc * fc, fc), fc)
            u = u_ref[slot, half]
            prev = jnp.where(first, 0.0, carry_ref[:, cols])
            carry_ref[:, cols] = u[tm - 2 * SUBLANES:]
            halves.append(_causal_conv(u, prev, cw_ref[:, cols], cb_ref[:, cols]))
        a, v = halves
        g_ref[slot] = (_silu(a) * v).astype(g_ref.dtype)

    def down(jc, slot):
        acc_ref[...] += _dot(g_ref[slot], wd_ref[pl.ds(pl.multiple_of(jc * fc, fc), fc), :])

    n = D_FF // fc

    def tick(i):
        if i < n:
            up(i, i % 2)
        if 0 <= i - 1 < n:
            act(i - 1, (i - 1) % 2)
        if 0 <= i - 2 < n:
            down(i - 2, i % 2)

    tick(0)
    tick(1)
    pairs = (n - 2) // 2

    def body(k, carry):
        i = 2 + 2 * k
        up(i, 0)
        act(i - 1, 1)
        down(i - 2, 0)
        up(i + 1, 1)
        act(i, 0)
        down(i - 1, 1)
        return carry

    lax.fori_loop(0, pairs, body, 0)
    for i in range(2 + 2 * pairs, n + 2):
        tick(i)

    out = x1 + mod_ref[0, 5:6, :] * acc_ref[...]
    for c in range(n_lt):
        for j in range(groups):
            buf_ref[c, pl.ds(j, SUBLANES, stride=pitch), :] = out[j * SUBLANES:(j + 1) * SUBLANES,
                                                                  c * LANES:(c + 1) * LANES]
    for c in range(n_lt):
        for sg in range(SUBLANES):
            o_ref[0, sg * groups:(sg + 1) * groups, c * LANES:(c + 1) * LANES] = (
                buf_ref[c, sg * pitch:sg * pitch + groups])


def _ffn(x, o_hg, o_nsa, mod, w_out, g2, w_up, conv_w, conv_b, w_down, tm, fc):
    bsz, t, _ = x.shape
    row_spec = lambda w: pl.BlockSpec((1, tm, w), lambda b, i: (b, i, 0))
    groups = tm // SUBLANES
    pad = SUBLANES if (groups // SUBLANES) % 2 == 0 else 0
    resident = lambda a: pl.BlockSpec(a.shape, lambda b, i: (0, 0), pipeline_mode=pl.Buffered(1))
    return pl.pallas_call(
        functools.partial(_ffn_kernel, fc=fc),
        grid=(bsz, t // tm),
        in_specs=[row_spec(D_MODEL), row_spec(HG_WIDTH), row_spec(NSA_WIDTH),
                  pl.BlockSpec((1, 6, D_MODEL), lambda b, i: (b, 0, 0)),
                  resident(w_out), resident(g2), resident(w_up), resident(conv_w), resident(conv_b),
                  resident(w_down)],
        out_specs=row_spec(D_MODEL),
        out_shape=jax.ShapeDtypeStruct(x.shape, F32),
        scratch_shapes=[pltpu.VMEM((2 * SUBLANES, 2 * D_FF), F32),
                        pltpu.VMEM((D_MODEL // LANES, tm + SUBLANES * pad, LANES), F32),
                        pltpu.VMEM((tm, D_MODEL), BF16),
                        pltpu.VMEM((2, 2, tm, fc), F32),
                        pltpu.VMEM((2, tm, fc), BF16),
                        pltpu.VMEM((tm, D_MODEL), F32)],
        compiler_params=pltpu.CompilerParams(
            dimension_semantics=("arbitrary", "arbitrary"), vmem_limit_bytes=VMEM_LIMIT),
        name="ffn",
    )(x, o_hg, o_nsa, mod, w_out, g2, w_up, conv_w, conv_b, w_down)


def _rope_tables():
    half = ROPE_DIM // 2
    inv = ROPE_THETA ** (-jnp.arange(half, dtype=F32) * 2.0 / ROPE_DIM)
    return (jnp.tile(inv.reshape(half, 1), (1, LANES)),)


def _gain_t(g):
    return jnp.tile(g.reshape(NSA_HD, 1), (LANES // NSA_HD, LANES))


def _selection_tables(t):
    n_seg = t // CMP_STRIDE
    nb = t // SLC_BLOCK
    cst = np.arange(n_seg) * CMP_STRIDE
    sst = np.arange(nb) * SLC_BLOCK
    ovl = np.clip(np.minimum(cst[:, None] + CMP_BLOCK, sst[None] + SLC_BLOCK)
                  - np.maximum(cst[:, None], sst[None]), 0, None) / CMP_BLOCK
    ovl[(t - CMP_BLOCK) // CMP_STRIDE + 1:] = 0.0
    return (jnp.asarray(ovl.T, dtype=F32),)


def _layer(x, mod, pos_row, l, p, tables):
    bsz, t, _ = x.shape
    inv_t, mt = tables
    w_in_p = jnp.pad(p["w_in"][l], ((0, 0), (0, IN_COLS_PAD - IN_COLS))).astype(BF16)
    qg_t = _gain_t(p["q_norm_g"][l])
    kg_t = jnp.stack([_gain_t(p["k_norm_g"][l, br]) for br in range(N_BRANCH)])
    zhg, q_t, kc, vc, ks, vst, kw, vwt, gates_t = _inproj(
        x, mod, p["norm1_g"][l].reshape(1, D_MODEL), w_in_p, pos_row, inv_t, qg_t, kg_t, tm=512)

    o_hg = _hgrn(zhg, p["lb_logits"], p["hg_norm_g"][l].reshape(1, HG_HD), l, tb=512)

    n_seg = t // CMP_STRIDE
    seg_w = CMP_STRIDE * NSA_HD
    pe2 = p["pe_cmp"][l].reshape(2, 2, seg_w)
    kcmp, vcmp_t = _compress(kc.reshape(bsz, NSA_KV_HEADS, n_seg, seg_w),
                             vc.reshape(bsz, NSA_KV_HEADS, n_seg, seg_w),
                             pe2, p["w_cmp1"][l].astype(BF16), p["w_cmp2"][l].astype(BF16))
    bound = (SCORE_BOUND_MARGIN * NSA_HD ** 0.5 * jnp.max(jnp.abs(p["q_norm_g"][l]))
             * jnp.max(jnp.abs(p["k_norm_g"][l, 1:]))).reshape(1).astype(F32)
    nsa_args = (q_t, kcmp, vcmp_t, ks, vst, kw, vwt, gates_t, mt, bound)
    o_nsa = lax.cond(bound[0] <= MAX_SCORE_BOUND,
                     lambda a: _nsa(*a, tq=128, tks=512, bounded=True),
                     lambda a: _nsa(*a, tq=128, tks=512, bounded=False), nsa_args)

    return _ffn(x, o_hg, o_nsa, mod, p["w_out"][l].astype(BF16), p["norm2_g"][l].reshape(1, D_MODEL),
                p["w_up"][l].astype(BF16), p["conv_w"][l], p["conv_b"][l].reshape(1, 2 * D_FF),
                p["w_down"][l].astype(BF16), tm=512, fc=256)


def kernel(x, c, positions, w_ada, b_ada, norm1_g, w_in, lb_logits, hg_norm_g, q_norm_g, k_norm_g, pe_cmp, w_cmp1, w_cmp2, w_out, norm2_g, w_up, conv_w, conv_b, w_down):
    p = dict(w_in=w_in, norm1_g=norm1_g, lb_logits=lb_logits, hg_norm_g=hg_norm_g, q_norm_g=q_norm_g,
             k_norm_g=k_norm_g, pe_cmp=pe_cmp, w_cmp1=w_cmp1, w_cmp2=w_cmp2, w_out=w_out,
             norm2_g=norm2_g, w_up=w_up, conv_w=conv_w, conv_b=conv_b, w_down=w_down)
    bsz, t, _ = x.shape
    tables = _rope_tables() + _selection_tables(t)
    pos_row = positions.reshape(bsz, 1, t)
    for l in range(w_ada.shape[0]):
        mod = _ada(c, w_ada[l], b_ada[l]).reshape(bsz, 6, D_MODEL)
        x = _layer(x, mod, pos_row, l, p, tables)
    return x
```

```python
import functools

import jax
import jax.numpy as jnp
import numpy as np
from jax import lax
from jax.experimental import pallas as pl
from jax.experimental.pallas import tpu as pltpu

D_MODEL = 1024
HG_HEADS = 4
HG_HD = 128
HG_WIDTH = HG_HEADS * HG_HD
HG_CHUNK = 64
HG_SUB = 8
LOG2E = 1.4426950408889634
NSA_HEADS = 8
NSA_KV_HEADS = 2
NSA_HD = 64
NSA_GROUP = NSA_HEADS // NSA_KV_HEADS
NSA_CHAIN = 4
NSA_WIDTH = NSA_HEADS * NSA_HD
N_BRANCH = 3
CMP_BLOCK = 32
CMP_STRIDE = 16
CMP_HIDDEN = 256
SLC_BLOCK = 64
SLC_TOPK = 16
WIN = 512
ROPE_DIM = NSA_HD // 4
ROPE_THETA = 500000.0
D_FF = 2816
CONV_W = 3
EPS = 1e-6
NEG = -1e30
SCORE_BOUND_MARGIN = 1.02
MAX_SCORE_BOUND = 40.0

LANES = 128
SUBLANES = 8
VMEM_LIMIT = 56 * 1024 * 1024

OFF_HG = 0
OFF_Q = 4 * HG_WIDTH
OFF_KV = OFF_Q + NSA_WIDTH
OFF_G = OFF_KV + 6 * NSA_KV_HEADS * NSA_HD
IN_COLS = OFF_G + N_BRANCH * NSA_HEADS
IN_COLS_PAD = OFF_G + LANES
GATE_ROWS = 32
VT_ROWS = NSA_HD + SUBLANES

BF16 = jnp.bfloat16
F32 = jnp.float32


def _dot(a, b):
    return jnp.dot(a, b, preferred_element_type=F32)


def _dot_nt(a, b):
    return lax.dot_general(a, b, (((1,), (1,)), ((), ())), preferred_element_type=F32)


def _sigmoid(x):
    return 1.0 / (1.0 + jnp.exp(-x))


def _silu(x):
    return x * _sigmoid(x)


def _ada_kernel(c_ref, w_ref, b_ref, o_ref):
    cs = _silu(c_ref[...])
    o_ref[...] = _dot(cs.astype(BF16), w_ref[...].astype(BF16)) + b_ref[...]


def _ada(c, w, b):
    bsz = c.shape[0]
    n = w.shape[1]
    tn = D_MODEL
    return pl.pallas_call(
        _ada_kernel,
        grid=(n // tn,),
        in_specs=[pl.BlockSpec((bsz, D_MODEL), lambda j: (0, 0)),
                  pl.BlockSpec((D_MODEL, tn), lambda j: (0, j)),
                  pl.BlockSpec((1, tn), lambda j: (0, j))],
        out_specs=pl.BlockSpec((bsz, tn), lambda j: (0, j)),
        out_shape=jax.ShapeDtypeStruct((bsz, n), F32),
        name="ada",
    )(c, w, b.reshape(1, n))


def _pair_norm_rope_t(xt, g_t, cos_t, sin_t):
    half = ROPE_DIM // 2
    outs = []
    for hh in range(2):
        x = xt[hh * NSA_HD:(hh + 1) * NSA_HD]
        ms = jnp.mean(x * x, axis=0, keepdims=True)
        xn = x * lax.rsqrt(ms + EPS) * g_t[hh * NSA_HD:(hh + 1) * NSA_HD]
        x1, x2 = xn[:half], xn[half:ROPE_DIM]
        outs += [x1 * cos_t - x2 * sin_t, x2 * cos_t + x1 * sin_t, xn[ROPE_DIM:]]
    return jnp.concatenate(outs, axis=0)


def _inproj_kernel(x_ref, mod_ref, g1_ref, w_ref, pos_ref, inv_ref, qg_ref, kg_ref,
                   zhg_ref, qt_ref, kc_ref, vc_ref, ks_ref, vst_ref, kw_ref, vwt_ref, gt_ref):
    x = x_ref[0]
    ms = jnp.mean(x * x, axis=-1, keepdims=True)
    y = x * lax.rsqrt(ms + EPS) * g1_ref[...]
    h = (y * (1.0 + mod_ref[0, 1:2, :]) + mod_ref[0, 0:1, :]).astype(BF16)

    zhg_ref[0] = _dot(h, w_ref[:, OFF_HG:OFF_Q])

    tm = x.shape[0]
    reps = tm // LANES

    def lane_tile(a):
        return jnp.concatenate([a] * reps, axis=1)

    ang = lane_tile(inv_ref[...]) * pos_ref[0].astype(F32)
    cos_t = jnp.cos(ang)
    sin_t = jnp.sin(ang)

    zq = _dot(h, w_ref[:, OFF_Q:OFF_KV])
    scale = NSA_HD ** -0.5 * LOG2E
    qg_t = lane_tile(qg_ref[...])
    for p in range(NSA_HEADS // 2):
        rt = _pair_norm_rope_t(zq[:, p * LANES:(p + 1) * LANES].T, qg_t, cos_t, sin_t)
        rt = (rt * scale).astype(qt_ref.dtype)
        qt_ref[0, 2 * p] = rt[:NSA_HD]
        qt_ref[0, 2 * p + 1] = rt[NSA_HD:]

    zkv = _dot(h, w_ref[:, OFF_KV:OFF_G])
    k_outs = (kc_ref, ks_ref, kw_ref)
    for br in range(N_BRANCH):
        kt = _pair_norm_rope_t(zkv[:, (2 * br) * LANES:(2 * br + 1) * LANES].T,
                               lane_tile(kg_ref[br]), cos_t, sin_t)
        kk = kt.T.astype(k_outs[br].dtype)
        for g in range(NSA_KV_HEADS):
            k_outs[br][0, g] = kk[:, g * NSA_HD:(g + 1) * NSA_HD]
    vc = zkv[:, LANES:2 * LANES]
    for g in range(NSA_KV_HEADS):
        vc_ref[0, g] = vc[:, g * NSA_HD:(g + 1) * NSA_HD]
    for br, vt_ref in ((1, vst_ref), (2, vwt_ref)):
        vt = zkv[:, (2 * br + 1) * LANES:(2 * br + 2) * LANES].T.astype(vt_ref.dtype)
        ones_row = jnp.where(lax.broadcasted_iota(jnp.int32, (SUBLANES, tm), 0) == 0, 1.0, 0.0).astype(vt_ref.dtype)
        for g in range(NSA_KV_HEADS):
            vt_ref[0, g, :NSA_HD] = vt[g * NSA_HD:(g + 1) * NSA_HD]
            vt_ref[0, g, NSA_HD:] = ones_row

    gates = _sigmoid(_dot(h, w_ref[:, OFF_G:IN_COLS_PAD]))
    gt_ref[0] = gates.T[:GATE_ROWS]


def _inproj(x, mod, g1, w_in_p, pos_row, inv_t, qg_t, kg_t, tm):
    bsz, t, _ = x.shape
    grid = (bsz, t // tm)
    kv_shape = (bsz, NSA_KV_HEADS, t, NSA_HD)
    kv_spec = pl.BlockSpec((1, NSA_KV_HEADS, tm, NSA_HD), lambda b, i: (b, 0, i, 0))
    vt_shape = (bsz, NSA_KV_HEADS, VT_ROWS, t)
    vt_spec = pl.BlockSpec((1, NSA_KV_HEADS, VT_ROWS, tm), lambda b, i: (b, 0, 0, i))
    const = lambda b, i: (0, 0)
    return pl.pallas_call(
        _inproj_kernel,
        grid=grid,
        in_specs=[pl.BlockSpec((1, tm, D_MODEL), lambda b, i: (b, i, 0)),
                  pl.BlockSpec((1, 6, D_MODEL), lambda b, i: (b, 0, 0)),
                  pl.BlockSpec((1, D_MODEL), const),
                  pl.BlockSpec((D_MODEL, IN_COLS_PAD), const),
                  pl.BlockSpec((1, 1, tm), lambda b, i: (b, 0, i)),
                  pl.BlockSpec(inv_t.shape, const),
                  pl.BlockSpec(qg_t.shape, const),
                  pl.BlockSpec(kg_t.shape, lambda b, i: (0, 0, 0))],
        out_specs=[pl.BlockSpec((1, tm, 4 * HG_WIDTH), lambda b, i: (b, i, 0)),
                   pl.BlockSpec((1, NSA_HEADS, NSA_HD, tm), lambda b, i: (b, 0, 0, i)),
                   kv_spec, kv_spec, kv_spec, vt_spec, kv_spec, vt_spec,
                   pl.BlockSpec((1, GATE_ROWS, tm), lambda b, i: (b, 0, i))],
        out_shape=[jax.ShapeDtypeStruct((bsz, t, 4 * HG_WIDTH), F32),
                   jax.ShapeDtypeStruct((bsz, NSA_HEADS, NSA_HD, t), BF16),
                   jax.ShapeDtypeStruct(kv_shape, F32),
                   jax.ShapeDtypeStruct(kv_shape, F32),
                   jax.ShapeDtypeStruct(kv_shape, BF16),
                   jax.ShapeDtypeStruct(vt_shape, BF16),
                   jax.ShapeDtypeStruct(kv_shape, BF16),
                   jax.ShapeDtypeStruct(vt_shape, BF16),
                   jax.ShapeDtypeStruct((bsz, GATE_ROWS, t), F32)],
        compiler_params=pltpu.CompilerParams(
            dimension_semantics=("arbitrary", "arbitrary"), vmem_limit_bytes=VMEM_LIMIT),
        name="inproj",
    )(x, mod, g1, w_in_p, pos_row, inv_t, qg_t, kg_t)


def _hgrn_chunk(zq, zf, zi, zg, lb, hg_g, st_ref, bk_ref, h, tri, level):
    c = HG_CHUNK
    e_z = jnp.exp(-jnp.abs(zf))
    logsig = jnp.minimum(zf, 0.0) - jnp.log(1.0 + e_z)
    a = jnp.log(lb)
    bb = jnp.log1p(-lb) + logsig
    logf = jnp.maximum(a, bb) + jnp.log(1.0 + jnp.exp(-jnp.abs(a - bb)))
    k = (1.0 - lb) * (jnp.where(zf >= 0.0, e_z, 1.0) / (1.0 + e_z))
    q = _silu(zq)
    v = zi
    l_hi = logf.astype(BF16)
    l_mid = (logf - l_hi.astype(F32)).astype(BF16)
    l_lo = (logf - l_hi.astype(F32) - l_mid.astype(F32)).astype(BF16)
    bc = _dot(tri, l_hi) + _dot(tri, l_mid) + _dot(tri, l_lo)

    col = lax.broadcasted_iota(jnp.int32, (HG_SUB, c), 1)
    b2 = bc * LOG2E
    bk_ref[0, h] = b2
    bk_ref[1, h] = k
    rows_a = []
    for i in range(c // HG_SUB):
        lo = i * HG_SUB
        b_i = b2[lo:lo + HG_SUB]
        q_i = q[lo:lo + HG_SUB]
        k_i = k[lo:lo + HG_SUB]
        a_i = jnp.zeros((HG_SUB, c), F32)
        for s in range(HG_SUB):
            b_s = bk_ref[0, h, lo + s:lo + s + 1, :]
            k_s = bk_ref[1, h, lo + s:lo + s + 1, :]
            e = jnp.exp2(jnp.minimum(b_i - b_s, 0.0)) * q_i * k_s
            a_i = jnp.where(col == lo + s, jnp.sum(e, axis=-1, keepdims=True), a_i)
        rows_a.append(a_i)
    amat = jnp.concatenate(rows_a, axis=0)

    size, idx = c // 2, 1
    while size >= HG_SUB:
        pieces = []
        for e0 in range(0, c, 2 * size):
            o0 = e0 + size
            r = bc[o0:o0 + 1]
            pieces.append(k[e0:o0] * jnp.exp(r - bc[e0:o0]))
            pieces.append(q[o0:o0 + size] * jnp.exp(bc[o0:o0 + size] - r))
        hmat = jnp.concatenate(pieces, axis=0).astype(BF16)
        amat = jnp.where(level == idx, _dot_nt(hmat, hmat), amat)
        size, idx = size // 2, idx + 1
    amat = jnp.where(level >= 0, amat, 0.0)

    st = st_ref[h]
    o = _dot_nt((q * jnp.exp(bc)).astype(BF16), st.astype(BF16)) + _dot(amat.astype(BF16), v.astype(BF16))
    bl = bc[c - 1:c]
    kdec = (k * jnp.exp(bl - bc)).astype(BF16)
    st_ref[h] = jnp.exp(bl) * st + _dot(v.T.astype(BF16), kdec)

    y = o * lax.rsqrt(jnp.mean(o * o, axis=-1, keepdims=True) + EPS) * hg_g
    return y * _silu(zg)


def _hgrn_kernel(z_ref, lbl_ref, g_ref, o_ref, st_ref, bk_ref, *, l_idx):
    @pl.when(pl.program_id(1) == 0)
    def _():
        st_ref[...] = jnp.zeros_like(st_ref)

    lg = lbl_ref[...]
    ex = jnp.exp(lg - jnp.max(lg, axis=0, keepdims=True))
    sm = ex / jnp.sum(ex, axis=0, keepdims=True)
    lb_all = jnp.sum(sm[:l_idx + 1], axis=0, keepdims=True)

    c = HG_CHUNK
    ri = lax.broadcasted_iota(jnp.int32, (c, c), 0)
    ci = lax.broadcasted_iota(jnp.int32, (c, c), 1)
    tri = jnp.where(ci <= ri, 1.0, 0.0).astype(BF16)
    level = jnp.where(ci > ri, -1, 0)
    size, idx = c // 2, 1
    while size >= HG_SUB:
        sh = size.bit_length() - 1
        paired = ((ri >> (sh + 1)) == (ci >> (sh + 1))) & ((ri >> sh) != (ci >> sh)) & (ci <= ri)
        level = jnp.where(paired, idx, level)
        size, idx = size // 2, idx + 1
    n_chunks = z_ref.shape[1] // c

    def body(ch, carry):
        r0 = pl.multiple_of(ch * c, c)
        for h in range(HG_HEADS):
            sl = slice(h * HG_HD, (h + 1) * HG_HD)
            zq = z_ref[0, pl.ds(r0, c), h * HG_HD:(h + 1) * HG_HD]
            zf = z_ref[0, pl.ds(r0, c), HG_WIDTH + h * HG_HD:HG_WIDTH + (h + 1) * HG_HD]
            zi = z_ref[0, pl.ds(r0, c), 2 * HG_WIDTH + h * HG_HD:2 * HG_WIDTH + (h + 1) * HG_HD]
            zg = z_ref[0, pl.ds(r0, c), 3 * HG_WIDTH + h * HG_HD:3 * HG_WIDTH + (h + 1) * HG_HD]
            y = _hgrn_chunk(zq, zf, zi, zg, lb_all[:, sl], g_ref[...], st_ref, bk_ref, h, tri, level)
            o_ref[0, pl.ds(r0, c), h * HG_HD:(h + 1) * HG_HD] = y.astype(o_ref.dtype)
        return carry

    lax.fori_loop(0, n_chunks, body, 0, unroll=8)


def _hgrn(zhg, lb_logits, hg_g, l_idx, tb):
    bsz, t, _ = zhg.shape
    return pl.pallas_call(
        functools.partial(_hgrn_kernel, l_idx=l_idx),
        grid=(bsz, t // tb),
        in_specs=[pl.BlockSpec((1, tb, 4 * HG_WIDTH), lambda b, i: (b, i, 0)),
                  pl.BlockSpec(lb_logits.shape, lambda b, i: (0, 0)),
                  pl.BlockSpec((1, HG_HD), lambda b, i: (0, 0))],
        out_specs=pl.BlockSpec((1, tb, HG_WIDTH), lambda b, i: (b, i, 0)),
        out_shape=jax.ShapeDtypeStruct((bsz, t, HG_WIDTH), BF16),
        scratch_shapes=[pltpu.VMEM((HG_HEADS, HG_HD, HG_HD), F32),
                        pltpu.VMEM((2, HG_HEADS, HG_CHUNK, HG_HD), F32)],
        compiler_params=pltpu.CompilerParams(
            dimension_semantics=("arbitrary", "arbitrary"), vmem_limit_bytes=VMEM_LIMIT),
        name="hgrn",
    )(zhg, lb_logits, hg_g)


def _compress_kernel(xk_ref, xv_ref, pe_ref, w1_ref, w2_ref, ko_ref, vo_ref):
    half = CMP_STRIDE * NSA_HD
    outs = []
    for kv, x_ref in enumerate((xk_ref, xv_ref)):
        x = x_ref[0, 0]
        ha = _dot((x + pe_ref[kv, 0:1, :]).astype(BF16), w1_ref[kv, :half, :])
        hb = _dot((x + pe_ref[kv, 1:2, :]).astype(BF16), w1_ref[kv, half:, :])
        n = x.shape[0]
        pre = ha + pltpu.roll(hb, n - 1, axis=0)
        outs.append(_dot(_silu(pre).astype(BF16), w2_ref[kv]))
    ko_ref[0, 0] = outs[0].astype(ko_ref.dtype)
    vo_ref[0, 0] = outs[1].T.astype(vo_ref.dtype)


def _compress(xk, xv, pe2, w1, w2):
    bsz, g, nseg, width = xk.shape
    x_spec = pl.BlockSpec((1, 1, nseg, width), lambda b, j: (b, j, 0, 0))
    return pl.pallas_call(
        _compress_kernel,
        grid=(bsz, g),
        in_specs=[x_spec, x_spec,
                  pl.BlockSpec(pe2.shape, lambda b, j: (0, 0, 0)),
                  pl.BlockSpec(w1.shape, lambda b, j: (0, 0, 0)),
                  pl.BlockSpec(w2.shape, lambda b, j: (0, 0, 0))],
        out_specs=[pl.BlockSpec((1, 1, nseg, NSA_HD), lambda b, j: (b, j, 0, 0)),
                   pl.BlockSpec((1, 1, NSA_HD, nseg), lambda b, j: (b, j, 0, 0))],
        out_shape=[jax.ShapeDtypeStruct((bsz, g, nseg, NSA_HD), BF16),
                   jax.ShapeDtypeStruct((bsz, g, NSA_HD, nseg), BF16)],
        compiler_params=pltpu.CompilerParams(
            dimension_semantics=("arbitrary", "arbitrary"), vmem_limit_bytes=VMEM_LIMIT),
        name="compress",
    )(xk, xv, pe2, w1, w2)


def _nsa_kernel(bound_ref, qt_ref, kc_ref, vct_ref, ks_ref, vst_ref, kw_ref, vwt_ref, gt_ref, mt_ref,
                o_ref, sel_ref, *, tq, tks, bounded):
    r = NSA_GROUP
    n_g = NSA_KV_HEADS
    ch = NSA_CHAIN
    lanes = ch * tq
    q0 = pl.program_id(1) * tq
    chains = [(g, g * r + c * ch) for g in range(n_g) for c in range(r // ch)]
    q_ts = [jnp.concatenate([qt_ref[0, h0 + i] for i in range(ch)], axis=1)
            for _, h0 in chains]

    def tile_heads(a):
        return jnp.concatenate([a] * ch, axis=1)

    n_blk = kc_ref.shape[2]
    nb = mt_ref.shape[0]
    blk_end = lax.broadcasted_iota(jnp.int32, (n_blk, tq), 0) * CMP_STRIDE + (CMP_BLOCK - 1)
    t_row = q0 + lax.broadcasted_iota(jnp.int32, (1, tq), 1)
    cvalid = tile_heads(jnp.where(blk_end <= t_row, 1.0, 0.0)) > 0.5
    some = tile_heads((t_row >= CMP_BLOCK - 1).astype(F32))
    j = lax.broadcasted_iota(jnp.int32, (nb, tq), 0)
    cur = jnp.right_shift(t_row, SLC_BLOCK.bit_length() - 1)
    forced = (j == 0) | (j == cur) | (j == cur - 1)
    o_cmp = []
    psum = [None] * n_g
    for (g, _), q_t in zip(chains, q_ts):
        s = jnp.where(cvalid, _dot(kc_ref[0, g], q_t), NEG)
        e = jnp.exp2(s - jnp.max(s, axis=0, keepdims=True))
        p = e * (some / jnp.sum(e, axis=0, keepdims=True))
        o_cmp.append(_dot(vct_ref[0, g], p.astype(BF16)))
        for i in range(ch):
            part = p[:, i * tq:(i + 1) * tq]
            psum[g] = part if psum[g] is None else psum[g] + part
    for g in range(n_g):
        imp = jnp.dot(mt_ref[...], psum[g], preferred_element_type=F32, precision=lax.Precision.HIGHEST)
        imp = jnp.where(j <= cur, jnp.where(forced, jnp.inf, imp), -1.0)
        rank = jnp.zeros((nb, tq), jnp.int32)
        for i in range(nb):
            row_i = imp[i:i + 1, :]
            ahead = (row_i > imp) | ((row_i == imp) & (j > i))
            rank = rank + ahead.astype(jnp.int32)
        sel_ref[g] = jnp.where(rank < min(SLC_TOPK, nb), 0.0, NEG)

    shift = bound_ref[0] * LOG2E if bounded else None

    span = WIN + tq
    start = pl.multiple_of(jnp.maximum(q0 - WIN, 0), tq)
    dist = (lax.broadcasted_iota(jnp.int32, (span, tq), 1)
            - lax.broadcasted_iota(jnp.int32, (span, tq), 0)) + (q0 - start)
    wbias = jnp.where((dist >= 0) & (dist < WIN), 0.0, NEG)
    wbias = tile_heads(wbias - shift if bounded else wbias)
    o_win = []
    for (g, _), q_t in zip(chains, q_ts):
        sw = _dot(kw_ref[0, g, pl.ds(start, span), :], q_t) + wbias
        ew = jnp.exp2(sw if bounded else sw - jnp.max(sw, axis=0, keepdims=True))
        ow = _dot(vwt_ref[0, g, :, pl.ds(start, span)], ew.astype(BF16))
        o_win.append(ow[:NSA_HD] * (1.0 / ow[NSA_HD:NSA_HD + 1]))

    blocks_per_tile = tks // SLC_BLOCK
    rel = (lax.broadcasted_iota(jnp.int32, (tks, tq), 1)
           - lax.broadcasted_iota(jnp.int32, (tks, tq), 0))

    def slc_bias(it, k0, need_causal):
        biases = []
        for g in range(n_g):
            if isinstance(it, int):
                rows = sel_ref[g, it * blocks_per_tile:(it + 1) * blocks_per_tile, :]
            else:
                rows = sel_ref[g, pl.ds(pl.multiple_of(it * blocks_per_tile, blocks_per_tile), blocks_per_tile), :]
            bias = jnp.concatenate([jnp.broadcast_to(rows[jj:jj + 1], (SLC_BLOCK, tq))
                                    for jj in range(blocks_per_tile)], axis=0)
            if need_causal:
                bias = jnp.where(rel + (q0 - k0) >= 0, bias, NEG)
            biases.append(tile_heads(bias - shift if bounded else bias))
        return biases

    def slc_body(it, carry, need_causal=True):
        k0 = it * tks if isinstance(it, int) else pl.multiple_of(it * tks, tks)
        biases = slc_bias(it, k0, need_causal)
        out = []
        for (g, _), q_t, state in zip(chains, q_ts, carry):
            sc = _dot(ks_ref[0, g, pl.ds(k0, tks), :], q_t) + biases[g]
            vt = vst_ref[0, g, :, pl.ds(k0, tks)]
            if bounded:
                (acc,) = state
                acc = acc + _dot(vt, jnp.exp2(sc).astype(BF16))
                out.append((acc,))
            else:
                m, acc = state
                m_new = jnp.maximum(m, jnp.max(sc, axis=0, keepdims=True))
                acc = jnp.exp2(m - m_new) * acc + _dot(vt, jnp.exp2(sc - m_new).astype(BF16))
                out.append((m_new, acc))
        return tuple(out)

    zero_state = (jnp.zeros((VT_ROWS, lanes), F32),)
    init = tuple(zero_state if bounded else (jnp.full((1, lanes), NEG, F32),) + zero_state for _ in chains)
    n_it = (q0 + tq + tks - 1) // tks
    if bounded:
        def slc_tiles(n):
            carry = init
            for it in range(n):
                carry = slc_body(it, carry, need_causal=(it == n - 1))
            return carry

        max_it = ks_ref.shape[2] // tks
        fin = lax.switch(n_it - 1, [functools.partial(slc_tiles, n) for n in range(1, max_it + 1)])
    else:
        fin = lax.fori_loop(0, n_it, slc_body, init)
    o_slc = [st[-1][:NSA_HD] * (1.0 / st[-1][NSA_HD:NSA_HD + 1]) for st in fin]

    for ci, (_, h0) in enumerate(chains):
        for i in range(ch):
            cols = slice(i * tq, (i + 1) * tq)
            head = h0 + i
            o_h = jnp.zeros((NSA_HD, tq), F32)
            for br, o_b in enumerate((o_cmp[ci], o_slc[ci], o_win[ci])):
                row = head * N_BRANCH + br
                o_h = o_h + gt_ref[0, row:row + 1, :] * o_b[:, cols]
            o_ref[0, :, head * NSA_HD:(head + 1) * NSA_HD] = o_h.T.astype(o_ref.dtype)


def _nsa(q_t, kc, vct, ks, vst, kw, vwt, gates_t, mt, bound, tq, tks, bounded):
    bsz, _, _, t = q_t.shape
    n_blk = kc.shape[2]
    n_g = NSA_KV_HEADS
    full = lambda b, i: (b, 0, 0, 0)
    k_spec = pl.BlockSpec((1, n_g, t, NSA_HD), full)
    vt_spec = pl.BlockSpec((1, n_g, VT_ROWS, t), full)
    return pl.pallas_call(
        functools.partial(_nsa_kernel, tq=tq, tks=tks, bounded=bounded),
        grid=(bsz, t // tq),
        in_specs=[pl.BlockSpec(memory_space=pltpu.SMEM),
                  pl.BlockSpec((1, NSA_HEADS, NSA_HD, tq), lambda b, i: (b, 0, 0, i)),
                  pl.BlockSpec((1, n_g, n_blk, NSA_HD), full),
                  pl.BlockSpec((1, n_g, NSA_HD, n_blk), full),
                  k_spec, vt_spec, k_spec, vt_spec,
                  pl.BlockSpec((1, GATE_ROWS, tq), lambda b, i: (b, 0, i)),
                  pl.BlockSpec(mt.shape, lambda b, i: (0, 0))],
        out_specs=pl.BlockSpec((1, tq, NSA_WIDTH), lambda b, i: (b, i, 0)),
        out_shape=jax.ShapeDtypeStruct((bsz, t, NSA_WIDTH), BF16),
        scratch_shapes=[pltpu.VMEM((n_g, mt.shape[0], tq), F32)],
        compiler_params=pltpu.CompilerParams(
            dimension_semantics=("arbitrary", "arbitrary"), vmem_limit_bytes=VMEM_LIMIT),
        name="nsa_bounded" if bounded else "nsa",
    )(bound, q_t, kc, vct, ks, vst, kw, vwt, gates_t, mt)


def _causal_conv(u, prev, cw, cb):
    tm = u.shape[0]
    g = SUBLANES
    r8 = lax.broadcasted_iota(jnp.int32, (g, u.shape[1]), 0)
    wrap1 = jnp.where(r8 == 0, prev[2 * g - 1:2 * g], pltpu.roll(u[tm - g:], 1, axis=0))
    wrap2 = jnp.where(r8 == 0, prev[g - 1:g], pltpu.roll(u[tm - 2 * g:tm - g], 1, axis=0))
    u1 = jnp.concatenate([wrap1, u[:tm - g]], axis=0)
    u2 = jnp.concatenate([wrap2, wrap1, u[:tm - 2 * g]], axis=0)
    return cb + u2 * cw[0:1] + u1 * cw[1:2] + u * cw[2:3]


def _ffn_kernel(x_ref, hg_ref, ns_ref, mod_ref, wo_ref, g2_ref, wu_ref, cw_ref, cb_ref, wd_ref,
                o_ref, carry_ref, buf_ref, h2_ref, u_ref, g_ref, acc_ref, *, fc):
    mix = _dot(hg_ref[0], wo_ref[:HG_WIDTH, :]) + _dot(ns_ref[0], wo_ref[HG_WIDTH:, :])
    x1_nat = x_ref[0] + mod_ref[0, 2:3, :] * mix
    n_lt = D_MODEL // LANES
    tm = x1_nat.shape[0]
    groups = tm // SUBLANES
    pitch = buf_ref.shape[1] // SUBLANES
    for c in range(n_lt):
        for sg in range(SUBLANES):
            buf_ref[c, sg * pitch:sg * pitch + groups] = x1_nat[sg * groups:(sg + 1) * groups,
                                                                c * LANES:(c + 1) * LANES]

    x1 = jnp.concatenate(
        [jnp.concatenate([buf_ref[c, pl.ds(j, SUBLANES, stride=pitch), :] for j in range(groups)], axis=0)
         for c in range(n_lt)], axis=1)
    y = x1 * lax.rsqrt(jnp.mean(x1 * x1, axis=-1, keepdims=True) + EPS) * g2_ref[...]
    h2_ref[...] = (y * (1.0 + mod_ref[0, 4:5, :]) + mod_ref[0, 3:4, :]).astype(h2_ref.dtype)
    acc_ref[...] = jnp.zeros_like(acc_ref)
    first = pl.program_id(1) == 0

    def up(jc, slot):
        for half in range(2):
            off = pl.multiple_of(half * D_FF + jc * fc, fc)
            u_ref[slot, half] = _dot(h2_ref[...], wu_ref[:, pl.ds(off, fc)])

    def act(jc, slot):
        halves = []
        for half in range(2):
            cols = pl.ds(pl.multiple_of(half * D_FF + jc * fc, fc), fc)
            u = u_ref[slot, half]
            prev = jnp.where(first, 0.0, carry_ref[:, cols])
            carry_ref[:, cols] = u[tm - 2 * SUBLANES:]
            halves.append(_causal_conv(u, prev, cw_ref[:, cols], cb_ref[:, cols]))
        a, v = halves
        g_ref[slot] = (_silu(a) * v).astype(g_ref.dtype)

    def down(jc, slot):
        acc_ref[...] += _dot(g_ref[slot], wd_ref[pl.ds(pl.multiple_of(jc * fc, fc), fc), :])

    n = D_FF // fc

    def tick(i):
        if i < n:
            up(i, i % 2)
        if 0 <= i - 1 < n:
            act(i - 1, (i - 1) % 2)
        if 0 <= i - 2 < n:
            down(i - 2, i % 2)

    for i in range(n + 2):
        tick(i)

    out = x1 + mod_ref[0, 5:6, :] * acc_ref[...]
    for c in range(n_lt):
        for j in range(groups):
            buf_ref[c, pl.ds(j, SUBLANES, stride=pitch), :] = out[j * SUBLANES:(j + 1) * SUBLANES,
                                                                  c * LANES:(c + 1) * LANES]
    for c in range(n_lt):
        for sg in range(SUBLANES):
            o_ref[0, sg * groups:(sg + 1) * groups, c * LANES:(c + 1) * LANES] = (
                buf_ref[c, sg * pitch:sg * pitch + groups])


def _ffn(x, o_hg, o_nsa, mod, w_out, g2, w_up, conv_w, conv_b, w_down, tm, fc):
    bsz, t, _ = x.shape
    row_spec = lambda w: pl.BlockSpec((1, tm, w), lambda b, i: (b, i, 0))
    groups = tm // SUBLANES
    pad = SUBLANES if (groups // SUBLANES) % 2 == 0 else 0
    resident = lambda a: pl.BlockSpec(a.shape, lambda b, i: (0, 0), pipeline_mode=pl.Buffered(1))
    return pl.pallas_call(
        functools.partial(_ffn_kernel, fc=fc),
        grid=(bsz, t // tm),
        in_specs=[row_spec(D_MODEL), row_spec(HG_WIDTH), row_spec(NSA_WIDTH),
                  pl.BlockSpec((1, 6, D_MODEL), lambda b, i: (b, 0, 0)),
                  resident(w_out), resident(g2), resident(w_up), resident(conv_w), resident(conv_b),
                  resident(w_down)],
        out_specs=row_spec(D_MODEL),
        out_shape=jax.ShapeDtypeStruct(x.shape, F32),
        scratch_shapes=[pltpu.VMEM((2 * SUBLANES, 2 * D_FF), F32),
                        pltpu.VMEM((D_MODEL // LANES, tm + SUBLANES * pad, LANES), F32),
                        pltpu.VMEM((tm, D_MODEL), BF16),
                        pltpu.VMEM((2, 2, tm, fc), F32),
                        pltpu.VMEM((2, tm, fc), BF16),
                        pltpu.VMEM((tm, D_MODEL), F32)],
        compiler_params=pltpu.CompilerParams(
            dimension_semantics=("arbitrary", "arbitrary"), vmem_limit_bytes=VMEM_LIMIT),
        name="ffn",
    )(x, o_hg, o_nsa, mod, w_out, g2, w_up, conv_w, conv_b, w_down)


def _rope_tables():
    half = ROPE_DIM // 2
    inv = ROPE_THETA ** (-jnp.arange(half, dtype=F32) * 2.0 / ROPE_DIM)
    return (jnp.tile(inv.reshape(half, 1), (1, LANES)),)


def _gain_t(g):
    return jnp.tile(g.reshape(NSA_HD, 1), (LANES // NSA_HD, LANES))


def _selection_tables(t):
    n_seg = t // CMP_STRIDE
    nb = t // SLC_BLOCK
    cst = np.arange(n_seg) * CMP_STRIDE
    sst = np.arange(nb) * SLC_BLOCK
    ovl = np.clip(np.minimum(cst[:, None] + CMP_BLOCK, sst[None] + SLC_BLOCK)
                  - np.maximum(cst[:, None], sst[None]), 0, None) / CMP_BLOCK
    ovl[(t - CMP_BLOCK) // CMP_STRIDE + 1:] = 0.0
    return (jnp.asarray(ovl.T, dtype=F32),)


def _layer(x, mod, pos_row, l, p, tables):
    bsz, t, _ = x.shape
    inv_t, mt = tables
    w_in_p = jnp.pad(p["w_in"][l], ((0, 0), (0, IN_COLS_PAD - IN_COLS))).astype(BF16)
    qg_t = _gain_t(p["q_norm_g"][l])
    kg_t = jnp.stack([_gain_t(p["k_norm_g"][l, br]) for br in range(N_BRANCH)])
    zhg, q_t, kc, vc, ks, vst, kw, vwt, gates_t = _inproj(
        x, mod, p["norm1_g"][l].reshape(1, D_MODEL), w_in_p, pos_row, inv_t, qg_t, kg_t, tm=512)

    o_hg = _hgrn(zhg, p["lb_logits"], p["hg_norm_g"][l].reshape(1, HG_HD), l, tb=512)

    n_seg = t // CMP_STRIDE
    seg_w = CMP_STRIDE * NSA_HD
    pe2 = p["pe_cmp"][l].reshape(2, 2, seg_w)
    kcmp, vcmp_t = _compress(kc.reshape(bsz, NSA_KV_HEADS, n_seg, seg_w),
                             vc.reshape(bsz, NSA_KV_HEADS, n_seg, seg_w),
                             pe2, p["w_cmp1"][l].astype(BF16), p["w_cmp2"][l].astype(BF16))
    bound = (SCORE_BOUND_MARGIN * NSA_HD ** 0.5 * jnp.max(jnp.abs(p["q_norm_g"][l]))
             * jnp.max(jnp.abs(p["k_norm_g"][l, 1:]))).reshape(1).astype(F32)
    nsa_args = (q_t, kcmp, vcmp_t, ks, vst, kw, vwt, gates_t, mt, bound)
    o_nsa = lax.cond(bound[0] <= MAX_SCORE_BOUND,
                     lambda a: _nsa(*a, tq=128, tks=512, bounded=True),
                     lambda a: _nsa(*a, tq=128, tks=512, bounded=False), nsa_args)

    return _ffn(x, o_hg, o_nsa, mod, p["w_out"][l].astype(BF16), p["norm2_g"][l].reshape(1, D_MODEL),
                p["w_up"][l].astype(BF16), p["conv_w"][l], p["conv_b"][l].reshape(1, 2 * D_FF),
                p["w_down"][l].astype(BF16), tm=512, fc=256)


def kernel(x, c, positions, w_ada, b_ada, norm1_g, w_in, lb_logits, hg_norm_g, q_norm_g, k_norm_g, pe_cmp, w_cmp1, w_cmp2, w_out, norm2_g, w_up, conv_w, conv_b, w_down):
    p = dict(w_in=w_in, norm1_g=norm1_g, lb_logits=lb_logits, hg_norm_g=hg_norm_g, q_norm_g=q_norm_g,
             k_norm_g=k_norm_g, pe_cmp=pe_cmp, w_cmp1=w_cmp1, w_cmp2=w_cmp2, w_out=w_out,
             norm2_g=norm2_g, w_up=w_up, conv_w=conv_w, conv_b=conv_b, w_down=w_down)
    bsz, t, _ = x.shape
    tables = _rope_tables() + _selection_tables(t)
    pos_row = positions.reshape(bsz, 1, t)
    for l in range(w_ada.shape[0]):
        mod = _ada(c, w_ada[l], b_ada[l]).reshape(bsz, 6, D_MODEL)
        x = _layer(x, mod, pos_row, l, p, tables)
    return x
```

```python
import functools

import jax
import jax.numpy as jnp
import numpy as np
from jax import lax
from jax.experimental import pallas as pl
from jax.experimental.pallas import tpu as pltpu

D_MODEL = 1024
HG_HEADS = 4
HG_HD = 128
HG_WIDTH = HG_HEADS * HG_HD
HG_CHUNK = 64
HG_SUB = 8
LOG2E = 1.4426950408889634
NSA_HEADS = 8
NSA_KV_HEADS = 2
NSA_HD = 64
NSA_GROUP = NSA_HEADS // NSA_KV_HEADS
NSA_CHAIN = 4
RANK_LANES = 4
NSA_WIDTH = NSA_HEADS * NSA_HD
N_BRANCH = 3
CMP_BLOCK = 32
CMP_STRIDE = 16
CMP_HIDDEN = 256
SLC_BLOCK = 64
SLC_TOPK = 16
WIN = 512
ROPE_DIM = NSA_HD // 4
ROPE_THETA = 500000.0
D_FF = 2816
CONV_W = 3
EPS = 1e-6
NEG = -1e30
SCORE_BOUND_MARGIN = 1.02
MAX_SCORE_BOUND = 40.0

LANES = 128
SUBLANES = 8
VMEM_LIMIT = 56 * 1024 * 1024

OFF_HG = 0
OFF_Q = 4 * HG_WIDTH
OFF_KV = OFF_Q + NSA_WIDTH
OFF_G = OFF_KV + 6 * NSA_KV_HEADS * NSA_HD
IN_COLS = OFF_G + N_BRANCH * NSA_HEADS
IN_COLS_PAD = OFF_G + LANES
GATE_ROWS = 32
VT_ROWS = NSA_HD + SUBLANES

BF16 = jnp.bfloat16
F32 = jnp.float32


def _dot(a, b):
    return jnp.dot(a, b, preferred_element_type=F32)


def _dot_nt(a, b):
    return lax.dot_general(a, b, (((1,), (1,)), ((), ())), preferred_element_type=F32)


def _sigmoid(x):
    return 1.0 / (1.0 + jnp.exp(-x))


def _silu(x):
    return x * _sigmoid(x)


def _ada_kernel(c_ref, w_ref, b_ref, o_ref):
    cs = _silu(c_ref[...])
    o_ref[...] = _dot(cs.astype(BF16), w_ref[...].astype(BF16)) + b_ref[...]


def _ada(c, w, b):
    bsz = c.shape[0]
    n = w.shape[1]
    tn = D_MODEL
    return pl.pallas_call(
        _ada_kernel,
        grid=(n // tn,),
        in_specs=[pl.BlockSpec((bsz, D_MODEL), lambda j: (0, 0)),
                  pl.BlockSpec((D_MODEL, tn), lambda j: (0, j)),
                  pl.BlockSpec((1, tn), lambda j: (0, j))],
        out_specs=pl.BlockSpec((bsz, tn), lambda j: (0, j)),
        out_shape=jax.ShapeDtypeStruct((bsz, n), F32),
        name="ada",
    )(c, w, b.reshape(1, n))


def _pair_norm_rope_t(xt, g_t, cos_t, sin_t):
    half = ROPE_DIM // 2
    outs = []
    for hh in range(2):
        x = xt[hh * NSA_HD:(hh + 1) * NSA_HD]
        ms = jnp.mean(x * x, axis=0, keepdims=True)
        xn = x * lax.rsqrt(ms + EPS) * g_t[hh * NSA_HD:(hh + 1) * NSA_HD]
        x1, x2 = xn[:half], xn[half:ROPE_DIM]
        outs += [x1 * cos_t - x2 * sin_t, x2 * cos_t + x1 * sin_t, xn[ROPE_DIM:]]
    return jnp.concatenate(outs, axis=0)


def _inproj_kernel(x_ref, mod_ref, g1_ref, w_ref, pos_ref, inv_ref, qg_ref, kg_ref,
                   zhg_ref, qt_ref, kc_ref, vc_ref, ks_ref, vst_ref, kw_ref, vwt_ref, gt_ref):
    x = x_ref[0]
    ms = jnp.mean(x * x, axis=-1, keepdims=True)
    y = x * lax.rsqrt(ms + EPS) * g1_ref[...]
    h = (y * (1.0 + mod_ref[0, 1:2, :]) + mod_ref[0, 0:1, :]).astype(BF16)

    zhg_ref[0] = _dot(h, w_ref[:, OFF_HG:OFF_Q])

    tm = x.shape[0]
    reps = tm // LANES

    def lane_tile(a):
        return jnp.concatenate([a] * reps, axis=1)

    ang = lane_tile(inv_ref[...]) * pos_ref[0].astype(F32)
    cos_t = jnp.cos(ang)
    sin_t = jnp.sin(ang)

    zq = _dot(h, w_ref[:, OFF_Q:OFF_KV])
    scale = NSA_HD ** -0.5 * LOG2E
    qg_t = lane_tile(qg_ref[...])
    for p in range(NSA_HEADS // 2):
        rt = _pair_norm_rope_t(zq[:, p * LANES:(p + 1) * LANES].T, qg_t, cos_t, sin_t)
        rt = (rt * scale).astype(qt_ref.dtype)
        qt_ref[0, 2 * p] = rt[:NSA_HD]
        qt_ref[0, 2 * p + 1] = rt[NSA_HD:]

    zkv = _dot(h, w_ref[:, OFF_KV:OFF_G])
    k_outs = (kc_ref, ks_ref, kw_ref)
    for br in range(N_BRANCH):
        kt = _pair_norm_rope_t(zkv[:, (2 * br) * LANES:(2 * br + 1) * LANES].T,
                               lane_tile(kg_ref[br]), cos_t, sin_t)
        kk = kt.T.astype(k_outs[br].dtype)
        for g in range(NSA_KV_HEADS):
            k_outs[br][0, g] = kk[:, g * NSA_HD:(g + 1) * NSA_HD]
    vc = zkv[:, LANES:2 * LANES]
    for g in range(NSA_KV_HEADS):
        vc_ref[0, g] = vc[:, g * NSA_HD:(g + 1) * NSA_HD]
    for br, vt_ref in ((1, vst_ref), (2, vwt_ref)):
        vt = zkv[:, (2 * br + 1) * LANES:(2 * br + 2) * LANES].T.astype(vt_ref.dtype)
        ones_row = jnp.where(lax.broadcasted_iota(jnp.int32, (SUBLANES, tm), 0) == 0, 1.0, 0.0).astype(vt_ref.dtype)
        for g in range(NSA_KV_HEADS):
            vt_ref[0, g, :NSA_HD] = vt[g * NSA_HD:(g + 1) * NSA_HD]
            vt_ref[0, g, NSA_HD:] = ones_row

    gates = _sigmoid(_dot(h, w_ref[:, OFF_G:IN_COLS_PAD]))
    gt_ref[0] = gates.T[:GATE_ROWS]


def _inproj(x, mod, g1, w_in_p, pos_row, inv_t, qg_t, kg_t, tm):
    bsz, t, _ = x.shape
    grid = (bsz, t // tm)
    kv_shape = (bsz, NSA_KV_HEADS, t, NSA_HD)
    kv_spec = pl.BlockSpec((1, NSA_KV_HEADS, tm, NSA_HD), lambda b, i: (b, 0, i, 0))
    vt_shape = (bsz, NSA_KV_HEADS, VT_ROWS, t)
    vt_spec = pl.BlockSpec((1, NSA_KV_HEADS, VT_ROWS, tm), lambda b, i: (b, 0, 0, i))
    const = lambda b, i: (0, 0)
    return pl.pallas_call(
        _inproj_kernel,
        grid=grid,
        in_specs=[pl.BlockSpec((1, tm, D_MODEL), lambda b, i: (b, i, 0)),
                  pl.BlockSpec((1, 6, D_MODEL), lambda b, i: (b, 0, 0)),
                  pl.BlockSpec((1, D_MODEL), const),
                  pl.BlockSpec((D_MODEL, IN_COLS_PAD), const),
                  pl.BlockSpec((1, 1, tm), lambda b, i: (b, 0, i)),
                  pl.BlockSpec(inv_t.shape, const),
                  pl.BlockSpec(qg_t.shape, const),
                  pl.BlockSpec(kg_t.shape, lambda b, i: (0, 0, 0))],
        out_specs=[pl.BlockSpec((1, tm, 4 * HG_WIDTH), lambda b, i: (b, i, 0)),
                   pl.BlockSpec((1, NSA_HEADS, NSA_HD, tm), lambda b, i: (b, 0, 0, i)),
                   kv_spec, kv_spec, kv_spec, vt_spec, kv_spec, vt_spec,
                   pl.BlockSpec((1, GATE_ROWS, tm), lambda b, i: (b, 0, i))],
        out_shape=[jax.ShapeDtypeStruct((bsz, t, 4 * HG_WIDTH), F32),
                   jax.ShapeDtypeStruct((bsz, NSA_HEADS, NSA_HD, t), BF16),
                   jax.ShapeDtypeStruct(kv_shape, F32),
                   jax.ShapeDtypeStruct(kv_shape, F32),
                   jax.ShapeDtypeStruct(kv_shape, BF16),
                   jax.ShapeDtypeStruct(vt_shape, BF16),
                   jax.ShapeDtypeStruct(kv_shape, BF16),
                   jax.ShapeDtypeStruct(vt_shape, BF16),
                   jax.ShapeDtypeStruct((bsz, GATE_ROWS, t), F32)],
        compiler_params=pltpu.CompilerParams(
            dimension_semantics=("arbitrary", "arbitrary"), vmem_limit_bytes=VMEM_LIMIT),
        name="inproj",
    )(x, mod, g1, w_in_p, pos_row, inv_t, qg_t, kg_t)


def _hgrn_chunk(zq, zf, zi, zg, lb, hg_g, st_ref, bk_ref, h, tri, level):
    c = HG_CHUNK
    e_z = jnp.exp(-jnp.abs(zf))
    logsig = jnp.minimum(zf, 0.0) - jnp.log(1.0 + e_z)
    a = jnp.log(lb)
    bb = jnp.log1p(-lb) + logsig
    logf = jnp.maximum(a, bb) + jnp.log(1.0 + jnp.exp(-jnp.abs(a - bb)))
    k = (1.0 - lb) * (jnp.where(zf >= 0.0, e_z, 1.0) / (1.0 + e_z))
    q = _silu(zq)
    v = zi
    l_hi = logf.astype(BF16)
    l_mid = (logf - l_hi.astype(F32)).astype(BF16)
    l_lo = (logf - l_hi.astype(F32) - l_mid.astype(F32)).astype(BF16)
    bc = _dot(tri, l_hi) + _dot(tri, l_mid) + _dot(tri, l_lo)

    col = lax.broadcasted_iota(jnp.int32, (HG_SUB, c), 1)
    b2 = bc * LOG2E
    bk_ref[0, h] = b2
    bk_ref[1, h] = k
    rows_a = []
    for i in range(c // HG_SUB):
        lo = i * HG_SUB
        b_i = b2[lo:lo + HG_SUB]
        q_i = q[lo:lo + HG_SUB]
        k_i = k[lo:lo + HG_SUB]
        a_i = jnp.zeros((HG_SUB, c), F32)
        for s in range(HG_SUB):
            b_s = bk_ref[0, h, lo + s:lo + s + 1, :]
            k_s = bk_ref[1, h, lo + s:lo + s + 1, :]
            e = jnp.exp2(jnp.minimum(b_i - b_s, 0.0)) * q_i * k_s
            a_i = jnp.where(col == lo + s, jnp.sum(e, axis=-1, keepdims=True), a_i)
        rows_a.append(a_i)
    amat = jnp.concatenate(rows_a, axis=0)

    size, idx = c // 2, 1
    while size >= HG_SUB:
        pieces = []
        for e0 in range(0, c, 2 * size):
            o0 = e0 + size
            r = bc[o0:o0 + 1]
            pieces.append(k[e0:o0] * jnp.exp(r - bc[e0:o0]))
            pieces.append(q[o0:o0 + size] * jnp.exp(bc[o0:o0 + size] - r))
        hmat = jnp.concatenate(pieces, axis=0).astype(BF16)
        amat = jnp.where(level == idx, _dot_nt(hmat, hmat), amat)
        size, idx = size // 2, idx + 1
    amat = jnp.where(level >= 0, amat, 0.0)

    st = st_ref[h]
    o = _dot_nt((q * jnp.exp(bc)).astype(BF16), st.astype(BF16)) + _dot(amat.astype(BF16), v.astype(BF16))
    bl = bc[c - 1:c]
    kdec = (k * jnp.exp(bl - bc)).astype(BF16)
    st_ref[h] = jnp.exp(bl) * st + _dot(v.T.astype(BF16), kdec)

    y = o * lax.rsqrt(jnp.mean(o * o, axis=-1, keepdims=True) + EPS) * hg_g
    return y * _silu(zg)


def _hgrn_kernel(z_ref, lbl_ref, g_ref, o_ref, st_ref, bk_ref, *, l_idx):
    @pl.when(pl.program_id(1) == 0)
    def _():
        st_ref[...] = jnp.zeros_like(st_ref)

    lg = lbl_ref[...]
    ex = jnp.exp(lg - jnp.max(lg, axis=0, keepdims=True))
    sm = ex / jnp.sum(ex, axis=0, keepdims=True)
    lb_all = jnp.sum(sm[:l_idx + 1], axis=0, keepdims=True)

    c = HG_CHUNK
    ri = lax.broadcasted_iota(jnp.int32, (c, c), 0)
    ci = lax.broadcasted_iota(jnp.int32, (c, c), 1)
    tri = jnp.where(ci <= ri, 1.0, 0.0).astype(BF16)
    level = jnp.where(ci > ri, -1, 0)
    size, idx = c // 2, 1
    while size >= HG_SUB:
        sh = size.bit_length() - 1
        paired = ((ri >> (sh + 1)) == (ci >> (sh + 1))) & ((ri >> sh) != (ci >> sh)) & (ci <= ri)
        level = jnp.where(paired, idx, level)
        size, idx = size // 2, idx + 1
    n_chunks = z_ref.shape[1] // c

    def body(ch, carry):
        r0 = pl.multiple_of(ch * c, c)
        for h in range(HG_HEADS):
            sl = slice(h * HG_HD, (h + 1) * HG_HD)
            zq = z_ref[0, pl.ds(r0, c), h * HG_HD:(h + 1) * HG_HD]
            zf = z_ref[0, pl.ds(r0, c), HG_WIDTH + h * HG_HD:HG_WIDTH + (h + 1) * HG_HD]
            zi = z_ref[0, pl.ds(r0, c), 2 * HG_WIDTH + h * HG_HD:2 * HG_WIDTH + (h + 1) * HG_HD]
            zg = z_ref[0, pl.ds(r0, c), 3 * HG_WIDTH + h * HG_HD:3 * HG_WIDTH + (h + 1) * HG_HD]
            y = _hgrn_chunk(zq, zf, zi, zg, lb_all[:, sl], g_ref[...], st_ref, bk_ref, h, tri, level)
            o_ref[0, pl.ds(r0, c), h * HG_HD:(h + 1) * HG_HD] = y.astype(o_ref.dtype)
        return carry

    lax.fori_loop(0, n_chunks, body, 0, unroll=8)


def _hgrn(zhg, lb_logits, hg_g, l_idx, tb):
    bsz, t, _ = zhg.shape
    return pl.pallas_call(
        functools.partial(_hgrn_kernel, l_idx=l_idx),
        grid=(bsz, t // tb),
        in_specs=[pl.BlockSpec((1, tb, 4 * HG_WIDTH), lambda b, i: (b, i, 0)),
                  pl.BlockSpec(lb_logits.shape, lambda b, i: (0, 0)),
                  pl.BlockSpec((1, HG_HD), lambda b, i: (0, 0))],
        out_specs=pl.BlockSpec((1, tb, HG_WIDTH), lambda b, i: (b, i, 0)),
        out_shape=jax.ShapeDtypeStruct((bsz, t, HG_WIDTH), BF16),
        scratch_shapes=[pltpu.VMEM((HG_HEADS, HG_HD, HG_HD), F32),
                        pltpu.VMEM((2, HG_HEADS, HG_CHUNK, HG_HD), F32)],
        compiler_params=pltpu.CompilerParams(
            dimension_semantics=("arbitrary", "arbitrary"), vmem_limit_bytes=VMEM_LIMIT),
        name="hgrn",
    )(zhg, lb_logits, hg_g)


def _compress_kernel(xk_ref, xv_ref, pe_ref, w1_ref, w2_ref, ko_ref, vo_ref):
    half = CMP_STRIDE * NSA_HD
    outs = []
    for kv, x_ref in enumerate((xk_ref, xv_ref)):
        x = x_ref[0, 0]
        ha = _dot((x + pe_ref[kv, 0:1, :]).astype(BF16), w1_ref[kv, :half, :])
        hb = _dot((x + pe_ref[kv, 1:2, :]).astype(BF16), w1_ref[kv, half:, :])
        n = x.shape[0]
        pre = ha + pltpu.roll(hb, n - 1, axis=0)
        outs.append(_dot(_silu(pre).astype(BF16), w2_ref[kv]))
    ko_ref[0, 0] = outs[0].astype(ko_ref.dtype)
    vo_ref[0, 0] = outs[1].T.astype(vo_ref.dtype)


def _compress(xk, xv, pe2, w1, w2):
    bsz, g, nseg, width = xk.shape
    x_spec = pl.BlockSpec((1, 1, nseg, width), lambda b, j: (b, j, 0, 0))
    return pl.pallas_call(
        _compress_kernel,
        grid=(bsz, g),
        in_specs=[x_spec, x_spec,
                  pl.BlockSpec(pe2.shape, lambda b, j: (0, 0, 0)),
                  pl.BlockSpec(w1.shape, lambda b, j: (0, 0, 0)),
                  pl.BlockSpec(w2.shape, lambda b, j: (0, 0, 0))],
        out_specs=[pl.BlockSpec((1, 1, nseg, NSA_HD), lambda b, j: (b, j, 0, 0)),
                   pl.BlockSpec((1, 1, NSA_HD, nseg), lambda b, j: (b, j, 0, 0))],
        out_shape=[jax.ShapeDtypeStruct((bsz, g, nseg, NSA_HD), BF16),
                   jax.ShapeDtypeStruct((bsz, g, NSA_HD, nseg), BF16)],
        compiler_params=pltpu.CompilerParams(
            dimension_semantics=("arbitrary", "arbitrary"), vmem_limit_bytes=VMEM_LIMIT),
        name="compress",
    )(xk, xv, pe2, w1, w2)


def _nsa_kernel(bound_ref, qt_ref, kc_ref, vct_ref, ks_ref, vst_ref, kw_ref, vwt_ref, gt_ref, mt_ref,
                o_ref, sel_ref, *, tq, tks, bounded):
    r = NSA_GROUP
    n_g = NSA_KV_HEADS
    ch = NSA_CHAIN
    lanes = ch * tq
    q0 = pl.program_id(1) * tq
    chains = [(g, g * r + c * ch) for g in range(n_g) for c in range(r // ch)]
    q_ts = [jnp.concatenate([qt_ref[0, h0 + i] for i in range(ch)], axis=1)
            for _, h0 in chains]

    def tile_heads(a):
        return jnp.concatenate([a] * ch, axis=1)

    n_blk = kc_ref.shape[2]
    nb = mt_ref.shape[0]
    blk_end = lax.broadcasted_iota(jnp.int32, (n_blk, tq), 0) * CMP_STRIDE + (CMP_BLOCK - 1)
    t_row = q0 + lax.broadcasted_iota(jnp.int32, (1, tq), 1)
    cvalid = tile_heads(jnp.where(blk_end <= t_row, 1.0, 0.0)) > 0.5
    some = tile_heads((t_row >= CMP_BLOCK - 1).astype(F32))
    j = lax.broadcasted_iota(jnp.int32, (nb, tq), 0)
    cur = jnp.right_shift(t_row, SLC_BLOCK.bit_length() - 1)
    forced = (j == 0) | (j == cur) | (j == cur - 1)
    o_cmp = []
    psum = [None] * n_g
    for (g, _), q_t in zip(chains, q_ts):
        s = jnp.where(cvalid, _dot(kc_ref[0, g], q_t), NEG)
        e = jnp.exp2(s - jnp.max(s, axis=0, keepdims=True))
        p = e * (some / jnp.sum(e, axis=0, keepdims=True))
        o_cmp.append(_dot(vct_ref[0, g], p.astype(BF16)))
        for i in range(ch):
            part = p[:, i * tq:(i + 1) * tq]
            psum[g] = part if psum[g] is None else psum[g] + part
    for g in range(n_g):
        p_hi = psum[g].astype(BF16)
        p_lo = (psum[g] - p_hi.astype(F32)).astype(BF16)
        imp = _dot(mt_ref[...], p_hi) + _dot(mt_ref[...], p_lo)
        imp = jnp.where(j <= cur, jnp.where(forced, jnp.inf, imp), -1.0)
        ranks = [jnp.zeros((nb, tq), jnp.int32) for _ in range(RANK_LANES)]
        for i in range(nb):
            row_i = imp[i:i + 1, :]
            ahead = (row_i > imp) | ((row_i == imp) & (j > i))
            ranks[i % RANK_LANES] = ranks[i % RANK_LANES] + ahead.astype(jnp.int32)
        rank = functools.reduce(lambda a, b: a + b, ranks)
        sel_ref[g] = jnp.where(rank < min(SLC_TOPK, nb), 0.0, NEG)

    shift = bound_ref[0] * LOG2E if bounded else None

    span = WIN + tq
    start = pl.multiple_of(jnp.maximum(q0 - WIN, 0), tq)
    dist = (lax.broadcasted_iota(jnp.int32, (span, tq), 1)
            - lax.broadcasted_iota(jnp.int32, (span, tq), 0)) + (q0 - start)
    wbias = jnp.where((dist >= 0) & (dist < WIN), 0.0, NEG)
    wbias = tile_heads(wbias - shift if bounded else wbias)
    o_win = []
    for (g, _), q_t in zip(chains, q_ts):
        sw = _dot(kw_ref[0, g, pl.ds(start, span), :], q_t) + wbias
        ew = jnp.exp2(sw if bounded else sw - jnp.max(sw, axis=0, keepdims=True))
        ow = _dot(vwt_ref[0, g, :, pl.ds(start, span)], ew.astype(BF16))
        o_win.append(ow[:NSA_HD] * (1.0 / ow[NSA_HD:NSA_HD + 1]))

    blocks_per_tile = tks // SLC_BLOCK
    rel = (lax.broadcasted_iota(jnp.int32, (tks, tq), 1)
           - lax.broadcasted_iota(jnp.int32, (tks, tq), 0))

    def slc_bias(it, k0, need_causal):
        biases = []
        for g in range(n_g):
            if isinstance(it, int):
                rows = sel_ref[g, it * blocks_per_tile:(it + 1) * blocks_per_tile, :]
            else:
                rows = sel_ref[g, pl.ds(pl.multiple_of(it * blocks_per_tile, blocks_per_tile), blocks_per_tile), :]
            bias = jnp.concatenate([jnp.broadcast_to(rows[jj:jj + 1], (SLC_BLOCK, tq))
                                    for jj in range(blocks_per_tile)], axis=0)
            if need_causal:
                bias = jnp.where(rel + (q0 - k0) >= 0, bias, NEG)
            biases.append(tile_heads(bias - shift if bounded else bias))
        return biases

    def slc_body(it, carry, need_causal=True):
        k0 = it * tks if isinstance(it, int) else pl.multiple_of(it * tks, tks)
        biases = slc_bias(it, k0, need_causal)
        out = []
        for (g, _), q_t, state in zip(chains, q_ts, carry):
            sc = _dot(ks_ref[0, g, pl.ds(k0, tks), :], q_t) + biases[g]
            vt = vst_ref[0, g, :, pl.ds(k0, tks)]
            if bounded:
                (acc,) = state
                acc = acc + _dot(vt, jnp.exp2(sc).astype(BF16))
                out.append((acc,))
            else:
                m, acc = state
                m_new = jnp.maximum(m, jnp.max(sc, axis=0, keepdims=True))
                acc = jnp.exp2(m - m_new) * acc + _dot(vt, jnp.exp2(sc - m_new).astype(BF16))
                out.append((m_new, acc))
        return tuple(out)

    zero_state = (jnp.zeros((VT_ROWS, lanes), F32),)
    init = tuple(zero_state if bounded else (jnp.full((1, lanes), NEG, F32),) + zero_state for _ in chains)
    n_it = (q0 + tq + tks - 1) // tks
    if bounded:
        def slc_tiles(n):
            carry = init
            for it in range(n):
                carry = slc_body(it, carry, need_causal=(it == n - 1))
            return carry

        max_it = ks_ref.shape[2] // tks
        fin = lax.switch(n_it - 1, [functools.partial(slc_tiles, n) for n in range(1, max_it + 1)])
    else:
        fin = lax.fori_loop(0, n_it, slc_body, init)
    o_slc = [st[-1][:NSA_HD] * (1.0 / st[-1][NSA_HD:NSA_HD + 1]) for st in fin]

    for ci, (_, h0) in enumerate(chains):
        for i in range(ch):
            cols = slice(i * tq, (i + 1) * tq)
            head = h0 + i
            o_h = jnp.zeros((NSA_HD, tq), F32)
            for br, o_b in enumerate((o_cmp[ci], o_slc[ci], o_win[ci])):
                row = head * N_BRANCH + br
                o_h = o_h + gt_ref[0, row:row + 1, :] * o_b[:, cols]
            o_ref[0, :, head * NSA_HD:(head + 1) * NSA_HD] = o_h.T.astype(o_ref.dtype)


def _nsa(q_t, kc, vct, ks, vst, kw, vwt, gates_t, mt, bound, tq, tks, bounded):
    bsz, _, _, t = q_t.shape
    n_blk = kc.shape[2]
    n_g = NSA_KV_HEADS
    full = lambda b, i: (b, 0, 0, 0)
    k_spec = pl.BlockSpec((1, n_g, t, NSA_HD), full)
    vt_spec = pl.BlockSpec((1, n_g, VT_ROWS, t), full)
    return pl.pallas_call(
        functools.partial(_nsa_kernel, tq=tq, tks=tks, bounded=bounded),
        grid=(bsz, t // tq),
        in_specs=[pl.BlockSpec(memory_space=pltpu.SMEM),
                  pl.BlockSpec((1, NSA_HEADS, NSA_HD, tq), lambda b, i: (b, 0, 0, i)),
                  pl.BlockSpec((1, n_g, n_blk, NSA_HD), full),
                  pl.BlockSpec((1, n_g, NSA_HD, n_blk), full),
                  k_spec, vt_spec, k_spec, vt_spec,
                  pl.BlockSpec((1, GATE_ROWS, tq), lambda b, i: (b, 0, i)),
                  pl.BlockSpec(mt.shape, lambda b, i: (0, 0))],
        out_specs=pl.BlockSpec((1, tq, NSA_WIDTH), lambda b, i: (b, i, 0)),
        out_shape=jax.ShapeDtypeStruct((bsz, t, NSA_WIDTH), BF16),
        scratch_shapes=[pltpu.VMEM((n_g, mt.shape[0], tq), F32)],
        compiler_params=pltpu.CompilerParams(
            dimension_semantics=("arbitrary", "arbitrary"), vmem_limit_bytes=VMEM_LIMIT),
        name="nsa_bounded" if bounded else "nsa",
    )(bound, q_t, kc, vct, ks, vst, kw, vwt, gates_t, mt)


def _causal_conv(u, prev, cw, cb):
    tm = u.shape[0]
    g = SUBLANES
    r8 = lax.broadcasted_iota(jnp.int32, (g, u.shape[1]), 0)
    wrap1 = jnp.where(r8 == 0, prev[2 * g - 1:2 * g], pltpu.roll(u[tm - g:], 1, axis=0))
    wrap2 = jnp.where(r8 == 0, prev[g - 1:g], pltpu.roll(u[tm - 2 * g:tm - g], 1, axis=0))
    u1 = jnp.concatenate([wrap1, u[:tm - g]], axis=0)
    u2 = jnp.concatenate([wrap2, wrap1, u[:tm - 2 * g]], axis=0)
    return cb + u2 * cw[0:1] + u1 * cw[1:2] + u * cw[2:3]


def _ffn_kernel(x_ref, hg_ref, ns_ref, mod_ref, wo_ref, g2_ref, wu_ref, cw_ref, cb_ref, wd_ref,
                o_ref, carry_ref, buf_ref, h2_ref, u_ref, g_ref, acc_ref, *, fc):
    mix = _dot(hg_ref[0], wo_ref[:HG_WIDTH, :]) + _dot(ns_ref[0], wo_ref[HG_WIDTH:, :])
    x1_nat = x_ref[0] + mod_ref[0, 2:3, :] * mix
    n_lt = D_MODEL // LANES
    tm = x1_nat.shape[0]
    groups = tm // SUBLANES
    pitch = buf_ref.shape[1] // SUBLANES
    for c in range(n_lt):
        for sg in range(SUBLANES):
            buf_ref[c, sg * pitch:sg * pitch + groups] = x1_nat[sg * groups:(sg + 1) * groups,
                                                                c * LANES:(c + 1) * LANES]

    x1 = jnp.concatenate(
        [jnp.concatenate([buf_ref[c, pl.ds(j, SUBLANES, stride=pitch), :] for j in range(groups)], axis=0)
         for c in range(n_lt)], axis=1)
    y = x1 * lax.rsqrt(jnp.mean(x1 * x1, axis=-1, keepdims=True) + EPS) * g2_ref[...]
    h2_ref[...] = (y * (1.0 + mod_ref[0, 4:5, :]) + mod_ref[0, 3:4, :]).astype(h2_ref.dtype)
    acc_ref[...] = jnp.zeros_like(acc_ref)
    first = pl.program_id(1) == 0

    def up(jc, slot):
        for half in range(2):
            off = pl.multiple_of(half * D_FF + jc * fc, fc)
            u_ref[slot, half] = _dot(h2_ref[...], wu_ref[:, pl.ds(off, fc)])

    def act(jc, slot):
        halves = []
        for half in range(2):
            cols = pl.ds(pl.multiple_of(half * D_FF + jc * fc, fc), fc)
            u = u_ref[slot, half]
            prev = jnp.where(first, 0.0, carry_ref[:, cols])
            carry_ref[:, cols] = u[tm - 2 * SUBLANES:]
            halves.append(_causal_conv(u, prev, cw_ref[:, cols], cb_ref[:, cols]))
        a, v = halves
        g_ref[slot] = (_silu(a) * v).astype(g_ref.dtype)

    def down(jc, slot):
        acc_ref[...] += _dot(g_ref[slot], wd_ref[pl.ds(pl.multiple_of(jc * fc, fc), fc), :])

    n = D_FF // fc

    def tick(i):
        if i < n:
            up(i, i % 2)
        if 0 <= i - 1 < n:
            act(i - 1, (i - 1) % 2)
        if 0 <= i - 2 < n:
            down(i - 2, i % 2)

    for i in range(n + 2):
        tick(i)

    out = x1 + mod_ref[0, 5:6, :] * acc_ref[...]
    for c in range(n_lt):
        for j in range(groups):
            buf_ref[c, pl.ds(j, SUBLANES, stride=pitch), :] = out[j * SUBLANES:(j + 1) * SUBLANES,
                                                                  c * LANES:(c + 1) * LANES]
    for c in range(n_lt):
        for sg in range(SUBLANES):
            o_ref[0, sg * groups:(sg + 1) * groups, c * LANES:(c + 1) * LANES] = (
                buf_ref[c, sg * pitch:sg * pitch + groups])


def _ffn(x, o_hg, o_nsa, mod, w_out, g2, w_up, conv_w, conv_b, w_down, tm, fc):
    bsz, t, _ = x.shape
    row_spec = lambda w: pl.BlockSpec((1, tm, w), lambda b, i: (b, i, 0))
    groups = tm // SUBLANES
    pad = SUBLANES if (groups // SUBLANES) % 2 == 0 else 0
    resident = lambda a: pl.BlockSpec(a.shape, lambda b, i: (0, 0), pipeline_mode=pl.Buffered(1))
    return pl.pallas_call(
        functools.partial(_ffn_kernel, fc=fc),
        grid=(bsz, t // tm),
        in_specs=[row_spec(D_MODEL), row_spec(HG_WIDTH), row_spec(NSA_WIDTH),
                  pl.BlockSpec((1, 6, D_MODEL), lambda b, i: (b, 0, 0)),
                  resident(w_out), resident(g2), resident(w_up), resident(conv_w), resident(conv_b),
                  resident(w_down)],
        out_specs=row_spec(D_MODEL),
        out_shape=jax.ShapeDtypeStruct(x.shape, F32),
        scratch_shapes=[pltpu.VMEM((2 * SUBLANES, 2 * D_FF), F32),
                        pltpu.VMEM((D_MODEL // LANES, tm + SUBLANES * pad, LANES), F32),
                        pltpu.VMEM((tm, D_MODEL), BF16),
                        pltpu.VMEM((2, 2, tm, fc), F32),
                        pltpu.VMEM((2, tm, fc), BF16),
                        pltpu.VMEM((tm, D_MODEL), F32)],
        compiler_params=pltpu.CompilerParams(
            dimension_semantics=("arbitrary", "arbitrary"), vmem_limit_bytes=VMEM_LIMIT),
        name="ffn",
    )(x, o_hg, o_nsa, mod, w_out, g2, w_up, conv_w, conv_b, w_down)


def _rope_tables():
    half = ROPE_DIM // 2
    inv = ROPE_THETA ** (-jnp.arange(half, dtype=F32) * 2.0 / ROPE_DIM)
    return (jnp.tile(inv.reshape(half, 1), (1, LANES)),)


def _gain_t(g):
    return jnp.tile(g.reshape(NSA_HD, 1), (LANES // NSA_HD, LANES))


def _selection_tables(t):
    n_seg = t // CMP_STRIDE
    nb = t // SLC_BLOCK
    cst = np.arange(n_seg) * CMP_STRIDE
    sst = np.arange(nb) * SLC_BLOCK
    ovl = np.clip(np.minimum(cst[:, None] + CMP_BLOCK, sst[None] + SLC_BLOCK)
                  - np.maximum(cst[:, None], sst[None]), 0, None) / CMP_BLOCK
    ovl[(t - CMP_BLOCK) // CMP_STRIDE + 1:] = 0.0
    return (jnp.asarray(ovl.T, dtype=BF16),)


def _layer(x, mod, pos_row, l, p, tables):
    bsz, t, _ = x.shape
    inv_t, mt = tables
    w_in_p = jnp.pad(p["w_in"][l], ((0, 0), (0, IN_COLS_PAD - IN_COLS))).astype(BF16)
    qg_t = _gain_t(p["q_norm_g"][l])
    kg_t = jnp.stack([_gain_t(p["k_norm_g"][l, br]) for br in range(N_BRANCH)])
    zhg, q_t, kc, vc, ks, vst, kw, vwt, gates_t = _inproj(
        x, mod, p["norm1_g"][l].reshape(1, D_MODEL), w_in_p, pos_row, inv_t, qg_t, kg_t, tm=512)

    o_hg = _hgrn(zhg, p["lb_logits"], p["hg_norm_g"][l].reshape(1, HG_HD), l, tb=512)

    n_seg = t // CMP_STRIDE
    seg_w = CMP_STRIDE * NSA_HD
    pe2 = p["pe_cmp"][l].reshape(2, 2, seg_w)
    kcmp, vcmp_t = _compress(kc.reshape(bsz, NSA_KV_HEADS, n_seg, seg_w),
                             vc.reshape(bsz, NSA_KV_HEADS, n_seg, seg_w),
                             pe2, p["w_cmp1"][l].astype(BF16), p["w_cmp2"][l].astype(BF16))
    bound = (SCORE_BOUND_MARGIN * NSA_HD ** 0.5 * jnp.max(jnp.abs(p["q_norm_g"][l]))
             * jnp.max(jnp.abs(p["k_norm_g"][l, 1:]))).reshape(1).astype(F32)
    nsa_args = (q_t, kcmp, vcmp_t, ks, vst, kw, vwt, gates_t, mt, bound)
    o_nsa = lax.cond(bound[0] <= MAX_SCORE_BOUND,
                     lambda a: _nsa(*a, tq=256, tks=512, bounded=True),
                     lambda a: _nsa(*a, tq=256, tks=512, bounded=False), nsa_args)

    return _ffn(x, o_hg, o_nsa, mod, p["w_out"][l].astype(BF16), p["norm2_g"][l].reshape(1, D_MODEL),
                p["w_up"][l].astype(BF16), p["conv_w"][l], p["conv_b"][l].reshape(1, 2 * D_FF),
                p["w_down"][l].astype(BF16), tm=512, fc=256)


def kernel(x, c, positions, w_ada, b_ada, norm1_g, w_in, lb_logits, hg_norm_g, q_norm_g, k_norm_g, pe_cmp, w_cmp1, w_cmp2, w_out, norm2_g, w_up, conv_w, conv_b, w_down):
    p = dict(w_in=w_in, norm1_g=norm1_g, lb_logits=lb_logits, hg_norm_g=hg_norm_g, q_norm_g=q_norm_g,
             k_norm_g=k_norm_g, pe_cmp=pe_cmp, w_cmp1=w_cmp1, w_cmp2=w_cmp2, w_out=w_out,
             norm2_g=norm2_g, w_up=w_up, conv_w=conv_w, conv_b=conv_b, w_down=w_down)
    bsz, t, _ = x.shape
    tables = _rope_tables() + _selection_tables(t)
    pos_row = positions.reshape(bsz, 1, t)
    for l in range(w_ada.shape[0]):
        mod = _ada(c, w_ada[l], b_ada[l]).reshape(bsz, 6, D_MODEL)
        x = _layer(x, mod, pos_row, l, p, tables)
    return x
```

```python
import functools

import jax
import jax.numpy as jnp
import numpy as np
from jax import lax
from jax.experimental import pallas as pl
from jax.experimental.pallas import tpu as pltpu

D_MODEL = 1024
HG_HEADS = 4
HG_HD = 128
HG_WIDTH = HG_HEADS * HG_HD
HG_CHUNK = 64
HG_SUB = 8
LOG2E = 1.4426950408889634
NSA_HEADS = 8
NSA_KV_HEADS = 2
NSA_HD = 64
NSA_GROUP = NSA_HEADS // NSA_KV_HEADS
NSA_CHAIN = 4
RANK_LANES = 4
NSA_WIDTH = NSA_HEADS * NSA_HD
N_BRANCH = 3
CMP_BLOCK = 32
CMP_STRIDE = 16
CMP_HIDDEN = 256
SLC_BLOCK = 64
SLC_TOPK = 16
WIN = 512
ROPE_DIM = NSA_HD // 4
ROPE_THETA = 500000.0
D_FF = 2816
CONV_W = 3
EPS = 1e-6
NEG = -1e30
SCORE_BOUND_MARGIN = 1.02
MAX_SCORE_BOUND = 40.0

LANES = 128
SUBLANES = 8
VMEM_LIMIT = 56 * 1024 * 1024

OFF_HG = 0
OFF_Q = 4 * HG_WIDTH
OFF_KV = OFF_Q + NSA_WIDTH
OFF_G = OFF_KV + 6 * NSA_KV_HEADS * NSA_HD
IN_COLS = OFF_G + N_BRANCH * NSA_HEADS
IN_COLS_PAD = OFF_G + LANES
GATE_ROWS = 32
VT_ROWS = NSA_HD + SUBLANES

BF16 = jnp.bfloat16
F32 = jnp.float32


def _dot(a, b):
    return jnp.dot(a, b, preferred_element_type=F32)


def _dot_nt(a, b):
    return lax.dot_general(a, b, (((1,), (1,)), ((), ())), preferred_element_type=F32)


def _sigmoid(x):
    return 1.0 / (1.0 + jnp.exp(-x))


def _silu(x):
    return x * _sigmoid(x)


def _ada_kernel(c_ref, w_ref, b_ref, o_ref):
    cs = _silu(c_ref[...])
    o_ref[...] = _dot(cs.astype(BF16), w_ref[...].astype(BF16)) + b_ref[...]


def _ada(c, w, b):
    bsz = c.shape[0]
    n = w.shape[1]
    tn = D_MODEL
    return pl.pallas_call(
        _ada_kernel,
        grid=(n // tn,),
        in_specs=[pl.BlockSpec((bsz, D_MODEL), lambda j: (0, 0)),
                  pl.BlockSpec((D_MODEL, tn), lambda j: (0, j)),
                  pl.BlockSpec((1, tn), lambda j: (0, j))],
        out_specs=pl.BlockSpec((bsz, tn), lambda j: (0, j)),
        out_shape=jax.ShapeDtypeStruct((bsz, n), F32),
        name="ada",
    )(c, w, b.reshape(1, n))


def _pair_norm_rope_t(xt, g_t, cos_t, sin_t):
    half = ROPE_DIM // 2
    outs = []
    for hh in range(2):
        x = xt[hh * NSA_HD:(hh + 1) * NSA_HD]
        ms = jnp.mean(x * x, axis=0, keepdims=True)
        xn = x * lax.rsqrt(ms + EPS) * g_t[hh * NSA_HD:(hh + 1) * NSA_HD]
        x1, x2 = xn[:half], xn[half:ROPE_DIM]
        outs += [x1 * cos_t - x2 * sin_t, x2 * cos_t + x1 * sin_t, xn[ROPE_DIM:]]
    return jnp.concatenate(outs, axis=0)


def _project_parts(x_ref, mod_ref, g1_ref, w_ref, pos_ref, inv_ref, qg_ref, kg_ref,
                   zhg_out, qt_ref, kc_ref, vc_ref, ks_ref, vst_ref, kw_ref, vwt_ref, gt_ref):
    x = x_ref[0]
    ms = jnp.mean(x * x, axis=-1, keepdims=True)
    y = x * lax.rsqrt(ms + EPS) * g1_ref[...]
    h = (y * (1.0 + mod_ref[0, 1:2, :]) + mod_ref[0, 0:1, :]).astype(BF16)

    tm = x.shape[0]
    reps = tm // LANES

    def lane_tile(a):
        return jnp.concatenate([a] * reps, axis=1)

    ang = lane_tile(inv_ref[...]) * pos_ref[0].astype(F32)
    cos_t = jnp.cos(ang)
    sin_t = jnp.sin(ang)
    scale = NSA_HD ** -0.5 * LOG2E
    vals = {}

    def hg_cols(lo, hi):
        def part():
            zhg_out[:, lo:hi] = _dot(h, w_ref[:, OFF_HG + lo:OFF_HG + hi])
        return part

    def q_pair(p):
        def part():
            if p % 2 == 0:
                vals["zq"] = _dot(h, w_ref[:, OFF_Q + p * LANES:OFF_Q + (p + 2) * LANES])
                vals["qg"] = lane_tile(qg_ref[...])
            off = (p % 2) * LANES
            rt = _pair_norm_rope_t(vals["zq"][:, off:off + LANES].T, vals["qg"], cos_t, sin_t)
            rt = (rt * scale).astype(qt_ref.dtype)
            qt_ref[0, 2 * p] = rt[:NSA_HD]
            qt_ref[0, 2 * p + 1] = rt[NSA_HD:]
        return part

    def kv_dot(br):
        vals["zkv"] = _dot(h, w_ref[:, OFF_KV + 2 * br * LANES:OFF_KV + (2 * br + 2) * LANES])

    def key(br, k_ref):
        def part():
            kv_dot(br)
            kt = _pair_norm_rope_t(vals["zkv"][:, :LANES].T, lane_tile(kg_ref[br]), cos_t, sin_t)
            kk = kt.T.astype(k_ref.dtype)
            for g in range(NSA_KV_HEADS):
                k_ref[0, g] = kk[:, g * NSA_HD:(g + 1) * NSA_HD]
            if br == 0:
                vc = vals["zkv"][:, LANES:]
                for g in range(NSA_KV_HEADS):
                    vc_ref[0, g] = vc[:, g * NSA_HD:(g + 1) * NSA_HD]
        return part

    def value_t(vt_ref):
        def part():
            vt = vals["zkv"][:, LANES:].T.astype(vt_ref.dtype)
            ones_row = jnp.where(lax.broadcasted_iota(jnp.int32, (SUBLANES, tm), 0) == 0, 1.0, 0.0).astype(vt_ref.dtype)
            for g in range(NSA_KV_HEADS):
                vt_ref[0, g, :NSA_HD] = vt[g * NSA_HD:(g + 1) * NSA_HD]
                vt_ref[0, g, NSA_HD:] = ones_row
        return part

    def gates_part():
        gates = _sigmoid(_dot(h, w_ref[:, OFF_G:IN_COLS_PAD]))
        gt_ref[0] = gates.T[:GATE_ROWS]

    wide = 2 * LANES
    hg = [hg_cols(lo, lo + wide) for lo in range(0, 4 * HG_WIDTH, wide)]
    rest = [q_pair(0), q_pair(1), q_pair(2), q_pair(3), key(0, kc_ref), key(1, ks_ref), value_t(vst_ref),
            key(2, kw_ref), value_t(vwt_ref), gates_part]
    parts = []
    for k in range(max(len(hg), len(rest))):
        parts += hg[k:k + 1] + rest[k:k + 1]
    return parts


def _hgrn_chunk(zq, zf, zi, zg, lb, hg_g, st_ref, bk_ref, h, tri, level):
    c = HG_CHUNK
    e_z = jnp.exp(-jnp.abs(zf))
    logsig = jnp.minimum(zf, 0.0) - jnp.log(1.0 + e_z)
    a = jnp.log(lb)
    bb = jnp.log1p(-lb) + logsig
    logf = jnp.maximum(a, bb) + jnp.log(1.0 + jnp.exp(-jnp.abs(a - bb)))
    k = (1.0 - lb) * (jnp.where(zf >= 0.0, e_z, 1.0) / (1.0 + e_z))
    q = _silu(zq)
    v = zi
    l_hi = logf.astype(BF16)
    l_mid = (logf - l_hi.astype(F32)).astype(BF16)
    l_lo = (logf - l_hi.astype(F32) - l_mid.astype(F32)).astype(BF16)
    bc = _dot(tri, l_hi) + _dot(tri, l_mid) + _dot(tri, l_lo)

    col = lax.broadcasted_iota(jnp.int32, (HG_SUB, c), 1)
    b2 = bc * LOG2E
    bk_ref[0, h] = b2
    bk_ref[1, h] = k
    rows_a = []
    for i in range(c // HG_SUB):
        lo = i * HG_SUB
        b_i = b2[lo:lo + HG_SUB]
        q_i = q[lo:lo + HG_SUB]
        k_i = k[lo:lo + HG_SUB]
        a_i = jnp.zeros((HG_SUB, c), F32)
        for s in range(HG_SUB):
            b_s = bk_ref[0, h, lo + s:lo + s + 1, :]
            k_s = bk_ref[1, h, lo + s:lo + s + 1, :]
            e = jnp.exp2(jnp.minimum(b_i - b_s, 0.0)) * q_i * k_s
            a_i = jnp.where(col == lo + s, jnp.sum(e, axis=-1, keepdims=True), a_i)
        rows_a.append(a_i)
    amat = jnp.concatenate(rows_a, axis=0)

    size, idx = c // 2, 1
    while size >= HG_SUB:
        pieces = []
        for e0 in range(0, c, 2 * size):
            o0 = e0 + size
            r = bc[o0:o0 + 1]
            pieces.append(k[e0:o0] * jnp.exp(r - bc[e0:o0]))
            pieces.append(q[o0:o0 + size] * jnp.exp(bc[o0:o0 + size] - r))
        hmat = jnp.concatenate(pieces, axis=0).astype(BF16)
        amat = jnp.where(level == idx, _dot_nt(hmat, hmat), amat)
        size, idx = size // 2, idx + 1
    amat = jnp.where(level >= 0, amat, 0.0)

    st = st_ref[h]
    o = _dot_nt((q * jnp.exp(bc)).astype(BF16), st.astype(BF16)) + _dot(amat.astype(BF16), v.astype(BF16))
    bl = bc[c - 1:c]
    kdec = (k * jnp.exp(bl - bc)).astype(BF16)
    st_ref[h] = jnp.exp(bl) * st + _dot(v.T.astype(BF16), kdec)

    y = o * lax.rsqrt(jnp.mean(o * o, axis=-1, keepdims=True) + EPS) * hg_g
    return y * _silu(zg)


def _hgrn_parts(z_in, lbl_ref, g_ref, o_ref, st_ref, bk_ref, l_idx):
    lg = lbl_ref[...]
    ex = jnp.exp(lg - jnp.max(lg, axis=0, keepdims=True))
    sm = ex / jnp.sum(ex, axis=0, keepdims=True)
    lb_all = jnp.sum(sm[:l_idx + 1], axis=0, keepdims=True)

    c = HG_CHUNK
    ri = lax.broadcasted_iota(jnp.int32, (c, c), 0)
    ci = lax.broadcasted_iota(jnp.int32, (c, c), 1)
    tri = jnp.where(ci <= ri, 1.0, 0.0).astype(BF16)
    level = jnp.where(ci > ri, -1, 0)
    size, idx = c // 2, 1
    while size >= HG_SUB:
        sh = size.bit_length() - 1
        paired = ((ri >> (sh + 1)) == (ci >> (sh + 1))) & ((ri >> sh) != (ci >> sh)) & (ci <= ri)
        level = jnp.where(paired, idx, level)
        size, idx = size // 2, idx + 1

    def chunk_head(ch, h):
        def part():
            rows = slice(ch * c, (ch + 1) * c)
            sl = slice(h * HG_HD, (h + 1) * HG_HD)
            zq = z_in[rows, h * HG_HD:(h + 1) * HG_HD]
            zf = z_in[rows, HG_WIDTH + h * HG_HD:HG_WIDTH + (h + 1) * HG_HD]
            zi = z_in[rows, 2 * HG_WIDTH + h * HG_HD:2 * HG_WIDTH + (h + 1) * HG_HD]
            zg = z_in[rows, 3 * HG_WIDTH + h * HG_HD:3 * HG_WIDTH + (h + 1) * HG_HD]
            y = _hgrn_chunk(zq, zf, zi, zg, lb_all[:, sl], g_ref[...], st_ref, bk_ref, h, tri, level)
            o_ref[0, rows, h * HG_HD:(h + 1) * HG_HD] = y.astype(o_ref.dtype)
        return part

    return [chunk_head(ch, h) for ch in range(z_in.shape[0] // c) for h in range(HG_HEADS)]


def _mix_kernel(x_ref, mod_ref, g1_ref, w_ref, pos_ref, inv_ref, qg_ref, kg_ref, lbl_ref, hgg_ref,
                qt_ref, kc_ref, vc_ref, ks_ref, vst_ref, kw_ref, vwt_ref, gt_ref, ohg_ref,
                zbuf_ref, st_ref, bk_ref, *, l_idx, tiles_per_seq):
    j = pl.program_id(0)

    @pl.when(j == 0)
    def _():
        zbuf_ref[0] = jnp.zeros((zbuf_ref.shape[1], zbuf_ref.shape[2]), F32)

    @pl.when((j == 0) | (lax.rem(j + tiles_per_seq - 1, tiles_per_seq) == 0))
    def _():
        st_ref[...] = jnp.zeros_like(st_ref)

    zbuf_ref[1] = zbuf_ref[0]
    rec = _hgrn_parts(zbuf_ref.at[1], lbl_ref, hgg_ref, ohg_ref, st_ref, bk_ref, l_idx)
    proj = _project_parts(x_ref, mod_ref, g1_ref, w_ref, pos_ref, inv_ref, qg_ref, kg_ref, zbuf_ref.at[0],
                          qt_ref, kc_ref, vc_ref, ks_ref, vst_ref, kw_ref, vwt_ref, gt_ref)
    for part in rec + proj:
        part()


def _mix(x, mod, g1, w_in_p, pos_row, inv_t, qg_t, kg_t, lb_logits, hg_g, l_idx, tm):
    bsz, t, _ = x.shape
    nm = t // tm
    n_tiles = bsz * nm

    def cur(j):
        jc = jnp.minimum(j, n_tiles - 1)
        return jc // nm, jc % nm

    def prev(j):
        jp = jnp.maximum(j - 1, 0)
        return jp // nm, jp % nm

    kv_shape = (bsz, NSA_KV_HEADS, t, NSA_HD)
    kv_spec = pl.BlockSpec((1, NSA_KV_HEADS, tm, NSA_HD), lambda j: (cur(j)[0], 0, cur(j)[1], 0))
    vt_shape = (bsz, NSA_KV_HEADS, VT_ROWS, t)
    vt_spec = pl.BlockSpec((1, NSA_KV_HEADS, VT_ROWS, tm), lambda j: (cur(j)[0], 0, 0, cur(j)[1]))
    const = lambda j: (0, 0)
    return pl.pallas_call(
        functools.partial(_mix_kernel, l_idx=l_idx, tiles_per_seq=nm),
        grid=(n_tiles + 1,),
        in_specs=[pl.BlockSpec((1, tm, D_MODEL), lambda j: (cur(j)[0], cur(j)[1], 0)),
                  pl.BlockSpec((1, 6, D_MODEL), lambda j: (cur(j)[0], 0, 0)),
                  pl.BlockSpec((1, D_MODEL), const),
                  pl.BlockSpec((D_MODEL, IN_COLS_PAD), const),
                  pl.BlockSpec((1, 1, tm), lambda j: (cur(j)[0], 0, cur(j)[1])),
                  pl.BlockSpec(inv_t.shape, const),
                  pl.BlockSpec(qg_t.shape, const),
                  pl.BlockSpec(kg_t.shape, lambda j: (0, 0, 0)),
                  pl.BlockSpec(lb_logits.shape, const),
                  pl.BlockSpec((1, HG_HD), const)],
        out_specs=[pl.BlockSpec((1, NSA_HEADS, NSA_HD, tm), lambda j: (cur(j)[0], 0, 0, cur(j)[1])),
                   kv_spec, kv_spec, kv_spec, vt_spec, kv_spec, vt_spec,
                   pl.BlockSpec((1, GATE_ROWS, tm), lambda j: (cur(j)[0], 0, cur(j)[1])),
                   pl.BlockSpec((1, tm, HG_WIDTH), lambda j: (prev(j)[0], prev(j)[1], 0))],
        out_shape=[jax.ShapeDtypeStruct((bsz, NSA_HEADS, NSA_HD, t), BF16),
                   jax.ShapeDtypeStruct(kv_shape, F32),
                   jax.ShapeDtypeStruct(kv_shape, F32),
                   jax.ShapeDtypeStruct(kv_shape, BF16),
                   jax.ShapeDtypeStruct(vt_shape, BF16),
                   jax.ShapeDtypeStruct(kv_shape, BF16),
                   jax.ShapeDtypeStruct(vt_shape, BF16),
                   jax.ShapeDtypeStruct((bsz, GATE_ROWS, t), F32),
                   jax.ShapeDtypeStruct((bsz, t, HG_WIDTH), BF16)],
        scratch_shapes=[pltpu.VMEM((2, tm, 4 * HG_WIDTH), F32),
                        pltpu.VMEM((HG_HEADS, HG_HD, HG_HD), F32),
                        pltpu.VMEM((2, HG_HEADS, HG_CHUNK, HG_HD), F32)],
        compiler_params=pltpu.CompilerParams(
            dimension_semantics=("arbitrary",), vmem_limit_bytes=VMEM_LIMIT),
        name="mix",
    )(x, mod, g1, w_in_p, pos_row, inv_t, qg_t, kg_t, lb_logits, hg_g)


def _compress_kernel(xk_ref, xv_ref, pe_ref, w1_ref, w2_ref, ko_ref, vo_ref):
    half = CMP_STRIDE * NSA_HD
    outs = []
    for kv, x_ref in enumerate((xk_ref, xv_ref)):
        x = x_ref[0, 0]
        ha = _dot((x + pe_ref[kv, 0:1, :]).astype(BF16), w1_ref[kv, :half, :])
        hb = _dot((x + pe_ref[kv, 1:2, :]).astype(BF16), w1_ref[kv, half:, :])
        n = x.shape[0]
        pre = ha + pltpu.roll(hb, n - 1, axis=0)
        outs.append(_dot(_silu(pre).astype(BF16), w2_ref[kv]))
    ko_ref[0, 0] = outs[0].astype(ko_ref.dtype)
    vo_ref[0, 0] = outs[1].T.astype(vo_ref.dtype)


def _compress(xk, xv, pe2, w1, w2):
    bsz, g, nseg, width = xk.shape
    x_spec = pl.BlockSpec((1, 1, nseg, width), lambda b, j: (b, j, 0, 0))
    return pl.pallas_call(
        _compress_kernel,
        grid=(bsz, g),
        in_specs=[x_spec, x_spec,
                  pl.BlockSpec(pe2.shape, lambda b, j: (0, 0, 0)),
                  pl.BlockSpec(w1.shape, lambda b, j: (0, 0, 0)),
                  pl.BlockSpec(w2.shape, lambda b, j: (0, 0, 0))],
        out_specs=[pl.BlockSpec((1, 1, nseg, NSA_HD), lambda b, j: (b, j, 0, 0)),
                   pl.BlockSpec((1, 1, NSA_HD, nseg), lambda b, j: (b, j, 0, 0))],
        out_shape=[jax.ShapeDtypeStruct((bsz, g, nseg, NSA_HD), BF16),
                   jax.ShapeDtypeStruct((bsz, g, NSA_HD, nseg), BF16)],
        compiler_params=pltpu.CompilerParams(
            dimension_semantics=("arbitrary", "arbitrary"), vmem_limit_bytes=VMEM_LIMIT),
        name="compress",
    )(xk, xv, pe2, w1, w2)


def _nsa_kernel(bound_ref, qt_ref, kc_ref, vct_ref, ks_ref, vst_ref, kw_ref, vwt_ref, gt_ref, mt_ref,
                o_ref, sel_ref, *, tq, tks, bounded):
    r = NSA_GROUP
    n_g = NSA_KV_HEADS
    ch = NSA_CHAIN
    lanes = ch * tq
    q0 = pl.program_id(1) * tq
    chains = [(g, g * r + c * ch) for g in range(n_g) for c in range(r // ch)]
    q_ts = [jnp.concatenate([qt_ref[0, h0 + i] for i in range(ch)], axis=1)
            for _, h0 in chains]

    def tile_heads(a):
        return jnp.concatenate([a] * ch, axis=1)

    n_blk = kc_ref.shape[2]
    nb = mt_ref.shape[0]
    blk_end = lax.broadcasted_iota(jnp.int32, (n_blk, tq), 0) * CMP_STRIDE + (CMP_BLOCK - 1)
    t_row = q0 + lax.broadcasted_iota(jnp.int32, (1, tq), 1)
    cvalid = tile_heads(jnp.where(blk_end <= t_row, 1.0, 0.0)) > 0.5
    some = tile_heads((t_row >= CMP_BLOCK - 1).astype(F32))
    j = lax.broadcasted_iota(jnp.int32, (nb, tq), 0)
    cur = jnp.right_shift(t_row, SLC_BLOCK.bit_length() - 1)
    forced = (j == 0) | (j == cur) | (j == cur - 1)
    o_cmp = []
    psum = [None] * n_g
    for (g, _), q_t in zip(chains, q_ts):
        s = jnp.where(cvalid, _dot(kc_ref[0, g], q_t), NEG)
        e = jnp.exp2(s - jnp.max(s, axis=0, keepdims=True))
        p = e * (some / jnp.sum(e, axis=0, keepdims=True))
        o_cmp.append(_dot(vct_ref[0, g], p.astype(BF16)))
        for i in range(ch):
            part = p[:, i * tq:(i + 1) * tq]
            psum[g] = part if psum[g] is None else psum[g] + part
    for g in range(n_g):
        p_hi = psum[g].astype(BF16)
        p_lo = (psum[g] - p_hi.astype(F32)).astype(BF16)
        imp = _dot(mt_ref[...], p_hi) + _dot(mt_ref[...], p_lo)
        imp = jnp.where(j <= cur, jnp.where(forced, jnp.inf, imp), -1.0)
        ranks = [jnp.zeros((nb, tq), jnp.int32) for _ in range(RANK_LANES)]
        for i in range(nb):
            row_i = imp[i:i + 1, :]
            ahead = (row_i > imp) | ((row_i == imp) & (j > i))
            ranks[i % RANK_LANES] = ranks[i % RANK_LANES] + ahead.astype(jnp.int32)
        rank = functools.reduce(lambda a, b: a + b, ranks)
        sel_ref[g] = jnp.where(rank < min(SLC_TOPK, nb), 0.0, NEG)

    shift = bound_ref[0] * LOG2E if bounded else None

    span = WIN + tq
    start = pl.multiple_of(jnp.maximum(q0 - WIN, 0), tq)
    dist = (lax.broadcasted_iota(jnp.int32, (span, tq), 1)
            - lax.broadcasted_iota(jnp.int32, (span, tq), 0)) + (q0 - start)
    wbias = jnp.where((dist >= 0) & (dist < WIN), 0.0, NEG)
    wbias = tile_heads(wbias - shift if bounded else wbias)
    o_win = []
    for (g, _), q_t in zip(chains, q_ts):
        sw = _dot(kw_ref[0, g, pl.ds(start, span), :], q_t) + wbias
        ew = jnp.exp2(sw if bounded else sw - jnp.max(sw, axis=0, keepdims=True))
        ow = _dot(vwt_ref[0, g, :, pl.ds(start, span)], ew.astype(BF16))
        o_win.append(ow[:NSA_HD] * (1.0 / ow[NSA_HD:NSA_HD + 1]))

    blocks_per_tile = tks // SLC_BLOCK
    rel = (lax.broadcasted_iota(jnp.int32, (tks, tq), 1)
           - lax.broadcasted_iota(jnp.int32, (tks, tq), 0))

    def slc_bias(it, k0, need_causal):
        biases = []
        for g in range(n_g):
            if isinstance(it, int):
                rows = sel_ref[g, it * blocks_per_tile:(it + 1) * blocks_per_tile, :]
            else:
                rows = sel_ref[g, pl.ds(pl.multiple_of(it * blocks_per_tile, blocks_per_tile), blocks_per_tile), :]
            bias = jnp.concatenate([jnp.broadcast_to(rows[jj:jj + 1], (SLC_BLOCK, tq))
                                    for jj in range(blocks_per_tile)], axis=0)
            if need_causal:
                bias = jnp.where(rel + (q0 - k0) >= 0, bias, NEG)
            biases.append(tile_heads(bias - shift if bounded else bias))
        return biases

    def slc_body(it, carry, need_causal=True):
        k0 = it * tks if isinstance(it, int) else pl.multiple_of(it * tks, tks)
        biases = slc_bias(it, k0, need_causal)
        out = []
        for (g, _), q_t, state in zip(chains, q_ts, carry):
            sc = _dot(ks_ref[0, g, pl.ds(k0, tks), :], q_t) + biases[g]
            vt = vst_ref[0, g, :, pl.ds(k0, tks)]
            if bounded:
                (acc,) = state
                acc = acc + _dot(vt, jnp.exp2(sc).astype(BF16))
                out.append((acc,))
            else:
                m, acc = state
                m_new = jnp.maximum(m, jnp.max(sc, axis=0, keepdims=True))
                acc = jnp.exp2(m - m_new) * acc + _dot(vt, jnp.exp2(sc - m_new).astype(BF16))
                out.append((m_new, acc))
        return tuple(out)

    zero_state = (jnp.zeros((VT_ROWS, lanes), F32),)
    init = tuple(zero_state if bounded else (jnp.full((1, lanes), NEG, F32),) + zero_state for _ in chains)
    n_it = (q0 + tq + tks - 1) // tks
    if bounded:
        def slc_tiles(n):
            carry = init
            for it in range(n):
                carry = slc_body(it, carry, need_causal=(it == n - 1))
            return carry

        max_it = ks_ref.shape[2] // tks
        fin = lax.switch(n_it - 1, [functools.partial(slc_tiles, n) for n in range(1, max_it + 1)])
    else:
        fin = lax.fori_loop(0, n_it, slc_body, init)
    o_slc = [st[-1][:NSA_HD] * (1.0 / st[-1][NSA_HD:NSA_HD + 1]) for st in fin]

    for ci, (_, h0) in enumerate(chains):
        for i in range(ch):
            cols = slice(i * tq, (i + 1) * tq)
            head = h0 + i
            o_h = jnp.zeros((NSA_HD, tq), F32)
            for br, o_b in enumerate((o_cmp[ci], o_slc[ci], o_win[ci])):
                row = head * N_BRANCH + br
                o_h = o_h + gt_ref[0, row:row + 1, :] * o_b[:, cols]
            o_ref[0, :, head * NSA_HD:(head + 1) * NSA_HD] = o_h.T.astype(o_ref.dtype)


def _nsa(q_t, kc, vct, ks, vst, kw, vwt, gates_t, mt, bound, tq, tks, bounded):
    bsz, _, _, t = q_t.shape
    n_blk = kc.shape[2]
    n_g = NSA_KV_HEADS
    full = lambda b, i: (b, 0, 0, 0)
    k_spec = pl.BlockSpec((1, n_g, t, NSA_HD), full)
    vt_spec = pl.BlockSpec((1, n_g, VT_ROWS, t), full)
    return pl.pallas_call(
        functools.partial(_nsa_kernel, tq=tq, tks=tks, bounded=bounded),
        grid=(bsz, t // tq),
        in_specs=[pl.BlockSpec(memory_space=pltpu.SMEM),
                  pl.BlockSpec((1, NSA_HEADS, NSA_HD, tq), lambda b, i: (b, 0, 0, i)),
                  pl.BlockSpec((1, n_g, n_blk, NSA_HD), full),
                  pl.BlockSpec((1, n_g, NSA_HD, n_blk), full),
                  k_spec, vt_spec, k_spec, vt_spec,
                  pl.BlockSpec((1, GATE_ROWS, tq), lambda b, i: (b, 0, i)),
                  pl.BlockSpec(mt.shape, lambda b, i: (0, 0))],
        out_specs=pl.BlockSpec((1, tq, NSA_WIDTH), lambda b, i: (b, i, 0)),
        out_shape=jax.ShapeDtypeStruct((bsz, t, NSA_WIDTH), BF16),
        scratch_shapes=[pltpu.VMEM((n_g, mt.shape[0], tq), F32)],
        compiler_params=pltpu.CompilerParams(
            dimension_semantics=("arbitrary", "arbitrary"), vmem_limit_bytes=VMEM_LIMIT),
        name="nsa_bounded" if bounded else "nsa",
    )(bound, q_t, kc, vct, ks, vst, kw, vwt, gates_t, mt)


def _causal_conv(u, prev, cw, cb):
    tm = u.shape[0]
    g = SUBLANES
    r8 = lax.broadcasted_iota(jnp.int32, (g, u.shape[1]), 0)
    wrap1 = jnp.where(r8 == 0, prev[2 * g - 1:2 * g], pltpu.roll(u[tm - g:], 1, axis=0))
    wrap2 = jnp.where(r8 == 0, prev[g - 1:g], pltpu.roll(u[tm - 2 * g:tm - g], 1, axis=0))
    u1 = jnp.concatenate([wrap1, u[:tm - g]], axis=0)
    u2 = jnp.concatenate([wrap2, wrap1, u[:tm - 2 * g]], axis=0)
    return cb + u2 * cw[0:1] + u1 * cw[1:2] + u * cw[2:3]


def _ffn_kernel(x_ref, hg_ref, ns_ref, mod_ref, wo_ref, g2_ref, wu_ref, cw_ref, cb_ref, wd_ref,
                o_ref, carry_ref, buf_ref, h2_ref, u_ref, g_ref, acc_ref, *, fc):
    mix = _dot(hg_ref[0], wo_ref[:HG_WIDTH, :]) + _dot(ns_ref[0], wo_ref[HG_WIDTH:, :])
    x1_nat = x_ref[0] + mod_ref[0, 2:3, :] * mix
    n_lt = D_MODEL // LANES
    tm = x1_nat.shape[0]
    groups = tm // SUBLANES
    pitch = buf_ref.shape[1] // SUBLANES
    for c in range(n_lt):
        for sg in range(SUBLANES):
            buf_ref[c, sg * pitch:sg * pitch + groups] = x1_nat[sg * groups:(sg + 1) * groups,
                                                                c * LANES:(c + 1) * LANES]

    x1 = jnp.concatenate(
        [jnp.concatenate([buf_ref[c, pl.ds(j, SUBLANES, stride=pitch), :] for j in range(groups)], axis=0)
         for c in range(n_lt)], axis=1)
    y = x1 * lax.rsqrt(jnp.mean(x1 * x1, axis=-1, keepdims=True) + EPS) * g2_ref[...]
    h2_ref[...] = (y * (1.0 + mod_ref[0, 4:5, :]) + mod_ref[0, 3:4, :]).astype(h2_ref.dtype)
    acc_ref[...] = jnp.zeros_like(acc_ref)
    first = pl.program_id(1) == 0

    def up(jc, slot):
        for half in range(2):
            off = pl.multiple_of(half * D_FF + jc * fc, fc)
            u_ref[slot, half] = _dot(h2_ref[...], wu_ref[:, pl.ds(off, fc)])

    def act(jc, slot):
        halves = []
        for half in range(2):
            cols = pl.ds(pl.multiple_of(half * D_FF + jc * fc, fc), fc)
            u = u_ref[slot, half]
            prev = jnp.where(first, 0.0, carry_ref[:, cols])
            carry_ref[:, cols] = u[tm - 2 * SUBLANES:]
            halves.append(_causal_conv(u, prev, cw_ref[:, cols], cb_ref[:, cols]))
        a, v = halves
        g_ref[slot] = (_silu(a) * v).astype(g_ref.dtype)

    def down(jc, slot):
        acc_ref[...] += _dot(g_ref[slot], wd_ref[pl.ds(pl.multiple_of(jc * fc, fc), fc), :])

    n = D_FF // fc

    def tick(i):
        if i < n:
            up(i, i % 2)
        if 0 <= i - 1 < n:
            act(i - 1, (i - 1) % 2)
        if 0 <= i - 2 < n:
            down(i - 2, i % 2)

    for i in range(n + 2):
        tick(i)

    out = x1 + mod_ref[0, 5:6, :] * acc_ref[...]
    for c in range(n_lt):
        for j in range(groups):
            buf_ref[c, pl.ds(j, SUBLANES, stride=pitch), :] = out[j * SUBLANES:(j + 1) * SUBLANES,
                                                                  c * LANES:(c + 1) * LANES]
    for c in range(n_lt):
        for sg in range(SUBLANES):
            o_ref[0, sg * groups:(sg + 1) * groups, c * LANES:(c + 1) * LANES] = (
                buf_ref[c, sg * pitch:sg * pitch + groups])


def _ffn(x, o_hg, o_nsa, mod, w_out, g2, w_up, conv_w, conv_b, w_down, tm, fc):
    bsz, t, _ = x.shape
    row_spec = lambda w: pl.BlockSpec((1, tm, w), lambda b, i: (b, i, 0))
    groups = tm // SUBLANES
    pad = SUBLANES if (groups // SUBLANES) % 2 == 0 else 0
    resident = lambda a: pl.BlockSpec(a.shape, lambda b, i: (0, 0), pipeline_mode=pl.Buffered(1))
    return pl.pallas_call(
        functools.partial(_ffn_kernel, fc=fc),
        grid=(bsz, t // tm),
        in_specs=[row_spec(D_MODEL), row_spec(HG_WIDTH), row_spec(NSA_WIDTH),
                  pl.BlockSpec((1, 6, D_MODEL), lambda b, i: (b, 0, 0)),
                  resident(w_out), resident(g2), resident(w_up), resident(conv_w), resident(conv_b),
                  resident(w_down)],
        out_specs=row_spec(D_MODEL),
        out_shape=jax.ShapeDtypeStruct(x.shape, F32),
        scratch_shapes=[pltpu.VMEM((2 * SUBLANES, 2 * D_FF), F32),
                        pltpu.VMEM((D_MODEL // LANES, tm + SUBLANES * pad, LANES), F32),
                        pltpu.VMEM((tm, D_MODEL), BF16),
                        pltpu.VMEM((2, 2, tm, fc), F32),
                        pltpu.VMEM((2, tm, fc), BF16),
                        pltpu.VMEM((tm, D_MODEL), F32)],
        compiler_params=pltpu.CompilerParams(
            dimension_semantics=("arbitrary", "arbitrary"), vmem_limit_bytes=VMEM_LIMIT),
        name="ffn",
    )(x, o_hg, o_nsa, mod, w_out, g2, w_up, conv_w, conv_b, w_down)


def _rope_tables():
    half = ROPE_DIM // 2
    inv = ROPE_THETA ** (-jnp.arange(half, dtype=F32) * 2.0 / ROPE_DIM)
    return (jnp.tile(inv.reshape(half, 1), (1, LANES)),)


def _gain_t(g):
    return jnp.tile(g.reshape(NSA_HD, 1), (LANES // NSA_HD, LANES))


def _selection_tables(t):
    n_seg = t // CMP_STRIDE
    nb = t // SLC_BLOCK
    cst = np.arange(n_seg) * CMP_STRIDE
    sst = np.arange(nb) * SLC_BLOCK
    ovl = np.clip(np.minimum(cst[:, None] + CMP_BLOCK, sst[None] + SLC_BLOCK)
                  - np.maximum(cst[:, None], sst[None]), 0, None) / CMP_BLOCK
    ovl[(t - CMP_BLOCK) // CMP_STRIDE + 1:] = 0.0
    return (jnp.asarray(ovl.T, dtype=BF16),)


def _layer(x, mod, pos_row, l, p, tables):
    bsz, t, _ = x.shape
    inv_t, mt = tables
    w_in_p = jnp.pad(p["w_in"][l], ((0, 0), (0, IN_COLS_PAD - IN_COLS))).astype(BF16)
    qg_t = _gain_t(p["q_norm_g"][l])
    kg_t = jnp.stack([_gain_t(p["k_norm_g"][l, br]) for br in range(N_BRANCH)])
    q_t, kc, vc, ks, vst, kw, vwt, gates_t, o_hg = _mix(
        x, mod, p["norm1_g"][l].reshape(1, D_MODEL), w_in_p, pos_row, inv_t, qg_t, kg_t,
        p["lb_logits"], p["hg_norm_g"][l].reshape(1, HG_HD), l, tm=512)

    n_seg = t // CMP_STRIDE
    seg_w = CMP_STRIDE * NSA_HD
    pe2 = p["pe_cmp"][l].reshape(2, 2, seg_w)
    kcmp, vcmp_t = _compress(kc.reshape(bsz, NSA_KV_HEADS, n_seg, seg_w),
                             vc.reshape(bsz, NSA_KV_HEADS, n_seg, seg_w),
                             pe2, p["w_cmp1"][l].astype(BF16), p["w_cmp2"][l].astype(BF16))
    bound = (SCORE_BOUND_MARGIN * NSA_HD ** 0.5 * jnp.max(jnp.abs(p["q_norm_g"][l]))
             * jnp.max(jnp.abs(p["k_norm_g"][l, 1:]))).reshape(1).astype(F32)
    nsa_args = (q_t, kcmp, vcmp_t, ks, vst, kw, vwt, gates_t, mt, bound)
    o_nsa = lax.cond(bound[0] <= MAX_SCORE_BOUND,
                     lambda a: _nsa(*a, tq=256, tks=512, bounded=True),
                     lambda a: _nsa(*a, tq=256, tks=512, bounded=False), nsa_args)

    return _ffn(x, o_hg, o_nsa, mod, p["w_out"][l].astype(BF16), p["norm2_g"][l].reshape(1, D_MODEL),
                p["w_up"][l].astype(BF16), p["conv_w"][l], p["conv_b"][l].reshape(1, 2 * D_FF),
                p["w_down"][l].astype(BF16), tm=256, fc=256)


def kernel(x, c, positions, w_ada, b_ada, norm1_g, w_in, lb_logits, hg_norm_g, q_norm_g, k_norm_g, pe_cmp, w_cmp1, w_cmp2, w_out, norm2_g, w_up, conv_w, conv_b, w_down):
    p = dict(w_in=w_in, norm1_g=norm1_g, lb_logits=lb_logits, hg_norm_g=hg_norm_g, q_norm_g=q_norm_g,
             k_norm_g=k_norm_g, pe_cmp=pe_cmp, w_cmp1=w_cmp1, w_cmp2=w_cmp2, w_out=w_out,
             norm2_g=norm2_g, w_up=w_up, conv_w=conv_w, conv_b=conv_b, w_down=w_down)
    bsz, t, _ = x.shape
    tables = _rope_tables() + _selection_tables(t)
    pos_row = positions.reshape(bsz, 1, t)
    for l in range(w_ada.shape[0]):
        mod = _ada(c, w_ada[l], b_ada[l]).reshape(bsz, 6, D_MODEL)
        x = _layer(x, mod, pos_row, l, p, tables)
    return x
```

```python
import functools

import jax
import jax.numpy as jnp
import numpy as np
from jax import lax
from jax.experimental import pallas as pl
from jax.experimental.pallas import tpu as pltpu

D_MODEL = 1024
HG_HEADS = 4
HG_HD = 128
HG_WIDTH = HG_HEADS * HG_HD
HG_CHUNK = 64
HG_SUB = 8
LOG2E = 1.4426950408889634
NSA_HEADS = 8
NSA_KV_HEADS = 2
NSA_HD = 64
NSA_GROUP = NSA_HEADS // NSA_KV_HEADS
NSA_CHAIN = 4
RANK_LANES = 4
NSA_WIDTH = NSA_HEADS * NSA_HD
N_BRANCH = 3
CMP_BLOCK = 32
CMP_STRIDE = 16
CMP_HIDDEN = 256
SLC_BLOCK = 64
SLC_TOPK = 16
WIN = 512
ROPE_DIM = NSA_HD // 4
ROPE_THETA = 500000.0
D_FF = 2816
CONV_W = 3
EPS = 1e-6
NEG = -1e30
SCORE_BOUND_MARGIN = 1.02
MAX_SCORE_BOUND = 40.0

LANES = 128
SUBLANES = 8
VMEM_LIMIT = 56 * 1024 * 1024

OFF_HG = 0
OFF_Q = 4 * HG_WIDTH
OFF_KV = OFF_Q + NSA_WIDTH
OFF_G = OFF_KV + 6 * NSA_KV_HEADS * NSA_HD
IN_COLS = OFF_G + N_BRANCH * NSA_HEADS
GATE_ROWS = N_BRANCH * NSA_HEADS
VT_ROWS = NSA_HD + SUBLANES

BF16 = jnp.bfloat16
F32 = jnp.float32


def _dot(a, b):
    return jnp.dot(a, b, preferred_element_type=F32)


def _dot_nt(a, b):
    return lax.dot_general(a, b, (((1,), (1,)), ((), ())), preferred_element_type=F32)


def _sigmoid(x):
    return 1.0 / (1.0 + jnp.exp(-x))


def _silu(x):
    return x * _sigmoid(x)


def _ada_kernel(c_ref, w_ref, b_ref, o_ref):
    cs = _silu(c_ref[...])
    o_ref[...] = _dot(cs.astype(BF16), w_ref[...].astype(BF16)) + b_ref[...]


def _ada(c, w, b):
    bsz = c.shape[0]
    n = w.shape[1]
    tn = D_MODEL
    return pl.pallas_call(
        _ada_kernel,
        grid=(n // tn,),
        in_specs=[pl.BlockSpec((bsz, D_MODEL), lambda j: (0, 0)),
                  pl.BlockSpec((D_MODEL, tn), lambda j: (0, j)),
                  pl.BlockSpec((1, tn), lambda j: (0, j))],
        out_specs=pl.BlockSpec((bsz, tn), lambda j: (0, j)),
        out_shape=jax.ShapeDtypeStruct((bsz, n), F32),
        name="ada",
    )(c, w, b.reshape(1, n))


def _pair_norm_rope_t(xt, g_t, cos_t, sin_t):
    half = ROPE_DIM // 2
    outs = []
    for hh in range(2):
        x = xt[hh * NSA_HD:(hh + 1) * NSA_HD]
        ms = jnp.mean(x * x, axis=0, keepdims=True)
        xn = x * lax.rsqrt(ms + EPS) * g_t[hh * NSA_HD:(hh + 1) * NSA_HD]
        x1, x2 = xn[:half], xn[half:ROPE_DIM]
        outs += [x1 * cos_t - x2 * sin_t, x2 * cos_t + x1 * sin_t, xn[ROPE_DIM:]]
    return jnp.concatenate(outs, axis=0)


def _project_parts(x_ref, mod_ref, g1_ref, w_ref, pos_ref, inv_ref, qg_ref, kg_ref,
                   zhg_out, qt_ref, kc_ref, vc_ref, ks_ref, vst_ref, kw_ref, vwt_ref, gt_ref, seg_ref):
    x = x_ref[0]
    ms = jnp.mean(x * x, axis=-1, keepdims=True)
    y = x * lax.rsqrt(ms + EPS) * g1_ref[...]
    h = (y * (1.0 + mod_ref[0, 1:2, :]) + mod_ref[0, 0:1, :]).astype(BF16)

    tm = x.shape[0]
    reps = tm // LANES

    def lane_tile(a):
        return jnp.concatenate([a] * reps, axis=1)

    ang = lane_tile(inv_ref[...]) * pos_ref[0].astype(F32)
    cos_t = jnp.cos(ang)
    sin_t = jnp.sin(ang)
    scale = NSA_HD ** -0.5 * LOG2E
    vals = {}

    def hg_cols(lo, hi):
        def part():
            zhg_out[:, lo:hi] = _dot(h, w_ref[:, OFF_HG + lo:OFF_HG + hi])
        return part

    def q_pair(p):
        def part():
            if p % 2 == 0:
                vals["zq"] = _dot(h, w_ref[:, OFF_Q + p * LANES:OFF_Q + (p + 2) * LANES])
                vals["qg"] = lane_tile(qg_ref[...])
            off = (p % 2) * LANES
            rt = _pair_norm_rope_t(vals["zq"][:, off:off + LANES].T, vals["qg"], cos_t, sin_t)
            rt = (rt * scale).astype(qt_ref.dtype)
            qt_ref[0, 2 * p] = rt[:NSA_HD]
            qt_ref[0, 2 * p + 1] = rt[NSA_HD:]
        return part

    def kv_dot(br):
        vals["zkv"] = _dot(h, w_ref[:, OFF_KV + 2 * br * LANES:OFF_KV + (2 * br + 2) * LANES])

    def store_segments(a, out_ref):
        seg_ref[...] = a
        n_seg = tm // CMP_STRIDE
        lane = lax.broadcasted_iota(jnp.int32, (n_seg, LANES), 1)
        for l in range(0, CMP_STRIDE, 2):
            even = seg_ref[pl.ds(l, n_seg, stride=CMP_STRIDE), :]
            odd = seg_ref[pl.ds(l + 1, n_seg, stride=CMP_STRIDE), :]
            cols = slice((l // 2) * LANES, (l // 2 + 1) * LANES)
            out_ref[0, 0, :, cols] = jnp.where(lane < NSA_HD, even, pltpu.roll(odd, NSA_HD, axis=1))
            out_ref[0, 1, :, cols] = jnp.where(lane < NSA_HD, pltpu.roll(even, NSA_HD, axis=1), odd)

    def key(br, k_ref):
        def part():
            kv_dot(br)
            kt = _pair_norm_rope_t(vals["zkv"][:, :LANES].T, lane_tile(kg_ref[br]), cos_t, sin_t)
            if br == 0:
                store_segments(kt.T, k_ref)
                store_segments(vals["zkv"][:, LANES:], vc_ref)
            else:
                kk = kt.T.astype(k_ref.dtype)
                for g in range(NSA_KV_HEADS):
                    k_ref[0, g] = kk[:, g * NSA_HD:(g + 1) * NSA_HD]
        return part

    def value_t(vt_ref):
        def part():
            vt = vals["zkv"][:, LANES:].T.astype(vt_ref.dtype)
            ones_row = jnp.where(lax.broadcasted_iota(jnp.int32, (SUBLANES, tm), 0) == 0, 1.0, 0.0).astype(vt_ref.dtype)
            for g in range(NSA_KV_HEADS):
                vt_ref[0, g, :NSA_HD] = vt[g * NSA_HD:(g + 1) * NSA_HD]
                vt_ref[0, g, NSA_HD:] = ones_row
        return part

    def gates_part():
        gates = _sigmoid(_dot(h, w_ref[:, OFF_G:IN_COLS]))
        wide = jnp.concatenate([gates, jnp.zeros((tm, LANES - GATE_ROWS), F32)], axis=1)
        gt_ref[0] = wide.T[:GATE_ROWS]

    wide = 2 * LANES
    hg = [hg_cols(lo, lo + wide) for lo in range(0, 4 * HG_WIDTH, wide)]
    rest = [q_pair(0), q_pair(1), q_pair(2), q_pair(3), key(0, kc_ref), key(1, ks_ref), value_t(vst_ref),
            key(2, kw_ref), value_t(vwt_ref), gates_part]
    parts = []
    for k in range(max(len(hg), len(rest))):
        parts += hg[k:k + 1] + rest[k:k + 1]
    return parts


def _hgrn_chunk(zq, zf, zi, zg, lb, hg_g, st_ref, bk_ref, h, tri, level):
    c = HG_CHUNK
    e_z = jnp.exp(-jnp.abs(zf))
    logsig = jnp.minimum(zf, 0.0) - jnp.log(1.0 + e_z)
    a = jnp.log(lb)
    bb = jnp.log1p(-lb) + logsig
    logf = jnp.maximum(a, bb) + jnp.log(1.0 + jnp.exp(-jnp.abs(a - bb)))
    k = (1.0 - lb) * (jnp.where(zf >= 0.0, e_z, 1.0) / (1.0 + e_z))
    q = _silu(zq)
    v = zi
    l_hi = logf.astype(BF16)
    l_mid = (logf - l_hi.astype(F32)).astype(BF16)
    l_lo = (logf - l_hi.astype(F32) - l_mid.astype(F32)).astype(BF16)
    bc = _dot(tri, l_hi) + _dot(tri, l_mid) + _dot(tri, l_lo)

    col = lax.broadcasted_iota(jnp.int32, (HG_SUB, c), 1)
    b2 = bc * LOG2E
    bk_ref[0, h] = b2
    bk_ref[1, h] = k
    rows_a = []
    for i in range(c // HG_SUB):
        lo = i * HG_SUB
        b_i = b2[lo:lo + HG_SUB]
        q_i = q[lo:lo + HG_SUB]
        k_i = k[lo:lo + HG_SUB]
        a_i = jnp.zeros((HG_SUB, c), F32)
        for s in range(HG_SUB):
            b_s = bk_ref[0, h, lo + s:lo + s + 1, :]
            k_s = bk_ref[1, h, lo + s:lo + s + 1, :]
            e = jnp.exp2(jnp.minimum(b_i - b_s, 0.0)) * q_i * k_s
            a_i = jnp.where(col == lo + s, jnp.sum(e, axis=-1, keepdims=True), a_i)
        rows_a.append(a_i)
    amat = jnp.concatenate(rows_a, axis=0)

    size, idx = c // 2, 1
    while size >= HG_SUB:
        pieces = []
        for e0 in range(0, c, 2 * size):
            o0 = e0 + size
            r = bc[o0:o0 + 1]
            pieces.append(k[e0:o0] * jnp.exp(r - bc[e0:o0]))
            pieces.append(q[o0:o0 + size] * jnp.exp(bc[o0:o0 + size] - r))
        hmat = jnp.concatenate(pieces, axis=0).astype(BF16)
        amat = jnp.where(level == idx, _dot_nt(hmat, hmat), amat)
        size, idx = size // 2, idx + 1
    amat = jnp.where(level >= 0, amat, 0.0)

    st = st_ref[h]
    o = _dot_nt((q * jnp.exp(bc)).astype(BF16), st.astype(BF16)) + _dot(amat.astype(BF16), v.astype(BF16))
    bl = bc[c - 1:c]
    kdec = (k * jnp.exp(bl - bc)).astype(BF16)
    st_ref[h] = jnp.exp(bl) * st + _dot(v.T.astype(BF16), kdec)

    y = o * lax.rsqrt(jnp.mean(o * o, axis=-1, keepdims=True) + EPS) * hg_g
    return y * _silu(zg)


def _hgrn_parts(z_in, lbl_ref, g_ref, o_ref, st_ref, bk_ref, l_idx):
    lg = lbl_ref[...]
    ex = jnp.exp(lg - jnp.max(lg, axis=0, keepdims=True))
    sm = ex / jnp.sum(ex, axis=0, keepdims=True)
    lb_all = jnp.sum(sm[:l_idx + 1], axis=0, keepdims=True)

    c = HG_CHUNK
    ri = lax.broadcasted_iota(jnp.int32, (c, c), 0)
    ci = lax.broadcasted_iota(jnp.int32, (c, c), 1)
    tri = jnp.where(ci <= ri, 1.0, 0.0).astype(BF16)
    level = jnp.where(ci > ri, -1, 0)
    size, idx = c // 2, 1
    while size >= HG_SUB:
        sh = size.bit_length() - 1
        paired = ((ri >> (sh + 1)) == (ci >> (sh + 1))) & ((ri >> sh) != (ci >> sh)) & (ci <= ri)
        level = jnp.where(paired, idx, level)
        size, idx = size // 2, idx + 1

    def chunk_head(ch, h):
        def part():
            rows = slice(ch * c, (ch + 1) * c)
            sl = slice(h * HG_HD, (h + 1) * HG_HD)
            zq = z_in[rows, h * HG_HD:(h + 1) * HG_HD]
            zf = z_in[rows, HG_WIDTH + h * HG_HD:HG_WIDTH + (h + 1) * HG_HD]
            zi = z_in[rows, 2 * HG_WIDTH + h * HG_HD:2 * HG_WIDTH + (h + 1) * HG_HD]
            zg = z_in[rows, 3 * HG_WIDTH + h * HG_HD:3 * HG_WIDTH + (h + 1) * HG_HD]
            y = _hgrn_chunk(zq, zf, zi, zg, lb_all[:, sl], g_ref[...], st_ref, bk_ref, h, tri, level)
            o_ref[0, rows, h * HG_HD:(h + 1) * HG_HD] = y.astype(o_ref.dtype)
        return part

    return [chunk_head(ch, h) for ch in range(z_in.shape[0] // c) for h in range(HG_HEADS)]


def _mix_kernel(x_ref, mod_ref, g1_ref, w_ref, pos_ref, inv_ref, qg_ref, kg_ref, lbl_ref, hgg_ref,
                qt_ref, kc_ref, vc_ref, ks_ref, vst_ref, kw_ref, vwt_ref, gt_ref, ohg_ref,
                zbuf_ref, st_ref, bk_ref, seg_ref, *, l_idx, tiles_per_seq):
    j = pl.program_id(0)

    @pl.when(j == 0)
    def _():
        zbuf_ref[0] = jnp.zeros((zbuf_ref.shape[1], zbuf_ref.shape[2]), F32)

    @pl.when((j == 0) | (lax.rem(j + tiles_per_seq - 1, tiles_per_seq) == 0))
    def _():
        st_ref[...] = jnp.zeros_like(st_ref)

    zbuf_ref[1] = zbuf_ref[0]
    rec = _hgrn_parts(zbuf_ref.at[1], lbl_ref, hgg_ref, ohg_ref, st_ref, bk_ref, l_idx)
    proj = _project_parts(x_ref, mod_ref, g1_ref, w_ref, pos_ref, inv_ref, qg_ref, kg_ref, zbuf_ref.at[0],
                          qt_ref, kc_ref, vc_ref, ks_ref, vst_ref, kw_ref, vwt_ref, gt_ref, seg_ref)
    for part in rec + proj:
        part()


def _mix(x, mod, g1, w_in_p, pos_row, inv_t, qg_t, kg_t, lb_logits, hg_g, l_idx, tm):
    bsz, t, _ = x.shape
    nm = t // tm
    n_tiles = bsz * nm

    def cur(j):
        jc = jnp.minimum(j, n_tiles - 1)
        return jc // nm, jc % nm

    def prev(j):
        jp = jnp.maximum(j - 1, 0)
        return jp // nm, jp % nm

    kv_shape = (bsz, NSA_KV_HEADS, t, NSA_HD)
    kv_spec = pl.BlockSpec((1, NSA_KV_HEADS, tm, NSA_HD), lambda j: (cur(j)[0], 0, cur(j)[1], 0))
    seg_w = CMP_STRIDE * NSA_HD
    seg_shape = (bsz, NSA_KV_HEADS, t // CMP_STRIDE, seg_w)
    seg_spec = pl.BlockSpec((1, NSA_KV_HEADS, tm // CMP_STRIDE, seg_w), lambda j: (cur(j)[0], 0, cur(j)[1], 0))
    vt_shape = (bsz, NSA_KV_HEADS, VT_ROWS, t)
    vt_spec = pl.BlockSpec((1, NSA_KV_HEADS, VT_ROWS, tm), lambda j: (cur(j)[0], 0, 0, cur(j)[1]))
    const = lambda j: (0, 0)
    return pl.pallas_call(
        functools.partial(_mix_kernel, l_idx=l_idx, tiles_per_seq=nm),
        grid=(n_tiles + 1,),
        in_specs=[pl.BlockSpec((1, tm, D_MODEL), lambda j: (cur(j)[0], cur(j)[1], 0)),
                  pl.BlockSpec((1, 6, D_MODEL), lambda j: (cur(j)[0], 0, 0)),
                  pl.BlockSpec((1, D_MODEL), const),
                  pl.BlockSpec((D_MODEL, IN_COLS), const),
                  pl.BlockSpec((1, 1, tm), lambda j: (cur(j)[0], 0, cur(j)[1])),
                  pl.BlockSpec(inv_t.shape, const),
                  pl.BlockSpec(qg_t.shape, const),
                  pl.BlockSpec(kg_t.shape, lambda j: (0, 0, 0)),
                  pl.BlockSpec(lb_logits.shape, const),
                  pl.BlockSpec((1, HG_HD), const)],
        out_specs=[pl.BlockSpec((1, NSA_HEADS, NSA_HD, tm), lambda j: (cur(j)[0], 0, 0, cur(j)[1])),
                   seg_spec, seg_spec, kv_spec, vt_spec, kv_spec, vt_spec,
                   pl.BlockSpec((1, GATE_ROWS, tm), lambda j: (cur(j)[0], 0, cur(j)[1])),
                   pl.BlockSpec((1, tm, HG_WIDTH), lambda j: (prev(j)[0], prev(j)[1], 0))],
        out_shape=[jax.ShapeDtypeStruct((bsz, NSA_HEADS, NSA_HD, t), BF16),
                   jax.ShapeDtypeStruct(seg_shape, F32),
                   jax.ShapeDtypeStruct(seg_shape, F32),
                   jax.ShapeDtypeStruct(kv_shape, BF16),
                   jax.ShapeDtypeStruct(vt_shape, BF16),
                   jax.ShapeDtypeStruct(kv_shape, BF16),
                   jax.ShapeDtypeStruct(vt_shape, BF16),
                   jax.ShapeDtypeStruct((bsz, GATE_ROWS, t), F32),
                   jax.ShapeDtypeStruct((bsz, t, HG_WIDTH), BF16)],
        scratch_shapes=[pltpu.VMEM((2, tm, 4 * HG_WIDTH), F32),
                        pltpu.VMEM((HG_HEADS, HG_HD, HG_HD), F32),
                        pltpu.VMEM((2, HG_HEADS, HG_CHUNK, HG_HD), F32),
                        pltpu.VMEM((tm, LANES), F32)],
        compiler_params=pltpu.CompilerParams(
            dimension_semantics=("arbitrary",), vmem_limit_bytes=VMEM_LIMIT),
        name="mix",
    )(x, mod, g1, w_in_p, pos_row, inv_t, qg_t, kg_t, lb_logits, hg_g)


def _compress_kernel(xk_ref, xv_ref, pe_ref, w1_ref, w2_ref, ko_ref, vo_ref):
    half = CMP_STRIDE * NSA_HD
    outs = []
    for kv, x_ref in enumerate((xk_ref, xv_ref)):
        x = x_ref[0, 0]
        ha = _dot((x + pe_ref[kv, 0:1, :]).astype(BF16), w1_ref[kv, :half, :])
        hb = _dot((x + pe_ref[kv, 1:2, :]).astype(BF16), w1_ref[kv, half:, :])
        n = x.shape[0]
        pre = ha + pltpu.roll(hb, n - 1, axis=0)
        outs.append(_dot(_silu(pre).astype(BF16), w2_ref[kv]))
    ko_ref[0, 0] = outs[0].astype(ko_ref.dtype)
    vo_ref[0, 0] = outs[1].T.astype(vo_ref.dtype)


def _compress(xk, xv, pe2, w1, w2):
    bsz, g, nseg, width = xk.shape
    x_spec = pl.BlockSpec((1, 1, nseg, width), lambda b, j: (b, j, 0, 0))
    return pl.pallas_call(
        _compress_kernel,
        grid=(bsz, g),
        in_specs=[x_spec, x_spec,
                  pl.BlockSpec(pe2.shape, lambda b, j: (0, 0, 0)),
                  pl.BlockSpec(w1.shape, lambda b, j: (0, 0, 0)),
                  pl.BlockSpec(w2.shape, lambda b, j: (0, 0, 0))],
        out_specs=[pl.BlockSpec((1, 1, nseg, NSA_HD), lambda b, j: (b, j, 0, 0)),
                   pl.BlockSpec((1, 1, NSA_HD, nseg), lambda b, j: (b, j, 0, 0))],
        out_shape=[jax.ShapeDtypeStruct((bsz, g, nseg, NSA_HD), BF16),
                   jax.ShapeDtypeStruct((bsz, g, NSA_HD, nseg), BF16)],
        compiler_params=pltpu.CompilerParams(
            dimension_semantics=("arbitrary", "arbitrary"), vmem_limit_bytes=VMEM_LIMIT),
        name="compress",
    )(xk, xv, pe2, w1, w2)


def _nsa_kernel(bound_ref, qt_ref, kc_ref, vct_ref, ks_ref, vst_ref, kw_ref, vwt_ref, gt_ref, mt_ref,
                o_ref, sel_ref, *, tq, tks, bounded):
    r = NSA_GROUP
    n_g = NSA_KV_HEADS
    ch = NSA_CHAIN
    lanes = ch * tq
    q0 = pl.program_id(1) * tq
    chains = [(g, g * r + c * ch) for g in range(n_g) for c in range(r // ch)]
    q_ts = [jnp.concatenate([qt_ref[0, h0 + i] for i in range(ch)], axis=1)
            for _, h0 in chains]

    def tile_heads(a):
        return jnp.concatenate([a] * ch, axis=1)

    n_blk = kc_ref.shape[2]
    nb = mt_ref.shape[0]
    blk_end = lax.broadcasted_iota(jnp.int32, (n_blk, tq), 0) * CMP_STRIDE + (CMP_BLOCK - 1)
    t_row = q0 + lax.broadcasted_iota(jnp.int32, (1, tq), 1)
    cvalid = tile_heads(jnp.where(blk_end <= t_row, 1.0, 0.0)) > 0.5
    some = tile_heads((t_row >= CMP_BLOCK - 1).astype(F32))
    j = lax.broadcasted_iota(jnp.int32, (nb, tq), 0)
    cur = jnp.right_shift(t_row, SLC_BLOCK.bit_length() - 1)
    forced = (j == 0) | (j == cur) | (j == cur - 1)
    o_cmp = []
    psum = [None] * n_g
    for (g, _), q_t in zip(chains, q_ts):
        s = jnp.where(cvalid, _dot(kc_ref[0, g], q_t), NEG)
        e = jnp.exp2(s - jnp.max(s, axis=0, keepdims=True))
        p = e * (some / jnp.sum(e, axis=0, keepdims=True))
        o_cmp.append(_dot(vct_ref[0, g], p.astype(BF16)))
        for i in range(ch):
            part = p[:, i * tq:(i + 1) * tq]
            psum[g] = part if psum[g] is None else psum[g] + part
    for g in range(n_g):
        p_hi = psum[g].astype(BF16)
        p_lo = (psum[g] - p_hi.astype(F32)).astype(BF16)
        imp = _dot(mt_ref[...], p_hi) + _dot(mt_ref[...], p_lo)
        imp = jnp.where(j <= cur, jnp.where(forced, jnp.inf, imp), -1.0)
        ranks = [jnp.zeros((nb, tq), jnp.int32) for _ in range(RANK_LANES)]
        for i in range(nb):
            row_i = imp[i:i + 1, :]
            ahead = (row_i > imp) | ((row_i == imp) & (j > i))
            ranks[i % RANK_LANES] = ranks[i % RANK_LANES] + ahead.astype(jnp.int32)
        rank = functools.reduce(lambda a, b: a + b, ranks)
        sel_ref[g] = jnp.where(rank < min(SLC_TOPK, nb), 0.0, NEG)

    shift = bound_ref[0] * LOG2E if bounded else None

    span = WIN + tq
    start = pl.multiple_of(jnp.maximum(q0 - WIN, 0), tq)
    dist = (lax.broadcasted_iota(jnp.int32, (span, tq), 1)
            - lax.broadcasted_iota(jnp.int32, (span, tq), 0)) + (q0 - start)
    wbias = jnp.where((dist >= 0) & (dist < WIN), 0.0, NEG)
    wbias = tile_heads(wbias - shift if bounded else wbias)
    o_win = []
    for (g, _), q_t in zip(chains, q_ts):
        sw = _dot(kw_ref[0, g, pl.ds(start, span), :], q_t) + wbias
        ew = jnp.exp2(sw if bounded else sw - jnp.max(sw, axis=0, keepdims=True))
        ow = _dot(vwt_ref[0, g, :, pl.ds(start, span)], ew.astype(BF16))
        o_win.append(ow[:NSA_HD] * (1.0 / ow[NSA_HD:NSA_HD + 1]))

    blocks_per_tile = tks // SLC_BLOCK
    rel = (lax.broadcasted_iota(jnp.int32, (tks, tq), 1)
           - lax.broadcasted_iota(jnp.int32, (tks, tq), 0))

    def slc_bias(it, k0, need_causal):
        biases = []
        for g in range(n_g):
            if isinstance(it, int):
                rows = sel_ref[g, it * blocks_per_tile:(it + 1) * blocks_per_tile, :]
            else:
                rows = sel_ref[g, pl.ds(pl.multiple_of(it * blocks_per_tile, blocks_per_tile), blocks_per_tile), :]
            bias = jnp.concatenate([jnp.broadcast_to(rows[jj:jj + 1], (SLC_BLOCK, tq))
                                    for jj in range(blocks_per_tile)], axis=0)
            if need_causal:
                bias = jnp.where(rel + (q0 - k0) >= 0, bias, NEG)
            biases.append(tile_heads(bias - shift if bounded else bias))
        return biases

    def slc_body(it, carry, need_causal=True):
        k0 = it * tks if isinstance(it, int) else pl.multiple_of(it * tks, tks)
        biases = slc_bias(it, k0, need_causal)
        out = []
        for (g, _), q_t, state in zip(chains, q_ts, carry):
            sc = _dot(ks_ref[0, g, pl.ds(k0, tks), :], q_t) + biases[g]
            vt = vst_ref[0, g, :, pl.ds(k0, tks)]
            if bounded:
                (acc,) = state
                acc = acc + _dot(vt, jnp.exp2(sc).astype(BF16))
                out.append((acc,))
            else:
                m, acc = state
                m_new = jnp.maximum(m, jnp.max(sc, axis=0, keepdims=True))
                acc = jnp.exp2(m - m_new) * acc + _dot(vt, jnp.exp2(sc - m_new).astype(BF16))
                out.append((m_new, acc))
        return tuple(out)

    zero_state = (jnp.zeros((VT_ROWS, lanes), F32),)
    init = tuple(zero_state if bounded else (jnp.full((1, lanes), NEG, F32),) + zero_state for _ in chains)
    n_it = (q0 + tq + tks - 1) // tks
    if bounded:
        def slc_tiles(n):
            carry = init
            for it in range(n):
                carry = slc_body(it, carry, need_causal=(it == n - 1))
            return carry

        max_it = ks_ref.shape[2] // tks
        fin = lax.switch(n_it - 1, [functools.partial(slc_tiles, n) for n in range(1, max_it + 1)])
    else:
        fin = lax.fori_loop(0, n_it, slc_body, init)
    o_slc = [st[-1][:NSA_HD] * (1.0 / st[-1][NSA_HD:NSA_HD + 1]) for st in fin]

    for ci, (_, h0) in enumerate(chains):
        for i in range(ch):
            cols = slice(i * tq, (i + 1) * tq)
            head = h0 + i
            o_h = jnp.zeros((NSA_HD, tq), F32)
            for br, o_b in enumerate((o_cmp[ci], o_slc[ci], o_win[ci])):
                row = head * N_BRANCH + br
                o_h = o_h + gt_ref[0, row:row + 1, :] * o_b[:, cols]
            o_ref[0, :, head * NSA_HD:(head + 1) * NSA_HD] = o_h.T.astype(o_ref.dtype)


def _nsa(q_t, kc, vct, ks, vst, kw, vwt, gates_t, mt, bound, tq, tks, bounded):
    bsz, _, _, t = q_t.shape
    n_blk = kc.shape[2]
    n_g = NSA_KV_HEADS
    full = lambda b, i: (b, 0, 0, 0)
    k_spec = pl.BlockSpec((1, n_g, t, NSA_HD), full)
    vt_spec = pl.BlockSpec((1, n_g, VT_ROWS, t), full)
    return pl.pallas_call(
        functools.partial(_nsa_kernel, tq=tq, tks=tks, bounded=bounded),
        grid=(bsz, t // tq),
        in_specs=[pl.BlockSpec(memory_space=pltpu.SMEM),
                  pl.BlockSpec((1, NSA_HEADS, NSA_HD, tq), lambda b, i: (b, 0, 0, i)),
                  pl.BlockSpec((1, n_g, n_blk, NSA_HD), full),
                  pl.BlockSpec((1, n_g, NSA_HD, n_blk), full),
                  k_spec, vt_spec, k_spec, vt_spec,
                  pl.BlockSpec((1, GATE_ROWS, tq), lambda b, i: (b, 0, i)),
                  pl.BlockSpec(mt.shape, lambda b, i: (0, 0))],
        out_specs=pl.BlockSpec((1, tq, NSA_WIDTH), lambda b, i: (b, i, 0)),
        out_shape=jax.ShapeDtypeStruct((bsz, t, NSA_WIDTH), BF16),
        scratch_shapes=[pltpu.VMEM((n_g, mt.shape[0], tq), F32)],
        compiler_params=pltpu.CompilerParams(
            dimension_semantics=("arbitrary", "arbitrary"), vmem_limit_bytes=VMEM_LIMIT),
        name="nsa_bounded" if bounded else "nsa",
    )(bound, q_t, kc, vct, ks, vst, kw, vwt, gates_t, mt)


def _causal_conv(u, prev, cw, cb):
    tm = u.shape[0]
    g = SUBLANES
    r8 = lax.broadcasted_iota(jnp.int32, (g, u.shape[1]), 0)
    wrap1 = jnp.where(r8 == 0, prev[2 * g - 1:2 * g], pltpu.roll(u[tm - g:], 1, axis=0))
    wrap2 = jnp.where(r8 == 0, prev[g - 1:g], pltpu.roll(u[tm - 2 * g:tm - g], 1, axis=0))
    u1 = jnp.concatenate([wrap1, u[:tm - g]], axis=0)
    u2 = jnp.concatenate([wrap2, wrap1, u[:tm - 2 * g]], axis=0)
    return cb + u2 * cw[0:1] + u1 * cw[1:2] + u * cw[2:3]


def _ffn_kernel(x_ref, hg_ref, ns_ref, mod_ref, wo_ref, g2_ref, wu_ref, cw_ref, cb_ref, wd_ref,
                o_ref, carry_ref, buf_ref, h2_ref, u_ref, g_ref, acc_ref, *, fc):
    mix = _dot(hg_ref[0], wo_ref[:HG_WIDTH, :]) + _dot(ns_ref[0], wo_ref[HG_WIDTH:, :])
    x1_nat = x_ref[0] + mod_ref[0, 2:3, :] * mix
    n_lt = D_MODEL // LANES
    tm = x1_nat.shape[0]
    groups = tm // SUBLANES
    pitch = buf_ref.shape[1] // SUBLANES
    for c in range(n_lt):
        for sg in range(SUBLANES):
            buf_ref[c, sg * pitch:sg * pitch + groups] = x1_nat[sg * groups:(sg + 1) * groups,
                                                                c * LANES:(c + 1) * LANES]

    x1 = jnp.concatenate(
        [jnp.concatenate([buf_ref[c, pl.ds(j, SUBLANES, stride=pitch), :] for j in range(groups)], axis=0)
         for c in range(n_lt)], axis=1)
    y = x1 * lax.rsqrt(jnp.mean(x1 * x1, axis=-1, keepdims=True) + EPS) * g2_ref[...]
    h2_ref[...] = (y * (1.0 + mod_ref[0, 4:5, :]) + mod_ref[0, 3:4, :]).astype(h2_ref.dtype)
    acc_ref[...] = jnp.zeros_like(acc_ref)
    first = pl.program_id(1) == 0

    def up(jc, slot):
        for half in range(2):
            off = pl.multiple_of(half * D_FF + jc * fc, fc)
            u_ref[slot, half] = _dot(h2_ref[...], wu_ref[:, pl.ds(off, fc)])

    def act(jc, slot):
        halves = []
        for half in range(2):
            cols = pl.ds(pl.multiple_of(half * D_FF + jc * fc, fc), fc)
            u = u_ref[slot, half]
            prev = jnp.where(first, 0.0, carry_ref[:, cols])
            carry_ref[:, cols] = u[tm - 2 * SUBLANES:]
            halves.append(_causal_conv(u, prev, cw_ref[:, cols], cb_ref[:, cols]))
        a, v = halves
        g_ref[slot] = (_silu(a) * v).astype(g_ref.dtype)

    def down(jc, slot):
        acc_ref[...] += _dot(g_ref[slot], wd_ref[pl.ds(pl.multiple_of(jc * fc, fc), fc), :])

    n = D_FF // fc

    def tick(i):
        if i < n:
            up(i, i % 2)
        if 0 <= i - 1 < n:
            act(i - 1, (i - 1) % 2)
        if 0 <= i - 2 < n:
            down(i - 2, i % 2)

    for i in range(n + 2):
        tick(i)

    out = x1 + mod_ref[0, 5:6, :] * acc_ref[...]
    for c in range(n_lt):
        for j in range(groups):
            buf_ref[c, pl.ds(j, SUBLANES, stride=pitch), :] = out[j * SUBLANES:(j + 1) * SUBLANES,
                                                                  c * LANES:(c + 1) * LANES]
    for c in range(n_lt):
        for sg in range(SUBLANES):
            o_ref[0, sg * groups:(sg + 1) * groups, c * LANES:(c + 1) * LANES] = (
                buf_ref[c, sg * pitch:sg * pitch + groups])


def _ffn(x, o_hg, o_nsa, mod, w_out, g2, w_up, conv_w, conv_b, w_down, tm, fc):
    bsz, t, _ = x.shape
    row_spec = lambda w: pl.BlockSpec((1, tm, w), lambda b, i: (b, i, 0))
    groups = tm // SUBLANES
    pad = SUBLANES if (groups // SUBLANES) % 2 == 0 else 0
    resident = lambda a: pl.BlockSpec(a.shape, lambda b, i: (0, 0), pipeline_mode=pl.Buffered(1))
    return pl.pallas_call(
        functools.partial(_ffn_kernel, fc=fc),
        grid=(bsz, t // tm),
        in_specs=[row_spec(D_MODEL), row_spec(HG_WIDTH), row_spec(NSA_WIDTH),
                  pl.BlockSpec((1, 6, D_MODEL), lambda b, i: (b, 0, 0)),
                  resident(w_out), resident(g2), resident(w_up), resident(conv_w), resident(conv_b),
                  resident(w_down)],
        out_specs=row_spec(D_MODEL),
        out_shape=jax.ShapeDtypeStruct(x.shape, F32),
        scratch_shapes=[pltpu.VMEM((2 * SUBLANES, 2 * D_FF), F32),
                        pltpu.VMEM((D_MODEL // LANES, tm + SUBLANES * pad, LANES), F32),
                        pltpu.VMEM((tm, D_MODEL), BF16),
                        pltpu.VMEM((2, 2, tm, fc), F32),
                        pltpu.VMEM((2, tm, fc), BF16),
                        pltpu.VMEM((tm, D_MODEL), F32)],
        compiler_params=pltpu.CompilerParams(
            dimension_semantics=("arbitrary", "arbitrary"), vmem_limit_bytes=VMEM_LIMIT),
        name="ffn",
    )(x, o_hg, o_nsa, mod, w_out, g2, w_up, conv_w, conv_b, w_down)


def _rope_tables():
    half = ROPE_DIM // 2
    inv = ROPE_THETA ** (-jnp.arange(half, dtype=F32) * 2.0 / ROPE_DIM)
    return (jnp.tile(inv.reshape(half, 1), (1, LANES)),)


def _gain_t(g):
    return jnp.tile(g.reshape(NSA_HD, 1), (LANES // NSA_HD, LANES))


def _selection_tables(t):
    n_seg = t // CMP_STRIDE
    nb = t // SLC_BLOCK
    cst = np.arange(n_seg) * CMP_STRIDE
    sst = np.arange(nb) * SLC_BLOCK
    ovl = np.clip(np.minimum(cst[:, None] + CMP_BLOCK, sst[None] + SLC_BLOCK)
                  - np.maximum(cst[:, None], sst[None]), 0, None) / CMP_BLOCK
    ovl[(t - CMP_BLOCK) // CMP_STRIDE + 1:] = 0.0
    return (jnp.asarray(ovl.T, dtype=BF16),)


def _layer(x, mod, pos_row, l, p, tables):
    bsz, t, _ = x.shape
    inv_t, mt = tables
    w_in_p = p["w_in"][l].astype(BF16)
    qg_t = _gain_t(p["q_norm_g"][l])
    kg_t = jnp.stack([_gain_t(p["k_norm_g"][l, br]) for br in range(N_BRANCH)])
    q_t, kc, vc, ks, vst, kw, vwt, gates_t, o_hg = _mix(
        x, mod, p["norm1_g"][l].reshape(1, D_MODEL), w_in_p, pos_row, inv_t, qg_t, kg_t,
        p["lb_logits"], p["hg_norm_g"][l].reshape(1, HG_HD), l, tm=512)

    pe2 = p["pe_cmp"][l].reshape(2, 2, CMP_STRIDE * NSA_HD)
    kcmp, vcmp_t = _compress(kc, vc, pe2, p["w_cmp1"][l].astype(BF16), p["w_cmp2"][l].astype(BF16))
    bound = (SCORE_BOUND_MARGIN * NSA_HD ** 0.5 * jnp.max(jnp.abs(p["q_norm_g"][l]))
             * jnp.max(jnp.abs(p["k_norm_g"][l, 1:]))).reshape(1).astype(F32)
    nsa_args = (q_t, kcmp, vcmp_t, ks, vst, kw, vwt, gates_t, mt, bound)
    o_nsa = lax.cond(bound[0] <= MAX_SCORE_BOUND,
                     lambda a: _nsa(*a, tq=256, tks=512, bounded=True),
                     lambda a: _nsa(*a, tq=256, tks=512, bounded=False), nsa_args)

    return _ffn(x, o_hg, o_nsa, mod, p["w_out"][l].astype(BF16), p["norm2_g"][l].reshape(1, D_MODEL),
                p["w_up"][l].astype(BF16), p["conv_w"][l], p["conv_b"][l].reshape(1, 2 * D_FF),
                p["w_down"][l].astype(BF16), tm=256, fc=256)


def kernel(x, c, positions, w_ada, b_ada, norm1_g, w_in, lb_logits, hg_norm_g, q_norm_g, k_norm_g, pe_cmp, w_cmp1, w_cmp2, w_out, norm2_g, w_up, conv_w, conv_b, w_down):
    p = dict(w_in=w_in, norm1_g=norm1_g, lb_logits=lb_logits, hg_norm_g=hg_norm_g, q_norm_g=q_norm_g,
             k_norm_g=k_norm_g, pe_cmp=pe_cmp, w_cmp1=w_cmp1, w_cmp2=w_cmp2, w_out=w_out,
             norm2_g=norm2_g, w_up=w_up, conv_w=conv_w, conv_b=conv_b, w_down=w_down)
    bsz, t, _ = x.shape
    tables = _rope_tables() + _selection_tables(t)
    pos_row = positions.reshape(bsz, 1, t)
    for l in range(w_ada.shape[0]):
        mod = _ada(c, w_ada[l], b_ada[l]).reshape(bsz, 6, D_MODEL)
        x = _layer(x, mod, pos_row, l, p, tables)
    return x
```

```python
import functools

import jax
import jax.numpy as jnp
import numpy as np
from jax import lax
from jax.experimental import pallas as pl
from jax.experimental.pallas import tpu as pltpu

D_MODEL = 1024
HG_HEADS = 4
HG_HD = 128
HG_WIDTH = HG_HEADS * HG_HD
HG_CHUNK = 128
HG_SUB = 8
LOG2E = 1.4426950408889634
NSA_HEADS = 8
NSA_KV_HEADS = 2
NSA_HD = 64
NSA_GROUP = NSA_HEADS // NSA_KV_HEADS
NSA_CHAIN = 4
RANK_LANES = 4
NSA_WIDTH = NSA_HEADS * NSA_HD
N_BRANCH = 3
CMP_BLOCK = 32
CMP_STRIDE = 16
CMP_HIDDEN = 256
SLC_BLOCK = 64
SLC_TOPK = 16
WIN = 512
ROPE_DIM = NSA_HD // 4
ROPE_THETA = 500000.0
D_FF = 2816
CONV_W = 3
EPS = 1e-6
NEG = -1e30
SCORE_BOUND_MARGIN = 1.02
MAX_SCORE_BOUND = 40.0

LANES = 128
SUBLANES = 8
VMEM_LIMIT = 56 * 1024 * 1024

TM_MIX = 512
TQ_NSA = 256
TKS_NSA = 512
TM_FFN = 256
FC_FFN = 256

OFF_HG = 0
OFF_Q = 4 * HG_WIDTH
OFF_KV = OFF_Q + NSA_WIDTH
OFF_G = OFF_KV + 6 * NSA_KV_HEADS * NSA_HD
IN_COLS = OFF_G + N_BRANCH * NSA_HEADS
GATE_ROWS = N_BRANCH * NSA_HEADS
VT_ROWS = NSA_HD + SUBLANES

BF16 = jnp.bfloat16
F32 = jnp.float32


def _dot(a, b):
    return jnp.dot(a, b, preferred_element_type=F32)


def _dot_nt(a, b):
    return lax.dot_general(a, b, (((1,), (1,)), ((), ())), preferred_element_type=F32)


def _sigmoid(x):
    return 1.0 / (1.0 + jnp.exp(-x))


def _silu(x):
    return x * _sigmoid(x)


def _ada_kernel(c_ref, w_ref, b_ref, o_ref):
    cs = _silu(c_ref[...])
    o_ref[...] = _dot(cs.astype(BF16), w_ref[...].astype(BF16)) + b_ref[...]


def _ada(c, w, b):
    bsz = c.shape[0]
    n = w.shape[1]
    tn = D_MODEL
    return pl.pallas_call(
        _ada_kernel,
        grid=(n // tn,),
        in_specs=[pl.BlockSpec((bsz, D_MODEL), lambda j: (0, 0)),
                  pl.BlockSpec((D_MODEL, tn), lambda j: (0, j)),
                  pl.BlockSpec((1, tn), lambda j: (0, j))],
        out_specs=pl.BlockSpec((bsz, tn), lambda j: (0, j)),
        out_shape=jax.ShapeDtypeStruct((bsz, n), F32),
        name="ada",
    )(c, w, b.reshape(1, n))


def _pair_norm_rope_t(xt, g_t, cos_t, sin_t):
    half = ROPE_DIM // 2
    outs = []
    for hh in range(2):
        x = xt[hh * NSA_HD:(hh + 1) * NSA_HD]
        ms = jnp.mean(x * x, axis=0, keepdims=True)
        xn = x * lax.rsqrt(ms + EPS) * g_t[hh * NSA_HD:(hh + 1) * NSA_HD]
        x1, x2 = xn[:half], xn[half:ROPE_DIM]
        outs += [x1 * cos_t - x2 * sin_t, x2 * cos_t + x1 * sin_t, xn[ROPE_DIM:]]
    return jnp.concatenate(outs, axis=0)


def _project_parts(x_ref, mod_ref, g1_ref, w_ref, pos_ref, inv_ref, qg_ref, kg_ref,
                   zhg_out, qt_ref, kc_ref, vc_ref, ks_ref, vst_ref, kw_ref, vwt_ref, gt_ref, seg_ref):
    x = x_ref[0]
    ms = jnp.mean(x * x, axis=-1, keepdims=True)
    y = x * lax.rsqrt(ms + EPS) * g1_ref[...]
    h = (y * (1.0 + mod_ref[0, 1:2, :]) + mod_ref[0, 0:1, :]).astype(BF16)

    tm = x.shape[0]
    reps = tm // LANES

    def lane_tile(a):
        return jnp.concatenate([a] * reps, axis=1)

    ang = lane_tile(inv_ref[...]) * pos_ref[0].astype(F32)
    cos_t = jnp.cos(ang)
    sin_t = jnp.sin(ang)
    scale = NSA_HD ** -0.5 * LOG2E
    vals = {}

    def hg_cols(lo, hi):
        def part():
            zhg_out[:, lo:hi] = _dot(h, w_ref[:, OFF_HG + lo:OFF_HG + hi])
        return part

    def q_pair(p):
        def part():
            if p % 2 == 0:
                vals["zq"] = _dot(h, w_ref[:, OFF_Q + p * LANES:OFF_Q + (p + 2) * LANES])
                vals["qg"] = lane_tile(qg_ref[...])
            off = (p % 2) * LANES
            rt = _pair_norm_rope_t(vals["zq"][:, off:off + LANES].T, vals["qg"], cos_t, sin_t)
            rt = (rt * scale).astype(qt_ref.dtype)
            qt_ref[0, 2 * p] = rt[:NSA_HD]
            qt_ref[0, 2 * p + 1] = rt[NSA_HD:]
        return part

    def kv_dot(br):
        vals["zkv"] = _dot(h, w_ref[:, OFF_KV + 2 * br * LANES:OFF_KV + (2 * br + 2) * LANES])

    def store_segments(a, out_ref):
        seg_ref[...] = a
        n_seg = tm // CMP_STRIDE
        lane = lax.broadcasted_iota(jnp.int32, (n_seg, LANES), 1)
        for l in range(0, CMP_STRIDE, 2):
            even = seg_ref[pl.ds(l, n_seg, stride=CMP_STRIDE), :]
            odd = seg_ref[pl.ds(l + 1, n_seg, stride=CMP_STRIDE), :]
            cols = slice((l // 2) * LANES, (l // 2 + 1) * LANES)
            out_ref[0, 0, :, cols] = jnp.where(lane < NSA_HD, even, pltpu.roll(odd, NSA_HD, axis=1))
            out_ref[0, 1, :, cols] = jnp.where(lane < NSA_HD, pltpu.roll(even, NSA_HD, axis=1), odd)

    def key(br, k_ref):
        def part():
            kv_dot(br)
            kt = _pair_norm_rope_t(vals["zkv"][:, :LANES].T, lane_tile(kg_ref[br]), cos_t, sin_t)
            if br == 0:
                store_segments(kt.T, k_ref)
                store_segments(vals["zkv"][:, LANES:], vc_ref)
            else:
                kk = kt.T.astype(k_ref.dtype)
                for g in range(NSA_KV_HEADS):
                    k_ref[0, g] = kk[:, g * NSA_HD:(g + 1) * NSA_HD]
        return part

    def value_t(vt_ref):
        def part():
            vt = vals["zkv"][:, LANES:].T.astype(vt_ref.dtype)
            ones_row = jnp.where(lax.broadcasted_iota(jnp.int32, (SUBLANES, tm), 0) == 0, 1.0, 0.0).astype(vt_ref.dtype)
            for g in range(NSA_KV_HEADS):
                vt_ref[0, g, :NSA_HD] = vt[g * NSA_HD:(g + 1) * NSA_HD]
                vt_ref[0, g, NSA_HD:] = ones_row
        return part

    def gates_part():
        gates = _sigmoid(_dot(h, w_ref[:, OFF_G:IN_COLS]))
        wide = jnp.concatenate([gates, jnp.zeros((tm, LANES - GATE_ROWS), F32)], axis=1)
        gt_ref[0] = wide.T[:GATE_ROWS]

    wide = 2 * LANES
    hg = [hg_cols(lo, lo + wide) for lo in range(0, 4 * HG_WIDTH, wide)]
    rest = [q_pair(0), q_pair(1), q_pair(2), q_pair(3), key(0, kc_ref), key(1, ks_ref), value_t(vst_ref),
            key(2, kw_ref), value_t(vwt_ref), gates_part]
    parts = []
    for k in range(max(len(hg), len(rest))):
        parts += hg[k:k + 1] + rest[k:k + 1]
    return parts


def _hgrn_chunk(zq, zf, zi, zg, lb, hg_g, st_ref, bk_ref, h, tri, level):
    c = HG_CHUNK
    e_z = jnp.exp(-jnp.abs(zf))
    logsig = jnp.minimum(zf, 0.0) - jnp.log(1.0 + e_z)
    a = jnp.log(lb)
    bb = jnp.log1p(-lb) + logsig
    logf = jnp.maximum(a, bb) + jnp.log(1.0 + jnp.exp(-jnp.abs(a - bb)))
    k = (1.0 - lb) * (jnp.where(zf >= 0.0, e_z, 1.0) / (1.0 + e_z))
    q = _silu(zq)
    v = zi
    l_hi = logf.astype(BF16)
    l_mid = (logf - l_hi.astype(F32)).astype(BF16)
    l_lo = (logf - l_hi.astype(F32) - l_mid.astype(F32)).astype(BF16)
    bc = _dot(tri, l_hi) + _dot(tri, l_mid) + _dot(tri, l_lo)

    col = lax.broadcasted_iota(jnp.int32, (HG_SUB, c), 1)
    b2 = bc * LOG2E
    bk_ref[0, h] = b2
    bk_ref[1, h] = k
    rows_a = []
    for i in range(c // HG_SUB):
        lo = i * HG_SUB
        b_i = b2[lo:lo + HG_SUB]
        q_i = q[lo:lo + HG_SUB]
        k_i = k[lo:lo + HG_SUB]
        a_i = jnp.zeros((HG_SUB, c), F32)
        for s in range(HG_SUB):
            b_s = bk_ref[0, h, lo + s:lo + s + 1, :]
            k_s = bk_ref[1, h, lo + s:lo + s + 1, :]
            e = jnp.exp2(jnp.minimum(b_i - b_s, 0.0)) * q_i * k_s
            a_i = jnp.where(col == lo + s, jnp.sum(e, axis=-1, keepdims=True), a_i)
        rows_a.append(a_i)
    amat = jnp.concatenate(rows_a, axis=0)

    size, idx = c // 2, 1
    while size >= HG_SUB:
        pieces = []
        for e0 in range(0, c, 2 * size):
            o0 = e0 + size
            r = bc[o0:o0 + 1]
            pieces.append(k[e0:o0] * jnp.exp(r - bc[e0:o0]))
            pieces.append(q[o0:o0 + size] * jnp.exp(bc[o0:o0 + size] - r))
        hmat = jnp.concatenate(pieces, axis=0).astype(BF16)
        amat = jnp.where(level == idx, _dot_nt(hmat, hmat), amat)
        size, idx = size // 2, idx + 1
    amat = jnp.where(level >= 0, amat, 0.0)

    st = st_ref[h]
    o = _dot_nt((q * jnp.exp(bc)).astype(BF16), st.astype(BF16)) + _dot(amat.astype(BF16), v.astype(BF16))
    bl = bc[c - 1:c]
    kdec = (k * jnp.exp(bl - bc)).astype(BF16)
    st_ref[h] = jnp.exp(bl) * st + _dot(v.T.astype(BF16), kdec)

    y = o * lax.rsqrt(jnp.mean(o * o, axis=-1, keepdims=True) + EPS) * hg_g
    return y * _silu(zg)


def _hgrn_parts(z_in, lbl_ref, g_ref, o_ref, st_ref, bk_ref, l_idx):
    lg = lbl_ref[...]
    ex = jnp.exp(lg - jnp.max(lg, axis=0, keepdims=True))
    sm = ex / jnp.sum(ex, axis=0, keepdims=True)
    lb_all = jnp.sum(sm[:l_idx + 1], axis=0, keepdims=True)

    c = HG_CHUNK
    ri = lax.broadcasted_iota(jnp.int32, (c, c), 0)
    ci = lax.broadcasted_iota(jnp.int32, (c, c), 1)
    tri = jnp.where(ci <= ri, 1.0, 0.0).astype(BF16)
    level = jnp.where(ci > ri, -1, 0)
    size, idx = c // 2, 1
    while size >= HG_SUB:
        sh = size.bit_length() - 1
        paired = ((ri >> (sh + 1)) == (ci >> (sh + 1))) & ((ri >> sh) != (ci >> sh)) & (ci <= ri)
        level = jnp.where(paired, idx, level)
        size, idx = size // 2, idx + 1

    def chunk_head(ch, h):
        def part():
            rows = slice(ch * c, (ch + 1) * c)
            sl = slice(h * HG_HD, (h + 1) * HG_HD)
            zq = z_in[rows, h * HG_HD:(h + 1) * HG_HD]
            zf = z_in[rows, HG_WIDTH + h * HG_HD:HG_WIDTH + (h + 1) * HG_HD]
            zi = z_in[rows, 2 * HG_WIDTH + h * HG_HD:2 * HG_WIDTH + (h + 1) * HG_HD]
            zg = z_in[rows, 3 * HG_WIDTH + h * HG_HD:3 * HG_WIDTH + (h + 1) * HG_HD]
            y = _hgrn_chunk(zq, zf, zi, zg, lb_all[:, sl], g_ref[...], st_ref, bk_ref, h, tri, level)
            o_ref[0, rows, h * HG_HD:(h + 1) * HG_HD] = y.astype(o_ref.dtype)
        return part

    return [chunk_head(ch, h) for ch in range(z_in.shape[0] // c) for h in range(HG_HEADS)]


def _mix_kernel(x_ref, mod_ref, g1_ref, w_ref, pos_ref, inv_ref, qg_ref, kg_ref, lbl_ref, hgg_ref,
                qt_ref, kc_ref, vc_ref, ks_ref, vst_ref, kw_ref, vwt_ref, gt_ref, ohg_ref,
                zbuf_ref, st_ref, bk_ref, seg_ref, *, l_idx, tiles_per_seq):
    j = pl.program_id(0)

    @pl.when(j == 0)
    def _():
        zbuf_ref[0] = jnp.zeros((zbuf_ref.shape[1], zbuf_ref.shape[2]), F32)

    @pl.when((j == 0) | (lax.rem(j + tiles_per_seq - 1, tiles_per_seq) == 0))
    def _():
        st_ref[...] = jnp.zeros_like(st_ref)

    zbuf_ref[1] = zbuf_ref[0]
    rec = _hgrn_parts(zbuf_ref.at[1], lbl_ref, hgg_ref, ohg_ref, st_ref, bk_ref, l_idx)
    proj = _project_parts(x_ref, mod_ref, g1_ref, w_ref, pos_ref, inv_ref, qg_ref, kg_ref, zbuf_ref.at[0],
                          qt_ref, kc_ref, vc_ref, ks_ref, vst_ref, kw_ref, vwt_ref, gt_ref, seg_ref)
    for part in rec + proj:
        part()


def _mix(x, mod, g1, w_in_p, pos_row, inv_t, qg_t, kg_t, lb_logits, hg_g, l_idx, tm):
    bsz, t, _ = x.shape
    nm = t // tm
    n_tiles = bsz * nm

    def cur(j):
        jc = jnp.minimum(j, n_tiles - 1)
        return jc // nm, jc % nm

    def prev(j):
        jp = jnp.maximum(j - 1, 0)
        return jp // nm, jp % nm

    kv_shape = (bsz, NSA_KV_HEADS, t, NSA_HD)
    kv_spec = pl.BlockSpec((1, NSA_KV_HEADS, tm, NSA_HD), lambda j: (cur(j)[0], 0, cur(j)[1], 0))
    seg_w = CMP_STRIDE * NSA_HD
    seg_shape = (bsz, NSA_KV_HEADS, t // CMP_STRIDE, seg_w)
    seg_spec = pl.BlockSpec((1, NSA_KV_HEADS, tm // CMP_STRIDE, seg_w), lambda j: (cur(j)[0], 0, cur(j)[1], 0))
    vt_shape = (bsz, NSA_KV_HEADS, VT_ROWS, t)
    vt_spec = pl.BlockSpec((1, NSA_KV_HEADS, VT_ROWS, tm), lambda j: (cur(j)[0], 0, 0, cur(j)[1]))
    const = lambda j: (0, 0)
    return pl.pallas_call(
        functools.partial(_mix_kernel, l_idx=l_idx, tiles_per_seq=nm),
        grid=(n_tiles + 1,),
        in_specs=[pl.BlockSpec((1, tm, D_MODEL), lambda j: (cur(j)[0], cur(j)[1], 0)),
                  pl.BlockSpec((1, 6, D_MODEL), lambda j: (cur(j)[0], 0, 0)),
                  pl.BlockSpec((1, D_MODEL), const),
                  pl.BlockSpec((D_MODEL, IN_COLS), const),
                  pl.BlockSpec((1, 1, tm), lambda j: (cur(j)[0], 0, cur(j)[1])),
                  pl.BlockSpec(inv_t.shape, const),
                  pl.BlockSpec(qg_t.shape, const),
                  pl.BlockSpec(kg_t.shape, lambda j: (0, 0, 0)),
                  pl.BlockSpec(lb_logits.shape, const),
                  pl.BlockSpec((1, HG_HD), const)],
        out_specs=[pl.BlockSpec((1, NSA_HEADS, NSA_HD, tm), lambda j: (cur(j)[0], 0, 0, cur(j)[1])),
                   seg_spec, seg_spec, kv_spec, vt_spec, kv_spec, vt_spec,
                   pl.BlockSpec((1, GATE_ROWS, tm), lambda j: (cur(j)[0], 0, cur(j)[1])),
                   pl.BlockSpec((1, tm, HG_WIDTH), lambda j: (prev(j)[0], prev(j)[1], 0))],
        out_shape=[jax.ShapeDtypeStruct((bsz, NSA_HEADS, NSA_HD, t), BF16),
                   jax.ShapeDtypeStruct(seg_shape, F32),
                   jax.ShapeDtypeStruct(seg_shape, F32),
                   jax.ShapeDtypeStruct(kv_shape, BF16),
                   jax.ShapeDtypeStruct(vt_shape, BF16),
                   jax.ShapeDtypeStruct(kv_shape, BF16),
                   jax.ShapeDtypeStruct(vt_shape, BF16),
                   jax.ShapeDtypeStruct((bsz, GATE_ROWS, t), F32),
                   jax.ShapeDtypeStruct((bsz, t, HG_WIDTH), BF16)],
        scratch_shapes=[pltpu.VMEM((2, tm, 4 * HG_WIDTH), F32),
                        pltpu.VMEM((HG_HEADS, HG_HD, HG_HD), F32),
                        pltpu.VMEM((2, HG_HEADS, HG_CHUNK, HG_HD), F32),
                        pltpu.VMEM((tm, LANES), F32)],
        compiler_params=pltpu.CompilerParams(
            dimension_semantics=("arbitrary",), vmem_limit_bytes=VMEM_LIMIT),
        name="mix",
    )(x, mod, g1, w_in_p, pos_row, inv_t, qg_t, kg_t, lb_logits, hg_g)


def _compress_kernel(xk_ref, xv_ref, pe_ref, w1_ref, w2_ref, ko_ref, vo_ref):
    half = CMP_STRIDE * NSA_HD
    outs = []
    for kv, x_ref in enumerate((xk_ref, xv_ref)):
        x = x_ref[0, 0]
        ha = _dot((x + pe_ref[kv, 0:1, :]).astype(BF16), w1_ref[kv, :half, :])
        hb = _dot((x + pe_ref[kv, 1:2, :]).astype(BF16), w1_ref[kv, half:, :])
        n = x.shape[0]
        pre = ha + pltpu.roll(hb, n - 1, axis=0)
        outs.append(_dot(_silu(pre).astype(BF16), w2_ref[kv]))
    ko_ref[0, 0] = outs[0].astype(ko_ref.dtype)
    vo_ref[0, 0] = outs[1].T.astype(vo_ref.dtype)


def _compress(xk, xv, pe2, w1, w2):
    bsz, g, nseg, width = xk.shape
    x_spec = pl.BlockSpec((1, 1, nseg, width), lambda b, j: (b, j, 0, 0))
    return pl.pallas_call(
        _compress_kernel,
        grid=(bsz, g),
        in_specs=[x_spec, x_spec,
                  pl.BlockSpec(pe2.shape, lambda b, j: (0, 0, 0)),
                  pl.BlockSpec(w1.shape, lambda b, j: (0, 0, 0)),
                  pl.BlockSpec(w2.shape, lambda b, j: (0, 0, 0))],
        out_specs=[pl.BlockSpec((1, 1, nseg, NSA_HD), lambda b, j: (b, j, 0, 0)),
                   pl.BlockSpec((1, 1, NSA_HD, nseg), lambda b, j: (b, j, 0, 0))],
        out_shape=[jax.ShapeDtypeStruct((bsz, g, nseg, NSA_HD), BF16),
                   jax.ShapeDtypeStruct((bsz, g, NSA_HD, nseg), BF16)],
        compiler_params=pltpu.CompilerParams(
            dimension_semantics=("arbitrary", "arbitrary"), vmem_limit_bytes=VMEM_LIMIT),
        name="compress",
    )(xk, xv, pe2, w1, w2)


def _nsa_kernel(bound_ref, qt_ref, kc_ref, vct_ref, ks_ref, vst_ref, kw_ref, vwt_ref, gt_ref, mt_ref,
                o_ref, sel_ref, *, tq, tks, bounded):
    r = NSA_GROUP
    n_g = NSA_KV_HEADS
    ch = NSA_CHAIN
    lanes = ch * tq
    q0 = pl.program_id(1) * tq
    chains = [(g, g * r + c * ch) for g in range(n_g) for c in range(r // ch)]
    q_ts = [jnp.concatenate([qt_ref[0, h0 + i] for i in range(ch)], axis=1)
            for _, h0 in chains]

    def tile_heads(a):
        return jnp.concatenate([a] * ch, axis=1)

    n_blk = kc_ref.shape[2]
    nb = mt_ref.shape[0]
    blk_end = lax.broadcasted_iota(jnp.int32, (n_blk, tq), 0) * CMP_STRIDE + (CMP_BLOCK - 1)
    t_row = q0 + lax.broadcasted_iota(jnp.int32, (1, tq), 1)
    cvalid = tile_heads(jnp.where(blk_end <= t_row, 1.0, 0.0)) > 0.5
    some = tile_heads((t_row >= CMP_BLOCK - 1).astype(F32))
    j = lax.broadcasted_iota(jnp.int32, (nb, tq), 0)
    cur = jnp.right_shift(t_row, SLC_BLOCK.bit_length() - 1)
    forced = (j == 0) | (j == cur) | (j == cur - 1)
    o_cmp = []
    psum = [None] * n_g
    for (g, _), q_t in zip(chains, q_ts):
        s = jnp.where(cvalid, _dot(kc_ref[0, g], q_t), NEG)
        e = jnp.exp2(s - jnp.max(s, axis=0, keepdims=True))
        p = e * (some / jnp.sum(e, axis=0, keepdims=True))
        o_cmp.append(_dot(vct_ref[0, g], p.astype(BF16)))
        for i in range(ch):
            part = p[:, i * tq:(i + 1) * tq]
            psum[g] = part if psum[g] is None else psum[g] + part
    for g in range(n_g):
        p_hi = psum[g].astype(BF16)
        p_lo = (psum[g] - p_hi.astype(F32)).astype(BF16)
        imp = _dot(mt_ref[...], p_hi) + _dot(mt_ref[...], p_lo)
        imp = jnp.where(j <= cur, jnp.where(forced, jnp.inf, imp), -1.0)
        ranks = [jnp.zeros((nb, tq), jnp.int32) for _ in range(RANK_LANES)]
        for i in range(nb):
            row_i = imp[i:i + 1, :]
            ahead = (row_i > imp) | ((row_i == imp) & (j > i))
            ranks[i % RANK_LANES] = ranks[i % RANK_LANES] + ahead.astype(jnp.int32)
        rank = functools.reduce(lambda a, b: a + b, ranks)
        sel_ref[g] = jnp.where(rank < min(SLC_TOPK, nb), 0.0, NEG)

    shift = bound_ref[0] * LOG2E if bounded else None

    span = WIN + tq
    start = pl.multiple_of(jnp.maximum(q0 - WIN, 0), tq)
    dist = (lax.broadcasted_iota(jnp.int32, (span, tq), 1)
            - lax.broadcasted_iota(jnp.int32, (span, tq), 0)) + (q0 - start)
    wbias = jnp.where((dist >= 0) & (dist < WIN), 0.0, NEG)
    wbias = tile_heads(wbias - shift if bounded else wbias)
    o_win = []
    for (g, _), q_t in zip(chains, q_ts):
        sw = _dot(kw_ref[0, g, pl.ds(start, span), :], q_t) + wbias
        ew = jnp.exp2(sw if bounded else sw - jnp.max(sw, axis=0, keepdims=True))
        ow = _dot(vwt_ref[0, g, :, pl.ds(start, span)], ew.astype(BF16))
        o_win.append(ow[:NSA_HD] * (1.0 / ow[NSA_HD:NSA_HD + 1]))

    blocks_per_tile = tks // SLC_BLOCK
    rel = (lax.broadcasted_iota(jnp.int32, (tks, tq), 1)
           - lax.broadcasted_iota(jnp.int32, (tks, tq), 0))

    def slc_bias(it, k0, need_causal):
        biases = []
        for g in range(n_g):
            if isinstance(it, int):
                rows = sel_ref[g, it * blocks_per_tile:(it + 1) * blocks_per_tile, :]
            else:
                rows = sel_ref[g, pl.ds(pl.multiple_of(it * blocks_per_tile, blocks_per_tile), blocks_per_tile), :]
            bias = jnp.concatenate([jnp.broadcast_to(rows[jj:jj + 1], (SLC_BLOCK, tq))
                                    for jj in range(blocks_per_tile)], axis=0)
            if need_causal:
                bias = jnp.where(rel + (q0 - k0) >= 0, bias, NEG)
            biases.append(tile_heads(bias - shift if bounded else bias))
        return biases

    def slc_body(it, carry, need_causal=True):
        k0 = it * tks if isinstance(it, int) else pl.multiple_of(it * tks, tks)
        biases = slc_bias(it, k0, need_causal)
        out = []
        for (g, _), q_t, state in zip(chains, q_ts, carry):
            sc = _dot(ks_ref[0, g, pl.ds(k0, tks), :], q_t) + biases[g]
            vt = vst_ref[0, g, :, pl.ds(k0, tks)]
            if bounded:
                (acc,) = state
                acc = acc + _dot(vt, jnp.exp2(sc).astype(BF16))
                out.append((acc,))
            else:
                m, acc = state
                m_new = jnp.maximum(m, jnp.max(sc, axis=0, keepdims=True))
                acc = jnp.exp2(m - m_new) * acc + _dot(vt, jnp.exp2(sc - m_new).astype(BF16))
                out.append((m_new, acc))
        return tuple(out)

    zero_state = (jnp.zeros((VT_ROWS, lanes), F32),)
    init = tuple(zero_state if bounded else (jnp.full((1, lanes), NEG, F32),) + zero_state for _ in chains)
    n_it = (q0 + tq + tks - 1) // tks
    if bounded:
        def slc_tiles(n):
            carry = init
            for it in range(n):
                carry = slc_body(it, carry, need_causal=(it == n - 1))
            return carry

        max_it = ks_ref.shape[2] // tks
        fin = lax.switch(n_it - 1, [functools.partial(slc_tiles, n) for n in range(1, max_it + 1)])
    else:
        fin = lax.fori_loop(0, n_it, slc_body, init)
    o_slc = [st[-1][:NSA_HD] * (1.0 / st[-1][NSA_HD:NSA_HD + 1]) for st in fin]

    for ci, (_, h0) in enumerate(chains):
        for i in range(ch):
            cols = slice(i * tq, (i + 1) * tq)
            head = h0 + i
            o_h = jnp.zeros((NSA_HD, tq), F32)
            for br, o_b in enumerate((o_cmp[ci], o_slc[ci], o_win[ci])):
                row = head * N_BRANCH + br
                o_h = o_h + gt_ref[0, row:row + 1, :] * o_b[:, cols]
            o_ref[0, :, head * NSA_HD:(head + 1) * NSA_HD] = o_h.T.astype(o_ref.dtype)


def _nsa(q_t, kc, vct, ks, vst, kw, vwt, gates_t, mt, bound, tq, tks, bounded):
    bsz, _, _, t = q_t.shape
    n_blk = kc.shape[2]
    n_g = NSA_KV_HEADS
    full = lambda b, i: (b, 0, 0, 0)
    k_spec = pl.BlockSpec((1, n_g, t, NSA_HD), full)
    vt_spec = pl.BlockSpec((1, n_g, VT_ROWS, t), full)
    return pl.pallas_call(
        functools.partial(_nsa_kernel, tq=tq, tks=tks, bounded=bounded),
        grid=(bsz, t // tq),
        in_specs=[pl.BlockSpec(memory_space=pltpu.SMEM),
                  pl.BlockSpec((1, NSA_HEADS, NSA_HD, tq), lambda b, i: (b, 0, 0, i)),
                  pl.BlockSpec((1, n_g, n_blk, NSA_HD), full),
                  pl.BlockSpec((1, n_g, NSA_HD, n_blk), full),
                  k_spec, vt_spec, k_spec, vt_spec,
                  pl.BlockSpec((1, GATE_ROWS, tq), lambda b, i: (b, 0, i)),
                  pl.BlockSpec(mt.shape, lambda b, i: (0, 0))],
        out_specs=pl.BlockSpec((1, tq, NSA_WIDTH), lambda b, i: (b, i, 0)),
        out_shape=jax.ShapeDtypeStruct((bsz, t, NSA_WIDTH), BF16),
        scratch_shapes=[pltpu.VMEM((n_g, mt.shape[0], tq), F32)],
        compiler_params=pltpu.CompilerParams(
            dimension_semantics=("arbitrary", "arbitrary"), vmem_limit_bytes=VMEM_LIMIT),
        name="nsa_bounded" if bounded else "nsa",
    )(bound, q_t, kc, vct, ks, vst, kw, vwt, gates_t, mt)


def _causal_conv(u, prev, cw, cb):
    tm = u.shape[0]
    g = SUBLANES
    r8 = lax.broadcasted_iota(jnp.int32, (g, u.shape[1]), 0)
    wrap1 = jnp.where(r8 == 0, prev[2 * g - 1:2 * g], pltpu.roll(u[tm - g:], 1, axis=0))
    wrap2 = jnp.where(r8 == 0, prev[g - 1:g], pltpu.roll(u[tm - 2 * g:tm - g], 1, axis=0))
    u1 = jnp.concatenate([wrap1, u[:tm - g]], axis=0)
    u2 = jnp.concatenate([wrap2, wrap1, u[:tm - 2 * g]], axis=0)
    return cb + u2 * cw[0:1] + u1 * cw[1:2] + u * cw[2:3]


def _ffn_kernel(x_ref, hg_ref, ns_ref, mod_ref, wo_ref, g2_ref, wu_ref, cw_ref, cb_ref, wd_ref,
                o_ref, carry_ref, buf_ref, h2_ref, u_ref, g_ref, acc_ref, *, fc):
    mix = _dot(hg_ref[0], wo_ref[:HG_WIDTH, :]) + _dot(ns_ref[0], wo_ref[HG_WIDTH:, :])
    x1_nat = x_ref[0] + mod_ref[0, 2:3, :] * mix
    n_lt = D_MODEL // LANES
    tm = x1_nat.shape[0]
    groups = tm // SUBLANES
    pitch = buf_ref.shape[1] // SUBLANES
    for c in range(n_lt):
        for sg in range(SUBLANES):
            buf_ref[c, sg * pitch:sg * pitch + groups] = x1_nat[sg * groups:(sg + 1) * groups,
                                                                c * LANES:(c + 1) * LANES]

    x1 = jnp.concatenate(
        [jnp.concatenate([buf_ref[c, pl.ds(j, SUBLANES, stride=pitch), :] for j in range(groups)], axis=0)
         for c in range(n_lt)], axis=1)
    y = x1 * lax.rsqrt(jnp.mean(x1 * x1, axis=-1, keepdims=True) + EPS) * g2_ref[...]
    h2_ref[...] = (y * (1.0 + mod_ref[0, 4:5, :]) + mod_ref[0, 3:4, :]).astype(h2_ref.dtype)
    acc_ref[...] = jnp.zeros_like(acc_ref)
    first = pl.program_id(1) == 0

    def up(jc, slot):
        for half in range(2):
            off = pl.multiple_of(half * D_FF + jc * fc, fc)
            u_ref[slot, half] = _dot(h2_ref[...], wu_ref[:, pl.ds(off, fc)])

    def act(jc, slot):
        halves = []
        for half in range(2):
            cols = pl.ds(pl.multiple_of(half * D_FF + jc * fc, fc), fc)
            u = u_ref[slot, half]
            prev = jnp.where(first, 0.0, carry_ref[:, cols])
            carry_ref[:, cols] = u[tm - 2 * SUBLANES:]
            halves.append(_causal_conv(u, prev, cw_ref[:, cols], cb_ref[:, cols]))
        a, v = halves
        g_ref[slot] = (_silu(a) * v).astype(g_ref.dtype)

    def down(jc, slot):
        acc_ref[...] += _dot(g_ref[slot], wd_ref[pl.ds(pl.multiple_of(jc * fc, fc), fc), :])

    n = D_FF // fc

    def tick(i):
        if i < n:
            up(i, i % 2)
        if 0 <= i - 1 < n:
            act(i - 1, (i - 1) % 2)
        if 0 <= i - 2 < n:
            down(i - 2, i % 2)

    for i in range(n + 2):
        tick(i)

    out = x1 + mod_ref[0, 5:6, :] * acc_ref[...]
    for c in range(n_lt):
        for j in range(groups):
            buf_ref[c, pl.ds(j, SUBLANES, stride=pitch), :] = out[j * SUBLANES:(j + 1) * SUBLANES,
                                                                  c * LANES:(c + 1) * LANES]
    for c in range(n_lt):
        for sg in range(SUBLANES):
            o_ref[0, sg * groups:(sg + 1) * groups, c * LANES:(c + 1) * LANES] = (
                buf_ref[c, sg * pitch:sg * pitch + groups])


def _ffn(x, o_hg, o_nsa, mod, w_out, g2, w_up, conv_w, conv_b, w_down, tm, fc):
    bsz, t, _ = x.shape
    row_spec = lambda w: pl.BlockSpec((1, tm, w), lambda b, i: (b, i, 0))
    groups = tm // SUBLANES
    pad = SUBLANES if (groups // SUBLANES) % 2 == 0 else 0
    resident = lambda a: pl.BlockSpec(a.shape, lambda b, i: (0, 0), pipeline_mode=pl.Buffered(1))
    return pl.pallas_call(
        functools.partial(_ffn_kernel, fc=fc),
        grid=(bsz, t // tm),
        in_specs=[row_spec(D_MODEL), row_spec(HG_WIDTH), row_spec(NSA_WIDTH),
                  pl.BlockSpec((1, 6, D_MODEL), lambda b, i: (b, 0, 0)),
                  resident(w_out), resident(g2), resident(w_up), resident(conv_w), resident(conv_b),
                  resident(w_down)],
        out_specs=row_spec(D_MODEL),
        out_shape=jax.ShapeDtypeStruct(x.shape, F32),
        scratch_shapes=[pltpu.VMEM((2 * SUBLANES, 2 * D_FF), F32),
                        pltpu.VMEM((D_MODEL // LANES, tm + SUBLANES * pad, LANES), F32),
                        pltpu.VMEM((tm, D_MODEL), BF16),
                        pltpu.VMEM((2, 2, tm, fc), F32),
                        pltpu.VMEM((2, tm, fc), BF16),
                        pltpu.VMEM((tm, D_MODEL), F32)],
        compiler_params=pltpu.CompilerParams(
            dimension_semantics=("arbitrary", "arbitrary"), vmem_limit_bytes=VMEM_LIMIT),
        name="ffn",
    )(x, o_hg, o_nsa, mod, w_out, g2, w_up, conv_w, conv_b, w_down)


def _rope_tables():
    half = ROPE_DIM // 2
    inv = ROPE_THETA ** (-jnp.arange(half, dtype=F32) * 2.0 / ROPE_DIM)
    return (jnp.tile(inv.reshape(half, 1), (1, LANES)),)


def _gain_t(g):
    return jnp.tile(g.reshape(NSA_HD, 1), (LANES // NSA_HD, LANES))


def _selection_tables(t):
    n_seg = t // CMP_STRIDE
    nb = t // SLC_BLOCK
    cst = np.arange(n_seg) * CMP_STRIDE
    sst = np.arange(nb) * SLC_BLOCK
    ovl = np.clip(np.minimum(cst[:, None] + CMP_BLOCK, sst[None] + SLC_BLOCK)
                  - np.maximum(cst[:, None], sst[None]), 0, None) / CMP_BLOCK
    ovl[(t - CMP_BLOCK) // CMP_STRIDE + 1:] = 0.0
    return (jnp.asarray(ovl.T, dtype=BF16),)


def _layer(x, mod, pos_row, l, p, tables):
    bsz, t, _ = x.shape
    inv_t, mt = tables
    w_in_p = p["w_in"][l].astype(BF16)
    qg_t = _gain_t(p["q_norm_g"][l])
    kg_t = jnp.stack([_gain_t(p["k_norm_g"][l, br]) for br in range(N_BRANCH)])
    q_t, kc, vc, ks, vst, kw, vwt, gates_t, o_hg = _mix(
        x, mod, p["norm1_g"][l].reshape(1, D_MODEL), w_in_p, pos_row, inv_t, qg_t, kg_t,
        p["lb_logits"], p["hg_norm_g"][l].reshape(1, HG_HD), l, tm=TM_MIX)

    pe2 = p["pe_cmp"][l].reshape(2, 2, CMP_STRIDE * NSA_HD)
    kcmp, vcmp_t = _compress(kc, vc, pe2, p["w_cmp1"][l].astype(BF16), p["w_cmp2"][l].astype(BF16))
    bound = (SCORE_BOUND_MARGIN * NSA_HD ** 0.5 * jnp.max(jnp.abs(p["q_norm_g"][l]))
             * jnp.max(jnp.abs(p["k_norm_g"][l, 1:]))).reshape(1).astype(F32)
    nsa_args = (q_t, kcmp, vcmp_t, ks, vst, kw, vwt, gates_t, mt, bound)
    o_nsa = lax.cond(bound[0] <= MAX_SCORE_BOUND,
                     lambda a: _nsa(*a, tq=TQ_NSA, tks=TKS_NSA, bounded=True),
                     lambda a: _nsa(*a, tq=TQ_NSA, tks=TKS_NSA, bounded=False), nsa_args)

    return _ffn(x, o_hg, o_nsa, mod, p["w_out"][l].astype(BF16), p["norm2_g"][l].reshape(1, D_MODEL),
                p["w_up"][l].astype(BF16), p["conv_w"][l], p["conv_b"][l].reshape(1, 2 * D_FF),
                p["w_down"][l].astype(BF16), tm=TM_FFN, fc=FC_FFN)


def kernel(x, c, positions, w_ada, b_ada, norm1_g, w_in, lb_logits, hg_norm_g, q_norm_g, k_norm_g, pe_cmp, w_cmp1, w_cmp2, w_out, norm2_g, w_up, conv_w, conv_b, w_down):
    p = dict(w_in=w_in, norm1_g=norm1_g, lb_logits=lb_logits, hg_norm_g=hg_norm_g, q_norm_g=q_norm_g,
             k_norm_g=k_norm_g, pe_cmp=pe_cmp, w_cmp1=w_cmp1, w_cmp2=w_cmp2, w_out=w_out,
             norm2_g=norm2_g, w_up=w_up, conv_w=conv_w, conv_b=conv_b, w_down=w_down)
    bsz, t, _ = x.shape
    assert x.shape[2] == D_MODEL and t % TM_MIX == 0 and t % TKS_NSA == 0 and t >= WIN + TQ_NSA
    assert TKS_NSA % TQ_NSA == 0 and TM_MIX % HG_CHUNK == 0 and D_FF % FC_FFN == 0
    tables = _rope_tables() + _selection_tables(t)
    pos_row = positions.reshape(bsz, 1, t)
    for l in range(w_ada.shape[0]):
        mod = _ada(c, w_ada[l], b_ada[l]).reshape(bsz, 6, D_MODEL)
        x = _layer(x, mod, pos_row, l, p, tables)
    return x
```

```python
import functools

import jax
import jax.numpy as jnp
import numpy as np
from jax import lax
from jax.experimental import pallas as pl
from jax.experimental.pallas import tpu as pltpu

D_MODEL = 1024
HG_HEADS = 4
HG_HD = 128
HG_WIDTH = HG_HEADS * HG_HD
HG_CHUNK = 128
HG_SUB = 8
LOG2E = 1.4426950408889634
NSA_HEADS = 8
NSA_KV_HEADS = 2
NSA_HD = 64
NSA_GROUP = NSA_HEADS // NSA_KV_HEADS
NSA_CHAIN = 4
RANK_LANES = 4
NSA_WIDTH = NSA_HEADS * NSA_HD
N_BRANCH = 3
CMP_BLOCK = 32
CMP_STRIDE = 16
CMP_HIDDEN = 256
SLC_BLOCK = 64
SLC_TOPK = 16
WIN = 512
ROPE_DIM = NSA_HD // 4
ROPE_THETA = 500000.0
D_FF = 2816
CONV_W = 3
EPS = 1e-6
NEG = -1e30
SCORE_BOUND_MARGIN = 1.02
MAX_SCORE_BOUND = 40.0

LANES = 128
SUBLANES = 8
VMEM_LIMIT = 56 * 1024 * 1024

TM_MIX = 512
TQ_NSA = 256
TKS_NSA = 512
TM_FFN = 512
SUB_FFN = 256
HEAD_AT = 7
FC_FFN = 256

OFF_HG = 0
OFF_Q = 4 * HG_WIDTH
OFF_KV = OFF_Q + NSA_WIDTH
OFF_G = OFF_KV + 6 * NSA_KV_HEADS * NSA_HD
IN_COLS = OFF_G + N_BRANCH * NSA_HEADS
GATE_ROWS = N_BRANCH * NSA_HEADS
VT_ROWS = NSA_HD + SUBLANES

BF16 = jnp.bfloat16
F32 = jnp.float32


def _dot(a, b):
    return jnp.dot(a, b, preferred_element_type=F32)


def _dot_nt(a, b):
    return lax.dot_general(a, b, (((1,), (1,)), ((), ())), preferred_element_type=F32)


def _sigmoid(x):
    return 1.0 / (1.0 + jnp.exp(-x))


def _silu(x):
    return x * _sigmoid(x)


def _ada_kernel(c_ref, w_ref, b_ref, o_ref):
    cs = _silu(c_ref[...])
    o_ref[...] = _dot(cs.astype(BF16), w_ref[...].astype(BF16)) + b_ref[...]


def _ada(c, w, b):
    bsz = c.shape[0]
    n = w.shape[1]
    tn = D_MODEL
    return pl.pallas_call(
        _ada_kernel,
        grid=(n // tn,),
        in_specs=[pl.BlockSpec((bsz, D_MODEL), lambda j: (0, 0)),
                  pl.BlockSpec((D_MODEL, tn), lambda j: (0, j)),
                  pl.BlockSpec((1, tn), lambda j: (0, j))],
        out_specs=pl.BlockSpec((bsz, tn), lambda j: (0, j)),
        out_shape=jax.ShapeDtypeStruct((bsz, n), F32),
        name="ada",
    )(c, w, b.reshape(1, n))


def _pair_norm_rope_t(xt, g_t, cos_t, sin_t):
    half = ROPE_DIM // 2
    outs = []
    for hh in range(2):
        x = xt[hh * NSA_HD:(hh + 1) * NSA_HD]
        ms = jnp.mean(x * x, axis=0, keepdims=True)
        xn = x * lax.rsqrt(ms + EPS) * g_t[hh * NSA_HD:(hh + 1) * NSA_HD]
        x1, x2 = xn[:half], xn[half:ROPE_DIM]
        outs += [x1 * cos_t - x2 * sin_t, x2 * cos_t + x1 * sin_t, xn[ROPE_DIM:]]
    return jnp.concatenate(outs, axis=0)


def _project_parts(x_ref, mod_ref, g1_ref, w_ref, pos_ref, inv_ref, qg_ref, kg_ref,
                   zhg_out, qt_ref, kc_ref, vc_ref, ks_ref, vst_ref, kw_ref, vwt_ref, gt_ref, seg_ref):
    x = x_ref[0]
    ms = jnp.mean(x * x, axis=-1, keepdims=True)
    y = x * lax.rsqrt(ms + EPS) * g1_ref[...]
    h = (y * (1.0 + mod_ref[0, 1:2, :]) + mod_ref[0, 0:1, :]).astype(BF16)

    tm = x.shape[0]
    reps = tm // LANES

    def lane_tile(a):
        return jnp.concatenate([a] * reps, axis=1)

    ang = lane_tile(inv_ref[...]) * pos_ref[0].astype(F32)
    cos_t = jnp.cos(ang)
    sin_t = jnp.sin(ang)
    scale = NSA_HD ** -0.5 * LOG2E
    vals = {}

    def hg_cols(lo, hi):
        def part():
            zhg_out[:, lo:hi] = _dot(h, w_ref[:, OFF_HG + lo:OFF_HG + hi])
        return part

    def q_pair(p):
        def part():
            if p % 2 == 0:
                vals["zq"] = _dot(h, w_ref[:, OFF_Q + p * LANES:OFF_Q + (p + 2) * LANES])
                vals["qg"] = lane_tile(qg_ref[...])
            off = (p % 2) * LANES
            rt = _pair_norm_rope_t(vals["zq"][:, off:off + LANES].T, vals["qg"], cos_t, sin_t)
            rt = (rt * scale).astype(qt_ref.dtype)
            qt_ref[0, 2 * p] = rt[:NSA_HD]
            qt_ref[0, 2 * p + 1] = rt[NSA_HD:]
        return part

    def kv_dot(br):
        vals["zkv"] = _dot(h, w_ref[:, OFF_KV + 2 * br * LANES:OFF_KV + (2 * br + 2) * LANES])

    def store_segments(a, out_ref):
        seg_ref[...] = a
        n_seg = tm // CMP_STRIDE
        lane = lax.broadcasted_iota(jnp.int32, (n_seg, LANES), 1)
        for l in range(0, CMP_STRIDE, 2):
            even = seg_ref[pl.ds(l, n_seg, stride=CMP_STRIDE), :]
            odd = seg_ref[pl.ds(l + 1, n_seg, stride=CMP_STRIDE), :]
            cols = slice((l // 2) * LANES, (l // 2 + 1) * LANES)
            out_ref[0, 0, :, cols] = jnp.where(lane < NSA_HD, even, pltpu.roll(odd, NSA_HD, axis=1))
            out_ref[0, 1, :, cols] = jnp.where(lane < NSA_HD, pltpu.roll(even, NSA_HD, axis=1), odd)

    def key(br, k_ref):
        def part():
            kv_dot(br)
            kt = _pair_norm_rope_t(vals["zkv"][:, :LANES].T, lane_tile(kg_ref[br]), cos_t, sin_t)
            if br == 0:
                store_segments(kt.T, k_ref)
                store_segments(vals["zkv"][:, LANES:], vc_ref)
            else:
                kk = kt.T.astype(k_ref.dtype)
                for g in range(NSA_KV_HEADS):
                    k_ref[0, g] = kk[:, g * NSA_HD:(g + 1) * NSA_HD]
        return part

    def value_t(vt_ref):
        def part():
            vt = vals["zkv"][:, LANES:].T.astype(vt_ref.dtype)
            ones_row = jnp.where(lax.broadcasted_iota(jnp.int32, (SUBLANES, tm), 0) == 0, 1.0, 0.0).astype(vt_ref.dtype)
            for g in range(NSA_KV_HEADS):
                vt_ref[0, g, :NSA_HD] = vt[g * NSA_HD:(g + 1) * NSA_HD]
                vt_ref[0, g, NSA_HD:] = ones_row
        return part

    def gates_part():
        gates = _sigmoid(_dot(h, w_ref[:, OFF_G:IN_COLS]))
        wide = jnp.concatenate([gates, jnp.zeros((tm, LANES - GATE_ROWS), F32)], axis=1)
        gt_ref[0] = wide.T[:GATE_ROWS]

    wide = 2 * LANES
    hg = [hg_cols(lo, lo + wide) for lo in range(0, 4 * HG_WIDTH, wide)]
    rest = [q_pair(0), q_pair(1), q_pair(2), q_pair(3), key(0, kc_ref), key(1, ks_ref), value_t(vst_ref),
            key(2, kw_ref), value_t(vwt_ref), gates_part]
    parts = []
    for k in range(max(len(hg), len(rest))):
        parts += hg[k:k + 1] + rest[k:k + 1]
    return parts


def _hgrn_chunk(zq, zf, zi, zg, lb, hg_g, st_ref, bk_ref, h, tri, level):
    c = HG_CHUNK
    e_z = jnp.exp(-jnp.abs(zf))
    logsig = jnp.minimum(zf, 0.0) - jnp.log(1.0 + e_z)
    a = jnp.log(lb)
    bb = jnp.log1p(-lb) + logsig
    logf = jnp.maximum(a, bb) + jnp.log(1.0 + jnp.exp(-jnp.abs(a - bb)))
    k = (1.0 - lb) * (jnp.where(zf >= 0.0, e_z, 1.0) / (1.0 + e_z))
    q = _silu(zq)
    v = zi
    l_hi = logf.astype(BF16)
    l_mid = (logf - l_hi.astype(F32)).astype(BF16)
    l_lo = (logf - l_hi.astype(F32) - l_mid.astype(F32)).astype(BF16)
    bc = _dot(tri, l_hi) + _dot(tri, l_mid) + _dot(tri, l_lo)

    col = lax.broadcasted_iota(jnp.int32, (HG_SUB, c), 1)
    b2 = bc * LOG2E
    bk_ref[0, h] = b2
    bk_ref[1, h] = k
    rows_a = []
    for i in range(c // HG_SUB):
        lo = i * HG_SUB
        b_i = b2[lo:lo + HG_SUB]
        q_i = q[lo:lo + HG_SUB]
        k_i = k[lo:lo + HG_SUB]
        a_i = jnp.zeros((HG_SUB, c), F32)
        for s in range(HG_SUB):
            b_s = bk_ref[0, h, lo + s:lo + s + 1, :]
            k_s = bk_ref[1, h, lo + s:lo + s + 1, :]
            e = jnp.exp2(jnp.minimum(b_i - b_s, 0.0)) * q_i * k_s
            a_i = jnp.where(col == lo + s, jnp.sum(e, axis=-1, keepdims=True), a_i)
        rows_a.append(a_i)
    amat = jnp.concatenate(rows_a, axis=0)

    size, idx = c // 2, 1
    while size >= HG_SUB:
        pieces = []
        for e0 in range(0, c, 2 * size):
            o0 = e0 + size
            r = bc[o0:o0 + 1]
            pieces.append(k[e0:o0] * jnp.exp(r - bc[e0:o0]))
            pieces.append(q[o0:o0 + size] * jnp.exp(bc[o0:o0 + size] - r))
        hmat = jnp.concatenate(pieces, axis=0).astype(BF16)
        amat = jnp.where(level == idx, _dot_nt(hmat, hmat), amat)
        size, idx = size // 2, idx + 1
    amat = jnp.where(level >= 0, amat, 0.0)

    st = st_ref[h]
    o = _dot_nt((q * jnp.exp(bc)).astype(BF16), st.astype(BF16)) + _dot(amat.astype(BF16), v.astype(BF16))
    bl = bc[c - 1:c]
    kdec = (k * jnp.exp(bl - bc)).astype(BF16)
    st_ref[h] = jnp.exp(bl) * st + _dot(v.T.astype(BF16), kdec)

    y = o * lax.rsqrt(jnp.mean(o * o, axis=-1, keepdims=True) + EPS) * hg_g
    return y * _silu(zg)


def _hgrn_parts(z_in, lbl_ref, g_ref, o_ref, st_ref, bk_ref, l_idx):
    lg = lbl_ref[...]
    ex = jnp.exp(lg - jnp.max(lg, axis=0, keepdims=True))
    sm = ex / jnp.sum(ex, axis=0, keepdims=True)
    lb_all = jnp.sum(sm[:l_idx + 1], axis=0, keepdims=True)

    c = HG_CHUNK
    ri = lax.broadcasted_iota(jnp.int32, (c, c), 0)
    ci = lax.broadcasted_iota(jnp.int32, (c, c), 1)
    tri = jnp.where(ci <= ri, 1.0, 0.0).astype(BF16)
    level = jnp.where(ci > ri, -1, 0)
    size, idx = c // 2, 1
    while size >= HG_SUB:
        sh = size.bit_length() - 1
        paired = ((ri >> (sh + 1)) == (ci >> (sh + 1))) & ((ri >> sh) != (ci >> sh)) & (ci <= ri)
        level = jnp.where(paired, idx, level)
        size, idx = size // 2, idx + 1

    def chunk_head(ch, h):
        def part():
            rows = slice(ch * c, (ch + 1) * c)
            sl = slice(h * HG_HD, (h + 1) * HG_HD)
            zq = z_in[rows, h * HG_HD:(h + 1) * HG_HD]
            zf = z_in[rows, HG_WIDTH + h * HG_HD:HG_WIDTH + (h + 1) * HG_HD]
            zi = z_in[rows, 2 * HG_WIDTH + h * HG_HD:2 * HG_WIDTH + (h + 1) * HG_HD]
            zg = z_in[rows, 3 * HG_WIDTH + h * HG_HD:3 * HG_WIDTH + (h + 1) * HG_HD]
            y = _hgrn_chunk(zq, zf, zi, zg, lb_all[:, sl], g_ref[...], st_ref, bk_ref, h, tri, level)
            o_ref[0, rows, h * HG_HD:(h + 1) * HG_HD] = y.astype(o_ref.dtype)
        return part

    return [chunk_head(ch, h) for ch in range(z_in.shape[0] // c) for h in range(HG_HEADS)]


def _mix_kernel(x_ref, mod_ref, g1_ref, w_ref, pos_ref, inv_ref, qg_ref, kg_ref, lbl_ref, hgg_ref,
                qt_ref, kc_ref, vc_ref, ks_ref, vst_ref, kw_ref, vwt_ref, gt_ref, ohg_ref,
                zbuf_ref, st_ref, bk_ref, seg_ref, *, l_idx, tiles_per_seq):
    j = pl.program_id(0)

    @pl.when(j == 0)
    def _():
        zbuf_ref[0] = jnp.zeros((zbuf_ref.shape[1], zbuf_ref.shape[2]), F32)

    @pl.when((j == 0) | (lax.rem(j + tiles_per_seq - 1, tiles_per_seq) == 0))
    def _():
        st_ref[...] = jnp.zeros_like(st_ref)

    zbuf_ref[1] = zbuf_ref[0]
    rec = _hgrn_parts(zbuf_ref.at[1], lbl_ref, hgg_ref, ohg_ref, st_ref, bk_ref, l_idx)
    proj = _project_parts(x_ref, mod_ref, g1_ref, w_ref, pos_ref, inv_ref, qg_ref, kg_ref, zbuf_ref.at[0],
                          qt_ref, kc_ref, vc_ref, ks_ref, vst_ref, kw_ref, vwt_ref, gt_ref, seg_ref)
    for part in rec + proj:
        part()


def _mix(x, mod, g1, w_in_p, pos_row, inv_t, qg_t, kg_t, lb_logits, hg_g, l_idx, tm):
    bsz, t, _ = x.shape
    nm = t // tm
    n_tiles = bsz * nm

    def cur(j):
        jc = jnp.minimum(j, n_tiles - 1)
        return jc // nm, jc % nm

    def prev(j):
        jp = jnp.maximum(j - 1, 0)
        return jp // nm, jp % nm

    kv_shape = (bsz, NSA_KV_HEADS, t, NSA_HD)
    kv_spec = pl.BlockSpec((1, NSA_KV_HEADS, tm, NSA_HD), lambda j: (cur(j)[0], 0, cur(j)[1], 0))
    seg_w = CMP_STRIDE * NSA_HD
    seg_shape = (bsz, NSA_KV_HEADS, t // CMP_STRIDE, seg_w)
    seg_spec = pl.BlockSpec((1, NSA_KV_HEADS, tm // CMP_STRIDE, seg_w), lambda j: (cur(j)[0], 0, cur(j)[1], 0))
    vt_shape = (bsz, NSA_KV_HEADS, VT_ROWS, t)
    vt_spec = pl.BlockSpec((1, NSA_KV_HEADS, VT_ROWS, tm), lambda j: (cur(j)[0], 0, 0, cur(j)[1]))
    const = lambda j: (0, 0)
    return pl.pallas_call(
        functools.partial(_mix_kernel, l_idx=l_idx, tiles_per_seq=nm),
        grid=(n_tiles + 1,),
        in_specs=[pl.BlockSpec((1, tm, D_MODEL), lambda j: (cur(j)[0], cur(j)[1], 0)),
                  pl.BlockSpec((1, 6, D_MODEL), lambda j: (cur(j)[0], 0, 0)),
                  pl.BlockSpec((1, D_MODEL), const),
                  pl.BlockSpec((D_MODEL, IN_COLS), const),
                  pl.BlockSpec((1, 1, tm), lambda j: (cur(j)[0], 0, cur(j)[1])),
                  pl.BlockSpec(inv_t.shape, const),
                  pl.BlockSpec(qg_t.shape, const),
                  pl.BlockSpec(kg_t.shape, lambda j: (0, 0, 0)),
                  pl.BlockSpec(lb_logits.shape, const),
                  pl.BlockSpec((1, HG_HD), const)],
        out_specs=[pl.BlockSpec((1, NSA_HEADS, NSA_HD, tm), lambda j: (cur(j)[0], 0, 0, cur(j)[1])),
                   seg_spec, seg_spec, kv_spec, vt_spec, kv_spec, vt_spec,
                   pl.BlockSpec((1, GATE_ROWS, tm), lambda j: (cur(j)[0], 0, cur(j)[1])),
                   pl.BlockSpec((1, tm, HG_WIDTH), lambda j: (prev(j)[0], prev(j)[1], 0))],
        out_shape=[jax.ShapeDtypeStruct((bsz, NSA_HEADS, NSA_HD, t), BF16),
                   jax.ShapeDtypeStruct(seg_shape, F32),
                   jax.ShapeDtypeStruct(seg_shape, F32),
                   jax.ShapeDtypeStruct(kv_shape, BF16),
                   jax.ShapeDtypeStruct(vt_shape, BF16),
                   jax.ShapeDtypeStruct(kv_shape, BF16),
                   jax.ShapeDtypeStruct(vt_shape, BF16),
                   jax.ShapeDtypeStruct((bsz, GATE_ROWS, t), F32),
                   jax.ShapeDtypeStruct((bsz, t, HG_WIDTH), BF16)],
        scratch_shapes=[pltpu.VMEM((2, tm, 4 * HG_WIDTH), F32),
                        pltpu.VMEM((HG_HEADS, HG_HD, HG_HD), F32),
                        pltpu.VMEM((2, HG_HEADS, HG_CHUNK, HG_HD), F32),
                        pltpu.VMEM((tm, LANES), F32)],
        compiler_params=pltpu.CompilerParams(
            dimension_semantics=("arbitrary",), vmem_limit_bytes=VMEM_LIMIT),
        name="mix",
    )(x, mod, g1, w_in_p, pos_row, inv_t, qg_t, kg_t, lb_logits, hg_g)


def _compress_kernel(xk_ref, xv_ref, pe_ref, w1_ref, w2_ref, ko_ref, vo_ref):
    half = CMP_STRIDE * NSA_HD
    outs = []
    for kv, x_ref in enumerate((xk_ref, xv_ref)):
        x = x_ref[0, 0]
        ha = _dot((x + pe_ref[kv, 0:1, :]).astype(BF16), w1_ref[kv, :half, :])
        hb = _dot((x + pe_ref[kv, 1:2, :]).astype(BF16), w1_ref[kv, half:, :])
        n = x.shape[0]
        pre = ha + pltpu.roll(hb, n - 1, axis=0)
        outs.append(_dot(_silu(pre).astype(BF16), w2_ref[kv]))
    ko_ref[0, 0] = outs[0].astype(ko_ref.dtype)
    vo_ref[0, 0] = outs[1].T.astype(vo_ref.dtype)


def _compress(xk, xv, pe2, w1, w2):
    bsz, g, nseg, width = xk.shape
    x_spec = pl.BlockSpec((1, 1, nseg, width), lambda b, j: (b, j, 0, 0))
    return pl.pallas_call(
        _compress_kernel,
        grid=(bsz, g),
        in_specs=[x_spec, x_spec,
                  pl.BlockSpec(pe2.shape, lambda b, j: (0, 0, 0)),
                  pl.BlockSpec(w1.shape, lambda b, j: (0, 0, 0)),
                  pl.BlockSpec(w2.shape, lambda b, j: (0, 0, 0))],
        out_specs=[pl.BlockSpec((1, 1, nseg, NSA_HD), lambda b, j: (b, j, 0, 0)),
                   pl.BlockSpec((1, 1, NSA_HD, nseg), lambda b, j: (b, j, 0, 0))],
        out_shape=[jax.ShapeDtypeStruct((bsz, g, nseg, NSA_HD), BF16),
                   jax.ShapeDtypeStruct((bsz, g, NSA_HD, nseg), BF16)],
        compiler_params=pltpu.CompilerParams(
            dimension_semantics=("arbitrary", "arbitrary"), vmem_limit_bytes=VMEM_LIMIT),
        name="compress",
    )(xk, xv, pe2, w1, w2)


def _nsa_kernel(bound_ref, qt_ref, kc_ref, vct_ref, ks_ref, vst_ref, kw_ref, vwt_ref, gt_ref, mt_ref,
                o_ref, sel_ref, *, tq, tks, bounded):
    r = NSA_GROUP
    n_g = NSA_KV_HEADS
    ch = NSA_CHAIN
    lanes = ch * tq
    q0 = pl.program_id(1) * tq
    chains = [(g, g * r + c * ch) for g in range(n_g) for c in range(r // ch)]
    q_ts = [jnp.concatenate([qt_ref[0, h0 + i] for i in range(ch)], axis=1)
            for _, h0 in chains]

    def tile_heads(a):
        return jnp.concatenate([a] * ch, axis=1)

    n_blk = kc_ref.shape[2]
    nb = mt_ref.shape[0]
    blk_end = lax.broadcasted_iota(jnp.int32, (n_blk, tq), 0) * CMP_STRIDE + (CMP_BLOCK - 1)
    t_row = q0 + lax.broadcasted_iota(jnp.int32, (1, tq), 1)
    cvalid = tile_heads(jnp.where(blk_end <= t_row, 1.0, 0.0)) > 0.5
    some = tile_heads((t_row >= CMP_BLOCK - 1).astype(F32))
    j = lax.broadcasted_iota(jnp.int32, (nb, tq), 0)
    cur = jnp.right_shift(t_row, SLC_BLOCK.bit_length() - 1)
    forced = (j == 0) | (j == cur) | (j == cur - 1)
    o_cmp = []
    psum = [None] * n_g
    for (g, _), q_t in zip(chains, q_ts):
        s = jnp.where(cvalid, _dot(kc_ref[0, g], q_t), NEG)
        e = jnp.exp2(s - jnp.max(s, axis=0, keepdims=True))
        p = e * (some / jnp.sum(e, axis=0, keepdims=True))
        o_cmp.append(_dot(vct_ref[0, g], p.astype(BF16)))
        for i in range(ch):
            part = p[:, i * tq:(i + 1) * tq]
            psum[g] = part if psum[g] is None else psum[g] + part
    for g in range(n_g):
        p_hi = psum[g].astype(BF16)
        p_lo = (psum[g] - p_hi.astype(F32)).astype(BF16)
        imp = _dot(mt_ref[...], p_hi) + _dot(mt_ref[...], p_lo)
        imp = jnp.where(j <= cur, jnp.where(forced, jnp.inf, imp), -1.0)
        ranks = [jnp.zeros((nb, tq), jnp.int32) for _ in range(RANK_LANES)]
        for i in range(nb):
            row_i = imp[i:i + 1, :]
            ahead = (row_i > imp) | ((row_i == imp) & (j > i))
            ranks[i % RANK_LANES] = ranks[i % RANK_LANES] + ahead.astype(jnp.int32)
        rank = functools.reduce(lambda a, b: a + b, ranks)
        sel_ref[g] = jnp.where(rank < min(SLC_TOPK, nb), 0.0, NEG)

    shift = bound_ref[0] * LOG2E if bounded else None

    span = WIN + tq
    start = pl.multiple_of(jnp.maximum(q0 - WIN, 0), tq)
    dist = (lax.broadcasted_iota(jnp.int32, (span, tq), 1)
            - lax.broadcasted_iota(jnp.int32, (span, tq), 0)) + (q0 - start)
    wbias = jnp.where((dist >= 0) & (dist < WIN), 0.0, NEG)
    wbias = tile_heads(wbias - shift if bounded else wbias)
    o_win = []
    for (g, _), q_t in zip(chains, q_ts):
        sw = _dot(kw_ref[0, g, pl.ds(start, span), :], q_t) + wbias
        ew = jnp.exp2(sw if bounded else sw - jnp.max(sw, axis=0, keepdims=True))
        ow = _dot(vwt_ref[0, g, :, pl.ds(start, span)], ew.astype(BF16))
        o_win.append(ow[:NSA_HD] * (1.0 / ow[NSA_HD:NSA_HD + 1]))

    blocks_per_tile = tks // SLC_BLOCK
    rel = (lax.broadcasted_iota(jnp.int32, (tks, tq), 1)
           - lax.broadcasted_iota(jnp.int32, (tks, tq), 0))

    def slc_bias(it, k0, need_causal):
        biases = []
        for g in range(n_g):
            if isinstance(it, int):
                rows = sel_ref[g, it * blocks_per_tile:(it + 1) * blocks_per_tile, :]
            else:
                rows = sel_ref[g, pl.ds(pl.multiple_of(it * blocks_per_tile, blocks_per_tile), blocks_per_tile), :]
            bias = jnp.concatenate([jnp.broadcast_to(rows[jj:jj + 1], (SLC_BLOCK, tq))
                                    for jj in range(blocks_per_tile)], axis=0)
            if need_causal:
                bias = jnp.where(rel + (q0 - k0) >= 0, bias, NEG)
            biases.append(tile_heads(bias - shift if bounded else bias))
        return biases

    def slc_body(it, carry, need_causal=True):
        k0 = it * tks if isinstance(it, int) else pl.multiple_of(it * tks, tks)
        biases = slc_bias(it, k0, need_causal)
        out = []
        for (g, _), q_t, state in zip(chains, q_ts, carry):
            sc = _dot(ks_ref[0, g, pl.ds(k0, tks), :], q_t) + biases[g]
            vt = vst_ref[0, g, :, pl.ds(k0, tks)]
            if bounded:
                (acc,) = state
                acc = acc + _dot(vt, jnp.exp2(sc).astype(BF16))
                out.append((acc,))
            else:
                m, acc = state
                m_new = jnp.maximum(m, jnp.max(sc, axis=0, keepdims=True))
                acc = jnp.exp2(m - m_new) * acc + _dot(vt, jnp.exp2(sc - m_new).astype(BF16))
                out.append((m_new, acc))
        return tuple(out)

    zero_state = (jnp.zeros((VT_ROWS, lanes), F32),)
    init = tuple(zero_state if bounded else (jnp.full((1, lanes), NEG, F32),) + zero_state for _ in chains)
    n_it = (q0 + tq + tks - 1) // tks
    if bounded:
        def slc_tiles(n):
            carry = init
            for it in range(n):
                carry = slc_body(it, carry, need_causal=(it == n - 1))
            return carry

        max_it = ks_ref.shape[2] // tks
        fin = lax.switch(n_it - 1, [functools.partial(slc_tiles, n) for n in range(1, max_it + 1)])
    else:
        fin = lax.fori_loop(0, n_it, slc_body, init)
    o_slc = [st[-1][:NSA_HD] * (1.0 / st[-1][NSA_HD:NSA_HD + 1]) for st in fin]

    for ci, (_, h0) in enumerate(chains):
        for i in range(ch):
            cols = slice(i * tq, (i + 1) * tq)
            head = h0 + i
            o_h = jnp.zeros((NSA_HD, tq), F32)
            for br, o_b in enumerate((o_cmp[ci], o_slc[ci], o_win[ci])):
                row = head * N_BRANCH + br
                o_h = o_h + gt_ref[0, row:row + 1, :] * o_b[:, cols]
            o_ref[0, :, head * NSA_HD:(head + 1) * NSA_HD] = o_h.T.astype(o_ref.dtype)


def _nsa(q_t, kc, vct, ks, vst, kw, vwt, gates_t, mt, bound, tq, tks, bounded):
    bsz, _, _, t = q_t.shape
    n_blk = kc.shape[2]
    n_g = NSA_KV_HEADS
    full = lambda b, i: (b, 0, 0, 0)
    k_spec = pl.BlockSpec((1, n_g, t, NSA_HD), full)
    vt_spec = pl.BlockSpec((1, n_g, VT_ROWS, t), full)
    return pl.pallas_call(
        functools.partial(_nsa_kernel, tq=tq, tks=tks, bounded=bounded),
        grid=(bsz, t // tq),
        in_specs=[pl.BlockSpec(memory_space=pltpu.SMEM),
                  pl.BlockSpec((1, NSA_HEADS, NSA_HD, tq), lambda b, i: (b, 0, 0, i)),
                  pl.BlockSpec((1, n_g, n_blk, NSA_HD), full),
                  pl.BlockSpec((1, n_g, NSA_HD, n_blk), full),
                  k_spec, vt_spec, k_spec, vt_spec,
                  pl.BlockSpec((1, GATE_ROWS, tq), lambda b, i: (b, 0, i)),
                  pl.BlockSpec(mt.shape, lambda b, i: (0, 0))],
        out_specs=pl.BlockSpec((1, tq, NSA_WIDTH), lambda b, i: (b, i, 0)),
        out_shape=jax.ShapeDtypeStruct((bsz, t, NSA_WIDTH), BF16),
        scratch_shapes=[pltpu.VMEM((n_g, mt.shape[0], tq), F32)],
        compiler_params=pltpu.CompilerParams(
            dimension_semantics=("arbitrary", "arbitrary"), vmem_limit_bytes=VMEM_LIMIT),
        name="nsa_bounded" if bounded else "nsa",
    )(bound, q_t, kc, vct, ks, vst, kw, vwt, gates_t, mt)


def _causal_conv(u, prev, cw, cb):
    tm = u.shape[0]
    g = SUBLANES
    r8 = lax.broadcasted_iota(jnp.int32, (g, u.shape[1]), 0)
    wrap1 = jnp.where(r8 == 0, prev[2 * g - 1:2 * g], pltpu.roll(u[tm - g:], 1, axis=0))
    wrap2 = jnp.where(r8 == 0, prev[g - 1:g], pltpu.roll(u[tm - 2 * g:tm - g], 1, axis=0))
    u1 = jnp.concatenate([wrap1, u[:tm - g]], axis=0)
    u2 = jnp.concatenate([wrap2, wrap1, u[:tm - 2 * g]], axis=0)
    return cb + u2 * cw[0:1] + u1 * cw[1:2] + u * cw[2:3]


def _ffn_kernel(x_ref, hg_ref, ns_ref, mod_ref, wo_ref, g2_ref, wu_ref, cw_ref, cb_ref, wd_ref,
                o_ref, carry_ref, buf_ref, x1_ref, h2_ref, u_ref, g_ref, acc_ref, *, fc):
    n_sub, sub = x1_ref.shape[0], x1_ref.shape[1]
    n_lt = D_MODEL // LANES
    groups = sub // SUBLANES
    pitch = buf_ref.shape[2] // SUBLANES
    first = pl.program_id(1) == 0

    def head(s):
        rows = slice(s * sub, (s + 1) * sub)
        mix = _dot(hg_ref[0, rows], wo_ref[:HG_WIDTH, :]) + _dot(ns_ref[0, rows], wo_ref[HG_WIDTH:, :])
        x1_nat = x_ref[0, rows] + mod_ref[0, 2:3, :] * mix
        for c in range(n_lt):
            for sg in range(SUBLANES):
                buf_ref[s, c, sg * pitch:sg * pitch + groups] = x1_nat[sg * groups:(sg + 1) * groups,
                                                                       c * LANES:(c + 1) * LANES]
        x1 = jnp.concatenate(
            [jnp.concatenate([buf_ref[s, c, pl.ds(j, SUBLANES, stride=pitch), :] for j in range(groups)], axis=0)
             for c in range(n_lt)], axis=1)
        x1_ref[s] = x1
        y = x1 * lax.rsqrt(jnp.mean(x1 * x1, axis=-1, keepdims=True) + EPS) * g2_ref[...]
        h2_ref[s] = (y * (1.0 + mod_ref[0, 4:5, :]) + mod_ref[0, 3:4, :]).astype(h2_ref.dtype)
        acc_ref[s] = jnp.zeros((sub, D_MODEL), F32)

    def tail(s):
        out = x1_ref[s] + mod_ref[0, 5:6, :] * acc_ref[s]
        for c in range(n_lt):
            for j in range(groups):
                buf_ref[s, c, pl.ds(j, SUBLANES, stride=pitch), :] = out[j * SUBLANES:(j + 1) * SUBLANES,
                                                                         c * LANES:(c + 1) * LANES]
        for c in range(n_lt):
            for sg in range(SUBLANES):
                o_ref[0, s * sub + sg * groups:s * sub + (sg + 1) * groups, c * LANES:(c + 1) * LANES] = (
                    buf_ref[s, c, sg * pitch:sg * pitch + groups])

    n = D_FF // fc

    def up(c, slot):
        s, jc = divmod(c, n)
        for half in range(2):
            off = half * D_FF + jc * fc
            u_ref[slot, half] = _dot(h2_ref[s], wu_ref[:, off:off + fc])

    def act(c, slot):
        s, jc = divmod(c, n)
        halves = []
        for half in range(2):
            cols = slice(half * D_FF + jc * fc, half * D_FF + (jc + 1) * fc)
            u = u_ref[slot, half]
            prev = jnp.where(first, 0.0, carry_ref[:, cols]) if s == 0 else carry_ref[:, cols]
            carry_ref[:, cols] = u[sub - 2 * SUBLANES:]
            halves.append(_causal_conv(u, prev, cw_ref[:, cols], cb_ref[:, cols]))
        a, v = halves
        g_ref[slot] = (_silu(a) * v).astype(g_ref.dtype)

    def down(c, slot):
        s, jc = divmod(c, n)
        acc_ref[s] += _dot(g_ref[slot], wd_ref[jc * fc:(jc + 1) * fc, :])

    total = n_sub * n
    head(0)
    for i in range(total + 2):
        if i < total:
            up(i, i % 2)
        if 0 <= i - 1 < total:
            act(i - 1, (i - 1) % 2)
        if 0 <= i - 2 < total:
            down(i - 2, i % 2)
        s, jc = divmod(i, n)
        if jc == HEAD_AT and s + 1 < n_sub:
            head(s + 1)
        if jc == 1 and 1 <= s <= n_sub:
            tail(s - 1)


def _ffn(x, o_hg, o_nsa, mod, w_out, g2, w_up, conv_w, conv_b, w_down, tm, sub, fc):
    bsz, t, _ = x.shape
    row_spec = lambda w: pl.BlockSpec((1, tm, w), lambda b, i: (b, i, 0))
    groups = sub // SUBLANES
    pad = SUBLANES if (groups // SUBLANES) % 2 == 0 else 0
    n_sub = tm // sub
    resident = lambda a: pl.BlockSpec(a.shape, lambda b, i: (0, 0), pipeline_mode=pl.Buffered(1))
    return pl.pallas_call(
        functools.partial(_ffn_kernel, fc=fc),
        grid=(bsz, t // tm),
        in_specs=[row_spec(D_MODEL), row_spec(HG_WIDTH), row_spec(NSA_WIDTH),
                  pl.BlockSpec((1, 6, D_MODEL), lambda b, i: (b, 0, 0)),
                  resident(w_out), resident(g2), resident(w_up), resident(conv_w), resident(conv_b),
                  resident(w_down)],
        out_specs=row_spec(D_MODEL),
        out_shape=jax.ShapeDtypeStruct(x.shape, F32),
        scratch_shapes=[pltpu.VMEM((2 * SUBLANES, 2 * D_FF), F32),
                        pltpu.VMEM((n_sub, D_MODEL // LANES, sub + SUBLANES * pad, LANES), F32),
                        pltpu.VMEM((n_sub, sub, D_MODEL), F32),
                        pltpu.VMEM((n_sub, sub, D_MODEL), BF16),
                        pltpu.VMEM((2, 2, sub, fc), F32),
                        pltpu.VMEM((2, sub, fc), BF16),
                        pltpu.VMEM((n_sub, sub, D_MODEL), F32)],
        compiler_params=pltpu.CompilerParams(
            dimension_semantics=("arbitrary", "arbitrary"), vmem_limit_bytes=VMEM_LIMIT),
        name="ffn",
    )(x, o_hg, o_nsa, mod, w_out, g2, w_up, conv_w, conv_b, w_down)


def _rope_tables():
    half = ROPE_DIM // 2
    inv = ROPE_THETA ** (-jnp.arange(half, dtype=F32) * 2.0 / ROPE_DIM)
    return (jnp.tile(inv.reshape(half, 1), (1, LANES)),)


def _gain_t(g):
    return jnp.tile(g.reshape(NSA_HD, 1), (LANES // NSA_HD, LANES))


def _selection_tables(t):
    n_seg = t // CMP_STRIDE
    nb = t // SLC_BLOCK
    cst = np.arange(n_seg) * CMP_STRIDE
    sst = np.arange(nb) * SLC_BLOCK
    ovl = np.clip(np.minimum(cst[:, None] + CMP_BLOCK, sst[None] + SLC_BLOCK)
                  - np.maximum(cst[:, None], sst[None]), 0, None) / CMP_BLOCK
    ovl[(t - CMP_BLOCK) // CMP_STRIDE + 1:] = 0.0
    return (jnp.asarray(ovl.T, dtype=BF16),)


def _layer(x, mod, pos_row, l, p, tables):
    bsz, t, _ = x.shape
    inv_t, mt = tables
    w_in_p = p["w_in"][l].astype(BF16)
    qg_t = _gain_t(p["q_norm_g"][l])
    kg_t = jnp.stack([_gain_t(p["k_norm_g"][l, br]) for br in range(N_BRANCH)])
    q_t, kc, vc, ks, vst, kw, vwt, gates_t, o_hg = _mix(
        x, mod, p["norm1_g"][l].reshape(1, D_MODEL), w_in_p, pos_row, inv_t, qg_t, kg_t,
        p["lb_logits"], p["hg_norm_g"][l].reshape(1, HG_HD), l, tm=TM_MIX)

    pe2 = p["pe_cmp"][l].reshape(2, 2, CMP_STRIDE * NSA_HD)
    kcmp, vcmp_t = _compress(kc, vc, pe2, p["w_cmp1"][l].astype(BF16), p["w_cmp2"][l].astype(BF16))
    bound = (SCORE_BOUND_MARGIN * NSA_HD ** 0.5 * jnp.max(jnp.abs(p["q_norm_g"][l]))
             * jnp.max(jnp.abs(p["k_norm_g"][l, 1:]))).reshape(1).astype(F32)
    nsa_args = (q_t, kcmp, vcmp_t, ks, vst, kw, vwt, gates_t, mt, bound)
    o_nsa = lax.cond(bound[0] <= MAX_SCORE_BOUND,
                     lambda a: _nsa(*a, tq=TQ_NSA, tks=TKS_NSA, bounded=True),
                     lambda a: _nsa(*a, tq=TQ_NSA, tks=TKS_NSA, bounded=False), nsa_args)

    return _ffn(x, o_hg, o_nsa, mod, p["w_out"][l].astype(BF16), p["norm2_g"][l].reshape(1, D_MODEL),
                p["w_up"][l].astype(BF16), p["conv_w"][l], p["conv_b"][l].reshape(1, 2 * D_FF),
                p["w_down"][l].astype(BF16), tm=TM_FFN, sub=SUB_FFN, fc=FC_FFN)


def kernel(x, c, positions, w_ada, b_ada, norm1_g, w_in, lb_logits, hg_norm_g, q_norm_g, k_norm_g, pe_cmp, w_cmp1, w_cmp2, w_out, norm2_g, w_up, conv_w, conv_b, w_down):
    p = dict(w_in=w_in, norm1_g=norm1_g, lb_logits=lb_logits, hg_norm_g=hg_norm_g, q_norm_g=q_norm_g,
             k_norm_g=k_norm_g, pe_cmp=pe_cmp, w_cmp1=w_cmp1, w_cmp2=w_cmp2, w_out=w_out,
             norm2_g=norm2_g, w_up=w_up, conv_w=conv_w, conv_b=conv_b, w_down=w_down)
    bsz, t, _ = x.shape
    assert x.shape[2] == D_MODEL and t % TM_MIX == 0 and t % TKS_NSA == 0 and t >= WIN + TQ_NSA
    assert TKS_NSA % TQ_NSA == 0 and TM_MIX % HG_CHUNK == 0 and D_FF % FC_FFN == 0 and TM_FFN % SUB_FFN == 0
    tables = _rope_tables() + _selection_tables(t)
    pos_row = positions.reshape(bsz, 1, t)
    for l in range(w_ada.shape[0]):
        mod = _ada(c, w_ada[l], b_ada[l]).reshape(bsz, 6, D_MODEL)
        x = _layer(x, mod, pos_row, l, p, tables)
    return x
```

```python
import functools

import jax
import jax.numpy as jnp
import numpy as np
from jax import lax
from jax.experimental import pallas as pl
from jax.experimental.pallas import tpu as pltpu

D_MODEL = 1024
HG_HEADS = 4
HG_HD = 128
HG_WIDTH = HG_HEADS * HG_HD
HG_CHUNK = 128
HG_SUB = 8
LOG2E = 1.4426950408889634
NSA_HEADS = 8
NSA_KV_HEADS = 2
NSA_HD = 64
NSA_GROUP = NSA_HEADS // NSA_KV_HEADS
NSA_CHAIN = 4
RANK_LANES = 4
NSA_WIDTH = NSA_HEADS * NSA_HD
N_BRANCH = 3
CMP_BLOCK = 32
CMP_STRIDE = 16
CMP_HIDDEN = 256
SLC_BLOCK = 64
SLC_TOPK = 16
WIN = 512
ROPE_DIM = NSA_HD // 4
ROPE_THETA = 500000.0
D_FF = 2816
CONV_W = 3
EPS = 1e-6
NEG = -1e30
SCORE_BOUND_MARGIN = 1.02
MAX_SCORE_BOUND = 40.0

LANES = 128
SUBLANES = 8
VMEM_LIMIT = 56 * 1024 * 1024

TM_MIX = 512
TQ_NSA = 256
TKS_NSA = 512
TM_FFN = 512
SUB_FFN = 256
HEAD_AT = 7
FC_FFN = 256

OFF_HG = 0
OFF_Q = 4 * HG_WIDTH
OFF_KV = OFF_Q + NSA_WIDTH
OFF_G = OFF_KV + 6 * NSA_KV_HEADS * NSA_HD
IN_COLS = OFF_G + N_BRANCH * NSA_HEADS
GATE_ROWS = N_BRANCH * NSA_HEADS
VT_ROWS = NSA_HD + SUBLANES

BF16 = jnp.bfloat16
F32 = jnp.float32


def _dot(a, b):
    return jnp.dot(a, b, preferred_element_type=F32)


def _dot_nt(a, b):
    return lax.dot_general(a, b, (((1,), (1,)), ((), ())), preferred_element_type=F32)


def _sigmoid(x):
    return 1.0 / (1.0 + jnp.exp(-x))


def _silu(x):
    return x * _sigmoid(x)


def _ada_kernel(c_ref, w_ref, b_ref, o_ref):
    cs = _silu(c_ref[...])
    o_ref[...] = _dot(cs.astype(BF16), w_ref[...].astype(BF16)) + b_ref[...]


def _ada(c, w, b):
    bsz = c.shape[0]
    n = w.shape[1]
    tn = D_MODEL
    return pl.pallas_call(
        _ada_kernel,
        grid=(n // tn,),
        in_specs=[pl.BlockSpec((bsz, D_MODEL), lambda j: (0, 0)),
                  pl.BlockSpec((D_MODEL, tn), lambda j: (0, j)),
                  pl.BlockSpec((1, tn), lambda j: (0, j))],
        out_specs=pl.BlockSpec((bsz, tn), lambda j: (0, j)),
        out_shape=jax.ShapeDtypeStruct((bsz, n), F32),
        name="ada",
    )(c, w, b.reshape(1, n))


def _pair_norm_rope_t(xt, g_t, cos_t, sin_t):
    half = ROPE_DIM // 2
    outs = []
    for hh in range(2):
        x = xt[hh * NSA_HD:(hh + 1) * NSA_HD]
        ms = jnp.mean(x * x, axis=0, keepdims=True)
        xn = x * lax.rsqrt(ms + EPS) * g_t[hh * NSA_HD:(hh + 1) * NSA_HD]
        x1, x2 = xn[:half], xn[half:ROPE_DIM]
        outs += [x1 * cos_t - x2 * sin_t, x2 * cos_t + x1 * sin_t, xn[ROPE_DIM:]]
    return jnp.concatenate(outs, axis=0)


def _project_parts(x_ref, mod_ref, g1_ref, w_ref, pos_ref, inv_ref, qg_ref, kg_ref,
                   zhg_out, qt_ref, kc_ref, vc_ref, ks_ref, vst_ref, kw_ref, vwt_ref, gt_ref, seg_ref):
    x = x_ref[0]
    ms = jnp.mean(x * x, axis=-1, keepdims=True)
    y = x * lax.rsqrt(ms + EPS) * g1_ref[...]
    h = (y * (1.0 + mod_ref[0, 1:2, :]) + mod_ref[0, 0:1, :]).astype(BF16)

    tm = x.shape[0]
    reps = tm // LANES

    def lane_tile(a):
        return jnp.concatenate([a] * reps, axis=1)

    ang = lane_tile(inv_ref[...]) * pos_ref[0].astype(F32)
    cos_t = jnp.cos(ang)
    sin_t = jnp.sin(ang)
    scale = NSA_HD ** -0.5 * LOG2E
    vals = {}

    def hg_cols(lo, hi):
        def part():
            zhg_out[:, lo:hi] = _dot(h, w_ref[:, OFF_HG + lo:OFF_HG + hi])
        return part

    def q_pair(p):
        def part():
            if p % 2 == 0:
                vals["zq"] = _dot(h, w_ref[:, OFF_Q + p * LANES:OFF_Q + (p + 2) * LANES])
                vals["qg"] = lane_tile(qg_ref[...])
            off = (p % 2) * LANES
            rt = _pair_norm_rope_t(vals["zq"][:, off:off + LANES].T, vals["qg"], cos_t, sin_t)
            rt = (rt * scale).astype(qt_ref.dtype)
            qt_ref[0, 2 * p] = rt[:NSA_HD]
            qt_ref[0, 2 * p + 1] = rt[NSA_HD:]
        return part

    def kv_dot(br):
        vals["zkv"] = _dot(h, w_ref[:, OFF_KV + 2 * br * LANES:OFF_KV + (2 * br + 2) * LANES])

    def store_segments(a, out_ref):
        seg_ref[...] = a
        n_seg = tm // CMP_STRIDE
        lane = lax.broadcasted_iota(jnp.int32, (n_seg, LANES), 1)
        for l in range(0, CMP_STRIDE, 2):
            even = seg_ref[pl.ds(l, n_seg, stride=CMP_STRIDE), :]
            odd = seg_ref[pl.ds(l + 1, n_seg, stride=CMP_STRIDE), :]
            cols = slice((l // 2) * LANES, (l // 2 + 1) * LANES)
            out_ref[0, 0, :, cols] = jnp.where(lane < NSA_HD, even, pltpu.roll(odd, NSA_HD, axis=1))
            out_ref[0, 1, :, cols] = jnp.where(lane < NSA_HD, pltpu.roll(even, NSA_HD, axis=1), odd)

    def key(br, k_ref):
        def part():
            kv_dot(br)
            kt = _pair_norm_rope_t(vals["zkv"][:, :LANES].T, lane_tile(kg_ref[br]), cos_t, sin_t)
            if br == 0:
                store_segments(kt.T, k_ref)
                store_segments(vals["zkv"][:, LANES:], vc_ref)
            else:
                kk = kt.T.astype(k_ref.dtype)
                for g in range(NSA_KV_HEADS):
                    k_ref[0, g] = kk[:, g * NSA_HD:(g + 1) * NSA_HD]
        return part

    def value_t(vt_ref):
        def part():
            vt = vals["zkv"][:, LANES:].T.astype(vt_ref.dtype)
            ones_row = jnp.where(lax.broadcasted_iota(jnp.int32, (SUBLANES, tm), 0) == 0, 1.0, 0.0).astype(vt_ref.dtype)
            for g in range(NSA_KV_HEADS):
                vt_ref[0, g, :NSA_HD] = vt[g * NSA_HD:(g + 1) * NSA_HD]
                vt_ref[0, g, NSA_HD:] = ones_row
        return part

    def gates_part():
        gates = _sigmoid(_dot(h, w_ref[:, OFF_G:IN_COLS]))
        wide = jnp.concatenate([gates, jnp.zeros((tm, LANES - GATE_ROWS), F32)], axis=1)
        gt_ref[0] = wide.T[:GATE_ROWS]

    wide = 2 * LANES
    hg = [hg_cols(lo, lo + wide) for lo in range(0, 4 * HG_WIDTH, wide)]
    rest = [q_pair(0), q_pair(1), q_pair(2), q_pair(3), key(0, kc_ref), key(1, ks_ref), value_t(vst_ref),
            key(2, kw_ref), value_t(vwt_ref), gates_part]
    parts = []
    for k in range(max(len(hg), len(rest))):
        parts += hg[k:k + 1] + rest[k:k + 1]
    return parts


def _hgrn_chunk(zq, zf, zi, zg, lb, hg_g, st_ref, bk_ref, h, tri, level):
    c = HG_CHUNK
    e_z = jnp.exp(-jnp.abs(zf))
    logsig = jnp.minimum(zf, 0.0) - jnp.log(1.0 + e_z)
    a = jnp.log(lb)
    bb = jnp.log1p(-lb) + logsig
    logf = jnp.maximum(a, bb) + jnp.log(1.0 + jnp.exp(-jnp.abs(a - bb)))
    k = (1.0 - lb) * (jnp.where(zf >= 0.0, e_z, 1.0) / (1.0 + e_z))
    q = _silu(zq)
    v = zi
    l_hi = logf.astype(BF16)
    l_mid = (logf - l_hi.astype(F32)).astype(BF16)
    l_lo = (logf - l_hi.astype(F32) - l_mid.astype(F32)).astype(BF16)
    bc = _dot(tri, l_hi) + _dot(tri, l_mid) + _dot(tri, l_lo)

    col = lax.broadcasted_iota(jnp.int32, (HG_SUB, c), 1)
    b2 = bc * LOG2E
    bk_ref[0, h] = b2
    bk_ref[1, h] = k
    rows_a = []
    for i in range(c // HG_SUB):
        lo = i * HG_SUB
        b_i = b2[lo:lo + HG_SUB]
        q_i = q[lo:lo + HG_SUB]
        k_i = k[lo:lo + HG_SUB]
        a_i = jnp.zeros((HG_SUB, c), F32)
        for s in range(HG_SUB):
            b_s = bk_ref[0, h, lo + s:lo + s + 1, :]
            k_s = bk_ref[1, h, lo + s:lo + s + 1, :]
            e = jnp.exp2(jnp.minimum(b_i - b_s, 0.0)) * q_i * k_s
            a_i = jnp.where(col == lo + s, jnp.sum(e, axis=-1, keepdims=True), a_i)
        rows_a.append(a_i)
    amat = jnp.concatenate(rows_a, axis=0)

    size, idx = c // 2, 1
    while size >= HG_SUB:
        pieces = []
        for e0 in range(0, c, 2 * size):
            o0 = e0 + size
            r = bc[o0:o0 + 1]
            pieces.append(k[e0:o0] * jnp.exp(r - bc[e0:o0]))
            pieces.append(q[o0:o0 + size] * jnp.exp(bc[o0:o0 + size] - r))
        hmat = jnp.concatenate(pieces, axis=0).astype(BF16)
        amat = jnp.where(level == idx, _dot_nt(hmat, hmat), amat)
        size, idx = size // 2, idx + 1
    amat = jnp.where(level >= 0, amat, 0.0)

    st = st_ref[h]
    o = _dot_nt((q * jnp.exp(bc)).astype(BF16), st.astype(BF16)) + _dot(amat.astype(BF16), v.astype(BF16))
    bl = bc[c - 1:c]
    kdec = (k * jnp.exp(bl - bc)).astype(BF16)
    st_ref[h] = jnp.exp(bl) * st + _dot(v.T.astype(BF16), kdec)

    y = o * lax.rsqrt(jnp.mean(o * o, axis=-1, keepdims=True) + EPS) * hg_g
    return y * _silu(zg)


def _hgrn_parts(z_in, lbl_ref, g_ref, o_ref, st_ref, bk_ref, l_idx):
    lg = lbl_ref[...]
    ex = jnp.exp(lg - jnp.max(lg, axis=0, keepdims=True))
    sm = ex / jnp.sum(ex, axis=0, keepdims=True)
    lb_all = jnp.sum(sm[:l_idx + 1], axis=0, keepdims=True)

    c = HG_CHUNK
    ri = lax.broadcasted_iota(jnp.int32, (c, c), 0)
    ci = lax.broadcasted_iota(jnp.int32, (c, c), 1)
    tri = jnp.where(ci <= ri, 1.0, 0.0).astype(BF16)
    level = jnp.where(ci > ri, -1, 0)
    size, idx = c // 2, 1
    while size >= HG_SUB:
        sh = size.bit_length() - 1
        paired = ((ri >> (sh + 1)) == (ci >> (sh + 1))) & ((ri >> sh) != (ci >> sh)) & (ci <= ri)
        level = jnp.where(paired, idx, level)
        size, idx = size // 2, idx + 1

    def chunk_head(ch, h):
        def part():
            rows = slice(ch * c, (ch + 1) * c)
            sl = slice(h * HG_HD, (h + 1) * HG_HD)
            zq = z_in[rows, h * HG_HD:(h + 1) * HG_HD]
            zf = z_in[rows, HG_WIDTH + h * HG_HD:HG_WIDTH + (h + 1) * HG_HD]
            zi = z_in[rows, 2 * HG_WIDTH + h * HG_HD:2 * HG_WIDTH + (h + 1) * HG_HD]
            zg = z_in[rows, 3 * HG_WIDTH + h * HG_HD:3 * HG_WIDTH + (h + 1) * HG_HD]
            y = _hgrn_chunk(zq, zf, zi, zg, lb_all[:, sl], g_ref[...], st_ref, bk_ref, h, tri, level)
            o_ref[0, rows, h * HG_HD:(h + 1) * HG_HD] = y.astype(o_ref.dtype)
        return part

    return [chunk_head(ch, h) for ch in range(z_in.shape[0] // c) for h in range(HG_HEADS)]


def _mix_kernel(x_ref, mod_ref, g1_ref, w_ref, pos_ref, inv_ref, qg_ref, kg_ref, lbl_ref, hgg_ref,
                qt_ref, kc_ref, vc_ref, ks_ref, vst_ref, kw_ref, vwt_ref, gt_ref, ohg_ref,
                zbuf_ref, st_ref, bk_ref, seg_ref, *, l_idx, tiles_per_seq):
    j = pl.program_id(0)

    @pl.when(j == 0)
    def _():
        zbuf_ref[0] = jnp.zeros((zbuf_ref.shape[1], zbuf_ref.shape[2]), F32)

    @pl.when((j == 0) | (lax.rem(j + tiles_per_seq - 1, tiles_per_seq) == 0))
    def _():
        st_ref[...] = jnp.zeros_like(st_ref)

    zbuf_ref[1] = zbuf_ref[0]
    rec = _hgrn_parts(zbuf_ref.at[1], lbl_ref, hgg_ref, ohg_ref, st_ref, bk_ref, l_idx)
    proj = _project_parts(x_ref, mod_ref, g1_ref, w_ref, pos_ref, inv_ref, qg_ref, kg_ref, zbuf_ref.at[0],
                          qt_ref, kc_ref, vc_ref, ks_ref, vst_ref, kw_ref, vwt_ref, gt_ref, seg_ref)
    for part in rec + proj:
        part()


def _mix(x, mod, g1, w_in_p, pos_row, inv_t, qg_t, kg_t, lb_logits, hg_g, l_idx, tm):
    bsz, t, _ = x.shape
    nm = t // tm
    n_tiles = bsz * nm

    def cur(j):
        jc = jnp.minimum(j, n_tiles - 1)
        return jc // nm, jc % nm

    def prev(j):
        jp = jnp.maximum(j - 1, 0)
        return jp // nm, jp % nm

    kv_shape = (bsz, NSA_KV_HEADS, t, NSA_HD)
    kv_spec = pl.BlockSpec((1, NSA_KV_HEADS, tm, NSA_HD), lambda j: (cur(j)[0], 0, cur(j)[1], 0))
    seg_w = CMP_STRIDE * NSA_HD
    seg_shape = (bsz, NSA_KV_HEADS, t // CMP_STRIDE, seg_w)
    seg_spec = pl.BlockSpec((1, NSA_KV_HEADS, tm // CMP_STRIDE, seg_w), lambda j: (cur(j)[0], 0, cur(j)[1], 0))
    vt_shape = (bsz, NSA_KV_HEADS, VT_ROWS, t)
    vt_spec = pl.BlockSpec((1, NSA_KV_HEADS, VT_ROWS, tm), lambda j: (cur(j)[0], 0, 0, cur(j)[1]))
    const = lambda j: (0, 0)
    return pl.pallas_call(
        functools.partial(_mix_kernel, l_idx=l_idx, tiles_per_seq=nm),
        grid=(n_tiles + 1,),
        in_specs=[pl.BlockSpec((1, tm, D_MODEL), lambda j: (cur(j)[0], cur(j)[1], 0)),
                  pl.BlockSpec((1, 6, D_MODEL), lambda j: (cur(j)[0], 0, 0)),
                  pl.BlockSpec((1, D_MODEL), const),
                  pl.BlockSpec((D_MODEL, IN_COLS), const),
                  pl.BlockSpec((1, 1, tm), lambda j: (cur(j)[0], 0, cur(j)[1])),
                  pl.BlockSpec(inv_t.shape, const),
                  pl.BlockSpec(qg_t.shape, const),
                  pl.BlockSpec(kg_t.shape, lambda j: (0, 0, 0)),
                  pl.BlockSpec(lb_logits.shape, const),
                  pl.BlockSpec((1, HG_HD), const)],
        out_specs=[pl.BlockSpec((1, NSA_HEADS, NSA_HD, tm), lambda j: (cur(j)[0], 0, 0, cur(j)[1])),
                   seg_spec, seg_spec, kv_spec, vt_spec, kv_spec, vt_spec,
                   pl.BlockSpec((1, GATE_ROWS, tm), lambda j: (cur(j)[0], 0, cur(j)[1])),
                   pl.BlockSpec((1, tm, HG_WIDTH), lambda j: (prev(j)[0], prev(j)[1], 0))],
        out_shape=[jax.ShapeDtypeStruct((bsz, NSA_HEADS, NSA_HD, t), BF16),
                   jax.ShapeDtypeStruct(seg_shape, F32),
                   jax.ShapeDtypeStruct(seg_shape, F32),
                   jax.ShapeDtypeStruct(kv_shape, BF16),
                   jax.ShapeDtypeStruct(vt_shape, BF16),
                   jax.ShapeDtypeStruct(kv_shape, BF16),
                   jax.ShapeDtypeStruct(vt_shape, BF16),
                   jax.ShapeDtypeStruct((bsz, GATE_ROWS, t), F32),
                   jax.ShapeDtypeStruct((bsz, t, HG_WIDTH), BF16)],
        scratch_shapes=[pltpu.VMEM((2, tm, 4 * HG_WIDTH), F32),
                        pltpu.VMEM((HG_HEADS, HG_HD, HG_HD), F32),
                        pltpu.VMEM((2, HG_HEADS, HG_CHUNK, HG_HD), F32),
                        pltpu.VMEM((tm, LANES), F32)],
        compiler_params=pltpu.CompilerParams(
            dimension_semantics=("arbitrary",), vmem_limit_bytes=VMEM_LIMIT),
        name="mix",
    )(x, mod, g1, w_in_p, pos_row, inv_t, qg_t, kg_t, lb_logits, hg_g)


def _compress_kernel(xk_ref, xv_ref, pe_ref, w1_ref, w2_ref, ko_ref, vo_ref):
    half = CMP_STRIDE * NSA_HD
    n_g, nseg = xk_ref.shape[1], xk_ref.shape[2]
    outs = []
    for kv, x_ref in enumerate((xk_ref, xv_ref)):
        x = x_ref[0].reshape(n_g * nseg, half)
        ha = _dot((x + pe_ref[kv, 0:1, :]).astype(BF16), w1_ref[kv, :half, :])
        hb = _dot((x + pe_ref[kv, 1:2, :]).astype(BF16), w1_ref[kv, half:, :])
        pre = ha + pltpu.roll(hb, n_g * nseg - 1, axis=0)
        outs.append(_dot(_silu(pre).astype(BF16), w2_ref[kv]))
    for g in range(n_g):
        rows = slice(g * nseg, (g + 1) * nseg)
        ko_ref[0, g] = outs[0][rows].astype(ko_ref.dtype)
        vo_ref[0, g] = outs[1][rows].T.astype(vo_ref.dtype)


def _compress(xk, xv, pe2, w1, w2):
    bsz, g, nseg, width = xk.shape
    x_spec = pl.BlockSpec((1, g, nseg, width), lambda b: (b, 0, 0, 0))
    return pl.pallas_call(
        _compress_kernel,
        grid=(bsz,),
        in_specs=[x_spec, x_spec,
                  pl.BlockSpec(pe2.shape, lambda b: (0, 0, 0)),
                  pl.BlockSpec(w1.shape, lambda b: (0, 0, 0)),
                  pl.BlockSpec(w2.shape, lambda b: (0, 0, 0))],
        out_specs=[pl.BlockSpec((1, g, nseg, NSA_HD), lambda b: (b, 0, 0, 0)),
                   pl.BlockSpec((1, g, NSA_HD, nseg), lambda b: (b, 0, 0, 0))],
        out_shape=[jax.ShapeDtypeStruct((bsz, g, nseg, NSA_HD), BF16),
                   jax.ShapeDtypeStruct((bsz, g, NSA_HD, nseg), BF16)],
        compiler_params=pltpu.CompilerParams(
            dimension_semantics=("arbitrary",), vmem_limit_bytes=VMEM_LIMIT),
        name="compress",
    )(xk, xv, pe2, w1, w2)


def _nsa_kernel(bound_ref, qt_ref, kc_ref, vct_ref, ks_ref, vst_ref, kw_ref, vwt_ref, gt_ref, mt_ref,
                o_ref, sel_ref, *, tq, tks, bounded):
    r = NSA_GROUP
    n_g = NSA_KV_HEADS
    ch = NSA_CHAIN
    lanes = ch * tq
    q0 = pl.program_id(1) * tq
    chains = [(g, g * r + c * ch) for g in range(n_g) for c in range(r // ch)]
    q_ts = [jnp.concatenate([qt_ref[0, h0 + i] for i in range(ch)], axis=1)
            for _, h0 in chains]

    def tile_heads(a):
        return jnp.concatenate([a] * ch, axis=1)

    n_blk = kc_ref.shape[2]
    nb = mt_ref.shape[0]
    blk_end = lax.broadcasted_iota(jnp.int32, (n_blk, tq), 0) * CMP_STRIDE + (CMP_BLOCK - 1)
    t_row = q0 + lax.broadcasted_iota(jnp.int32, (1, tq), 1)
    cvalid = tile_heads(jnp.where(blk_end <= t_row, 1.0, 0.0)) > 0.5
    some = tile_heads((t_row >= CMP_BLOCK - 1).astype(F32))
    j = lax.broadcasted_iota(jnp.int32, (nb, tq), 0)
    cur = jnp.right_shift(t_row, SLC_BLOCK.bit_length() - 1)
    forced = (j == 0) | (j == cur) | (j == cur - 1)
    o_cmp = []
    psum = [None] * n_g
    for (g, _), q_t in zip(chains, q_ts):
        s = jnp.where(cvalid, _dot(kc_ref[0, g], q_t), NEG)
        e = jnp.exp2(s - jnp.max(s, axis=0, keepdims=True))
        p = e * (some / jnp.sum(e, axis=0, keepdims=True))
        o_cmp.append(_dot(vct_ref[0, g], p.astype(BF16)))
        for i in range(ch):
            part = p[:, i * tq:(i + 1) * tq]
            psum[g] = part if psum[g] is None else psum[g] + part
    for g in range(n_g):
        p_hi = psum[g].astype(BF16)
        p_lo = (psum[g] - p_hi.astype(F32)).astype(BF16)
        imp = _dot(mt_ref[...], p_hi) + _dot(mt_ref[...], p_lo)
        imp = jnp.where(j <= cur, jnp.where(forced, jnp.inf, imp), -1.0)
        ranks = [jnp.zeros((nb, tq), jnp.int32) for _ in range(RANK_LANES)]
        for i in range(nb):
            row_i = imp[i:i + 1, :]
            ahead = (row_i > imp) | ((row_i == imp) & (j > i))
            ranks[i % RANK_LANES] = ranks[i % RANK_LANES] + ahead.astype(jnp.int32)
        rank = functools.reduce(lambda a, b: a + b, ranks)
        sel_ref[g] = jnp.where(rank < min(SLC_TOPK, nb), 0.0, NEG)

    shift = bound_ref[0] * LOG2E if bounded else None

    span = WIN + tq
    start = pl.multiple_of(jnp.maximum(q0 - WIN, 0), tq)
    dist = (lax.broadcasted_iota(jnp.int32, (span, tq), 1)
            - lax.broadcasted_iota(jnp.int32, (span, tq), 0)) + (q0 - start)
    wbias = jnp.where((dist >= 0) & (dist < WIN), 0.0, NEG)
    wbias = tile_heads(wbias - shift if bounded else wbias)
    o_win = []
    for (g, _), q_t in zip(chains, q_ts):
        sw = _dot(kw_ref[0, g, pl.ds(start, span), :], q_t) + wbias
        ew = jnp.exp2(sw if bounded else sw - jnp.max(sw, axis=0, keepdims=True))
        ow = _dot(vwt_ref[0, g, :, pl.ds(start, span)], ew.astype(BF16))
        o_win.append(ow[:NSA_HD] * (1.0 / ow[NSA_HD:NSA_HD + 1]))

    blocks_per_tile = tks // SLC_BLOCK
    rel = (lax.broadcasted_iota(jnp.int32, (tks, tq), 1)
           - lax.broadcasted_iota(jnp.int32, (tks, tq), 0))

    def slc_bias(it, k0, need_causal):
        biases = []
        for g in range(n_g):
            if isinstance(it, int):
                rows = sel_ref[g, it * blocks_per_tile:(it + 1) * blocks_per_tile, :]
            else:
                rows = sel_ref[g, pl.ds(pl.multiple_of(it * blocks_per_tile, blocks_per_tile), blocks_per_tile), :]
            bias = jnp.concatenate([jnp.broadcast_to(rows[jj:jj + 1], (SLC_BLOCK, tq))
                                    for jj in range(blocks_per_tile)], axis=0)
            if need_causal:
                bias = jnp.where(rel + (q0 - k0) >= 0, bias, NEG)
            biases.append(tile_heads(bias - shift if bounded else bias))
        return biases

    def slc_body(it, carry, need_causal=True):
        k0 = it * tks if isinstance(it, int) else pl.multiple_of(it * tks, tks)
        biases = slc_bias(it, k0, need_causal)
        out = []
        for (g, _), q_t, state in zip(chains, q_ts, carry):
            sc = _dot(ks_ref[0, g, pl.ds(k0, tks), :], q_t) + biases[g]
            vt = vst_ref[0, g, :, pl.ds(k0, tks)]
            if bounded:
                (acc,) = state
                acc = acc + _dot(vt, jnp.exp2(sc).astype(BF16))
                out.append((acc,))
            else:
                m, acc = state
                m_new = jnp.maximum(m, jnp.max(sc, axis=0, keepdims=True))
                acc = jnp.exp2(m - m_new) * acc + _dot(vt, jnp.exp2(sc - m_new).astype(BF16))
                out.append((m_new, acc))
        return tuple(out)

    zero_state = (jnp.zeros((VT_ROWS, lanes), F32),)
    init = tuple(zero_state if bounded else (jnp.full((1, lanes), NEG, F32),) + zero_state for _ in chains)
    n_it = (q0 + tq + tks - 1) // tks
    if bounded:
        def slc_tiles(n):
            carry = init
            for it in range(n):
                carry = slc_body(it, carry, need_causal=(it == n - 1))
            return carry

        max_it = ks_ref.shape[2] // tks
        fin = lax.switch(n_it - 1, [functools.partial(slc_tiles, n) for n in range(1, max_it + 1)])
    else:
        fin = lax.fori_loop(0, n_it, slc_body, init)
    o_slc = [st[-1][:NSA_HD] * (1.0 / st[-1][NSA_HD:NSA_HD + 1]) for st in fin]

    for ci, (_, h0) in enumerate(chains):
        for i in range(ch):
            cols = slice(i * tq, (i + 1) * tq)
            head = h0 + i
            o_h = jnp.zeros((NSA_HD, tq), F32)
            for br, o_b in enumerate((o_cmp[ci], o_slc[ci], o_win[ci])):
                row = head * N_BRANCH + br
                o_h = o_h + gt_ref[0, row:row + 1, :] * o_b[:, cols]
            o_ref[0, :, head * NSA_HD:(head + 1) * NSA_HD] = o_h.T.astype(o_ref.dtype)


def _nsa(q_t, kc, vct, ks, vst, kw, vwt, gates_t, mt, bound, tq, tks, bounded):
    bsz, _, _, t = q_t.shape
    n_blk = kc.shape[2]
    n_g = NSA_KV_HEADS
    full = lambda b, i: (b, 0, 0, 0)
    k_spec = pl.BlockSpec((1, n_g, t, NSA_HD), full)
    vt_spec = pl.BlockSpec((1, n_g, VT_ROWS, t), full)
    return pl.pallas_call(
        functools.partial(_nsa_kernel, tq=tq, tks=tks, bounded=bounded),
        grid=(bsz, t // tq),
        in_specs=[pl.BlockSpec(memory_space=pltpu.SMEM),
                  pl.BlockSpec((1, NSA_HEADS, NSA_HD, tq), lambda b, i: (b, 0, 0, i)),
                  pl.BlockSpec((1, n_g, n_blk, NSA_HD), full),
                  pl.BlockSpec((1, n_g, NSA_HD, n_blk), full),
                  k_spec, vt_spec, k_spec, vt_spec,
                  pl.BlockSpec((1, GATE_ROWS, tq), lambda b, i: (b, 0, i)),
                  pl.BlockSpec(mt.shape, lambda b, i: (0, 0))],
        out_specs=pl.BlockSpec((1, tq, NSA_WIDTH), lambda b, i: (b, i, 0)),
        out_shape=jax.ShapeDtypeStruct((bsz, t, NSA_WIDTH), BF16),
        scratch_shapes=[pltpu.VMEM((n_g, mt.shape[0], tq), F32)],
        compiler_params=pltpu.CompilerParams(
            dimension_semantics=("arbitrary", "arbitrary"), vmem_limit_bytes=VMEM_LIMIT),
        name="nsa_bounded" if bounded else "nsa",
    )(bound, q_t, kc, vct, ks, vst, kw, vwt, gates_t, mt)


def _causal_conv(u, prev, cw, cb):
    tm = u.shape[0]
    g = SUBLANES
    r8 = lax.broadcasted_iota(jnp.int32, (g, u.shape[1]), 0)
    wrap1 = jnp.where(r8 == 0, prev[2 * g - 1:2 * g], pltpu.roll(u[tm - g:], 1, axis=0))
    wrap2 = jnp.where(r8 == 0, prev[g - 1:g], pltpu.roll(u[tm - 2 * g:tm - g], 1, axis=0))
    u1 = jnp.concatenate([wrap1, u[:tm - g]], axis=0)
    u2 = jnp.concatenate([wrap2, wrap1, u[:tm - 2 * g]], axis=0)
    return cb + u2 * cw[0:1] + u1 * cw[1:2] + u * cw[2:3]


def _ffn_kernel(x_ref, hg_ref, ns_ref, mod_ref, wo_ref, g2_ref, wu_ref, cw_ref, cb_ref, wd_ref,
                o_ref, carry_ref, buf_ref, x1_ref, h2_ref, u_ref, g_ref, acc_ref, *, fc):
    n_sub, sub = x1_ref.shape[0], x1_ref.shape[1]
    n_lt = D_MODEL // LANES
    groups = sub // SUBLANES
    pitch = buf_ref.shape[2] // SUBLANES
    first = pl.program_id(1) == 0

    def head(s):
        rows = slice(s * sub, (s + 1) * sub)
        mix = _dot(hg_ref[0, rows], wo_ref[:HG_WIDTH, :]) + _dot(ns_ref[0, rows], wo_ref[HG_WIDTH:, :])
        x1_nat = x_ref[0, rows] + mod_ref[0, 2:3, :] * mix
        for c in range(n_lt):
            for sg in range(SUBLANES):
                buf_ref[s, c, sg * pitch:sg * pitch + groups] = x1_nat[sg * groups:(sg + 1) * groups,
                                                                       c * LANES:(c + 1) * LANES]
        x1 = jnp.concatenate(
            [jnp.concatenate([buf_ref[s, c, pl.ds(j, SUBLANES, stride=pitch), :] for j in range(groups)], axis=0)
             for c in range(n_lt)], axis=1)
        x1_ref[s] = x1
        y = x1 * lax.rsqrt(jnp.mean(x1 * x1, axis=-1, keepdims=True) + EPS) * g2_ref[...]
        h2_ref[s] = (y * (1.0 + mod_ref[0, 4:5, :]) + mod_ref[0, 3:4, :]).astype(h2_ref.dtype)
        acc_ref[s] = jnp.zeros((sub, D_MODEL), F32)

    def tail(s):
        out = x1_ref[s] + mod_ref[0, 5:6, :] * acc_ref[s]
        for c in range(n_lt):
            for j in range(groups):
                buf_ref[s, c, pl.ds(j, SUBLANES, stride=pitch), :] = out[j * SUBLANES:(j + 1) * SUBLANES,
                                                                         c * LANES:(c + 1) * LANES]
        for c in range(n_lt):
            for sg in range(SUBLANES):
                o_ref[0, s * sub + sg * groups:s * sub + (sg + 1) * groups, c * LANES:(c + 1) * LANES] = (
                    buf_ref[s, c, sg * pitch:sg * pitch + groups])

    n = D_FF // fc

    def up(c, slot):
        s, jc = divmod(c, n)
        for half in range(2):
            off = half * D_FF + jc * fc
            u_ref[slot, half] = _dot(h2_ref[s], wu_ref[:, off:off + fc])

    def act(c, slot):
        s, jc = divmod(c, n)
        halves = []
        for half in range(2):
            cols = slice(half * D_FF + jc * fc, half * D_FF + (jc + 1) * fc)
            u = u_ref[slot, half]
            prev = jnp.where(first, 0.0, carry_ref[:, cols]) if s == 0 else carry_ref[:, cols]
            carry_ref[:, cols] = u[sub - 2 * SUBLANES:]
            halves.append(_causal_conv(u, prev, cw_ref[:, cols], cb_ref[:, cols]))
        a, v = halves
        g_ref[slot] = (_silu(a) * v).astype(g_ref.dtype)

    def down(c, slot):
        s, jc = divmod(c, n)
        acc_ref[s] += _dot(g_ref[slot], wd_ref[jc * fc:(jc + 1) * fc, :])

    total = n_sub * n
    head(0)
    for i in range(total + 2):
        if i < total:
            up(i, i % 2)
        if 0 <= i - 1 < total:
            act(i - 1, (i - 1) % 2)
        if 0 <= i - 2 < total:
            down(i - 2, i % 2)
        s, jc = divmod(i, n)
        if jc == HEAD_AT and s + 1 < n_sub:
            head(s + 1)
        if jc == 1 and 1 <= s <= n_sub:
            tail(s - 1)


def _ffn(x, o_hg, o_nsa, mod, w_out, g2, w_up, conv_w, conv_b, w_down, tm, sub, fc):
    bsz, t, _ = x.shape
    row_spec = lambda w: pl.BlockSpec((1, tm, w), lambda b, i: (b, i, 0))
    groups = sub // SUBLANES
    pad = SUBLANES if (groups // SUBLANES) % 2 == 0 else 0
    n_sub = tm // sub
    resident = lambda a: pl.BlockSpec(a.shape, lambda b, i: (0, 0), pipeline_mode=pl.Buffered(1))
    return pl.pallas_call(
        functools.partial(_ffn_kernel, fc=fc),
        grid=(bsz, t // tm),
        in_specs=[row_spec(D_MODEL), row_spec(HG_WIDTH), row_spec(NSA_WIDTH),
                  pl.BlockSpec((1, 6, D_MODEL), lambda b, i: (b, 0, 0)),
                  resident(w_out), resident(g2), resident(w_up), resident(conv_w), resident(conv_b),
                  resident(w_down)],
        out_specs=row_spec(D_MODEL),
        out_shape=jax.ShapeDtypeStruct(x.shape, F32),
        scratch_shapes=[pltpu.VMEM((2 * SUBLANES, 2 * D_FF), F32),
                        pltpu.VMEM((n_sub, D_MODEL // LANES, sub + SUBLANES * pad, LANES), F32),
                        pltpu.VMEM((n_sub, sub, D_MODEL), F32),
                        pltpu.VMEM((n_sub, sub, D_MODEL), BF16),
                        pltpu.VMEM((2, 2, sub, fc), F32),
                        pltpu.VMEM((2, sub, fc), BF16),
                        pltpu.VMEM((n_sub, sub, D_MODEL), F32)],
        compiler_params=pltpu.CompilerParams(
            dimension_semantics=("arbitrary", "arbitrary"), vmem_limit_bytes=VMEM_LIMIT),
        name="ffn",
    )(x, o_hg, o_nsa, mod, w_out, g2, w_up, conv_w, conv_b, w_down)


def _rope_tables():
    half = ROPE_DIM // 2
    inv = ROPE_THETA ** (-jnp.arange(half, dtype=F32) * 2.0 / ROPE_DIM)
    return (jnp.tile(inv.reshape(half, 1), (1, LANES)),)


def _gain_t(g):
    return jnp.tile(g.reshape(NSA_HD, 1), (LANES // NSA_HD, LANES))


def _selection_tables(t):
    n_seg = t // CMP_STRIDE
    nb = t // SLC_BLOCK
    cst = np.arange(n_seg) * CMP_STRIDE
    sst = np.arange(nb) * SLC_BLOCK
    ovl = np.clip(np.minimum(cst[:, None] + CMP_BLOCK, sst[None] + SLC_BLOCK)
                  - np.maximum(cst[:, None], sst[None]), 0, None) / CMP_BLOCK
    ovl[(t - CMP_BLOCK) // CMP_STRIDE + 1:] = 0.0
    return (jnp.asarray(ovl.T, dtype=BF16),)


def _layer(x, mod, pos_row, l, p, tables):
    bsz, t, _ = x.shape
    inv_t, mt = tables
    w_in_p = p["w_in"][l].astype(BF16)
    qg_t = _gain_t(p["q_norm_g"][l])
    kg_t = jnp.stack([_gain_t(p["k_norm_g"][l, br]) for br in range(N_BRANCH)])
    q_t, kc, vc, ks, vst, kw, vwt, gates_t, o_hg = _mix(
        x, mod, p["norm1_g"][l].reshape(1, D_MODEL), w_in_p, pos_row, inv_t, qg_t, kg_t,
        p["lb_logits"], p["hg_norm_g"][l].reshape(1, HG_HD), l, tm=TM_MIX)

    pe2 = p["pe_cmp"][l].reshape(2, 2, CMP_STRIDE * NSA_HD)
    kcmp, vcmp_t = _compress(kc, vc, pe2, p["w_cmp1"][l].astype(BF16), p["w_cmp2"][l].astype(BF16))
    bound = (SCORE_BOUND_MARGIN * NSA_HD ** 0.5 * jnp.max(jnp.abs(p["q_norm_g"][l]))
             * jnp.max(jnp.abs(p["k_norm_g"][l, 1:]))).reshape(1).astype(F32)
    nsa_args = (q_t, kcmp, vcmp_t, ks, vst, kw, vwt, gates_t, mt, bound)
    o_nsa = lax.cond(bound[0] <= MAX_SCORE_BOUND,
                     lambda a: _nsa(*a, tq=TQ_NSA, tks=TKS_NSA, bounded=True),
                     lambda a: _nsa(*a, tq=TQ_NSA, tks=TKS_NSA, bounded=False), nsa_args)

    return _ffn(x, o_hg, o_nsa, mod, p["w_out"][l].astype(BF16), p["norm2_g"][l].reshape(1, D_MODEL),
                p["w_up"][l].astype(BF16), p["conv_w"][l], p["conv_b"][l].reshape(1, 2 * D_FF),
                p["w_down"][l].astype(BF16), tm=TM_FFN, sub=SUB_FFN, fc=FC_FFN)


def kernel(x, c, positions, w_ada, b_ada, norm1_g, w_in, lb_logits, hg_norm_g, q_norm_g, k_norm_g, pe_cmp, w_cmp1, w_cmp2, w_out, norm2_g, w_up, conv_w, conv_b, w_down):
    p = dict(w_in=w_in, norm1_g=norm1_g, lb_logits=lb_logits, hg_norm_g=hg_norm_g, q_norm_g=q_norm_g,
             k_norm_g=k_norm_g, pe_cmp=pe_cmp, w_cmp1=w_cmp1, w_cmp2=w_cmp2, w_out=w_out,
             norm2_g=norm2_g, w_up=w_up, conv_w=conv_w, conv_b=conv_b, w_down=w_down)
    bsz, t, _ = x.shape
    assert x.shape[2] == D_MODEL and t % TM_MIX == 0 and t % TKS_NSA == 0 and t >= WIN + TQ_NSA
    assert TKS_NSA % TQ_NSA == 0 and TM_MIX % HG_CHUNK == 0 and D_FF % FC_FFN == 0 and TM_FFN % SUB_FFN == 0
    tables = _rope_tables() + _selection_tables(t)
    pos_row = positions.reshape(bsz, 1, t)
    for l in range(w_ada.shape[0]):
        mod = _ada(c, w_ada[l], b_ada[l]).reshape(bsz, 6, D_MODEL)
        x = _layer(x, mod, pos_row, l, p, tables)
    return x
```

```python
import functools

import jax
import jax.numpy as jnp
import numpy as np
from jax import lax
from jax.experimental import pallas as pl
from jax.experimental.pallas import tpu as pltpu

D_MODEL = 1024
HG_HEADS = 4
HG_HD = 128
HG_WIDTH = HG_HEADS * HG_HD
HG_CHUNK = 128
HG_SUB = 8
LOG2E = 1.4426950408889634
NSA_HEADS = 8
NSA_KV_HEADS = 2
NSA_HD = 64
NSA_GROUP = NSA_HEADS // NSA_KV_HEADS
NSA_CHAIN = 4
RANK_LANES = 4
NSA_WIDTH = NSA_HEADS * NSA_HD
N_BRANCH = 3
CMP_BLOCK = 32
CMP_STRIDE = 16
CMP_HIDDEN = 256
SLC_BLOCK = 64
SLC_TOPK = 16
WIN = 512
ROPE_DIM = NSA_HD // 4
ROPE_THETA = 500000.0
D_FF = 2816
CONV_W = 3
EPS = 1e-6
NEG = -1e30
SCORE_BOUND_MARGIN = 1.02
MAX_SCORE_BOUND = 40.0

LANES = 128
SUBLANES = 8
VMEM_LIMIT = 56 * 1024 * 1024

TM_MIX = 512
TQ_NSA = 256
TKS_NSA = 512
TM_FFN = 512
SUB_FFN = 256
HEAD_AT = 7
FC_FFN = 256

OFF_HG = 0
OFF_Q = 4 * HG_WIDTH
OFF_KV = OFF_Q + NSA_WIDTH
OFF_G = OFF_KV + 6 * NSA_KV_HEADS * NSA_HD
IN_COLS = OFF_G + N_BRANCH * NSA_HEADS
GATE_ROWS = N_BRANCH * NSA_HEADS
VT_ROWS = NSA_HD + SUBLANES

BF16 = jnp.bfloat16
F32 = jnp.float32


def _dot(a, b):
    return jnp.dot(a, b, preferred_element_type=F32)


def _dot_nt(a, b):
    return lax.dot_general(a, b, (((1,), (1,)), ((), ())), preferred_element_type=F32)


def _exp_neg(x):
    return jnp.exp2(x * (-LOG2E))


def _sigmoid(x):
    return 1.0 / (1.0 + _exp_neg(x))


def _silu(x):
    return x * _sigmoid(x)


def _ada_kernel(c_ref, w_ref, b_ref, o_ref):
    cs = _silu(c_ref[...])
    o_ref[...] = _dot(cs.astype(BF16), w_ref[...].astype(BF16)) + b_ref[...]


def _ada(c, w, b):
    bsz = c.shape[0]
    n = w.shape[1]
    tn = D_MODEL
    return pl.pallas_call(
        _ada_kernel,
        grid=(n // tn,),
        in_specs=[pl.BlockSpec((bsz, D_MODEL), lambda j: (0, 0)),
                  pl.BlockSpec((D_MODEL, tn), lambda j: (0, j)),
                  pl.BlockSpec((1, tn), lambda j: (0, j))],
        out_specs=pl.BlockSpec((bsz, tn), lambda j: (0, j)),
        out_shape=jax.ShapeDtypeStruct((bsz, n), F32),
        name="ada",
    )(c, w, b.reshape(1, n))


def _pair_norm_rope_t(xt, g_t, cos_t, sin_t):
    half = ROPE_DIM // 2
    outs = []
    for hh in range(2):
        x = xt[hh * NSA_HD:(hh + 1) * NSA_HD]
        ms = jnp.mean(x * x, axis=0, keepdims=True)
        xn = x * lax.rsqrt(ms + EPS) * g_t[hh * NSA_HD:(hh + 1) * NSA_HD]
        x1, x2 = xn[:half], xn[half:ROPE_DIM]
        outs += [x1 * cos_t - x2 * sin_t, x2 * cos_t + x1 * sin_t, xn[ROPE_DIM:]]
    return jnp.concatenate(outs, axis=0)


def _project_parts(x_ref, mod_ref, g1_ref, w_ref, pos_ref, inv_ref, qg_ref, kg_ref,
                   zhg_out, qt_ref, kc_ref, vc_ref, ks_ref, vst_ref, kw_ref, vwt_ref, gt_ref, seg_ref):
    x = x_ref[0]
    ms = jnp.mean(x * x, axis=-1, keepdims=True)
    y = x * lax.rsqrt(ms + EPS) * g1_ref[...]
    h = (y * (1.0 + mod_ref[0, 1:2, :]) + mod_ref[0, 0:1, :]).astype(BF16)

    tm = x.shape[0]
    reps = tm // LANES

    def lane_tile(a):
        return jnp.concatenate([a] * reps, axis=1)

    ang = lane_tile(inv_ref[...]) * pos_ref[0].astype(F32)
    cos_t = jnp.cos(ang)
    sin_t = jnp.sin(ang)
    scale = NSA_HD ** -0.5 * LOG2E
    vals = {}

    def hg_cols(lo, hi):
        def part():
            zhg_out[:, lo:hi] = _dot(h, w_ref[:, OFF_HG + lo:OFF_HG + hi])
        return part

    def q_pair(p):
        def part():
            if p % 2 == 0:
                vals["zq"] = _dot(h, w_ref[:, OFF_Q + p * LANES:OFF_Q + (p + 2) * LANES])
                vals["qg"] = lane_tile(qg_ref[...])
            off = (p % 2) * LANES
            rt = _pair_norm_rope_t(vals["zq"][:, off:off + LANES].T, vals["qg"], cos_t, sin_t)
            rt = (rt * scale).astype(qt_ref.dtype)
            qt_ref[0, 2 * p] = rt[:NSA_HD]
            qt_ref[0, 2 * p + 1] = rt[NSA_HD:]
        return part

    def kv_dot(br):
        vals["zkv"] = _dot(h, w_ref[:, OFF_KV + 2 * br * LANES:OFF_KV + (2 * br + 2) * LANES])

    def store_segments(a, out_ref):
        seg_ref[...] = a
        n_seg = tm // CMP_STRIDE
        lane = lax.broadcasted_iota(jnp.int32, (n_seg, LANES), 1)
        for l in range(0, CMP_STRIDE, 2):
            even = seg_ref[pl.ds(l, n_seg, stride=CMP_STRIDE), :]
            odd = seg_ref[pl.ds(l + 1, n_seg, stride=CMP_STRIDE), :]
            cols = slice((l // 2) * LANES, (l // 2 + 1) * LANES)
            out_ref[0, 0, :, cols] = jnp.where(lane < NSA_HD, even, pltpu.roll(odd, NSA_HD, axis=1))
            out_ref[0, 1, :, cols] = jnp.where(lane < NSA_HD, pltpu.roll(even, NSA_HD, axis=1), odd)

    def key(br, k_ref):
        def part():
            kv_dot(br)
            kt = _pair_norm_rope_t(vals["zkv"][:, :LANES].T, lane_tile(kg_ref[br]), cos_t, sin_t)
            if br == 0:
                store_segments(kt.T, k_ref)
                store_segments(vals["zkv"][:, LANES:], vc_ref)
            else:
                kk = kt.T.astype(k_ref.dtype)
                for g in range(NSA_KV_HEADS):
                    k_ref[0, g] = kk[:, g * NSA_HD:(g + 1) * NSA_HD]
        return part

    def value_t(vt_ref):
        def part():
            vt = vals["zkv"][:, LANES:].T.astype(vt_ref.dtype)
            ones_row = jnp.where(lax.broadcasted_iota(jnp.int32, (SUBLANES, tm), 0) == 0, 1.0, 0.0).astype(vt_ref.dtype)
            for g in range(NSA_KV_HEADS):
                vt_ref[0, g, :NSA_HD] = vt[g * NSA_HD:(g + 1) * NSA_HD]
                vt_ref[0, g, NSA_HD:] = ones_row
        return part

    def gates_part():
        gates = _sigmoid(_dot(h, w_ref[:, OFF_G:IN_COLS]))
        wide = jnp.concatenate([gates, jnp.zeros((tm, LANES - GATE_ROWS), F32)], axis=1)
        gt_ref[0] = wide.T[:GATE_ROWS]

    wide = 2 * LANES
    hg = [hg_cols(lo, lo + wide) for lo in range(0, 4 * HG_WIDTH, wide)]
    rest = [q_pair(0), q_pair(1), q_pair(2), q_pair(3), key(0, kc_ref), key(1, ks_ref), value_t(vst_ref),
            key(2, kw_ref), value_t(vwt_ref), gates_part]
    parts = []
    for k in range(max(len(hg), len(rest))):
        parts += hg[k:k + 1] + rest[k:k + 1]
    return parts


def _hgrn_chunk(zq, zf, zi, zg, lb, hg_g, st_ref, bk_ref, h, tri, level):
    c = HG_CHUNK
    e_z = _exp_neg(jnp.abs(zf))
    logsig = jnp.minimum(zf, 0.0) - jnp.log(1.0 + e_z)
    a = jnp.log(lb)
    bb = jnp.log1p(-lb) + logsig
    logf = jnp.maximum(a, bb) + jnp.log(1.0 + _exp_neg(jnp.abs(a - bb)))
    k = (1.0 - lb) * (jnp.where(zf >= 0.0, e_z, 1.0) / (1.0 + e_z))
    q = _silu(zq)
    v = zi
    l_hi = logf.astype(BF16)
    l_mid = (logf - l_hi.astype(F32)).astype(BF16)
    l_lo = (logf - l_hi.astype(F32) - l_mid.astype(F32)).astype(BF16)
    bc = _dot(tri, l_hi) + _dot(tri, l_mid) + _dot(tri, l_lo)

    col = lax.broadcasted_iota(jnp.int32, (HG_SUB, c), 1)
    b2 = bc * LOG2E
    bk_ref[0, h] = b2
    bk_ref[1, h] = k
    rows_a = []
    for i in range(c // HG_SUB):
        lo = i * HG_SUB
        b_i = b2[lo:lo + HG_SUB]
        q_i = q[lo:lo + HG_SUB]
        k_i = k[lo:lo + HG_SUB]
        a_i = jnp.zeros((HG_SUB, c), F32)
        for s in range(HG_SUB):
            b_s = bk_ref[0, h, lo + s:lo + s + 1, :]
            k_s = bk_ref[1, h, lo + s:lo + s + 1, :]
            e = jnp.exp2(jnp.minimum(b_i - b_s, 0.0)) * q_i * k_s
            a_i = jnp.where(col == lo + s, jnp.sum(e, axis=-1, keepdims=True), a_i)
        rows_a.append(a_i)
    amat = jnp.concatenate(rows_a, axis=0)

    size, idx = c // 2, 1
    while size >= HG_SUB:
        pieces = []
        for e0 in range(0, c, 2 * size):
            o0 = e0 + size
            r = b2[o0:o0 + 1]
            pieces.append(k[e0:o0] * jnp.exp2(r - b2[e0:o0]))
            pieces.append(q[o0:o0 + size] * jnp.exp2(b2[o0:o0 + size] - r))
        hmat = jnp.concatenate(pieces, axis=0).astype(BF16)
        amat = jnp.where(level == idx, _dot_nt(hmat, hmat), amat)
        size, idx = size // 2, idx + 1
    amat = jnp.where(level >= 0, amat, 0.0)

    st = st_ref[h]
    o = _dot_nt((q * jnp.exp2(b2)).astype(BF16), st.astype(BF16)) + _dot(amat.astype(BF16), v.astype(BF16))
    bl = b2[c - 1:c]
    kdec = (k * jnp.exp2(bl - b2)).astype(BF16)
    st_ref[h] = jnp.exp2(bl) * st + _dot(v.T.astype(BF16), kdec)

    y = o * lax.rsqrt(jnp.mean(o * o, axis=-1, keepdims=True) + EPS) * hg_g
    return y * _silu(zg)


def _hgrn_parts(z_in, lbl_ref, g_ref, o_ref, st_ref, bk_ref, l_idx):
    lg = lbl_ref[...]
    ex = jnp.exp(lg - jnp.max(lg, axis=0, keepdims=True))
    sm = ex / jnp.sum(ex, axis=0, keepdims=True)
    lb_all = jnp.sum(sm[:l_idx + 1], axis=0, keepdims=True)

    c = HG_CHUNK
    ri = lax.broadcasted_iota(jnp.int32, (c, c), 0)
    ci = lax.broadcasted_iota(jnp.int32, (c, c), 1)
    tri = jnp.where(ci <= ri, 1.0, 0.0).astype(BF16)
    level = jnp.where(ci > ri, -1, 0)
    size, idx = c // 2, 1
    while size >= HG_SUB:
        sh = size.bit_length() - 1
        paired = ((ri >> (sh + 1)) == (ci >> (sh + 1))) & ((ri >> sh) != (ci >> sh)) & (ci <= ri)
        level = jnp.where(paired, idx, level)
        size, idx = size // 2, idx + 1

    def chunk_head(ch, h):
        def part():
            rows = slice(ch * c, (ch + 1) * c)
            sl = slice(h * HG_HD, (h + 1) * HG_HD)
            zq = z_in[rows, h * HG_HD:(h + 1) * HG_HD]
            zf = z_in[rows, HG_WIDTH + h * HG_HD:HG_WIDTH + (h + 1) * HG_HD]
            zi = z_in[rows, 2 * HG_WIDTH + h * HG_HD:2 * HG_WIDTH + (h + 1) * HG_HD]
            zg = z_in[rows, 3 * HG_WIDTH + h * HG_HD:3 * HG_WIDTH + (h + 1) * HG_HD]
            y = _hgrn_chunk(zq, zf, zi, zg, lb_all[:, sl], g_ref[...], st_ref, bk_ref, h, tri, level)
            o_ref[0, rows, h * HG_HD:(h + 1) * HG_HD] = y.astype(o_ref.dtype)
        return part

    return [chunk_head(ch, h) for ch in range(z_in.shape[0] // c) for h in range(HG_HEADS)]


def _mix_kernel(x_ref, mod_ref, g1_ref, w_ref, pos_ref, inv_ref, qg_ref, kg_ref, lbl_ref, hgg_ref,
                qt_ref, kc_ref, vc_ref, ks_ref, vst_ref, kw_ref, vwt_ref, gt_ref, ohg_ref,
                zbuf_ref, st_ref, bk_ref, seg_ref, *, l_idx, tiles_per_seq):
    j = pl.program_id(0)

    @pl.when(j == 0)
    def _():
        zbuf_ref[0] = jnp.zeros((zbuf_ref.shape[1], zbuf_ref.shape[2]), F32)

    @pl.when((j == 0) | (lax.rem(j + tiles_per_seq - 1, tiles_per_seq) == 0))
    def _():
        st_ref[...] = jnp.zeros_like(st_ref)

    zbuf_ref[1] = zbuf_ref[0]
    rec = _hgrn_parts(zbuf_ref.at[1], lbl_ref, hgg_ref, ohg_ref, st_ref, bk_ref, l_idx)
    proj = _project_parts(x_ref, mod_ref, g1_ref, w_ref, pos_ref, inv_ref, qg_ref, kg_ref, zbuf_ref.at[0],
                          qt_ref, kc_ref, vc_ref, ks_ref, vst_ref, kw_ref, vwt_ref, gt_ref, seg_ref)
    for part in rec + proj:
        part()


def _mix(x, mod, g1, w_in_p, pos_row, inv_t, qg_t, kg_t, lb_logits, hg_g, l_idx, tm):
    bsz, t, _ = x.shape
    nm = t // tm
    n_tiles = bsz * nm

    def cur(j):
        jc = jnp.minimum(j, n_tiles - 1)
        return jc // nm, jc % nm

    def prev(j):
        jp = jnp.maximum(j - 1, 0)
        return jp // nm, jp % nm

    kv_shape = (bsz, NSA_KV_HEADS, t, NSA_HD)
    kv_spec = pl.BlockSpec((1, NSA_KV_HEADS, tm, NSA_HD), lambda j: (cur(j)[0], 0, cur(j)[1], 0))
    seg_w = CMP_STRIDE * NSA_HD
    seg_shape = (bsz, NSA_KV_HEADS, t // CMP_STRIDE, seg_w)
    seg_spec = pl.BlockSpec((1, NSA_KV_HEADS, tm // CMP_STRIDE, seg_w), lambda j: (cur(j)[0], 0, cur(j)[1], 0))
    vt_shape = (bsz, NSA_KV_HEADS, VT_ROWS, t)
    vt_spec = pl.BlockSpec((1, NSA_KV_HEADS, VT_ROWS, tm), lambda j: (cur(j)[0], 0, 0, cur(j)[1]))
    const = lambda j: (0, 0)
    return pl.pallas_call(
        functools.partial(_mix_kernel, l_idx=l_idx, tiles_per_seq=nm),
        grid=(n_tiles + 1,),
        in_specs=[pl.BlockSpec((1, tm, D_MODEL), lambda j: (cur(j)[0], cur(j)[1], 0)),
                  pl.BlockSpec((1, 6, D_MODEL), lambda j: (cur(j)[0], 0, 0)),
                  pl.BlockSpec((1, D_MODEL), const),
                  pl.BlockSpec((D_MODEL, IN_COLS), const),
                  pl.BlockSpec((1, 1, tm), lambda j: (cur(j)[0], 0, cur(j)[1])),
                  pl.BlockSpec(inv_t.shape, const),
                  pl.BlockSpec(qg_t.shape, const),
                  pl.BlockSpec(kg_t.shape, lambda j: (0, 0, 0)),
                  pl.BlockSpec(lb_logits.shape, const),
                  pl.BlockSpec((1, HG_HD), const)],
        out_specs=[pl.BlockSpec((1, NSA_HEADS, NSA_HD, tm), lambda j: (cur(j)[0], 0, 0, cur(j)[1])),
                   seg_spec, seg_spec, kv_spec, vt_spec, kv_spec, vt_spec,
                   pl.BlockSpec((1, GATE_ROWS, tm), lambda j: (cur(j)[0], 0, cur(j)[1])),
                   pl.BlockSpec((1, tm, HG_WIDTH), lambda j: (prev(j)[0], prev(j)[1], 0))],
        out_shape=[jax.ShapeDtypeStruct((bsz, NSA_HEADS, NSA_HD, t), BF16),
                   jax.ShapeDtypeStruct(seg_shape, F32),
                   jax.ShapeDtypeStruct(seg_shape, F32),
                   jax.ShapeDtypeStruct(kv_shape, BF16),
                   jax.ShapeDtypeStruct(vt_shape, BF16),
                   jax.ShapeDtypeStruct(kv_shape, BF16),
                   jax.ShapeDtypeStruct(vt_shape, BF16),
                   jax.ShapeDtypeStruct((bsz, GATE_ROWS, t), F32),
                   jax.ShapeDtypeStruct((bsz, t, HG_WIDTH), BF16)],
        scratch_shapes=[pltpu.VMEM((2, tm, 4 * HG_WIDTH), F32),
                        pltpu.VMEM((HG_HEADS, HG_HD, HG_HD), F32),
                        pltpu.VMEM((2, HG_HEADS, HG_CHUNK, HG_HD), F32),
                        pltpu.VMEM((tm, LANES), F32)],
        compiler_params=pltpu.CompilerParams(
            dimension_semantics=("arbitrary",), vmem_limit_bytes=VMEM_LIMIT),
        name="mix",
    )(x, mod, g1, w_in_p, pos_row, inv_t, qg_t, kg_t, lb_logits, hg_g)


def _compress_kernel(xk_ref, xv_ref, pe_ref, w1_ref, w2_ref, ko_ref, vo_ref):
    half = CMP_STRIDE * NSA_HD
    n_g, nseg = xk_ref.shape[1], xk_ref.shape[2]
    outs = []
    for kv, x_ref in enumerate((xk_ref, xv_ref)):
        x = x_ref[0].reshape(n_g * nseg, half)
        ha = _dot((x + pe_ref[kv, 0:1, :]).astype(BF16), w1_ref[kv, :half, :])
        hb = _dot((x + pe_ref[kv, 1:2, :]).astype(BF16), w1_ref[kv, half:, :])
        pre = ha + pltpu.roll(hb, n_g * nseg - 1, axis=0)
        outs.append(_dot(_silu(pre).astype(BF16), w2_ref[kv]))
    for g in range(n_g):
        rows = slice(g * nseg, (g + 1) * nseg)
        ko_ref[0, g] = outs[0][rows].astype(ko_ref.dtype)
        vo_ref[0, g] = outs[1][rows].T.astype(vo_ref.dtype)


def _compress(xk, xv, pe2, w1, w2):
    bsz, g, nseg, width = xk.shape
    x_spec = pl.BlockSpec((1, g, nseg, width), lambda b: (b, 0, 0, 0))
    return pl.pallas_call(
        _compress_kernel,
        grid=(bsz,),
        in_specs=[x_spec, x_spec,
                  pl.BlockSpec(pe2.shape, lambda b: (0, 0, 0)),
                  pl.BlockSpec(w1.shape, lambda b: (0, 0, 0)),
                  pl.BlockSpec(w2.shape, lambda b: (0, 0, 0))],
        out_specs=[pl.BlockSpec((1, g, nseg, NSA_HD), lambda b: (b, 0, 0, 0)),
                   pl.BlockSpec((1, g, NSA_HD, nseg), lambda b: (b, 0, 0, 0))],
        out_shape=[jax.ShapeDtypeStruct((bsz, g, nseg, NSA_HD), BF16),
                   jax.ShapeDtypeStruct((bsz, g, NSA_HD, nseg), BF16)],
        compiler_params=pltpu.CompilerParams(
            dimension_semantics=("arbitrary",), vmem_limit_bytes=VMEM_LIMIT),
        name="compress",
    )(xk, xv, pe2, w1, w2)


def _nsa_kernel(bound_ref, qt_ref, kc_ref, vct_ref, ks_ref, vst_ref, kw_ref, vwt_ref, gt_ref, mt_ref,
                o_ref, sel_ref, *, tq, tks, bounded):
    r = NSA_GROUP
    n_g = NSA_KV_HEADS
    ch = NSA_CHAIN
    lanes = ch * tq
    q0 = pl.program_id(1) * tq
    chains = [(g, g * r + c * ch) for g in range(n_g) for c in range(r // ch)]
    q_ts = [jnp.concatenate([qt_ref[0, h0 + i] for i in range(ch)], axis=1)
            for _, h0 in chains]

    def tile_heads(a):
        return jnp.concatenate([a] * ch, axis=1)

    n_blk = kc_ref.shape[2]
    nb = mt_ref.shape[0]
    blk_end = lax.broadcasted_iota(jnp.int32, (n_blk, tq), 0) * CMP_STRIDE + (CMP_BLOCK - 1)
    t_row = q0 + lax.broadcasted_iota(jnp.int32, (1, tq), 1)
    cvalid = tile_heads(jnp.where(blk_end <= t_row, 1.0, 0.0)) > 0.5
    some = tile_heads((t_row >= CMP_BLOCK - 1).astype(F32))
    j = lax.broadcasted_iota(jnp.int32, (nb, tq), 0)
    cur = jnp.right_shift(t_row, SLC_BLOCK.bit_length() - 1)
    forced = (j == 0) | (j == cur) | (j == cur - 1)
    o_cmp = []
    psum = [None] * n_g
    for (g, _), q_t in zip(chains, q_ts):
        s = jnp.where(cvalid, _dot(kc_ref[0, g], q_t), NEG)
        e = jnp.exp2(s - jnp.max(s, axis=0, keepdims=True))
        p = e * (some / jnp.sum(e, axis=0, keepdims=True))
        o_cmp.append(_dot(vct_ref[0, g], p.astype(BF16)))
        for i in range(ch):
            part = p[:, i * tq:(i + 1) * tq]
            psum[g] = part if psum[g] is None else psum[g] + part
    for g in range(n_g):
        p_hi = psum[g].astype(BF16)
        p_lo = (psum[g] - p_hi.astype(F32)).astype(BF16)
        imp = _dot(mt_ref[...], p_hi) + _dot(mt_ref[...], p_lo)
        imp = jnp.where(j <= cur, jnp.where(forced, jnp.inf, imp), -1.0)
        ranks = [jnp.zeros((nb, tq), jnp.int32) for _ in range(RANK_LANES)]
        for i in range(nb):
            row_i = imp[i:i + 1, :]
            ahead = (row_i > imp) | ((row_i == imp) & (j > i))
            ranks[i % RANK_LANES] = ranks[i % RANK_LANES] + ahead.astype(jnp.int32)
        rank = functools.reduce(lambda a, b: a + b, ranks)
        sel_ref[g] = jnp.where(rank < min(SLC_TOPK, nb), 0.0, NEG)

    shift = bound_ref[0] * LOG2E if bounded else None

    span = WIN + tq
    start = pl.multiple_of(jnp.maximum(q0 - WIN, 0), tq)
    dist = (lax.broadcasted_iota(jnp.int32, (span, tq), 1)
            - lax.broadcasted_iota(jnp.int32, (span, tq), 0)) + (q0 - start)
    wbias = jnp.where((dist >= 0) & (dist < WIN), 0.0, NEG)
    wbias = tile_heads(wbias - shift if bounded else wbias)
    o_win = []
    for (g, _), q_t in zip(chains, q_ts):
        sw = _dot(kw_ref[0, g, pl.ds(start, span), :], q_t) + wbias
        ew = jnp.exp2(sw if bounded else sw - jnp.max(sw, axis=0, keepdims=True))
        ow = _dot(vwt_ref[0, g, :, pl.ds(start, span)], ew.astype(BF16))
        o_win.append(ow[:NSA_HD] * (1.0 / ow[NSA_HD:NSA_HD + 1]))

    blocks_per_tile = tks // SLC_BLOCK
    rel = (lax.broadcasted_iota(jnp.int32, (tks, tq), 1)
           - lax.broadcasted_iota(jnp.int32, (tks, tq), 0))

    def slc_bias(it, k0, need_causal):
        biases = []
        for g in range(n_g):
            if isinstance(it, int):
                rows = sel_ref[g, it * blocks_per_tile:(it + 1) * blocks_per_tile, :]
            else:
                rows = sel_ref[g, pl.ds(pl.multiple_of(it * blocks_per_tile, blocks_per_tile), blocks_per_tile), :]
            bias = jnp.concatenate([jnp.broadcast_to(rows[jj:jj + 1], (SLC_BLOCK, tq))
                                    for jj in range(blocks_per_tile)], axis=0)
            if need_causal:
                bias = jnp.where(rel + (q0 - k0) >= 0, bias, NEG)
            biases.append(tile_heads(bias - shift if bounded else bias))
        return biases

    def slc_body(it, carry, need_causal=True):
        k0 = it * tks if isinstance(it, int) else pl.multiple_of(it * tks, tks)
        biases = slc_bias(it, k0, need_causal)
        out = []
        for (g, _), q_t, state in zip(chains, q_ts, carry):
            sc = _dot(ks_ref[0, g, pl.ds(k0, tks), :], q_t) + biases[g]
            vt = vst_ref[0, g, :, pl.ds(k0, tks)]
            if bounded:
                (acc,) = state
                acc = acc + _dot(vt, jnp.exp2(sc).astype(BF16))
                out.append((acc,))
            else:
                m, acc = state
                m_new = jnp.maximum(m, jnp.max(sc, axis=0, keepdims=True))
                acc = jnp.exp2(m - m_new) * acc + _dot(vt, jnp.exp2(sc - m_new).astype(BF16))
                out.append((m_new, acc))
        return tuple(out)

    zero_state = (jnp.zeros((VT_ROWS, lanes), F32),)
    init = tuple(zero_state if bounded else (jnp.full((1, lanes), NEG, F32),) + zero_state for _ in chains)
    n_it = (q0 + tq + tks - 1) // tks
    if bounded:
        def slc_tiles(n):
            carry = init
            for it in range(n):
                carry = slc_body(it, carry, need_causal=(it == n - 1))
            return carry

        max_it = ks_ref.shape[2] // tks
        fin = lax.switch(n_it - 1, [functools.partial(slc_tiles, n) for n in range(1, max_it + 1)])
    else:
        fin = lax.fori_loop(0, n_it, slc_body, init)
    o_slc = [st[-1][:NSA_HD] * (1.0 / st[-1][NSA_HD:NSA_HD + 1]) for st in fin]

    for ci, (_, h0) in enumerate(chains):
        for i in range(ch):
            cols = slice(i * tq, (i + 1) * tq)
            head = h0 + i
            o_h = jnp.zeros((NSA_HD, tq), F32)
            for br, o_b in enumerate((o_cmp[ci], o_slc[ci], o_win[ci])):
                row = head * N_BRANCH + br
                o_h = o_h + gt_ref[0, row:row + 1, :] * o_b[:, cols]
            o_ref[0, :, head * NSA_HD:(head + 1) * NSA_HD] = o_h.T.astype(o_ref.dtype)


def _nsa(q_t, kc, vct, ks, vst, kw, vwt, gates_t, mt, bound, tq, tks, bounded):
    bsz, _, _, t = q_t.shape
    n_blk = kc.shape[2]
    n_g = NSA_KV_HEADS
    full = lambda b, i: (b, 0, 0, 0)
    k_spec = pl.BlockSpec((1, n_g, t, NSA_HD), full)
    vt_spec = pl.BlockSpec((1, n_g, VT_ROWS, t), full)
    return pl.pallas_call(
        functools.partial(_nsa_kernel, tq=tq, tks=tks, bounded=bounded),
        grid=(bsz, t // tq),
        in_specs=[pl.BlockSpec(memory_space=pltpu.SMEM),
                  pl.BlockSpec((1, NSA_HEADS, NSA_HD, tq), lambda b, i: (b, 0, 0, i)),
                  pl.BlockSpec((1, n_g, n_blk, NSA_HD), full),
                  pl.BlockSpec((1, n_g, NSA_HD, n_blk), full),
                  k_spec, vt_spec, k_spec, vt_spec,
                  pl.BlockSpec((1, GATE_ROWS, tq), lambda b, i: (b, 0, i)),
                  pl.BlockSpec(mt.shape, lambda b, i: (0, 0))],
        out_specs=pl.BlockSpec((1, tq, NSA_WIDTH), lambda b, i: (b, i, 0)),
        out_shape=jax.ShapeDtypeStruct((bsz, t, NSA_WIDTH), BF16),
        scratch_shapes=[pltpu.VMEM((n_g, mt.shape[0], tq), F32)],
        compiler_params=pltpu.CompilerParams(
            dimension_semantics=("arbitrary", "arbitrary"), vmem_limit_bytes=VMEM_LIMIT),
        name="nsa_bounded" if bounded else "nsa",
    )(bound, q_t, kc, vct, ks, vst, kw, vwt, gates_t, mt)


def _causal_conv(u, prev, cw, cb):
    tm = u.shape[0]
    g = SUBLANES
    r8 = lax.broadcasted_iota(jnp.int32, (g, u.shape[1]), 0)
    wrap1 = jnp.where(r8 == 0, prev[2 * g - 1:2 * g], pltpu.roll(u[tm - g:], 1, axis=0))
    wrap2 = jnp.where(r8 == 0, prev[g - 1:g], pltpu.roll(u[tm - 2 * g:tm - g], 1, axis=0))
    u1 = jnp.concatenate([wrap1, u[:tm - g]], axis=0)
    u2 = jnp.concatenate([wrap2, wrap1, u[:tm - 2 * g]], axis=0)
    return cb + u2 * cw[0:1] + u1 * cw[1:2] + u * cw[2:3]


def _ffn_kernel(x_ref, hg_ref, ns_ref, mod_ref, wo_ref, g2_ref, wu_ref, cw_ref, cb_ref, wd_ref,
                o_ref, carry_ref, buf_ref, x1_ref, h2_ref, u_ref, g_ref, acc_ref, *, fc):
    n_sub, sub = x1_ref.shape[0], x1_ref.shape[1]
    n_lt = D_MODEL // LANES
    groups = sub // SUBLANES
    pitch = buf_ref.shape[2] // SUBLANES
    first = pl.program_id(1) == 0

    def head(s):
        rows = slice(s * sub, (s + 1) * sub)
        mix = _dot(hg_ref[0, rows], wo_ref[:HG_WIDTH, :]) + _dot(ns_ref[0, rows], wo_ref[HG_WIDTH:, :])
        x1_nat = x_ref[0, rows] + mod_ref[0, 2:3, :] * mix
        for c in range(n_lt):
            for sg in range(SUBLANES):
                buf_ref[s, c, sg * pitch:sg * pitch + groups] = x1_nat[sg * groups:(sg + 1) * groups,
                                                                       c * LANES:(c + 1) * LANES]
        x1 = jnp.concatenate(
            [jnp.concatenate([buf_ref[s, c, pl.ds(j, SUBLANES, stride=pitch), :] for j in range(groups)], axis=0)
             for c in range(n_lt)], axis=1)
        x1_ref[s] = x1
        y = x1 * lax.rsqrt(jnp.mean(x1 * x1, axis=-1, keepdims=True) + EPS) * g2_ref[...]
        h2_ref[s] = (y * (1.0 + mod_ref[0, 4:5, :]) + mod_ref[0, 3:4, :]).astype(h2_ref.dtype)
        acc_ref[s] = jnp.zeros((sub, D_MODEL), F32)

    def tail(s):
        out = x1_ref[s] + mod_ref[0, 5:6, :] * acc_ref[s]
        for c in range(n_lt):
            for j in range(groups):
                buf_ref[s, c, pl.ds(j, SUBLANES, stride=pitch), :] = out[j * SUBLANES:(j + 1) * SUBLANES,
                                                                         c * LANES:(c + 1) * LANES]
        for c in range(n_lt):
            for sg in range(SUBLANES):
                o_ref[0, s * sub + sg * groups:s * sub + (sg + 1) * groups, c * LANES:(c + 1) * LANES] = (
                    buf_ref[s, c, sg * pitch:sg * pitch + groups])

    n = D_FF // fc

    def up(c, slot):
        s, jc = divmod(c, n)
        for half in range(2):
            off = half * D_FF + jc * fc
            u_ref[slot, half] = _dot(h2_ref[s], wu_ref[:, off:off + fc])

    def act(c, slot):
        s, jc = divmod(c, n)
        halves = []
        for half in range(2):
            cols = slice(half * D_FF + jc * fc, half * D_FF + (jc + 1) * fc)
            u = u_ref[slot, half]
            prev = jnp.where(first, 0.0, carry_ref[:, cols]) if s == 0 else carry_ref[:, cols]
            carry_ref[:, cols] = u[sub - 2 * SUBLANES:]
            halves.append(_causal_conv(u, prev, cw_ref[:, cols], cb_ref[:, cols]))
        a, v = halves
        g_ref[slot] = (_silu(a) * v).astype(g_ref.dtype)

    def down(c, slot):
        s, jc = divmod(c, n)
        acc_ref[s] += _dot(g_ref[slot], wd_ref[jc * fc:(jc + 1) * fc, :])

    total = n_sub * n
    head(0)
    for i in range(total + 2):
        if i < total:
            up(i, i % 2)
        if 0 <= i - 1 < total:
            act(i - 1, (i - 1) % 2)
        if 0 <= i - 2 < total:
            down(i - 2, i % 2)
        s, jc = divmod(i, n)
        if jc == HEAD_AT and s + 1 < n_sub:
            head(s + 1)
        if jc == 1 and 1 <= s <= n_sub:
            tail(s - 1)


def _ffn(x, o_hg, o_nsa, mod, w_out, g2, w_up, conv_w, conv_b, w_down, tm, sub, fc):
    bsz, t, _ = x.shape
    row_spec = lambda w: pl.BlockSpec((1, tm, w), lambda b, i: (b, i, 0))
    groups = sub // SUBLANES
    pad = SUBLANES if (groups // SUBLANES) % 2 == 0 else 0
    n_sub = tm // sub
    resident = lambda a: pl.BlockSpec(a.shape, lambda b, i: (0, 0), pipeline_mode=pl.Buffered(1))
    return pl.pallas_call(
        functools.partial(_ffn_kernel, fc=fc),
        grid=(bsz, t // tm),
        in_specs=[row_spec(D_MODEL), row_spec(HG_WIDTH), row_spec(NSA_WIDTH),
                  pl.BlockSpec((1, 6, D_MODEL), lambda b, i: (b, 0, 0)),
                  resident(w_out), resident(g2), resident(w_up), resident(conv_w), resident(conv_b),
                  resident(w_down)],
        out_specs=row_spec(D_MODEL),
        out_shape=jax.ShapeDtypeStruct(x.shape, F32),
        scratch_shapes=[pltpu.VMEM((2 * SUBLANES, 2 * D_FF), F32),
                        pltpu.VMEM((n_sub, D_MODEL // LANES, sub + SUBLANES * pad, LANES), F32),
                        pltpu.VMEM((n_sub, sub, D_MODEL), F32),
                        pltpu.VMEM((n_sub, sub, D_MODEL), BF16),
                        pltpu.VMEM((2, 2, sub, fc), F32),
                        pltpu.VMEM((2, sub, fc), BF16),
                        pltpu.VMEM((n_sub, sub, D_MODEL), F32)],
        compiler_params=pltpu.CompilerParams(
            dimension_semantics=("arbitrary", "arbitrary"), vmem_limit_bytes=VMEM_LIMIT),
        name="ffn",
    )(x, o_hg, o_nsa, mod, w_out, g2, w_up, conv_w, conv_b, w_down)


def _rope_tables():
    half = ROPE_DIM // 2
    inv = ROPE_THETA ** (-jnp.arange(half, dtype=F32) * 2.0 / ROPE_DIM)
    return (jnp.tile(inv.reshape(half, 1), (1, LANES)),)


def _gain_t(g):
    return jnp.tile(g.reshape(NSA_HD, 1), (LANES // NSA_HD, LANES))


def _selection_tables(t):
    n_seg = t // CMP_STRIDE
    nb = t // SLC_BLOCK
    cst = np.arange(n_seg) * CMP_STRIDE
    sst = np.arange(nb) * SLC_BLOCK
    ovl = np.clip(np.minimum(cst[:, None] + CMP_BLOCK, sst[None] + SLC_BLOCK)
                  - np.maximum(cst[:, None], sst[None]), 0, None) / CMP_BLOCK
    ovl[(t - CMP_BLOCK) // CMP_STRIDE + 1:] = 0.0
    return (jnp.asarray(ovl.T, dtype=BF16),)


def _layer(x, mod, pos_row, l, p, tables):
    bsz, t, _ = x.shape
    inv_t, mt = tables
    w_in_p = p["w_in"][l].astype(BF16)
    qg_t = _gain_t(p["q_norm_g"][l])
    kg_t = jnp.stack([_gain_t(p["k_norm_g"][l, br]) for br in range(N_BRANCH)])
    q_t, kc, vc, ks, vst, kw, vwt, gates_t, o_hg = _mix(
        x, mod, p["norm1_g"][l].reshape(1, D_MODEL), w_in_p, pos_row, inv_t, qg_t, kg_t,
        p["lb_logits"], p["hg_norm_g"][l].reshape(1, HG_HD), l, tm=TM_MIX)

    pe2 = p["pe_cmp"][l].reshape(2, 2, CMP_STRIDE * NSA_HD)
    kcmp, vcmp_t = _compress(kc, vc, pe2, p["w_cmp1"][l].astype(BF16), p["w_cmp2"][l].astype(BF16))
    bound = (SCORE_BOUND_MARGIN * NSA_HD ** 0.5 * jnp.max(jnp.abs(p["q_norm_g"][l]))
             * jnp.max(jnp.abs(p["k_norm_g"][l, 1:]))).reshape(1).astype(F32)
    nsa_args = (q_t, kcmp, vcmp_t, ks, vst, kw, vwt, gates_t, mt, bound)
    o_nsa = lax.cond(bound[0] <= MAX_SCORE_BOUND,
                     lambda a: _nsa(*a, tq=TQ_NSA, tks=TKS_NSA, bounded=True),
                     lambda a: _nsa(*a, tq=TQ_NSA, tks=TKS_NSA, bounded=False), nsa_args)

    return _ffn(x, o_hg, o_nsa, mod, p["w_out"][l].astype(BF16), p["norm2_g"][l].reshape(1, D_MODEL),
                p["w_up"][l].astype(BF16), p["conv_w"][l], p["conv_b"][l].reshape(1, 2 * D_FF),
                p["w_down"][l].astype(BF16), tm=TM_FFN, sub=SUB_FFN, fc=FC_FFN)


def kernel(x, c, positions, w_ada, b_ada, norm1_g, w_in, lb_logits, hg_norm_g, q_norm_g, k_norm_g, pe_cmp, w_cmp1, w_cmp2, w_out, norm2_g, w_up, conv_w, conv_b, w_down):
    p = dict(w_in=w_in, norm1_g=norm1_g, lb_logits=lb_logits, hg_norm_g=hg_norm_g, q_norm_g=q_norm_g,
             k_norm_g=k_norm_g, pe_cmp=pe_cmp, w_cmp1=w_cmp1, w_cmp2=w_cmp2, w_out=w_out,
             norm2_g=norm2_g, w_up=w_up, conv_w=conv_w, conv_b=conv_b, w_down=w_down)
    bsz, t, _ = x.shape
    assert x.shape[2] == D_MODEL and t % TM_MIX == 0 and t % TKS_NSA == 0 and t >= WIN + TQ_NSA
    assert TKS_NSA % TQ_NSA == 0 and TM_MIX % HG_CHUNK == 0 and D_FF % FC_FFN == 0 and TM_FFN % SUB_FFN == 0
    tables = _rope_tables() + _selection_tables(t)
    pos_row = positions.reshape(bsz, 1, t)
    for l in range(w_ada.shape[0]):
        mod = _ada(c, w_ada[l], b_ada[l]).reshape(bsz, 6, D_MODEL)
        x = _layer(x, mod, pos_row, l, p, tables)
    return x
```

```python
import functools

import jax
import jax.numpy as jnp
import numpy as np
from jax import lax
from jax.experimental import pallas as pl
from jax.experimental.pallas import tpu as pltpu

D_MODEL = 1024
HG_HEADS = 4
HG_HD = 128
HG_WIDTH = HG_HEADS * HG_HD
HG_CHUNK = 128
HG_SUB = 8
LOG2E = 1.4426950408889634
NSA_HEADS = 8
NSA_KV_HEADS = 2
NSA_HD = 64
NSA_GROUP = NSA_HEADS // NSA_KV_HEADS
NSA_CHAIN = 4
RANK_LANES = 4
NSA_WIDTH = NSA_HEADS * NSA_HD
N_BRANCH = 3
CMP_BLOCK = 32
CMP_STRIDE = 16
CMP_HIDDEN = 256
SLC_BLOCK = 64
SLC_TOPK = 16
WIN = 512
ROPE_DIM = NSA_HD // 4
ROPE_THETA = 500000.0
D_FF = 2816
CONV_W = 3
EPS = 1e-6
NEG = -1e30
SCORE_BOUND_MARGIN = 1.02
MAX_SCORE_BOUND = 40.0

LANES = 128
SUBLANES = 8
VMEM_LIMIT = 56 * 1024 * 1024

TM_MIX = 512
TQ_NSA = 256
TKS_NSA = 512
TM_FFN = 512
SUB_FFN = 256
HEAD_AT = 7
FC_FFN = 256

OFF_HG = 0
OFF_Q = 4 * HG_WIDTH
OFF_KV = OFF_Q + NSA_WIDTH
OFF_G = OFF_KV + 6 * NSA_KV_HEADS * NSA_HD
IN_COLS = OFF_G + N_BRANCH * NSA_HEADS
GATE_ROWS = N_BRANCH * NSA_HEADS
VT_ROWS = NSA_HD + SUBLANES

BF16 = jnp.bfloat16
F32 = jnp.float32


def _dot(a, b):
    return jnp.dot(a, b, preferred_element_type=F32)


def _dot_nt(a, b):
    return lax.dot_general(a, b, (((1,), (1,)), ((), ())), preferred_element_type=F32)


def _exp_neg(x):
    return jnp.exp2(x * (-LOG2E))


def _sigmoid(x):
    return 1.0 / (1.0 + _exp_neg(x))


def _silu(x):
    return x * _sigmoid(x)


def _ada_kernel(c_ref, w_ref, b_ref, o_ref):
    cs = _silu(c_ref[...])
    o_ref[...] = _dot(cs.astype(BF16), w_ref[...].astype(BF16)) + b_ref[...]


def _ada(c, w, b):
    bsz = c.shape[0]
    n = w.shape[1]
    tn = D_MODEL
    return pl.pallas_call(
        _ada_kernel,
        grid=(n // tn,),
        in_specs=[pl.BlockSpec((bsz, D_MODEL), lambda j: (0, 0)),
                  pl.BlockSpec((D_MODEL, tn), lambda j: (0, j)),
                  pl.BlockSpec((1, tn), lambda j: (0, j))],
        out_specs=pl.BlockSpec((bsz, tn), lambda j: (0, j)),
        out_shape=jax.ShapeDtypeStruct((bsz, n), F32),
        name="ada",
    )(c, w, b.reshape(1, n))


def _pair_norm_rope_t(xt, g_t, cos_t, sin_t):
    half = ROPE_DIM // 2
    outs = []
    for hh in range(2):
        x = xt[hh * NSA_HD:(hh + 1) * NSA_HD]
        ms = jnp.mean(x * x, axis=0, keepdims=True)
        xn = x * lax.rsqrt(ms + EPS) * g_t[hh * NSA_HD:(hh + 1) * NSA_HD]
        x1, x2 = xn[:half], xn[half:ROPE_DIM]
        outs += [x1 * cos_t - x2 * sin_t, x2 * cos_t + x1 * sin_t, xn[ROPE_DIM:]]
    return jnp.concatenate(outs, axis=0)


def _project_parts(x_ref, mod_ref, g1_ref, w_ref, pos_ref, inv_ref, qg_ref, kg_ref,
                   zhg_out, qt_ref, kc_ref, vc_ref, ks_ref, vst_ref, kw_ref, vwt_ref, gt_ref, seg_ref):
    x = x_ref[0]
    ms = jnp.mean(x * x, axis=-1, keepdims=True)
    y = x * lax.rsqrt(ms + EPS) * g1_ref[...]
    h = (y * (1.0 + mod_ref[0, 1:2, :]) + mod_ref[0, 0:1, :]).astype(BF16)

    tm = x.shape[0]
    reps = tm // LANES

    def lane_tile(a):
        return jnp.concatenate([a] * reps, axis=1)

    ang = lane_tile(inv_ref[...]) * pos_ref[0].astype(F32)
    cos_t = jnp.cos(ang)
    sin_t = jnp.sin(ang)
    scale = NSA_HD ** -0.5 * LOG2E
    vals = {}

    def hg_cols(lo, hi):
        def part():
            zhg_out[:, lo:hi] = _dot(h, w_ref[:, OFF_HG + lo:OFF_HG + hi])
        return part

    def q_pair(p):
        def part():
            if p % 2 == 0:
                vals["zq"] = _dot(h, w_ref[:, OFF_Q + p * LANES:OFF_Q + (p + 2) * LANES])
                vals["qg"] = lane_tile(qg_ref[...])
            off = (p % 2) * LANES
            rt = _pair_norm_rope_t(vals["zq"][:, off:off + LANES].T, vals["qg"], cos_t, sin_t)
            rt = (rt * scale).astype(qt_ref.dtype)
            qt_ref[0, 2 * p] = rt[:NSA_HD]
            qt_ref[0, 2 * p + 1] = rt[NSA_HD:]
        return part

    def kv_dot(br):
        vals["zkv"] = _dot(h, w_ref[:, OFF_KV + 2 * br * LANES:OFF_KV + (2 * br + 2) * LANES])

    def store_segments(a, out_ref):
        seg_ref[...] = a
        n_seg = tm // CMP_STRIDE
        lane = lax.broadcasted_iota(jnp.int32, (n_seg, LANES), 1)
        for l in range(0, CMP_STRIDE, 2):
            even = seg_ref[pl.ds(l, n_seg, stride=CMP_STRIDE), :]
            odd = seg_ref[pl.ds(l + 1, n_seg, stride=CMP_STRIDE), :]
            cols = slice((l // 2) * LANES, (l // 2 + 1) * LANES)
            out_ref[0, 0, :, cols] = jnp.where(lane < NSA_HD, even, pltpu.roll(odd, NSA_HD, axis=1))
            out_ref[0, 1, :, cols] = jnp.where(lane < NSA_HD, pltpu.roll(even, NSA_HD, axis=1), odd)

    def key(br, k_ref):
        def part():
            kv_dot(br)
            kt = _pair_norm_rope_t(vals["zkv"][:, :LANES].T, lane_tile(kg_ref[br]), cos_t, sin_t)
            if br == 0:
                store_segments(kt.T, k_ref)
                store_segments(vals["zkv"][:, LANES:], vc_ref)
            else:
                kk = kt.T.astype(k_ref.dtype)
                for g in range(NSA_KV_HEADS):
                    k_ref[0, g] = kk[:, g * NSA_HD:(g + 1) * NSA_HD]
        return part

    def value_t(vt_ref):
        def part():
            vt = vals["zkv"][:, LANES:].T.astype(vt_ref.dtype)
            ones_row = jnp.where(lax.broadcasted_iota(jnp.int32, (SUBLANES, tm), 0) == 0, 1.0, 0.0).astype(vt_ref.dtype)
            for g in range(NSA_KV_HEADS):
                vt_ref[0, g, :NSA_HD] = vt[g * NSA_HD:(g + 1) * NSA_HD]
                vt_ref[0, g, NSA_HD:] = ones_row
        return part

    def gates_part():
        gates = _sigmoid(_dot(h, w_ref[:, OFF_G:IN_COLS]))
        wide = jnp.concatenate([gates, jnp.zeros((tm, LANES - GATE_ROWS), F32)], axis=1)
        gt_ref[0] = wide.T[:GATE_ROWS]

    wide = 2 * LANES
    hg = [hg_cols(lo, lo + wide) for lo in range(0, 4 * HG_WIDTH, wide)]
    rest = [q_pair(0), q_pair(1), q_pair(2), q_pair(3), key(0, kc_ref), key(1, ks_ref), value_t(vst_ref),
            key(2, kw_ref), value_t(vwt_ref), gates_part]
    parts = []
    for k in range(max(len(hg), len(rest))):
        parts += hg[k:k + 1] + rest[k:k + 1]
    return parts


def _hgrn_chunk(zq, zf, zi, zg, lb, hg_g, st_ref, bk_ref, h, tri, level):
    c = HG_CHUNK
    e_z = _exp_neg(jnp.abs(zf))
    logsig = jnp.minimum(zf, 0.0) - jnp.log(1.0 + e_z)
    a = jnp.log(lb)
    bb = jnp.log1p(-lb) + logsig
    logf = jnp.maximum(a, bb) + jnp.log(1.0 + _exp_neg(jnp.abs(a - bb)))
    k = (1.0 - lb) * (jnp.where(zf >= 0.0, e_z, 1.0) / (1.0 + e_z))
    q = _silu(zq)
    v = zi
    l_hi = logf.astype(BF16)
    l_mid = (logf - l_hi.astype(F32)).astype(BF16)
    l_lo = (logf - l_hi.astype(F32) - l_mid.astype(F32)).astype(BF16)
    bc = _dot(tri, l_hi) + _dot(tri, l_mid) + _dot(tri, l_lo)

    col = lax.broadcasted_iota(jnp.int32, (HG_SUB, c), 1)
    b2 = bc * LOG2E
    bk_ref[0, h] = b2
    bk_ref[1, h] = k
    rows_a = []
    for i in range(c // HG_SUB):
        lo = i * HG_SUB
        b_i = b2[lo:lo + HG_SUB]
        q_i = q[lo:lo + HG_SUB]
        k_i = k[lo:lo + HG_SUB]
        a_i = jnp.zeros((HG_SUB, c), F32)
        for s in range(HG_SUB):
            b_s = bk_ref[0, h, lo + s:lo + s + 1, :]
            k_s = bk_ref[1, h, lo + s:lo + s + 1, :]
            e = jnp.exp2(jnp.minimum(b_i - b_s, 0.0)) * q_i * k_s
            a_i = jnp.where(col == lo + s, jnp.sum(e, axis=-1, keepdims=True), a_i)
        rows_a.append(a_i)
    amat = jnp.concatenate(rows_a, axis=0)

    size, idx = c // 2, 1
    while size >= HG_SUB:
        pieces = []
        for e0 in range(0, c, 2 * size):
            o0 = e0 + size
            r = b2[o0:o0 + 1]
            pieces.append(k[e0:o0] * jnp.exp2(r - b2[e0:o0]))
            pieces.append(q[o0:o0 + size] * jnp.exp2(b2[o0:o0 + size] - r))
        hmat = jnp.concatenate(pieces, axis=0).astype(BF16)
        amat = jnp.where(level == idx, _dot_nt(hmat, hmat), amat)
        size, idx = size // 2, idx + 1
    amat = jnp.where(level >= 0, amat, 0.0)

    st = st_ref[h]
    o = _dot_nt((q * jnp.exp2(b2)).astype(BF16), st.astype(BF16)) + _dot(amat.astype(BF16), v.astype(BF16))
    bl = b2[c - 1:c]
    kdec = (k * jnp.exp2(bl - b2)).astype(BF16)
    st_ref[h] = jnp.exp2(bl) * st + _dot(v.T.astype(BF16), kdec)

    y = o * lax.rsqrt(jnp.mean(o * o, axis=-1, keepdims=True) + EPS) * hg_g
    return y * _silu(zg)


def _hgrn_parts(z_in, lbl_ref, g_ref, o_ref, st_ref, bk_ref, l_idx):
    lg = lbl_ref[...]
    ex = jnp.exp(lg - jnp.max(lg, axis=0, keepdims=True))
    sm = ex / jnp.sum(ex, axis=0, keepdims=True)
    lb_all = jnp.sum(sm[:l_idx + 1], axis=0, keepdims=True)

    c = HG_CHUNK
    ri = lax.broadcasted_iota(jnp.int32, (c, c), 0)
    ci = lax.broadcasted_iota(jnp.int32, (c, c), 1)
    tri = jnp.where(ci <= ri, 1.0, 0.0).astype(BF16)
    level = jnp.where(ci > ri, -1, 0)
    size, idx = c // 2, 1
    while size >= HG_SUB:
        sh = size.bit_length() - 1
        paired = ((ri >> (sh + 1)) == (ci >> (sh + 1))) & ((ri >> sh) != (ci >> sh)) & (ci <= ri)
        level = jnp.where(paired, idx, level)
        size, idx = size // 2, idx + 1

    def chunk_head(ch, h):
        def part():
            rows = slice(ch * c, (ch + 1) * c)
            sl = slice(h * HG_HD, (h + 1) * HG_HD)
            zq = z_in[rows, h * HG_HD:(h + 1) * HG_HD]
            zf = z_in[rows, HG_WIDTH + h * HG_HD:HG_WIDTH + (h + 1) * HG_HD]
            zi = z_in[rows, 2 * HG_WIDTH + h * HG_HD:2 * HG_WIDTH + (h + 1) * HG_HD]
            zg = z_in[rows, 3 * HG_WIDTH + h * HG_HD:3 * HG_WIDTH + (h + 1) * HG_HD]
            y = _hgrn_chunk(zq, zf, zi, zg, lb_all[:, sl], g_ref[...], st_ref, bk_ref, h, tri, level)
            o_ref[0, rows, h * HG_HD:(h + 1) * HG_HD] = y.astype(o_ref.dtype)
        return part

    return [chunk_head(ch, h) for ch in range(z_in.shape[0] // c) for h in range(HG_HEADS)]


def _mix_kernel(x_ref, mod_ref, g1_ref, w_ref, pos_ref, inv_ref, qg_ref, kg_ref, lbl_ref, hgg_ref,
                qt_ref, kc_ref, vc_ref, ks_ref, vst_ref, kw_ref, vwt_ref, gt_ref, ohg_ref,
                zbuf_ref, st_ref, bk_ref, seg_ref, *, l_idx, tiles_per_seq):
    j = pl.program_id(0)

    @pl.when(j == 0)
    def _():
        zbuf_ref[...] = jnp.zeros_like(zbuf_ref)

    @pl.when((j == 0) | (lax.rem(j + tiles_per_seq - 1, tiles_per_seq) == 0))
    def _():
        st_ref[...] = jnp.zeros_like(st_ref)

    rec = _hgrn_parts(zbuf_ref, lbl_ref, hgg_ref, ohg_ref, st_ref, bk_ref, l_idx)
    proj = _project_parts(x_ref, mod_ref, g1_ref, w_ref, pos_ref, inv_ref, qg_ref, kg_ref, zbuf_ref,
                          qt_ref, kc_ref, vc_ref, ks_ref, vst_ref, kw_ref, vwt_ref, gt_ref, seg_ref)
    for part in rec + proj:
        part()


def _mix(x, mod, g1, w_in_p, pos_row, inv_t, qg_t, kg_t, lb_logits, hg_g, l_idx, tm):
    bsz, t, _ = x.shape
    nm = t // tm
    n_tiles = bsz * nm

    def cur(j):
        jc = jnp.minimum(j, n_tiles - 1)
        return jc // nm, jc % nm

    def prev(j):
        jp = jnp.maximum(j - 1, 0)
        return jp // nm, jp % nm

    kv_shape = (bsz, NSA_KV_HEADS, t, NSA_HD)
    kv_spec = pl.BlockSpec((1, NSA_KV_HEADS, tm, NSA_HD), lambda j: (cur(j)[0], 0, cur(j)[1], 0))
    seg_w = CMP_STRIDE * NSA_HD
    seg_shape = (bsz, NSA_KV_HEADS, t // CMP_STRIDE, seg_w)
    seg_spec = pl.BlockSpec((1, NSA_KV_HEADS, tm // CMP_STRIDE, seg_w), lambda j: (cur(j)[0], 0, cur(j)[1], 0))
    vt_shape = (bsz, NSA_KV_HEADS, VT_ROWS, t)
    vt_spec = pl.BlockSpec((1, NSA_KV_HEADS, VT_ROWS, tm), lambda j: (cur(j)[0], 0, 0, cur(j)[1]))
    const = lambda j: (0, 0)
    return pl.pallas_call(
        functools.partial(_mix_kernel, l_idx=l_idx, tiles_per_seq=nm),
        grid=(n_tiles + 1,),
        in_specs=[pl.BlockSpec((1, tm, D_MODEL), lambda j: (cur(j)[0], cur(j)[1], 0)),
                  pl.BlockSpec((1, 6, D_MODEL), lambda j: (cur(j)[0], 0, 0)),
                  pl.BlockSpec((1, D_MODEL), const),
                  pl.BlockSpec((D_MODEL, IN_COLS), const),
                  pl.BlockSpec((1, 1, tm), lambda j: (cur(j)[0], 0, cur(j)[1])),
                  pl.BlockSpec(inv_t.shape, const),
                  pl.BlockSpec(qg_t.shape, const),
                  pl.BlockSpec(kg_t.shape, lambda j: (0, 0, 0)),
                  pl.BlockSpec(lb_logits.shape, const),
                  pl.BlockSpec((1, HG_HD), const)],
        out_specs=[pl.BlockSpec((1, NSA_HEADS, NSA_HD, tm), lambda j: (cur(j)[0], 0, 0, cur(j)[1])),
                   seg_spec, seg_spec, kv_spec, vt_spec, kv_spec, vt_spec,
                   pl.BlockSpec((1, GATE_ROWS, tm), lambda j: (cur(j)[0], 0, cur(j)[1])),
                   pl.BlockSpec((1, tm, HG_WIDTH), lambda j: (prev(j)[0], prev(j)[1], 0))],
        out_shape=[jax.ShapeDtypeStruct((bsz, NSA_HEADS, NSA_HD, t), BF16),
                   jax.ShapeDtypeStruct(seg_shape, F32),
                   jax.ShapeDtypeStruct(seg_shape, F32),
                   jax.ShapeDtypeStruct(kv_shape, BF16),
                   jax.ShapeDtypeStruct(vt_shape, BF16),
                   jax.ShapeDtypeStruct(kv_shape, BF16),
                   jax.ShapeDtypeStruct(vt_shape, BF16),
                   jax.ShapeDtypeStruct((bsz, GATE_ROWS, t), F32),
                   jax.ShapeDtypeStruct((bsz, t, HG_WIDTH), BF16)],
        scratch_shapes=[pltpu.VMEM((tm, 4 * HG_WIDTH), F32),
                        pltpu.VMEM((HG_HEADS, HG_HD, HG_HD), F32),
                        pltpu.VMEM((2, HG_HEADS, HG_CHUNK, HG_HD), F32),
                        pltpu.VMEM((tm, LANES), F32)],
        compiler_params=pltpu.CompilerParams(
            dimension_semantics=("arbitrary",), vmem_limit_bytes=VMEM_LIMIT),
        name="mix",
    )(x, mod, g1, w_in_p, pos_row, inv_t, qg_t, kg_t, lb_logits, hg_g)


def _compress_kernel(xk_ref, xv_ref, pe_ref, w1_ref, w2_ref, ko_ref, vo_ref):
    half = CMP_STRIDE * NSA_HD
    n_g, nseg = xk_ref.shape[1], xk_ref.shape[2]
    outs = []
    for kv, x_ref in enumerate((xk_ref, xv_ref)):
        x = x_ref[0].reshape(n_g * nseg, half)
        ha = _dot((x + pe_ref[kv, 0:1, :]).astype(BF16), w1_ref[kv, :half, :])
        hb = _dot((x + pe_ref[kv, 1:2, :]).astype(BF16), w1_ref[kv, half:, :])
        pre = ha + pltpu.roll(hb, n_g * nseg - 1, axis=0)
        outs.append(_dot(_silu(pre).astype(BF16), w2_ref[kv]))
    for g in range(n_g):
        rows = slice(g * nseg, (g + 1) * nseg)
        ko_ref[0, g] = outs[0][rows].astype(ko_ref.dtype)
        vo_ref[0, g] = outs[1][rows].T.astype(vo_ref.dtype)


def _compress(xk, xv, pe2, w1, w2):
    bsz, g, nseg, width = xk.shape
    x_spec = pl.BlockSpec((1, g, nseg, width), lambda b: (b, 0, 0, 0))
    return pl.pallas_call(
        _compress_kernel,
        grid=(bsz,),
        in_specs=[x_spec, x_spec,
                  pl.BlockSpec(pe2.shape, lambda b: (0, 0, 0)),
                  pl.BlockSpec(w1.shape, lambda b: (0, 0, 0)),
                  pl.BlockSpec(w2.shape, lambda b: (0, 0, 0))],
        out_specs=[pl.BlockSpec((1, g, nseg, NSA_HD), lambda b: (b, 0, 0, 0)),
                   pl.BlockSpec((1, g, NSA_HD, nseg), lambda b: (b, 0, 0, 0))],
        out_shape=[jax.ShapeDtypeStruct((bsz, g, nseg, NSA_HD), BF16),
                   jax.ShapeDtypeStruct((bsz, g, NSA_HD, nseg), BF16)],
        compiler_params=pltpu.CompilerParams(
            dimension_semantics=("arbitrary",), vmem_limit_bytes=VMEM_LIMIT),
        name="compress",
    )(xk, xv, pe2, w1, w2)


def _nsa_kernel(bound_ref, qt_ref, kc_ref, vct_ref, ks_ref, vst_ref, kw_ref, vwt_ref, gt_ref, mt_ref,
                o_ref, sel_ref, *, tq, tks, bounded):
    r = NSA_GROUP
    n_g = NSA_KV_HEADS
    ch = NSA_CHAIN
    lanes = ch * tq
    q0 = pl.program_id(1) * tq
    chains = [(g, g * r + c * ch) for g in range(n_g) for c in range(r // ch)]
    q_ts = [jnp.concatenate([qt_ref[0, h0 + i] for i in range(ch)], axis=1)
            for _, h0 in chains]

    def tile_heads(a):
        return jnp.concatenate([a] * ch, axis=1)

    n_blk = kc_ref.shape[2]
    nb = mt_ref.shape[0]
    blk_end = lax.broadcasted_iota(jnp.int32, (n_blk, tq), 0) * CMP_STRIDE + (CMP_BLOCK - 1)
    t_row = q0 + lax.broadcasted_iota(jnp.int32, (1, tq), 1)
    cvalid = tile_heads(jnp.where(blk_end <= t_row, 1.0, 0.0)) > 0.5
    some = tile_heads((t_row >= CMP_BLOCK - 1).astype(F32))
    j = lax.broadcasted_iota(jnp.int32, (nb, tq), 0)
    cur = jnp.right_shift(t_row, SLC_BLOCK.bit_length() - 1)
    forced = (j == 0) | (j == cur) | (j == cur - 1)
    o_cmp = []
    psum = [None] * n_g
    for (g, _), q_t in zip(chains, q_ts):
        s = jnp.where(cvalid, _dot(kc_ref[0, g], q_t), NEG)
        e = jnp.exp2(s - jnp.max(s, axis=0, keepdims=True))
        p = e * (some / jnp.sum(e, axis=0, keepdims=True))
        o_cmp.append(_dot(vct_ref[0, g], p.astype(BF16)))
        for i in range(ch):
            part = p[:, i * tq:(i + 1) * tq]
            psum[g] = part if psum[g] is None else psum[g] + part
    for g in range(n_g):
        p_hi = psum[g].astype(BF16)
        p_lo = (psum[g] - p_hi.astype(F32)).astype(BF16)
        imp = _dot(mt_ref[...], p_hi) + _dot(mt_ref[...], p_lo)
        imp = jnp.where(j <= cur, jnp.where(forced, jnp.inf, imp), -1.0)
        ranks = [jnp.zeros((nb, tq), jnp.int32) for _ in range(RANK_LANES)]
        for i in range(nb):
            row_i = imp[i:i + 1, :]
            ahead = (row_i > imp) | ((row_i == imp) & (j > i))
            ranks[i % RANK_LANES] = ranks[i % RANK_LANES] + ahead.astype(jnp.int32)
        rank = functools.reduce(lambda a, b: a + b, ranks)
        sel_ref[g] = jnp.where(rank < min(SLC_TOPK, nb), 0.0, NEG)

    shift = bound_ref[0] * LOG2E if bounded else None

    span = WIN + tq
    start = pl.multiple_of(jnp.maximum(q0 - WIN, 0), tq)
    dist = (lax.broadcasted_iota(jnp.int32, (span, tq), 1)
            - lax.broadcasted_iota(jnp.int32, (span, tq), 0)) + (q0 - start)
    wbias = jnp.where((dist >= 0) & (dist < WIN), 0.0, NEG)
    wbias = tile_heads(wbias - shift if bounded else wbias)
    o_win = []
    for (g, _), q_t in zip(chains, q_ts):
        sw = _dot(kw_ref[0, g, pl.ds(start, span), :], q_t) + wbias
        ew = jnp.exp2(sw if bounded else sw - jnp.max(sw, axis=0, keepdims=True))
        ow = _dot(vwt_ref[0, g, :, pl.ds(start, span)], ew.astype(BF16))
        o_win.append(ow[:NSA_HD] * (1.0 / ow[NSA_HD:NSA_HD + 1]))

    blocks_per_tile = tks // SLC_BLOCK
    rel = (lax.broadcasted_iota(jnp.int32, (tks, tq), 1)
           - lax.broadcasted_iota(jnp.int32, (tks, tq), 0))

    def slc_bias(it, k0, need_causal):
        biases = []
        for g in range(n_g):
            if isinstance(it, int):
                rows = sel_ref[g, it * blocks_per_tile:(it + 1) * blocks_per_tile, :]
            else:
                rows = sel_ref[g, pl.ds(pl.multiple_of(it * blocks_per_tile, blocks_per_tile), blocks_per_tile), :]
            bias = jnp.concatenate([jnp.broadcast_to(rows[jj:jj + 1], (SLC_BLOCK, tq))
                                    for jj in range(blocks_per_tile)], axis=0)
            if need_causal:
                bias = jnp.where(rel + (q0 - k0) >= 0, bias, NEG)
            biases.append(tile_heads(bias - shift if bounded else bias))
        return biases

    def slc_body(it, carry, need_causal=True):
        k0 = it * tks if isinstance(it, int) else pl.multiple_of(it * tks, tks)
        biases = slc_bias(it, k0, need_causal)
        out = []
        for (g, _), q_t, state in zip(chains, q_ts, carry):
            sc = _dot(ks_ref[0, g, pl.ds(k0, tks), :], q_t) + biases[g]
            vt = vst_ref[0, g, :, pl.ds(k0, tks)]
            if bounded:
                (acc,) = state
                acc = acc + _dot(vt, jnp.exp2(sc).astype(BF16))
                out.append((acc,))
            else:
                m, acc = state
                m_new = jnp.maximum(m, jnp.max(sc, axis=0, keepdims=True))
                acc = jnp.exp2(m - m_new) * acc + _dot(vt, jnp.exp2(sc - m_new).astype(BF16))
                out.append((m_new, acc))
        return tuple(out)

    zero_state = (jnp.zeros((VT_ROWS, lanes), F32),)
    init = tuple(zero_state if bounded else (jnp.full((1, lanes), NEG, F32),) + zero_state for _ in chains)
    n_it = (q0 + tq + tks - 1) // tks
    if bounded:
        def slc_tiles(n):
            carry = init
            for it in range(n):
                carry = slc_body(it, carry, need_causal=(it == n - 1))
            return carry

        max_it = ks_ref.shape[2] // tks
        fin = lax.switch(n_it - 1, [functools.partial(slc_tiles, n) for n in range(1, max_it + 1)])
    else:
        fin = lax.fori_loop(0, n_it, slc_body, init)
    o_slc = [st[-1][:NSA_HD] * (1.0 / st[-1][NSA_HD:NSA_HD + 1]) for st in fin]

    heads_per_store = LANES // NSA_HD
    for ci, (_, h0) in enumerate(chains):
        for i0 in range(0, ch, heads_per_store):
            gated = []
            for i in range(i0, i0 + heads_per_store):
                cols = slice(i * tq, (i + 1) * tq)
                o_h = jnp.zeros((NSA_HD, tq), F32)
                for br, o_b in enumerate((o_cmp[ci], o_slc[ci], o_win[ci])):
                    row = (h0 + i) * N_BRANCH + br
                    o_h = o_h + gt_ref[0, row:row + 1, :] * o_b[:, cols]
                gated.append(o_h)
            lo = (h0 + i0) * NSA_HD
            o_ref[0, :, lo:lo + LANES] = jnp.concatenate(gated, axis=0).T.astype(o_ref.dtype)


def _nsa(q_t, kc, vct, ks, vst, kw, vwt, gates_t, mt, bound, tq, tks, bounded):
    bsz, _, _, t = q_t.shape
    n_blk = kc.shape[2]
    n_g = NSA_KV_HEADS
    full = lambda b, i: (b, 0, 0, 0)
    k_spec = pl.BlockSpec((1, n_g, t, NSA_HD), full)
    vt_spec = pl.BlockSpec((1, n_g, VT_ROWS, t), full)
    return pl.pallas_call(
        functools.partial(_nsa_kernel, tq=tq, tks=tks, bounded=bounded),
        grid=(bsz, t // tq),
        in_specs=[pl.BlockSpec(memory_space=pltpu.SMEM),
                  pl.BlockSpec((1, NSA_HEADS, NSA_HD, tq), lambda b, i: (b, 0, 0, i)),
                  pl.BlockSpec((1, n_g, n_blk, NSA_HD), full),
                  pl.BlockSpec((1, n_g, NSA_HD, n_blk), full),
                  k_spec, vt_spec, k_spec, vt_spec,
                  pl.BlockSpec((1, GATE_ROWS, tq), lambda b, i: (b, 0, i)),
                  pl.BlockSpec(mt.shape, lambda b, i: (0, 0))],
        out_specs=pl.BlockSpec((1, tq, NSA_WIDTH), lambda b, i: (b, i, 0)),
        out_shape=jax.ShapeDtypeStruct((bsz, t, NSA_WIDTH), BF16),
        scratch_shapes=[pltpu.VMEM((n_g, mt.shape[0], tq), F32)],
        compiler_params=pltpu.CompilerParams(
            dimension_semantics=("arbitrary", "arbitrary"), vmem_limit_bytes=VMEM_LIMIT),
        name="nsa_bounded" if bounded else "nsa",
    )(bound, q_t, kc, vct, ks, vst, kw, vwt, gates_t, mt)


def _causal_conv(u, prev, cw, cb):
    tm = u.shape[0]
    g = SUBLANES
    r8 = lax.broadcasted_iota(jnp.int32, (g, u.shape[1]), 0)
    wrap1 = jnp.where(r8 == 0, prev[2 * g - 1:2 * g], pltpu.roll(u[tm - g:], 1, axis=0))
    wrap2 = jnp.where(r8 == 0, prev[g - 1:g], pltpu.roll(u[tm - 2 * g:tm - g], 1, axis=0))
    u1 = jnp.concatenate([wrap1, u[:tm - g]], axis=0)
    u2 = jnp.concatenate([wrap2, wrap1, u[:tm - 2 * g]], axis=0)
    return cb + u2 * cw[0:1] + u1 * cw[1:2] + u * cw[2:3]


def _ffn_kernel(x_ref, hg_ref, ns_ref, mod_ref, wo_ref, g2_ref, wu_ref, cw_ref, cb_ref, wd_ref,
                o_ref, carry_ref, buf_ref, x1_ref, h2_ref, u_ref, g_ref, acc_ref, *, fc):
    n_sub, sub = x1_ref.shape[0], x1_ref.shape[1]
    n_lt = D_MODEL // LANES
    groups = sub // SUBLANES
    pitch = buf_ref.shape[2] // SUBLANES
    first = pl.program_id(1) == 0

    def head(s):
        rows = slice(s * sub, (s + 1) * sub)
        mix = _dot(hg_ref[0, rows], wo_ref[:HG_WIDTH, :]) + _dot(ns_ref[0, rows], wo_ref[HG_WIDTH:, :])
        x1_nat = x_ref[0, rows] + mod_ref[0, 2:3, :] * mix
        for c in range(n_lt):
            for sg in range(SUBLANES):
                buf_ref[s, c, sg * pitch:sg * pitch + groups] = x1_nat[sg * groups:(sg + 1) * groups,
                                                                       c * LANES:(c + 1) * LANES]
        x1 = jnp.concatenate(
            [jnp.concatenate([buf_ref[s, c, pl.ds(j, SUBLANES, stride=pitch), :] for j in range(groups)], axis=0)
             for c in range(n_lt)], axis=1)
        x1_ref[s] = x1
        y = x1 * lax.rsqrt(jnp.mean(x1 * x1, axis=-1, keepdims=True) + EPS) * g2_ref[...]
        h2_ref[s] = (y * (1.0 + mod_ref[0, 4:5, :]) + mod_ref[0, 3:4, :]).astype(h2_ref.dtype)
        acc_ref[s] = jnp.zeros((sub, D_MODEL), F32)

    def tail(s):
        out = x1_ref[s] + mod_ref[0, 5:6, :] * acc_ref[s]
        for c in range(n_lt):
            for j in range(groups):
                buf_ref[s, c, pl.ds(j, SUBLANES, stride=pitch), :] = out[j * SUBLANES:(j + 1) * SUBLANES,
                                                                         c * LANES:(c + 1) * LANES]
        for c in range(n_lt):
            for sg in range(SUBLANES):
                o_ref[0, s * sub + sg * groups:s * sub + (sg + 1) * groups, c * LANES:(c + 1) * LANES] = (
                    buf_ref[s, c, sg * pitch:sg * pitch + groups])

    n = D_FF // fc

    def up(c, slot):
        s, jc = divmod(c, n)
        for half in range(2):
            off = half * D_FF + jc * fc
            u_ref[slot, half] = _dot(h2_ref[s], wu_ref[:, off:off + fc])

    def act(c, slot):
        s, jc = divmod(c, n)
        halves = []
        for half in range(2):
            cols = slice(half * D_FF + jc * fc, half * D_FF + (jc + 1) * fc)
            u = u_ref[slot, half]
            prev = jnp.where(first, 0.0, carry_ref[:, cols]) if s == 0 else carry_ref[:, cols]
            carry_ref[:, cols] = u[sub - 2 * SUBLANES:]
            halves.append(_causal_conv(u, prev, cw_ref[:, cols], cb_ref[:, cols]))
        a, v = halves
        g_ref[slot] = (_silu(a) * v).astype(g_ref.dtype)

    def down(c, slot):
        s, jc = divmod(c, n)
        acc_ref[s] += _dot(g_ref[slot], wd_ref[jc * fc:(jc + 1) * fc, :])

    total = n_sub * n
    head(0)
    for i in range(total + 2):
        if i < total:
            up(i, i % 2)
        if 0 <= i - 1 < total:
            act(i - 1, (i - 1) % 2)
        if 0 <= i - 2 < total:
            down(i - 2, i % 2)
        s, jc = divmod(i, n)
        if jc == HEAD_AT and s + 1 < n_sub:
            head(s + 1)
        if jc == 1 and 1 <= s <= n_sub:
            tail(s - 1)


def _ffn(x, o_hg, o_nsa, mod, w_out, g2, w_up, conv_w, conv_b, w_down, tm, sub, fc):
    bsz, t, _ = x.shape
    row_spec = lambda w: pl.BlockSpec((1, tm, w), lambda b, i: (b, i, 0))
    groups = sub // SUBLANES
    pad = SUBLANES if (groups // SUBLANES) % 2 == 0 else 0
    n_sub = tm // sub
    resident = lambda a: pl.BlockSpec(a.shape, lambda b, i: (0, 0), pipeline_mode=pl.Buffered(1))
    return pl.pallas_call(
        functools.partial(_ffn_kernel, fc=fc),
        grid=(bsz, t // tm),
        in_specs=[row_spec(D_MODEL), row_spec(HG_WIDTH), row_spec(NSA_WIDTH),
                  pl.BlockSpec((1, 6, D_MODEL), lambda b, i: (b, 0, 0)),
                  resident(w_out), resident(g2), resident(w_up), resident(conv_w), resident(conv_b),
                  resident(w_down)],
        out_specs=row_spec(D_MODEL),
        out_shape=jax.ShapeDtypeStruct(x.shape, F32),
        scratch_shapes=[pltpu.VMEM((2 * SUBLANES, 2 * D_FF), F32),
                        pltpu.VMEM((n_sub, D_MODEL // LANES, sub + SUBLANES * pad, LANES), F32),
                        pltpu.VMEM((n_sub, sub, D_MODEL), F32),
                        pltpu.VMEM((n_sub, sub, D_MODEL), BF16),
                        pltpu.VMEM((2, 2, sub, fc), F32),
                        pltpu.VMEM((2, sub, fc), BF16),
                        pltpu.VMEM((n_sub, sub, D_MODEL), F32)],
        compiler_params=pltpu.CompilerParams(
            dimension_semantics=("arbitrary", "arbitrary"), vmem_limit_bytes=VMEM_LIMIT),
        name="ffn",
    )(x, o_hg, o_nsa, mod, w_out, g2, w_up, conv_w, conv_b, w_down)


def _rope_tables():
    half = ROPE_DIM // 2
    inv = ROPE_THETA ** (-jnp.arange(half, dtype=F32) * 2.0 / ROPE_DIM)
    return (jnp.tile(inv.reshape(half, 1), (1, LANES)),)


def _gain_t(g):
    return jnp.tile(g.reshape(NSA_HD, 1), (LANES // NSA_HD, LANES))


def _selection_tables(t):
    n_seg = t // CMP_STRIDE
    nb = t // SLC_BLOCK
    cst = np.arange(n_seg) * CMP_STRIDE
    sst = np.arange(nb) * SLC_BLOCK
    ovl = np.clip(np.minimum(cst[:, None] + CMP_BLOCK, sst[None] + SLC_BLOCK)
                  - np.maximum(cst[:, None], sst[None]), 0, None) / CMP_BLOCK
    ovl[(t - CMP_BLOCK) // CMP_STRIDE + 1:] = 0.0
    return (jnp.asarray(ovl.T, dtype=BF16),)


def _layer(x, mod, pos_row, l, p, tables):
    bsz, t, _ = x.shape
    inv_t, mt = tables
    w_in_p = p["w_in"][l].astype(BF16)
    qg_t = _gain_t(p["q_norm_g"][l])
    kg_t = jnp.stack([_gain_t(p["k_norm_g"][l, br]) for br in range(N_BRANCH)])
    q_t, kc, vc, ks, vst, kw, vwt, gates_t, o_hg = _mix(
        x, mod, p["norm1_g"][l].reshape(1, D_MODEL), w_in_p, pos_row, inv_t, qg_t, kg_t,
        p["lb_logits"], p["hg_norm_g"][l].reshape(1, HG_HD), l, tm=TM_MIX)

    pe2 = p["pe_cmp"][l].reshape(2, 2, CMP_STRIDE * NSA_HD)
    kcmp, vcmp_t = _compress(kc, vc, pe2, p["w_cmp1"][l].astype(BF16), p["w_cmp2"][l].astype(BF16))
    bound = (SCORE_BOUND_MARGIN * NSA_HD ** 0.5 * jnp.max(jnp.abs(p["q_norm_g"][l]))
             * jnp.max(jnp.abs(p["k_norm_g"][l, 1:]))).reshape(1).astype(F32)
    nsa_args = (q_t, kcmp, vcmp_t, ks, vst, kw, vwt, gates_t, mt, bound)
    o_nsa = lax.cond(bound[0] <= MAX_SCORE_BOUND,
                     lambda a: _nsa(*a, tq=TQ_NSA, tks=TKS_NSA, bounded=True),
                     lambda a: _nsa(*a, tq=TQ_NSA, tks=TKS_NSA, bounded=False), nsa_args)

    return _ffn(x, o_hg, o_nsa, mod, p["w_out"][l].astype(BF16), p["norm2_g"][l].reshape(1, D_MODEL),
                p["w_up"][l].astype(BF16), p["conv_w"][l], p["conv_b"][l].reshape(1, 2 * D_FF),
                p["w_down"][l].astype(BF16), tm=TM_FFN, sub=SUB_FFN, fc=FC_FFN)


def kernel(x, c, positions, w_ada, b_ada, norm1_g, w_in, lb_logits, hg_norm_g, q_norm_g, k_norm_g, pe_cmp, w_cmp1, w_cmp2, w_out, norm2_g, w_up, conv_w, conv_b, w_down):
    p = dict(w_in=w_in, norm1_g=norm1_g, lb_logits=lb_logits, hg_norm_g=hg_norm_g, q_norm_g=q_norm_g,
             k_norm_g=k_norm_g, pe_cmp=pe_cmp, w_cmp1=w_cmp1, w_cmp2=w_cmp2, w_out=w_out,
             norm2_g=norm2_g, w_up=w_up, conv_w=conv_w, conv_b=conv_b, w_down=w_down)
    bsz, t, _ = x.shape
    assert x.shape[2] == D_MODEL and t % TM_MIX == 0 and t % TKS_NSA == 0 and t >= WIN + TQ_NSA
    assert TKS_NSA % TQ_NSA == 0 and TM_MIX % HG_CHUNK == 0 and D_FF % FC_FFN == 0 and TM_FFN % SUB_FFN == 0
    tables = _rope_tables() + _selection_tables(t)
    pos_row = positions.reshape(bsz, 1, t)
    for l in range(w_ada.shape[0]):
        mod = _ada(c, w_ada[l], b_ada[l]).reshape(bsz, 6, D_MODEL)
        x = _layer(x, mod, pos_row, l, p, tables)
    return x
```

```python
import functools

import jax
import jax.numpy as jnp
import numpy as np
from jax import lax
from jax.experimental import pallas as pl
from jax.experimental.pallas import tpu as pltpu

D_MODEL = 1024
HG_HEADS = 4
HG_HD = 128
HG_WIDTH = HG_HEADS * HG_HD
HG_CHUNK = 128
HG_SUB = 8
LOG2E = 1.4426950408889634
NSA_HEADS = 8
NSA_KV_HEADS = 2
NSA_HD = 64
NSA_GROUP = NSA_HEADS // NSA_KV_HEADS
NSA_CHAIN = 4
RANK_LANES = 4
NSA_WIDTH = NSA_HEADS * NSA_HD
N_BRANCH = 3
CMP_BLOCK = 32
CMP_STRIDE = 16
CMP_HIDDEN = 256
SLC_BLOCK = 64
SLC_TOPK = 16
WIN = 512
ROPE_DIM = NSA_HD // 4
ROPE_THETA = 500000.0
D_FF = 2816
CONV_W = 3
EPS = 1e-6
NEG = -1e30
SCORE_BOUND_MARGIN = 1.02
MAX_SCORE_BOUND = 40.0

LANES = 128
SUBLANES = 8
VMEM_LIMIT = 56 * 1024 * 1024

TM_MIX = 512
TQ_NSA = 256
TKS_NSA = 512
TM_FFN = 512
SUB_FFN = 256
HEAD_AT = 7
FC_FFN = 256

OFF_HG = 0
OFF_Q = 4 * HG_WIDTH
OFF_KV = OFF_Q + NSA_WIDTH
OFF_G = OFF_KV + 6 * NSA_KV_HEADS * NSA_HD
IN_COLS = OFF_G + N_BRANCH * NSA_HEADS
GATE_ROWS = N_BRANCH * NSA_HEADS
VT_ROWS = NSA_HD + SUBLANES

BF16 = jnp.bfloat16
F32 = jnp.float32


def _dot(a, b):
    return jnp.dot(a, b, preferred_element_type=F32)


def _dot_nt(a, b):
    return lax.dot_general(a, b, (((1,), (1,)), ((), ())), preferred_element_type=F32)


def _exp_neg(x):
    return jnp.exp2(x * (-LOG2E))


def _sigmoid(x):
    return 1.0 / (1.0 + _exp_neg(x))


def _silu(x):
    return x * _sigmoid(x)


def _ada_kernel(c_ref, w_ref, b_ref, o_ref):
    cs = _silu(c_ref[...])
    o_ref[...] = _dot(cs.astype(BF16), w_ref[...].astype(BF16)) + b_ref[...]


def _ada(c, w, b):
    bsz = c.shape[0]
    n = w.shape[1]
    tn = D_MODEL
    return pl.pallas_call(
        _ada_kernel,
        grid=(n // tn,),
        in_specs=[pl.BlockSpec((bsz, D_MODEL), lambda j: (0, 0)),
                  pl.BlockSpec((D_MODEL, tn), lambda j: (0, j)),
                  pl.BlockSpec((1, tn), lambda j: (0, j))],
        out_specs=pl.BlockSpec((bsz, tn), lambda j: (0, j)),
        out_shape=jax.ShapeDtypeStruct((bsz, n), F32),
        name="ada",
    )(c, w, b.reshape(1, n))


def _pair_norm_rope_t(xt, g_t, cos_t, sin_t):
    half = ROPE_DIM // 2
    outs = []
    for hh in range(2):
        x = xt[hh * NSA_HD:(hh + 1) * NSA_HD]
        ms = jnp.mean(x * x, axis=0, keepdims=True)
        xn = x * lax.rsqrt(ms + EPS) * g_t[hh * NSA_HD:(hh + 1) * NSA_HD]
        x1, x2 = xn[:half], xn[half:ROPE_DIM]
        outs += [x1 * cos_t - x2 * sin_t, x2 * cos_t + x1 * sin_t, xn[ROPE_DIM:]]
    return jnp.concatenate(outs, axis=0)


def _project_parts(x_ref, mod_ref, g1_ref, w_ref, pos_ref, inv_ref, qg_ref, kg_ref,
                   zhg_out, qt_ref, kc_ref, vc_ref, ks_ref, vst_ref, kw_ref, vwt_ref, gt_ref, seg_ref):
    x = x_ref[0]
    ms = jnp.mean(x * x, axis=-1, keepdims=True)
    y = x * lax.rsqrt(ms + EPS) * g1_ref[...]
    h = (y * (1.0 + mod_ref[0, 1:2, :]) + mod_ref[0, 0:1, :]).astype(BF16)

    tm = x.shape[0]
    reps = tm // LANES

    def lane_tile(a):
        return jnp.concatenate([a] * reps, axis=1)

    ang = lane_tile(inv_ref[...]) * pos_ref[0].astype(F32)
    cos_t = jnp.cos(ang)
    sin_t = jnp.sin(ang)
    scale = NSA_HD ** -0.5 * LOG2E
    vals = {}

    def hg_cols(lo, hi):
        def part():
            zhg_out[:, lo:hi] = _dot(h, w_ref[:, OFF_HG + lo:OFF_HG + hi])
        return part

    def q_pair(p):
        def part():
            if p % 2 == 0:
                vals["zq"] = _dot(h, w_ref[:, OFF_Q + p * LANES:OFF_Q + (p + 2) * LANES])
                vals["qg"] = lane_tile(qg_ref[...])
            off = (p % 2) * LANES
            rt = _pair_norm_rope_t(vals["zq"][:, off:off + LANES].T, vals["qg"], cos_t, sin_t)
            rt = (rt * scale).astype(qt_ref.dtype)
            qt_ref[0, 2 * p] = rt[:NSA_HD]
            qt_ref[0, 2 * p + 1] = rt[NSA_HD:]
        return part

    def kv_dot(br):
        vals["zkv"] = _dot(h, w_ref[:, OFF_KV + 2 * br * LANES:OFF_KV + (2 * br + 2) * LANES])

    def store_segments(a, out_ref):
        seg_ref[...] = a
        n_seg = tm // CMP_STRIDE
        lane = lax.broadcasted_iota(jnp.int32, (n_seg, LANES), 1)
        for l in range(0, CMP_STRIDE, 2):
            even = seg_ref[pl.ds(l, n_seg, stride=CMP_STRIDE), :]
            odd = seg_ref[pl.ds(l + 1, n_seg, stride=CMP_STRIDE), :]
            cols = slice((l // 2) * LANES, (l // 2 + 1) * LANES)
            out_ref[0, 0, :, cols] = jnp.where(lane < NSA_HD, even, pltpu.roll(odd, NSA_HD, axis=1))
            out_ref[0, 1, :, cols] = jnp.where(lane < NSA_HD, pltpu.roll(even, NSA_HD, axis=1), odd)

    def key(br, k_ref):
        def part():
            kv_dot(br)
            kt = _pair_norm_rope_t(vals["zkv"][:, :LANES].T, lane_tile(kg_ref[br]), cos_t, sin_t)
            if br == 0:
                store_segments(kt.T, k_ref)
                store_segments(vals["zkv"][:, LANES:], vc_ref)
            else:
                kk = kt.T.astype(k_ref.dtype)
                for g in range(NSA_KV_HEADS):
                    k_ref[0, g] = kk[:, g * NSA_HD:(g + 1) * NSA_HD]
        return part

    def value_t(vt_ref):
        def part():
            vt = vals["zkv"][:, LANES:].T.astype(vt_ref.dtype)
            ones_row = jnp.where(lax.broadcasted_iota(jnp.int32, (SUBLANES, tm), 0) == 0, 1.0, 0.0).astype(vt_ref.dtype)
            for g in range(NSA_KV_HEADS):
                vt_ref[0, g, :NSA_HD] = vt[g * NSA_HD:(g + 1) * NSA_HD]
                vt_ref[0, g, NSA_HD:] = ones_row
        return part

    def gates_part():
        gates = _sigmoid(_dot(h, w_ref[:, OFF_G:IN_COLS]))
        wide = jnp.concatenate([gates, jnp.zeros((tm, LANES - GATE_ROWS), F32)], axis=1)
        gt_ref[0] = wide.T[:GATE_ROWS]

    wide = 2 * LANES
    hg = [hg_cols(lo, lo + wide) for lo in range(0, 4 * HG_WIDTH, wide)]
    rest = [q_pair(0), q_pair(1), q_pair(2), q_pair(3), key(0, kc_ref), key(1, ks_ref), value_t(vst_ref),
            key(2, kw_ref), value_t(vwt_ref), gates_part]
    parts = []
    for k in range(max(len(hg), len(rest))):
        parts += hg[k:k + 1] + rest[k:k + 1]
    return parts


def _hgrn_chunk(zq, zf, zi, zg, lb, hg_g, st_ref, bk_ref, h, tri, level):
    c = HG_CHUNK
    e_z = _exp_neg(jnp.abs(zf))
    logsig = jnp.minimum(zf, 0.0) - jnp.log(1.0 + e_z)
    a = jnp.log(lb)
    bb = jnp.log1p(-lb) + logsig
    logf = jnp.maximum(a, bb) + jnp.log(1.0 + _exp_neg(jnp.abs(a - bb)))
    k = (1.0 - lb) * (jnp.where(zf >= 0.0, e_z, 1.0) / (1.0 + e_z))
    q = _silu(zq)
    v = zi
    l_hi = logf.astype(BF16)
    l_mid = (logf - l_hi.astype(F32)).astype(BF16)
    l_lo = (logf - l_hi.astype(F32) - l_mid.astype(F32)).astype(BF16)
    bc = _dot(tri, l_hi) + _dot(tri, l_mid) + _dot(tri, l_lo)

    col = lax.broadcasted_iota(jnp.int32, (HG_SUB, c), 1)
    b2 = bc * LOG2E
    bk_ref[0, h] = b2
    bk_ref[1, h] = k
    rows_a = []
    for i in range(c // HG_SUB):
        lo = i * HG_SUB
        b_i = b2[lo:lo + HG_SUB]
        q_i = q[lo:lo + HG_SUB]
        a_i = jnp.zeros((HG_SUB, c), F32)
        for s in range(HG_SUB):
            b_s = bk_ref[0, h, lo + s:lo + s + 1, :]
            k_s = bk_ref[1, h, lo + s:lo + s + 1, :]
            e = jnp.exp2(jnp.minimum(b_i - b_s, 0.0)) * q_i * k_s
            a_i = jnp.where(col == lo + s, jnp.sum(e, axis=-1, keepdims=True), a_i)
        rows_a.append(a_i)
    amat = jnp.concatenate(rows_a, axis=0)

    size, idx = c // 2, 1
    while size >= HG_SUB:
        pieces = []
        for e0 in range(0, c, 2 * size):
            o0 = e0 + size
            r = b2[o0:o0 + 1]
            pieces.append(k[e0:o0] * jnp.exp2(r - b2[e0:o0]))
            pieces.append(q[o0:o0 + size] * jnp.exp2(b2[o0:o0 + size] - r))
        hmat = jnp.concatenate(pieces, axis=0).astype(BF16)
        amat = jnp.where(level == idx, _dot_nt(hmat, hmat), amat)
        size, idx = size // 2, idx + 1
    amat = jnp.where(level >= 0, amat, 0.0)

    st = st_ref[h]
    o = _dot_nt((q * jnp.exp2(b2)).astype(BF16), st.astype(BF16)) + _dot(amat.astype(BF16), v.astype(BF16))
    bl = b2[c - 1:c]
    kdec = (k * jnp.exp2(bl - b2)).astype(BF16)
    st_ref[h] = jnp.exp2(bl) * st + _dot(v.T.astype(BF16), kdec)

    y = o * lax.rsqrt(jnp.mean(o * o, axis=-1, keepdims=True) + EPS) * hg_g
    return y * _silu(zg)


def _hgrn_parts(z_in, lbl_ref, g_ref, o_ref, st_ref, bk_ref, l_idx):
    lg = lbl_ref[...]
    ex = jnp.exp(lg - jnp.max(lg, axis=0, keepdims=True))
    sm = ex / jnp.sum(ex, axis=0, keepdims=True)
    lb_all = jnp.sum(sm[:l_idx + 1], axis=0, keepdims=True)

    c = HG_CHUNK
    ri = lax.broadcasted_iota(jnp.int32, (c, c), 0)
    ci = lax.broadcasted_iota(jnp.int32, (c, c), 1)
    tri = jnp.where(ci <= ri, 1.0, 0.0).astype(BF16)
    level = jnp.where(ci > ri, -1, 0)
    size, idx = c // 2, 1
    while size >= HG_SUB:
        sh = size.bit_length() - 1
        paired = ((ri >> (sh + 1)) == (ci >> (sh + 1))) & ((ri >> sh) != (ci >> sh)) & (ci <= ri)
        level = jnp.where(paired, idx, level)
        size, idx = size // 2, idx + 1

    def chunk_head(ch, h):
        def part():
            rows = slice(ch * c, (ch + 1) * c)
            sl = slice(h * HG_HD, (h + 1) * HG_HD)
            zq = z_in[rows, h * HG_HD:(h + 1) * HG_HD]
            zf = z_in[rows, HG_WIDTH + h * HG_HD:HG_WIDTH + (h + 1) * HG_HD]
            zi = z_in[rows, 2 * HG_WIDTH + h * HG_HD:2 * HG_WIDTH + (h + 1) * HG_HD]
            zg = z_in[rows, 3 * HG_WIDTH + h * HG_HD:3 * HG_WIDTH + (h + 1) * HG_HD]
            y = _hgrn_chunk(zq, zf, zi, zg, lb_all[:, sl], g_ref[...], st_ref, bk_ref, h, tri, level)
            o_ref[0, rows, h * HG_HD:(h + 1) * HG_HD] = y.astype(o_ref.dtype)
        return part

    return [chunk_head(ch, h) for ch in range(z_in.shape[0] // c) for h in range(HG_HEADS)]


def _mix_kernel(x_ref, mod_ref, g1_ref, w_ref, pos_ref, inv_ref, qg_ref, kg_ref, lbl_ref, hgg_ref,
                qt_ref, kc_ref, vc_ref, ks_ref, vst_ref, kw_ref, vwt_ref, gt_ref, ohg_ref,
                zbuf_ref, st_ref, bk_ref, seg_ref, *, l_idx, tiles_per_seq):
    j = pl.program_id(0)

    @pl.when(j == 0)
    def _():
        zbuf_ref[...] = jnp.zeros_like(zbuf_ref)

    @pl.when((j == 0) | (lax.rem(j + tiles_per_seq - 1, tiles_per_seq) == 0))
    def _():
        st_ref[...] = jnp.zeros_like(st_ref)

    rec = _hgrn_parts(zbuf_ref, lbl_ref, hgg_ref, ohg_ref, st_ref, bk_ref, l_idx)
    proj = _project_parts(x_ref, mod_ref, g1_ref, w_ref, pos_ref, inv_ref, qg_ref, kg_ref, zbuf_ref,
                          qt_ref, kc_ref, vc_ref, ks_ref, vst_ref, kw_ref, vwt_ref, gt_ref, seg_ref)
    for part in rec + proj:
        part()


def _mix(x, mod, g1, w_in_p, pos_row, inv_t, qg_t, kg_t, lb_logits, hg_g, l_idx, tm):
    bsz, t, _ = x.shape
    nm = t // tm
    n_tiles = bsz * nm

    def cur(j):
        jc = jnp.minimum(j, n_tiles - 1)
        return jc // nm, jc % nm

    def prev(j):
        jp = jnp.maximum(j - 1, 0)
        return jp // nm, jp % nm

    kv_shape = (bsz, NSA_KV_HEADS, t, NSA_HD)
    kv_spec = pl.BlockSpec((1, NSA_KV_HEADS, tm, NSA_HD), lambda j: (cur(j)[0], 0, cur(j)[1], 0))
    seg_w = CMP_STRIDE * NSA_HD
    seg_shape = (bsz, NSA_KV_HEADS, t // CMP_STRIDE, seg_w)
    seg_spec = pl.BlockSpec((1, NSA_KV_HEADS, tm // CMP_STRIDE, seg_w), lambda j: (cur(j)[0], 0, cur(j)[1], 0))
    vt_shape = (bsz, NSA_KV_HEADS, VT_ROWS, t)
    vt_spec = pl.BlockSpec((1, NSA_KV_HEADS, VT_ROWS, tm), lambda j: (cur(j)[0], 0, 0, cur(j)[1]))
    const = lambda j: (0, 0)
    return pl.pallas_call(
        functools.partial(_mix_kernel, l_idx=l_idx, tiles_per_seq=nm),
        grid=(n_tiles + 1,),
        in_specs=[pl.BlockSpec((1, tm, D_MODEL), lambda j: (cur(j)[0], cur(j)[1], 0)),
                  pl.BlockSpec((1, 6, D_MODEL), lambda j: (cur(j)[0], 0, 0)),
                  pl.BlockSpec((1, D_MODEL), const),
                  pl.BlockSpec((D_MODEL, IN_COLS), const),
                  pl.BlockSpec((1, 1, tm), lambda j: (cur(j)[0], 0, cur(j)[1])),
                  pl.BlockSpec(inv_t.shape, const),
                  pl.BlockSpec(qg_t.shape, const),
                  pl.BlockSpec(kg_t.shape, lambda j: (0, 0, 0)),
                  pl.BlockSpec(lb_logits.shape, const),
                  pl.BlockSpec((1, HG_HD), const)],
        out_specs=[pl.BlockSpec((1, NSA_HEADS, NSA_HD, tm), lambda j: (cur(j)[0], 0, 0, cur(j)[1])),
                   seg_spec, seg_spec, kv_spec, vt_spec, kv_spec, vt_spec,
                   pl.BlockSpec((1, GATE_ROWS, tm), lambda j: (cur(j)[0], 0, cur(j)[1])),
                   pl.BlockSpec((1, tm, HG_WIDTH), lambda j: (prev(j)[0], prev(j)[1], 0))],
        out_shape=[jax.ShapeDtypeStruct((bsz, NSA_HEADS, NSA_HD, t), BF16),
                   jax.ShapeDtypeStruct(seg_shape, F32),
                   jax.ShapeDtypeStruct(seg_shape, F32),
                   jax.ShapeDtypeStruct(kv_shape, BF16),
                   jax.ShapeDtypeStruct(vt_shape, BF16),
                   jax.ShapeDtypeStruct(kv_shape, BF16),
                   jax.ShapeDtypeStruct(vt_shape, BF16),
                   jax.ShapeDtypeStruct((bsz, GATE_ROWS, t), F32),
                   jax.ShapeDtypeStruct((bsz, t, HG_WIDTH), BF16)],
        scratch_shapes=[pltpu.VMEM((tm, 4 * HG_WIDTH), F32),
                        pltpu.VMEM((HG_HEADS, HG_HD, HG_HD), F32),
                        pltpu.VMEM((2, HG_HEADS, HG_CHUNK, HG_HD), F32),
                        pltpu.VMEM((tm, LANES), F32)],
        compiler_params=pltpu.CompilerParams(
            dimension_semantics=("arbitrary",), vmem_limit_bytes=VMEM_LIMIT),
        name="mix",
    )(x, mod, g1, w_in_p, pos_row, inv_t, qg_t, kg_t, lb_logits, hg_g)


def _compress_kernel(xk_ref, xv_ref, pe_ref, w1_ref, w2_ref, ko_ref, vo_ref):
    half = CMP_STRIDE * NSA_HD
    n_g, nseg = xk_ref.shape[1], xk_ref.shape[2]
    outs = []
    for kv, x_ref in enumerate((xk_ref, xv_ref)):
        x = x_ref[0].reshape(n_g * nseg, half)
        ha = _dot((x + pe_ref[kv, 0:1, :]).astype(BF16), w1_ref[kv, :half, :])
        hb = _dot((x + pe_ref[kv, 1:2, :]).astype(BF16), w1_ref[kv, half:, :])
        pre = ha + pltpu.roll(hb, n_g * nseg - 1, axis=0)
        outs.append(_dot(_silu(pre).astype(BF16), w2_ref[kv]))
    for g in range(n_g):
        rows = slice(g * nseg, (g + 1) * nseg)
        ko_ref[0, g] = outs[0][rows].astype(ko_ref.dtype)
        vo_ref[0, g] = outs[1][rows].T.astype(vo_ref.dtype)


def _compress(xk, xv, pe2, w1, w2):
    bsz, g, nseg, width = xk.shape
    x_spec = pl.BlockSpec((1, g, nseg, width), lambda b: (b, 0, 0, 0))
    return pl.pallas_call(
        _compress_kernel,
        grid=(bsz,),
        in_specs=[x_spec, x_spec,
                  pl.BlockSpec(pe2.shape, lambda b: (0, 0, 0)),
                  pl.BlockSpec(w1.shape, lambda b: (0, 0, 0)),
                  pl.BlockSpec(w2.shape, lambda b: (0, 0, 0))],
        out_specs=[pl.BlockSpec((1, g, nseg, NSA_HD), lambda b: (b, 0, 0, 0)),
                   pl.BlockSpec((1, g, NSA_HD, nseg), lambda b: (b, 0, 0, 0))],
        out_shape=[jax.ShapeDtypeStruct((bsz, g, nseg, NSA_HD), BF16),
                   jax.ShapeDtypeStruct((bsz, g, NSA_HD, nseg), BF16)],
        compiler_params=pltpu.CompilerParams(
            dimension_semantics=("arbitrary",), vmem_limit_bytes=VMEM_LIMIT),
        name="compress",
    )(xk, xv, pe2, w1, w2)


def _nsa_kernel(bound_ref, qt_ref, kc_ref, vct_ref, ks_ref, vst_ref, kw_ref, vwt_ref, gt_ref, mt_ref,
                o_ref, sel_ref, *, tq, tks, bounded):
    r = NSA_GROUP
    n_g = NSA_KV_HEADS
    ch = NSA_CHAIN
    lanes = ch * tq
    q0 = pl.program_id(1) * tq
    chains = [(g, g * r + c * ch) for g in range(n_g) for c in range(r // ch)]
    q_ts = [jnp.concatenate([qt_ref[0, h0 + i] for i in range(ch)], axis=1)
            for _, h0 in chains]

    def tile_heads(a):
        return jnp.concatenate([a] * ch, axis=1)

    shift = bound_ref[0] * LOG2E if bounded else None

    span = WIN + tq
    start = pl.multiple_of(jnp.maximum(q0 - WIN, 0), tq)
    dist = (lax.broadcasted_iota(jnp.int32, (span, tq), 1)
            - lax.broadcasted_iota(jnp.int32, (span, tq), 0)) + (q0 - start)
    wbias = jnp.where((dist >= 0) & (dist < WIN), 0.0, NEG)
    wbias = tile_heads(wbias - shift if bounded else wbias)
    o_win = []
    for (g, _), q_t in zip(chains, q_ts):
        sw = _dot(kw_ref[0, g, pl.ds(start, span), :], q_t) + wbias
        ew = jnp.exp2(sw if bounded else sw - jnp.max(sw, axis=0, keepdims=True))
        ow = _dot(vwt_ref[0, g, :, pl.ds(start, span)], ew.astype(BF16))
        o_win.append(ow[:NSA_HD] * (1.0 / ow[NSA_HD:NSA_HD + 1]))

    n_blk = kc_ref.shape[2]
    nb = mt_ref.shape[0]
    blk_end = lax.broadcasted_iota(jnp.int32, (n_blk, tq), 0) * CMP_STRIDE + (CMP_BLOCK - 1)
    t_row = q0 + lax.broadcasted_iota(jnp.int32, (1, tq), 1)
    cvalid = tile_heads(jnp.where(blk_end <= t_row, 1.0, 0.0)) > 0.5
    some = tile_heads((t_row >= CMP_BLOCK - 1).astype(F32))
    j = lax.broadcasted_iota(jnp.int32, (nb, tq), 0)
    cur = jnp.right_shift(t_row, SLC_BLOCK.bit_length() - 1)
    forced = (j == 0) | (j == cur) | (j == cur - 1)
    o_cmp = []
    psum = [None] * n_g
    for (g, _), q_t in zip(chains, q_ts):
        s = jnp.where(cvalid, _dot(kc_ref[0, g], q_t), NEG)
        e = jnp.exp2(s - jnp.max(s, axis=0, keepdims=True))
        p = e * (some / jnp.sum(e, axis=0, keepdims=True))
        o_cmp.append(_dot(vct_ref[0, g], p.astype(BF16)))
        for i in range(ch):
            part = p[:, i * tq:(i + 1) * tq]
            psum[g] = part if psum[g] is None else psum[g] + part
    for g in range(n_g):
        p_hi = psum[g].astype(BF16)
        p_lo = (psum[g] - p_hi.astype(F32)).astype(BF16)
        imp = _dot(mt_ref[...], p_hi) + _dot(mt_ref[...], p_lo)
        imp = jnp.where(j <= cur, jnp.where(forced, jnp.inf, imp), -1.0)
        ranks = [jnp.zeros((nb, tq), jnp.int32) for _ in range(RANK_LANES)]
        for i in range(nb):
            row_i = imp[i:i + 1, :]
            ahead = (row_i > imp) | ((row_i == imp) & (j > i))
            ranks[i % RANK_LANES] = ranks[i % RANK_LANES] + ahead.astype(jnp.int32)
        rank = functools.reduce(lambda a, b: a + b, ranks)
        sel_ref[g] = jnp.where(rank < min(SLC_TOPK, nb), 0.0, NEG)

    blocks_per_tile = tks // SLC_BLOCK
    rel = (lax.broadcasted_iota(jnp.int32, (tks, tq), 1)
           - lax.broadcasted_iota(jnp.int32, (tks, tq), 0))

    def slc_bias(it, k0, need_causal):
        biases = []
        for g in range(n_g):
            if isinstance(it, int):
                rows = sel_ref[g, it * blocks_per_tile:(it + 1) * blocks_per_tile, :]
            else:
                rows = sel_ref[g, pl.ds(pl.multiple_of(it * blocks_per_tile, blocks_per_tile), blocks_per_tile), :]
            bias = jnp.concatenate([jnp.broadcast_to(rows[jj:jj + 1], (SLC_BLOCK, tq))
                                    for jj in range(blocks_per_tile)], axis=0)
            if need_causal:
                bias = jnp.where(rel + (q0 - k0) >= 0, bias, NEG)
            biases.append(tile_heads(bias - shift if bounded else bias))
        return biases

    def slc_body(it, carry, need_causal=True):
        k0 = it * tks if isinstance(it, int) else pl.multiple_of(it * tks, tks)
        biases = slc_bias(it, k0, need_causal)
        out = []
        for (g, _), q_t, state in zip(chains, q_ts, carry):
            sc = _dot(ks_ref[0, g, pl.ds(k0, tks), :], q_t) + biases[g]
            vt = vst_ref[0, g, :, pl.ds(k0, tks)]
            if bounded:
                (acc,) = state
                acc = acc + _dot(vt, jnp.exp2(sc).astype(BF16))
                out.append((acc,))
            else:
                m, acc = state
                m_new = jnp.maximum(m, jnp.max(sc, axis=0, keepdims=True))
                acc = jnp.exp2(m - m_new) * acc + _dot(vt, jnp.exp2(sc - m_new).astype(BF16))
                out.append((m_new, acc))
        return tuple(out)

    zero_state = (jnp.zeros((VT_ROWS, lanes), F32),)
    init = tuple(zero_state if bounded else (jnp.full((1, lanes), NEG, F32),) + zero_state for _ in chains)
    n_it = (q0 + tq + tks - 1) // tks
    if bounded:
        def slc_tiles(n):
            carry = init
            for it in range(n):
                carry = slc_body(it, carry, need_causal=(it == n - 1))
            return carry

        max_it = ks_ref.shape[2] // tks
        fin = lax.switch(n_it - 1, [functools.partial(slc_tiles, n) for n in range(1, max_it + 1)])
    else:
        fin = lax.fori_loop(0, n_it, slc_body, init)
    o_slc = [st[-1][:NSA_HD] * (1.0 / st[-1][NSA_HD:NSA_HD + 1]) for st in fin]

    heads_per_store = LANES // NSA_HD
    for ci, (_, h0) in enumerate(chains):
        for i0 in range(0, ch, heads_per_store):
            gated = []
            for i in range(i0, i0 + heads_per_store):
                cols = slice(i * tq, (i + 1) * tq)
                o_h = jnp.zeros((NSA_HD, tq), F32)
                for br, o_b in enumerate((o_cmp[ci], o_slc[ci], o_win[ci])):
                    row = (h0 + i) * N_BRANCH + br
                    o_h = o_h + gt_ref[0, row:row + 1, :] * o_b[:, cols]
                gated.append(o_h)
            lo = (h0 + i0) * NSA_HD
            o_ref[0, :, lo:lo + LANES] = jnp.concatenate(gated, axis=0).T.astype(o_ref.dtype)


def _nsa(q_t, kc, vct, ks, vst, kw, vwt, gates_t, mt, bound, tq, tks, bounded):
    bsz, _, _, t = q_t.shape
    n_blk = kc.shape[2]
    n_g = NSA_KV_HEADS
    full = lambda b, i: (b, 0, 0, 0)
    k_spec = pl.BlockSpec((1, n_g, t, NSA_HD), full)
    vt_spec = pl.BlockSpec((1, n_g, VT_ROWS, t), full)
    return pl.pallas_call(
        functools.partial(_nsa_kernel, tq=tq, tks=tks, bounded=bounded),
        grid=(bsz, t // tq),
        in_specs=[pl.BlockSpec(memory_space=pltpu.SMEM),
                  pl.BlockSpec((1, NSA_HEADS, NSA_HD, tq), lambda b, i: (b, 0, 0, i)),
                  pl.BlockSpec((1, n_g, n_blk, NSA_HD), full),
                  pl.BlockSpec((1, n_g, NSA_HD, n_blk), full),
                  k_spec, vt_spec, k_spec, vt_spec,
                  pl.BlockSpec((1, GATE_ROWS, tq), lambda b, i: (b, 0, i)),
                  pl.BlockSpec(mt.shape, lambda b, i: (0, 0))],
        out_specs=pl.BlockSpec((1, tq, NSA_WIDTH), lambda b, i: (b, i, 0)),
        out_shape=jax.ShapeDtypeStruct((bsz, t, NSA_WIDTH), BF16),
        scratch_shapes=[pltpu.VMEM((n_g, mt.shape[0], tq), F32)],
        compiler_params=pltpu.CompilerParams(
            dimension_semantics=("arbitrary", "arbitrary"), vmem_limit_bytes=VMEM_LIMIT),
        name="nsa_bounded" if bounded else "nsa",
    )(bound, q_t, kc, vct, ks, vst, kw, vwt, gates_t, mt)


def _causal_conv(u, prev, cw, cb):
    tm = u.shape[0]
    g = SUBLANES
    r8 = lax.broadcasted_iota(jnp.int32, (g, u.shape[1]), 0)
    wrap1 = jnp.where(r8 == 0, prev[2 * g - 1:2 * g], pltpu.roll(u[tm - g:], 1, axis=0))
    wrap2 = jnp.where(r8 == 0, prev[g - 1:g], pltpu.roll(u[tm - 2 * g:tm - g], 1, axis=0))
    u1 = jnp.concatenate([wrap1, u[:tm - g]], axis=0)
    u2 = jnp.concatenate([wrap2, wrap1, u[:tm - 2 * g]], axis=0)
    return cb + u2 * cw[0:1] + u1 * cw[1:2] + u * cw[2:3]


def _ffn_kernel(x_ref, hg_ref, ns_ref, mod_ref, wo_ref, g2_ref, wu_ref, cw_ref, cb_ref, wd_ref,
                o_ref, carry_ref, buf_ref, x1_ref, h2_ref, u_ref, g_ref, acc_ref, *, fc):
    n_sub, sub = x1_ref.shape[0], x1_ref.shape[1]
    n_lt = D_MODEL // LANES
    groups = sub // SUBLANES
    pitch = buf_ref.shape[2] // SUBLANES
    first = pl.program_id(1) == 0

    def head(s):
        rows = slice(s * sub, (s + 1) * sub)
        mix = _dot(hg_ref[0, rows], wo_ref[:HG_WIDTH, :]) + _dot(ns_ref[0, rows], wo_ref[HG_WIDTH:, :])
        x1_nat = x_ref[0, rows] + mod_ref[0, 2:3, :] * mix
        for c in range(n_lt):
            for sg in range(SUBLANES):
                buf_ref[s, c, sg * pitch:sg * pitch + groups] = x1_nat[sg * groups:(sg + 1) * groups,
                                                                       c * LANES:(c + 1) * LANES]
        x1 = jnp.concatenate(
            [jnp.concatenate([buf_ref[s, c, pl.ds(j, SUBLANES, stride=pitch), :] for j in range(groups)], axis=0)
             for c in range(n_lt)], axis=1)
        x1_ref[s] = x1
        y = x1 * lax.rsqrt(jnp.mean(x1 * x1, axis=-1, keepdims=True) + EPS) * g2_ref[...]
        h2_ref[s] = (y * (1.0 + mod_ref[0, 4:5, :]) + mod_ref[0, 3:4, :]).astype(h2_ref.dtype)
        acc_ref[s] = jnp.zeros((sub, D_MODEL), F32)

    def tail(s):
        out = x1_ref[s] + mod_ref[0, 5:6, :] * acc_ref[s]
        for c in range(n_lt):
            for j in range(groups):
                buf_ref[s, c, pl.ds(j, SUBLANES, stride=pitch), :] = out[j * SUBLANES:(j + 1) * SUBLANES,
                                                                         c * LANES:(c + 1) * LANES]
        for c in range(n_lt):
            for sg in range(SUBLANES):
                o_ref[0, s * sub + sg * groups:s * sub + (sg + 1) * groups, c * LANES:(c + 1) * LANES] = (
                    buf_ref[s, c, sg * pitch:sg * pitch + groups])

    n = D_FF // fc

    def up(c, slot):
        s, jc = divmod(c, n)
        for half in range(2):
            off = half * D_FF + jc * fc
            u_ref[slot, half] = _dot(h2_ref[s], wu_ref[:, off:off + fc])

    def act(c, slot):
        s, jc = divmod(c, n)
        halves = []
        for half in range(2):
            cols = slice(half * D_FF + jc * fc, half * D_FF + (jc + 1) * fc)
            u = u_ref[slot, half]
            prev = jnp.where(first, 0.0, carry_ref[:, cols]) if s == 0 else carry_ref[:, cols]
            carry_ref[:, cols] = u[sub - 2 * SUBLANES:]
            halves.append(_causal_conv(u, prev, cw_ref[:, cols], cb_ref[:, cols]))
        a, v = halves
        g_ref[slot] = (_silu(a) * v).astype(g_ref.dtype)

    def down(c, slot):
        s, jc = divmod(c, n)
        acc_ref[s] += _dot(g_ref[slot], wd_ref[jc * fc:(jc + 1) * fc, :])

    total = n_sub * n
    head(0)
    for i in range(total + 2):
        if i < total:
            up(i, i % 2)
        if 0 <= i - 1 < total:
            act(i - 1, (i - 1) % 2)
        if 0 <= i - 2 < total:
            down(i - 2, i % 2)
        s, jc = divmod(i, n)
        if jc == HEAD_AT and s + 1 < n_sub:
            head(s + 1)
        if jc == 1 and 1 <= s <= n_sub:
            tail(s - 1)


def _ffn(x, o_hg, o_nsa, mod, w_out, g2, w_up, conv_w, conv_b, w_down, tm, sub, fc):
    bsz, t, _ = x.shape
    row_spec = lambda w: pl.BlockSpec((1, tm, w), lambda b, i: (b, i, 0))
    groups = sub // SUBLANES
    pad = SUBLANES if (groups // SUBLANES) % 2 == 0 else 0
    n_sub = tm // sub
    resident = lambda a: pl.BlockSpec(a.shape, lambda b, i: (0, 0), pipeline_mode=pl.Buffered(1))
    return pl.pallas_call(
        functools.partial(_ffn_kernel, fc=fc),
        grid=(bsz, t // tm),
        in_specs=[row_spec(D_MODEL), row_spec(HG_WIDTH), row_spec(NSA_WIDTH),
                  pl.BlockSpec((1, 6, D_MODEL), lambda b, i: (b, 0, 0)),
                  resident(w_out), resident(g2), resident(w_up), resident(conv_w), resident(conv_b),
                  resident(w_down)],
        out_specs=row_spec(D_MODEL),
        out_shape=jax.ShapeDtypeStruct(x.shape, F32),
        scratch_shapes=[pltpu.VMEM((2 * SUBLANES, 2 * D_FF), F32),
                        pltpu.VMEM((n_sub, D_MODEL // LANES, sub + SUBLANES * pad, LANES), F32),
                        pltpu.VMEM((n_sub, sub, D_MODEL), F32),
                        pltpu.VMEM((n_sub, sub, D_MODEL), BF16),
                        pltpu.VMEM((2, 2, sub, fc), F32),
                        pltpu.VMEM((2, sub, fc), BF16),
                        pltpu.VMEM((n_sub, sub, D_MODEL), F32)],
        compiler_params=pltpu.CompilerParams(
            dimension_semantics=("arbitrary", "arbitrary"), vmem_limit_bytes=VMEM_LIMIT),
        name="ffn",
    )(x, o_hg, o_nsa, mod, w_out, g2, w_up, conv_w, conv_b, w_down)


def _rope_tables():
    half = ROPE_DIM // 2
    inv = ROPE_THETA ** (-jnp.arange(half, dtype=F32) * 2.0 / ROPE_DIM)
    return (jnp.tile(inv.reshape(half, 1), (1, LANES)),)


def _gain_t(g):
    return jnp.tile(g.reshape(NSA_HD, 1), (LANES // NSA_HD, LANES))


def _selection_tables(t):
    n_seg = t // CMP_STRIDE
    nb = t // SLC_BLOCK
    cst = np.arange(n_seg) * CMP_STRIDE
    sst = np.arange(nb) * SLC_BLOCK
    ovl = np.clip(np.minimum(cst[:, None] + CMP_BLOCK, sst[None] + SLC_BLOCK)
                  - np.maximum(cst[:, None], sst[None]), 0, None) / CMP_BLOCK
    ovl[(t - CMP_BLOCK) // CMP_STRIDE + 1:] = 0.0
    return (jnp.asarray(ovl.T, dtype=BF16),)


def _layer(x, mod, pos_row, l, p, tables):
    bsz, t, _ = x.shape
    inv_t, mt = tables
    w_in_p = p["w_in"][l].astype(BF16)
    qg_t = _gain_t(p["q_norm_g"][l])
    kg_t = jnp.stack([_gain_t(p["k_norm_g"][l, br]) for br in range(N_BRANCH)])
    q_t, kc, vc, ks, vst, kw, vwt, gates_t, o_hg = _mix(
        x, mod, p["norm1_g"][l].reshape(1, D_MODEL), w_in_p, pos_row, inv_t, qg_t, kg_t,
        p["lb_logits"], p["hg_norm_g"][l].reshape(1, HG_HD), l, tm=TM_MIX)

    pe2 = p["pe_cmp"][l].reshape(2, 2, CMP_STRIDE * NSA_HD)
    kcmp, vcmp_t = _compress(kc, vc, pe2, p["w_cmp1"][l].astype(BF16), p["w_cmp2"][l].astype(BF16))
    bound = (SCORE_BOUND_MARGIN * NSA_HD ** 0.5 * jnp.max(jnp.abs(p["q_norm_g"][l]))
             * jnp.max(jnp.abs(p["k_norm_g"][l, 1:]))).reshape(1).astype(F32)
    nsa_args = (q_t, kcmp, vcmp_t, ks, vst, kw, vwt, gates_t, mt, bound)
    o_nsa = lax.cond(bound[0] <= MAX_SCORE_BOUND,
                     lambda a: _nsa(*a, tq=TQ_NSA, tks=TKS_NSA, bounded=True),
                     lambda a: _nsa(*a, tq=TQ_NSA, tks=TKS_NSA, bounded=False), nsa_args)

    return _ffn(x, o_hg, o_nsa, mod, p["w_out"][l].astype(BF16), p["norm2_g"][l].reshape(1, D_MODEL),
                p["w_up"][l].astype(BF16), p["conv_w"][l], p["conv_b"][l].reshape(1, 2 * D_FF),
                p["w_down"][l].astype(BF16), tm=TM_FFN, sub=SUB_FFN, fc=FC_FFN)


def kernel(x, c, positions, w_ada, b_ada, norm1_g, w_in, lb_logits, hg_norm_g, q_norm_g, k_norm_g, pe_cmp, w_cmp1, w_cmp2, w_out, norm2_g, w_up, conv_w, conv_b, w_down):
    p = dict(w_in=w_in, norm1_g=norm1_g, lb_logits=lb_logits, hg_norm_g=hg_norm_g, q_norm_g=q_norm_g,
             k_norm_g=k_norm_g, pe_cmp=pe_cmp, w_cmp1=w_cmp1, w_cmp2=w_cmp2, w_out=w_out,
             norm2_g=norm2_g, w_up=w_up, conv_w=conv_w, conv_b=conv_b, w_down=w_down)
    bsz, t, _ = x.shape
    assert x.shape[2] == D_MODEL and t % TM_MIX == 0 and t % TKS_NSA == 0 and t >= WIN + TQ_NSA
    assert TKS_NSA % TQ_NSA == 0 and TM_MIX % HG_CHUNK == 0 and D_FF % FC_FFN == 0 and TM_FFN % SUB_FFN == 0
    tables = _rope_tables() + _selection_tables(t)
    pos_row = positions.reshape(bsz, 1, t)
    for l in range(w_ada.shape[0]):
        mod = _ada(c, w_ada[l], b_ada[l]).reshape(bsz, 6, D_MODEL)
        x = _layer(x, mod, pos_row, l, p, tables)
    return x
```

```python
import functools

import jax
import jax.numpy as jnp
import numpy as np
from jax import lax
from jax.experimental import pallas as pl
from jax.experimental.pallas import tpu as pltpu

D_MODEL = 1024
HG_HEADS = 4
HG_HD = 128
HG_WIDTH = HG_HEADS * HG_HD
HG_CHUNK = 128
HG_SUB = 8
LOG2E = 1.4426950408889634
NSA_HEADS = 8
NSA_KV_HEADS = 2
NSA_HD = 64
NSA_GROUP = NSA_HEADS // NSA_KV_HEADS
NSA_CHAIN = 4
RANK_LANES = 4
NSA_WIDTH = NSA_HEADS * NSA_HD
N_BRANCH = 3
CMP_BLOCK = 32
CMP_STRIDE = 16
CMP_HIDDEN = 256
SLC_BLOCK = 64
SLC_TOPK = 16
WIN = 512
ROPE_DIM = NSA_HD // 4
ROPE_THETA = 500000.0
D_FF = 2816
CONV_W = 3
EPS = 1e-6
NEG = -1e30
SCORE_BOUND_MARGIN = 1.02
MAX_SCORE_BOUND = 40.0

LANES = 128
SUBLANES = 8
VMEM_LIMIT = 56 * 1024 * 1024

TM_MIX = 512
TQ_NSA = 256
TKS_NSA = 512
TM_FFN = 512
SUB_FFN = 256
HEAD_AT = 7
FC_FFN = 256

OFF_HG = 0
OFF_Q = 4 * HG_WIDTH
OFF_KV = OFF_Q + NSA_WIDTH
OFF_G = OFF_KV + 6 * NSA_KV_HEADS * NSA_HD
IN_COLS = OFF_G + N_BRANCH * NSA_HEADS
GATE_ROWS = N_BRANCH * NSA_HEADS
VT_ROWS = NSA_HD + SUBLANES

BF16 = jnp.bfloat16
F32 = jnp.float32


def _dot(a, b):
    return jnp.dot(a, b, preferred_element_type=F32)


def _dot_nt(a, b):
    return lax.dot_general(a, b, (((1,), (1,)), ((), ())), preferred_element_type=F32)


def _exp_neg(x):
    return jnp.exp2(x * (-LOG2E))


def _sigmoid(x):
    return 1.0 / (1.0 + _exp_neg(x))


def _silu(x):
    return x * _sigmoid(x)


def _ada_kernel(c_ref, w_ref, b_ref, o_ref):
    cs = _silu(c_ref[...])
    o_ref[...] = _dot(cs.astype(BF16), w_ref[...].astype(BF16)) + b_ref[...]


def _ada(c, w, b):
    bsz = c.shape[0]
    n = w.shape[1]
    tn = D_MODEL
    return pl.pallas_call(
        _ada_kernel,
        grid=(n // tn,),
        in_specs=[pl.BlockSpec((bsz, D_MODEL), lambda j: (0, 0)),
                  pl.BlockSpec((D_MODEL, tn), lambda j: (0, j)),
                  pl.BlockSpec((1, tn), lambda j: (0, j))],
        out_specs=pl.BlockSpec((bsz, tn), lambda j: (0, j)),
        out_shape=jax.ShapeDtypeStruct((bsz, n), F32),
        name="ada",
    )(c, w, b.reshape(1, n))


def _pair_norm_rope_t(xt, g_t, cos_t, sin_t):
    half = ROPE_DIM // 2
    outs = []
    for hh in range(2):
        x = xt[hh * NSA_HD:(hh + 1) * NSA_HD]
        ms = jnp.mean(x * x, axis=0, keepdims=True)
        xn = x * lax.rsqrt(ms + EPS) * g_t[hh * NSA_HD:(hh + 1) * NSA_HD]
        x1, x2 = xn[:half], xn[half:ROPE_DIM]
        outs += [x1 * cos_t - x2 * sin_t, x2 * cos_t + x1 * sin_t, xn[ROPE_DIM:]]
    return jnp.concatenate(outs, axis=0)


def _project_parts(x_ref, mod_ref, g1_ref, w_ref, pos_ref, inv_ref, qg_ref, kg_ref,
                   zhg_out, qt_ref, kc_ref, vc_ref, ks_ref, vst_ref, kw_ref, vwt_ref, gt_ref, seg_ref):
    x = x_ref[0]
    ms = jnp.mean(x * x, axis=-1, keepdims=True)
    y = x * lax.rsqrt(ms + EPS) * g1_ref[...]
    h = (y * (1.0 + mod_ref[0, 1:2, :]) + mod_ref[0, 0:1, :]).astype(BF16)

    tm = x.shape[0]
    reps = tm // LANES

    def lane_tile(a):
        return jnp.concatenate([a] * reps, axis=1)

    ang = lane_tile(inv_ref[...]) * pos_ref[0].astype(F32)
    cos_t = jnp.cos(ang)
    sin_t = jnp.sin(ang)
    scale = NSA_HD ** -0.5 * LOG2E
    vals = {}

    def hg_cols(lo, hi):
        def part():
            zhg_out[:, lo:hi] = _dot(h, w_ref[:, OFF_HG + lo:OFF_HG + hi])
        return part

    def q_pair(p):
        def part():
            if p % 2 == 0:
                vals["zq"] = _dot(h, w_ref[:, OFF_Q + p * LANES:OFF_Q + (p + 2) * LANES])
                vals["qg"] = lane_tile(qg_ref[...])
            off = (p % 2) * LANES
            rt = _pair_norm_rope_t(vals["zq"][:, off:off + LANES].T, vals["qg"], cos_t, sin_t)
            rt = (rt * scale).astype(qt_ref.dtype)
            qt_ref[0, 2 * p] = rt[:NSA_HD]
            qt_ref[0, 2 * p + 1] = rt[NSA_HD:]
        return part

    def kv_dot(br):
        vals["zkv"] = _dot(h, w_ref[:, OFF_KV + 2 * br * LANES:OFF_KV + (2 * br + 2) * LANES])

    def store_segments(a, out_ref):
        seg_ref[...] = a
        n_seg = tm // CMP_STRIDE
        lane = lax.broadcasted_iota(jnp.int32, (n_seg, LANES), 1)
        for l in range(0, CMP_STRIDE, 2):
            even = seg_ref[pl.ds(l, n_seg, stride=CMP_STRIDE), :]
            odd = seg_ref[pl.ds(l + 1, n_seg, stride=CMP_STRIDE), :]
            cols = slice((l // 2) * LANES, (l // 2 + 1) * LANES)
            out_ref[0, 0, :, cols] = jnp.where(lane < NSA_HD, even, pltpu.roll(odd, NSA_HD, axis=1))
            out_ref[0, 1, :, cols] = jnp.where(lane < NSA_HD, pltpu.roll(even, NSA_HD, axis=1), odd)

    def key(br, k_ref):
        def part():
            kv_dot(br)
            kt = _pair_norm_rope_t(vals["zkv"][:, :LANES].T, lane_tile(kg_ref[br]), cos_t, sin_t)
            if br == 0:
                store_segments(kt.T, k_ref)
                store_segments(vals["zkv"][:, LANES:], vc_ref)
            else:
                kk = kt.T.astype(k_ref.dtype)
                for g in range(NSA_KV_HEADS):
                    k_ref[0, g] = kk[:, g * NSA_HD:(g + 1) * NSA_HD]
        return part

    def value_t(vt_ref):
        def part():
            vt = vals["zkv"][:, LANES:].T.astype(vt_ref.dtype)
            ones_row = jnp.where(lax.broadcasted_iota(jnp.int32, (SUBLANES, tm), 0) == 0, 1.0, 0.0).astype(vt_ref.dtype)
            for g in range(NSA_KV_HEADS):
                vt_ref[0, g, :NSA_HD] = vt[g * NSA_HD:(g + 1) * NSA_HD]
                vt_ref[0, g, NSA_HD:] = ones_row
        return part

    def gates_part():
        gates = _sigmoid(_dot(h, w_ref[:, OFF_G:IN_COLS]))
        wide = jnp.concatenate([gates, jnp.zeros((tm, LANES - GATE_ROWS), F32)], axis=1)
        gt_ref[0] = wide.T[:GATE_ROWS]

    wide = 2 * LANES
    hg = [hg_cols(lo, lo + wide) for lo in range(0, 4 * HG_WIDTH, wide)]
    rest = [q_pair(0), q_pair(1), q_pair(2), q_pair(3), key(0, kc_ref), key(1, ks_ref), value_t(vst_ref),
            key(2, kw_ref), value_t(vwt_ref), gates_part]
    parts = []
    for k in range(max(len(hg), len(rest))):
        parts += hg[k:k + 1] + rest[k:k + 1]
    return parts


def _hgrn_chunk(zq, zf, zi, zg, lb, hg_g, st_ref, bk_ref, h, tri, level):
    c = HG_CHUNK
    e_z = _exp_neg(jnp.abs(zf))
    logsig = jnp.minimum(zf, 0.0) - jnp.log(1.0 + e_z)
    a = jnp.log(lb)
    bb = jnp.log1p(-lb) + logsig
    logf = jnp.maximum(a, bb) + jnp.log(1.0 + _exp_neg(jnp.abs(a - bb)))
    k = (1.0 - lb) * (jnp.where(zf >= 0.0, e_z, 1.0) / (1.0 + e_z))
    q = _silu(zq)
    v = zi
    l_hi = logf.astype(BF16)
    l_mid = (logf - l_hi.astype(F32)).astype(BF16)
    l_lo = (logf - l_hi.astype(F32) - l_mid.astype(F32)).astype(BF16)
    bc = _dot(tri, l_hi) + _dot(tri, l_mid) + _dot(tri, l_lo)

    col = lax.broadcasted_iota(jnp.int32, (HG_SUB, c), 1)
    b2 = bc * LOG2E
    bk_ref[0, h] = b2
    bk_ref[1, h] = k
    rows_a = []
    for i in range(c // HG_SUB):
        lo = i * HG_SUB
        b_i = b2[lo:lo + HG_SUB]
        q_i = q[lo:lo + HG_SUB]
        a_i = jnp.zeros((HG_SUB, c), F32)
        for s in range(HG_SUB):
            b_s = bk_ref[0, h, lo + s:lo + s + 1, :]
            k_s = bk_ref[1, h, lo + s:lo + s + 1, :]
            e = jnp.exp2(jnp.minimum(b_i - b_s, 0.0)) * q_i * k_s
            a_i = jnp.where(col == lo + s, jnp.sum(e, axis=-1, keepdims=True), a_i)
        rows_a.append(a_i)
    amat = jnp.concatenate(rows_a, axis=0)

    size, idx = c // 2, 1
    while size >= HG_SUB:
        pieces = []
        for e0 in range(0, c, 2 * size):
            o0 = e0 + size
            r = b2[o0:o0 + 1]
            pieces.append(k[e0:o0] * jnp.exp2(r - b2[e0:o0]))
            pieces.append(q[o0:o0 + size] * jnp.exp2(b2[o0:o0 + size] - r))
        hmat = jnp.concatenate(pieces, axis=0).astype(BF16)
        amat = jnp.where(level == idx, _dot_nt(hmat, hmat), amat)
        size, idx = size // 2, idx + 1
    amat = jnp.where(level >= 0, amat, 0.0)

    st = st_ref[h]
    o = _dot_nt((q * jnp.exp2(b2)).astype(BF16), st.astype(BF16)) + _dot(amat.astype(BF16), v.astype(BF16))
    bl = b2[c - 1:c]
    kdec = (k * jnp.exp2(bl - b2)).astype(BF16)
    st_ref[h] = jnp.exp2(bl) * st + _dot(v.T.astype(BF16), kdec)

    y = o * lax.rsqrt(jnp.mean(o * o, axis=-1, keepdims=True) + EPS) * hg_g
    return y * _silu(zg)


def _hgrn_parts(z_in, lbl_ref, g_ref, o_ref, st_ref, bk_ref, l_idx):
    lg = lbl_ref[...]
    ex = jnp.exp(lg - jnp.max(lg, axis=0, keepdims=True))
    sm = ex / jnp.sum(ex, axis=0, keepdims=True)
    lb_all = jnp.sum(sm[:l_idx + 1], axis=0, keepdims=True)

    c = HG_CHUNK
    ri = lax.broadcasted_iota(jnp.int32, (c, c), 0)
    ci = lax.broadcasted_iota(jnp.int32, (c, c), 1)
    tri = jnp.where(ci <= ri, 1.0, 0.0).astype(BF16)
    level = jnp.where(ci > ri, -1, 0)
    size, idx = c // 2, 1
    while size >= HG_SUB:
        sh = size.bit_length() - 1
        paired = ((ri >> (sh + 1)) == (ci >> (sh + 1))) & ((ri >> sh) != (ci >> sh)) & (ci <= ri)
        level = jnp.where(paired, idx, level)
        size, idx = size // 2, idx + 1

    def chunk_head(ch, h):
        def part():
            rows = slice(ch * c, (ch + 1) * c)
            sl = slice(h * HG_HD, (h + 1) * HG_HD)
            zq = z_in[rows, h * HG_HD:(h + 1) * HG_HD]
            zf = z_in[rows, HG_WIDTH + h * HG_HD:HG_WIDTH + (h + 1) * HG_HD]
            zi = z_in[rows, 2 * HG_WIDTH + h * HG_HD:2 * HG_WIDTH + (h + 1) * HG_HD]
            zg = z_in[rows, 3 * HG_WIDTH + h * HG_HD:3 * HG_WIDTH + (h + 1) * HG_HD]
            y = _hgrn_chunk(zq, zf, zi, zg, lb_all[:, sl], g_ref[...], st_ref, bk_ref, h, tri, level)
            o_ref[0, rows, h * HG_HD:(h + 1) * HG_HD] = y.astype(o_ref.dtype)
        return part

    return [chunk_head(ch, h) for ch in range(z_in.shape[0] // c) for h in range(HG_HEADS)]


def _mix_kernel(x_ref, mod_ref, g1_ref, w_ref, pos_ref, inv_ref, qg_ref, kg_ref, lbl_ref, hgg_ref,
                qt_ref, kc_ref, vc_ref, ks_ref, vst_ref, kw_ref, vwt_ref, gt_ref, ohg_ref,
                zbuf_ref, st_ref, bk_ref, seg_ref, *, l_idx, tiles_per_seq):
    j = pl.program_id(0)

    @pl.when(j == 0)
    def _():
        zbuf_ref[...] = jnp.zeros_like(zbuf_ref)

    @pl.when((j == 0) | (lax.rem(j + tiles_per_seq - 1, tiles_per_seq) == 0))
    def _():
        st_ref[...] = jnp.zeros_like(st_ref)

    rec = _hgrn_parts(zbuf_ref, lbl_ref, hgg_ref, ohg_ref, st_ref, bk_ref, l_idx)
    proj = _project_parts(x_ref, mod_ref, g1_ref, w_ref, pos_ref, inv_ref, qg_ref, kg_ref, zbuf_ref,
                          qt_ref, kc_ref, vc_ref, ks_ref, vst_ref, kw_ref, vwt_ref, gt_ref, seg_ref)
    for part in rec + proj:
        part()


def _mix(x, mod, g1, w_in_p, pos_row, inv_t, qg_t, kg_t, lb_logits, hg_g, l_idx, tm):
    bsz, t, _ = x.shape
    nm = t // tm
    n_tiles = bsz * nm

    def cur(j):
        jc = jnp.minimum(j, n_tiles - 1)
        return jc // nm, jc % nm

    def prev(j):
        jp = jnp.maximum(j - 1, 0)
        return jp // nm, jp % nm

    kv_shape = (bsz, NSA_KV_HEADS, t, NSA_HD)
    kv_spec = pl.BlockSpec((1, NSA_KV_HEADS, tm, NSA_HD), lambda j: (cur(j)[0], 0, cur(j)[1], 0))
    seg_w = CMP_STRIDE * NSA_HD
    seg_shape = (bsz, NSA_KV_HEADS, t // CMP_STRIDE, seg_w)
    seg_spec = pl.BlockSpec((1, NSA_KV_HEADS, tm // CMP_STRIDE, seg_w), lambda j: (cur(j)[0], 0, cur(j)[1], 0))
    vt_shape = (bsz, NSA_KV_HEADS, VT_ROWS, t)
    vt_spec = pl.BlockSpec((1, NSA_KV_HEADS, VT_ROWS, tm), lambda j: (cur(j)[0], 0, 0, cur(j)[1]))
    const = lambda j: (0, 0)
    return pl.pallas_call(
        functools.partial(_mix_kernel, l_idx=l_idx, tiles_per_seq=nm),
        grid=(n_tiles + 1,),
        in_specs=[pl.BlockSpec((1, tm, D_MODEL), lambda j: (cur(j)[0], cur(j)[1], 0)),
                  pl.BlockSpec((1, 6, D_MODEL), lambda j: (cur(j)[0], 0, 0)),
                  pl.BlockSpec((1, D_MODEL), const),
                  pl.BlockSpec((D_MODEL, IN_COLS), const),
                  pl.BlockSpec((1, 1, tm), lambda j: (cur(j)[0], 0, cur(j)[1])),
                  pl.BlockSpec(inv_t.shape, const),
                  pl.BlockSpec(qg_t.shape, const),
                  pl.BlockSpec(kg_t.shape, lambda j: (0, 0, 0)),
                  pl.BlockSpec(lb_logits.shape, const),
                  pl.BlockSpec((1, HG_HD), const)],
        out_specs=[pl.BlockSpec((1, NSA_HEADS, NSA_HD, tm), lambda j: (cur(j)[0], 0, 0, cur(j)[1])),
                   seg_spec, seg_spec, kv_spec, vt_spec, kv_spec, vt_spec,
                   pl.BlockSpec((1, GATE_ROWS, tm), lambda j: (cur(j)[0], 0, cur(j)[1])),
                   pl.BlockSpec((1, tm, HG_WIDTH), lambda j: (prev(j)[0], prev(j)[1], 0))],
        out_shape=[jax.ShapeDtypeStruct((bsz, NSA_HEADS, NSA_HD, t), BF16),
                   jax.ShapeDtypeStruct(seg_shape, F32),
                   jax.ShapeDtypeStruct(seg_shape, F32),
                   jax.ShapeDtypeStruct(kv_shape, BF16),
                   jax.ShapeDtypeStruct(vt_shape, BF16),
                   jax.ShapeDtypeStruct(kv_shape, BF16),
                   jax.ShapeDtypeStruct(vt_shape, BF16),
                   jax.ShapeDtypeStruct((bsz, GATE_ROWS, t), F32),
                   jax.ShapeDtypeStruct((bsz, t, HG_WIDTH), BF16)],
        scratch_shapes=[pltpu.VMEM((tm, 4 * HG_WIDTH), F32),
                        pltpu.VMEM((HG_HEADS, HG_HD, HG_HD), F32),
                        pltpu.VMEM((2, HG_HEADS, HG_CHUNK, HG_HD), F32),
                        pltpu.VMEM((tm, LANES), F32)],
        compiler_params=pltpu.CompilerParams(
            dimension_semantics=("arbitrary",), vmem_limit_bytes=VMEM_LIMIT),
        name="mix",
    )(x, mod, g1, w_in_p, pos_row, inv_t, qg_t, kg_t, lb_logits, hg_g)


def _compress_kernel(xk_ref, xv_ref, pe_ref, w1_ref, w2_ref, ko_ref, vo_ref):
    half = CMP_STRIDE * NSA_HD
    n_g, nseg = xk_ref.shape[1], xk_ref.shape[2]
    outs = []
    for kv, x_ref in enumerate((xk_ref, xv_ref)):
        x = x_ref[0].reshape(n_g * nseg, half)
        ha = _dot((x + pe_ref[kv, 0:1, :]).astype(BF16), w1_ref[kv, :half, :])
        hb = _dot((x + pe_ref[kv, 1:2, :]).astype(BF16), w1_ref[kv, half:, :])
        pre = ha + pltpu.roll(hb, n_g * nseg - 1, axis=0)
        outs.append(_dot(_silu(pre).astype(BF16), w2_ref[kv]))
    for g in range(n_g):
        rows = slice(g * nseg, (g + 1) * nseg)
        ko_ref[0, g] = outs[0][rows].astype(ko_ref.dtype)
        vo_ref[0, g] = outs[1][rows].T.astype(vo_ref.dtype)


def _compress(xk, xv, pe2, w1, w2):
    bsz, g, nseg, width = xk.shape
    x_spec = pl.BlockSpec((1, g, nseg, width), lambda b: (b, 0, 0, 0))
    return pl.pallas_call(
        _compress_kernel,
        grid=(bsz,),
        in_specs=[x_spec, x_spec,
                  pl.BlockSpec(pe2.shape, lambda b: (0, 0, 0)),
                  pl.BlockSpec(w1.shape, lambda b: (0, 0, 0)),
                  pl.BlockSpec(w2.shape, lambda b: (0, 0, 0))],
        out_specs=[pl.BlockSpec((1, g, nseg, NSA_HD), lambda b: (b, 0, 0, 0)),
                   pl.BlockSpec((1, g, NSA_HD, nseg), lambda b: (b, 0, 0, 0))],
        out_shape=[jax.ShapeDtypeStruct((bsz, g, nseg, NSA_HD), BF16),
                   jax.ShapeDtypeStruct((bsz, g, NSA_HD, nseg), BF16)],
        compiler_params=pltpu.CompilerParams(
            dimension_semantics=("arbitrary",), vmem_limit_bytes=VMEM_LIMIT),
        name="compress",
    )(xk, xv, pe2, w1, w2)


def _nsa_kernel(bound_ref, qt_ref, kc_ref, vct_ref, ks_ref, vst_ref, kw_ref, vwt_ref, gt_ref, mt_ref,
                o_ref, sel_ref, *, tq, tks, bounded):
    r = NSA_GROUP
    n_g = NSA_KV_HEADS
    ch = NSA_CHAIN
    lanes = ch * tq
    q0 = pl.program_id(1) * tq
    chains = [(g, g * r + c * ch) for g in range(n_g) for c in range(r // ch)]
    q_ts = [jnp.concatenate([qt_ref[0, h0 + i] for i in range(ch)], axis=1)
            for _, h0 in chains]

    def tile_heads(a):
        return jnp.concatenate([a] * ch, axis=1)

    shift = bound_ref[0] * LOG2E if bounded else None

    span = WIN + tq
    start = pl.multiple_of(jnp.maximum(q0 - WIN, 0), tq)
    dist = (lax.broadcasted_iota(jnp.int32, (span, tq), 1)
            - lax.broadcasted_iota(jnp.int32, (span, tq), 0)) + (q0 - start)
    wbias = jnp.where((dist >= 0) & (dist < WIN), 0.0, NEG)
    wbias = tile_heads(wbias - shift if bounded else wbias)
    o_win = []
    for (g, _), q_t in zip(chains, q_ts):
        sw = _dot(kw_ref[0, g, pl.ds(start, span), :], q_t) + wbias
        ew = jnp.exp2(sw if bounded else sw - jnp.max(sw, axis=0, keepdims=True))
        ow = _dot(vwt_ref[0, g, :, pl.ds(start, span)], ew.astype(BF16))
        o_win.append(ow[:NSA_HD] * (1.0 / ow[NSA_HD:NSA_HD + 1]))

    n_blk = kc_ref.shape[2]
    nb = mt_ref.shape[0]
    blk_end = lax.broadcasted_iota(jnp.int32, (n_blk, tq), 0) * CMP_STRIDE + (CMP_BLOCK - 1)
    t_row = q0 + lax.broadcasted_iota(jnp.int32, (1, tq), 1)
    cvalid = tile_heads(jnp.where(blk_end <= t_row, 1.0, 0.0)) > 0.5
    some = tile_heads((t_row >= CMP_BLOCK - 1).astype(F32))
    j = lax.broadcasted_iota(jnp.int32, (nb, tq), 0)
    cur = jnp.right_shift(t_row, SLC_BLOCK.bit_length() - 1)
    forced = (j == 0) | (j == cur) | (j == cur - 1)
    o_cmp = []
    psum = [None] * n_g
    for (g, _), q_t in zip(chains, q_ts):
        s = jnp.where(cvalid, _dot(kc_ref[0, g], q_t), NEG)
        e = jnp.exp2(s - jnp.max(s, axis=0, keepdims=True))
        p = e * (some / jnp.sum(e, axis=0, keepdims=True))
        o_cmp.append(_dot(vct_ref[0, g], p.astype(BF16)))
        for i in range(ch):
            part = p[:, i * tq:(i + 1) * tq]
            psum[g] = part if psum[g] is None else psum[g] + part
    for g in range(n_g):
        p_hi = psum[g].astype(BF16)
        p_lo = (psum[g] - p_hi.astype(F32)).astype(BF16)
        imp = _dot(mt_ref[...], p_hi) + _dot(mt_ref[...], p_lo)
        imp = jnp.where(j <= cur, jnp.where(forced, jnp.inf, imp), -1.0)
        ranks = [jnp.zeros((nb, tq), jnp.int32) for _ in range(RANK_LANES)]
        for i in range(nb):
            row_i = imp[i:i + 1, :]
            ahead = (row_i > imp) | ((row_i == imp) & (j > i))
            ranks[i % RANK_LANES] = ranks[i % RANK_LANES] + ahead.astype(jnp.int32)
        rank = functools.reduce(lambda a, b: a + b, ranks)
        sel_ref[g] = jnp.where(rank < min(SLC_TOPK, nb), 0.0, NEG)

    blocks_per_tile = tks // SLC_BLOCK
    rel = (lax.broadcasted_iota(jnp.int32, (tks, tq), 1)
           - lax.broadcasted_iota(jnp.int32, (tks, tq), 0))

    def slc_bias(it, k0, need_causal):
        biases = []
        for g in range(n_g):
            if isinstance(it, int):
                rows = sel_ref[g, it * blocks_per_tile:(it + 1) * blocks_per_tile, :]
            else:
                rows = sel_ref[g, pl.ds(pl.multiple_of(it * blocks_per_tile, blocks_per_tile), blocks_per_tile), :]
            bias = jnp.concatenate([jnp.broadcast_to(rows[jj:jj + 1], (SLC_BLOCK, tq))
                                    for jj in range(blocks_per_tile)], axis=0)
            if need_causal:
                bias = jnp.where(rel + (q0 - k0) >= 0, bias, NEG)
            biases.append(tile_heads(bias - shift if bounded else bias))
        return biases

    def slc_body(it, carry, need_causal=True):
        k0 = it * tks if isinstance(it, int) else pl.multiple_of(it * tks, tks)
        biases = slc_bias(it, k0, need_causal)
        out = []
        for (g, _), q_t, state in zip(chains, q_ts, carry):
            sc = _dot(ks_ref[0, g, pl.ds(k0, tks), :], q_t) + biases[g]
            vt = vst_ref[0, g, :, pl.ds(k0, tks)]
            if bounded:
                (acc,) = state
                acc = acc + _dot(vt, jnp.exp2(sc).astype(BF16))
                out.append((acc,))
            else:
                m, acc = state
                m_new = jnp.maximum(m, jnp.max(sc, axis=0, keepdims=True))
                acc = jnp.exp2(m - m_new) * acc + _dot(vt, jnp.exp2(sc - m_new).astype(BF16))
                out.append((m_new, acc))
        return tuple(out)

    zero_state = (jnp.zeros((VT_ROWS, lanes), F32),)
    init = tuple(zero_state if bounded else (jnp.full((1, lanes), NEG, F32),) + zero_state for _ in chains)
    n_it = (q0 + tq + tks - 1) // tks
    if bounded:
        def slc_tiles(n):
            carry = init
            for it in range(n):
                carry = slc_body(it, carry, need_causal=(it == n - 1))
            return carry

        max_it = ks_ref.shape[2] // tks
        fin = lax.switch(n_it - 1, [functools.partial(slc_tiles, n) for n in range(1, max_it + 1)])
    else:
        fin = lax.fori_loop(0, n_it, slc_body, init)
    o_slc = [st[-1][:NSA_HD] * (1.0 / st[-1][NSA_HD:NSA_HD + 1]) for st in fin]

    heads_per_store = LANES // NSA_HD
    for ci, (_, h0) in enumerate(chains):
        for i0 in range(0, ch, heads_per_store):
            gated = []
            for i in range(i0, i0 + heads_per_store):
                cols = slice(i * tq, (i + 1) * tq)
                o_h = jnp.zeros((NSA_HD, tq), F32)
                for br, o_b in enumerate((o_cmp[ci], o_slc[ci], o_win[ci])):
                    row = (h0 + i) * N_BRANCH + br
                    o_h = o_h + gt_ref[0, row:row + 1, :] * o_b[:, cols]
                gated.append(o_h)
            lo = (h0 + i0) * NSA_HD
            o_ref[0, :, lo:lo + LANES] = jnp.concatenate(gated, axis=0).T.astype(o_ref.dtype)


def _nsa(q_t, kc, vct, ks, vst, kw, vwt, gates_t, mt, bound, tq, tks, bounded):
    bsz, _, _, t = q_t.shape
    n_blk = kc.shape[2]
    n_g = NSA_KV_HEADS
    full = lambda b, i: (b, 0, 0, 0)
    k_spec = pl.BlockSpec((1, n_g, t, NSA_HD), full)
    vt_spec = pl.BlockSpec((1, n_g, VT_ROWS, t), full)
    return pl.pallas_call(
        functools.partial(_nsa_kernel, tq=tq, tks=tks, bounded=bounded),
        grid=(bsz, t // tq),
        in_specs=[pl.BlockSpec(memory_space=pltpu.SMEM),
                  pl.BlockSpec((1, NSA_HEADS, NSA_HD, tq), lambda b, i: (b, 0, 0, i)),
                  pl.BlockSpec((1, n_g, n_blk, NSA_HD), full),
                  pl.BlockSpec((1, n_g, NSA_HD, n_blk), full),
                  k_spec, vt_spec, k_spec, vt_spec,
                  pl.BlockSpec((1, GATE_ROWS, tq), lambda b, i: (b, 0, i)),
                  pl.BlockSpec(mt.shape, lambda b, i: (0, 0))],
        out_specs=pl.BlockSpec((1, tq, NSA_WIDTH), lambda b, i: (b, i, 0)),
        out_shape=jax.ShapeDtypeStruct((bsz, t, NSA_WIDTH), BF16),
        scratch_shapes=[pltpu.VMEM((n_g, mt.shape[0], tq), F32)],
        compiler_params=pltpu.CompilerParams(
            dimension_semantics=("arbitrary", "arbitrary"), vmem_limit_bytes=VMEM_LIMIT),
        name="nsa_bounded" if bounded else "nsa",
    )(bound, q_t, kc, vct, ks, vst, kw, vwt, gates_t, mt)


def _causal_conv(u, prev, cw, cb):
    tm = u.shape[0]
    g = SUBLANES
    r8 = lax.broadcasted_iota(jnp.int32, (g, u.shape[1]), 0)
    wrap1 = jnp.where(r8 == 0, prev[2 * g - 1:2 * g], pltpu.roll(u[tm - g:], 1, axis=0))
    wrap2 = jnp.where(r8 == 0, prev[g - 1:g], pltpu.roll(u[tm - 2 * g:tm - g], 1, axis=0))
    u1 = jnp.concatenate([wrap1, u[:tm - g]], axis=0)
    u2 = jnp.concatenate([wrap2, wrap1, u[:tm - 2 * g]], axis=0)
    return cb + u2 * cw[0:1] + u1 * cw[1:2] + u * cw[2:3]


def _ffn_kernel(x_ref, hg_ref, ns_ref, mod_ref, wo_ref, g2_ref, wu_ref, cw_ref, cb_ref, wd_ref,
                o_ref, carry_ref, buf_ref, x1_ref, h2_ref, u_ref, g_ref, acc_ref, *, fc):
    n_sub, sub = x1_ref.shape[0], x1_ref.shape[1]
    n_lt = D_MODEL // LANES
    groups = sub // SUBLANES
    pitch = buf_ref.shape[2] // SUBLANES
    first = pl.program_id(1) == 0

    def head(s):
        rows = slice(s * sub, (s + 1) * sub)
        mix = _dot(hg_ref[0, rows], wo_ref[:HG_WIDTH, :]) + _dot(ns_ref[0, rows], wo_ref[HG_WIDTH:, :])
        x1_nat = x_ref[0, rows] + mod_ref[0, 2:3, :] * mix
        for c in range(n_lt):
            for sg in range(SUBLANES):
                buf_ref[s, c, sg * pitch:sg * pitch + groups] = x1_nat[sg * groups:(sg + 1) * groups,
                                                                       c * LANES:(c + 1) * LANES]
        x1 = jnp.concatenate(
            [jnp.concatenate([buf_ref[s, c, pl.ds(j, SUBLANES, stride=pitch), :] for j in range(groups)], axis=0)
             for c in range(n_lt)], axis=1)
        x1_ref[s] = x1
        y = x1 * lax.rsqrt(jnp.mean(x1 * x1, axis=-1, keepdims=True) + EPS) * g2_ref[...]
        h2_ref[s] = (y * (1.0 + mod_ref[0, 4:5, :]) + mod_ref[0, 3:4, :]).astype(h2_ref.dtype)
        acc_ref[s] = jnp.zeros((sub, D_MODEL), F32)

    def tail(s):
        out = x1_ref[s] + mod_ref[0, 5:6, :] * acc_ref[s]
        for c in range(n_lt):
            for j in range(groups):
                buf_ref[s, c, pl.ds(j, SUBLANES, stride=pitch), :] = out[j * SUBLANES:(j + 1) * SUBLANES,
                                                                         c * LANES:(c + 1) * LANES]
        for c in range(n_lt):
            for sg in range(SUBLANES):
                o_ref[0, s * sub + sg * groups:s * sub + (sg + 1) * groups, c * LANES:(c + 1) * LANES] = (
                    buf_ref[s, c, sg * pitch:sg * pitch + groups])

    n = D_FF // fc

    def up(c, slot):
        s, jc = divmod(c, n)
        for half in range(2):
            off = half * D_FF + jc * fc
            u_ref[slot, half] = _dot(h2_ref[s], wu_ref[:, off:off + fc])

    def act(c, slot):
        s, jc = divmod(c, n)
        halves = []
        for half in range(2):
            cols = slice(half * D_FF + jc * fc, half * D_FF + (jc + 1) * fc)
            u = u_ref[slot, half]
            prev = jnp.where(first, 0.0, carry_ref[:, cols]) if s == 0 else carry_ref[:, cols]
            carry_ref[:, cols] = u[sub - 2 * SUBLANES:]
            halves.append(_causal_conv(u, prev, cw_ref[:, cols], cb_ref[:, cols]))
        a, v = halves
        g_ref[slot] = (_silu(a) * v).astype(g_ref.dtype)

    def down(c, slot):
        s, jc = divmod(c, n)
        acc_ref[s] += _dot(g_ref[slot], wd_ref[jc * fc:(jc + 1) * fc, :])

    total = n_sub * n
    head(0)
    for i in range(total + 2):
        if 0 <= i - 2 < total:
            down(i - 2, i % 2)
        if i < total:
            up(i, i % 2)
        if 0 <= i - 1 < total:
            act(i - 1, (i - 1) % 2)
        s, jc = divmod(i, n)
        if jc == HEAD_AT and s + 1 < n_sub:
            head(s + 1)
        if jc == 1 and 1 <= s <= n_sub:
            tail(s - 1)


def _ffn(x, o_hg, o_nsa, mod, w_out, g2, w_up, conv_w, conv_b, w_down, tm, sub, fc):
    bsz, t, _ = x.shape
    row_spec = lambda w: pl.BlockSpec((1, tm, w), lambda b, i: (b, i, 0))
    groups = sub // SUBLANES
    pad = SUBLANES if (groups // SUBLANES) % 2 == 0 else 0
    n_sub = tm // sub
    resident = lambda a: pl.BlockSpec(a.shape, lambda b, i: (0, 0), pipeline_mode=pl.Buffered(1))
    return pl.pallas_call(
        functools.partial(_ffn_kernel, fc=fc),
        grid=(bsz, t // tm),
        in_specs=[row_spec(D_MODEL), row_spec(HG_WIDTH), row_spec(NSA_WIDTH),
                  pl.BlockSpec((1, 6, D_MODEL), lambda b, i: (b, 0, 0)),
                  resident(w_out), resident(g2), resident(w_up), resident(conv_w), resident(conv_b),
                  resident(w_down)],
        out_specs=row_spec(D_MODEL),
        out_shape=jax.ShapeDtypeStruct(x.shape, F32),
        scratch_shapes=[pltpu.VMEM((2 * SUBLANES, 2 * D_FF), F32),
                        pltpu.VMEM((n_sub, D_MODEL // LANES, sub + SUBLANES * pad, LANES), F32),
                        pltpu.VMEM((n_sub, sub, D_MODEL), F32),
                        pltpu.VMEM((n_sub, sub, D_MODEL), BF16),
                        pltpu.VMEM((2, 2, sub, fc), F32),
                        pltpu.VMEM((2, sub, fc), BF16),
                        pltpu.VMEM((n_sub, sub, D_MODEL), F32)],
        compiler_params=pltpu.CompilerParams(
            dimension_semantics=("arbitrary", "arbitrary"), vmem_limit_bytes=VMEM_LIMIT),
        name="ffn",
    )(x, o_hg, o_nsa, mod, w_out, g2, w_up, conv_w, conv_b, w_down)


def _rope_tables():
    half = ROPE_DIM // 2
    inv = ROPE_THETA ** (-jnp.arange(half, dtype=F32) * 2.0 / ROPE_DIM)
    return (jnp.tile(inv.reshape(half, 1), (1, LANES)),)


def _gain_t(g):
    return jnp.tile(g.reshape(NSA_HD, 1), (LANES // NSA_HD, LANES))


def _selection_tables(t):
    n_seg = t // CMP_STRIDE
    nb = t // SLC_BLOCK
    cst = np.arange(n_seg) * CMP_STRIDE
    sst = np.arange(nb) * SLC_BLOCK
    ovl = np.clip(np.minimum(cst[:, None] + CMP_BLOCK, sst[None] + SLC_BLOCK)
                  - np.maximum(cst[:, None], sst[None]), 0, None) / CMP_BLOCK
    ovl[(t - CMP_BLOCK) // CMP_STRIDE + 1:] = 0.0
    return (jnp.asarray(ovl.T, dtype=BF16),)


def _layer(x, mod, pos_row, l, p, tables):
    bsz, t, _ = x.shape
    inv_t, mt = tables
    w_in_p = p["w_in"][l].astype(BF16)
    qg_t = _gain_t(p["q_norm_g"][l])
    kg_t = jnp.stack([_gain_t(p["k_norm_g"][l, br]) for br in range(N_BRANCH)])
    q_t, kc, vc, ks, vst, kw, vwt, gates_t, o_hg = _mix(
        x, mod, p["norm1_g"][l].reshape(1, D_MODEL), w_in_p, pos_row, inv_t, qg_t, kg_t,
        p["lb_logits"], p["hg_norm_g"][l].reshape(1, HG_HD), l, tm=TM_MIX)

    pe2 = p["pe_cmp"][l].reshape(2, 2, CMP_STRIDE * NSA_HD)
    kcmp, vcmp_t = _compress(kc, vc, pe2, p["w_cmp1"][l].astype(BF16), p["w_cmp2"][l].astype(BF16))
    bound = (SCORE_BOUND_MARGIN * NSA_HD ** 0.5 * jnp.max(jnp.abs(p["q_norm_g"][l]))
             * jnp.max(jnp.abs(p["k_norm_g"][l, 1:]))).reshape(1).astype(F32)
    nsa_args = (q_t, kcmp, vcmp_t, ks, vst, kw, vwt, gates_t, mt, bound)
    o_nsa = lax.cond(bound[0] <= MAX_SCORE_BOUND,
                     lambda a: _nsa(*a, tq=TQ_NSA, tks=TKS_NSA, bounded=True),
                     lambda a: _nsa(*a, tq=TQ_NSA, tks=TKS_NSA, bounded=False), nsa_args)

    return _ffn(x, o_hg, o_nsa, mod, p["w_out"][l].astype(BF16), p["norm2_g"][l].reshape(1, D_MODEL),
                p["w_up"][l].astype(BF16), p["conv_w"][l], p["conv_b"][l].reshape(1, 2 * D_FF),
                p["w_down"][l].astype(BF16), tm=TM_FFN, sub=SUB_FFN, fc=FC_FFN)


def kernel(x, c, positions, w_ada, b_ada, norm1_g, w_in, lb_logits, hg_norm_g, q_norm_g, k_norm_g, pe_cmp, w_cmp1, w_cmp2, w_out, norm2_g, w_up, conv_w, conv_b, w_down):
    p = dict(w_in=w_in, norm1_g=norm1_g, lb_logits=lb_logits, hg_norm_g=hg_norm_g, q_norm_g=q_norm_g,
             k_norm_g=k_norm_g, pe_cmp=pe_cmp, w_cmp1=w_cmp1, w_cmp2=w_cmp2, w_out=w_out,
             norm2_g=norm2_g, w_up=w_up, conv_w=conv_w, conv_b=conv_b, w_down=w_down)
    bsz, t, _ = x.shape
    assert x.shape[2] == D_MODEL and t % TM_MIX == 0 and t % TKS_NSA == 0 and t >= WIN + TQ_NSA
    assert TKS_NSA % TQ_NSA == 0 and TM_MIX % HG_CHUNK == 0 and D_FF % FC_FFN == 0 and TM_FFN % SUB_FFN == 0
    tables = _rope_tables() + _selection_tables(t)
    pos_row = positions.reshape(bsz, 1, t)
    for l in range(w_ada.shape[0]):
        mod = _ada(c, w_ada[l], b_ada[l]).reshape(bsz, 6, D_MODEL)
        x = _layer(x, mod, pos_row, l, p, tables)
    return x
```

```python
import functools

import jax
import jax.numpy as jnp
import numpy as np
from jax import lax
from jax.experimental import pallas as pl
from jax.experimental.pallas import tpu as pltpu

D_MODEL = 1024
HG_HEADS = 4
HG_HD = 128
HG_WIDTH = HG_HEADS * HG_HD
HG_CHUNK = 128
HG_SUB = 8
LOG2E = 1.4426950408889634
NSA_HEADS = 8
NSA_KV_HEADS = 2
NSA_HD = 64
NSA_GROUP = NSA_HEADS // NSA_KV_HEADS
NSA_CHAIN = 4
RANK_LANES = 4
NSA_WIDTH = NSA_HEADS * NSA_HD
N_BRANCH = 3
CMP_BLOCK = 32
CMP_STRIDE = 16
CMP_HIDDEN = 256
SLC_BLOCK = 64
SLC_TOPK = 16
WIN = 512
ROPE_DIM = NSA_HD // 4
ROPE_THETA = 500000.0
D_FF = 2816
CONV_W = 3
EPS = 1e-6
NEG = -1e30
SCORE_BOUND_MARGIN = 1.02
MAX_SCORE_BOUND = 40.0

LANES = 128
SUBLANES = 8
VMEM_LIMIT = 56 * 1024 * 1024

TM_MIX = 512
TQ_NSA = 256
TKS_NSA = 512
TM_FFN = 512
SUB_FFN = 256
HEAD_AT = 7
FC_FFN = 256

OFF_HG = 0
OFF_Q = 4 * HG_WIDTH
OFF_KV = OFF_Q + NSA_WIDTH
OFF_G = OFF_KV + 6 * NSA_KV_HEADS * NSA_HD
IN_COLS = OFF_G + N_BRANCH * NSA_HEADS
GATE_ROWS = N_BRANCH * NSA_HEADS
VT_ROWS = NSA_HD + SUBLANES

BF16 = jnp.bfloat16
F32 = jnp.float32


def _dot(a, b):
    return jnp.dot(a, b, preferred_element_type=F32)


def _dot_nt(a, b):
    return lax.dot_general(a, b, (((1,), (1,)), ((), ())), preferred_element_type=F32)


def _exp_neg(x):
    return jnp.exp2(x * (-LOG2E))


def _sigmoid(x):
    return 1.0 / (1.0 + _exp_neg(x))


def _silu(x):
    return x * _sigmoid(x)


def _ada_kernel(c_ref, w_ref, b_ref, o_ref):
    cs = _silu(c_ref[...])
    o_ref[...] = _dot(cs.astype(BF16), w_ref[...].astype(BF16)) + b_ref[...]


def _ada(c, w, b):
    bsz = c.shape[0]
    n = w.shape[1]
    tn = D_MODEL
    return pl.pallas_call(
        _ada_kernel,
        grid=(n // tn,),
        in_specs=[pl.BlockSpec((bsz, D_MODEL), lambda j: (0, 0)),
                  pl.BlockSpec((D_MODEL, tn), lambda j: (0, j)),
                  pl.BlockSpec((1, tn), lambda j: (0, j))],
        out_specs=pl.BlockSpec((bsz, tn), lambda j: (0, j)),
        out_shape=jax.ShapeDtypeStruct((bsz, n), F32),
        name="ada",
    )(c, w, b.reshape(1, n))


def _pair_norm_rope_t(xt, g_t, cos_t, sin_t):
    half = ROPE_DIM // 2
    outs = []
    for hh in range(2):
        x = xt[hh * NSA_HD:(hh + 1) * NSA_HD]
        ms = jnp.mean(x * x, axis=0, keepdims=True)
        xn = x * lax.rsqrt(ms + EPS) * g_t[hh * NSA_HD:(hh + 1) * NSA_HD]
        x1, x2 = xn[:half], xn[half:ROPE_DIM]
        outs += [x1 * cos_t - x2 * sin_t, x2 * cos_t + x1 * sin_t, xn[ROPE_DIM:]]
    return jnp.concatenate(outs, axis=0)


def _project_parts(x_ref, mod_ref, g1_ref, w_ref, pos_ref, inv_ref, qg_ref, kg_ref,
                   zhg_out, qt_ref, kc_ref, vc_ref, ks_ref, vst_ref, kw_ref, vwt_ref, gt_ref, seg_ref):
    x = x_ref[0]
    ms = jnp.mean(x * x, axis=-1, keepdims=True)
    y = x * lax.rsqrt(ms + EPS) * g1_ref[...]
    h = (y * (1.0 + mod_ref[0, 1:2, :]) + mod_ref[0, 0:1, :]).astype(BF16)

    tm = x.shape[0]
    reps = tm // LANES

    def lane_tile(a):
        return jnp.concatenate([a] * reps, axis=1)

    ang = lane_tile(inv_ref[...]) * pos_ref[0].astype(F32)
    cos_t = jnp.cos(ang)
    sin_t = jnp.sin(ang)
    scale = NSA_HD ** -0.5 * LOG2E
    vals = {}

    def hg_cols(lo, hi):
        def part():
            zhg_out[:, lo:hi] = _dot(h, w_ref[:, OFF_HG + lo:OFF_HG + hi])
        return part

    def q_pair(p):
        def part():
            if p % 2 == 0:
                vals["zq"] = _dot(h, w_ref[:, OFF_Q + p * LANES:OFF_Q + (p + 2) * LANES])
                vals["qg"] = lane_tile(qg_ref[...])
            off = (p % 2) * LANES
            rt = _pair_norm_rope_t(vals["zq"][:, off:off + LANES].T, vals["qg"], cos_t, sin_t)
            rt = (rt * scale).astype(qt_ref.dtype)
            qt_ref[0, 2 * p] = rt[:NSA_HD]
            qt_ref[0, 2 * p + 1] = rt[NSA_HD:]
        return part

    def kv_dot(br):
        vals["zkv"] = _dot(h, w_ref[:, OFF_KV + 2 * br * LANES:OFF_KV + (2 * br + 2) * LANES])

    def store_segments(a, out_ref):
        seg_ref[...] = a
        n_seg = tm // CMP_STRIDE
        lane = lax.broadcasted_iota(jnp.int32, (n_seg, LANES), 1)
        for l in range(0, CMP_STRIDE, 2):
            even = seg_ref[pl.ds(l, n_seg, stride=CMP_STRIDE), :]
            odd = seg_ref[pl.ds(l + 1, n_seg, stride=CMP_STRIDE), :]
            cols = slice((l // 2) * LANES, (l // 2 + 1) * LANES)
            out_ref[0, 0, :, cols] = jnp.where(lane < NSA_HD, even, pltpu.roll(odd, NSA_HD, axis=1))
            out_ref[0, 1, :, cols] = jnp.where(lane < NSA_HD, pltpu.roll(even, NSA_HD, axis=1), odd)

    def key(br, k_ref):
        def part():
            kv_dot(br)
            kt = _pair_norm_rope_t(vals["zkv"][:, :LANES].T, lane_tile(kg_ref[br]), cos_t, sin_t)
            if br == 0:
                store_segments(kt.T, k_ref)
                store_segments(vals["zkv"][:, LANES:], vc_ref)
            else:
                kk = kt.T.astype(k_ref.dtype)
                for g in range(NSA_KV_HEADS):
                    k_ref[0, g] = kk[:, g * NSA_HD:(g + 1) * NSA_HD]
        return part

    def value_t(vt_ref):
        def part():
            vt = vals["zkv"][:, LANES:].T.astype(vt_ref.dtype)
            ones_row = jnp.where(lax.broadcasted_iota(jnp.int32, (SUBLANES, tm), 0) == 0, 1.0, 0.0).astype(vt_ref.dtype)
            for g in range(NSA_KV_HEADS):
                vt_ref[0, g, :NSA_HD] = vt[g * NSA_HD:(g + 1) * NSA_HD]
                vt_ref[0, g, NSA_HD:] = ones_row
        return part

    def gates_part():
        gates = _sigmoid(_dot(h, w_ref[:, OFF_G:IN_COLS]))
        wide = jnp.concatenate([gates, jnp.zeros((tm, LANES - GATE_ROWS), F32)], axis=1)
        gt_ref[0] = wide.T[:GATE_ROWS]

    wide = 2 * LANES
    hg = [hg_cols(lo, lo + wide) for lo in range(0, 4 * HG_WIDTH, wide)]
    rest = [q_pair(0), q_pair(1), q_pair(2), q_pair(3), key(0, kc_ref), key(1, ks_ref), value_t(vst_ref),
            key(2, kw_ref), value_t(vwt_ref), gates_part]
    parts = []
    for k in range(max(len(hg), len(rest))):
        parts += hg[k:k + 1] + rest[k:k + 1]
    return parts


def _hgrn_chunk(zq, zf, zi, zg, lb, hg_g, st_ref, bk_ref, h, tri, level):
    c = HG_CHUNK
    e_z = _exp_neg(jnp.abs(zf))
    logsig = jnp.minimum(zf, 0.0) - jnp.log(1.0 + e_z)
    a = jnp.log(lb)
    bb = jnp.log1p(-lb) + logsig
    logf = jnp.maximum(a, bb) + jnp.log(1.0 + _exp_neg(jnp.abs(a - bb)))
    k = (1.0 - lb) * (jnp.where(zf >= 0.0, e_z, 1.0) / (1.0 + e_z))
    q = _silu(zq)
    v = zi
    l_hi = logf.astype(BF16)
    l_mid = (logf - l_hi.astype(F32)).astype(BF16)
    l_lo = (logf - l_hi.astype(F32) - l_mid.astype(F32)).astype(BF16)
    bc = _dot(tri, l_hi) + _dot(tri, l_mid) + _dot(tri, l_lo)

    col = lax.broadcasted_iota(jnp.int32, (HG_SUB, c), 1)
    b2 = bc * LOG2E
    bk_ref[0, h] = b2
    bk_ref[1, h] = k
    rows_a = []
    for i in range(c // HG_SUB):
        lo = i * HG_SUB
        b_i = b2[lo:lo + HG_SUB]
        q_i = q[lo:lo + HG_SUB]
        a_i = jnp.zeros((HG_SUB, c), F32)
        for s in range(HG_SUB):
            b_s = bk_ref[0, h, lo + s:lo + s + 1, :]
            k_s = bk_ref[1, h, lo + s:lo + s + 1, :]
            e = jnp.exp2(jnp.minimum(b_i - b_s, 0.0)) * q_i * k_s
            a_i = jnp.where(col == lo + s, jnp.sum(e, axis=-1, keepdims=True), a_i)
        rows_a.append(a_i)
    amat = jnp.concatenate(rows_a, axis=0)

    size, idx = c // 2, 1
    while size >= HG_SUB:
        pieces = []
        for e0 in range(0, c, 2 * size):
            o0 = e0 + size
            r = b2[o0:o0 + 1]
            pieces.append(k[e0:o0] * jnp.exp2(r - b2[e0:o0]))
            pieces.append(q[o0:o0 + size] * jnp.exp2(b2[o0:o0 + size] - r))
        hmat = jnp.concatenate(pieces, axis=0).astype(BF16)
        amat = jnp.where(level == idx, _dot_nt(hmat, hmat), amat)
        size, idx = size // 2, idx + 1
    amat = jnp.where(level >= 0, amat, 0.0)

    st = st_ref[h]
    o = _dot_nt((q * jnp.exp2(b2)).astype(BF16), st.astype(BF16)) + _dot(amat.astype(BF16), v.astype(BF16))
    bl = b2[c - 1:c]
    kdec = (k * jnp.exp2(bl - b2)).astype(BF16)
    st_ref[h] = jnp.exp2(bl) * st + _dot(v.T.astype(BF16), kdec)

    y = o * lax.rsqrt(jnp.mean(o * o, axis=-1, keepdims=True) + EPS) * hg_g
    return y * _silu(zg)


def _hgrn_parts(z_in, lbl_ref, g_ref, o_ref, st_ref, bk_ref, l_idx):
    lg = lbl_ref[...]
    ex = jnp.exp(lg - jnp.max(lg, axis=0, keepdims=True))
    sm = ex / jnp.sum(ex, axis=0, keepdims=True)
    lb_all = jnp.sum(sm[:l_idx + 1], axis=0, keepdims=True)

    c = HG_CHUNK
    ri = lax.broadcasted_iota(jnp.int32, (c, c), 0)
    ci = lax.broadcasted_iota(jnp.int32, (c, c), 1)
    tri = jnp.where(ci <= ri, 1.0, 0.0).astype(BF16)
    level = jnp.where(ci > ri, -1, 0)
    size, idx = c // 2, 1
    while size >= HG_SUB:
        sh = size.bit_length() - 1
        paired = ((ri >> (sh + 1)) == (ci >> (sh + 1))) & ((ri >> sh) != (ci >> sh)) & (ci <= ri)
        level = jnp.where(paired, idx, level)
        size, idx = size // 2, idx + 1

    def chunk_head(ch, h):
        def part():
            rows = slice(ch * c, (ch + 1) * c)
            sl = slice(h * HG_HD, (h + 1) * HG_HD)
            zq = z_in[rows, h * HG_HD:(h + 1) * HG_HD]
            zf = z_in[rows, HG_WIDTH + h * HG_HD:HG_WIDTH + (h + 1) * HG_HD]
            zi = z_in[rows, 2 * HG_WIDTH + h * HG_HD:2 * HG_WIDTH + (h + 1) * HG_HD]
            zg = z_in[rows, 3 * HG_WIDTH + h * HG_HD:3 * HG_WIDTH + (h + 1) * HG_HD]
            y = _hgrn_chunk(zq, zf, zi, zg, lb_all[:, sl], g_ref[...], st_ref, bk_ref, h, tri, level)
            o_ref[0, rows, h * HG_HD:(h + 1) * HG_HD] = y.astype(o_ref.dtype)
        return part

    return [chunk_head(ch, h) for ch in range(z_in.shape[0] // c) for h in range(HG_HEADS)]


def _mix_kernel(x_ref, mod_ref, g1_ref, w_ref, pos_ref, inv_ref, qg_ref, kg_ref, lbl_ref, hgg_ref,
                qt_ref, kc_ref, vc_ref, ks_ref, vst_ref, kw_ref, vwt_ref, gt_ref, ohg_ref,
                zbuf_ref, st_ref, bk_ref, seg_ref, *, l_idx, tiles_per_seq):
    j = pl.program_id(0)

    @pl.when(j == 0)
    def _():
        zbuf_ref[...] = jnp.zeros_like(zbuf_ref)

    @pl.when((j == 0) | (lax.rem(j + tiles_per_seq - 1, tiles_per_seq) == 0))
    def _():
        st_ref[...] = jnp.zeros_like(st_ref)

    rec = _hgrn_parts(zbuf_ref, lbl_ref, hgg_ref, ohg_ref, st_ref, bk_ref, l_idx)
    proj = _project_parts(x_ref, mod_ref, g1_ref, w_ref, pos_ref, inv_ref, qg_ref, kg_ref, zbuf_ref,
                          qt_ref, kc_ref, vc_ref, ks_ref, vst_ref, kw_ref, vwt_ref, gt_ref, seg_ref)
    for part in rec + proj:
        part()


def _mix(x, mod, g1, w_in_p, pos_row, inv_t, qg_t, kg_t, lb_logits, hg_g, l_idx, tm):
    bsz, t, _ = x.shape
    nm = t // tm
    n_tiles = bsz * nm

    def cur(j):
        jc = jnp.minimum(j, n_tiles - 1)
        return jc // nm, jc % nm

    def prev(j):
        jp = jnp.maximum(j - 1, 0)
        return jp // nm, jp % nm

    kv_shape = (bsz, NSA_KV_HEADS, t, NSA_HD)
    kv_spec = pl.BlockSpec((1, NSA_KV_HEADS, tm, NSA_HD), lambda j: (cur(j)[0], 0, cur(j)[1], 0))
    seg_w = CMP_STRIDE * NSA_HD
    seg_shape = (bsz, NSA_KV_HEADS, t // CMP_STRIDE, seg_w)
    seg_spec = pl.BlockSpec((1, NSA_KV_HEADS, tm // CMP_STRIDE, seg_w), lambda j: (cur(j)[0], 0, cur(j)[1], 0))
    vt_shape = (bsz, NSA_KV_HEADS, VT_ROWS, t)
    vt_spec = pl.BlockSpec((1, NSA_KV_HEADS, VT_ROWS, tm), lambda j: (cur(j)[0], 0, 0, cur(j)[1]))
    const = lambda j: (0, 0)
    return pl.pallas_call(
        functools.partial(_mix_kernel, l_idx=l_idx, tiles_per_seq=nm),
        grid=(n_tiles + 1,),
        in_specs=[pl.BlockSpec((1, tm, D_MODEL), lambda j: (cur(j)[0], cur(j)[1], 0)),
                  pl.BlockSpec((1, 6, D_MODEL), lambda j: (cur(j)[0], 0, 0)),
                  pl.BlockSpec((1, D_MODEL), const),
                  pl.BlockSpec((D_MODEL, IN_COLS), const),
                  pl.BlockSpec((1, 1, tm), lambda j: (cur(j)[0], 0, cur(j)[1])),
                  pl.BlockSpec(inv_t.shape, const),
                  pl.BlockSpec(qg_t.shape, const),
                  pl.BlockSpec(kg_t.shape, lambda j: (0, 0, 0)),
                  pl.BlockSpec(lb_logits.shape, const),
                  pl.BlockSpec((1, HG_HD), const)],
        out_specs=[pl.BlockSpec((1, NSA_HEADS, NSA_HD, tm), lambda j: (cur(j)[0], 0, 0, cur(j)[1])),
                   seg_spec, seg_spec, kv_spec, vt_spec, kv_spec, vt_spec,
                   pl.BlockSpec((1, GATE_ROWS, tm), lambda j: (cur(j)[0], 0, cur(j)[1])),
                   pl.BlockSpec((1, tm, HG_WIDTH), lambda j: (prev(j)[0], prev(j)[1], 0))],
        out_shape=[jax.ShapeDtypeStruct((bsz, NSA_HEADS, NSA_HD, t), BF16),
                   jax.ShapeDtypeStruct(seg_shape, F32),
                   jax.ShapeDtypeStruct(seg_shape, F32),
                   jax.ShapeDtypeStruct(kv_shape, BF16),
                   jax.ShapeDtypeStruct(vt_shape, BF16),
                   jax.ShapeDtypeStruct(kv_shape, BF16),
                   jax.ShapeDtypeStruct(vt_shape, BF16),
                   jax.ShapeDtypeStruct((bsz, GATE_ROWS, t), F32),
                   jax.ShapeDtypeStruct((bsz, t, HG_WIDTH), BF16)],
        scratch_shapes=[pltpu.VMEM((tm, 4 * HG_WIDTH), F32),
                        pltpu.VMEM((HG_HEADS, HG_HD, HG_HD), F32),
                        pltpu.VMEM((2, HG_HEADS, HG_CHUNK, HG_HD), F32),
                        pltpu.VMEM((tm, LANES), F32)],
        compiler_params=pltpu.CompilerParams(
            dimension_semantics=("arbitrary",), vmem_limit_bytes=VMEM_LIMIT),
        name="mix",
    )(x, mod, g1, w_in_p, pos_row, inv_t, qg_t, kg_t, lb_logits, hg_g)


def _compress_kernel(xk_ref, xv_ref, pe_ref, w1_ref, w2_ref, ko_ref, vo_ref):
    half = CMP_STRIDE * NSA_HD
    n_g, nseg = xk_ref.shape[1], xk_ref.shape[2]
    outs = []
    for kv, x_ref in enumerate((xk_ref, xv_ref)):
        x = x_ref[0].reshape(n_g * nseg, half)
        ha = _dot((x + pe_ref[kv, 0:1, :]).astype(BF16), w1_ref[kv, :half, :])
        hb = _dot((x + pe_ref[kv, 1:2, :]).astype(BF16), w1_ref[kv, half:, :])
        pre = ha + pltpu.roll(hb, n_g * nseg - 1, axis=0)
        outs.append(_dot(_silu(pre).astype(BF16), w2_ref[kv]))
    for g in range(n_g):
        rows = slice(g * nseg, (g + 1) * nseg)
        ko_ref[0, g] = outs[0][rows].astype(ko_ref.dtype)
        vo_ref[0, g] = outs[1][rows].T.astype(vo_ref.dtype)


def _compress(xk, xv, pe2, w1, w2):
    bsz, g, nseg, width = xk.shape
    x_spec = pl.BlockSpec((1, g, nseg, width), lambda b: (b, 0, 0, 0))
    return pl.pallas_call(
        _compress_kernel,
        grid=(bsz,),
        in_specs=[x_spec, x_spec,
                  pl.BlockSpec(pe2.shape, lambda b: (0, 0, 0)),
                  pl.BlockSpec(w1.shape, lambda b: (0, 0, 0)),
                  pl.BlockSpec(w2.shape, lambda b: (0, 0, 0))],
        out_specs=[pl.BlockSpec((1, g, nseg, NSA_HD), lambda b: (b, 0, 0, 0)),
                   pl.BlockSpec((1, g, NSA_HD, nseg), lambda b: (b, 0, 0, 0))],
        out_shape=[jax.ShapeDtypeStruct((bsz, g, nseg, NSA_HD), BF16),
                   jax.ShapeDtypeStruct((bsz, g, NSA_HD, nseg), BF16)],
        compiler_params=pltpu.CompilerParams(
            dimension_semantics=("arbitrary",), vmem_limit_bytes=VMEM_LIMIT),
        name="compress",
    )(xk, xv, pe2, w1, w2)


def _nsa_kernel(bound_ref, qt_ref, kc_ref, vct_ref, ks_ref, vst_ref, kw_ref, vwt_ref, gt_ref, mt_ref,
                o_ref, sel_ref, *, tq, tks, bounded):
    r = NSA_GROUP
    n_g = NSA_KV_HEADS
    ch = NSA_CHAIN
    lanes = ch * tq
    q0 = pl.program_id(1) * tq
    chains = [(g, g * r + c * ch) for g in range(n_g) for c in range(r // ch)]
    q_ts = [jnp.concatenate([qt_ref[0, h0 + i] for i in range(ch)], axis=1)
            for _, h0 in chains]

    def tile_heads(a):
        return jnp.concatenate([a] * ch, axis=1)

    shift = bound_ref[0] * LOG2E if bounded else None

    span = WIN + tq
    start = pl.multiple_of(jnp.maximum(q0 - WIN, 0), tq)
    dist = (lax.broadcasted_iota(jnp.int32, (span, tq), 1)
            - lax.broadcasted_iota(jnp.int32, (span, tq), 0)) + (q0 - start)
    wbias = jnp.where((dist >= 0) & (dist < WIN), 0.0, NEG)
    wbias = tile_heads(wbias - shift if bounded else wbias)
    o_win = []
    for (g, _), q_t in zip(chains, q_ts):
        sw = _dot(kw_ref[0, g, pl.ds(start, span), :], q_t) + wbias
        ew = jnp.exp2(sw if bounded else sw - jnp.max(sw, axis=0, keepdims=True))
        ow = _dot(vwt_ref[0, g, :, pl.ds(start, span)], ew.astype(BF16))
        o_win.append(ow[:NSA_HD] * (1.0 / ow[NSA_HD:NSA_HD + 1]))

    n_blk = kc_ref.shape[2]
    nb = mt_ref.shape[0]
    blk_end = lax.broadcasted_iota(jnp.int32, (n_blk, tq), 0) * CMP_STRIDE + (CMP_BLOCK - 1)
    t_row = q0 + lax.broadcasted_iota(jnp.int32, (1, tq), 1)
    cvalid = tile_heads(jnp.where(blk_end <= t_row, 1.0, 0.0)) > 0.5
    some = tile_heads((t_row >= CMP_BLOCK - 1).astype(F32))
    j = lax.broadcasted_iota(jnp.int32, (nb, tq), 0)
    cur = jnp.right_shift(t_row, SLC_BLOCK.bit_length() - 1)
    forced = (j == 0) | (j == cur) | (j == cur - 1)
    o_cmp = []
    psum = [None] * n_g
    for (g, _), q_t in zip(chains, q_ts):
        s = jnp.where(cvalid, _dot(kc_ref[0, g], q_t), NEG)
        e = jnp.exp2(s - jnp.max(s, axis=0, keepdims=True))
        p = e * (some / jnp.sum(e, axis=0, keepdims=True))
        o_cmp.append(_dot(vct_ref[0, g], p.astype(BF16)))
        for i in range(ch):
            part = p[:, i * tq:(i + 1) * tq]
            psum[g] = part if psum[g] is None else psum[g] + part
    for g in range(n_g):
        p_hi = psum[g].astype(BF16)
        p_lo = (psum[g] - p_hi.astype(F32)).astype(BF16)
        imp = _dot(mt_ref[...], p_hi) + _dot(mt_ref[...], p_lo)
        imp = jnp.where(j <= cur, jnp.where(forced, jnp.inf, imp), -1.0)
        ranks = [jnp.zeros((nb, tq), jnp.int32) for _ in range(RANK_LANES)]
        for i in range(nb):
            row_i = imp[i:i + 1, :]
            ahead = (row_i > imp) | ((row_i == imp) & (j > i))
            ranks[i % RANK_LANES] = ranks[i % RANK_LANES] + ahead.astype(jnp.int32)
        rank = functools.reduce(lambda a, b: a + b, ranks)
        sel_ref[g] = jnp.where(rank < min(SLC_TOPK, nb), 0.0, NEG)

    blocks_per_tile = tks // SLC_BLOCK
    rel = (lax.broadcasted_iota(jnp.int32, (tks, tq), 1)
           - lax.broadcasted_iota(jnp.int32, (tks, tq), 0))

    def slc_bias(it, k0, need_causal):
        biases = []
        for g in range(n_g):
            if isinstance(it, int):
                rows = sel_ref[g, it * blocks_per_tile:(it + 1) * blocks_per_tile, :]
            else:
                rows = sel_ref[g, pl.ds(pl.multiple_of(it * blocks_per_tile, blocks_per_tile), blocks_per_tile), :]
            bias = jnp.concatenate([jnp.broadcast_to(rows[jj:jj + 1], (SLC_BLOCK, tq))
                                    for jj in range(blocks_per_tile)], axis=0)
            if need_causal:
                bias = jnp.where(rel + (q0 - k0) >= 0, bias, NEG)
            biases.append(tile_heads(bias - shift if bounded else bias))
        return biases

    def slc_body(it, carry, need_causal=True):
        k0 = it * tks if isinstance(it, int) else pl.multiple_of(it * tks, tks)
        biases = slc_bias(it, k0, need_causal)
        out = []
        for (g, _), q_t, state in zip(chains, q_ts, carry):
            sc = _dot(ks_ref[0, g, pl.ds(k0, tks), :], q_t) + biases[g]
            vt = vst_ref[0, g, :, pl.ds(k0, tks)]
            if bounded:
                (acc,) = state
                acc = acc + _dot(vt, jnp.exp2(sc).astype(BF16))
                out.append((acc,))
            else:
                m, acc = state
                m_new = jnp.maximum(m, jnp.max(sc, axis=0, keepdims=True))
                acc = jnp.exp2(m - m_new) * acc + _dot(vt, jnp.exp2(sc - m_new).astype(BF16))
                out.append((m_new, acc))
        return tuple(out)

    zero_state = (jnp.zeros((VT_ROWS, lanes), F32),)
    init = tuple(zero_state if bounded else (jnp.full((1, lanes), NEG, F32),) + zero_state for _ in chains)
    n_it = (q0 + tq + tks - 1) // tks
    if bounded:
        def slc_tiles(n):
            carry = init
            for it in range(n):
                carry = slc_body(it, carry, need_causal=(it == n - 1))
            return carry

        max_it = ks_ref.shape[2] // tks
        fin = lax.switch(n_it - 1, [functools.partial(slc_tiles, n) for n in range(1, max_it + 1)])
    else:
        fin = lax.fori_loop(0, n_it, slc_body, init)
    o_slc = [st[-1][:NSA_HD] * (1.0 / st[-1][NSA_HD:NSA_HD + 1]) for st in fin]

    heads_per_store = LANES // NSA_HD
    for ci, (_, h0) in enumerate(chains):
        for i0 in range(0, ch, heads_per_store):
            gated = []
            for i in range(i0, i0 + heads_per_store):
                cols = slice(i * tq, (i + 1) * tq)
                o_h = jnp.zeros((NSA_HD, tq), F32)
                for br, o_b in enumerate((o_cmp[ci], o_slc[ci], o_win[ci])):
                    row = (h0 + i) * N_BRANCH + br
                    o_h = o_h + gt_ref[0, row:row + 1, :] * o_b[:, cols]
                gated.append(o_h)
            lo = (h0 + i0) * NSA_HD
            o_ref[0, :, lo:lo + LANES] = jnp.concatenate(gated, axis=0).T.astype(o_ref.dtype)


def _nsa(q_t, kc, vct, ks, vst, kw, vwt, gates_t, mt, bound, tq, tks, bounded):
    bsz, _, _, t = q_t.shape
    n_blk = kc.shape[2]
    n_g = NSA_KV_HEADS
    full = lambda b, i: (b, 0, 0, 0)
    k_spec = pl.BlockSpec((1, n_g, t, NSA_HD), full)
    vt_spec = pl.BlockSpec((1, n_g, VT_ROWS, t), full)
    return pl.pallas_call(
        functools.partial(_nsa_kernel, tq=tq, tks=tks, bounded=bounded),
        grid=(bsz, t // tq),
        in_specs=[pl.BlockSpec(memory_space=pltpu.SMEM),
                  pl.BlockSpec((1, NSA_HEADS, NSA_HD, tq), lambda b, i: (b, 0, 0, i)),
                  pl.BlockSpec((1, n_g, n_blk, NSA_HD), full),
                  pl.BlockSpec((1, n_g, NSA_HD, n_blk), full),
                  k_spec, vt_spec, k_spec, vt_spec,
                  pl.BlockSpec((1, GATE_ROWS, tq), lambda b, i: (b, 0, i)),
                  pl.BlockSpec(mt.shape, lambda b, i: (0, 0))],
        out_specs=pl.BlockSpec((1, tq, NSA_WIDTH), lambda b, i: (b, i, 0)),
        out_shape=jax.ShapeDtypeStruct((bsz, t, NSA_WIDTH), BF16),
        scratch_shapes=[pltpu.VMEM((n_g, mt.shape[0], tq), F32)],
        compiler_params=pltpu.CompilerParams(
            dimension_semantics=("arbitrary", "arbitrary"), vmem_limit_bytes=VMEM_LIMIT),
        name="nsa_bounded" if bounded else "nsa",
    )(bound, q_t, kc, vct, ks, vst, kw, vwt, gates_t, mt)


def _causal_conv(u, prev, cw, cb):
    tm = u.shape[0]
    g = SUBLANES
    r8 = lax.broadcasted_iota(jnp.int32, (g, u.shape[1]), 0)
    wrap1 = jnp.where(r8 == 0, prev[2 * g - 1:2 * g], pltpu.roll(u[tm - g:], 1, axis=0))
    wrap2 = jnp.where(r8 == 0, prev[g - 1:g], pltpu.roll(u[tm - 2 * g:tm - g], 1, axis=0))
    u1 = jnp.concatenate([wrap1, u[:tm - g]], axis=0)
    u2 = jnp.concatenate([wrap2, wrap1, u[:tm - 2 * g]], axis=0)
    return cb + u2 * cw[0:1] + u1 * cw[1:2] + u * cw[2:3]


def _ffn_kernel(x_ref, hg_ref, ns_ref, mod_ref, wo_ref, g2_ref, wu_ref, cw_ref, cb_ref, wd_ref,
                o_ref, carry_ref, buf_ref, x1_ref, h2_ref, u_ref, g_ref, acc_ref, *, fc):
    n_sub, sub = x1_ref.shape[0], x1_ref.shape[1]
    n_lt = D_MODEL // LANES
    groups = sub // SUBLANES
    pitch = buf_ref.shape[2] // SUBLANES
    first = pl.program_id(1) == 0

    def head(s):
        rows = slice(s * sub, (s + 1) * sub)
        mix = _dot(hg_ref[0, rows], wo_ref[:HG_WIDTH, :]) + _dot(ns_ref[0, rows], wo_ref[HG_WIDTH:, :])
        x1_nat = x_ref[0, rows] + mod_ref[0, 2:3, :] * mix
        for c in range(n_lt):
            for sg in range(SUBLANES):
                buf_ref[s, c, sg * pitch:sg * pitch + groups] = x1_nat[sg * groups:(sg + 1) * groups,
                                                                       c * LANES:(c + 1) * LANES]
        x1 = jnp.concatenate(
            [jnp.concatenate([buf_ref[s, c, pl.ds(j, SUBLANES, stride=pitch), :] for j in range(groups)], axis=0)
             for c in range(n_lt)], axis=1)
        x1_ref[s] = x1
        y = x1 * lax.rsqrt(jnp.mean(x1 * x1, axis=-1, keepdims=True) + EPS) * g2_ref[...]
        h2_ref[s] = (y * (1.0 + mod_ref[0, 4:5, :]) + mod_ref[0, 3:4, :]).astype(h2_ref.dtype)
        acc_ref[s] = jnp.zeros((sub, D_MODEL), F32)

    def tail(s):
        out = x1_ref[s] + mod_ref[0, 5:6, :] * acc_ref[s]
        for c in range(n_lt):
            for j in range(groups):
                buf_ref[s, c, pl.ds(j, SUBLANES, stride=pitch), :] = out[j * SUBLANES:(j + 1) * SUBLANES,
                                                                         c * LANES:(c + 1) * LANES]
        for c in range(n_lt):
            for sg in range(SUBLANES):
                o_ref[0, s * sub + sg * groups:s * sub + (sg + 1) * groups, c * LANES:(c + 1) * LANES] = (
                    buf_ref[s, c, sg * pitch:sg * pitch + groups])

    n = D_FF // fc

    def up(c, slot):
        s, jc = divmod(c, n)
        for half in range(2):
            off = half * D_FF + jc * fc
            u_ref[slot, half] = _dot(h2_ref[s], wu_ref[:, off:off + fc])

    def act(c, slot):
        s, jc = divmod(c, n)
        halves = []
        for half in range(2):
            cols = slice(half * D_FF + jc * fc, half * D_FF + (jc + 1) * fc)
            u = u_ref[slot, half]
            prev = jnp.where(first, 0.0, carry_ref[:, cols]) if s == 0 else carry_ref[:, cols]
            carry_ref[:, cols] = u[sub - 2 * SUBLANES:]
            halves.append(_causal_conv(u, prev, cw_ref[:, cols], cb_ref[:, cols]))
        a, v = halves
        g_ref[slot] = (_silu(a) * v).astype(g_ref.dtype)

    def down(c, slot):
        s, jc = divmod(c, n)
        acc_ref[s] += _dot(g_ref[slot], wd_ref[jc * fc:(jc + 1) * fc, :])

    total = n_sub * n
    head(0)
    for i in range(total + 1):
        if i < total:
            up(i, i % 2)
        if 0 <= i - 1 < total:
            act(i - 1, (i - 1) % 2)
            down(i - 1, (i - 1) % 2)
        s, jc = divmod(i, n)
        if jc == HEAD_AT and s + 1 < n_sub:
            head(s + 1)
        if jc == 0 and 1 <= s <= n_sub:
            tail(s - 1)


def _ffn(x, o_hg, o_nsa, mod, w_out, g2, w_up, conv_w, conv_b, w_down, tm, sub, fc):
    bsz, t, _ = x.shape
    row_spec = lambda w: pl.BlockSpec((1, tm, w), lambda b, i: (b, i, 0))
    groups = sub // SUBLANES
    pad = SUBLANES if (groups // SUBLANES) % 2 == 0 else 0
    n_sub = tm // sub
    resident = lambda a: pl.BlockSpec(a.shape, lambda b, i: (0, 0), pipeline_mode=pl.Buffered(1))
    return pl.pallas_call(
        functools.partial(_ffn_kernel, fc=fc),
        grid=(bsz, t // tm),
        in_specs=[row_spec(D_MODEL), row_spec(HG_WIDTH), row_spec(NSA_WIDTH),
                  pl.BlockSpec((1, 6, D_MODEL), lambda b, i: (b, 0, 0)),
                  resident(w_out), resident(g2), resident(w_up), resident(conv_w), resident(conv_b),
                  resident(w_down)],
        out_specs=row_spec(D_MODEL),
        out_shape=jax.ShapeDtypeStruct(x.shape, F32),
        scratch_shapes=[pltpu.VMEM((2 * SUBLANES, 2 * D_FF), F32),
                        pltpu.VMEM((n_sub, D_MODEL // LANES, sub + SUBLANES * pad, LANES), F32),
                        pltpu.VMEM((n_sub, sub, D_MODEL), F32),
                        pltpu.VMEM((n_sub, sub, D_MODEL), BF16),
                        pltpu.VMEM((2, 2, sub, fc), F32),
                        pltpu.VMEM((2, sub, fc), BF16),
                        pltpu.VMEM((n_sub, sub, D_MODEL), F32)],
        compiler_params=pltpu.CompilerParams(
            dimension_semantics=("arbitrary", "arbitrary"), vmem_limit_bytes=VMEM_LIMIT),
        name="ffn",
    )(x, o_hg, o_nsa, mod, w_out, g2, w_up, conv_w, conv_b, w_down)


def _rope_tables():
    half = ROPE_DIM // 2
    inv = ROPE_THETA ** (-jnp.arange(half, dtype=F32) * 2.0 / ROPE_DIM)
    return (jnp.tile(inv.reshape(half, 1), (1, LANES)),)


def _gain_t(g):
    return jnp.tile(g.reshape(NSA_HD, 1), (LANES // NSA_HD, LANES))


def _selection_tables(t):
    n_seg = t // CMP_STRIDE
    nb = t // SLC_BLOCK
    cst = np.arange(n_seg) * CMP_STRIDE
    sst = np.arange(nb) * SLC_BLOCK
    ovl = np.clip(np.minimum(cst[:, None] + CMP_BLOCK, sst[None] + SLC_BLOCK)
                  - np.maximum(cst[:, None], sst[None]), 0, None) / CMP_BLOCK
    ovl[(t - CMP_BLOCK) // CMP_STRIDE + 1:] = 0.0
    return (jnp.asarray(ovl.T, dtype=BF16),)


def _layer(x, mod, pos_row, l, p, tables):
    bsz, t, _ = x.shape
    inv_t, mt = tables
    w_in_p = p["w_in"][l].astype(BF16)
    qg_t = _gain_t(p["q_norm_g"][l])
    kg_t = jnp.stack([_gain_t(p["k_norm_g"][l, br]) for br in range(N_BRANCH)])
    q_t, kc, vc, ks, vst, kw, vwt, gates_t, o_hg = _mix(
        x, mod, p["norm1_g"][l].reshape(1, D_MODEL), w_in_p, pos_row, inv_t, qg_t, kg_t,
        p["lb_logits"], p["hg_norm_g"][l].reshape(1, HG_HD), l, tm=TM_MIX)

    pe2 = p["pe_cmp"][l].reshape(2, 2, CMP_STRIDE * NSA_HD)
    kcmp, vcmp_t = _compress(kc, vc, pe2, p["w_cmp1"][l].astype(BF16), p["w_cmp2"][l].astype(BF16))
    bound = (SCORE_BOUND_MARGIN * NSA_HD ** 0.5 * jnp.max(jnp.abs(p["q_norm_g"][l]))
             * jnp.max(jnp.abs(p["k_norm_g"][l, 1:]))).reshape(1).astype(F32)
    nsa_args = (q_t, kcmp, vcmp_t, ks, vst, kw, vwt, gates_t, mt, bound)
    o_nsa = lax.cond(bound[0] <= MAX_SCORE_BOUND,
                     lambda a: _nsa(*a, tq=TQ_NSA, tks=TKS_NSA, bounded=True),
                     lambda a: _nsa(*a, tq=TQ_NSA, tks=TKS_NSA, bounded=False), nsa_args)

    return _ffn(x, o_hg, o_nsa, mod, p["w_out"][l].astype(BF16), p["norm2_g"][l].reshape(1, D_MODEL),
                p["w_up"][l].astype(BF16), p["conv_w"][l], p["conv_b"][l].reshape(1, 2 * D_FF),
                p["w_down"][l].astype(BF16), tm=TM_FFN, sub=SUB_FFN, fc=FC_FFN)


def kernel(x, c, positions, w_ada, b_ada, norm1_g, w_in, lb_logits, hg_norm_g, q_norm_g, k_norm_g, pe_cmp, w_cmp1, w_cmp2, w_out, norm2_g, w_up, conv_w, conv_b, w_down):
    p = dict(w_in=w_in, norm1_g=norm1_g, lb_logits=lb_logits, hg_norm_g=hg_norm_g, q_norm_g=q_norm_g,
             k_norm_g=k_norm_g, pe_cmp=pe_cmp, w_cmp1=w_cmp1, w_cmp2=w_cmp2, w_out=w_out,
             norm2_g=norm2_g, w_up=w_up, conv_w=conv_w, conv_b=conv_b, w_down=w_down)
    bsz, t, _ = x.shape
    assert x.shape[2] == D_MODEL and t % TM_MIX == 0 and t % TKS_NSA == 0 and t >= WIN + TQ_NSA
    assert TKS_NSA % TQ_NSA == 0 and TM_MIX % HG_CHUNK == 0 and D_FF % FC_FFN == 0 and TM_FFN % SUB_FFN == 0
    tables = _rope_tables() + _selection_tables(t)
    pos_row = positions.reshape(bsz, 1, t)
    for l in range(w_ada.shape[0]):
        mod = _ada(c, w_ada[l], b_ada[l]).reshape(bsz, 6, D_MODEL)
        x = _layer(x, mod, pos_row, l, p, tables)
    return x
```
